```python
import jax
import jax.numpy as jnp
from jax import lax
import numpy as np

D_MODEL = 1024
BATCH = 8
SEQ = 16384
DEPTH = 4

CHUNK = 64
N_META = 16
META_PAD = CHUNK - N_META
D_CONV = 512
CONV_WIDTH = 31
HG_HEADS = 4
HG_DK = 128
HG_DV = 128
D_HG_K = HG_HEADS * HG_DK
D_HG = HG_HEADS * HG_DV
F_FLOOR = 1e-30
ATT_Q_HEADS = 8
ATT_KV_HEADS = 2
ATT_HEAD_DIM = 64
ATT_GROUP = ATT_Q_HEADS // ATT_KV_HEADS
D_ATT = ATT_Q_HEADS * ATT_HEAD_DIM
D_KV = ATT_KV_HEADS * ATT_HEAD_DIM
WINDOW = 128
WINDOW_CHUNKS = WINDOW // CHUNK
N_BRANCH = 3
EPS = 1e-6
IN_SIZES = (2 * D_CONV, D_CONV, D_HG_K, D_HG_K, D_HG, D_HG, D_ATT, D_KV, D_KV, D_ATT, N_BRANCH * D_MODEL)
D_IN = 2 * D_CONV + D_CONV + 2 * D_HG_K + 2 * D_HG + 2 * D_ATT + 2 * D_KV + N_BRANCH * D_MODEL

kernel_name = "hybrid_conv_hgrn2_swa_meta_trunk"


def _rmsnorm(x, g):
    xf = x.astype(jnp.float32)
    return xf * lax.rsqrt(jnp.mean(xf * xf, axis=-1, keepdims=True) + EPS) * g.astype(jnp.float32)


def _conv_branch(a_in, a_gate, valid, conv_w, conv_b, ln_g, ln_b, w_o):
    a = a_in.astype(jnp.float32)
    u = a[..., :D_CONV] * jax.nn.sigmoid(a[..., D_CONV:])
    u = jnp.where(valid[None, :, None], u, 0.0)
    y = lax.conv_general_dilated(
        u, conv_w.astype(jnp.float32)[:, None, :], window_strides=(1,),
        padding=[(CONV_WIDTH - 1, 0)], dimension_numbers=("NWC", "WIO", "NWC"),
        feature_group_count=D_CONV)
    y = y + conv_b.astype(jnp.float32)
    mu = jnp.mean(y, axis=-1, keepdims=True)
    var = jnp.mean(jnp.square(y - mu), axis=-1, keepdims=True)
    y = (y - mu) * lax.rsqrt(var + EPS) * ln_g.astype(jnp.float32) + ln_b.astype(jnp.float32)
    y = jax.nn.silu(y) * jax.nn.silu(a_gate.astype(jnp.float32))
    return y.astype(w_o.dtype) @ w_o


def _hgrn2_branch(q, fz, i, gate, valid, lb, gn_g, w_o):
    B, L, _ = q.shape
    n_chunks = L // CHUNK
    qh = jax.nn.silu(q.astype(jnp.float32)).reshape(B, L, HG_HEADS, HG_DK)
    z = fz.astype(jnp.float32).reshape(B, L, HG_HEADS, HG_DK)
    lbh = lb.astype(jnp.float32).reshape(HG_HEADS, HG_DK)
    f = lbh + (1.0 - lbh) * jax.nn.sigmoid(z)
    logf = jnp.log(jnp.maximum(f, F_FLOOR))
    kh = (1.0 - lbh) * jax.nn.sigmoid(-z)
    vmask = valid[None, :, None, None]
    logf = jnp.where(vmask, logf, 0.0)
    kh = jnp.where(vmask, kh, 0.0)
    vh = i.astype(jnp.float32).reshape(B, L, HG_HEADS, HG_DV)

    def to_chunks(t):
        return t.reshape(B, n_chunks, CHUNK, HG_HEADS, -1).transpose(1, 0, 3, 2, 4)

    tri = jnp.tril(jnp.ones((CHUNK, CHUNK), dtype=bool))[:, :, None]

    def step(S, inp):
        qc, kc, vc, gc = inp
        b = jnp.cumsum(gc, axis=2)
        b_last = b[:, :, -1:, :]
        rel = b[:, :, :, None, :] - b[:, :, None, :, :]
        decay = jnp.exp(jnp.where(tri, rel, -jnp.inf))
        scores = jnp.einsum("bhtd,bhtsd,bhsd->bhts", qc, decay, kc)
        o = (jnp.einsum("bhts,bhsv->bhtv", scores, vc)
             + jnp.einsum("bhtd,bhdv->bhtv", qc * jnp.exp(b), S))
        S = (jnp.exp(b_last)[:, :, 0, :, None] * S
             + jnp.einsum("bhsd,bhsv->bhdv", kc * jnp.exp(b_last - b), vc))
        return S, o

    S0 = jnp.zeros((B, HG_HEADS, HG_DK, HG_DV), jnp.float32)
    _, o = lax.scan(step, S0, (to_chunks(qh), to_chunks(kh), to_chunks(vh), to_chunks(logf)))
    o = o.transpose(1, 0, 3, 2, 4).reshape(B, L, HG_HEADS, HG_DV)
    o = _rmsnorm(o, gn_g).reshape(B, L, D_HG) * jax.nn.silu(gate.astype(jnp.float32))
    return o.astype(w_o.dtype) @ w_o


def _swa_key_mask(n_chunks):
    c = jnp.arange(n_chunks)[:, None]
    j = jnp.arange(CHUNK)[None, :]
    meta_ok = (c > WINDOW_CHUNKS) & (j >= META_PAD)
    band = [((c - WINDOW_CHUNKS + r) * CHUNK + j) >= META_PAD for r in range(WINDOW_CHUNKS + 1)]
    return jnp.concatenate([meta_ok] + band, axis=1)


def _band(t, n_chunks):
    tp = jnp.pad(t, ((0, 0), (WINDOW_CHUNKS, 0), (0, 0), (0, 0), (0, 0)))
    meta = jnp.broadcast_to(t[:, :1], t.shape)
    return jnp.concatenate([meta] + [tp[:, r:r + n_chunks] for r in range(WINDOW_CHUNKS + 1)], axis=2)


def _swa_branch(q, k, v, gate, qn_g, kn_g, sinks, w_o, key_mask):
    B, L, _ = q.shape
    n_chunks = L // CHUNK
    qh = _rmsnorm(q.reshape(B, L, ATT_Q_HEADS, ATT_HEAD_DIM), qn_g).reshape(
        B, n_chunks, CHUNK, ATT_KV_HEADS, ATT_GROUP, ATT_HEAD_DIM)
    kh = _rmsnorm(k.reshape(B, L, ATT_KV_HEADS, ATT_HEAD_DIM), kn_g).reshape(
        B, n_chunks, CHUNK, ATT_KV_HEADS, ATT_HEAD_DIM)
    vh = v.astype(jnp.float32).reshape(B, n_chunks, CHUNK, ATT_KV_HEADS, ATT_HEAD_DIM)
    kb = _band(kh, n_chunks)
    vb = _band(vh, n_chunks)
    s = jnp.einsum("bnqhgd,bnkhd->bnhgqk", qh, kb) * (ATT_HEAD_DIM ** -0.5)
    s = jnp.where(key_mask[None, :, None, None, None, :], s, -jnp.inf)
    sink = sinks.astype(jnp.float32).reshape(ATT_KV_HEADS, ATT_GROUP)[None, None, :, :, None, None]
    m = jnp.maximum(jnp.max(s, axis=-1, keepdims=True), sink)
    p = jnp.exp(s - m)
    denom = jnp.sum(p, axis=-1, keepdims=True) + jnp.exp(sink - m)
    o = jnp.einsum("bnhgqk,bnkhd->bnqhgd", p / denom, vb).reshape(B, L, D_ATT)
    o = o * jax.nn.silu(gate.astype(jnp.float32))
    return o.astype(w_o.dtype) @ w_o


def _fwd_setup_inputs(seed: int = 0) -> dict:
    key = jax.random.key(seed)
    ks = jax.random.split(key, 20)
    f32 = jnp.float32
    nrm = lambda k, shape, scale: jax.random.normal(k, shape, f32) * scale
    return {
        "x": nrm(ks[0], (BATCH, SEQ, D_MODEL), 1.0),
        "meta_tokens": nrm(ks[1], (N_META, D_MODEL), 1.0),
        "norm_g": 1.0 + nrm(ks[2], (DEPTH, D_MODEL), 0.05),
        "w_in": nrm(ks[3], (DEPTH, D_MODEL, D_IN), D_MODEL ** -0.5),
        "conv_w": nrm(ks[4], (DEPTH, CONV_WIDTH, D_CONV), CONV_WIDTH ** -0.5),
        "conv_b": nrm(ks[5], (DEPTH, D_CONV), 0.02),
        "conv_ln_g": 1.0 + nrm(ks[6], (DEPTH, D_CONV), 0.05),
        "conv_ln_b": nrm(ks[7], (DEPTH, D_CONV), 0.02),
        "w_conv_out": nrm(ks[8], (DEPTH, D_CONV, D_MODEL), D_CONV ** -0.5),
        "hg_lower_bounds": nrm(ks[9], (DEPTH, D_HG_K), 0.1),
        "hg_norm_g": 1.0 + nrm(ks[10], (DEPTH, HG_DV), 0.05),
        "w_hg_out": nrm(ks[11], (DEPTH, D_HG, D_MODEL), D_HG ** -0.5),
        "q_norm_g": 1.0 + nrm(ks[12], (DEPTH, ATT_HEAD_DIM), 0.05),
        "k_norm_g": 1.0 + nrm(ks[13], (DEPTH, ATT_HEAD_DIM), 0.05),
        "attn_sinks": nrm(ks[14], (DEPTH, ATT_Q_HEADS), 0.5),
        "w_att_out": nrm(ks[15], (DEPTH, D_ATT, D_MODEL), D_ATT ** -0.5),
        "w_out": nrm(ks[16], (DEPTH, D_MODEL, D_MODEL), D_MODEL ** -0.5),
    }


def _fwd_reference(x, meta_tokens, norm_g, w_in, conv_w, conv_b, conv_ln_g, conv_ln_b, w_conv_out,
              hg_lower_bounds, hg_norm_g, w_hg_out, q_norm_g, k_norm_g, attn_sinks, w_att_out, w_out):
    B = x.shape[0]
    dt = x.dtype
    h = jnp.concatenate([
        jnp.zeros((B, META_PAD, D_MODEL), dt),
        jnp.broadcast_to(meta_tokens.astype(dt)[None], (B, N_META, D_MODEL)),
        x], axis=1)
    L = h.shape[1]
    n_chunks = L // CHUNK
    valid = jnp.arange(L) >= META_PAD
    key_mask = _swa_key_mask(n_chunks)
    lb_sm = jax.nn.softmax(hg_lower_bounds.astype(jnp.float32), axis=0)
    lb_all = jnp.clip(jnp.cumsum(lb_sm, axis=0) - lb_sm[0:1], 0.0, 1.0)
    split_idx = [int(v) for v in np.cumsum(IN_SIZES)[:-1]]

    for l in range(DEPTH):
        hn = _rmsnorm(h, norm_g[l]).astype(dt)
        u = hn @ w_in[l]
        (a_in, a_gate, b_q, b_f, b_i, b_gate, c_q, c_k, c_v, c_gate, g_logits) = jnp.split(u, split_idx, axis=-1)
        z_a = _conv_branch(a_in, a_gate, valid, conv_w[l], conv_b[l], conv_ln_g[l], conv_ln_b[l], w_conv_out[l])
        z_b = _hgrn2_branch(b_q, b_f, b_i, b_gate, valid, lb_all[l], hg_norm_g[l], w_hg_out[l])
        z_c = _swa_branch(c_q, c_k, c_v, c_gate, q_norm_g[l], k_norm_g[l], attn_sinks[l], w_att_out[l], key_mask)
        g = jax.nn.sigmoid(g_logits.astype(jnp.float32))
        mixed = (g[..., :D_MODEL] * z_a.astype(jnp.float32)
                 + g[..., D_MODEL:2 * D_MODEL] * z_b.astype(jnp.float32)
                 + g[..., 2 * D_MODEL:] * z_c.astype(jnp.float32))
        h = h + mixed.astype(dt) @ w_out[l]

    return h[:, CHUNK:]


import jax as _jax
import jax.numpy as _jnp

TWIN_FORMAT = 'train_step'
FWD_PARAMS = ['x', 'meta_tokens', 'norm_g', 'w_in', 'conv_w', 'conv_b', 'conv_ln_g', 'conv_ln_b', 'w_conv_out', 'hg_lower_bounds', 'hg_norm_g', 'w_hg_out', 'q_norm_g', 'k_norm_g', 'attn_sinks', 'w_att_out', 'w_out']
TWIN_WEIGHTS = ['meta_tokens', 'norm_g', 'w_in', 'conv_w', 'conv_b', 'conv_ln_g', 'conv_ln_b', 'w_conv_out', 'hg_lower_bounds', 'hg_norm_g', 'w_hg_out', 'q_norm_g', 'k_norm_g', 'attn_sinks', 'w_att_out', 'w_out']
TWIN_DIFF_INPUT = 'x'
TWIN_INPUTS = ['x', 'meta_tokens', 'norm_g', 'w_in', 'conv_w', 'conv_b', 'conv_ln_g', 'conv_ln_b', 'w_conv_out', 'hg_lower_bounds', 'hg_norm_g', 'w_hg_out', 'q_norm_g', 'k_norm_g', 'attn_sinks', 'w_att_out', 'w_out', 'loss_target', 'm_meta_tokens', 'm_norm_g', 'm_w_in', 'm_conv_w', 'm_conv_b', 'm_conv_ln_g', 'm_conv_ln_b', 'm_w_conv_out', 'm_hg_lower_bounds', 'm_hg_norm_g', 'm_w_hg_out', 'm_q_norm_g', 'm_k_norm_g', 'm_attn_sinks', 'm_w_att_out', 'm_w_out', 'v_meta_tokens', 'v_norm_g', 'v_w_in', 'v_conv_w', 'v_conv_b', 'v_conv_ln_g', 'v_conv_ln_b', 'v_w_conv_out', 'v_hg_lower_bounds', 'v_hg_norm_g', 'v_w_hg_out', 'v_q_norm_g', 'v_k_norm_g', 'v_attn_sinks', 'v_w_att_out', 'v_w_out']
TWIN_OUTPUTS = ['loss', 'grad_x', 'grad_meta_tokens', 'grad_norm_g', 'grad_w_in', 'grad_conv_w', 'grad_conv_b', 'grad_conv_ln_g', 'grad_conv_ln_b', 'grad_w_conv_out', 'grad_hg_lower_bounds', 'grad_hg_norm_g', 'grad_w_hg_out', 'grad_q_norm_g', 'grad_k_norm_g', 'grad_attn_sinks', 'grad_w_att_out', 'grad_w_out', 'delta_meta_tokens', 'delta_norm_g', 'delta_w_in', 'delta_conv_w', 'delta_conv_b', 'delta_conv_ln_g', 'delta_conv_ln_b', 'delta_w_conv_out', 'delta_hg_lower_bounds', 'delta_hg_norm_g', 'delta_w_hg_out', 'delta_q_norm_g', 'delta_k_norm_g', 'delta_attn_sinks', 'delta_w_att_out', 'delta_w_out', 'new_m_meta_tokens', 'new_m_norm_g', 'new_m_w_in', 'new_m_conv_w', 'new_m_conv_b', 'new_m_conv_ln_g', 'new_m_conv_ln_b', 'new_m_w_conv_out', 'new_m_hg_lower_bounds', 'new_m_hg_norm_g', 'new_m_w_hg_out', 'new_m_q_norm_g', 'new_m_k_norm_g', 'new_m_attn_sinks', 'new_m_w_att_out', 'new_m_w_out', 'new_v_meta_tokens', 'new_v_norm_g', 'new_v_w_in', 'new_v_conv_w', 'new_v_conv_b', 'new_v_conv_ln_g', 'new_v_conv_ln_b', 'new_v_w_conv_out', 'new_v_hg_lower_bounds', 'new_v_hg_norm_g', 'new_v_w_hg_out', 'new_v_q_norm_g', 'new_v_k_norm_g', 'new_v_attn_sinks', 'new_v_w_att_out', 'new_v_w_out']
TWIN_LEAF_KINDS = {'loss': 'loss', 'grad_x': 'grad_x', 'grad_meta_tokens': 'grad_w', 'grad_norm_g': 'grad_w', 'grad_w_in': 'grad_w', 'grad_conv_w': 'grad_w', 'grad_conv_b': 'grad_w', 'grad_conv_ln_g': 'grad_w', 'grad_conv_ln_b': 'grad_w', 'grad_w_conv_out': 'grad_w', 'grad_hg_lower_bounds': 'grad_w', 'grad_hg_norm_g': 'grad_w', 'grad_w_hg_out': 'grad_w', 'grad_q_norm_g': 'grad_w', 'grad_k_norm_g': 'grad_w', 'grad_attn_sinks': 'grad_w', 'grad_w_att_out': 'grad_w', 'grad_w_out': 'grad_w', 'delta_meta_tokens': 'delta_w', 'delta_norm_g': 'delta_w', 'delta_w_in': 'delta_w', 'delta_conv_w': 'delta_w', 'delta_conv_b': 'delta_w', 'delta_conv_ln_g': 'delta_w', 'delta_conv_ln_b': 'delta_w', 'delta_w_conv_out': 'delta_w', 'delta_hg_lower_bounds': 'delta_w', 'delta_hg_norm_g': 'delta_w', 'delta_w_hg_out': 'delta_w', 'delta_q_norm_g': 'delta_w', 'delta_k_norm_g': 'delta_w', 'delta_attn_sinks': 'delta_w', 'delta_w_att_out': 'delta_w', 'delta_w_out': 'delta_w', 'new_m_meta_tokens': 'new_m', 'new_m_norm_g': 'new_m', 'new_m_w_in': 'new_m', 'new_m_conv_w': 'new_m', 'new_m_conv_b': 'new_m', 'new_m_conv_ln_g': 'new_m', 'new_m_conv_ln_b': 'new_m', 'new_m_w_conv_out': 'new_m', 'new_m_hg_lower_bounds': 'new_m', 'new_m_hg_norm_g': 'new_m', 'new_m_w_hg_out': 'new_m', 'new_m_q_norm_g': 'new_m', 'new_m_k_norm_g': 'new_m', 'new_m_attn_sinks': 'new_m', 'new_m_w_att_out': 'new_m', 'new_m_w_out': 'new_m', 'new_v_meta_tokens': 'new_v', 'new_v_norm_g': 'new_v', 'new_v_w_in': 'new_v', 'new_v_conv_w': 'new_v', 'new_v_conv_b': 'new_v', 'new_v_conv_ln_g': 'new_v', 'new_v_conv_ln_b': 'new_v', 'new_v_w_conv_out': 'new_v', 'new_v_hg_lower_bounds': 'new_v', 'new_v_hg_norm_g': 'new_v', 'new_v_w_hg_out': 'new_v', 'new_v_q_norm_g': 'new_v', 'new_v_k_norm_g': 'new_v', 'new_v_attn_sinks': 'new_v', 'new_v_w_att_out': 'new_v', 'new_v_w_out': 'new_v'}


def _forward(args):
    return _fwd_reference(*[args[k] for k in FWD_PARAMS])


def _output_shape():
    def fwd():
        inp = _fwd_setup_inputs(0)
        return _fwd_reference(*[inp[k] for k in FWD_PARAMS])
    out = _jax.eval_shape(fwd)
    return out.shape, out.dtype

N_MICROBATCH = 1
ADAM_LR = 0.001
ADAM_B1 = 0.9
ADAM_B2 = 0.999
ADAM_EPS = 1e-08
ADAM_WD = 0.01
ADAM_STEP = 10
PER_EXAMPLE_BATCH_AXIS = {'x': 0, 'loss_target': 0}
SHARED_INPUTS = []
_WEIGHT_DTYPES = {'meta_tokens': _jnp.float32, 'norm_g': _jnp.float32, 'w_in': _jnp.float32, 'conv_w': _jnp.float32, 'conv_b': _jnp.float32, 'conv_ln_g': _jnp.float32, 'conv_ln_b': _jnp.float32, 'w_conv_out': _jnp.float32, 'hg_lower_bounds': _jnp.float32, 'hg_norm_g': _jnp.float32, 'w_hg_out': _jnp.float32, 'q_norm_g': _jnp.float32, 'k_norm_g': _jnp.float32, 'attn_sinks': _jnp.float32, 'w_att_out': _jnp.float32, 'w_out': _jnp.float32}
MOMENT_SCALE = {'meta_tokens': 3.149623e-02, 'norm_g': 2.328320e+01, 'w_in': 2.398678e-01, 'conv_w': 2.950727e-01, 'conv_b': 3.437216e+00, 'conv_ln_g': 1.149464e+01, 'conv_ln_b': 7.619041e+00, 'w_conv_out': 3.367834e-01, 'hg_lower_bounds': 3.562125e-02, 'hg_norm_g': 1.082679e+02, 'w_hg_out': 5.014502e-01, 'q_norm_g': 1.460956e+00, 'k_norm_g': 1.492975e+00, 'attn_sinks': 3.230504e-02, 'w_att_out': 3.921291e-02, 'w_out': 5.793316e-01}


def _to_microbatches(a, axis):
    t = _jnp.moveaxis(a, axis, 0)
    t = t.reshape((N_MICROBATCH, t.shape[0] // N_MICROBATCH) + t.shape[1:])
    return _jnp.moveaxis(t, 1, axis + 1)


def setup_inputs(seed: int = 0) -> dict:
    inp = _fwd_setup_inputs(seed)
    key = _jax.random.fold_in(_jax.random.key(seed), 7919)
    shape, _ = _output_shape()
    out = dict(inp)
    out["loss_target"] = _jax.random.normal(_jax.random.fold_in(key, 0), shape, _jnp.float32)
    for i, name in enumerate(TWIN_WEIGHTS):
        w = inp[name].astype(_jnp.float32)
        if MOMENT_SCALE is None:
            s = _jnp.sqrt(_jnp.mean(_jnp.square(w)) + 1e-30)
        else:
            s = MOMENT_SCALE[name]
        km, kv = _jax.random.split(_jax.random.fold_in(key, i + 1))
        out[name] = w
        out["m_" + name] = s * _jax.random.normal(km, w.shape, _jnp.float32)
        out["v_" + name] = (s * s) * _jax.random.uniform(kv, w.shape, _jnp.float32, 0.5, 1.5)
    if N_MICROBATCH > 1:
        for name, axis in PER_EXAMPLE_BATCH_AXIS.items():
            out[name] = _to_microbatches(out[name], axis)
    return {'x': out['x'], 'meta_tokens': out['meta_tokens'], 'norm_g': out['norm_g'], 'w_in': out['w_in'], 'conv_w': out['conv_w'], 'conv_b': out['conv_b'], 'conv_ln_g': out['conv_ln_g'], 'conv_ln_b': out['conv_ln_b'], 'w_conv_out': out['w_conv_out'], 'hg_lower_bounds': out['hg_lower_bounds'], 'hg_norm_g': out['hg_norm_g'], 'w_hg_out': out['w_hg_out'], 'q_norm_g': out['q_norm_g'], 'k_norm_g': out['k_norm_g'], 'attn_sinks': out['attn_sinks'], 'w_att_out': out['w_att_out'], 'w_out': out['w_out'], 'loss_target': out['loss_target'], 'm_meta_tokens': out['m_meta_tokens'], 'm_norm_g': out['m_norm_g'], 'm_w_in': out['m_w_in'], 'm_conv_w': out['m_conv_w'], 'm_conv_b': out['m_conv_b'], 'm_conv_ln_g': out['m_conv_ln_g'], 'm_conv_ln_b': out['m_conv_ln_b'], 'm_w_conv_out': out['m_w_conv_out'], 'm_hg_lower_bounds': out['m_hg_lower_bounds'], 'm_hg_norm_g': out['m_hg_norm_g'], 'm_w_hg_out': out['m_w_hg_out'], 'm_q_norm_g': out['m_q_norm_g'], 'm_k_norm_g': out['m_k_norm_g'], 'm_attn_sinks': out['m_attn_sinks'], 'm_w_att_out': out['m_w_att_out'], 'm_w_out': out['m_w_out'], 'v_meta_tokens': out['v_meta_tokens'], 'v_norm_g': out['v_norm_g'], 'v_w_in': out['v_w_in'], 'v_conv_w': out['v_conv_w'], 'v_conv_b': out['v_conv_b'], 'v_conv_ln_g': out['v_conv_ln_g'], 'v_conv_ln_b': out['v_conv_ln_b'], 'v_w_conv_out': out['v_w_conv_out'], 'v_hg_lower_bounds': out['v_hg_lower_bounds'], 'v_hg_norm_g': out['v_hg_norm_g'], 'v_w_hg_out': out['v_w_hg_out'], 'v_q_norm_g': out['v_q_norm_g'], 'v_k_norm_g': out['v_k_norm_g'], 'v_attn_sinks': out['v_attn_sinks'], 'v_w_att_out': out['v_w_att_out'], 'v_w_out': out['v_w_out']}


def _loss(weights, diff, rest, loss_target):
    with _jax.named_scope("forward"):
        args = {**rest, TWIN_DIFF_INPUT: diff, **{k: w.astype(_WEIGHT_DTYPES[k]) for k, w in weights.items()}}
        y = _forward(args)
    with _jax.named_scope("loss_head"):
        err = _jnp.square(y.astype(_jnp.float32) - loss_target)
        return 0.5 * _jnp.sum(_jnp.mean(err, axis=-1)) if err.ndim else 0.5 * err


def _adamw(w, g, m, v):
    m = ADAM_B1 * m + (1.0 - ADAM_B1) * g
    v = ADAM_B2 * v + (1.0 - ADAM_B2) * _jnp.square(g)
    m_hat = m / (1.0 - ADAM_B1 ** ADAM_STEP)
    v_hat = v / (1.0 - ADAM_B2 ** ADAM_STEP)
    delta = -ADAM_LR * (m_hat / (_jnp.sqrt(v_hat) + ADAM_EPS) + ADAM_WD * w)
    return delta, m, v


def reference(x, meta_tokens, norm_g, w_in, conv_w, conv_b, conv_ln_g, conv_ln_b, w_conv_out, hg_lower_bounds, hg_norm_g, w_hg_out, q_norm_g, k_norm_g, attn_sinks, w_att_out, w_out, loss_target, m_meta_tokens, m_norm_g, m_w_in, m_conv_w, m_conv_b, m_conv_ln_g, m_conv_ln_b, m_w_conv_out, m_hg_lower_bounds, m_hg_norm_g, m_w_hg_out, m_q_norm_g, m_k_norm_g, m_attn_sinks, m_w_att_out, m_w_out, v_meta_tokens, v_norm_g, v_w_in, v_conv_w, v_conv_b, v_conv_ln_g, v_conv_ln_b, v_w_conv_out, v_hg_lower_bounds, v_hg_norm_g, v_w_hg_out, v_q_norm_g, v_k_norm_g, v_attn_sinks, v_w_att_out, v_w_out):
    given = dict(x=x, meta_tokens=meta_tokens, norm_g=norm_g, w_in=w_in, conv_w=conv_w, conv_b=conv_b, conv_ln_g=conv_ln_g, conv_ln_b=conv_ln_b, w_conv_out=w_conv_out, hg_lower_bounds=hg_lower_bounds, hg_norm_g=hg_norm_g, w_hg_out=w_hg_out, q_norm_g=q_norm_g, k_norm_g=k_norm_g, attn_sinks=attn_sinks, w_att_out=w_att_out, w_out=w_out, loss_target=loss_target, m_meta_tokens=m_meta_tokens, m_norm_g=m_norm_g, m_w_in=m_w_in, m_conv_w=m_conv_w, m_conv_b=m_conv_b, m_conv_ln_g=m_conv_ln_g, m_conv_ln_b=m_conv_ln_b, m_w_conv_out=m_w_conv_out, m_hg_lower_bounds=m_hg_lower_bounds, m_hg_norm_g=m_hg_norm_g, m_w_hg_out=m_w_hg_out, m_q_norm_g=m_q_norm_g, m_k_norm_g=m_k_norm_g, m_attn_sinks=m_attn_sinks, m_w_att_out=m_w_att_out, m_w_out=m_w_out, v_meta_tokens=v_meta_tokens, v_norm_g=v_norm_g, v_w_in=v_w_in, v_conv_w=v_conv_w, v_conv_b=v_conv_b, v_conv_ln_g=v_conv_ln_g, v_conv_ln_b=v_conv_ln_b, v_w_conv_out=v_w_conv_out, v_hg_lower_bounds=v_hg_lower_bounds, v_hg_norm_g=v_hg_norm_g, v_w_hg_out=v_w_hg_out, v_q_norm_g=v_q_norm_g, v_k_norm_g=v_k_norm_g, v_attn_sinks=v_attn_sinks, v_w_att_out=v_w_att_out, v_w_out=v_w_out)
    weights = {n: given[n] for n in TWIN_WEIGHTS}
    shared = {n: given[n] for n in SHARED_INPUTS}
    per_example = {n: given[n] for n in ['x']}
    grad_fn = _jax.value_and_grad(_loss, argnums=(0, 1))

    def one_microbatch(ex, loss_target):
        ex = dict(ex)
        diff = ex.pop(TWIN_DIFF_INPUT)
        return grad_fn(weights, diff, {**shared, **ex}, loss_target)

    if N_MICROBATCH == 1:
        loss, (grad_w, grad_x) = one_microbatch(per_example, given["loss_target"])
    else:
        def body(carry, xs):
            loss_sum, grad_sum = carry
            l_k, (gw_k, gx_k) = one_microbatch(xs[0], xs[1])
            with _jax.named_scope("update"):
                return (loss_sum + l_k, _jax.tree.map(_jnp.add, grad_sum, gw_k)), gx_k

        init = (_jnp.zeros((), _jnp.float32), _jax.tree.map(_jnp.zeros_like, weights))
        (loss, grad_w), grad_x = _jax.lax.scan(body, init, (per_example, given["loss_target"]))
    with _jax.named_scope("update"):
        delta_w, new_m, new_v = {}, {}, {}
        for n in TWIN_WEIGHTS:
            delta_w[n], new_m[n], new_v[n] = _adamw(weights[n], grad_w[n], given["m_" + n], given["v_" + n])
    return (loss, grad_x, *[grad_w[n] for n in TWIN_WEIGHTS], *[delta_w[n] for n in TWIN_WEIGHTS],
            *[new_m[n] for n in TWIN_WEIGHTS], *[new_v[n] for n in TWIN_WEIGHTS])
```

```python
import functools

import jax
import jax.numpy as jnp
from jax import lax
from jax.experimental import pallas as pl
from jax.experimental.pallas import tpu as pltpu

F32 = jnp.float32
BF = jnp.bfloat16

D_MODEL = 1024
DEPTH = 4
CHUNK = 64
N_META = 16
META_PAD = CHUNK - N_META
D_CONV = 512
CONV_WIDTH = 31
HG_HEADS = 4
HG_D = 128
ATT_Q_HEADS = 8
ATT_KV_HEADS = 2
ATT_HD = 64
ATT_GROUP = ATT_Q_HEADS // ATT_KV_HEADS
EPS = 1e-6
F_FLOOR = 1e-30
NEG = -1e30

ADAM_LR = 0.001
ADAM_B1 = 0.9
ADAM_B2 = 0.999
ADAM_EPS = 1e-08
ADAM_WD = 0.01
ADAM_STEP = 10

TR = 512
TRM = 256
QB = 128
HALO = 128
TAIL_PAD = TR - CHUNK
VMEM_LIMIT = 56 * 1024 * 1024

N_G, N_A, N_B, N_C = 3 * D_MODEL, 3 * D_CONV, 4 * 512, 2 * 512 + 2 * 128

MESH = pl.DeviceIdType.MESH


def _cp(sem=None, vmem=VMEM_LIMIT, **kw):
    if sem is None:
        return pltpu.CompilerParams(vmem_limit_bytes=vmem, **kw)
    return pltpu.CompilerParams(dimension_semantics=sem, vmem_limit_bytes=vmem, **kw)


def _nn(a, b):
    return lax.dot_general(a, b, (((1,), (0,)), ((), ())), preferred_element_type=F32)


def _nt(a, b):
    return lax.dot_general(a, b, (((1,), (1,)), ((), ())), preferred_element_type=F32)


def _tn(a, b):
    return lax.dot_general(a, b, (((0,), (0,)), ((), ())), preferred_element_type=F32)


def _sig(x):
    return jax.nn.sigmoid(x)


def _silu(x):
    return x * _sig(x)


def _dsilu(x):
    s = _sig(x)
    return s * (1.0 + x * (1.0 - s))


def _split3(x):
    hi = x.astype(BF)
    r1 = x - hi.astype(F32)
    mid = r1.astype(BF)
    lo = (r1 - mid.astype(F32)).astype(BF)
    return hi, mid, lo


def _mm3(t, x):
    hi, mid, lo = _split3(x)
    return _nn(t, hi) + _nn(t, mid) + _nn(t, lo)


def _chunk_tri(n, upper):
    r = lax.broadcasted_iota(jnp.int32, (n, n), 0)
    c = lax.broadcasted_iota(jnp.int32, (n, n), 1)
    same = jnp.right_shift(r, 6) == jnp.right_shift(c, 6)
    tri = (c >= r) if upper else (c <= r)
    return jnp.where(same & tri, 1.0, 0.0).astype(BF)


def _matmul(a, b, *, ta=False, tb=False, out_dtype, tm, tn, tk, name, col_major_grid=False):
    if ta:
        K, M = a.shape
    else:
        M, K = a.shape
    N = b.shape[0] if tb else b.shape[1]
    assert M % tm == 0 and N % tn == 0 and K % tk == 0, (name, M, N, K, tm, tn, tk)
    nk = K // tk
    if col_major_grid:
        grid = (N // tn, M // tm, nk)
        ij = lambda g0, g1: (g1, g0)
    else:
        grid = (M // tm, N // tn, nk)
        ij = lambda g0, g1: (g0, g1)
    if ta:
        a_spec = pl.BlockSpec((tk, tm), lambda g0, g1, k: (k, ij(g0, g1)[0]))
    else:
        a_spec = pl.BlockSpec((tm, tk), lambda g0, g1, k: (ij(g0, g1)[0], k))
    if tb:
        b_spec = pl.BlockSpec((tn, tk), lambda g0, g1, k: (ij(g0, g1)[1], k))
    else:
        b_spec = pl.BlockSpec((tk, tn), lambda g0, g1, k: (k, ij(g0, g1)[1]))
    o_spec = pl.BlockSpec((tm, tn), lambda g0, g1, k: ij(g0, g1))
    dims = (((0 if ta else 1,), (1 if tb else 0,)), ((), ()))
    use_acc = nk > 1 and out_dtype != F32

    def body(a_ref, b_ref, o_ref, *scr):
        k = pl.program_id(2)
        p = lax.dot_general(a_ref[...].astype(BF), b_ref[...].astype(BF), dims, preferred_element_type=F32)
        if nk == 1:
            o_ref[...] = p.astype(out_dtype)
        else:
            acc = scr[0] if use_acc else o_ref

            @pl.when(k == 0)
            def _():
                acc[...] = p

            @pl.when(k > 0)
            def _():
                acc[...] += p

            if use_acc:
                @pl.when(k == nk - 1)
                def _():
                    o_ref[...] = acc[...].astype(out_dtype)

    return pl.pallas_call(
        body, name=name, grid=grid, in_specs=[a_spec, b_spec], out_specs=o_spec,
        out_shape=jax.ShapeDtypeStruct((M, N), out_dtype),
        scratch_shapes=[pltpu.VMEM((tm, tn), F32)] if use_acc else [],
        compiler_params=_cp(("parallel", "parallel", "arbitrary")),
    )(a, b)


def _rms_fwd(h, g):
    Lp = h.shape[0]

    def body(h_ref, g_ref, o_ref):
        x = h_ref[...]
        r = lax.rsqrt(jnp.mean(x * x, axis=-1, keepdims=True) + EPS)
        o_ref[...] = (x * r * g_ref[...]).astype(BF)

    return pl.pallas_call(
        body, name="rms_fwd", grid=(Lp // TR,),
        in_specs=[pl.BlockSpec((TR, D_MODEL), lambda i: (i, 0)), pl.BlockSpec((1, D_MODEL), lambda i: (0, 0))],
        out_specs=pl.BlockSpec((TR, D_MODEL), lambda i: (i, 0)),
        out_shape=jax.ShapeDtypeStruct((Lp, D_MODEL), BF),
        compiler_params=_cp(("parallel",)),
    )(h, g)


def _glu(ua, row):
    a = ua[:, 0:D_CONV].astype(F32)
    gl = ua[:, D_CONV:2 * D_CONV].astype(F32)
    return jnp.where(row >= META_PAD, a * _sig(gl), 0.0)


def _conv_fwd(ua, cw, cvec):
    Lp = ua.shape[0]
    nt = Lp // TR
    hb = TR // CHUNK

    def body(cur_ref, halo_ref, w_ref, v_ref, ya_ref, yc_ref, ext):
        i = pl.program_id(0)
        row = i * TR + lax.broadcasted_iota(jnp.int32, (TR, 1), 0)
        hrow = i * TR - CHUNK + lax.broadcasted_iota(jnp.int32, (CHUNK, 1), 0)
        ext[pl.ds(0, CHUNK), :] = jnp.where(i > 0, _glu(halo_ref[...], hrow), 0.0)
        ext[pl.ds(CHUNK, TR), :] = _glu(cur_ref[...], row)
        acc = jnp.zeros((TR, D_CONV), F32)
        for j in range(CONV_WIDTH):
            acc = acc + ext[pl.ds(CHUNK - (CONV_WIDTH - 1) + j, TR), :] * w_ref[j:j + 1, :]
        y = acc + v_ref[0:1, :]
        yc_ref[...] = y
        mu = jnp.mean(y, axis=-1, keepdims=True)
        d = y - mu
        var = jnp.mean(d * d, axis=-1, keepdims=True)
        yn = d * lax.rsqrt(var + EPS) * v_ref[1:2, :] + v_ref[2:3, :]
        ya_ref[...] = (_silu(yn) * _silu(cur_ref[:, 2 * D_CONV:3 * D_CONV].astype(F32))).astype(BF)

    return pl.pallas_call(
        body, name="conv_fwd", grid=(nt,),
        in_specs=[pl.BlockSpec((TR, N_A), lambda i: (i, 0)),
                  pl.BlockSpec((CHUNK, N_A), lambda i: (jnp.maximum(i * hb - 1, 0), 0)),
                  pl.BlockSpec((CONV_WIDTH, D_CONV), lambda i: (0, 0)),
                  pl.BlockSpec((8, D_CONV), lambda i: (0, 0))],
        out_specs=[pl.BlockSpec((TR, D_CONV), lambda i: (i, 0)), pl.BlockSpec((TR, D_CONV), lambda i: (i, 0))],
        out_shape=[jax.ShapeDtypeStruct((Lp, D_CONV), BF), jax.ShapeDtypeStruct((Lp, D_CONV), F32)],
        scratch_shapes=[pltpu.VMEM((TR + CHUNK, D_CONV), F32)],
        compiler_params=_cp(("parallel",)),
    )(ua, ua, cw, cvec)


def _conv_bwd(ua, yconv, dya, cw, cvec):
    Lp = ua.shape[0]
    nt = Lp // TR
    hb = TR // CHUNK
    nhb = Lp // CHUNK

    def ln_bwd(y, dout, gate, v_ref):
        mu = jnp.mean(y, axis=-1, keepdims=True)
        d = y - mu
        var = jnp.mean(d * d, axis=-1, keepdims=True)
        rstd = lax.rsqrt(var + EPS)
        xhat = d * rstd
        yn = xhat * v_ref[1:2, :] + v_ref[2:3, :]
        dyn = dout * _silu(gate) * _dsilu(yn)
        dxh = dyn * v_ref[1:2, :]
        dyc = rstd * (dxh - jnp.mean(dxh, axis=-1, keepdims=True) - xhat * jnp.mean(dxh * xhat, axis=-1, keepdims=True))
        return dyc, dyn, xhat, yn

    def body(cur_ref, prev_ref, next_ref, yc_ref, ycn_ref, dy_ref, dyn_ref, w_ref, v_ref,
             du_ref, dw_ref, dv_ref, uext, dext):
        i = pl.program_id(0)

        @pl.when(i == 0)
        def _():
            dw_ref[...] = jnp.zeros_like(dw_ref)
            dv_ref[...] = jnp.zeros_like(dv_ref)

        row = i * TR + lax.broadcasted_iota(jnp.int32, (TR, 1), 0)
        hrow = i * TR - CHUNK + lax.broadcasted_iota(jnp.int32, (CHUNK, 1), 0)
        uext[pl.ds(0, CHUNK), :] = jnp.where(i > 0, _glu(prev_ref[...], hrow), 0.0)
        uext[pl.ds(CHUNK, TR), :] = _glu(cur_ref[...], row)

        gate = cur_ref[:, 2 * D_CONV:3 * D_CONV].astype(F32)
        dout = dy_ref[...].astype(F32)
        dyc, dyn, xhat, yn = ln_bwd(yc_ref[...], dout, gate, v_ref)
        dgate = dout * _silu(yn) * _dsilu(gate)
        dyc_n, _, _, _ = ln_bwd(ycn_ref[...], dyn_ref[...].astype(F32),
                                next_ref[:, 2 * D_CONV:3 * D_CONV].astype(F32), v_ref)
        dext[pl.ds(0, TR), :] = dyc
        dext[pl.ds(TR, CHUNK), :] = jnp.where(i < nt - 1, dyc_n, 0.0)

        dv_ref[0:1, :] += jnp.sum(dyc, axis=0, keepdims=True)
        dv_ref[1:2, :] += jnp.sum(dyn * xhat, axis=0, keepdims=True)
        dv_ref[2:3, :] += jnp.sum(dyn, axis=0, keepdims=True)

        dglu = jnp.zeros((TR, D_CONV), F32)
        for j in range(CONV_WIDTH):
            dglu = dglu + dext[pl.ds(CONV_WIDTH - 1 - j, TR), :] * w_ref[j:j + 1, :]
            dw_ref[j:j + 1, :] += jnp.sum(dyc * uext[pl.ds(CHUNK - (CONV_WIDTH - 1) + j, TR), :], axis=0, keepdims=True)

        a = cur_ref[:, 0:D_CONV].astype(F32)
        sg = _sig(cur_ref[:, D_CONV:2 * D_CONV].astype(F32))
        dglu = jnp.where(row >= META_PAD, dglu, 0.0)
        du_ref[:, 0:D_CONV] = (dglu * sg).astype(BF)
        du_ref[:, D_CONV:2 * D_CONV] = (dglu * a * sg * (1.0 - sg)).astype(BF)
        du_ref[:, 2 * D_CONV:3 * D_CONV] = dgate.astype(BF)

    nxt = lambda i: (jnp.minimum(i * hb + hb, nhb - 1), 0)
    return pl.pallas_call(
        body, name="conv_bwd", grid=(nt,),
        in_specs=[pl.BlockSpec((TR, N_A), lambda i: (i, 0)),
                  pl.BlockSpec((CHUNK, N_A), lambda i: (jnp.maximum(i * hb - 1, 0), 0)),
                  pl.BlockSpec((CHUNK, N_A), nxt),
                  pl.BlockSpec((TR, D_CONV), lambda i: (i, 0)),
                  pl.BlockSpec((CHUNK, D_CONV), nxt),
                  pl.BlockSpec((TR, D_CONV), lambda i: (i, 0)),
                  pl.BlockSpec((CHUNK, D_CONV), nxt),
                  pl.BlockSpec((CONV_WIDTH, D_CONV), lambda i: (0, 0)),
                  pl.BlockSpec((8, D_CONV), lambda i: (0, 0))],
        out_specs=[pl.BlockSpec((TR, N_A), lambda i: (i, 0)),
                   pl.BlockSpec((32, D_CONV), lambda i: (0, 0)),
                   pl.BlockSpec((8, D_CONV), lambda i: (0, 0))],
        out_shape=[jax.ShapeDtypeStruct((Lp, N_A), BF), jax.ShapeDtypeStruct((32, D_CONV), F32),
                   jax.ShapeDtypeStruct((8, D_CONV), F32)],
        scratch_shapes=[pltpu.VMEM((TR + CHUNK, D_CONV), F32), pltpu.VMEM((TR + CHUNK, D_CONV), F32)],
        compiler_params=_cp(("arbitrary",)),
    )(ua, ua, ua, yconv, yconv, dya, dya, cw, cvec)


def _hg_gates(ub_ref, lbv, row):
    q = ub_ref[:, 0:512].astype(F32)
    z = ub_ref[:, 512:1024].astype(F32)
    valid = row >= META_PAD
    sig = _sig(z)
    f = lbv + (1.0 - lbv) * sig
    g = jnp.where(valid, jnp.log(jnp.maximum(f, F_FLOOR)), 0.0)
    k = jnp.where(valid, (1.0 - lbv) * _sig(-z), 0.0)
    return q, k, g, sig, f


def _hg_chunk_terms(b_c, q_c, k_c):
    bm = b_c[CHUNK // 2 - 1:CHUNK // 2, :]
    bl = b_c[CHUNK - 1:CHUNK, :]
    e1 = jnp.exp(b_c - bm)
    e2 = jnp.exp(bm - b_c)
    e0 = jnp.exp(b_c)
    e3 = jnp.exp(bl - b_c)
    el = jnp.exp(bl)
    return e1, e2, e0, e3, el, q_c * e1, k_c * e2, q_c * e0, k_c * e3


def _hg_fwd(ub, lb, gn4):
    Lp = ub.shape[0]
    nt = Lp // TR
    cpt = TR // CHUNK

    def body(ub_ref, lb_ref, gn_ref, yb_ref, o_ref, ss_ref, st, bsc, qsc, ksc):
        i = pl.program_id(0)

        @pl.when(i == 0)
        def _():
            st[...] = jnp.zeros_like(st)

        row = i * TR + lax.broadcasted_iota(jnp.int32, (TR, 1), 0)
        q, k, g, _, _ = _hg_gates(ub_ref, lb_ref[...], row)
        qsc[...] = _silu(q)
        ksc[...] = k
        bsc[...] = _mm3(_chunk_tri(TR, False), g)
        tri = lax.broadcasted_iota(jnp.int32, (CHUNK, CHUNK), 1) <= lax.broadcasted_iota(jnp.int32, (CHUNK, CHUNK), 0)

        def chunk(c, carry):
            r0 = pl.multiple_of(c * CHUNK, CHUNK)
            rows = pl.ds(r0, CHUNK)
            _, _, _, _, el, qe, ke, qE, kd = _hg_chunk_terms(bsc[rows, :], qsc[rows, :], ksc[rows, :])
            qe, ke, qE, kd = qe.astype(BF), ke.astype(BF), qE.astype(BF), kd.astype(BF)
            for h in range(HG_HEADS):
                sl = slice(HG_D * h, HG_D * (h + 1))
                v_h = ub_ref[rows, 1024 + HG_D * h:1024 + HG_D * (h + 1)]
                s_in = st[h]
                ss_ref[c, h] = s_in
                a = jnp.where(tri, _nt(qe[:, sl], ke[:, sl]), 0.0)
                o_ref[rows, sl] = _nn(a.astype(BF), v_h) + _nt(qE[:, sl], s_in.astype(BF))
                st[h] = el[:, sl] * s_in + _tn(v_h, kd[:, sl])
            return carry

        lax.fori_loop(0, cpt, chunk, 0)

        gate = ub_ref[:, 1536:2048].astype(F32)
        for h in range(HG_HEADS):
            sl = slice(HG_D * h, HG_D * (h + 1))
            o = o_ref[:, sl]
            r = lax.rsqrt(jnp.mean(o * o, axis=-1, keepdims=True) + EPS)
            yb_ref[:, sl] = (o * r * gn_ref[:, sl] * _silu(gate[:, sl])).astype(BF)

    return pl.pallas_call(
        body, name="hgrn_fwd", grid=(nt,),
        in_specs=[pl.BlockSpec((TR, N_B), lambda i: (i, 0)), pl.BlockSpec((1, 512), lambda i: (0, 0)),
                  pl.BlockSpec((1, 512), lambda i: (0, 0))],
        out_specs=[pl.BlockSpec((TR, 512), lambda i: (i, 0)), pl.BlockSpec((TR, 512), lambda i: (i, 0)),
                   pl.BlockSpec((cpt, HG_HEADS, HG_D, HG_D), lambda i: (i, 0, 0, 0))],
        out_shape=[jax.ShapeDtypeStruct((Lp, 512), BF), jax.ShapeDtypeStruct((Lp, 512), F32),
                   jax.ShapeDtypeStruct((Lp // CHUNK, HG_HEADS, HG_D, HG_D), F32)],
        scratch_shapes=[pltpu.VMEM((HG_HEADS, HG_D, HG_D), F32), pltpu.VMEM((TR, 512), F32),
                        pltpu.VMEM((TR, 512), F32), pltpu.VMEM((TR, 512), F32)],
        compiler_params=_cp(("arbitrary",)),
    )(ub, lb, gn4)


def _hg_bwd(ub, lb, gn4, o_save, s_save, dyb):
    Lp = ub.shape[0]
    nt = Lp // TR
    cpt = TR // CHUNK

    def body(ub_ref, lb_ref, gn_ref, o_ref, ss_ref, dy_ref, du_ref, ds_ref,
             dst, bsc, qsc, ksc, dosc, dqsc, dksc, dbsc):
        i = pl.program_id(0)
        t = nt - 1 - i

        @pl.when(i == 0)
        def _():
            dst[...] = jnp.zeros_like(dst)
            ds_ref[...] = jnp.zeros_like(ds_ref)

        lbv = lb_ref[...]
        row = t * TR + lax.broadcasted_iota(jnp.int32, (TR, 1), 0)
        valid = row >= META_PAD
        q, k, g, sig, f = _hg_gates(ub_ref, lbv, row)
        qsc[...] = _silu(q)
        ksc[...] = k
        bsc[...] = _mm3(_chunk_tri(TR, False), g)

        gate = ub_ref[:, 1536:2048].astype(F32)
        dy = dy_ref[...].astype(F32)
        dgn = jnp.zeros((1, 512), F32)
        for h in range(HG_HEADS):
            sl = slice(HG_D * h, HG_D * (h + 1))
            o = o_ref[:, sl]
            r = lax.rsqrt(jnp.mean(o * o, axis=-1, keepdims=True) + EPS)
            ohat = o * r
            don = dy[:, sl] * _silu(gate[:, sl])
            du_ref[:, 1536 + HG_D * h:1536 + HG_D * (h + 1)] = (
                dy[:, sl] * ohat * gn_ref[:, sl] * _dsilu(gate[:, sl])).astype(BF)
            ds_ref[1:2, sl] += jnp.sum(don * ohat, axis=0, keepdims=True)
            gd = don * gn_ref[:, sl]
            dosc[:, sl] = r * (gd - ohat * jnp.mean(gd * ohat, axis=-1, keepdims=True))

        tri = lax.broadcasted_iota(jnp.int32, (CHUNK, CHUNK), 1) <= lax.broadcasted_iota(jnp.int32, (CHUNK, CHUNK), 0)
        last = lax.broadcasted_iota(jnp.int32, (CHUNK, 1), 0) == CHUNK - 1

        def chunk(cc, carry):
            c = cpt - 1 - cc
            r0 = pl.multiple_of(c * CHUNK, CHUNK)
            rows = pl.ds(r0, CHUNK)
            e1, e2, e0, e3, el, qe, ke, qE, kd = _hg_chunk_terms(bsc[rows, :], qsc[rows, :], ksc[rows, :])
            qe_b, ke_b, qE_b, kd_b = qe.astype(BF), ke.astype(BF), qE.astype(BF), kd.astype(BF)
            do_c = dosc[rows, :].astype(BF)
            for h in range(HG_HEADS):
                sl = slice(HG_D * h, HG_D * (h + 1))
                v_h = ub_ref[rows, 1024 + HG_D * h:1024 + HG_D * (h + 1)]
                do_h = do_c[:, sl]
                s_in = ss_ref[c, h]
                d_s = dst[h]
                d_s_b = d_s.astype(BF)
                a = jnp.where(tri, _nt(qe_b[:, sl], ke_b[:, sl]), 0.0).astype(BF)
                da = jnp.where(tri, _nt(do_h, v_h), 0.0).astype(BF)
                dv = _tn(a, do_h) + _nt(kd_b[:, sl], d_s_b)
                dqE = _nn(do_h, s_in.astype(BF))
                dqe = _nn(da, ke_b[:, sl])
                dke = _tn(da, qe_b[:, sl])
                dkd = _nn(v_h, d_s_b)
                del_h = jnp.sum(s_in * d_s, axis=0, keepdims=True)
                dst[h] = el[:, sl] * d_s + _tn(do_h, qE_b[:, sl])
                dqsc[rows, sl] = dqE * e0[:, sl] + dqe * e1[:, sl]
                dksc[rows, sl] = dke * e2[:, sl] + dkd * e3[:, sl]
                tkd = dkd * kd[:, sl]
                dbl = jnp.sum(tkd, axis=0, keepdims=True) + del_h * el[:, sl]
                dbsc[rows, sl] = dqE * qE[:, sl] + dqe * qe[:, sl] - dke * ke[:, sl] - tkd + jnp.where(last, dbl, 0.0)
                du_ref[rows, 1024 + HG_D * h:1024 + HG_D * (h + 1)] = dv.astype(BF)
            return carry

        lax.fori_loop(0, cpt, chunk, 0)

        dg = _mm3(_chunk_tri(TR, True), dbsc[...])
        df = jnp.where(valid & (f > F_FLOOR), dg / f, 0.0)
        dk = jnp.where(valid, dksc[...], 0.0)
        nsig = _sig(-ub_ref[:, 512:1024].astype(F32))
        dsig = (df - dk) * (1.0 - lbv)
        ds_ref[0:1, :] += jnp.sum(df * (1.0 - sig) - dk * nsig, axis=0, keepdims=True)
        du_ref[:, 512:1024] = (dsig * sig * (1.0 - sig)).astype(BF)
        du_ref[:, 0:512] = (dqsc[...] * _dsilu(q)).astype(BF)

    rev = lambda i: (nt - 1 - i, 0)
    return pl.pallas_call(
        body, name="hgrn_bwd", grid=(nt,),
        in_specs=[pl.BlockSpec((TR, N_B), rev), pl.BlockSpec((1, 512), lambda i: (0, 0)),
                  pl.BlockSpec((1, 512), lambda i: (0, 0)), pl.BlockSpec((TR, 512), rev),
                  pl.BlockSpec((cpt, HG_HEADS, HG_D, HG_D), lambda i: (nt - 1 - i, 0, 0, 0)),
                  pl.BlockSpec((TR, 512), rev)],
        out_specs=[pl.BlockSpec((TR, N_B), rev), pl.BlockSpec((8, 512), lambda i: (0, 0))],
        out_shape=[jax.ShapeDtypeStruct((Lp, N_B), BF), jax.ShapeDtypeStruct((8, 512), F32)],
        scratch_shapes=[pltpu.VMEM((HG_HEADS, HG_D, HG_D), F32)] + [pltpu.VMEM((TR, 512), F32)] * 7,
        compiler_params=_cp(("arbitrary",)),
    )(ub, lb, gn4, o_save, s_save, dyb)


_KCOL = (2 * 512) // 128
_VCOL = _KCOL + 1
_WIN = CHUNK + HALO + TR


def _swa_in_specs(nt, rev):
    tile = (lambda i: nt - 1 - i) if rev else (lambda i: i)
    hpt = TR // HALO
    return [
        pl.BlockSpec((TR, 512), lambda i: (tile(i), 0)),
        pl.BlockSpec((TR, 512), lambda i: (tile(i), 1)),
        pl.BlockSpec((TR, 128), lambda i: (tile(i), _KCOL)),
        pl.BlockSpec((TR, 128), lambda i: (tile(i), _VCOL)),
        pl.BlockSpec((HALO, 128), lambda i: (jnp.maximum(tile(i) * hpt - 1, 0), _KCOL)),
        pl.BlockSpec((HALO, 128), lambda i: (jnp.maximum(tile(i) * hpt - 1, 0), _VCOL)),
        pl.BlockSpec((CHUNK, 128), lambda i: (0, _KCOL)),
        pl.BlockSpec((CHUNK, 128), lambda i: (0, _VCOL)),
        pl.BlockSpec((1, ATT_HD), lambda i: (0, 0)),
        pl.BlockSpec((1, ATT_HD), lambda i: (0, 0)),
        pl.BlockSpec((1, ATT_Q_HEADS), lambda i: (0, 0)),
    ]


def _head_norm(x, gain):
    r = lax.rsqrt(jnp.mean(x * x, axis=-1, keepdims=True) + EPS)
    xhat = x * r
    return xhat * gain, xhat, r


def _swa_fill_windows(t, kc_ref, vc_ref, kh_ref, vh_ref, km_ref, vm_ref, kg, kwin, vwin):
    for (src, r0, n) in ((km_ref, 0, CHUNK), (kh_ref, CHUNK, HALO), (kc_ref, CHUNK + HALO, TR)):
        x = src[...].astype(F32)
        for hh in range(ATT_KV_HEADS):
            sl = slice(ATT_HD * hh, ATT_HD * (hh + 1))
            kn, _, _ = _head_norm(x[:, sl], kg)
            kwin[pl.ds(r0, n), sl] = kn.astype(BF)
    vwin[pl.ds(0, CHUNK), :] = vm_ref[...]
    vwin[pl.ds(CHUNK, HALO), :] = vh_ref[...]
    vwin[pl.ds(CHUNK + HALO, TR), :] = vc_ref[...]


def _swa_masks(t, qb):
    q0 = t * TR + qb * QB
    qc = jnp.right_shift(q0 + lax.broadcasted_iota(jnp.int32, (QB, 1), 0), 6)
    kabs = q0 - HALO + lax.broadcasted_iota(jnp.int32, (1, QB + HALO), 1)
    kc = jnp.right_shift(kabs + HALO, 6) - HALO // CHUNK
    mask_w = (kc <= qc) & (kc >= qc - 2) & (kabs >= META_PAD)
    mask_m = (qc > 2) & (lax.broadcasted_iota(jnp.int32, (1, CHUNK), 1) >= META_PAD)
    return mask_m, mask_w


def _swa_fwd(uc, qg, kg, sinks):
    Lp = uc.shape[0]
    nt = Lp // TR
    nqb = TR // QB

    def body(q_ref, g_ref, kc_ref, vc_ref, kh_ref, vh_ref, km_ref, vm_ref, qg_ref, kg_ref, sk_ref,
             yc_ref, o_ref, lse_ref, kwin, vwin, qn):
        t = pl.program_id(0)
        _swa_fill_windows(t, kc_ref, vc_ref, kh_ref, vh_ref, km_ref, vm_ref, kg_ref[...], kwin, vwin)
        for j in range(ATT_Q_HEADS):
            sl = slice(ATT_HD * j, ATT_HD * (j + 1))
            y, _, _ = _head_norm(q_ref[:, sl].astype(F32), qg_ref[...])
            qn[:, sl] = (y * (ATT_HD ** -0.5)).astype(BF)
        lane8 = lax.broadcasted_iota(jnp.int32, (QB, ATT_Q_HEADS), 1)

        def qblock(qb, carry):
            r0 = pl.multiple_of(qb * QB, QB)
            rows = pl.ds(r0, QB)
            wrows = pl.ds(pl.multiple_of(CHUNK + qb * QB, CHUNK), QB + HALO)
            mask_m, mask_w = _swa_masks(t, qb)
            lse_all = jnp.zeros((QB, ATT_Q_HEADS), F32)
            for j in range(ATT_Q_HEADS):
                sl = slice(ATT_HD * j, ATT_HD * (j + 1))
                kv = j // ATT_GROUP
                ksl = slice(ATT_HD * kv, ATT_HD * (kv + 1))
                qh = qn[rows, sl]
                s_m = jnp.where(mask_m, _nt(qh, kwin[pl.ds(0, CHUNK), ksl]), NEG)
                s_w = jnp.where(mask_w, _nt(qh, kwin[wrows, ksl]), NEG)
                sink = sk_ref[:, j:j + 1]
                m = jnp.maximum(jnp.maximum(jnp.max(s_m, axis=-1, keepdims=True),
                                            jnp.max(s_w, axis=-1, keepdims=True)), sink)
                p_m = jnp.exp(s_m - m)
                p_w = jnp.exp(s_w - m)
                den = jnp.sum(p_m, axis=-1, keepdims=True) + jnp.sum(p_w, axis=-1, keepdims=True) + jnp.exp(sink - m)
                o = (_nn(p_m.astype(BF), vwin[pl.ds(0, CHUNK), ksl]) + _nn(p_w.astype(BF), vwin[wrows, ksl])) / den
                o_ref[rows, sl] = o
                lse_all = jnp.where(lane8 == j, m + jnp.log(den), lse_all)
            lse_ref[rows, :] = lse_all
            return carry

        lax.fori_loop(0, nqb, qblock, 0)
        yc_ref[...] = (o_ref[...] * _silu(g_ref[...].astype(F32))).astype(BF)

    return pl.pallas_call(
        body, name="swa_fwd", grid=(nt,), in_specs=_swa_in_specs(nt, False),
        out_specs=[pl.BlockSpec((TR, 512), lambda i: (i, 0)), pl.BlockSpec((TR, 512), lambda i: (i, 0)),
                   pl.BlockSpec((TR, ATT_Q_HEADS), lambda i: (i, 0))],
        out_shape=[jax.ShapeDtypeStruct((Lp, 512), BF), jax.ShapeDtypeStruct((Lp, 512), F32),
                   jax.ShapeDtypeStruct((Lp, ATT_Q_HEADS), F32)],
        scratch_shapes=[pltpu.VMEM((_WIN, 128), BF), pltpu.VMEM((_WIN, 128), BF), pltpu.VMEM((TR, 512), BF)],
        compiler_params=_cp(("parallel",)),
    )(uc, uc, uc, uc, uc, uc, uc, uc, qg, kg, sinks)


def _swa_bwd(uc, qg, kg, sinks, o_save, lse, dyc):
    Lp = uc.shape[0]
    nt = Lp // TR
    nqb = TR // QB
    C0 = CHUNK + HALO

    def body(q_ref, g_ref, kc_ref, vc_ref, kh_ref, vh_ref, km_ref, vm_ref, qg_ref, kg_ref, sk_ref,
             o_ref, lse_ref, dy_ref, du_ref, ds_ref,
             kwin, vwin, qn, dosc, dqacc, dkacc, dvacc, carry_k, carry_v, meta_k, meta_v):
        i = pl.program_id(0)
        t = nt - 1 - i

        @pl.when(i == 0)
        def _():
            carry_k[...] = jnp.zeros_like(carry_k)
            carry_v[...] = jnp.zeros_like(carry_v)
            meta_k[...] = jnp.zeros_like(meta_k)
            meta_v[...] = jnp.zeros_like(meta_v)
            ds_ref[...] = jnp.zeros_like(ds_ref)

        _swa_fill_windows(t, kc_ref, vc_ref, kh_ref, vh_ref, km_ref, vm_ref, kg_ref[...], kwin, vwin)
        for j in range(ATT_Q_HEADS):
            sl = slice(ATT_HD * j, ATT_HD * (j + 1))
            y, _, _ = _head_norm(q_ref[:, sl].astype(F32), qg_ref[...])
            qn[:, sl] = (y * (ATT_HD ** -0.5)).astype(BF)
        gate = g_ref[...].astype(F32)
        dy = dy_ref[...].astype(F32)
        dosc[...] = dy * _silu(gate)
        du_ref[:, 512:1024] = (dy * o_ref[...] * _dsilu(gate)).astype(BF)
        dkacc[...] = jnp.zeros_like(dkacc)
        dvacc[...] = jnp.zeros_like(dvacc)
        lane128 = lax.broadcasted_iota(jnp.int32, (1, 128), 1)

        def qblock(qb, dsink):
            r0 = pl.multiple_of(qb * QB, QB)
            rows = pl.ds(r0, QB)
            wrows = pl.ds(pl.multiple_of(CHUNK + qb * QB, CHUNK), QB + HALO)
            mrows = pl.ds(0, CHUNK)
            mask_m, mask_w = _swa_masks(t, qb)
            lse_blk = lse_ref[rows, :]
            for j in range(ATT_Q_HEADS):
                sl = slice(ATT_HD * j, ATT_HD * (j + 1))
                kv = j // ATT_GROUP
                ksl = slice(ATT_HD * kv, ATT_HD * (kv + 1))
                qh = qn[rows, sl]
                lse_j = lse_blk[:, j:j + 1]
                p_m = jnp.exp(jnp.where(mask_m, _nt(qh, kwin[mrows, ksl]), NEG) - lse_j)
                p_w = jnp.exp(jnp.where(mask_w, _nt(qh, kwin[wrows, ksl]), NEG) - lse_j)
                do_h = dosc[rows, sl]
                dd = jnp.sum(do_h * o_ref[rows, sl], axis=-1, keepdims=True)
                do_b = do_h.astype(BF)
                ds_m = (p_m * (_nt(do_b, vwin[mrows, ksl]) - dd)).astype(BF)
                ds_w = (p_w * (_nt(do_b, vwin[wrows, ksl]) - dd)).astype(BF)
                dqacc[rows, sl] = _nn(ds_m, kwin[mrows, ksl]) + _nn(ds_w, kwin[wrows, ksl])
                dkacc[mrows, ksl] += _tn(ds_m, qh)
                dkacc[wrows, ksl] += _tn(ds_w, qh)
                dvacc[mrows, ksl] += _tn(p_m.astype(BF), do_b)
                dvacc[wrows, ksl] += _tn(p_w.astype(BF), do_b)
                ps = jnp.exp(sk_ref[:, j:j + 1] - lse_j)
                dsink = dsink - jnp.where(lane128 == j, jnp.sum(ps * dd, axis=0, keepdims=True), 0.0)
            return dsink

        dsink = lax.fori_loop(0, nqb, qblock, jnp.zeros((1, 128), F32))
        ds_ref[2:3, :] += dsink

        meta_k[...] += dkacc[pl.ds(0, CHUNK), :]
        meta_v[...] += dvacc[pl.ds(0, CHUNK), :]
        first = jnp.where(t == 0, 1.0, 0.0)
        dkacc[pl.ds(C0 + TR - HALO, HALO), :] += carry_k[...]
        dvacc[pl.ds(C0 + TR - HALO, HALO), :] += carry_v[...]
        dkacc[pl.ds(C0, CHUNK), :] += first * meta_k[...]
        dvacc[pl.ds(C0, CHUNK), :] += first * meta_v[...]
        carry_k[...] = dkacc[pl.ds(CHUNK, HALO), :]
        carry_v[...] = dvacc[pl.ds(CHUNK, HALO), :]

        du_ref[:, 1152:1280] = dvacc[pl.ds(C0, TR), :].astype(BF)
        kraw = kc_ref[...].astype(F32)
        dkg = jnp.zeros((1, ATT_HD), F32)
        for hh in range(ATT_KV_HEADS):
            sl = slice(ATT_HD * hh, ATT_HD * (hh + 1))
            _, xhat, r = _head_norm(kraw[:, sl], kg_ref[...])
            dkn = dkacc[pl.ds(C0, TR), sl]
            dkg = dkg + jnp.sum(dkn * xhat, axis=0, keepdims=True)
            gd = dkn * kg_ref[...]
            du_ref[:, 1024 + ATT_HD * hh:1024 + ATT_HD * (hh + 1)] = (
                r * (gd - xhat * jnp.mean(gd * xhat, axis=-1, keepdims=True))).astype(BF)
        ds_ref[1:2, 0:ATT_HD] += dkg
        dqg = jnp.zeros((1, ATT_HD), F32)
        for j in range(ATT_Q_HEADS):
            sl = slice(ATT_HD * j, ATT_HD * (j + 1))
            _, xhat, r = _head_norm(q_ref[:, sl].astype(F32), qg_ref[...])
            dqn = dqacc[:, sl] * (ATT_HD ** -0.5)
            dqg = dqg + jnp.sum(dqn * xhat, axis=0, keepdims=True)
            gd = dqn * qg_ref[...]
            du_ref[:, sl] = (r * (gd - xhat * jnp.mean(gd * xhat, axis=-1, keepdims=True))).astype(BF)
        ds_ref[0:1, 0:ATT_HD] += dqg

    rev = lambda i: (nt - 1 - i, 0)
    return pl.pallas_call(
        body, name="swa_bwd", grid=(nt,),
        in_specs=_swa_in_specs(nt, True) + [pl.BlockSpec((TR, 512), rev), pl.BlockSpec((TR, ATT_Q_HEADS), rev),
                                            pl.BlockSpec((TR, 512), rev)],
        out_specs=[pl.BlockSpec((TR, N_C), rev), pl.BlockSpec((8, 128), lambda i: (0, 0))],
        out_shape=[jax.ShapeDtypeStruct((Lp, N_C), BF), jax.ShapeDtypeStruct((8, 128), F32)],
        scratch_shapes=[pltpu.VMEM((_WIN, 128), BF), pltpu.VMEM((_WIN, 128), BF), pltpu.VMEM((TR, 512), BF),
                        pltpu.VMEM((TR, 512), F32), pltpu.VMEM((TR, 512), F32),
                        pltpu.VMEM((_WIN, 128), F32), pltpu.VMEM((_WIN, 128), F32),
                        pltpu.VMEM((HALO, 128), F32), pltpu.VMEM((HALO, 128), F32),
                        pltpu.VMEM((CHUNK, 128), F32), pltpu.VMEM((CHUNK, 128), F32)],
        compiler_params=_cp(("arbitrary",)),
    )(uc, uc, uc, uc, uc, uc, uc, uc, qg, kg, sinks, o_save, lse, dyc)


def _mix_fwd(h, ya, yb, yc, ug, wa, wb, wc, wo):
    Lp = h.shape[0]
    wspec = lambda r: pl.BlockSpec((r, D_MODEL), lambda i: (0, 0))
    yspec = pl.BlockSpec((TRM, 512), lambda i: (i, 0))
    hspec = pl.BlockSpec((TRM, D_MODEL), lambda i: (i, 0))

    def body(h_ref, ya_ref, yb_ref, yc_ref, ug_ref, wa_ref, wb_ref, wc_ref, wo_ref, hn_ref, za_ref, zb_ref, zc_ref):
        mixed = jnp.zeros((TRM, D_MODEL), F32)
        for n, (y_ref, w_ref, z_ref) in enumerate(((ya_ref, wa_ref, za_ref), (yb_ref, wb_ref, zb_ref),
                                                   (yc_ref, wc_ref, zc_ref))):
            z = _nn(y_ref[...], w_ref[...])
            z_ref[...] = z.astype(BF)
            mixed = mixed + _sig(ug_ref[:, D_MODEL * n:D_MODEL * (n + 1)].astype(F32)) * z
        hn_ref[...] = h_ref[...] + _nn(mixed.astype(BF), wo_ref[...])

    return pl.pallas_call(
        body, name="mix_fwd", grid=(Lp // TRM,),
        in_specs=[hspec, yspec, yspec, yspec, pl.BlockSpec((TRM, N_G), lambda i: (i, 0)),
                  wspec(512), wspec(512), wspec(512), wspec(D_MODEL)],
        out_specs=[hspec, hspec, hspec, hspec],
        out_shape=[jax.ShapeDtypeStruct((Lp, D_MODEL), F32)] + [jax.ShapeDtypeStruct((Lp, D_MODEL), BF)] * 3,
        compiler_params=_cp(("parallel",)),
    )(h, ya, yb, yc, ug, wa, wb, wc, wo)


def _mix_bwd(dh, za, zb, zc, ug, wa, wb, wc, wo):
    Lp = dh.shape[0]
    wspec = lambda r: pl.BlockSpec((r, D_MODEL), lambda i: (0, 0))
    yspec = pl.BlockSpec((TRM, 512), lambda i: (i, 0))
    hspec = pl.BlockSpec((TRM, D_MODEL), lambda i: (i, 0))
    gspec = pl.BlockSpec((TRM, N_G), lambda i: (i, 0))

    def body(dh_ref, za_ref, zb_ref, zc_ref, ug_ref, wa_ref, wb_ref, wc_ref, wo_ref,
             dug_ref, mx_ref, dza_ref, dzb_ref, dzc_ref, dya_ref, dyb_ref, dyc_ref):
        dmix = _nt(dh_ref[...].astype(BF), wo_ref[...])
        mixed = jnp.zeros((TRM, D_MODEL), F32)
        for n, (z_ref, w_ref, dz_ref, dy_ref) in enumerate(((za_ref, wa_ref, dza_ref, dya_ref),
                                                            (zb_ref, wb_ref, dzb_ref, dyb_ref),
                                                            (zc_ref, wc_ref, dzc_ref, dyc_ref))):
            sl = slice(D_MODEL * n, D_MODEL * (n + 1))
            z = z_ref[...].astype(F32)
            gt = _sig(ug_ref[:, sl].astype(F32))
            mixed = mixed + gt * z
            dug_ref[:, sl] = (dmix * z * gt * (1.0 - gt)).astype(BF)
            dz = (dmix * gt).astype(BF)
            dz_ref[...] = dz
            dy_ref[...] = _nt(dz, w_ref[...]).astype(BF)
        mx_ref[...] = mixed.astype(BF)

    bf = lambda n: jax.ShapeDtypeStruct((Lp, n), BF)
    return pl.pallas_call(
        body, name="mix_bwd", grid=(Lp // TRM,),
        in_specs=[hspec, hspec, hspec, hspec, gspec, wspec(512), wspec(512), wspec(512), wspec(D_MODEL)],
        out_specs=[gspec, hspec, hspec, hspec, hspec, yspec, yspec, yspec],
        out_shape=[bf(N_G), bf(D_MODEL), bf(D_MODEL), bf(D_MODEL), bf(D_MODEL), bf(512), bf(512), bf(512)],
        compiler_params=_cp(("parallel",)),
    )(dh, za, zb, zc, ug, wa, wb, wc, wo)


def _inproj_bwd(dus, ws, h, dh, g):
    Lp = h.shape[0]
    widths = [w.shape[1] for w in ws]

    def body(dg_ref, da_ref, db_ref, dc_ref, wg_ref, wa_ref, wb_ref, wc_ref, h_ref, dh_ref, g_ref, o_ref, gg_ref):
        @pl.when(pl.program_id(0) == 0)
        def _():
            gg_ref[...] = jnp.zeros_like(gg_ref)

        dhn = (_nt(dg_ref[...], wg_ref[...]) + _nt(da_ref[...], wa_ref[...])
               + _nt(db_ref[...], wb_ref[...]) + _nt(dc_ref[...], wc_ref[...]))
        x = h_ref[...]
        r = lax.rsqrt(jnp.mean(x * x, axis=-1, keepdims=True) + EPS)
        xhat = x * r
        gg_ref[0:1, :] += jnp.sum(dhn * xhat, axis=0, keepdims=True)
        gd = dhn * g_ref[...]
        o_ref[...] = dh_ref[...] + r * (gd - xhat * jnp.mean(gd * xhat, axis=-1, keepdims=True))

    hspec = pl.BlockSpec((TRM, D_MODEL), lambda i: (i, 0))
    return pl.pallas_call(
        body, name="inproj_bwd", grid=(Lp // TRM,),
        in_specs=[pl.BlockSpec((TRM, n), lambda i: (i, 0)) for n in widths]
        + [pl.BlockSpec((D_MODEL, n), lambda i: (0, 0), pipeline_mode=pl.Buffered(1)) for n in widths]
        + [hspec, hspec, pl.BlockSpec((1, D_MODEL), lambda i: (0, 0))],
        out_specs=[hspec, pl.BlockSpec((8, D_MODEL), lambda i: (0, 0))],
        out_shape=[jax.ShapeDtypeStruct((Lp, D_MODEL), F32), jax.ShapeDtypeStruct((8, D_MODEL), F32)],
        compiler_params=_cp(("arbitrary",)),
    )(*dus, *ws, h, dh, g)


def _loss_head(h, tgt_pad, seq):
    Lp = h.shape[0]
    nt = Lp // TR

    def body(h_ref, t_ref, dh_ref, l_ref):
        i = pl.program_id(0)

        @pl.when(i == 0)
        def _():
            l_ref[...] = jnp.zeros_like(l_ref)

        row = i * TR + lax.broadcasted_iota(jnp.int32, (TR, 1), 0)
        e = jnp.where((row >= CHUNK) & (row < CHUNK + seq), h_ref[...] - t_ref[...], 0.0)
        dh_ref[...] = e * (1.0 / D_MODEL)
        l_ref[...] += (0.5 / D_MODEL) * jnp.sum(jnp.sum(e * e, axis=0, keepdims=True), axis=1, keepdims=True)

    hspec = pl.BlockSpec((TR, D_MODEL), lambda i: (i, 0))
    return pl.pallas_call(
        body, name="loss_head", grid=(nt,), in_specs=[hspec, hspec],
        out_specs=[hspec, pl.BlockSpec((8, 128), lambda i: (0, 0))],
        out_shape=[jax.ShapeDtypeStruct((Lp, D_MODEL), F32), jax.ShapeDtypeStruct((8, 128), F32)],
        compiler_params=_cp(("arbitrary",)),
    )(h, tgt_pad)


def _lb_softmax(lb_ref):
    x = lb_ref[...]
    e = jnp.exp(x - jnp.max(x, axis=0, keepdims=True))
    return e / jnp.sum(e, axis=0, keepdims=True)


def _lb_fwd(hg_lb):
    def body(lb_ref, o_ref):
        sm = _lb_softmax(lb_ref)
        acc = jnp.zeros((1, 512), F32)
        for l in range(DEPTH):
            if l > 0:
                acc = acc + sm[l:l + 1, :]
            o_ref[l:l + 1, :] = jnp.clip(acc, 0.0, 1.0)

    return pl.pallas_call(body, name="lb_fwd", out_shape=jax.ShapeDtypeStruct((DEPTH, 512), F32))(hg_lb)


def _lb_bwd(hg_lb, dlb_all):
    def body(lb_ref, d_ref, o_ref):
        sm = _lb_softmax(lb_ref)
        acc = jnp.zeros((1, 512), F32)
        gm = []
        for l in range(DEPTH):
            if l > 0:
                acc = acc + sm[l:l + 1, :]
            gm.append(jnp.where((acc >= 0.0) & (acc <= 1.0), d_ref[l:l + 1, :], 0.0))
        dsm = [jnp.zeros((1, 512), F32)]
        for j in range(1, DEPTH):
            s = gm[j]
            for l in range(j + 1, DEPTH):
                s = s + gm[l]
            dsm.append(s)
        dot = dsm[0] * sm[0:1, :]
        for j in range(1, DEPTH):
            dot = dot + dsm[j] * sm[j:j + 1, :]
        for j in range(DEPTH):
            o_ref[j:j + 1, :] = sm[j:j + 1, :] * (dsm[j] - dot)

    return pl.pallas_call(body, name="lb_bwd", out_shape=jax.ShapeDtypeStruct((DEPTH, 512), F32))(hg_lb, dlb_all)


_ANY = pl.BlockSpec(memory_space=pl.ANY)


def _chip_peers():
    x, y, c = lax.axis_index("x"), lax.axis_index("y"), lax.axis_index("c")
    return (x, y, c), [(1 - x, y, c), (x, 1 - y, c), (1 - x, 1 - y, c)]


def _gather_chips(arrs):
    n = len(arrs)

    def body(*refs):
        ins, outs = refs[:n], refs[n:2 * n]
        send, recv, loc = refs[2 * n:]
        (x, y, c), peers = _chip_peers()
        me = 2 * x + y
        copies = []
        for a in range(n):
            cp = pltpu.make_async_copy(ins[a], outs[a].at[me], loc.at[a])
            cp.start()
            copies.append(cp)
        rdmas = []
        for a in range(n):
            for p, peer in enumerate(peers):
                r = pltpu.make_async_remote_copy(src_ref=ins[a], dst_ref=outs[a].at[me], send_sem=send.at[a, p],
                                                 recv_sem=recv.at[a, p], device_id=peer, device_id_type=MESH)
                r.start()
                rdmas.append(r)
        for r in rdmas:
            r.wait()
        for cp in copies:
            cp.wait()

    return pl.pallas_call(
        body, name="gather_chips", in_specs=[_ANY] * n, out_specs=[_ANY] * n,
        out_shape=[jax.ShapeDtypeStruct((4,) + a.shape, a.dtype) for a in arrs],
        scratch_shapes=[pltpu.SemaphoreType.DMA((n, 3)), pltpu.SemaphoreType.DMA((n, 3)), pltpu.SemaphoreType.DMA((n,))],
        compiler_params=pltpu.CompilerParams(has_side_effects=True),
    )(*arrs)


def _scatter_chips(arrs):
    n = len(arrs)

    def body(*refs):
        ins, outs = refs[:n], refs[n:2 * n]
        send, recv, loc = refs[2 * n:]
        (x, y, c), peers = _chip_peers()
        me = 2 * x + y
        copies = []
        for a in range(n):
            cp = pltpu.make_async_copy(ins[a].at[me], outs[a].at[0], loc.at[a])
            cp.start()
            copies.append(cp)
        rdmas = []
        for a in range(n):
            for p, (px, py, pc) in enumerate(peers):
                r = pltpu.make_async_remote_copy(src_ref=ins[a].at[2 * px + py], dst_ref=outs[a].at[1 + p],
                                                 send_sem=send.at[a, p], recv_sem=recv.at[a, p],
                                                 device_id=(px, py, pc), device_id_type=MESH)
                r.start()
                rdmas.append(r)
        for r in rdmas:
            r.wait()
        for cp in copies:
            cp.wait()

    return pl.pallas_call(
        body, name="scatter_chips", in_specs=[_ANY] * n, out_specs=[_ANY] * n,
        out_shape=[jax.ShapeDtypeStruct(a.shape, a.dtype) for a in arrs],
        scratch_shapes=[pltpu.SemaphoreType.DMA((n, 3)), pltpu.SemaphoreType.DMA((n, 3)), pltpu.SemaphoreType.DMA((n,))],
        compiler_params=pltpu.CompilerParams(has_side_effects=True),
    )(*arrs)


def _swap_cores(arrs):
    n = len(arrs)

    def body(*refs):
        ins, outs = refs[:n], refs[n:2 * n]
        send, recv = refs[2 * n:]
        x, y, c = lax.axis_index("x"), lax.axis_index("y"), lax.axis_index("c")
        rdmas = []
        for a in range(n):
            r = pltpu.make_async_remote_copy(src_ref=ins[a], dst_ref=outs[a], send_sem=send.at[a], recv_sem=recv.at[a],
                                             device_id=(x, y, 1 - c), device_id_type=MESH)
            r.start()
            rdmas.append(r)
        for r in rdmas:
            r.wait()

    return pl.pallas_call(
        body, name="swap_cores", in_specs=[_ANY] * n, out_specs=[_ANY] * n,
        out_shape=[jax.ShapeDtypeStruct(a.shape, a.dtype) for a in arrs],
        scratch_shapes=[pltpu.SemaphoreType.DMA((n,)), pltpu.SemaphoreType.DMA((n,))],
        compiler_params=pltpu.CompilerParams(has_side_effects=True),
    )(*arrs)


def _allsum_small(p):
    R = p.shape[0]

    def body(p_ref, o_ref, buf, send, recv):
        x, y, c = lax.axis_index("x"), lax.axis_index("y"), lax.axis_index("c")
        me = 4 * x + 2 * y + c
        buf[me] = p_ref[...]
        rdmas = []
        for k in range(1, 8):
            peer = (x ^ (k >> 2), y ^ ((k >> 1) & 1), c ^ (k & 1))
            r = pltpu.make_async_remote_copy(src_ref=p_ref, dst_ref=buf.at[me], send_sem=send.at[k - 1],
                                             recv_sem=recv.at[k - 1], device_id=peer, device_id_type=MESH)
            r.start()
            rdmas.append(r)
        for r in rdmas:
            r.wait()
        acc = buf[0]
        for d in range(1, 8):
            acc = acc + buf[d]
        o_ref[...] = acc

    return pl.pallas_call(
        body, name="allsum_small", out_shape=jax.ShapeDtypeStruct((R, 512), F32),
        in_specs=[pl.BlockSpec(memory_space=pltpu.VMEM)], out_specs=pl.BlockSpec(memory_space=pltpu.VMEM),
        scratch_shapes=[pltpu.VMEM((8, R, 512), F32), pltpu.SemaphoreType.DMA((7,)), pltpu.SemaphoreType.DMA((7,))],
        compiler_params=_cp(has_side_effects=True),
    )(p)


def _sum4(parts, name):
    _, R, C = parts.shape
    tr = 256 if R % 256 == 0 else R

    def body(p_ref, o_ref):
        o_ref[...] = ((p_ref[0] + p_ref[1]) + p_ref[2]) + p_ref[3]

    return pl.pallas_call(
        body, name=name, grid=(R // tr,), in_specs=[pl.BlockSpec((4, tr, C), lambda i: (0, i, 0))],
        out_specs=pl.BlockSpec((tr, C), lambda i: (i, 0)), out_shape=jax.ShapeDtypeStruct((R, C), F32),
        compiler_params=_cp(("parallel",)),
    )(parts)


def _adamw(w, m, v, g0, g1, name):
    R, C = w.shape
    tr = 256 if R % 256 == 0 else R
    two = g1 is not None
    c1 = 1.0 / (1.0 - ADAM_B1 ** ADAM_STEP)
    c2 = 1.0 / (1.0 - ADAM_B2 ** ADAM_STEP)

    def body(*refs):
        if two:
            w_ref, m_ref, v_ref, a_ref, b_ref, g_ref, d_ref, nm_ref, nv_ref = refs
            g = a_ref[...] + b_ref[...]
        else:
            w_ref, m_ref, v_ref, a_ref, g_ref, d_ref, nm_ref, nv_ref = refs
            g = a_ref[...]
        g_ref[...] = g
        m = ADAM_B1 * m_ref[...] + (1.0 - ADAM_B1) * g
        v = ADAM_B2 * v_ref[...] + (1.0 - ADAM_B2) * (g * g)
        nm_ref[...] = m
        nv_ref[...] = v
        d_ref[...] = -ADAM_LR * ((m * c1) / (jnp.sqrt(v * c2) + ADAM_EPS) + ADAM_WD * w_ref[...])

    spec = pl.BlockSpec((tr, C), lambda i: (i, 0))
    n_in = 5 if two else 4
    ins = (w, m, v, g0, g1) if two else (w, m, v, g0)
    return pl.pallas_call(
        body, name=name, grid=(R // tr,), in_specs=[spec] * n_in, out_specs=[spec] * 4,
        out_shape=[jax.ShapeDtypeStruct((R, C), F32)] * 4, compiler_params=_cp(("parallel",)),
    )(*ins)


def _pad8(a):
    r = (-a.shape[0]) % 8
    return a if r == 0 else jnp.pad(a, ((0, r), (0, 0)))


def _local_step(x, tgt, meta, P):
    seq = x.shape[0]
    assert seq % TR == 0
    Lp = seq + TR
    h = jnp.concatenate([jnp.zeros((META_PAD, D_MODEL), F32), meta, x, jnp.zeros((TAIL_PAD, D_MODEL), F32)], axis=0)
    tgt_pad = jnp.pad(tgt, ((CHUNK, TAIL_PAD), (0, 0)))

    saved = []
    for l in range(DEPTH):
        p = P[l]
        hn = _rms_fwd(h, p["norm_g"])
        mm = functools.partial(_matmul, out_dtype=BF, tm=TR, tk=D_MODEL, col_major_grid=True)
        ug = mm(hn, p["w_g"], tn=N_G // 2, name="inproj_g")
        ua = mm(hn, p["w_a"], tn=N_A, name="inproj_a")
        ub = mm(hn, p["w_b"], tn=N_B, name="inproj_b")
        uc = mm(hn, p["w_c"], tn=N_C, name="inproj_c")
        ya, yconv = _conv_fwd(ua, p["conv_w"], p["conv_vec"])
        yb, o_hg, s_hg = _hg_fwd(ub, p["lb"], p["gn4"])
        yc, o_at, lse = _swa_fwd(uc, p["qg"], p["kg"], p["sinks"])
        h_new, za, zb, zc = _mix_fwd(h, ya, yb, yc, ug, p["w_ao"], p["w_bo"], p["w_co"], p["w_out"])
        saved.append(dict(h=h, hn=hn, ug=ug, ua=ua, ub=ub, uc=uc, ya=ya, yconv=yconv, yb=yb, o_hg=o_hg, s_hg=s_hg,
                          yc=yc, o_at=o_at, lse=lse, za=za, zb=zb, zc=zc))
        h = h_new

    dh, loss8 = _loss_head(h, tgt_pad, seq)

    grads = [None] * DEPTH
    for l in reversed(range(DEPTH)):
        p, s = P[l], saved[l]
        dug, mixed, dza, dzb, dzc, dya, dyb, dyc = _mix_bwd(dh, s["za"], s["zb"], s["zc"], s["ug"],
                                                             p["w_ao"], p["w_bo"], p["w_co"], p["w_out"])
        tnmm = functools.partial(_matmul, ta=True, out_dtype=F32, tk=TR)
        g = {}
        g["w_out"] = tnmm(mixed, dh, tm=D_MODEL, tn=D_MODEL, name="dw_out")
        g["w_ao"] = tnmm(s["ya"], dza, tm=512, tn=D_MODEL, name="dw_ao")
        g["w_bo"] = tnmm(s["yb"], dzb, tm=512, tn=D_MODEL, name="dw_bo")
        g["w_co"] = tnmm(s["yc"], dzc, tm=512, tn=D_MODEL, name="dw_co")
        dua, g["conv_w"], g["conv_vec"] = _conv_bwd(s["ua"], s["yconv"], dya, p["conv_w"], p["conv_vec"])
        dub, g["hg_small"] = _hg_bwd(s["ub"], p["lb"], p["gn4"], s["o_hg"], s["s_hg"], dyb)
        duc, g["at_small"] = _swa_bwd(s["uc"], p["qg"], p["kg"], p["sinks"], s["o_at"], s["lse"], dyc)
        g["w_g"] = tnmm(s["hn"], dug, tm=D_MODEL, tn=N_G // 2, name="dw_in_g")
        g["w_a"] = tnmm(s["hn"], dua, tm=D_MODEL, tn=N_A, name="dw_in_a")
        g["w_b"] = tnmm(s["hn"], dub, tm=D_MODEL, tn=N_B, name="dw_in_b")
        g["w_c"] = tnmm(s["hn"], duc, tm=D_MODEL, tn=N_C, name="dw_in_c")
        dh, g["norm_g"] = _inproj_bwd([dug, dua, dub, duc], [p["w_g"], p["w_a"], p["w_b"], p["w_c"]],
                                      s["h"], dh, p["norm_g"])
        grads[l] = g
    return loss8, dh, grads


def _split_w_in(w):
    return dict(w_a=w[:, 0:1536], w_b=w[:, 1536:3584],
                w_c=jnp.concatenate([w[:, 3584:4096], w[:, 4352:4864], w[:, 4096:4352]], axis=1),
                w_g=w[:, 4864:7936])


def _join_w_in(g):
    c = g["w_c"]
    return jnp.concatenate([g["w_a"], g["w_b"], c[:, 0:512], c[:, 1024:1280], c[:, 512:1024], g["w_g"]], axis=1)


_SMALL = (("norm_g", 8), ("meta", 32), ("conv_w", 32 * DEPTH), ("conv_b", 8), ("conv_ln_g", 8), ("conv_ln_b", 8),
          ("lb", 8), ("hg_norm_g", 8), ("q_norm_g", 8), ("k_norm_g", 8), ("sinks", 8))


def _small_offsets():
    off, o = {}, 0
    for name, rows in _SMALL:
        off[name] = (o, rows)
        o += rows
    return off, o


def _pack_small(d):
    parts = []
    for name, rows in _SMALL:
        a = d[name]
        parts.append(jnp.pad(a, ((0, rows - a.shape[0]), (0, 512 - a.shape[1]))))
    return jnp.concatenate(parts, axis=0)


def kernel(x, meta_tokens, norm_g, w_in, conv_w, conv_b, conv_ln_g, conv_ln_b, w_conv_out, hg_lower_bounds, hg_norm_g, w_hg_out, q_norm_g, k_norm_g, attn_sinks, w_att_out, w_out, loss_target, m_meta_tokens, m_norm_g, m_w_in, m_conv_w, m_conv_b, m_conv_ln_g, m_conv_ln_b, m_w_conv_out, m_hg_lower_bounds, m_hg_norm_g, m_w_hg_out, m_q_norm_g, m_k_norm_g, m_attn_sinks, m_w_att_out, m_w_out, v_meta_tokens, v_norm_g, v_w_in, v_conv_w, v_conv_b, v_conv_ln_g, v_conv_ln_b, v_w_conv_out, v_hg_lower_bounds, v_hg_norm_g, v_w_hg_out, v_q_norm_g, v_k_norm_g, v_attn_sinks, v_w_att_out, v_w_out):
    xi, yi = lax.axis_index("x"), lax.axis_index("y")
    chip = 2 * xi + yi
    NS = w_in.shape[2]
    CS = conv_w.shape[2]
    MS = meta_tokens.shape[1]

    g_win, g_wao, g_wbo, g_wco, g_wout, g_meta, g_convw = _gather_chips([
        w_in.astype(BF).reshape(DEPTH * D_MODEL, NS), w_conv_out.astype(BF).reshape(DEPTH * 512, MS),
        w_hg_out.astype(BF).reshape(DEPTH * 512, MS), w_att_out.astype(BF).reshape(DEPTH * 512, MS),
        w_out.astype(BF).reshape(DEPTH * MS, D_MODEL), meta_tokens, conv_w.reshape(DEPTH * CONV_WIDTH, CS)])
    full_cols = lambda g, rows: g.reshape(4, DEPTH, rows, -1).transpose(1, 2, 0, 3).reshape(DEPTH, rows, -1)
    win_f = full_cols(g_win, D_MODEL)
    wao_f, wbo_f, wco_f = full_cols(g_wao, 512), full_cols(g_wbo, 512), full_cols(g_wco, 512)
    wout_f = g_wout.reshape(4, DEPTH, MS, D_MODEL).transpose(1, 0, 2, 3).reshape(DEPTH, D_MODEL, D_MODEL)
    meta_f = g_meta.transpose(1, 0, 2).reshape(N_META, D_MODEL)
    convw_f = full_cols(g_convw, CONV_WIDTH)

    lb_all = _lb_fwd(hg_lower_bounds)
    P = []
    for l in range(DEPTH):
        p = _split_w_in(win_f[l])
        p.update(w_ao=wao_f[l], w_bo=wbo_f[l], w_co=wco_f[l], w_out=wout_f[l], norm_g=norm_g[l:l + 1],
                 conv_w=convw_f[l], conv_vec=_pad8(jnp.stack([conv_b[l], conv_ln_g[l], conv_ln_b[l]])),
                 lb=lb_all[l:l + 1], gn4=jnp.tile(hg_norm_g[l:l + 1], (1, HG_HEADS)),
                 qg=q_norm_g[l:l + 1], kg=k_norm_g[l:l + 1], sinks=attn_sinks[l:l + 1])
        P.append(p)

    loss8, dh0, grads = _local_step(x[0], loss_target[0], meta_f, P)
    seq = x.shape[1]
    grad_x = dh0[CHUNK:CHUNK + seq][None]
    loss = lax.psum(loss8[0, 0], ("x", "y", "c"))

    shard_cols = lambda a: a.reshape(a.shape[0], 4, -1).transpose(1, 0, 2)
    stack = lambda f: jnp.concatenate([f(grads[l]) for l in range(DEPTH)], axis=1)
    big = [stack(lambda g: shard_cols(_join_w_in(g))), stack(lambda g: shard_cols(g["w_ao"])),
           stack(lambda g: shard_cols(g["w_bo"])), stack(lambda g: shard_cols(g["w_co"])),
           stack(lambda g: g["w_out"].reshape(4, MS, D_MODEL))]
    parts = _scatter_chips(big)
    mine = [_sum4(pp, name="sum_chips") for pp in parts]
    theirs = _swap_cores(mine)

    dlb_all = jnp.concatenate([grads[l]["hg_small"][0:1] for l in range(DEPTH)], axis=0)
    small = dict(
        norm_g=jnp.concatenate([grads[l]["norm_g"][0:1] for l in range(DEPTH)], axis=0).reshape(8, 512),
        meta=dh0[META_PAD:CHUNK].reshape(32, 512),
        conv_w=jnp.concatenate([grads[l]["conv_w"] for l in range(DEPTH)], axis=0),
        conv_b=jnp.concatenate([grads[l]["conv_vec"][0:1] for l in range(DEPTH)], axis=0),
        conv_ln_g=jnp.concatenate([grads[l]["conv_vec"][1:2] for l in range(DEPTH)], axis=0),
        conv_ln_b=jnp.concatenate([grads[l]["conv_vec"][2:3] for l in range(DEPTH)], axis=0),
        lb=_lb_bwd(hg_lower_bounds, dlb_all),
        hg_norm_g=jnp.concatenate([grads[l]["hg_small"][1:2].reshape(HG_HEADS, HG_D).sum(0, keepdims=True)
                                   for l in range(DEPTH)], axis=0),
        q_norm_g=jnp.concatenate([grads[l]["at_small"][0:1, 0:ATT_HD] for l in range(DEPTH)], axis=0),
        k_norm_g=jnp.concatenate([grads[l]["at_small"][1:2, 0:ATT_HD] for l in range(DEPTH)], axis=0),
        sinks=jnp.concatenate([grads[l]["at_small"][2:3, 0:ATT_Q_HEADS] for l in range(DEPTH)], axis=0),
    )
    gsum = _allsum_small(_pack_small(small))
    off, _ = _small_offsets()

    def take(name, rows, cols):
        o, _ = off[name]
        return gsum[o:o + rows, 0:cols]

    g_meta_full = take("meta", 32, 512).reshape(N_META, D_MODEL)
    g_convw_full = take("conv_w", 32 * DEPTH, 512).reshape(DEPTH, 32, 512)[:, :CONV_WIDTH]
    small_grads = dict(
        norm_g=take("norm_g", 8, 512),
        meta=lax.dynamic_slice_in_dim(g_meta_full, chip * MS, MS, axis=1),
        conv_w=lax.dynamic_slice_in_dim(g_convw_full, chip * CS, CS, axis=2).reshape(DEPTH * CONV_WIDTH, CS),
        conv_b=take("conv_b", DEPTH, 512), conv_ln_g=take("conv_ln_g", DEPTH, 512), conv_ln_b=take("conv_ln_b", DEPTH, 512),
        lb=take("lb", DEPTH, 512), hg_norm_g=take("hg_norm_g", DEPTH, HG_D), q_norm_g=take("q_norm_g", DEPTH, ATT_HD),
        k_norm_g=take("k_norm_g", DEPTH, ATT_HD), sinks=take("sinks", DEPTH, ATT_Q_HEADS))

    def big_update(w, m, v, a, b, name):
        shp = w.shape
        r2 = lambda t: t.reshape(-1, shp[-1])
        outs = _adamw(r2(w), r2(m), r2(v), a, b, name)
        return [o.reshape(shp) for o in outs]

    res = {}
    res["w_in"] = big_update(w_in, m_w_in, v_w_in, mine[0], theirs[0], "adamw_w_in")
    res["w_conv_out"] = big_update(w_conv_out, m_w_conv_out, v_w_conv_out, mine[1], theirs[1], "adamw_w_ao")
    res["w_hg_out"] = big_update(w_hg_out, m_w_hg_out, v_w_hg_out, mine[2], theirs[2], "adamw_w_bo")
    res["w_att_out"] = big_update(w_att_out, m_w_att_out, v_w_att_out, mine[3], theirs[3], "adamw_w_co")
    res["w_out"] = big_update(w_out, m_w_out, v_w_out, mine[4], theirs[4], "adamw_w_out")

    small_w = dict(meta=(meta_tokens, m_meta_tokens, v_meta_tokens), norm_g=(norm_g, m_norm_g, v_norm_g),
                   conv_w=(conv_w, m_conv_w, v_conv_w), conv_b=(conv_b, m_conv_b, v_conv_b),
                   conv_ln_g=(conv_ln_g, m_conv_ln_g, v_conv_ln_g), conv_ln_b=(conv_ln_b, m_conv_ln_b, v_conv_ln_b),
                   lb=(hg_lower_bounds, m_hg_lower_bounds, v_hg_lower_bounds),
                   hg_norm_g=(hg_norm_g, m_hg_norm_g, v_hg_norm_g), q_norm_g=(q_norm_g, m_q_norm_g, v_q_norm_g),
                   k_norm_g=(k_norm_g, m_k_norm_g, v_k_norm_g), sinks=(attn_sinks, m_attn_sinks, v_attn_sinks))
    view = lambda n, t: t.reshape(-1, 512) if n == "norm_g" else t.reshape(-1, t.shape[-1])
    pw, pm, pv = (_pack_rows([view(n, small_w[n][k]) for n in small_w]) for k in range(3))
    pg = _pack_rows([small_grads[n] for n in small_w])
    packed = _adamw(pw, pm, pv, pg, None, "adamw_small")
    o = 0
    for n in small_w:
        r, cdim = view(n, small_w[n][0]).shape
        res[n] = [t[o:o + r, 0:cdim].reshape(small_w[n][0].shape) for t in packed]
        o += -(-r // 8) * 8

    order = [("meta", None), ("norm_g", None), ("w_in", None), ("conv_w", None), ("conv_b", None), ("conv_ln_g", None),
             ("conv_ln_b", None), ("w_conv_out", None), ("lb", None), ("hg_norm_g", None), ("w_hg_out", None),
             ("q_norm_g", None), ("k_norm_g", None), ("sinks", None), ("w_att_out", None), ("w_out", None)]
    outs = [loss, grad_x]
    for k in range(4):
        outs += [res[n][k] for n, _ in order]
    return tuple(outs)


def _pack_rows(arrs):
    parts = []
    for a in arrs:
        r = (-a.shape[0]) % 8
        parts.append(jnp.pad(a, ((0, r), (0, 512 - a.shape[1]))))
    return jnp.concatenate(parts, axis=0)
```

```python
import functools

import jax
import jax.numpy as jnp
from jax import lax
from jax.experimental import pallas as pl
from jax.experimental.pallas import tpu as pltpu

F32 = jnp.float32
BF = jnp.bfloat16

D_MODEL = 1024
DEPTH = 4
CHUNK = 64
N_META = 16
META_PAD = CHUNK - N_META
D_CONV = 512
CONV_WIDTH = 31
HG_HEADS = 4
HG_D = 128
ATT_Q_HEADS = 8
ATT_KV_HEADS = 2
ATT_HD = 64
ATT_GROUP = ATT_Q_HEADS // ATT_KV_HEADS
EPS = 1e-6
F_FLOOR = 1e-30
NEG = -1e30

ADAM_LR = 0.001
ADAM_B1 = 0.9
ADAM_B2 = 0.999
ADAM_EPS = 1e-08
ADAM_WD = 0.01
ADAM_STEP = 10

TR = 512
TRM = 256
CONV_RB = 32
QB = 128
HALO = 128
TAIL_PAD = TR - CHUNK
VMEM_LIMIT = 56 * 1024 * 1024

N_G, N_A, N_B, N_C = 3 * D_MODEL, 3 * D_CONV, 4 * 512, 2 * 512 + 2 * 128

MESH = pl.DeviceIdType.MESH


def _cp(sem=None, vmem=VMEM_LIMIT, **kw):
    if sem is None:
        return pltpu.CompilerParams(vmem_limit_bytes=vmem, **kw)
    return pltpu.CompilerParams(dimension_semantics=sem, vmem_limit_bytes=vmem, **kw)


def _nn(a, b):
    return lax.dot_general(a, b, (((1,), (0,)), ((), ())), preferred_element_type=F32)


def _nt(a, b):
    return lax.dot_general(a, b, (((1,), (1,)), ((), ())), preferred_element_type=F32)


def _tn(a, b):
    return lax.dot_general(a, b, (((0,), (0,)), ((), ())), preferred_element_type=F32)


def _sig(x):
    return jax.nn.sigmoid(x)


def _silu(x):
    return x * _sig(x)


def _dsilu(x):
    s = _sig(x)
    return s * (1.0 + x * (1.0 - s))


def _split3(x):
    hi = x.astype(BF)
    r1 = x - hi.astype(F32)
    mid = r1.astype(BF)
    lo = (r1 - mid.astype(F32)).astype(BF)
    return hi, mid, lo


def _mm3(t, x):
    hi, mid, lo = _split3(x)
    return _nn(t, hi) + _nn(t, mid) + _nn(t, lo)


def _chunk_tri(n, upper):
    r = lax.broadcasted_iota(jnp.int32, (n, n), 0)
    c = lax.broadcasted_iota(jnp.int32, (n, n), 1)
    same = jnp.right_shift(r, 6) == jnp.right_shift(c, 6)
    tri = (c >= r) if upper else (c <= r)
    return jnp.where(same & tri, 1.0, 0.0).astype(BF)


def _matmul(a, b, *, ta=False, tb=False, out_dtype, tm, tn, tk, name, col_major_grid=False):
    if ta:
        K, M = a.shape
    else:
        M, K = a.shape
    N = b.shape[0] if tb else b.shape[1]
    assert M % tm == 0 and N % tn == 0 and K % tk == 0, (name, M, N, K, tm, tn, tk)
    nk = K // tk
    if col_major_grid:
        grid = (N // tn, M // tm, nk)
        ij = lambda g0, g1: (g1, g0)
    else:
        grid = (M // tm, N // tn, nk)
        ij = lambda g0, g1: (g0, g1)
    if ta:
        a_spec = pl.BlockSpec((tk, tm), lambda g0, g1, k: (k, ij(g0, g1)[0]))
    else:
        a_spec = pl.BlockSpec((tm, tk), lambda g0, g1, k: (ij(g0, g1)[0], k))
    if tb:
        b_spec = pl.BlockSpec((tn, tk), lambda g0, g1, k: (ij(g0, g1)[1], k))
    else:
        b_spec = pl.BlockSpec((tk, tn), lambda g0, g1, k: (k, ij(g0, g1)[1]))
    o_spec = pl.BlockSpec((tm, tn), lambda g0, g1, k: ij(g0, g1))
    dims = (((0 if ta else 1,), (1 if tb else 0,)), ((), ()))
    use_acc = nk > 1 and out_dtype != F32

    def body(a_ref, b_ref, o_ref, *scr):
        k = pl.program_id(2)
        p = lax.dot_general(a_ref[...].astype(BF), b_ref[...].astype(BF), dims, preferred_element_type=F32)
        if nk == 1:
            o_ref[...] = p.astype(out_dtype)
        else:
            acc = scr[0] if use_acc else o_ref

            @pl.when(k == 0)
            def _():
                acc[...] = p

            @pl.when(k > 0)
            def _():
                acc[...] += p

            if use_acc:
                @pl.when(k == nk - 1)
                def _():
                    o_ref[...] = acc[...].astype(out_dtype)

    return pl.pallas_call(
        body, name=name, grid=grid, in_specs=[a_spec, b_spec], out_specs=o_spec,
        out_shape=jax.ShapeDtypeStruct((M, N), out_dtype),
        scratch_shapes=[pltpu.VMEM((tm, tn), F32)] if use_acc else [],
        compiler_params=_cp(("parallel", "parallel", "arbitrary")),
    )(a, b)


def _rms_fwd(h, g):
    Lp = h.shape[0]

    def body(h_ref, g_ref, o_ref):
        x = h_ref[...]
        r = lax.rsqrt(jnp.mean(x * x, axis=-1, keepdims=True) + EPS)
        o_ref[...] = (x * r * g_ref[...]).astype(BF)

    return pl.pallas_call(
        body, name="rms_fwd", grid=(Lp // TR,),
        in_specs=[pl.BlockSpec((TR, D_MODEL), lambda i: (i, 0)), pl.BlockSpec((1, D_MODEL), lambda i: (0, 0))],
        out_specs=pl.BlockSpec((TR, D_MODEL), lambda i: (i, 0)),
        out_shape=jax.ShapeDtypeStruct((Lp, D_MODEL), BF),
        compiler_params=_cp(("parallel",)),
    )(h, g)


def _glu(ua, row):
    a = ua[:, 0:D_CONV].astype(F32)
    gl = ua[:, D_CONV:2 * D_CONV].astype(F32)
    return jnp.where(row >= META_PAD, a * _sig(gl), 0.0)


_SH_ROWS = TR + CHUNK - 8


def _fill_shifts(src, sh):
    for b in range(1, 8):
        sh[b - 1] = src[pl.ds(b, _SH_ROWS), :]


def _shifted(src, sh, start, n):
    b = start % 8
    if b == 0:
        return src[pl.ds(start, n), :]
    return sh[b - 1, pl.ds(start - b, n), :]


def _conv_fwd(ua, cw, cvec):
    Lp = ua.shape[0]
    nt = Lp // TR
    hb = TR // CHUNK

    def body(cur_ref, halo_ref, w_ref, v_ref, ya_ref, yc_ref, ext, sh):
        i = pl.program_id(0)
        row = i * TR + lax.broadcasted_iota(jnp.int32, (TR, 1), 0)
        hrow = i * TR - CHUNK + lax.broadcasted_iota(jnp.int32, (CHUNK, 1), 0)
        ext[pl.ds(0, CHUNK), :] = jnp.where(i > 0, _glu(halo_ref[...], hrow), 0.0)
        ext[pl.ds(CHUNK, TR), :] = _glu(cur_ref[...], row)
        _fill_shifts(ext, sh)
        for rb in range(TR // CONV_RB):
            r0 = rb * CONV_RB
            rows = pl.ds(r0, CONV_RB)
            acc = jnp.zeros((CONV_RB, D_CONV), F32)
            for j in range(CONV_WIDTH):
                acc = acc + _shifted(ext, sh, r0 + CHUNK - (CONV_WIDTH - 1) + j, CONV_RB) * w_ref[j:j + 1, :]
            y = acc + v_ref[0:1, :]
            yc_ref[rows, :] = y
            mu = jnp.mean(y, axis=-1, keepdims=True)
            d = y - mu
            var = jnp.mean(d * d, axis=-1, keepdims=True)
            yn = d * lax.rsqrt(var + EPS) * v_ref[1:2, :] + v_ref[2:3, :]
            ya_ref[rows, :] = (_silu(yn) * _silu(cur_ref[rows, 2 * D_CONV:3 * D_CONV].astype(F32))).astype(BF)

    return pl.pallas_call(
        body, name="conv_fwd", grid=(nt,),
        in_specs=[pl.BlockSpec((TR, N_A), lambda i: (i, 0)),
                  pl.BlockSpec((CHUNK, N_A), lambda i: (jnp.maximum(i * hb - 1, 0), 0)),
                  pl.BlockSpec((CONV_WIDTH, D_CONV), lambda i: (0, 0)),
                  pl.BlockSpec((8, D_CONV), lambda i: (0, 0))],
        out_specs=[pl.BlockSpec((TR, D_CONV), lambda i: (i, 0)), pl.BlockSpec((TR, D_CONV), lambda i: (i, 0))],
        out_shape=[jax.ShapeDtypeStruct((Lp, D_CONV), BF), jax.ShapeDtypeStruct((Lp, D_CONV), F32)],
        scratch_shapes=[pltpu.VMEM((TR + CHUNK, D_CONV), F32), pltpu.VMEM((7, _SH_ROWS, D_CONV), F32)],
        compiler_params=_cp(("parallel",)),
    )(ua, ua, cw, cvec)


def _conv_bwd(ua, yconv, dya, cw, cvec):
    Lp = ua.shape[0]
    nt = Lp // TR
    hb = TR // CHUNK
    nhb = Lp // CHUNK

    def ln_bwd(y, dout, gate, v_ref):
        mu = jnp.mean(y, axis=-1, keepdims=True)
        d = y - mu
        var = jnp.mean(d * d, axis=-1, keepdims=True)
        rstd = lax.rsqrt(var + EPS)
        xhat = d * rstd
        yn = xhat * v_ref[1:2, :] + v_ref[2:3, :]
        dyn = dout * _silu(gate) * _dsilu(yn)
        dxh = dyn * v_ref[1:2, :]
        dyc = rstd * (dxh - jnp.mean(dxh, axis=-1, keepdims=True) - xhat * jnp.mean(dxh * xhat, axis=-1, keepdims=True))
        return dyc, dyn, xhat, yn

    def body(cur_ref, prev_ref, next_ref, yc_ref, ycn_ref, dy_ref, dyn_ref, w_ref, v_ref,
             du_ref, dw_ref, dv_ref, uext, dext, dwacc, ush, dsh):
        i = pl.program_id(0)

        @pl.when(i == 0)
        def _():
            dwacc[...] = jnp.zeros_like(dwacc)
            dv_ref[...] = jnp.zeros_like(dv_ref)

        row = i * TR + lax.broadcasted_iota(jnp.int32, (TR, 1), 0)
        hrow = i * TR - CHUNK + lax.broadcasted_iota(jnp.int32, (CHUNK, 1), 0)
        uext[pl.ds(0, CHUNK), :] = jnp.where(i > 0, _glu(prev_ref[...], hrow), 0.0)
        uext[pl.ds(CHUNK, TR), :] = _glu(cur_ref[...], row)

        s_b = jnp.zeros((1, D_CONV), F32)
        s_g = jnp.zeros((1, D_CONV), F32)
        s_bb = jnp.zeros((1, D_CONV), F32)
        for rb in range(TR // CONV_RB):
            rows = pl.ds(rb * CONV_RB, CONV_RB)
            gate = cur_ref[rows, 2 * D_CONV:3 * D_CONV].astype(F32)
            dout = dy_ref[rows, :].astype(F32)
            dyc, dyn, xhat, yn = ln_bwd(yc_ref[rows, :], dout, gate, v_ref)
            du_ref[rows, 2 * D_CONV:3 * D_CONV] = (dout * _silu(yn) * _dsilu(gate)).astype(BF)
            dext[rows, :] = dyc
            s_b = s_b + jnp.sum(dyc, axis=0, keepdims=True)
            s_g = s_g + jnp.sum(dyn * xhat, axis=0, keepdims=True)
            s_bb = s_bb + jnp.sum(dyn, axis=0, keepdims=True)
        dv_ref[0:1, :] += s_b
        dv_ref[1:2, :] += s_g
        dv_ref[2:3, :] += s_bb
        dyc_n, _, _, _ = ln_bwd(ycn_ref[...], dyn_ref[...].astype(F32),
                                next_ref[:, 2 * D_CONV:3 * D_CONV].astype(F32), v_ref)
        dext[pl.ds(TR, CHUNK), :] = jnp.where(i < nt - 1, dyc_n, 0.0)
        _fill_shifts(uext, ush)
        _fill_shifts(dext, dsh)

        for rb in range(TR // CONV_RB):
            r0 = rb * CONV_RB
            rows = pl.ds(r0, CONV_RB)
            d_blk = dext[rows, :]
            dglu = jnp.zeros((CONV_RB, D_CONV), F32)
            for j in range(CONV_WIDTH):
                dglu = dglu + _shifted(dext, dsh, r0 + CONV_WIDTH - 1 - j, CONV_RB) * w_ref[j:j + 1, :]
                prod = d_blk * _shifted(uext, ush, r0 + CHUNK - (CONV_WIDTH - 1) + j, CONV_RB)
                part = prod[0:8, :]
                for s in range(1, CONV_RB // 8):
                    part = part + prod[8 * s:8 * s + 8, :]
                dwacc[j] += part
            a = cur_ref[rows, 0:D_CONV].astype(F32)
            sg = _sig(cur_ref[rows, D_CONV:2 * D_CONV].astype(F32))
            grow = i * TR + r0 + lax.broadcasted_iota(jnp.int32, (CONV_RB, 1), 0)
            dglu = jnp.where(grow >= META_PAD, dglu, 0.0)
            du_ref[rows, 0:D_CONV] = (dglu * sg).astype(BF)
            du_ref[rows, D_CONV:2 * D_CONV] = (dglu * a * sg * (1.0 - sg)).astype(BF)

        @pl.when(i == nt - 1)
        def _():
            dw_ref[...] = jnp.sum(dwacc[...], axis=1)

    nxt = lambda i: (jnp.minimum(i * hb + hb, nhb - 1), 0)
    return pl.pallas_call(
        body, name="conv_bwd", grid=(nt,),
        in_specs=[pl.BlockSpec((TR, N_A), lambda i: (i, 0)),
                  pl.BlockSpec((CHUNK, N_A), lambda i: (jnp.maximum(i * hb - 1, 0), 0)),
                  pl.BlockSpec((CHUNK, N_A), nxt),
                  pl.BlockSpec((TR, D_CONV), lambda i: (i, 0)),
                  pl.BlockSpec((CHUNK, D_CONV), nxt),
                  pl.BlockSpec((TR, D_CONV), lambda i: (i, 0)),
                  pl.BlockSpec((CHUNK, D_CONV), nxt),
                  pl.BlockSpec((CONV_WIDTH, D_CONV), lambda i: (0, 0)),
                  pl.BlockSpec((8, D_CONV), lambda i: (0, 0))],
        out_specs=[pl.BlockSpec((TR, N_A), lambda i: (i, 0)),
                   pl.BlockSpec((32, D_CONV), lambda i: (0, 0)),
                   pl.BlockSpec((8, D_CONV), lambda i: (0, 0))],
        out_shape=[jax.ShapeDtypeStruct((Lp, N_A), BF), jax.ShapeDtypeStruct((32, D_CONV), F32),
                   jax.ShapeDtypeStruct((8, D_CONV), F32)],
        scratch_shapes=[pltpu.VMEM((TR + CHUNK, D_CONV), F32), pltpu.VMEM((TR + CHUNK, D_CONV), F32),
                        pltpu.VMEM((32, 8, D_CONV), F32), pltpu.VMEM((7, _SH_ROWS, D_CONV), F32),
                        pltpu.VMEM((7, _SH_ROWS, D_CONV), F32)],
        compiler_params=_cp(("arbitrary",)),
    )(ua, ua, ua, yconv, yconv, dya, dya, cw, cvec)


def _hg_gates(ub_ref, lbv, row):
    q = ub_ref[:, 0:512].astype(F32)
    z = ub_ref[:, 512:1024].astype(F32)
    valid = row >= META_PAD
    sig = _sig(z)
    f = lbv + (1.0 - lbv) * sig
    g = jnp.where(valid, jnp.log(jnp.maximum(f, F_FLOOR)), 0.0)
    k = jnp.where(valid, (1.0 - lbv) * _sig(-z), 0.0)
    return q, k, g, sig, f


def _hg_chunk_terms(b_c, q_c, k_c):
    bm = b_c[CHUNK // 2 - 1:CHUNK // 2, :]
    bl = b_c[CHUNK - 1:CHUNK, :]
    e1 = jnp.exp(b_c - bm)
    e2 = jnp.exp(bm - b_c)
    e0 = jnp.exp(b_c)
    e3 = jnp.exp(bl - b_c)
    el = jnp.exp(bl)
    return e1, e2, e0, e3, el, q_c * e1, k_c * e2, q_c * e0, k_c * e3


def _hg_fwd(ub, lb, gn4):
    Lp = ub.shape[0]
    nt = Lp // TR
    cpt = TR // CHUNK

    def body(ub_ref, lb_ref, gn_ref, yb_ref, o_ref, ss_ref, st, bsc, qsc, ksc):
        i = pl.program_id(0)

        @pl.when(i == 0)
        def _():
            st[...] = jnp.zeros_like(st)

        row = i * TR + lax.broadcasted_iota(jnp.int32, (TR, 1), 0)
        q, k, g, _, _ = _hg_gates(ub_ref, lb_ref[...], row)
        qsc[...] = _silu(q)
        ksc[...] = k
        bsc[...] = _mm3(_chunk_tri(TR, False), g)
        tri = lax.broadcasted_iota(jnp.int32, (CHUNK, CHUNK), 1) <= lax.broadcasted_iota(jnp.int32, (CHUNK, CHUNK), 0)

        def chunk(c, carry):
            r0 = pl.multiple_of(c * CHUNK, CHUNK)
            rows = pl.ds(r0, CHUNK)
            _, _, _, _, el, qe, ke, qE, kd = _hg_chunk_terms(bsc[rows, :], qsc[rows, :], ksc[rows, :])
            qe, ke, qE, kd = qe.astype(BF), ke.astype(BF), qE.astype(BF), kd.astype(BF)
            for h in range(HG_HEADS):
                sl = slice(HG_D * h, HG_D * (h + 1))
                v_h = ub_ref[rows, 1024 + HG_D * h:1024 + HG_D * (h + 1)]
                s_in = st[h]
                ss_ref[c, h] = s_in
                a = jnp.where(tri, _nt(qe[:, sl], ke[:, sl]), 0.0)
                o_ref[rows, sl] = _nn(a.astype(BF), v_h) + _nt(qE[:, sl], s_in.astype(BF))
                st[h] = el[:, sl] * s_in + _tn(v_h, kd[:, sl])
            return carry

        lax.fori_loop(0, cpt, chunk, 0)

        gate = ub_ref[:, 1536:2048].astype(F32)
        for h in range(HG_HEADS):
            sl = slice(HG_D * h, HG_D * (h + 1))
            o = o_ref[:, sl]
            r = lax.rsqrt(jnp.mean(o * o, axis=-1, keepdims=True) + EPS)
            yb_ref[:, sl] = (o * r * gn_ref[:, sl] * _silu(gate[:, sl])).astype(BF)

    return pl.pallas_call(
        body, name="hgrn_fwd", grid=(nt,),
        in_specs=[pl.BlockSpec((TR, N_B), lambda i: (i, 0)), pl.BlockSpec((1, 512), lambda i: (0, 0)),
                  pl.BlockSpec((1, 512), lambda i: (0, 0))],
        out_specs=[pl.BlockSpec((TR, 512), lambda i: (i, 0)), pl.BlockSpec((TR, 512), lambda i: (i, 0)),
                   pl.BlockSpec((cpt, HG_HEADS, HG_D, HG_D), lambda i: (i, 0, 0, 0))],
        out_shape=[jax.ShapeDtypeStruct((Lp, 512), BF), jax.ShapeDtypeStruct((Lp, 512), F32),
                   jax.ShapeDtypeStruct((Lp // CHUNK, HG_HEADS, HG_D, HG_D), F32)],
        scratch_shapes=[pltpu.VMEM((HG_HEADS, HG_D, HG_D), F32), pltpu.VMEM((TR, 512), F32),
                        pltpu.VMEM((TR, 512), F32), pltpu.VMEM((TR, 512), F32)],
        compiler_params=_cp(("arbitrary",)),
    )(ub, lb, gn4)


def _hg_bwd(ub, lb, gn4, o_save, s_save, dyb):
    Lp = ub.shape[0]
    nt = Lp // TR
    cpt = TR // CHUNK

    def body(ub_ref, lb_ref, gn_ref, o_ref, ss_ref, dy_ref, du_ref, ds_ref,
             dst, bsc, qsc, ksc, dosc, dqsc, dksc, dbsc):
        i = pl.program_id(0)
        t = nt - 1 - i

        @pl.when(i == 0)
        def _():
            dst[...] = jnp.zeros_like(dst)
            ds_ref[...] = jnp.zeros_like(ds_ref)

        lbv = lb_ref[...]
        row = t * TR + lax.broadcasted_iota(jnp.int32, (TR, 1), 0)
        valid = row >= META_PAD
        q, k, g, sig, f = _hg_gates(ub_ref, lbv, row)
        qsc[...] = _silu(q)
        ksc[...] = k
        bsc[...] = _mm3(_chunk_tri(TR, False), g)

        gate = ub_ref[:, 1536:2048].astype(F32)
        dy = dy_ref[...].astype(F32)
        dgn = jnp.zeros((1, 512), F32)
        for h in range(HG_HEADS):
            sl = slice(HG_D * h, HG_D * (h + 1))
            o = o_ref[:, sl]
            r = lax.rsqrt(jnp.mean(o * o, axis=-1, keepdims=True) + EPS)
            ohat = o * r
            don = dy[:, sl] * _silu(gate[:, sl])
            du_ref[:, 1536 + HG_D * h:1536 + HG_D * (h + 1)] = (
                dy[:, sl] * ohat * gn_ref[:, sl] * _dsilu(gate[:, sl])).astype(BF)
            ds_ref[1:2, sl] += jnp.sum(don * ohat, axis=0, keepdims=True)
            gd = don * gn_ref[:, sl]
            dosc[:, sl] = r * (gd - ohat * jnp.mean(gd * ohat, axis=-1, keepdims=True))

        tri = lax.broadcasted_iota(jnp.int32, (CHUNK, CHUNK), 1) <= lax.broadcasted_iota(jnp.int32, (CHUNK, CHUNK), 0)
        last = lax.broadcasted_iota(jnp.int32, (CHUNK, 1), 0) == CHUNK - 1

        def chunk(cc, carry):
            c = cpt - 1 - cc
            r0 = pl.multiple_of(c * CHUNK, CHUNK)
            rows = pl.ds(r0, CHUNK)
            e1, e2, e0, e3, el, qe, ke, qE, kd = _hg_chunk_terms(bsc[rows, :], qsc[rows, :], ksc[rows, :])
            qe_b, ke_b, qE_b, kd_b = qe.astype(BF), ke.astype(BF), qE.astype(BF), kd.astype(BF)
            do_c = dosc[rows, :].astype(BF)
            for h in range(HG_HEADS):
                sl = slice(HG_D * h, HG_D * (h + 1))
                v_h = ub_ref[rows, 1024 + HG_D * h:1024 + HG_D * (h + 1)]
                do_h = do_c[:, sl]
                s_in = ss_ref[c, h]
                d_s = dst[h]
                d_s_b = d_s.astype(BF)
                a = jnp.where(tri, _nt(qe_b[:, sl], ke_b[:, sl]), 0.0).astype(BF)
                da = jnp.where(tri, _nt(do_h, v_h), 0.0).astype(BF)
                dv = _tn(a, do_h) + _nt(kd_b[:, sl], d_s_b)
                dqE = _nn(do_h, s_in.astype(BF))
                dqe = _nn(da, ke_b[:, sl])
                dke = _tn(da, qe_b[:, sl])
                dkd = _nn(v_h, d_s_b)
                del_h = jnp.sum(s_in * d_s, axis=0, keepdims=True)
                dst[h] = el[:, sl] * d_s + _tn(do_h, qE_b[:, sl])
                dqsc[rows, sl] = dqE * e0[:, sl] + dqe * e1[:, sl]
                dksc[rows, sl] = dke * e2[:, sl] + dkd * e3[:, sl]
                tkd = dkd * kd[:, sl]
                dbl = jnp.sum(tkd, axis=0, keepdims=True) + del_h * el[:, sl]
                dbsc[rows, sl] = dqE * qE[:, sl] + dqe * qe[:, sl] - dke * ke[:, sl] - tkd + jnp.where(last, dbl, 0.0)
                du_ref[rows, 1024 + HG_D * h:1024 + HG_D * (h + 1)] = dv.astype(BF)
            return carry

        lax.fori_loop(0, cpt, chunk, 0)

        dg = _mm3(_chunk_tri(TR, True), dbsc[...])
        df = jnp.where(valid & (f > F_FLOOR), dg / f, 0.0)
        dk = jnp.where(valid, dksc[...], 0.0)
        nsig = _sig(-ub_ref[:, 512:1024].astype(F32))
        dsig = (df - dk) * (1.0 - lbv)
        ds_ref[0:1, :] += jnp.sum(df * (1.0 - sig) - dk * nsig, axis=0, keepdims=True)
        du_ref[:, 512:1024] = (dsig * sig * (1.0 - sig)).astype(BF)
        du_ref[:, 0:512] = (dqsc[...] * _dsilu(q)).astype(BF)

    rev = lambda i: (nt - 1 - i, 0)
    return pl.pallas_call(
        body, name="hgrn_bwd", grid=(nt,),
        in_specs=[pl.BlockSpec((TR, N_B), rev), pl.BlockSpec((1, 512), lambda i: (0, 0)),
                  pl.BlockSpec((1, 512), lambda i: (0, 0)), pl.BlockSpec((TR, 512), rev),
                  pl.BlockSpec((cpt, HG_HEADS, HG_D, HG_D), lambda i: (nt - 1 - i, 0, 0, 0)),
                  pl.BlockSpec((TR, 512), rev)],
        out_specs=[pl.BlockSpec((TR, N_B), rev), pl.BlockSpec((8, 512), lambda i: (0, 0))],
        out_shape=[jax.ShapeDtypeStruct((Lp, N_B), BF), jax.ShapeDtypeStruct((8, 512), F32)],
        scratch_shapes=[pltpu.VMEM((HG_HEADS, HG_D, HG_D), F32)] + [pltpu.VMEM((TR, 512), F32)] * 7,
        compiler_params=_cp(("arbitrary",)),
    )(ub, lb, gn4, o_save, s_save, dyb)


_KCOL = (2 * 512) // 128
_VCOL = _KCOL + 1


def _swa_in_specs(nt, rev):
    tile = (lambda i: nt - 1 - i) if rev else (lambda i: i)
    hpt = TR // HALO
    return [
        pl.BlockSpec((TR, 512), lambda i: (tile(i), 0)),
        pl.BlockSpec((TR, 512), lambda i: (tile(i), 1)),
        pl.BlockSpec((TR, 128), lambda i: (tile(i), _KCOL)),
        pl.BlockSpec((TR, 128), lambda i: (tile(i), _VCOL)),
        pl.BlockSpec((HALO, 128), lambda i: (jnp.maximum(tile(i) * hpt - 1, 0), _KCOL)),
        pl.BlockSpec((HALO, 128), lambda i: (jnp.maximum(tile(i) * hpt - 1, 0), _VCOL)),
        pl.BlockSpec((CHUNK, 128), lambda i: (0, _KCOL)),
        pl.BlockSpec((CHUNK, 128), lambda i: (0, _VCOL)),
        pl.BlockSpec((1, 512), lambda i: (0, 0)),
        pl.BlockSpec((1, 128), lambda i: (0, 0)),
        pl.BlockSpec((1, ATT_Q_HEADS), lambda i: (0, 0)),
    ]


_WROWS = 2 * CHUNK + HALO + TR
_W0 = 2 * CHUNK
_C0 = _W0 + HALO
_SCALE = ATT_HD ** -0.5


def _group_ones(n):
    r = lax.broadcasted_iota(jnp.int32, (n, n), 0)
    c = lax.broadcasted_iota(jnp.int32, (n, n), 1)
    return jnp.where(jnp.right_shift(r, 6) == jnp.right_shift(c, 6), 1.0, 0.0).astype(BF)


def _group_mean(x, ones):
    hi = x.astype(BF)
    lo = (x - hi.astype(F32)).astype(BF)
    return (_nn(hi, ones) + _nn(lo, ones)) * (1.0 / ATT_HD)


def _head_rms(x, ones):
    r = lax.rsqrt(_group_mean(x * x, ones) + EPS)
    return x * r, r


def _swa_windows(kc_ref, vc_ref, kh_ref, vh_ref, km_ref, vm_ref, kg2, ones, kwin, krwin, vwin, vrwin):
    meta = pl.ds(META_PAD, N_META)
    for (k, v, r0, n) in ((km_ref[meta, :], vm_ref[meta, :], 0, N_META), (kh_ref[...], vh_ref[...], _W0, HALO),
                          (kc_ref[...], vc_ref[...], _C0, TR)):
        xhat, _ = _head_rms(k.astype(F32), ones)
        kn = xhat * kg2
        kwin[pl.ds(r0, n), :] = kn.astype(BF)
        krwin[pl.ds(r0, n), :] = pltpu.roll(kn, ATT_HD, 1).astype(BF)
        vwin[pl.ds(r0, n), :] = v
        if vrwin is not None:
            vrwin[pl.ds(r0, n), :] = pltpu.roll(v.astype(F32), ATT_HD, 1).astype(BF)
    zero = jnp.zeros((_W0 - N_META, 128), BF)
    for w in (kwin, krwin, vwin, vrwin):
        if w is not None:
            w[pl.ds(N_META, _W0 - N_META), :] = zero


def _swa_masks_t(t, qb):
    q0 = t * TR + qb * QB
    qc = jnp.right_shift(q0 + lax.broadcasted_iota(jnp.int32, (1, QB), 1), 6)
    kabs = q0 - HALO + lax.broadcasted_iota(jnp.int32, (QB + HALO, 1), 0)
    kc = jnp.right_shift(kabs + HALO, 6) - HALO // CHUNK
    mask_w = (kc <= qc) & (kc >= qc - 2) & (kabs >= META_PAD)
    return qc > 2, mask_w


def _split_heads(x, lane_hi):
    return jnp.where(lane_hi, 0.0, x).astype(BF), jnp.where(lane_hi, x, 0.0).astype(BF)


def _swa_fwd(uc, qg8, kg2, sinks):
    Lp = uc.shape[0]
    nt = Lp // TR
    nqb = TR // QB

    def body(q_ref, g_ref, kc_ref, vc_ref, kh_ref, vh_ref, km_ref, vm_ref, qg_ref, kg_ref, sk_ref,
             yc_ref, o_ref, lse_ref, kwin, krwin, vwin, vt, qlo, qhi, ot):
        t = pl.program_id(0)
        _swa_windows(kc_ref, vc_ref, kh_ref, vh_ref, km_ref, vm_ref, kg_ref[...], _group_ones(128),
                     kwin, krwin, vwin, None)
        vt[...] = vwin[...].T
        xhat, _ = _head_rms(q_ref[...].astype(F32), _group_ones(512))
        lane_hi = (lax.broadcasted_iota(jnp.int32, (1, 512), 1) & ATT_HD) != 0
        lo, hi = _split_heads(xhat * qg_ref[...] * _SCALE, lane_hi)
        qlo[...] = lo
        qhi[...] = hi
        for qb in range(nqb):
            rows = pl.ds(qb * QB, QB)
            wrows = pl.ds(_W0 + qb * QB, QB + HALO)
            mrows = pl.ds(0, N_META)
            mask_m, mask_w = _swa_masks_t(t, qb)
            for j in range(ATT_Q_HEADS):
                p, e = j // 2, j % 2
                kv = j // ATT_GROUP
                ks = kwin if e == kv else krwin
                qp = (qlo, qhi)[e][rows, 128 * p:128 * (p + 1)]
                s_m = jnp.where(mask_m, _nt(ks[mrows, :], qp), NEG)
                s_w = jnp.where(mask_w, _nt(ks[wrows, :], qp), NEG)
                sink = sk_ref[:, j:j + 1]
                m = jnp.maximum(jnp.maximum(jnp.max(s_m, axis=0, keepdims=True),
                                            jnp.max(s_w, axis=0, keepdims=True)), sink)
                p_m = jnp.exp(s_m - m)
                p_w = jnp.exp(s_w - m)
                den = jnp.sum(p_m, axis=0, keepdims=True) + jnp.sum(p_w, axis=0, keepdims=True) + jnp.exp(sink - m)
                vrows = pl.ds(ATT_HD * kv, ATT_HD)
                ot[pl.ds(ATT_HD * j, ATT_HD), pl.ds(qb * QB, QB)] = (
                    _nn(vt[vrows, pl.ds(0, N_META)], p_m.astype(BF))
                    + _nn(vt[vrows, pl.ds(_W0 + qb * QB, QB + HALO)], p_w.astype(BF))) * (1.0 / den)
                lse_ref[j:j + 1, pl.ds(qb * QB, QB)] = m + jnp.log(den)
        o = ot[...].T
        o_ref[...] = o
        yc_ref[...] = (o * _silu(g_ref[...].astype(F32))).astype(BF)

    win = pltpu.VMEM((_WROWS, 128), BF)
    return pl.pallas_call(
        body, name="swa_fwd", grid=(nt,), in_specs=_swa_in_specs(nt, False),
        out_specs=[pl.BlockSpec((TR, 512), lambda i: (i, 0)), pl.BlockSpec((TR, 512), lambda i: (i, 0)),
                   pl.BlockSpec((ATT_Q_HEADS, TR), lambda i: (0, i))],
        out_shape=[jax.ShapeDtypeStruct((Lp, 512), BF), jax.ShapeDtypeStruct((Lp, 512), F32),
                   jax.ShapeDtypeStruct((ATT_Q_HEADS, Lp), F32)],
        scratch_shapes=[win, win, win, pltpu.VMEM((128, _WROWS), BF), pltpu.VMEM((TR, 512), BF),
                        pltpu.VMEM((TR, 512), BF), pltpu.VMEM((512, TR), F32)],
        compiler_params=_cp(("parallel",)),
    )(uc, uc, uc, uc, uc, uc, uc, uc, qg8, kg2, sinks)


def _swa_bwd(uc, qg8, kg2, sinks, o_save, lse, dyc):
    Lp = uc.shape[0]
    nt = Lp // TR
    nqb = TR // QB

    def body(q_ref, g_ref, kc_ref, vc_ref, kh_ref, vh_ref, km_ref, vm_ref, qg_ref, kg_ref, sk_ref,
             o_ref, lse_ref, dy_ref, du_ref, dg_ref, dsk_ref,
             kwin, krwin, vwin, vrwin, kt, krt, qlo, qhi, dolo, dohi, dqt, dk_dir, dk_rol, dv_dir, dv_rol,
             carry_k, carry_v, meta_k, meta_v):
        i = pl.program_id(0)
        t = nt - 1 - i

        @pl.when(i == 0)
        def _():
            carry_k[...] = jnp.zeros_like(carry_k)
            carry_v[...] = jnp.zeros_like(carry_v)
            meta_k[...] = jnp.zeros_like(meta_k)
            meta_v[...] = jnp.zeros_like(meta_v)
            dg_ref[...] = jnp.zeros_like(dg_ref)
            dsk_ref[...] = jnp.zeros_like(dsk_ref)

        ones128 = _group_ones(128)
        ones512 = _group_ones(512)
        _swa_windows(kc_ref, vc_ref, kh_ref, vh_ref, km_ref, vm_ref, kg_ref[...], ones128, kwin, krwin, vwin, vrwin)
        kt[...] = kwin[...].T
        krt[...] = krwin[...].T
        xhat_q, r_q = _head_rms(q_ref[...].astype(F32), ones512)
        lane_hi = (lax.broadcasted_iota(jnp.int32, (1, 512), 1) & ATT_HD) != 0
        lo, hi = _split_heads(xhat_q * qg_ref[...] * _SCALE, lane_hi)
        qlo[...] = lo
        qhi[...] = hi
        gate = g_ref[...].astype(F32)
        dy = dy_ref[...].astype(F32)
        do = dy * _silu(gate)
        o = o_ref[...]
        du_ref[:, 512:1024] = (dy * o * _dsilu(gate)).astype(BF)
        lo, hi = _split_heads(do, lane_hi)
        dolo[...] = lo
        dohi[...] = hi
        hsel = jnp.where(jnp.right_shift(lax.broadcasted_iota(jnp.int32, (ATT_Q_HEADS, 512), 1), 6)
                         == lax.broadcasted_iota(jnp.int32, (ATT_Q_HEADS, 512), 0), 1.0, 0.0).astype(BF)
        prod = do * o
        p_hi = prod.astype(BF)
        d_t = _nt(hsel, p_hi) + _nt(hsel, (prod - p_hi.astype(F32)).astype(BF))
        for acc in (dk_dir, dk_rol, dv_dir, dv_rol):
            acc[...] = jnp.zeros_like(acc)

        for qb in range(nqb):
            rows = pl.ds(qb * QB, QB)
            qcols = pl.ds(qb * QB, QB)
            wrows = pl.ds(_W0 + qb * QB, QB + HALO)
            mrows = pl.ds(0, N_META)
            mask_m, mask_w = _swa_masks_t(t, qb)
            for j in range(ATT_Q_HEADS):
                p, e = j // 2, j % 2
                kv = j // ATT_GROUP
                direct = e == kv
                ks, vs, ktr = (kwin, vwin, kt) if direct else (krwin, vrwin, krt)
                dk_acc, dv_acc = (dk_dir, dv_dir) if direct else (dk_rol, dv_rol)
                pair = slice(128 * p, 128 * (p + 1))
                qp = (qlo, qhi)[e][rows, pair]
                dop = (dolo, dohi)[e][rows, pair]
                lse_j = lse_ref[j:j + 1, qcols]
                d_j = d_t[j:j + 1, qb * QB:(qb + 1) * QB]
                p_m = jnp.exp(jnp.where(mask_m, _nt(ks[mrows, :], qp), NEG) - lse_j)
                p_w = jnp.exp(jnp.where(mask_w, _nt(ks[wrows, :], qp), NEG) - lse_j)
                ds_m = (p_m * (_nt(vs[mrows, :], dop) - d_j)).astype(BF)
                ds_w = (p_w * (_nt(vs[wrows, :], dop) - d_j)).astype(BF)
                hrows = pl.ds(ATT_HD * e, ATT_HD)
                dqt[pl.ds(ATT_HD * j, ATT_HD), qcols] = (_nn(ktr[hrows, pl.ds(0, N_META)], ds_m)
                                                         + _nn(ktr[hrows, pl.ds(_W0 + qb * QB, QB + HALO)], ds_w))
                dk_acc[mrows, :] += _nn(ds_m, qp)
                dk_acc[wrows, :] += _nn(ds_w, qp)
                dv_acc[mrows, :] += _nn(p_m.astype(BF), dop)
                dv_acc[wrows, :] += _nn(p_w.astype(BF), dop)
                dsk_ref[j:j + 1, :] -= jnp.exp(sk_ref[:, j:j + 1] - lse_j) * d_j

        dk_dir[...] += pltpu.roll(dk_rol[...], ATT_HD, 1)
        dv_dir[...] += pltpu.roll(dv_rol[...], ATT_HD, 1)
        meta_k[...] += dk_dir[pl.ds(0, N_META), :]
        meta_v[...] += dv_dir[pl.ds(0, N_META), :]
        first = jnp.where(t == 0, 1.0, 0.0)
        dk_dir[pl.ds(_C0 + TR - HALO, HALO), :] += carry_k[...]
        dv_dir[pl.ds(_C0 + TR - HALO, HALO), :] += carry_v[...]
        dk_dir[pl.ds(_C0 + META_PAD, N_META), :] += first * meta_k[...]
        dv_dir[pl.ds(_C0 + META_PAD, N_META), :] += first * meta_v[...]
        carry_k[...] = dk_dir[pl.ds(_W0, HALO), :]
        carry_v[...] = dv_dir[pl.ds(_W0, HALO), :]

        du_ref[:, 1152:1280] = dv_dir[pl.ds(_C0, TR), :].astype(BF)
        xhat_k, r_k = _head_rms(kc_ref[...].astype(F32), ones128)
        dkn = dk_dir[pl.ds(_C0, TR), :]
        dg_ref[1:2, 0:128] += jnp.sum(dkn * xhat_k, axis=0, keepdims=True)
        gd = dkn * kg_ref[...]
        du_ref[:, 1024:1152] = (r_k * (gd - xhat_k * _group_mean(gd * xhat_k, ones128))).astype(BF)
        dqn = dqt[...].T * _SCALE
        dg_ref[0:1, :] += jnp.sum(dqn * xhat_q, axis=0, keepdims=True)
        gd = dqn * qg_ref[...]
        du_ref[:, 0:512] = (r_q * (gd - xhat_q * _group_mean(gd * xhat_q, ones512))).astype(BF)

    rev = lambda i: (nt - 1 - i, 0)
    specs = _swa_in_specs(nt, True)
    win = pltpu.VMEM((_WROWS, 128), BF)
    wint = pltpu.VMEM((128, _WROWS), BF)
    tile_bf = pltpu.VMEM((TR, 512), BF)
    acc = pltpu.VMEM((_WROWS, 128), F32)
    return pl.pallas_call(
        body, name="swa_bwd", grid=(nt,),
        in_specs=specs + [pl.BlockSpec((TR, 512), rev), pl.BlockSpec((ATT_Q_HEADS, TR), lambda i: (0, nt - 1 - i)),
                          pl.BlockSpec((TR, 512), rev)],
        out_specs=[pl.BlockSpec((TR, N_C), rev), pl.BlockSpec((8, 512), lambda i: (0, 0)),
                   pl.BlockSpec((8, 128), lambda i: (0, 0))],
        out_shape=[jax.ShapeDtypeStruct((Lp, N_C), BF), jax.ShapeDtypeStruct((8, 512), F32),
                   jax.ShapeDtypeStruct((8, 128), F32)],
        scratch_shapes=[win, win, win, win, wint, wint, tile_bf, tile_bf, tile_bf, tile_bf,
                        pltpu.VMEM((512, TR), F32), acc, acc, acc, acc,
                        pltpu.VMEM((HALO, 128), F32), pltpu.VMEM((HALO, 128), F32),
                        pltpu.VMEM((N_META, 128), F32), pltpu.VMEM((N_META, 128), F32)],
        compiler_params=_cp(("arbitrary",)),
    )(uc, uc, uc, uc, uc, uc, uc, uc, qg8, kg2, sinks, o_save, lse, dyc)


def _mix_fwd(h, ya, yb, yc, ug, wa, wb, wc, wo):
    Lp = h.shape[0]
    wspec = lambda r: pl.BlockSpec((r, D_MODEL), lambda i: (0, 0))
    yspec = pl.BlockSpec((TRM, 512), lambda i: (i, 0))
    hspec = pl.BlockSpec((TRM, D_MODEL), lambda i: (i, 0))

    def body(h_ref, ya_ref, yb_ref, yc_ref, ug_ref, wa_ref, wb_ref, wc_ref, wo_ref, hn_ref, za_ref, zb_ref, zc_ref):
        mixed = jnp.zeros((TRM, D_MODEL), F32)
        for n, (y_ref, w_ref, z_ref) in enumerate(((ya_ref, wa_ref, za_ref), (yb_ref, wb_ref, zb_ref),
                                                   (yc_ref, wc_ref, zc_ref))):
            z = _nn(y_ref[...], w_ref[...])
            z_ref[...] = z.astype(BF)
            mixed = mixed + _sig(ug_ref[:, D_MODEL * n:D_MODEL * (n + 1)].astype(F32)) * z
        hn_ref[...] = h_ref[...] + _nn(mixed.astype(BF), wo_ref[...])

    return pl.pallas_call(
        body, name="mix_fwd", grid=(Lp // TRM,),
        in_specs=[hspec, yspec, yspec, yspec, pl.BlockSpec((TRM, N_G), lambda i: (i, 0)),
                  wspec(512), wspec(512), wspec(512), wspec(D_MODEL)],
        out_specs=[hspec, hspec, hspec, hspec],
        out_shape=[jax.ShapeDtypeStruct((Lp, D_MODEL), F32)] + [jax.ShapeDtypeStruct((Lp, D_MODEL), BF)] * 3,
        compiler_params=_cp(("parallel",)),
    )(h, ya, yb, yc, ug, wa, wb, wc, wo)


def _mix_bwd(dh, za, zb, zc, ug, wa, wb, wc, wo):
    Lp = dh.shape[0]
    wspec = lambda r: pl.BlockSpec((r, D_MODEL), lambda i: (0, 0))
    yspec = pl.BlockSpec((TRM, 512), lambda i: (i, 0))
    hspec = pl.BlockSpec((TRM, D_MODEL), lambda i: (i, 0))
    gspec = pl.BlockSpec((TRM, N_G), lambda i: (i, 0))

    def body(dh_ref, za_ref, zb_ref, zc_ref, ug_ref, wa_ref, wb_ref, wc_ref, wo_ref,
             dug_ref, mx_ref, dza_ref, dzb_ref, dzc_ref, dya_ref, dyb_ref, dyc_ref):
        dmix = _nt(dh_ref[...].astype(BF), wo_ref[...])
        mixed = jnp.zeros((TRM, D_MODEL), F32)
        for n, (z_ref, w_ref, dz_ref, dy_ref) in enumerate(((za_ref, wa_ref, dza_ref, dya_ref),
                                                            (zb_ref, wb_ref, dzb_ref, dyb_ref),
                                                            (zc_ref, wc_ref, dzc_ref, dyc_ref))):
            sl = slice(D_MODEL * n, D_MODEL * (n + 1))
            z = z_ref[...].astype(F32)
            gt = _sig(ug_ref[:, sl].astype(F32))
            mixed = mixed + gt * z
            dug_ref[:, sl] = (dmix * z * gt * (1.0 - gt)).astype(BF)
            dz = (dmix * gt).astype(BF)
            dz_ref[...] = dz
            dy_ref[...] = _nt(dz, w_ref[...]).astype(BF)
        mx_ref[...] = mixed.astype(BF)

    bf = lambda n: jax.ShapeDtypeStruct((Lp, n), BF)
    return pl.pallas_call(
        body, name="mix_bwd", grid=(Lp // TRM,),
        in_specs=[hspec, hspec, hspec, hspec, gspec, wspec(512), wspec(512), wspec(512), wspec(D_MODEL)],
        out_specs=[gspec, hspec, hspec, hspec, hspec, yspec, yspec, yspec],
        out_shape=[bf(N_G), bf(D_MODEL), bf(D_MODEL), bf(D_MODEL), bf(D_MODEL), bf(512), bf(512), bf(512)],
        compiler_params=_cp(("parallel",)),
    )(dh, za, zb, zc, ug, wa, wb, wc, wo)


def _inproj_bwd(dus, ws, h, dh, g):
    Lp = h.shape[0]
    widths = [w.shape[1] for w in ws]

    def body(dg_ref, da_ref, db_ref, dc_ref, wg_ref, wa_ref, wb_ref, wc_ref, h_ref, dh_ref, g_ref, o_ref, gg_ref):
        @pl.when(pl.program_id(0) == 0)
        def _():
            gg_ref[...] = jnp.zeros_like(gg_ref)

        dhn = (_nt(dg_ref[...], wg_ref[...]) + _nt(da_ref[...], wa_ref[...])
               + _nt(db_ref[...], wb_ref[...]) + _nt(dc_ref[...], wc_ref[...]))
        x = h_ref[...]
        r = lax.rsqrt(jnp.mean(x * x, axis=-1, keepdims=True) + EPS)
        xhat = x * r
        gg_ref[0:1, :] += jnp.sum(dhn * xhat, axis=0, keepdims=True)
        gd = dhn * g_ref[...]
        o_ref[...] = dh_ref[...] + r * (gd - xhat * jnp.mean(gd * xhat, axis=-1, keepdims=True))

    hspec = pl.BlockSpec((TRM, D_MODEL), lambda i: (i, 0))
    return pl.pallas_call(
        body, name="inproj_bwd", grid=(Lp // TRM,),
        in_specs=[pl.BlockSpec((TRM, n), lambda i: (i, 0)) for n in widths]
        + [pl.BlockSpec((D_MODEL, n), lambda i: (0, 0), pipeline_mode=pl.Buffered(1)) for n in widths]
        + [hspec, hspec, pl.BlockSpec((1, D_MODEL), lambda i: (0, 0))],
        out_specs=[hspec, pl.BlockSpec((8, D_MODEL), lambda i: (0, 0))],
        out_shape=[jax.ShapeDtypeStruct((Lp, D_MODEL), F32), jax.ShapeDtypeStruct((8, D_MODEL), F32)],
        compiler_params=_cp(("arbitrary",)),
    )(*dus, *ws, h, dh, g)


def _loss_head(h, tgt_pad, seq):
    Lp = h.shape[0]
    nt = Lp // TR

    def body(h_ref, t_ref, dh_ref, l_ref):
        i = pl.program_id(0)

        @pl.when(i == 0)
        def _():
            l_ref[...] = jnp.zeros_like(l_ref)

        row = i * TR + lax.broadcasted_iota(jnp.int32, (TR, 1), 0)
        e = jnp.where((row >= CHUNK) & (row < CHUNK + seq), h_ref[...] - t_ref[...], 0.0)
        dh_ref[...] = e * (1.0 / D_MODEL)
        l_ref[...] += (0.5 / D_MODEL) * jnp.sum(jnp.sum(e * e, axis=0, keepdims=True), axis=1, keepdims=True)

    hspec = pl.BlockSpec((TR, D_MODEL), lambda i: (i, 0))
    return pl.pallas_call(
        body, name="loss_head", grid=(nt,), in_specs=[hspec, hspec],
        out_specs=[hspec, pl.BlockSpec((8, 128), lambda i: (0, 0))],
        out_shape=[jax.ShapeDtypeStruct((Lp, D_MODEL), F32), jax.ShapeDtypeStruct((8, 128), F32)],
        compiler_params=_cp(("arbitrary",)),
    )(h, tgt_pad)


def _lb_softmax(lb_ref):
    x = lb_ref[...]
    e = jnp.exp(x - jnp.max(x, axis=0, keepdims=True))
    return e / jnp.sum(e, axis=0, keepdims=True)


def _lb_fwd(hg_lb):
    def body(lb_ref, o_ref):
        sm = _lb_softmax(lb_ref)
        acc = jnp.zeros((1, 512), F32)
        for l in range(DEPTH):
            if l > 0:
                acc = acc + sm[l:l + 1, :]
            o_ref[l:l + 1, :] = jnp.clip(acc, 0.0, 1.0)

    return pl.pallas_call(body, name="lb_fwd", out_shape=jax.ShapeDtypeStruct((DEPTH, 512), F32))(hg_lb)


def _lb_bwd(hg_lb, dlb_all):
    def body(lb_ref, d_ref, o_ref):
        sm = _lb_softmax(lb_ref)
        acc = jnp.zeros((1, 512), F32)
        gm = []
        for l in range(DEPTH):
            if l > 0:
                acc = acc + sm[l:l + 1, :]
            gm.append(jnp.where((acc >= 0.0) & (acc <= 1.0), d_ref[l:l + 1, :], 0.0))
        dsm = [jnp.zeros((1, 512), F32)]
        for j in range(1, DEPTH):
            s = gm[j]
            for l in range(j + 1, DEPTH):
                s = s + gm[l]
            dsm.append(s)
        dot = dsm[0] * sm[0:1, :]
        for j in range(1, DEPTH):
            dot = dot + dsm[j] * sm[j:j + 1, :]
        for j in range(DEPTH):
            o_ref[j:j + 1, :] = sm[j:j + 1, :] * (dsm[j] - dot)

    return pl.pallas_call(body, name="lb_bwd", out_shape=jax.ShapeDtypeStruct((DEPTH, 512), F32))(hg_lb, dlb_all)


_ANY = pl.BlockSpec(memory_space=pl.ANY)


def _chip_peers():
    x, y, c = lax.axis_index("x"), lax.axis_index("y"), lax.axis_index("c")
    return (x, y, c), [(1 - x, y, c), (x, 1 - y, c), (1 - x, 1 - y, c)]


def _gather_chips(arrs):
    n = len(arrs)

    def body(*refs):
        ins, outs = refs[:n], refs[n:2 * n]
        send, recv, loc = refs[2 * n:]
        (x, y, c), peers = _chip_peers()
        me = 2 * x + y
        copies = []
        for a in range(n):
            cp = pltpu.make_async_copy(ins[a], outs[a].at[me], loc.at[a])
            cp.start()
            copies.append(cp)
        rdmas = []
        for a in range(n):
            for p, peer in enumerate(peers):
                r = pltpu.make_async_remote_copy(src_ref=ins[a], dst_ref=outs[a].at[me], send_sem=send.at[a, p],
                                                 recv_sem=recv.at[a, p], device_id=peer, device_id_type=MESH)
                r.start()
                rdmas.append(r)
        for r in rdmas:
            r.wait()
        for cp in copies:
            cp.wait()

    return pl.pallas_call(
        body, name="gather_chips", in_specs=[_ANY] * n, out_specs=[_ANY] * n,
        out_shape=[jax.ShapeDtypeStruct((4,) + a.shape, a.dtype) for a in arrs],
        scratch_shapes=[pltpu.SemaphoreType.DMA((n, 3)), pltpu.SemaphoreType.DMA((n, 3)), pltpu.SemaphoreType.DMA((n,))],
        compiler_params=pltpu.CompilerParams(has_side_effects=True),
    )(*arrs)


def _scatter_chips(arrs):
    n = len(arrs)

    def body(*refs):
        ins, outs = refs[:n], refs[n:2 * n]
        send, recv, loc = refs[2 * n:]
        (x, y, c), peers = _chip_peers()
        me = 2 * x + y
        copies = []
        for a in range(n):
            cp = pltpu.make_async_copy(ins[a].at[me], outs[a].at[0], loc.at[a])
            cp.start()
            copies.append(cp)
        rdmas = []
        for a in range(n):
            for p, (px, py, pc) in enumerate(peers):
                r = pltpu.make_async_remote_copy(src_ref=ins[a].at[2 * px + py], dst_ref=outs[a].at[1 + p],
                                                 send_sem=send.at[a, p], recv_sem=recv.at[a, p],
                                                 device_id=(px, py, pc), device_id_type=MESH)
                r.start()
                rdmas.append(r)
        for r in rdmas:
            r.wait()
        for cp in copies:
            cp.wait()

    return pl.pallas_call(
        body, name="scatter_chips", in_specs=[_ANY] * n, out_specs=[_ANY] * n,
        out_shape=[jax.ShapeDtypeStruct(a.shape, a.dtype) for a in arrs],
        scratch_shapes=[pltpu.SemaphoreType.DMA((n, 3)), pltpu.SemaphoreType.DMA((n, 3)), pltpu.SemaphoreType.DMA((n,))],
        compiler_params=pltpu.CompilerParams(has_side_effects=True),
    )(*arrs)


def _swap_cores(arrs):
    n = len(arrs)

    def body(*refs):
        ins, outs = refs[:n], refs[n:2 * n]
        send, recv = refs[2 * n:]
        x, y, c = lax.axis_index("x"), lax.axis_index("y"), lax.axis_index("c")
        rdmas = []
        for a in range(n):
            r = pltpu.make_async_remote_copy(src_ref=ins[a], dst_ref=outs[a], send_sem=send.at[a], recv_sem=recv.at[a],
                                             device_id=(x, y, 1 - c), device_id_type=MESH)
            r.start()
            rdmas.append(r)
        for r in rdmas:
            r.wait()

    return pl.pallas_call(
        body, name="swap_cores", in_specs=[_ANY] * n, out_specs=[_ANY] * n,
        out_shape=[jax.ShapeDtypeStruct(a.shape, a.dtype) for a in arrs],
        scratch_shapes=[pltpu.SemaphoreType.DMA((n,)), pltpu.SemaphoreType.DMA((n,))],
        compiler_params=pltpu.CompilerParams(has_side_effects=True),
    )(*arrs)


def _allsum_small(p):
    R = p.shape[0]

    def body(p_ref, o_ref, buf, send, recv):
        x, y, c = lax.axis_index("x"), lax.axis_index("y"), lax.axis_index("c")
        me = 4 * x + 2 * y + c
        buf[me] = p_ref[...]
        rdmas = []
        for k in range(1, 8):
            peer = (x ^ (k >> 2), y ^ ((k >> 1) & 1), c ^ (k & 1))
            r = pltpu.make_async_remote_copy(src_ref=p_ref, dst_ref=buf.at[me], send_sem=send.at[k - 1],
                                             recv_sem=recv.at[k - 1], device_id=peer, device_id_type=MESH)
            r.start()
            rdmas.append(r)
        for r in rdmas:
            r.wait()
        acc = buf[0]
        for d in range(1, 8):
            acc = acc + buf[d]
        o_ref[...] = acc

    return pl.pallas_call(
        body, name="allsum_small", out_shape=jax.ShapeDtypeStruct((R, 512), F32),
        in_specs=[pl.BlockSpec(memory_space=pltpu.VMEM)], out_specs=pl.BlockSpec(memory_space=pltpu.VMEM),
        scratch_shapes=[pltpu.VMEM((8, R, 512), F32), pltpu.SemaphoreType.DMA((7,)), pltpu.SemaphoreType.DMA((7,))],
        compiler_params=_cp(has_side_effects=True),
    )(p)


def _sum4(parts, name):
    _, R, C = parts.shape
    tr = 256 if R % 256 == 0 else R

    def body(p_ref, o_ref):
        o_ref[...] = ((p_ref[0] + p_ref[1]) + p_ref[2]) + p_ref[3]

    return pl.pallas_call(
        body, name=name, grid=(R // tr,), in_specs=[pl.BlockSpec((4, tr, C), lambda i: (0, i, 0))],
        out_specs=pl.BlockSpec((tr, C), lambda i: (i, 0)), out_shape=jax.ShapeDtypeStruct((R, C), F32),
        compiler_params=_cp(("parallel",)),
    )(parts)


def _adamw(w, m, v, g0, g1, name):
    R, C = w.shape
    tr = 256 if R % 256 == 0 else R
    two = g1 is not None
    c1 = 1.0 / (1.0 - ADAM_B1 ** ADAM_STEP)
    c2 = 1.0 / (1.0 - ADAM_B2 ** ADAM_STEP)

    def body(*refs):
        if two:
            w_ref, m_ref, v_ref, a_ref, b_ref, g_ref, d_ref, nm_ref, nv_ref = refs
            g = a_ref[...] + b_ref[...]
        else:
            w_ref, m_ref, v_ref, a_ref, g_ref, d_ref, nm_ref, nv_ref = refs
            g = a_ref[...]
        g_ref[...] = g
        m = ADAM_B1 * m_ref[...] + (1.0 - ADAM_B1) * g
        v = ADAM_B2 * v_ref[...] + (1.0 - ADAM_B2) * (g * g)
        nm_ref[...] = m
        nv_ref[...] = v
        d_ref[...] = -ADAM_LR * ((m * c1) / (jnp.sqrt(v * c2) + ADAM_EPS) + ADAM_WD * w_ref[...])

    spec = pl.BlockSpec((tr, C), lambda i: (i, 0))
    n_in = 5 if two else 4
    ins = (w, m, v, g0, g1) if two else (w, m, v, g0)
    return pl.pallas_call(
        body, name=name, grid=(R // tr,), in_specs=[spec] * n_in, out_specs=[spec] * 4,
        out_shape=[jax.ShapeDtypeStruct((R, C), F32)] * 4, compiler_params=_cp(("parallel",)),
    )(*ins)


def _pad8(a):
    r = (-a.shape[0]) % 8
    return a if r == 0 else jnp.pad(a, ((0, r), (0, 0)))


def _local_step(x, tgt, meta, P):
    seq = x.shape[0]
    assert seq % TR == 0
    Lp = seq + TR
    h = jnp.concatenate([jnp.zeros((META_PAD, D_MODEL), F32), meta, x, jnp.zeros((TAIL_PAD, D_MODEL), F32)], axis=0)
    tgt_pad = jnp.pad(tgt, ((CHUNK, TAIL_PAD), (0, 0)))

    saved = []
    for l in range(DEPTH):
        p = P[l]
        hn = _rms_fwd(h, p["norm_g"])
        mm = functools.partial(_matmul, out_dtype=BF, tm=TR, tk=D_MODEL, col_major_grid=True)
        ug = mm(hn, p["w_g"], tn=N_G // 2, name="inproj_g")
        ua = mm(hn, p["w_a"], tn=N_A, name="inproj_a")
        ub = mm(hn, p["w_b"], tn=N_B, name="inproj_b")
        uc = mm(hn, p["w_c"], tn=N_C, name="inproj_c")
        ya, yconv = _conv_fwd(ua, p["conv_w"], p["conv_vec"])
        yb, o_hg, s_hg = _hg_fwd(ub, p["lb"], p["gn4"])
        yc, o_at, lse = _swa_fwd(uc, p["qg"], p["kg"], p["sinks"])
        h_new, za, zb, zc = _mix_fwd(h, ya, yb, yc, ug, p["w_ao"], p["w_bo"], p["w_co"], p["w_out"])
        saved.append(dict(h=h, hn=hn, ug=ug, ua=ua, ub=ub, uc=uc, ya=ya, yconv=yconv, yb=yb, o_hg=o_hg, s_hg=s_hg,
                          yc=yc, o_at=o_at, lse=lse, za=za, zb=zb, zc=zc))
        h = h_new

    dh, loss8 = _loss_head(h, tgt_pad, seq)

    grads = [None] * DEPTH
    for l in reversed(range(DEPTH)):
        p, s = P[l], saved[l]
        dug, mixed, dza, dzb, dzc, dya, dyb, dyc = _mix_bwd(dh, s["za"], s["zb"], s["zc"], s["ug"],
                                                             p["w_ao"], p["w_bo"], p["w_co"], p["w_out"])
        tnmm = functools.partial(_matmul, ta=True, out_dtype=F32, tk=TR)
        g = {}
        g["w_out"] = tnmm(mixed, dh, tm=D_MODEL, tn=D_MODEL, name="dw_out")
        g["w_ao"] = tnmm(s["ya"], dza, tm=512, tn=D_MODEL, name="dw_ao")
        g["w_bo"] = tnmm(s["yb"], dzb, tm=512, tn=D_MODEL, name="dw_bo")
        g["w_co"] = tnmm(s["yc"], dzc, tm=512, tn=D_MODEL, name="dw_co")
        dua, g["conv_w"], g["conv_vec"] = _conv_bwd(s["ua"], s["yconv"], dya, p["conv_w"], p["conv_vec"])
        dub, g["hg_small"] = _hg_bwd(s["ub"], p["lb"], p["gn4"], s["o_hg"], s["s_hg"], dyb)
        duc, g["at_gain"], g["at_sink"] = _swa_bwd(s["uc"], p["qg"], p["kg"], p["sinks"], s["o_at"], s["lse"], dyc)
        g["w_g"] = tnmm(s["hn"], dug, tm=D_MODEL, tn=N_G // 2, name="dw_in_g")
        g["w_a"] = tnmm(s["hn"], dua, tm=D_MODEL, tn=N_A, name="dw_in_a")
        g["w_b"] = tnmm(s["hn"], dub, tm=D_MODEL, tn=N_B, name="dw_in_b")
        g["w_c"] = tnmm(s["hn"], duc, tm=D_MODEL, tn=N_C, name="dw_in_c")
        dh, g["norm_g"] = _inproj_bwd([dug, dua, dub, duc], [p["w_g"], p["w_a"], p["w_b"], p["w_c"]],
                                      s["h"], dh, p["norm_g"])
        grads[l] = g
    return loss8, dh, grads


def _split_w_in(w):
    return dict(w_a=w[:, 0:1536], w_b=w[:, 1536:3584],
                w_c=jnp.concatenate([w[:, 3584:4096], w[:, 4352:4864], w[:, 4096:4352]], axis=1),
                w_g=w[:, 4864:7936])


def _join_w_in(g):
    c = g["w_c"]
    return jnp.concatenate([g["w_a"], g["w_b"], c[:, 0:512], c[:, 1024:1280], c[:, 512:1024], g["w_g"]], axis=1)


def _attn_small(g):
    return (g["at_gain"][0].reshape(ATT_Q_HEADS, ATT_HD).sum(0),
            g["at_gain"][1, 0:128].reshape(ATT_KV_HEADS, ATT_HD).sum(0), g["at_sink"].sum(1))


_SMALL = (("norm_g", 8), ("meta", 32), ("conv_w", 32 * DEPTH), ("conv_b", 8), ("conv_ln_g", 8), ("conv_ln_b", 8),
          ("lb", 8), ("hg_norm_g", 8), ("q_norm_g", 8), ("k_norm_g", 8), ("sinks", 8))


def _small_offsets():
    off, o = {}, 0
    for name, rows in _SMALL:
        off[name] = (o, rows)
        o += rows
    return off, o


def _pack_small(d):
    parts = []
    for name, rows in _SMALL:
        a = d[name]
        parts.append(jnp.pad(a, ((0, rows - a.shape[0]), (0, 512 - a.shape[1]))))
    return jnp.concatenate(parts, axis=0)


def kernel(x, meta_tokens, norm_g, w_in, conv_w, conv_b, conv_ln_g, conv_ln_b, w_conv_out, hg_lower_bounds, hg_norm_g, w_hg_out, q_norm_g, k_norm_g, attn_sinks, w_att_out, w_out, loss_target, m_meta_tokens, m_norm_g, m_w_in, m_conv_w, m_conv_b, m_conv_ln_g, m_conv_ln_b, m_w_conv_out, m_hg_lower_bounds, m_hg_norm_g, m_w_hg_out, m_q_norm_g, m_k_norm_g, m_attn_sinks, m_w_att_out, m_w_out, v_meta_tokens, v_norm_g, v_w_in, v_conv_w, v_conv_b, v_conv_ln_g, v_conv_ln_b, v_w_conv_out, v_hg_lower_bounds, v_hg_norm_g, v_w_hg_out, v_q_norm_g, v_k_norm_g, v_attn_sinks, v_w_att_out, v_w_out):
    xi, yi = lax.axis_index("x"), lax.axis_index("y")
    chip = 2 * xi + yi
    NS = w_in.shape[2]
    CS = conv_w.shape[2]
    MS = meta_tokens.shape[1]

    g_win, g_wao, g_wbo, g_wco, g_wout, g_meta, g_convw = _gather_chips([
        w_in.astype(BF).reshape(DEPTH * D_MODEL, NS), w_conv_out.astype(BF).reshape(DEPTH * 512, MS),
        w_hg_out.astype(BF).reshape(DEPTH * 512, MS), w_att_out.astype(BF).reshape(DEPTH * 512, MS),
        w_out.astype(BF).reshape(DEPTH * MS, D_MODEL), meta_tokens, conv_w.reshape(DEPTH * CONV_WIDTH, CS)])
    full_cols = lambda g, rows: g.reshape(4, DEPTH, rows, -1).transpose(1, 2, 0, 3).reshape(DEPTH, rows, -1)
    win_f = full_cols(g_win, D_MODEL)
    wao_f, wbo_f, wco_f = full_cols(g_wao, 512), full_cols(g_wbo, 512), full_cols(g_wco, 512)
    wout_f = g_wout.reshape(4, DEPTH, MS, D_MODEL).transpose(1, 0, 2, 3).reshape(DEPTH, D_MODEL, D_MODEL)
    meta_f = g_meta.transpose(1, 0, 2).reshape(N_META, D_MODEL)
    convw_f = full_cols(g_convw, CONV_WIDTH)

    lb_all = _lb_fwd(hg_lower_bounds)
    P = []
    for l in range(DEPTH):
        p = _split_w_in(win_f[l])
        p.update(w_ao=wao_f[l], w_bo=wbo_f[l], w_co=wco_f[l], w_out=wout_f[l], norm_g=norm_g[l:l + 1],
                 conv_w=convw_f[l], conv_vec=_pad8(jnp.stack([conv_b[l], conv_ln_g[l], conv_ln_b[l]])),
                 lb=lb_all[l:l + 1], gn4=jnp.tile(hg_norm_g[l:l + 1], (1, HG_HEADS)),
                 qg=jnp.tile(q_norm_g[l:l + 1], (1, ATT_Q_HEADS)), kg=jnp.tile(k_norm_g[l:l + 1], (1, ATT_KV_HEADS)),
                 sinks=attn_sinks[l:l + 1])
        P.append(p)

    loss8, dh0, grads = _local_step(x[0], loss_target[0], meta_f, P)
    seq = x.shape[1]
    grad_x = dh0[CHUNK:CHUNK + seq][None]
    loss = lax.psum(loss8[0, 0], ("x", "y", "c"))

    shard_cols = lambda a: a.reshape(a.shape[0], 4, -1).transpose(1, 0, 2)
    stack = lambda f: jnp.concatenate([f(grads[l]) for l in range(DEPTH)], axis=1)
    big = [stack(lambda g: shard_cols(_join_w_in(g))), stack(lambda g: shard_cols(g["w_ao"])),
           stack(lambda g: shard_cols(g["w_bo"])), stack(lambda g: shard_cols(g["w_co"])),
           stack(lambda g: g["w_out"].reshape(4, MS, D_MODEL))]
    parts = _scatter_chips(big)
    mine = [_sum4(pp, name="sum_chips") for pp in parts]
    theirs = _swap_cores(mine)

    dlb_all = jnp.concatenate([grads[l]["hg_small"][0:1] for l in range(DEPTH)], axis=0)
    small = dict(
        norm_g=jnp.concatenate([grads[l]["norm_g"][0:1] for l in range(DEPTH)], axis=0).reshape(8, 512),
        meta=dh0[META_PAD:CHUNK].reshape(32, 512),
        conv_w=jnp.concatenate([grads[l]["conv_w"] for l in range(DEPTH)], axis=0),
        conv_b=jnp.concatenate([grads[l]["conv_vec"][0:1] for l in range(DEPTH)], axis=0),
        conv_ln_g=jnp.concatenate([grads[l]["conv_vec"][1:2] for l in range(DEPTH)], axis=0),
        conv_ln_b=jnp.concatenate([grads[l]["conv_vec"][2:3] for l in range(DEPTH)], axis=0),
        lb=_lb_bwd(hg_lower_bounds, dlb_all),
        hg_norm_g=jnp.concatenate([grads[l]["hg_small"][1:2].reshape(HG_HEADS, HG_D).sum(0, keepdims=True)
                                   for l in range(DEPTH)], axis=0),
        q_norm_g=jnp.stack([_attn_small(grads[l])[0] for l in range(DEPTH)]),
        k_norm_g=jnp.stack([_attn_small(grads[l])[1] for l in range(DEPTH)]),
        sinks=jnp.stack([_attn_small(grads[l])[2] for l in range(DEPTH)]),
    )
    gsum = _allsum_small(_pack_small(small))
    off, _ = _small_offsets()

    def take(name, rows, cols):
        o, _ = off[name]
        return gsum[o:o + rows, 0:cols]

    g_meta_full = take("meta", 32, 512).reshape(N_META, D_MODEL)
    g_convw_full = take("conv_w", 32 * DEPTH, 512).reshape(DEPTH, 32, 512)[:, :CONV_WIDTH]
    small_grads = dict(
        norm_g=take("norm_g", 8, 512),
        meta=lax.dynamic_slice_in_dim(g_meta_full, chip * MS, MS, axis=1),
        conv_w=lax.dynamic_slice_in_dim(g_convw_full, chip * CS, CS, axis=2).reshape(DEPTH * CONV_WIDTH, CS),
        conv_b=take("conv_b", DEPTH, 512), conv_ln_g=take("conv_ln_g", DEPTH, 512), conv_ln_b=take("conv_ln_b", DEPTH, 512),
        lb=take("lb", DEPTH, 512), hg_norm_g=take("hg_norm_g", DEPTH, HG_D), q_norm_g=take("q_norm_g", DEPTH, ATT_HD),
        k_norm_g=take("k_norm_g", DEPTH, ATT_HD), sinks=take("sinks", DEPTH, ATT_Q_HEADS))

    def big_update(w, m, v, a, b, name):
        shp = w.shape
        r2 = lambda t: t.reshape(-1, shp[-1])
        outs = _adamw(r2(w), r2(m), r2(v), a, b, name)
        return [o.reshape(shp) for o in outs]

    res = {}
    res["w_in"] = big_update(w_in, m_w_in, v_w_in, mine[0], theirs[0], "adamw_w_in")
    res["w_conv_out"] = big_update(w_conv_out, m_w_conv_out, v_w_conv_out, mine[1], theirs[1], "adamw_w_ao")
    res["w_hg_out"] = big_update(w_hg_out, m_w_hg_out, v_w_hg_out, mine[2], theirs[2], "adamw_w_bo")
    res["w_att_out"] = big_update(w_att_out, m_w_att_out, v_w_att_out, mine[3], theirs[3], "adamw_w_co")
    res["w_out"] = big_update(w_out, m_w_out, v_w_out, mine[4], theirs[4], "adamw_w_out")

    small_w = dict(meta=(meta_tokens, m_meta_tokens, v_meta_tokens), norm_g=(norm_g, m_norm_g, v_norm_g),
                   conv_w=(conv_w, m_conv_w, v_conv_w), conv_b=(conv_b, m_conv_b, v_conv_b),
                   conv_ln_g=(conv_ln_g, m_conv_ln_g, v_conv_ln_g), conv_ln_b=(conv_ln_b, m_conv_ln_b, v_conv_ln_b),
                   lb=(hg_lower_bounds, m_hg_lower_bounds, v_hg_lower_bounds),
                   hg_norm_g=(hg_norm_g, m_hg_norm_g, v_hg_norm_g), q_norm_g=(q_norm_g, m_q_norm_g, v_q_norm_g),
                   k_norm_g=(k_norm_g, m_k_norm_g, v_k_norm_g), sinks=(attn_sinks, m_attn_sinks, v_attn_sinks))
    view = lambda n, t: t.reshape(-1, 512) if n == "norm_g" else t.reshape(-1, t.shape[-1])
    pw, pm, pv = (_pack_rows([view(n, small_w[n][k]) for n in small_w]) for k in range(3))
    pg = _pack_rows([small_grads[n] for n in small_w])
    packed = _adamw(pw, pm, pv, pg, None, "adamw_small")
    o = 0
    for n in small_w:
        r, cdim = view(n, small_w[n][0]).shape
        res[n] = [t[o:o + r, 0:cdim].reshape(small_w[n][0].shape) for t in packed]
        o += -(-r // 8) * 8

    order = [("meta", None), ("norm_g", None), ("w_in", None), ("conv_w", None), ("conv_b", None), ("conv_ln_g", None),
             ("conv_ln_b", None), ("w_conv_out", None), ("lb", None), ("hg_norm_g", None), ("w_hg_out", None),
             ("q_norm_g", None), ("k_norm_g", None), ("sinks", None), ("w_att_out", None), ("w_out", None)]
    outs = [loss, grad_x]
    for k in range(4):
        outs += [res[n][k] for n, _ in order]
    return tuple(outs)


def _pack_rows(arrs):
    parts = []
    for a in arrs:
        r = (-a.shape[0]) % 8
        parts.append(jnp.pad(a, ((0, r), (0, 512 - a.shape[1]))))
    return jnp.concatenate(parts, axis=0)
```

```python
import functools

import jax
import jax.numpy as jnp
from jax import lax
from jax.experimental import pallas as pl
from jax.experimental.pallas import tpu as pltpu

F32 = jnp.float32
BF = jnp.bfloat16

D_MODEL = 1024
DEPTH = 4
CHUNK = 64
N_META = 16
META_PAD = CHUNK - N_META
D_CONV = 512
CONV_WIDTH = 31
HG_HEADS = 4
HG_D = 128
ATT_Q_HEADS = 8
ATT_KV_HEADS = 2
ATT_HD = 64
ATT_GROUP = ATT_Q_HEADS // ATT_KV_HEADS
EPS = 1e-6
F_FLOOR = 1e-30
NEG = -1e30

ADAM_LR = 0.001
ADAM_B1 = 0.9
ADAM_B2 = 0.999
ADAM_EPS = 1e-08
ADAM_WD = 0.01
ADAM_STEP = 10

TR = 512
TRM = 256
CONV_RB = 32
QB = 128
HALO = 128
TAIL_PAD = TR - CHUNK
VMEM_LIMIT = 56 * 1024 * 1024

N_G, N_A, N_B, N_C = 3 * D_MODEL, 3 * D_CONV, 4 * 512, 2 * 512 + 2 * 128

MESH = pl.DeviceIdType.MESH


def _cp(sem=None, vmem=VMEM_LIMIT, **kw):
    if sem is None:
        return pltpu.CompilerParams(vmem_limit_bytes=vmem, **kw)
    return pltpu.CompilerParams(dimension_semantics=sem, vmem_limit_bytes=vmem, **kw)


def _nn(a, b):
    return lax.dot_general(a, b, (((1,), (0,)), ((), ())), preferred_element_type=F32)


def _nt(a, b):
    return lax.dot_general(a, b, (((1,), (1,)), ((), ())), preferred_element_type=F32)


def _tn(a, b):
    return lax.dot_general(a, b, (((0,), (0,)), ((), ())), preferred_element_type=F32)


def _sig(x):
    return jax.nn.sigmoid(x)


def _silu(x):
    return x * _sig(x)


def _dsilu(x):
    s = _sig(x)
    return s * (1.0 + x * (1.0 - s))


def _split3(x):
    hi = x.astype(BF)
    r1 = x - hi.astype(F32)
    mid = r1.astype(BF)
    lo = (r1 - mid.astype(F32)).astype(BF)
    return hi, mid, lo


def _mm3(t, x):
    hi, mid, lo = _split3(x)
    return _nn(t, hi) + _nn(t, mid) + _nn(t, lo)


def _chunk_tri(n, upper):
    r = lax.broadcasted_iota(jnp.int32, (n, n), 0)
    c = lax.broadcasted_iota(jnp.int32, (n, n), 1)
    same = jnp.right_shift(r, 6) == jnp.right_shift(c, 6)
    tri = (c >= r) if upper else (c <= r)
    return jnp.where(same & tri, 1.0, 0.0).astype(BF)


def _matmul(a, b, *, ta=False, tb=False, out_dtype, tm, tn, tk, name, col_major_grid=False):
    if ta:
        K, M = a.shape
    else:
        M, K = a.shape
    N = b.shape[0] if tb else b.shape[1]
    assert M % tm == 0 and N % tn == 0 and K % tk == 0, (name, M, N, K, tm, tn, tk)
    nk = K // tk
    if col_major_grid:
        grid = (N // tn, M // tm, nk)
        ij = lambda g0, g1: (g1, g0)
    else:
        grid = (M // tm, N // tn, nk)
        ij = lambda g0, g1: (g0, g1)
    if ta:
        a_spec = pl.BlockSpec((tk, tm), lambda g0, g1, k: (k, ij(g0, g1)[0]))
    else:
        a_spec = pl.BlockSpec((tm, tk), lambda g0, g1, k: (ij(g0, g1)[0], k))
    if tb:
        b_spec = pl.BlockSpec((tn, tk), lambda g0, g1, k: (ij(g0, g1)[1], k))
    else:
        b_spec = pl.BlockSpec((tk, tn), lambda g0, g1, k: (k, ij(g0, g1)[1]))
    o_spec = pl.BlockSpec((tm, tn), lambda g0, g1, k: ij(g0, g1))
    dims = (((0 if ta else 1,), (1 if tb else 0,)), ((), ()))
    use_acc = nk > 1 and out_dtype != F32

    def body(a_ref, b_ref, o_ref, *scr):
        k = pl.program_id(2)
        p = lax.dot_general(a_ref[...].astype(BF), b_ref[...].astype(BF), dims, preferred_element_type=F32)
        if nk == 1:
            o_ref[...] = p.astype(out_dtype)
        else:
            acc = scr[0] if use_acc else o_ref

            @pl.when(k == 0)
            def _():
                acc[...] = p

            @pl.when(k > 0)
            def _():
                acc[...] += p

            if use_acc:
                @pl.when(k == nk - 1)
                def _():
                    o_ref[...] = acc[...].astype(out_dtype)

    return pl.pallas_call(
        body, name=name, grid=grid, in_specs=[a_spec, b_spec], out_specs=o_spec,
        out_shape=jax.ShapeDtypeStruct((M, N), out_dtype),
        scratch_shapes=[pltpu.VMEM((tm, tn), F32)] if use_acc else [],
        compiler_params=_cp(("parallel", "parallel", "arbitrary")),
    )(a, b)


def _rms_fwd(h, g):
    Lp = h.shape[0]

    def body(h_ref, g_ref, o_ref):
        x = h_ref[...]
        r = lax.rsqrt(jnp.mean(x * x, axis=-1, keepdims=True) + EPS)
        o_ref[...] = (x * r * g_ref[...]).astype(BF)

    return pl.pallas_call(
        body, name="rms_fwd", grid=(Lp // TR,),
        in_specs=[pl.BlockSpec((TR, D_MODEL), lambda i: (i, 0)), pl.BlockSpec((1, D_MODEL), lambda i: (0, 0))],
        out_specs=pl.BlockSpec((TR, D_MODEL), lambda i: (i, 0)),
        out_shape=jax.ShapeDtypeStruct((Lp, D_MODEL), BF),
        compiler_params=_cp(("parallel",)),
    )(h, g)


def _glu(ua, row):
    a = ua[:, 0:D_CONV].astype(F32)
    gl = ua[:, D_CONV:2 * D_CONV].astype(F32)
    return jnp.where(row >= META_PAD, a * _sig(gl), 0.0)


_SH_ROWS = TR + CHUNK - 8


def _fill_shifts(src, sh):
    for b in range(1, 8):
        sh[b - 1] = src[pl.ds(b, _SH_ROWS), :]


def _shifted(src, sh, start, n):
    b = start % 8
    if b == 0:
        return src[pl.ds(start, n), :]
    return sh[b - 1, pl.ds(start - b, n), :]


def _conv_fwd(ua, cw, cvec):
    Lp = ua.shape[0]
    nt = Lp // TR
    hb = TR // CHUNK

    def body(cur_ref, halo_ref, w_ref, v_ref, ya_ref, yc_ref, ext, sh):
        i = pl.program_id(0)
        row = i * TR + lax.broadcasted_iota(jnp.int32, (TR, 1), 0)
        hrow = i * TR - CHUNK + lax.broadcasted_iota(jnp.int32, (CHUNK, 1), 0)
        ext[pl.ds(0, CHUNK), :] = jnp.where(i > 0, _glu(halo_ref[...], hrow), 0.0)
        ext[pl.ds(CHUNK, TR), :] = _glu(cur_ref[...], row)
        _fill_shifts(ext, sh)
        for rb in range(TR // CONV_RB):
            r0 = rb * CONV_RB
            rows = pl.ds(r0, CONV_RB)
            acc = jnp.zeros((CONV_RB, D_CONV), F32)
            for j in range(CONV_WIDTH):
                acc = acc + _shifted(ext, sh, r0 + CHUNK - (CONV_WIDTH - 1) + j, CONV_RB) * w_ref[j:j + 1, :]
            y = acc + v_ref[0:1, :]
            yc_ref[rows, :] = y
            mu = jnp.mean(y, axis=-1, keepdims=True)
            d = y - mu
            var = jnp.mean(d * d, axis=-1, keepdims=True)
            yn = d * lax.rsqrt(var + EPS) * v_ref[1:2, :] + v_ref[2:3, :]
            ya_ref[rows, :] = (_silu(yn) * _silu(cur_ref[rows, 2 * D_CONV:3 * D_CONV].astype(F32))).astype(BF)

    return pl.pallas_call(
        body, name="conv_fwd", grid=(nt,),
        in_specs=[pl.BlockSpec((TR, N_A), lambda i: (i, 0)),
                  pl.BlockSpec((CHUNK, N_A), lambda i: (jnp.maximum(i * hb - 1, 0), 0)),
                  pl.BlockSpec((CONV_WIDTH, D_CONV), lambda i: (0, 0)),
                  pl.BlockSpec((8, D_CONV), lambda i: (0, 0))],
        out_specs=[pl.BlockSpec((TR, D_CONV), lambda i: (i, 0)), pl.BlockSpec((TR, D_CONV), lambda i: (i, 0))],
        out_shape=[jax.ShapeDtypeStruct((Lp, D_CONV), BF), jax.ShapeDtypeStruct((Lp, D_CONV), F32)],
        scratch_shapes=[pltpu.VMEM((TR + CHUNK, D_CONV), F32), pltpu.VMEM((7, _SH_ROWS, D_CONV), F32)],
        compiler_params=_cp(("parallel",)),
    )(ua, ua, cw, cvec)


def _conv_bwd(ua, yconv, dya, cw, cvec):
    Lp = ua.shape[0]
    nt = Lp // TR
    hb = TR // CHUNK
    nhb = Lp // CHUNK

    def ln_bwd(y, dout, gate, v_ref):
        mu = jnp.mean(y, axis=-1, keepdims=True)
        d = y - mu
        var = jnp.mean(d * d, axis=-1, keepdims=True)
        rstd = lax.rsqrt(var + EPS)
        xhat = d * rstd
        yn = xhat * v_ref[1:2, :] + v_ref[2:3, :]
        dyn = dout * _silu(gate) * _dsilu(yn)
        dxh = dyn * v_ref[1:2, :]
        dyc = rstd * (dxh - jnp.mean(dxh, axis=-1, keepdims=True) - xhat * jnp.mean(dxh * xhat, axis=-1, keepdims=True))
        return dyc, dyn, xhat, yn

    def body(cur_ref, prev_ref, next_ref, yc_ref, ycn_ref, dy_ref, dyn_ref, w_ref, v_ref,
             du_ref, dw_ref, dv_ref, uext, dext, dwacc, ush, dsh):
        i = pl.program_id(0)

        @pl.when(i == 0)
        def _():
            dwacc[...] = jnp.zeros_like(dwacc)
            dv_ref[...] = jnp.zeros_like(dv_ref)

        row = i * TR + lax.broadcasted_iota(jnp.int32, (TR, 1), 0)
        hrow = i * TR - CHUNK + lax.broadcasted_iota(jnp.int32, (CHUNK, 1), 0)
        uext[pl.ds(0, CHUNK), :] = jnp.where(i > 0, _glu(prev_ref[...], hrow), 0.0)
        uext[pl.ds(CHUNK, TR), :] = _glu(cur_ref[...], row)

        s_b = jnp.zeros((1, D_CONV), F32)
        s_g = jnp.zeros((1, D_CONV), F32)
        s_bb = jnp.zeros((1, D_CONV), F32)
        for rb in range(TR // CONV_RB):
            rows = pl.ds(rb * CONV_RB, CONV_RB)
            gate = cur_ref[rows, 2 * D_CONV:3 * D_CONV].astype(F32)
            dout = dy_ref[rows, :].astype(F32)
            dyc, dyn, xhat, yn = ln_bwd(yc_ref[rows, :], dout, gate, v_ref)
            du_ref[rows, 2 * D_CONV:3 * D_CONV] = (dout * _silu(yn) * _dsilu(gate)).astype(BF)
            dext[rows, :] = dyc
            s_b = s_b + jnp.sum(dyc, axis=0, keepdims=True)
            s_g = s_g + jnp.sum(dyn * xhat, axis=0, keepdims=True)
            s_bb = s_bb + jnp.sum(dyn, axis=0, keepdims=True)
        dv_ref[0:1, :] += s_b
        dv_ref[1:2, :] += s_g
        dv_ref[2:3, :] += s_bb
        dyc_n, _, _, _ = ln_bwd(ycn_ref[...], dyn_ref[...].astype(F32),
                                next_ref[:, 2 * D_CONV:3 * D_CONV].astype(F32), v_ref)
        dext[pl.ds(TR, CHUNK), :] = jnp.where(i < nt - 1, dyc_n, 0.0)
        _fill_shifts(uext, ush)
        _fill_shifts(dext, dsh)

        for rb in range(TR // CONV_RB):
            r0 = rb * CONV_RB
            rows = pl.ds(r0, CONV_RB)
            d_blk = dext[rows, :]
            dglu = jnp.zeros((CONV_RB, D_CONV), F32)
            for j in range(CONV_WIDTH):
                dglu = dglu + _shifted(dext, dsh, r0 + CONV_WIDTH - 1 - j, CONV_RB) * w_ref[j:j + 1, :]
                prod = d_blk * _shifted(uext, ush, r0 + CHUNK - (CONV_WIDTH - 1) + j, CONV_RB)
                part = prod[0:8, :]
                for s in range(1, CONV_RB // 8):
                    part = part + prod[8 * s:8 * s + 8, :]
                dwacc[j] += part
            a = cur_ref[rows, 0:D_CONV].astype(F32)
            sg = _sig(cur_ref[rows, D_CONV:2 * D_CONV].astype(F32))
            grow = i * TR + r0 + lax.broadcasted_iota(jnp.int32, (CONV_RB, 1), 0)
            dglu = jnp.where(grow >= META_PAD, dglu, 0.0)
            du_ref[rows, 0:D_CONV] = (dglu * sg).astype(BF)
            du_ref[rows, D_CONV:2 * D_CONV] = (dglu * a * sg * (1.0 - sg)).astype(BF)

        @pl.when(i == nt - 1)
        def _():
            dw_ref[...] = jnp.sum(dwacc[...], axis=1)

    nxt = lambda i: (jnp.minimum(i * hb + hb, nhb - 1), 0)
    return pl.pallas_call(
        body, name="conv_bwd", grid=(nt,),
        in_specs=[pl.BlockSpec((TR, N_A), lambda i: (i, 0)),
                  pl.BlockSpec((CHUNK, N_A), lambda i: (jnp.maximum(i * hb - 1, 0), 0)),
                  pl.BlockSpec((CHUNK, N_A), nxt),
                  pl.BlockSpec((TR, D_CONV), lambda i: (i, 0)),
                  pl.BlockSpec((CHUNK, D_CONV), nxt),
                  pl.BlockSpec((TR, D_CONV), lambda i: (i, 0)),
                  pl.BlockSpec((CHUNK, D_CONV), nxt),
                  pl.BlockSpec((CONV_WIDTH, D_CONV), lambda i: (0, 0)),
                  pl.BlockSpec((8, D_CONV), lambda i: (0, 0))],
        out_specs=[pl.BlockSpec((TR, N_A), lambda i: (i, 0)),
                   pl.BlockSpec((32, D_CONV), lambda i: (0, 0)),
                   pl.BlockSpec((8, D_CONV), lambda i: (0, 0))],
        out_shape=[jax.ShapeDtypeStruct((Lp, N_A), BF), jax.ShapeDtypeStruct((32, D_CONV), F32),
                   jax.ShapeDtypeStruct((8, D_CONV), F32)],
        scratch_shapes=[pltpu.VMEM((TR + CHUNK, D_CONV), F32), pltpu.VMEM((TR + CHUNK, D_CONV), F32),
                        pltpu.VMEM((32, 8, D_CONV), F32), pltpu.VMEM((7, _SH_ROWS, D_CONV), F32),
                        pltpu.VMEM((7, _SH_ROWS, D_CONV), F32)],
        compiler_params=_cp(("arbitrary",)),
    )(ua, ua, ua, yconv, yconv, dya, dya, cw, cvec)


def _hg_gates(ub_ref, lbv, row):
    q = ub_ref[:, 0:512].astype(F32)
    z = ub_ref[:, 512:1024].astype(F32)
    valid = row >= META_PAD
    sig = _sig(z)
    f = lbv + (1.0 - lbv) * sig
    g = jnp.where(valid, jnp.log(jnp.maximum(f, F_FLOOR)), 0.0)
    k = jnp.where(valid, (1.0 - lbv) * _sig(-z), 0.0)
    return q, k, g, sig, f


def _hg_chunk_terms(b_c, q_c, k_c):
    bm = b_c[CHUNK // 2 - 1:CHUNK // 2, :]
    bl = b_c[CHUNK - 1:CHUNK, :]
    e1 = jnp.exp(b_c - bm)
    e2 = jnp.exp(bm - b_c)
    e0 = jnp.exp(b_c)
    e3 = jnp.exp(bl - b_c)
    el = jnp.exp(bl)
    return e1, e2, e0, e3, el, q_c * e1, k_c * e2, q_c * e0, k_c * e3


def _hg_fwd(ub, lb, gn4):
    Lp = ub.shape[0]
    nt = Lp // TR
    cpt = TR // CHUNK

    def body(ub_ref, lb_ref, gn_ref, yb_ref, o_ref, ss_ref, st, bsc, qsc, ksc):
        i = pl.program_id(0)

        @pl.when(i == 0)
        def _():
            st[...] = jnp.zeros_like(st)

        row = i * TR + lax.broadcasted_iota(jnp.int32, (TR, 1), 0)
        q, k, g, _, _ = _hg_gates(ub_ref, lb_ref[...], row)
        qsc[...] = _silu(q)
        ksc[...] = k
        bsc[...] = _mm3(_chunk_tri(TR, False), g)
        tri = lax.broadcasted_iota(jnp.int32, (CHUNK, CHUNK), 1) <= lax.broadcasted_iota(jnp.int32, (CHUNK, CHUNK), 0)

        def chunk(c, carry):
            r0 = pl.multiple_of(c * CHUNK, CHUNK)
            rows = pl.ds(r0, CHUNK)
            _, _, _, _, el, qe, ke, qE, kd = _hg_chunk_terms(bsc[rows, :], qsc[rows, :], ksc[rows, :])
            qe, ke, qE, kd = qe.astype(BF), ke.astype(BF), qE.astype(BF), kd.astype(BF)
            for h in range(HG_HEADS):
                sl = slice(HG_D * h, HG_D * (h + 1))
                v_h = ub_ref[rows, 1024 + HG_D * h:1024 + HG_D * (h + 1)]
                s_in = st[h]
                ss_ref[c, h] = s_in
                a = jnp.where(tri, _nt(qe[:, sl], ke[:, sl]), 0.0)
                o_ref[rows, sl] = _nn(a.astype(BF), v_h) + _nt(qE[:, sl], s_in.astype(BF))
                st[h] = el[:, sl] * s_in + _tn(v_h, kd[:, sl])
            return carry

        lax.fori_loop(0, cpt, chunk, 0)

        gate = ub_ref[:, 1536:2048].astype(F32)
        for h in range(HG_HEADS):
            sl = slice(HG_D * h, HG_D * (h + 1))
            o = o_ref[:, sl]
            r = lax.rsqrt(jnp.mean(o * o, axis=-1, keepdims=True) + EPS)
            yb_ref[:, sl] = (o * r * gn_ref[:, sl] * _silu(gate[:, sl])).astype(BF)

    return pl.pallas_call(
        body, name="hgrn_fwd", grid=(nt,),
        in_specs=[pl.BlockSpec((TR, N_B), lambda i: (i, 0)), pl.BlockSpec((1, 512), lambda i: (0, 0)),
                  pl.BlockSpec((1, 512), lambda i: (0, 0))],
        out_specs=[pl.BlockSpec((TR, 512), lambda i: (i, 0)), pl.BlockSpec((TR, 512), lambda i: (i, 0)),
                   pl.BlockSpec((cpt, HG_HEADS, HG_D, HG_D), lambda i: (i, 0, 0, 0))],
        out_shape=[jax.ShapeDtypeStruct((Lp, 512), BF), jax.ShapeDtypeStruct((Lp, 512), F32),
                   jax.ShapeDtypeStruct((Lp // CHUNK, HG_HEADS, HG_D, HG_D), F32)],
        scratch_shapes=[pltpu.VMEM((HG_HEADS, HG_D, HG_D), F32), pltpu.VMEM((TR, 512), F32),
                        pltpu.VMEM((TR, 512), F32), pltpu.VMEM((TR, 512), F32)],
        compiler_params=_cp(("arbitrary",)),
    )(ub, lb, gn4)


def _hg_bwd(ub, lb, gn4, o_save, s_save, dyb, carry=None):
    Lp = ub.shape[0]
    nt = Lp // TR
    cpt = TR // CHUNK

    def body(ub_ref, lb_ref, gn_ref, o_ref, ss_ref, dy_ref, du_ref, ds_ref,
             dst, bsc, qsc, ksc, dosc, dqsc, dksc, dbsc):
        i = pl.program_id(0)
        t = nt - 1 - i

        @pl.when(i == 0)
        def _():
            dst[...] = jnp.zeros_like(dst)
            ds_ref[...] = jnp.zeros_like(ds_ref)

        lbv = lb_ref[...]
        row = t * TR + lax.broadcasted_iota(jnp.int32, (TR, 1), 0)
        valid = row >= META_PAD
        q, k, g, sig, f = _hg_gates(ub_ref, lbv, row)
        qsc[...] = _silu(q)
        ksc[...] = k
        bsc[...] = _mm3(_chunk_tri(TR, False), g)

        gate = ub_ref[:, 1536:2048].astype(F32)
        dy = dy_ref[...].astype(F32)
        dgn = jnp.zeros((1, 512), F32)
        for h in range(HG_HEADS):
            sl = slice(HG_D * h, HG_D * (h + 1))
            o = o_ref[:, sl]
            r = lax.rsqrt(jnp.mean(o * o, axis=-1, keepdims=True) + EPS)
            ohat = o * r
            don = dy[:, sl] * _silu(gate[:, sl])
            du_ref[:, 1536 + HG_D * h:1536 + HG_D * (h + 1)] = (
                dy[:, sl] * ohat * gn_ref[:, sl] * _dsilu(gate[:, sl])).astype(BF)
            ds_ref[1:2, sl] += jnp.sum(don * ohat, axis=0, keepdims=True)
            gd = don * gn_ref[:, sl]
            dosc[:, sl] = r * (gd - ohat * jnp.mean(gd * ohat, axis=-1, keepdims=True))

        tri = lax.broadcasted_iota(jnp.int32, (CHUNK, CHUNK), 1) <= lax.broadcasted_iota(jnp.int32, (CHUNK, CHUNK), 0)
        last = lax.broadcasted_iota(jnp.int32, (CHUNK, 1), 0) == CHUNK - 1

        def chunk(cc, carry):
            c = cpt - 1 - cc
            r0 = pl.multiple_of(c * CHUNK, CHUNK)
            rows = pl.ds(r0, CHUNK)
            e1, e2, e0, e3, el, qe, ke, qE, kd = _hg_chunk_terms(bsc[rows, :], qsc[rows, :], ksc[rows, :])
            qe_b, ke_b, qE_b, kd_b = qe.astype(BF), ke.astype(BF), qE.astype(BF), kd.astype(BF)
            do_c = dosc[rows, :].astype(BF)
            for h in range(HG_HEADS):
                sl = slice(HG_D * h, HG_D * (h + 1))
                v_h = ub_ref[rows, 1024 + HG_D * h:1024 + HG_D * (h + 1)]
                do_h = do_c[:, sl]
                s_in = ss_ref[c, h]
                d_s = dst[h]
                d_s_b = d_s.astype(BF)
                a = jnp.where(tri, _nt(qe_b[:, sl], ke_b[:, sl]), 0.0).astype(BF)
                da = jnp.where(tri, _nt(do_h, v_h), 0.0).astype(BF)
                dv = _tn(a, do_h) + _nt(kd_b[:, sl], d_s_b)
                dqE = _nn(do_h, s_in.astype(BF))
                dqe = _nn(da, ke_b[:, sl])
                dke = _tn(da, qe_b[:, sl])
                dkd = _nn(v_h, d_s_b)
                del_h = jnp.sum(s_in * d_s, axis=0, keepdims=True)
                dst[h] = el[:, sl] * d_s + _tn(do_h, qE_b[:, sl])
                dqsc[rows, sl] = dqE * e0[:, sl] + dqe * e1[:, sl]
                dksc[rows, sl] = dke * e2[:, sl] + dkd * e3[:, sl]
                tkd = dkd * kd[:, sl]
                dbl = jnp.sum(tkd, axis=0, keepdims=True) + del_h * el[:, sl]
                dbsc[rows, sl] = dqE * qE[:, sl] + dqe * qe[:, sl] - dke * ke[:, sl] - tkd + jnp.where(last, dbl, 0.0)
                du_ref[rows, 1024 + HG_D * h:1024 + HG_D * (h + 1)] = dv.astype(BF)
            return carry

        lax.fori_loop(0, cpt, chunk, 0)

        dg = _mm3(_chunk_tri(TR, True), dbsc[...])
        df = jnp.where(valid & (f > F_FLOOR), dg / f, 0.0)
        dk = jnp.where(valid, dksc[...], 0.0)
        nsig = _sig(-ub_ref[:, 512:1024].astype(F32))
        dsig = (df - dk) * (1.0 - lbv)
        ds_ref[0:1, :] += jnp.sum(df * (1.0 - sig) - dk * nsig, axis=0, keepdims=True)
        du_ref[:, 512:1024] = (dsig * sig * (1.0 - sig)).astype(BF)
        du_ref[:, 0:512] = (dqsc[...] * _dsilu(q)).astype(BF)

    rev = lambda i: (nt - 1 - i, 0)
    in_specs = [pl.BlockSpec((TR, N_B), rev), pl.BlockSpec((1, 512), lambda i: (0, 0)),
                pl.BlockSpec((1, 512), lambda i: (0, 0)), pl.BlockSpec((TR, 512), rev),
                pl.BlockSpec((cpt, HG_HEADS, HG_D, HG_D), lambda i: (nt - 1 - i, 0, 0, 0)),
                pl.BlockSpec((TR, 512), rev)]
    out_specs = [pl.BlockSpec((TR, N_B), rev), pl.BlockSpec((8, 512), lambda i: (0, 0))]
    out_shape = [jax.ShapeDtypeStruct((Lp, N_B), BF), jax.ShapeDtypeStruct((8, 512), F32)]
    scratch = [pltpu.VMEM((HG_HEADS, HG_D, HG_D), F32)] + [pltpu.VMEM((TR, 512), F32)] * 7
    return _call_carrying(body, "hgrn_bwd", nt, in_specs, out_specs, out_shape, scratch,
                          (ub, lb, gn4, o_save, s_save, dyb), carry)


_KCOL = (2 * 512) // 128
_VCOL = _KCOL + 1


def _swa_in_specs(nt, rev):
    tile = (lambda i: nt - 1 - i) if rev else (lambda i: i)
    hpt = TR // HALO
    return [
        pl.BlockSpec((TR, 512), lambda i: (tile(i), 0)),
        pl.BlockSpec((TR, 512), lambda i: (tile(i), 1)),
        pl.BlockSpec((TR, 128), lambda i: (tile(i), _KCOL)),
        pl.BlockSpec((TR, 128), lambda i: (tile(i), _VCOL)),
        pl.BlockSpec((HALO, 128), lambda i: (jnp.maximum(tile(i) * hpt - 1, 0), _KCOL)),
        pl.BlockSpec((HALO, 128), lambda i: (jnp.maximum(tile(i) * hpt - 1, 0), _VCOL)),
        pl.BlockSpec((CHUNK, 128), lambda i: (0, _KCOL)),
        pl.BlockSpec((CHUNK, 128), lambda i: (0, _VCOL)),
        pl.BlockSpec((1, 512), lambda i: (0, 0)),
        pl.BlockSpec((1, 128), lambda i: (0, 0)),
        pl.BlockSpec((1, ATT_Q_HEADS), lambda i: (0, 0)),
    ]


_WROWS = 2 * CHUNK + HALO + TR
_W0 = 2 * CHUNK
_C0 = _W0 + HALO
_SCALE = ATT_HD ** -0.5


def _group_ones(n):
    r = lax.broadcasted_iota(jnp.int32, (n, n), 0)
    c = lax.broadcasted_iota(jnp.int32, (n, n), 1)
    return jnp.where(jnp.right_shift(r, 6) == jnp.right_shift(c, 6), 1.0, 0.0).astype(BF)


def _group_mean(x, ones):
    hi = x.astype(BF)
    lo = (x - hi.astype(F32)).astype(BF)
    return (_nn(hi, ones) + _nn(lo, ones)) * (1.0 / ATT_HD)


def _head_rms(x, ones):
    r = lax.rsqrt(_group_mean(x * x, ones) + EPS)
    return x * r, r


def _swa_windows(kc_ref, vc_ref, kh_ref, vh_ref, km_ref, vm_ref, kg2, ones, kwin, krwin, vwin, vrwin):
    meta = pl.ds(META_PAD, N_META)
    for (k, v, r0, n) in ((km_ref[meta, :], vm_ref[meta, :], 0, N_META), (kh_ref[...], vh_ref[...], _W0, HALO),
                          (kc_ref[...], vc_ref[...], _C0, TR)):
        xhat, _ = _head_rms(k.astype(F32), ones)
        kn = xhat * kg2
        kwin[pl.ds(r0, n), :] = kn.astype(BF)
        krwin[pl.ds(r0, n), :] = pltpu.roll(kn, ATT_HD, 1).astype(BF)
        vwin[pl.ds(r0, n), :] = v
        if vrwin is not None:
            vrwin[pl.ds(r0, n), :] = pltpu.roll(v.astype(F32), ATT_HD, 1).astype(BF)
    zero = jnp.zeros((_W0 - N_META, 128), BF)
    for w in (kwin, krwin, vwin, vrwin):
        if w is not None:
            w[pl.ds(N_META, _W0 - N_META), :] = zero


def _swa_masks_t(t, qb):
    q0 = t * TR + qb * QB
    qc = jnp.right_shift(q0 + lax.broadcasted_iota(jnp.int32, (1, QB), 1), 6)
    kabs = q0 - HALO + lax.broadcasted_iota(jnp.int32, (QB + HALO, 1), 0)
    kc = jnp.right_shift(kabs + HALO, 6) - HALO // CHUNK
    mask_w = (kc <= qc) & (kc >= qc - 2) & (kabs >= META_PAD)
    return qc > 2, mask_w


def _split_heads(x, lane_hi):
    return jnp.where(lane_hi, 0.0, x).astype(BF), jnp.where(lane_hi, x, 0.0).astype(BF)


def _call_carrying(body, name, nt, in_specs, out_specs, out_shape, scratch, args, carry):
    if carry is None:
        return pl.pallas_call(body, name=name, grid=(nt,), in_specs=in_specs, out_specs=out_specs, out_shape=out_shape,
                              scratch_shapes=scratch, compiler_params=_cp(("arbitrary",)))(*args)
    kind, arrs = carry
    n = len(arrs)
    return pl.pallas_call(
        _carry_exchange(body, len(in_specs), len(out_specs), nt, kind, n), name=name + "_" + kind, grid=(nt,),
        in_specs=in_specs + [_ANY] * n, out_specs=out_specs + [_ANY] * n,
        out_shape=out_shape + _exchange_out_shapes(kind, arrs), scratch_shapes=scratch + _exchange_sems(n),
        compiler_params=_cp(("arbitrary",), has_side_effects=True),
    )(*args, *arrs)


def _swa_fwd(uc, qg8, kg2, sinks, carry=None):
    Lp = uc.shape[0]
    nt = Lp // TR
    nqb = TR // QB

    def body(q_ref, g_ref, kc_ref, vc_ref, kh_ref, vh_ref, km_ref, vm_ref, qg_ref, kg_ref, sk_ref,
             yc_ref, o_ref, lse_ref, kwin, krwin, vwin, vt, qlo, qhi, ot):
        t = pl.program_id(0)
        _swa_windows(kc_ref, vc_ref, kh_ref, vh_ref, km_ref, vm_ref, kg_ref[...], _group_ones(128),
                     kwin, krwin, vwin, None)
        vt[...] = vwin[...].T
        xhat, _ = _head_rms(q_ref[...].astype(F32), _group_ones(512))
        lane_hi = (lax.broadcasted_iota(jnp.int32, (1, 512), 1) & ATT_HD) != 0
        lo, hi = _split_heads(xhat * qg_ref[...] * _SCALE, lane_hi)
        qlo[...] = lo
        qhi[...] = hi
        for qb in range(nqb):
            rows = pl.ds(qb * QB, QB)
            wrows = pl.ds(_W0 + qb * QB, QB + HALO)
            mrows = pl.ds(0, N_META)
            mask_m, mask_w = _swa_masks_t(t, qb)
            for j in range(ATT_Q_HEADS):
                p, e = j // 2, j % 2
                kv = j // ATT_GROUP
                ks = kwin if e == kv else krwin
                qp = (qlo, qhi)[e][rows, 128 * p:128 * (p + 1)]
                s_m = jnp.where(mask_m, _nt(ks[mrows, :], qp), NEG)
                s_w = jnp.where(mask_w, _nt(ks[wrows, :], qp), NEG)
                sink = sk_ref[:, j:j + 1]
                m = jnp.maximum(jnp.maximum(jnp.max(s_m, axis=0, keepdims=True),
                                            jnp.max(s_w, axis=0, keepdims=True)), sink)
                p_m = jnp.exp(s_m - m)
                p_w = jnp.exp(s_w - m)
                den = jnp.sum(p_m, axis=0, keepdims=True) + jnp.sum(p_w, axis=0, keepdims=True) + jnp.exp(sink - m)
                vrows = pl.ds(ATT_HD * kv, ATT_HD)
                ot[pl.ds(ATT_HD * j, ATT_HD), pl.ds(qb * QB, QB)] = (
                    _nn(vt[vrows, pl.ds(0, N_META)], p_m.astype(BF))
                    + _nn(vt[vrows, pl.ds(_W0 + qb * QB, QB + HALO)], p_w.astype(BF))) * (1.0 / den)
                lse_ref[j:j + 1, pl.ds(qb * QB, QB)] = m + jnp.log(den)
        o = ot[...].T
        o_ref[...] = o
        yc_ref[...] = (o * _silu(g_ref[...].astype(F32))).astype(BF)

    win = pltpu.VMEM((_WROWS, 128), BF)
    in_specs = _swa_in_specs(nt, False)
    out_specs = [pl.BlockSpec((TR, 512), lambda i: (i, 0)), pl.BlockSpec((TR, 512), lambda i: (i, 0)),
                 pl.BlockSpec((ATT_Q_HEADS, TR), lambda i: (0, i))]
    out_shape = [jax.ShapeDtypeStruct((Lp, 512), BF), jax.ShapeDtypeStruct((Lp, 512), F32),
                 jax.ShapeDtypeStruct((ATT_Q_HEADS, Lp), F32)]
    scratch = [win, win, win, pltpu.VMEM((128, _WROWS), BF), pltpu.VMEM((TR, 512), BF),
               pltpu.VMEM((TR, 512), BF), pltpu.VMEM((512, TR), F32)]
    return _call_carrying(body, "swa_fwd", nt, in_specs, out_specs, out_shape, scratch,
                          (uc, uc, uc, uc, uc, uc, uc, uc, qg8, kg2, sinks), carry)


def _swa_bwd(uc, qg8, kg2, sinks, o_save, lse, dyc):
    Lp = uc.shape[0]
    nt = Lp // TR
    nqb = TR // QB

    def body(q_ref, g_ref, kc_ref, vc_ref, kh_ref, vh_ref, km_ref, vm_ref, qg_ref, kg_ref, sk_ref,
             o_ref, lse_ref, dy_ref, du_ref, dg_ref, dsk_ref,
             kwin, krwin, vwin, vrwin, kt, krt, qlo, qhi, dolo, dohi, dqt, dk_dir, dk_rol, dv_dir, dv_rol,
             carry_k, carry_v, meta_k, meta_v):
        i = pl.program_id(0)
        t = nt - 1 - i

        @pl.when(i == 0)
        def _():
            carry_k[...] = jnp.zeros_like(carry_k)
            carry_v[...] = jnp.zeros_like(carry_v)
            meta_k[...] = jnp.zeros_like(meta_k)
            meta_v[...] = jnp.zeros_like(meta_v)
            dg_ref[...] = jnp.zeros_like(dg_ref)
            dsk_ref[...] = jnp.zeros_like(dsk_ref)

        ones128 = _group_ones(128)
        ones512 = _group_ones(512)
        _swa_windows(kc_ref, vc_ref, kh_ref, vh_ref, km_ref, vm_ref, kg_ref[...], ones128, kwin, krwin, vwin, vrwin)
        kt[...] = kwin[...].T
        krt[...] = krwin[...].T
        xhat_q, r_q = _head_rms(q_ref[...].astype(F32), ones512)
        lane_hi = (lax.broadcasted_iota(jnp.int32, (1, 512), 1) & ATT_HD) != 0
        lo, hi = _split_heads(xhat_q * qg_ref[...] * _SCALE, lane_hi)
        qlo[...] = lo
        qhi[...] = hi
        gate = g_ref[...].astype(F32)
        dy = dy_ref[...].astype(F32)
        do = dy * _silu(gate)
        o = o_ref[...]
        du_ref[:, 512:1024] = (dy * o * _dsilu(gate)).astype(BF)
        lo, hi = _split_heads(do, lane_hi)
        dolo[...] = lo
        dohi[...] = hi
        hsel = jnp.where(jnp.right_shift(lax.broadcasted_iota(jnp.int32, (ATT_Q_HEADS, 512), 1), 6)
                         == lax.broadcasted_iota(jnp.int32, (ATT_Q_HEADS, 512), 0), 1.0, 0.0).astype(BF)
        prod = do * o
        p_hi = prod.astype(BF)
        d_t = _nt(hsel, p_hi) + _nt(hsel, (prod - p_hi.astype(F32)).astype(BF))
        for acc in (dk_dir, dk_rol, dv_dir, dv_rol):
            acc[...] = jnp.zeros_like(acc)

        for qb in range(nqb):
            rows = pl.ds(qb * QB, QB)
            qcols = pl.ds(qb * QB, QB)
            wrows = pl.ds(_W0 + qb * QB, QB + HALO)
            mrows = pl.ds(0, N_META)
            mask_m, mask_w = _swa_masks_t(t, qb)
            for j in range(ATT_Q_HEADS):
                p, e = j // 2, j % 2
                kv = j // ATT_GROUP
                direct = e == kv
                ks, vs, ktr = (kwin, vwin, kt) if direct else (krwin, vrwin, krt)
                dk_acc, dv_acc = (dk_dir, dv_dir) if direct else (dk_rol, dv_rol)
                pair = slice(128 * p, 128 * (p + 1))
                qp = (qlo, qhi)[e][rows, pair]
                dop = (dolo, dohi)[e][rows, pair]
                lse_j = lse_ref[j:j + 1, qcols]
                d_j = d_t[j:j + 1, qb * QB:(qb + 1) * QB]
                p_m = jnp.exp(jnp.where(mask_m, _nt(ks[mrows, :], qp), NEG) - lse_j)
                p_w = jnp.exp(jnp.where(mask_w, _nt(ks[wrows, :], qp), NEG) - lse_j)
                ds_m = (p_m * (_nt(vs[mrows, :], dop) - d_j)).astype(BF)
                ds_w = (p_w * (_nt(vs[wrows, :], dop) - d_j)).astype(BF)
                hrows = pl.ds(ATT_HD * e, ATT_HD)
                dqt[pl.ds(ATT_HD * j, ATT_HD), qcols] = (_nn(ktr[hrows, pl.ds(0, N_META)], ds_m)
                                                         + _nn(ktr[hrows, pl.ds(_W0 + qb * QB, QB + HALO)], ds_w))
                dk_acc[mrows, :] += _nn(ds_m, qp)
                dk_acc[wrows, :] += _nn(ds_w, qp)
                dv_acc[mrows, :] += _nn(p_m.astype(BF), dop)
                dv_acc[wrows, :] += _nn(p_w.astype(BF), dop)
                dsk_ref[j:j + 1, :] -= jnp.exp(sk_ref[:, j:j + 1] - lse_j) * d_j

        dk_dir[...] += pltpu.roll(dk_rol[...], ATT_HD, 1)
        dv_dir[...] += pltpu.roll(dv_rol[...], ATT_HD, 1)
        meta_k[...] += dk_dir[pl.ds(0, N_META), :]
        meta_v[...] += dv_dir[pl.ds(0, N_META), :]
        first = jnp.where(t == 0, 1.0, 0.0)
        dk_dir[pl.ds(_C0 + TR - HALO, HALO), :] += carry_k[...]
        dv_dir[pl.ds(_C0 + TR - HALO, HALO), :] += carry_v[...]
        dk_dir[pl.ds(_C0 + META_PAD, N_META), :] += first * meta_k[...]
        dv_dir[pl.ds(_C0 + META_PAD, N_META), :] += first * meta_v[...]
        carry_k[...] = dk_dir[pl.ds(_W0, HALO), :]
        carry_v[...] = dv_dir[pl.ds(_W0, HALO), :]

        du_ref[:, 1152:1280] = dv_dir[pl.ds(_C0, TR), :].astype(BF)
        xhat_k, r_k = _head_rms(kc_ref[...].astype(F32), ones128)
        dkn = dk_dir[pl.ds(_C0, TR), :]
        dg_ref[1:2, 0:128] += jnp.sum(dkn * xhat_k, axis=0, keepdims=True)
        gd = dkn * kg_ref[...]
        du_ref[:, 1024:1152] = (r_k * (gd - xhat_k * _group_mean(gd * xhat_k, ones128))).astype(BF)
        dqn = dqt[...].T * _SCALE
        dg_ref[0:1, :] += jnp.sum(dqn * xhat_q, axis=0, keepdims=True)
        gd = dqn * qg_ref[...]
        du_ref[:, 0:512] = (r_q * (gd - xhat_q * _group_mean(gd * xhat_q, ones512))).astype(BF)

    rev = lambda i: (nt - 1 - i, 0)
    specs = _swa_in_specs(nt, True)
    win = pltpu.VMEM((_WROWS, 128), BF)
    wint = pltpu.VMEM((128, _WROWS), BF)
    tile_bf = pltpu.VMEM((TR, 512), BF)
    acc = pltpu.VMEM((_WROWS, 128), F32)
    return pl.pallas_call(
        body, name="swa_bwd", grid=(nt,),
        in_specs=specs + [pl.BlockSpec((TR, 512), rev), pl.BlockSpec((ATT_Q_HEADS, TR), lambda i: (0, nt - 1 - i)),
                          pl.BlockSpec((TR, 512), rev)],
        out_specs=[pl.BlockSpec((TR, N_C), rev), pl.BlockSpec((8, 512), lambda i: (0, 0)),
                   pl.BlockSpec((8, 128), lambda i: (0, 0))],
        out_shape=[jax.ShapeDtypeStruct((Lp, N_C), BF), jax.ShapeDtypeStruct((8, 512), F32),
                   jax.ShapeDtypeStruct((8, 128), F32)],
        scratch_shapes=[win, win, win, win, wint, wint, tile_bf, tile_bf, tile_bf, tile_bf,
                        pltpu.VMEM((512, TR), F32), acc, acc, acc, acc,
                        pltpu.VMEM((HALO, 128), F32), pltpu.VMEM((HALO, 128), F32),
                        pltpu.VMEM((N_META, 128), F32), pltpu.VMEM((N_META, 128), F32)],
        compiler_params=_cp(("arbitrary",)),
    )(uc, uc, uc, uc, uc, uc, uc, uc, qg8, kg2, sinks, o_save, lse, dyc)


def _mix_fwd(h, ya, yb, yc, ug, wa, wb, wc, wo):
    Lp = h.shape[0]
    wspec = lambda r: pl.BlockSpec((r, D_MODEL), lambda i: (0, 0))
    yspec = pl.BlockSpec((TRM, 512), lambda i: (i, 0))
    hspec = pl.BlockSpec((TRM, D_MODEL), lambda i: (i, 0))

    def body(h_ref, ya_ref, yb_ref, yc_ref, ug_ref, wa_ref, wb_ref, wc_ref, wo_ref, hn_ref, za_ref, zb_ref, zc_ref):
        mixed = jnp.zeros((TRM, D_MODEL), F32)
        for n, (y_ref, w_ref, z_ref) in enumerate(((ya_ref, wa_ref, za_ref), (yb_ref, wb_ref, zb_ref),
                                                   (yc_ref, wc_ref, zc_ref))):
            z = _nn(y_ref[...], w_ref[...])
            z_ref[...] = z.astype(BF)
            mixed = mixed + _sig(ug_ref[:, D_MODEL * n:D_MODEL * (n + 1)].astype(F32)) * z
        hn_ref[...] = h_ref[...] + _nn(mixed.astype(BF), wo_ref[...])

    return pl.pallas_call(
        body, name="mix_fwd", grid=(Lp // TRM,),
        in_specs=[hspec, yspec, yspec, yspec, pl.BlockSpec((TRM, N_G), lambda i: (i, 0)),
                  wspec(512), wspec(512), wspec(512), wspec(D_MODEL)],
        out_specs=[hspec, hspec, hspec, hspec],
        out_shape=[jax.ShapeDtypeStruct((Lp, D_MODEL), F32)] + [jax.ShapeDtypeStruct((Lp, D_MODEL), BF)] * 3,
        compiler_params=_cp(("parallel",)),
    )(h, ya, yb, yc, ug, wa, wb, wc, wo)


def _mix_bwd(dh, za, zb, zc, ug, wa, wb, wc, wo):
    Lp = dh.shape[0]
    wspec = lambda r: pl.BlockSpec((r, D_MODEL), lambda i: (0, 0))
    yspec = pl.BlockSpec((TRM, 512), lambda i: (i, 0))
    hspec = pl.BlockSpec((TRM, D_MODEL), lambda i: (i, 0))
    gspec = pl.BlockSpec((TRM, N_G), lambda i: (i, 0))

    def body(dh_ref, za_ref, zb_ref, zc_ref, ug_ref, wa_ref, wb_ref, wc_ref, wo_ref,
             dug_ref, mx_ref, dza_ref, dzb_ref, dzc_ref, dya_ref, dyb_ref, dyc_ref):
        dmix = _nt(dh_ref[...].astype(BF), wo_ref[...])
        mixed = jnp.zeros((TRM, D_MODEL), F32)
        for n, (z_ref, w_ref, dz_ref, dy_ref) in enumerate(((za_ref, wa_ref, dza_ref, dya_ref),
                                                            (zb_ref, wb_ref, dzb_ref, dyb_ref),
                                                            (zc_ref, wc_ref, dzc_ref, dyc_ref))):
            sl = slice(D_MODEL * n, D_MODEL * (n + 1))
            z = z_ref[...].astype(F32)
            gt = _sig(ug_ref[:, sl].astype(F32))
            mixed = mixed + gt * z
            dug_ref[:, sl] = (dmix * z * gt * (1.0 - gt)).astype(BF)
            dz = (dmix * gt).astype(BF)
            dz_ref[...] = dz
            dy_ref[...] = _nt(dz, w_ref[...]).astype(BF)
        mx_ref[...] = mixed.astype(BF)

    bf = lambda n: jax.ShapeDtypeStruct((Lp, n), BF)
    return pl.pallas_call(
        body, name="mix_bwd", grid=(Lp // TRM,),
        in_specs=[hspec, hspec, hspec, hspec, gspec, wspec(512), wspec(512), wspec(512), wspec(D_MODEL)],
        out_specs=[gspec, hspec, hspec, hspec, hspec, yspec, yspec, yspec],
        out_shape=[bf(N_G), bf(D_MODEL), bf(D_MODEL), bf(D_MODEL), bf(D_MODEL), bf(512), bf(512), bf(512)],
        compiler_params=_cp(("parallel",)),
    )(dh, za, zb, zc, ug, wa, wb, wc, wo)


def _inproj_bwd(dus, ws, h, dh, g):
    Lp = h.shape[0]
    widths = [w.shape[1] for w in ws]

    def body(dg_ref, da_ref, db_ref, dc_ref, wg_ref, wa_ref, wb_ref, wc_ref, h_ref, dh_ref, g_ref, o_ref, gg_ref):
        @pl.when(pl.program_id(0) == 0)
        def _():
            gg_ref[...] = jnp.zeros_like(gg_ref)

        dhn = (_nt(dg_ref[...], wg_ref[...]) + _nt(da_ref[...], wa_ref[...])
               + _nt(db_ref[...], wb_ref[...]) + _nt(dc_ref[...], wc_ref[...]))
        x = h_ref[...]
        r = lax.rsqrt(jnp.mean(x * x, axis=-1, keepdims=True) + EPS)
        xhat = x * r
        gg_ref[0:1, :] += jnp.sum(dhn * xhat, axis=0, keepdims=True)
        gd = dhn * g_ref[...]
        o_ref[...] = dh_ref[...] + r * (gd - xhat * jnp.mean(gd * xhat, axis=-1, keepdims=True))

    hspec = pl.BlockSpec((TRM, D_MODEL), lambda i: (i, 0))
    return pl.pallas_call(
        body, name="inproj_bwd", grid=(Lp // TRM,),
        in_specs=[pl.BlockSpec((TRM, n), lambda i: (i, 0)) for n in widths]
        + [pl.BlockSpec((D_MODEL, n), lambda i: (0, 0), pipeline_mode=pl.Buffered(1)) for n in widths]
        + [hspec, hspec, pl.BlockSpec((1, D_MODEL), lambda i: (0, 0))],
        out_specs=[hspec, pl.BlockSpec((8, D_MODEL), lambda i: (0, 0))],
        out_shape=[jax.ShapeDtypeStruct((Lp, D_MODEL), F32), jax.ShapeDtypeStruct((8, D_MODEL), F32)],
        compiler_params=_cp(("arbitrary",)),
    )(*dus, *ws, h, dh, g)


def _loss_head(h, tgt_pad, seq):
    Lp = h.shape[0]
    nt = Lp // TR

    def body(h_ref, t_ref, dh_ref, l_ref):
        i = pl.program_id(0)

        @pl.when(i == 0)
        def _():
            l_ref[...] = jnp.zeros_like(l_ref)

        row = i * TR + lax.broadcasted_iota(jnp.int32, (TR, 1), 0)
        e = jnp.where((row >= CHUNK) & (row < CHUNK + seq), h_ref[...] - t_ref[...], 0.0)
        dh_ref[...] = e * (1.0 / D_MODEL)
        l_ref[...] += (0.5 / D_MODEL) * jnp.sum(jnp.sum(e * e, axis=0, keepdims=True), axis=1, keepdims=True)

    hspec = pl.BlockSpec((TR, D_MODEL), lambda i: (i, 0))
    return pl.pallas_call(
        body, name="loss_head", grid=(nt,), in_specs=[hspec, hspec],
        out_specs=[hspec, pl.BlockSpec((8, 128), lambda i: (0, 0))],
        out_shape=[jax.ShapeDtypeStruct((Lp, D_MODEL), F32), jax.ShapeDtypeStruct((8, 128), F32)],
        compiler_params=_cp(("arbitrary",)),
    )(h, tgt_pad)


def _lb_softmax(lb_ref):
    x = lb_ref[...]
    e = jnp.exp(x - jnp.max(x, axis=0, keepdims=True))
    return e / jnp.sum(e, axis=0, keepdims=True)


def _lb_fwd(hg_lb):
    def body(lb_ref, o_ref):
        sm = _lb_softmax(lb_ref)
        acc = jnp.zeros((1, 512), F32)
        for l in range(DEPTH):
            if l > 0:
                acc = acc + sm[l:l + 1, :]
            o_ref[l:l + 1, :] = jnp.clip(acc, 0.0, 1.0)

    return pl.pallas_call(body, name="lb_fwd", out_shape=jax.ShapeDtypeStruct((DEPTH, 512), F32))(hg_lb)


def _lb_bwd(hg_lb, dlb_all):
    def body(lb_ref, d_ref, o_ref):
        sm = _lb_softmax(lb_ref)
        acc = jnp.zeros((1, 512), F32)
        gm = []
        for l in range(DEPTH):
            if l > 0:
                acc = acc + sm[l:l + 1, :]
            gm.append(jnp.where((acc >= 0.0) & (acc <= 1.0), d_ref[l:l + 1, :], 0.0))
        dsm = [jnp.zeros((1, 512), F32)]
        for j in range(1, DEPTH):
            s = gm[j]
            for l in range(j + 1, DEPTH):
                s = s + gm[l]
            dsm.append(s)
        dot = dsm[0] * sm[0:1, :]
        for j in range(1, DEPTH):
            dot = dot + dsm[j] * sm[j:j + 1, :]
        for j in range(DEPTH):
            o_ref[j:j + 1, :] = sm[j:j + 1, :] * (dsm[j] - dot)

    return pl.pallas_call(body, name="lb_bwd", out_shape=jax.ShapeDtypeStruct((DEPTH, 512), F32))(hg_lb, dlb_all)


_ANY = pl.BlockSpec(memory_space=pl.ANY)


def _chip_peers():
    x, y, c = lax.axis_index("x"), lax.axis_index("y"), lax.axis_index("c")
    return (x, y, c), [(1 - x, y, c), (x, 1 - y, c), (1 - x, 1 - y, c)]


def _exchange(kind, ins, outs, send, recv, loc):
    (x, y, c), peers = _chip_peers()
    me = 2 * x + y
    ds = []
    for a in range(len(ins)):
        if kind == "gather":
            ds.append(pltpu.make_async_copy(ins[a], outs[a].at[me], loc.at[a]))
        else:
            ds.append(pltpu.make_async_copy(ins[a].at[me], outs[a].at[0], loc.at[a]))
        for p, (px, py, pc) in enumerate(peers):
            src, dst = (ins[a], outs[a].at[me]) if kind == "gather" else (ins[a].at[2 * px + py], outs[a].at[1 + p])
            ds.append(pltpu.make_async_remote_copy(src_ref=src, dst_ref=dst, send_sem=send.at[a, p],
                                                   recv_sem=recv.at[a, p], device_id=(px, py, pc), device_id_type=MESH))
    return ds


def _exchange_out_shapes(kind, arrs):
    if kind == "gather":
        return [jax.ShapeDtypeStruct((4,) + a.shape, a.dtype) for a in arrs]
    return [jax.ShapeDtypeStruct(a.shape, a.dtype) for a in arrs]


def _exchange_sems(n):
    return [pltpu.SemaphoreType.DMA((n, 3)), pltpu.SemaphoreType.DMA((n, 3)), pltpu.SemaphoreType.DMA((n,))]


def _exchange_chips(kind, arrs):
    n = len(arrs)

    def body(*refs):
        ds = _exchange(kind, refs[:n], refs[n:2 * n], *refs[2 * n:])
        for d in ds:
            d.start()
        for d in ds:
            d.wait()

    return pl.pallas_call(
        body, name=kind + "_chips", in_specs=[_ANY] * n, out_specs=[_ANY] * n,
        out_shape=_exchange_out_shapes(kind, arrs), scratch_shapes=_exchange_sems(n),
        compiler_params=pltpu.CompilerParams(has_side_effects=True),
    )(*arrs)


def _carry_exchange(body, n_in, n_out, n_steps, kind, n):
    def wrapped(*refs):
        ins, cin = refs[:n_in], refs[n_in:n_in + n]
        outs, cout = refs[n_in + n:n_in + n + n_out], refs[n_in + n + n_out:n_in + 2 * n + n_out]
        scr, sems = refs[n_in + 2 * n + n_out:-3], refs[-3:]
        i = pl.program_id(0)

        @pl.when(i == 0)
        def _():
            for d in _exchange(kind, cin, cout, *sems):
                d.start()

        body(*ins, *outs, *scr)

        @pl.when(i == n_steps - 1)
        def _():
            for d in _exchange(kind, cin, cout, *sems):
                d.wait()

    return wrapped


def _swap_cores(arrs):
    n = len(arrs)

    def body(*refs):
        ins, outs = refs[:n], refs[n:2 * n]
        send, recv = refs[2 * n:]
        x, y, c = lax.axis_index("x"), lax.axis_index("y"), lax.axis_index("c")
        rdmas = []
        for a in range(n):
            r = pltpu.make_async_remote_copy(src_ref=ins[a], dst_ref=outs[a], send_sem=send.at[a], recv_sem=recv.at[a],
                                             device_id=(x, y, 1 - c), device_id_type=MESH)
            r.start()
            rdmas.append(r)
        for r in rdmas:
            r.wait()

    return pl.pallas_call(
        body, name="swap_cores", in_specs=[_ANY] * n, out_specs=[_ANY] * n,
        out_shape=[jax.ShapeDtypeStruct(a.shape, a.dtype) for a in arrs],
        scratch_shapes=[pltpu.SemaphoreType.DMA((n,)), pltpu.SemaphoreType.DMA((n,))],
        compiler_params=pltpu.CompilerParams(has_side_effects=True),
    )(*arrs)


def _allsum_small(p):
    R = p.shape[0]

    def body(p_ref, o_ref, buf, send, recv):
        x, y, c = lax.axis_index("x"), lax.axis_index("y"), lax.axis_index("c")
        me = 4 * x + 2 * y + c
        buf[me] = p_ref[...]
        rdmas = []
        for k in range(1, 8):
            peer = (x ^ (k >> 2), y ^ ((k >> 1) & 1), c ^ (k & 1))
            r = pltpu.make_async_remote_copy(src_ref=p_ref, dst_ref=buf.at[me], send_sem=send.at[k - 1],
                                             recv_sem=recv.at[k - 1], device_id=peer, device_id_type=MESH)
            r.start()
            rdmas.append(r)
        for r in rdmas:
            r.wait()
        acc = buf[0]
        for d in range(1, 8):
            acc = acc + buf[d]
        o_ref[...] = acc

    return pl.pallas_call(
        body, name="allsum_small", out_shape=jax.ShapeDtypeStruct((R, 512), F32),
        in_specs=[pl.BlockSpec(memory_space=pltpu.VMEM)], out_specs=pl.BlockSpec(memory_space=pltpu.VMEM),
        scratch_shapes=[pltpu.VMEM((8, R, 512), F32), pltpu.SemaphoreType.DMA((7,)), pltpu.SemaphoreType.DMA((7,))],
        compiler_params=_cp(has_side_effects=True),
    )(p)


def _sum4(parts, name):
    _, R, C = parts.shape
    tr = 256 if R % 256 == 0 else R

    def body(p_ref, o_ref):
        o_ref[...] = ((p_ref[0] + p_ref[1]) + p_ref[2]) + p_ref[3]

    return pl.pallas_call(
        body, name=name, grid=(R // tr,), in_specs=[pl.BlockSpec((4, tr, C), lambda i: (0, i, 0))],
        out_specs=pl.BlockSpec((tr, C), lambda i: (i, 0)), out_shape=jax.ShapeDtypeStruct((R, C), F32),
        compiler_params=_cp(("parallel",)),
    )(parts)


def _adamw(w, m, v, g0, g1, name):
    R, C = w.shape
    tr = 256 if R % 256 == 0 else R
    two = g1 is not None
    c1 = 1.0 / (1.0 - ADAM_B1 ** ADAM_STEP)
    c2 = 1.0 / (1.0 - ADAM_B2 ** ADAM_STEP)

    def body(*refs):
        if two:
            w_ref, m_ref, v_ref, a_ref, b_ref, g_ref, d_ref, nm_ref, nv_ref = refs
            g = a_ref[...] + b_ref[...]
        else:
            w_ref, m_ref, v_ref, a_ref, g_ref, d_ref, nm_ref, nv_ref = refs
            g = a_ref[...]
        g_ref[...] = g
        m = ADAM_B1 * m_ref[...] + (1.0 - ADAM_B1) * g
        v = ADAM_B2 * v_ref[...] + (1.0 - ADAM_B2) * (g * g)
        nm_ref[...] = m
        nv_ref[...] = v
        d_ref[...] = -ADAM_LR * ((m * c1) / (jnp.sqrt(v * c2) + ADAM_EPS) + ADAM_WD * w_ref[...])

    spec = pl.BlockSpec((tr, C), lambda i: (i, 0))
    n_in = 5 if two else 4
    ins = (w, m, v, g0, g1) if two else (w, m, v, g0)
    return pl.pallas_call(
        body, name=name, grid=(R // tr,), in_specs=[spec] * n_in, out_specs=[spec] * 4,
        out_shape=[jax.ShapeDtypeStruct((R, C), F32)] * 4, compiler_params=_cp(("parallel",)),
    )(*ins)


def _pad8(a):
    r = (-a.shape[0]) % 8
    return a if r == 0 else jnp.pad(a, ((0, r), (0, 0)))


def _local_step(x, tgt, meta, P, shards=None, prep=None, pack=None):
    seq = x.shape[0]
    assert seq % TR == 0
    Lp = seq + TR
    h = jnp.concatenate([jnp.zeros((META_PAD, D_MODEL), F32), meta, x, jnp.zeros((TAIL_PAD, D_MODEL), F32)], axis=0)
    tgt_pad = jnp.pad(tgt, ((CHUNK, TAIL_PAD), (0, 0)))

    P = list(P)
    saved = []
    for l in range(DEPTH):
        p = P[l]
        hn = _rms_fwd(h, p["norm_g"])
        mm = functools.partial(_matmul, out_dtype=BF, tm=TR, tk=D_MODEL, col_major_grid=True)
        ug = mm(hn, p["w_g"], tn=N_G // 2, name="inproj_g")
        ua = mm(hn, p["w_a"], tn=N_A, name="inproj_a")
        ub = mm(hn, p["w_b"], tn=N_B, name="inproj_b")
        uc = mm(hn, p["w_c"], tn=N_C, name="inproj_c")
        ya, yconv = _conv_fwd(ua, p["conv_w"], p["conv_vec"])
        yb, o_hg, s_hg = _hg_fwd(ub, p["lb"], p["gn4"])
        carry = ("gather", shards[l + 1]) if shards is not None and l + 1 < DEPTH else None
        res = _swa_fwd(uc, p["qg"], p["kg"], p["sinks"], carry)
        yc, o_at, lse = res[:3]
        if carry is not None:
            P.append(prep(l + 1, res[3:]))
        h_new, za, zb, zc = _mix_fwd(h, ya, yb, yc, ug, p["w_ao"], p["w_bo"], p["w_co"], p["w_out"])
        saved.append(dict(h=h, hn=hn, ug=ug, ua=ua, ub=ub, uc=uc, ya=ya, yconv=yconv, yb=yb, o_hg=o_hg, s_hg=s_hg,
                          yc=yc, o_at=o_at, lse=lse, za=za, zb=zb, zc=zc))
        h = h_new

    dh, loss8 = _loss_head(h, tgt_pad, seq)

    grads = [None] * DEPTH
    parts = [None] * DEPTH
    pending = None
    for l in reversed(range(DEPTH)):
        p, s = P[l], saved[l]
        dug, mixed, dza, dzb, dzc, dya, dyb, dyc = _mix_bwd(dh, s["za"], s["zb"], s["zc"], s["ug"],
                                                             p["w_ao"], p["w_bo"], p["w_co"], p["w_out"])
        tnmm = functools.partial(_matmul, ta=True, out_dtype=F32, tk=TR)
        g = {}
        g["w_out"] = tnmm(mixed, dh, tm=D_MODEL, tn=D_MODEL, name="dw_out")
        g["w_ao"] = tnmm(s["ya"], dza, tm=512, tn=D_MODEL, name="dw_ao")
        g["w_bo"] = tnmm(s["yb"], dzb, tm=512, tn=D_MODEL, name="dw_bo")
        g["w_co"] = tnmm(s["yc"], dzc, tm=512, tn=D_MODEL, name="dw_co")
        dua, g["conv_w"], g["conv_vec"] = _conv_bwd(s["ua"], s["yconv"], dya, p["conv_w"], p["conv_vec"])
        carry = ("scatter", pending[1]) if pending is not None else None
        res = _hg_bwd(s["ub"], p["lb"], p["gn4"], s["o_hg"], s["s_hg"], dyb, carry)
        dub, g["hg_small"] = res[:2]
        if carry is not None:
            parts[pending[0]] = res[2:]
        duc, g["at_gain"], g["at_sink"] = _swa_bwd(s["uc"], p["qg"], p["kg"], p["sinks"], s["o_at"], s["lse"], dyc)
        g["w_g"] = tnmm(s["hn"], dug, tm=D_MODEL, tn=N_G // 2, name="dw_in_g")
        g["w_a"] = tnmm(s["hn"], dua, tm=D_MODEL, tn=N_A, name="dw_in_a")
        g["w_b"] = tnmm(s["hn"], dub, tm=D_MODEL, tn=N_B, name="dw_in_b")
        g["w_c"] = tnmm(s["hn"], duc, tm=D_MODEL, tn=N_C, name="dw_in_c")
        dh, g["norm_g"] = _inproj_bwd([dug, dua, dub, duc], [p["w_g"], p["w_a"], p["w_b"], p["w_c"]],
                                      s["h"], dh, p["norm_g"])
        grads[l] = g
        if pack is not None:
            pending = (l, pack(g))
    if pending is not None:
        parts[pending[0]] = _exchange_chips("scatter", pending[1])
    return loss8, dh, grads, parts


def _split_w_in(w):
    return dict(w_a=w[:, 0:1536], w_b=w[:, 1536:3584],
                w_c=jnp.concatenate([w[:, 3584:4096], w[:, 4352:4864], w[:, 4096:4352]], axis=1),
                w_g=w[:, 4864:7936])


def _join_w_in(g):
    c = g["w_c"]
    return jnp.concatenate([g["w_a"], g["w_b"], c[:, 0:512], c[:, 1024:1280], c[:, 512:1024], g["w_g"]], axis=1)


def _attn_small(g):
    return (g["at_gain"][0].reshape(ATT_Q_HEADS, ATT_HD).sum(0),
            g["at_gain"][1, 0:128].reshape(ATT_KV_HEADS, ATT_HD).sum(0), g["at_sink"].sum(1))


_SMALL = (("norm_g", 8), ("meta", 32), ("conv_w", 32 * DEPTH), ("conv_b", 8), ("conv_ln_g", 8), ("conv_ln_b", 8),
          ("lb", 8), ("hg_norm_g", 8), ("q_norm_g", 8), ("k_norm_g", 8), ("sinks", 8))


def _small_offsets():
    off, o = {}, 0
    for name, rows in _SMALL:
        off[name] = (o, rows)
        o += rows
    return off, o


def _pack_small(d):
    parts = []
    for name, rows in _SMALL:
        a = d[name]
        parts.append(jnp.pad(a, ((0, rows - a.shape[0]), (0, 512 - a.shape[1]))))
    return jnp.concatenate(parts, axis=0)


def kernel(x, meta_tokens, norm_g, w_in, conv_w, conv_b, conv_ln_g, conv_ln_b, w_conv_out, hg_lower_bounds, hg_norm_g, w_hg_out, q_norm_g, k_norm_g, attn_sinks, w_att_out, w_out, loss_target, m_meta_tokens, m_norm_g, m_w_in, m_conv_w, m_conv_b, m_conv_ln_g, m_conv_ln_b, m_w_conv_out, m_hg_lower_bounds, m_hg_norm_g, m_w_hg_out, m_q_norm_g, m_k_norm_g, m_attn_sinks, m_w_att_out, m_w_out, v_meta_tokens, v_norm_g, v_w_in, v_conv_w, v_conv_b, v_conv_ln_g, v_conv_ln_b, v_w_conv_out, v_hg_lower_bounds, v_hg_norm_g, v_w_hg_out, v_q_norm_g, v_k_norm_g, v_attn_sinks, v_w_att_out, v_w_out):
    xi, yi = lax.axis_index("x"), lax.axis_index("y")
    chip = 2 * xi + yi
    NS = w_in.shape[2]
    CS = conv_w.shape[2]
    MS = meta_tokens.shape[1]

    shards = [[w_in[l].astype(BF), w_conv_out[l].astype(BF), w_hg_out[l].astype(BF), w_att_out[l].astype(BF),
               w_out[l].astype(BF)] for l in range(DEPTH)]
    *first, g_meta, g_convw = _exchange_chips(
        "gather", shards[0] + [meta_tokens, conv_w.reshape(DEPTH * CONV_WIDTH, CS)])
    cols = lambda g: g.transpose(1, 0, 2).reshape(g.shape[1], -1)
    meta_f = cols(g_meta)
    convw_f = cols(g_convw).reshape(DEPTH, CONV_WIDTH, D_CONV)
    lb_all = _lb_fwd(hg_lower_bounds)

    def prep(l, gathered):
        g_win, g_wao, g_wbo, g_wco, g_wout = gathered
        p = _split_w_in(cols(g_win))
        p.update(w_ao=cols(g_wao), w_bo=cols(g_wbo), w_co=cols(g_wco), w_out=g_wout.reshape(D_MODEL, D_MODEL),
                 norm_g=norm_g[l:l + 1], conv_w=convw_f[l],
                 conv_vec=_pad8(jnp.stack([conv_b[l], conv_ln_g[l], conv_ln_b[l]])),
                 lb=lb_all[l:l + 1], gn4=jnp.tile(hg_norm_g[l:l + 1], (1, HG_HEADS)),
                 qg=jnp.tile(q_norm_g[l:l + 1], (1, ATT_Q_HEADS)), kg=jnp.tile(k_norm_g[l:l + 1], (1, ATT_KV_HEADS)),
                 sinks=attn_sinks[l:l + 1])
        return p

    shard_cols = lambda a: a.reshape(a.shape[0], 4, -1).transpose(1, 0, 2)
    pack = lambda g: [shard_cols(_join_w_in(g)), shard_cols(g["w_ao"]), shard_cols(g["w_bo"]), shard_cols(g["w_co"]),
                      g["w_out"].reshape(4, MS, D_MODEL)]

    loss8, dh0, grads, parts = _local_step(x[0], loss_target[0], meta_f, [prep(0, first)], shards, prep, pack)
    seq = x.shape[1]
    grad_x = dh0[CHUNK:CHUNK + seq][None]
    loss = lax.psum(loss8[0, 0], ("x", "y", "c"))

    mine = [jnp.concatenate([_sum4(parts[l][a], name="sum_chips") for l in range(DEPTH)], axis=0) for a in range(5)]
    theirs = _swap_cores(mine)

    dlb_all = jnp.concatenate([grads[l]["hg_small"][0:1] for l in range(DEPTH)], axis=0)
    small = dict(
        norm_g=jnp.concatenate([grads[l]["norm_g"][0:1] for l in range(DEPTH)], axis=0).reshape(8, 512),
        meta=dh0[META_PAD:CHUNK].reshape(32, 512),
        conv_w=jnp.concatenate([grads[l]["conv_w"] for l in range(DEPTH)], axis=0),
        conv_b=jnp.concatenate([grads[l]["conv_vec"][0:1] for l in range(DEPTH)], axis=0),
        conv_ln_g=jnp.concatenate([grads[l]["conv_vec"][1:2] for l in range(DEPTH)], axis=0),
        conv_ln_b=jnp.concatenate([grads[l]["conv_vec"][2:3] for l in range(DEPTH)], axis=0),
        lb=_lb_bwd(hg_lower_bounds, dlb_all),
        hg_norm_g=jnp.concatenate([grads[l]["hg_small"][1:2].reshape(HG_HEADS, HG_D).sum(0, keepdims=True)
                                   for l in range(DEPTH)], axis=0),
        q_norm_g=jnp.stack([_attn_small(grads[l])[0] for l in range(DEPTH)]),
        k_norm_g=jnp.stack([_attn_small(grads[l])[1] for l in range(DEPTH)]),
        sinks=jnp.stack([_attn_small(grads[l])[2] for l in range(DEPTH)]),
    )
    gsum = _allsum_small(_pack_small(small))
    off, _ = _small_offsets()

    def take(name, rows, cols):
        o, _ = off[name]
        return gsum[o:o + rows, 0:cols]

    g_meta_full = take("meta", 32, 512).reshape(N_META, D_MODEL)
    g_convw_full = take("conv_w", 32 * DEPTH, 512).reshape(DEPTH, 32, 512)[:, :CONV_WIDTH]
    small_grads = dict(
        norm_g=take("norm_g", 8, 512),
        meta=lax.dynamic_slice_in_dim(g_meta_full, chip * MS, MS, axis=1),
        conv_w=lax.dynamic_slice_in_dim(g_convw_full, chip * CS, CS, axis=2).reshape(DEPTH * CONV_WIDTH, CS),
        conv_b=take("conv_b", DEPTH, 512), conv_ln_g=take("conv_ln_g", DEPTH, 512), conv_ln_b=take("conv_ln_b", DEPTH, 512),
        lb=take("lb", DEPTH, 512), hg_norm_g=take("hg_norm_g", DEPTH, HG_D), q_norm_g=take("q_norm_g", DEPTH, ATT_HD),
        k_norm_g=take("k_norm_g", DEPTH, ATT_HD), sinks=take("sinks", DEPTH, ATT_Q_HEADS))

    def big_update(w, m, v, a, b, name):
        shp = w.shape
        r2 = lambda t: t.reshape(-1, shp[-1])
        outs = _adamw(r2(w), r2(m), r2(v), a, b, name)
        return [o.reshape(shp) for o in outs]

    res = {}
    res["w_in"] = big_update(w_in, m_w_in, v_w_in, mine[0], theirs[0], "adamw_w_in")
    res["w_conv_out"] = big_update(w_conv_out, m_w_conv_out, v_w_conv_out, mine[1], theirs[1], "adamw_w_ao")
    res["w_hg_out"] = big_update(w_hg_out, m_w_hg_out, v_w_hg_out, mine[2], theirs[2], "adamw_w_bo")
    res["w_att_out"] = big_update(w_att_out, m_w_att_out, v_w_att_out, mine[3], theirs[3], "adamw_w_co")
    res["w_out"] = big_update(w_out, m_w_out, v_w_out, mine[4], theirs[4], "adamw_w_out")

    small_w = dict(meta=(meta_tokens, m_meta_tokens, v_meta_tokens), norm_g=(norm_g, m_norm_g, v_norm_g),
                   conv_w=(conv_w, m_conv_w, v_conv_w), conv_b=(conv_b, m_conv_b, v_conv_b),
                   conv_ln_g=(conv_ln_g, m_conv_ln_g, v_conv_ln_g), conv_ln_b=(conv_ln_b, m_conv_ln_b, v_conv_ln_b),
                   lb=(hg_lower_bounds, m_hg_lower_bounds, v_hg_lower_bounds),
                   hg_norm_g=(hg_norm_g, m_hg_norm_g, v_hg_norm_g), q_norm_g=(q_norm_g, m_q_norm_g, v_q_norm_g),
                   k_norm_g=(k_norm_g, m_k_norm_g, v_k_norm_g), sinks=(attn_sinks, m_attn_sinks, v_attn_sinks))
    view = lambda n, t: t.reshape(-1, 512) if n == "norm_g" else t.reshape(-1, t.shape[-1])
    pw, pm, pv = (_pack_rows([view(n, small_w[n][k]) for n in small_w]) for k in range(3))
    pg = _pack_rows([small_grads[n] for n in small_w])
    packed = _adamw(pw, pm, pv, pg, None, "adamw_small")
    o = 0
    for n in small_w:
        r, cdim = view(n, small_w[n][0]).shape
        res[n] = [t[o:o + r, 0:cdim].reshape(small_w[n][0].shape) for t in packed]
        o += -(-r // 8) * 8

    order = [("meta", None), ("norm_g", None), ("w_in", None), ("conv_w", None), ("conv_b", None), ("conv_ln_g", None),
             ("conv_ln_b", None), ("w_conv_out", None), ("lb", None), ("hg_norm_g", None), ("w_hg_out", None),
             ("q_norm_g", None), ("k_norm_g", None), ("sinks", None), ("w_att_out", None), ("w_out", None)]
    outs = [loss, grad_x]
    for k in range(4):
        outs += [res[n][k] for n, _ in order]
    return tuple(outs)


def _pack_rows(arrs):
    parts = []
    for a in arrs:
        r = (-a.shape[0]) % 8
        parts.append(jnp.pad(a, ((0, r), (0, 512 - a.shape[1]))))
    return jnp.concatenate(parts, axis=0)
```

```python
import functools

import jax
import jax.numpy as jnp
from jax import lax
from jax.experimental import pallas as pl
from jax.experimental.pallas import tpu as pltpu

F32 = jnp.float32
BF = jnp.bfloat16

D_MODEL = 1024
DEPTH = 4
CHUNK = 64
N_META = 16
META_PAD = CHUNK - N_META
D_CONV = 512
CONV_WIDTH = 31
HG_HEADS = 4
HG_D = 128
ATT_Q_HEADS = 8
ATT_KV_HEADS = 2
ATT_HD = 64
ATT_GROUP = ATT_Q_HEADS // ATT_KV_HEADS
EPS = 1e-6
F_FLOOR = 1e-30
NEG = -1e30

ADAM_LR = 0.001
ADAM_B1 = 0.9
ADAM_B2 = 0.999
ADAM_EPS = 1e-08
ADAM_WD = 0.01
ADAM_STEP = 10

TR = 512
TRM = 256
CONV_RB = 32
QB = 128
HALO = 128
TAIL_PAD = TR - CHUNK
VMEM_LIMIT = 56 * 1024 * 1024

N_G, N_A, N_B, N_C = 3 * D_MODEL, 3 * D_CONV, 4 * 512, 2 * 512 + 2 * 128

MESH = pl.DeviceIdType.MESH


def _cp(sem=None, vmem=VMEM_LIMIT, **kw):
    if sem is None:
        return pltpu.CompilerParams(vmem_limit_bytes=vmem, **kw)
    return pltpu.CompilerParams(dimension_semantics=sem, vmem_limit_bytes=vmem, **kw)


def _nn(a, b):
    return lax.dot_general(a, b, (((1,), (0,)), ((), ())), preferred_element_type=F32)


def _nt(a, b):
    return lax.dot_general(a, b, (((1,), (1,)), ((), ())), preferred_element_type=F32)


def _tn(a, b):
    return lax.dot_general(a, b, (((0,), (0,)), ((), ())), preferred_element_type=F32)


def _sig(x):
    return jax.nn.sigmoid(x)


def _silu(x):
    return x * _sig(x)


def _dsilu(x):
    s = _sig(x)
    return s * (1.0 + x * (1.0 - s))


def _split3(x):
    hi = x.astype(BF)
    r1 = x - hi.astype(F32)
    mid = r1.astype(BF)
    lo = (r1 - mid.astype(F32)).astype(BF)
    return hi, mid, lo


def _mm3(t, x):
    hi, mid, lo = _split3(x)
    return _nn(t, hi) + _nn(t, mid) + _nn(t, lo)


def _chunk_tri(n, upper):
    r = lax.broadcasted_iota(jnp.int32, (n, n), 0)
    c = lax.broadcasted_iota(jnp.int32, (n, n), 1)
    same = jnp.right_shift(r, 6) == jnp.right_shift(c, 6)
    tri = (c >= r) if upper else (c <= r)
    return jnp.where(same & tri, 1.0, 0.0).astype(BF)


def _matmul(a, b, *, ta=False, tb=False, out_dtype, tm, tn, tk, name, col_major_grid=False):
    if ta:
        K, M = a.shape
    else:
        M, K = a.shape
    N = b.shape[0] if tb else b.shape[1]
    assert M % tm == 0 and N % tn == 0 and K % tk == 0, (name, M, N, K, tm, tn, tk)
    nk = K // tk
    if col_major_grid:
        grid = (N // tn, M // tm, nk)
        ij = lambda g0, g1: (g1, g0)
    else:
        grid = (M // tm, N // tn, nk)
        ij = lambda g0, g1: (g0, g1)
    if ta:
        a_spec = pl.BlockSpec((tk, tm), lambda g0, g1, k: (k, ij(g0, g1)[0]))
    else:
        a_spec = pl.BlockSpec((tm, tk), lambda g0, g1, k: (ij(g0, g1)[0], k))
    if tb:
        b_spec = pl.BlockSpec((tn, tk), lambda g0, g1, k: (ij(g0, g1)[1], k))
    else:
        b_spec = pl.BlockSpec((tk, tn), lambda g0, g1, k: (k, ij(g0, g1)[1]))
    o_spec = pl.BlockSpec((tm, tn), lambda g0, g1, k: ij(g0, g1))
    dims = (((0 if ta else 1,), (1 if tb else 0,)), ((), ()))
    use_acc = nk > 1 and out_dtype != F32

    def body(a_ref, b_ref, o_ref, *scr):
        k = pl.program_id(2)
        p = lax.dot_general(a_ref[...].astype(BF), b_ref[...].astype(BF), dims, preferred_element_type=F32)
        if nk == 1:
            o_ref[...] = p.astype(out_dtype)
        else:
            acc = scr[0] if use_acc else o_ref

            @pl.when(k == 0)
            def _():
                acc[...] = p

            @pl.when(k > 0)
            def _():
                acc[...] += p

            if use_acc:
                @pl.when(k == nk - 1)
                def _():
                    o_ref[...] = acc[...].astype(out_dtype)

    return pl.pallas_call(
        body, name=name, grid=grid, in_specs=[a_spec, b_spec], out_specs=o_spec,
        out_shape=jax.ShapeDtypeStruct((M, N), out_dtype),
        scratch_shapes=[pltpu.VMEM((tm, tn), F32)] if use_acc else [],
        compiler_params=_cp(("parallel", "parallel", "arbitrary")),
    )(a, b)


def _rms_fwd(h, g):
    Lp = h.shape[0]

    def body(h_ref, g_ref, o_ref):
        x = h_ref[...]
        r = lax.rsqrt(jnp.mean(x * x, axis=-1, keepdims=True) + EPS)
        o_ref[...] = (x * r * g_ref[...]).astype(BF)

    return pl.pallas_call(
        body, name="rms_fwd", grid=(Lp // TR,),
        in_specs=[pl.BlockSpec((TR, D_MODEL), lambda i: (i, 0)), pl.BlockSpec((1, D_MODEL), lambda i: (0, 0))],
        out_specs=pl.BlockSpec((TR, D_MODEL), lambda i: (i, 0)),
        out_shape=jax.ShapeDtypeStruct((Lp, D_MODEL), BF),
        compiler_params=_cp(("parallel",)),
    )(h, g)


def _glu(ua, row):
    a = ua[:, 0:D_CONV].astype(F32)
    gl = ua[:, D_CONV:2 * D_CONV].astype(F32)
    return jnp.where(row >= META_PAD, a * _sig(gl), 0.0)


_SH_ROWS = TR + CHUNK - 8


def _fill_shifts(src, sh):
    for b in range(1, 8):
        sh[b - 1] = src[pl.ds(b, _SH_ROWS), :]


def _shifted(src, sh, start, n):
    b = start % 8
    if b == 0:
        return src[pl.ds(start, n), :]
    return sh[b - 1, pl.ds(start - b, n), :]


def _conv_fwd(ua, cw, cvec):
    Lp = ua.shape[0]
    nt = Lp // TR
    hb = TR // CHUNK

    def body(cur_ref, halo_ref, w_ref, v_ref, ya_ref, yc_ref, ext, sh):
        i = pl.program_id(0)
        row = i * TR + lax.broadcasted_iota(jnp.int32, (TR, 1), 0)
        hrow = i * TR - CHUNK + lax.broadcasted_iota(jnp.int32, (CHUNK, 1), 0)
        ext[pl.ds(0, CHUNK), :] = jnp.where(i > 0, _glu(halo_ref[...], hrow), 0.0)
        ext[pl.ds(CHUNK, TR), :] = _glu(cur_ref[...], row)
        _fill_shifts(ext, sh)
        for rb in range(TR // CONV_RB):
            r0 = rb * CONV_RB
            rows = pl.ds(r0, CONV_RB)
            acc = jnp.zeros((CONV_RB, D_CONV), F32)
            for j in range(CONV_WIDTH):
                acc = acc + _shifted(ext, sh, r0 + CHUNK - (CONV_WIDTH - 1) + j, CONV_RB) * w_ref[j:j + 1, :]
            y = acc + v_ref[0:1, :]
            yc_ref[rows, :] = y
            mu = jnp.mean(y, axis=-1, keepdims=True)
            d = y - mu
            var = jnp.mean(d * d, axis=-1, keepdims=True)
            yn = d * lax.rsqrt(var + EPS) * v_ref[1:2, :] + v_ref[2:3, :]
            ya_ref[rows, :] = (_silu(yn) * _silu(cur_ref[rows, 2 * D_CONV:3 * D_CONV].astype(F32))).astype(BF)

    return pl.pallas_call(
        body, name="conv_fwd", grid=(nt,),
        in_specs=[pl.BlockSpec((TR, N_A), lambda i: (i, 0)),
                  pl.BlockSpec((CHUNK, N_A), lambda i: (jnp.maximum(i * hb - 1, 0), 0)),
                  pl.BlockSpec((CONV_WIDTH, D_CONV), lambda i: (0, 0)),
                  pl.BlockSpec((8, D_CONV), lambda i: (0, 0))],
        out_specs=[pl.BlockSpec((TR, D_CONV), lambda i: (i, 0)), pl.BlockSpec((TR, D_CONV), lambda i: (i, 0))],
        out_shape=[jax.ShapeDtypeStruct((Lp, D_CONV), BF), jax.ShapeDtypeStruct((Lp, D_CONV), F32)],
        scratch_shapes=[pltpu.VMEM((TR + CHUNK, D_CONV), F32), pltpu.VMEM((7, _SH_ROWS, D_CONV), F32)],
        compiler_params=_cp(("parallel",)),
    )(ua, ua, cw, cvec)


def _conv_bwd(ua, yconv, dya, cw, cvec):
    Lp = ua.shape[0]
    nt = Lp // TR
    hb = TR // CHUNK
    nhb = Lp // CHUNK

    def ln_bwd(y, dout, gate, v_ref):
        mu = jnp.mean(y, axis=-1, keepdims=True)
        d = y - mu
        var = jnp.mean(d * d, axis=-1, keepdims=True)
        rstd = lax.rsqrt(var + EPS)
        xhat = d * rstd
        yn = xhat * v_ref[1:2, :] + v_ref[2:3, :]
        dyn = dout * _silu(gate) * _dsilu(yn)
        dxh = dyn * v_ref[1:2, :]
        dyc = rstd * (dxh - jnp.mean(dxh, axis=-1, keepdims=True) - xhat * jnp.mean(dxh * xhat, axis=-1, keepdims=True))
        return dyc, dyn, xhat, yn

    def body(cur_ref, prev_ref, next_ref, yc_ref, ycn_ref, dy_ref, dyn_ref, w_ref, v_ref,
             du_ref, dw_ref, dv_ref, uext, dext, dwacc, ush, dsh):
        i = pl.program_id(0)

        @pl.when(i == 0)
        def _():
            dwacc[...] = jnp.zeros_like(dwacc)
            dv_ref[...] = jnp.zeros_like(dv_ref)

        row = i * TR + lax.broadcasted_iota(jnp.int32, (TR, 1), 0)
        hrow = i * TR - CHUNK + lax.broadcasted_iota(jnp.int32, (CHUNK, 1), 0)
        uext[pl.ds(0, CHUNK), :] = jnp.where(i > 0, _glu(prev_ref[...], hrow), 0.0)
        uext[pl.ds(CHUNK, TR), :] = _glu(cur_ref[...], row)

        s_b = jnp.zeros((1, D_CONV), F32)
        s_g = jnp.zeros((1, D_CONV), F32)
        s_bb = jnp.zeros((1, D_CONV), F32)
        for rb in range(TR // CONV_RB):
            rows = pl.ds(rb * CONV_RB, CONV_RB)
            gate = cur_ref[rows, 2 * D_CONV:3 * D_CONV].astype(F32)
            dout = dy_ref[rows, :].astype(F32)
            dyc, dyn, xhat, yn = ln_bwd(yc_ref[rows, :], dout, gate, v_ref)
            du_ref[rows, 2 * D_CONV:3 * D_CONV] = (dout * _silu(yn) * _dsilu(gate)).astype(BF)
            dext[rows, :] = dyc
            s_b = s_b + jnp.sum(dyc, axis=0, keepdims=True)
            s_g = s_g + jnp.sum(dyn * xhat, axis=0, keepdims=True)
            s_bb = s_bb + jnp.sum(dyn, axis=0, keepdims=True)
        dv_ref[0:1, :] += s_b
        dv_ref[1:2, :] += s_g
        dv_ref[2:3, :] += s_bb
        dyc_n, _, _, _ = ln_bwd(ycn_ref[...], dyn_ref[...].astype(F32),
                                next_ref[:, 2 * D_CONV:3 * D_CONV].astype(F32), v_ref)
        dext[pl.ds(TR, CHUNK), :] = jnp.where(i < nt - 1, dyc_n, 0.0)
        _fill_shifts(uext, ush)
        _fill_shifts(dext, dsh)

        for rb in range(TR // CONV_RB):
            r0 = rb * CONV_RB
            rows = pl.ds(r0, CONV_RB)
            d_blk = dext[rows, :]
            dglu = jnp.zeros((CONV_RB, D_CONV), F32)
            for j in range(CONV_WIDTH):
                dglu = dglu + _shifted(dext, dsh, r0 + CONV_WIDTH - 1 - j, CONV_RB) * w_ref[j:j + 1, :]
                prod = d_blk * _shifted(uext, ush, r0 + CHUNK - (CONV_WIDTH - 1) + j, CONV_RB)
                part = prod[0:8, :]
                for s in range(1, CONV_RB // 8):
                    part = part + prod[8 * s:8 * s + 8, :]
                dwacc[j] += part
            a = cur_ref[rows, 0:D_CONV].astype(F32)
            sg = _sig(cur_ref[rows, D_CONV:2 * D_CONV].astype(F32))
            grow = i * TR + r0 + lax.broadcasted_iota(jnp.int32, (CONV_RB, 1), 0)
            dglu = jnp.where(grow >= META_PAD, dglu, 0.0)
            du_ref[rows, 0:D_CONV] = (dglu * sg).astype(BF)
            du_ref[rows, D_CONV:2 * D_CONV] = (dglu * a * sg * (1.0 - sg)).astype(BF)

        @pl.when(i == nt - 1)
        def _():
            dw_ref[...] = jnp.sum(dwacc[...], axis=1)

    nxt = lambda i: (jnp.minimum(i * hb + hb, nhb - 1), 0)
    return pl.pallas_call(
        body, name="conv_bwd", grid=(nt,),
        in_specs=[pl.BlockSpec((TR, N_A), lambda i: (i, 0)),
                  pl.BlockSpec((CHUNK, N_A), lambda i: (jnp.maximum(i * hb - 1, 0), 0)),
                  pl.BlockSpec((CHUNK, N_A), nxt),
                  pl.BlockSpec((TR, D_CONV), lambda i: (i, 0)),
                  pl.BlockSpec((CHUNK, D_CONV), nxt),
                  pl.BlockSpec((TR, D_CONV), lambda i: (i, 0)),
                  pl.BlockSpec((CHUNK, D_CONV), nxt),
                  pl.BlockSpec((CONV_WIDTH, D_CONV), lambda i: (0, 0)),
                  pl.BlockSpec((8, D_CONV), lambda i: (0, 0))],
        out_specs=[pl.BlockSpec((TR, N_A), lambda i: (i, 0)),
                   pl.BlockSpec((32, D_CONV), lambda i: (0, 0)),
                   pl.BlockSpec((8, D_CONV), lambda i: (0, 0))],
        out_shape=[jax.ShapeDtypeStruct((Lp, N_A), BF), jax.ShapeDtypeStruct((32, D_CONV), F32),
                   jax.ShapeDtypeStruct((8, D_CONV), F32)],
        scratch_shapes=[pltpu.VMEM((TR + CHUNK, D_CONV), F32), pltpu.VMEM((TR + CHUNK, D_CONV), F32),
                        pltpu.VMEM((32, 8, D_CONV), F32), pltpu.VMEM((7, _SH_ROWS, D_CONV), F32),
                        pltpu.VMEM((7, _SH_ROWS, D_CONV), F32)],
        compiler_params=_cp(("arbitrary",)),
    )(ua, ua, ua, yconv, yconv, dya, dya, cw, cvec)


def _hg_gates(ub_ref, lbv, row):
    q = ub_ref[:, 0:512].astype(F32)
    z = ub_ref[:, 512:1024].astype(F32)
    valid = row >= META_PAD
    sig = _sig(z)
    f = lbv + (1.0 - lbv) * sig
    g = jnp.where(valid, jnp.log(jnp.maximum(f, F_FLOOR)), 0.0)
    k = jnp.where(valid, (1.0 - lbv) * _sig(-z), 0.0)
    return q, k, g, sig, f


def _hg_chunk_terms(b_c, q_c, k_c):
    bm = b_c[CHUNK // 2 - 1:CHUNK // 2, :]
    bl = b_c[CHUNK - 1:CHUNK, :]
    e1 = jnp.exp(b_c - bm)
    e2 = jnp.exp(bm - b_c)
    e0 = jnp.exp(b_c)
    e3 = jnp.exp(bl - b_c)
    el = jnp.exp(bl)
    return e1, e2, e0, e3, el, q_c * e1, k_c * e2, q_c * e0, k_c * e3


def _hg_fwd(ub, lb, gn4):
    Lp = ub.shape[0]
    nt = Lp // TR
    cpt = TR // CHUNK

    def body(ub_ref, lb_ref, gn_ref, yb_ref, o_ref, ss_ref, st, bsc, qsc, ksc):
        i = pl.program_id(0)

        @pl.when(i == 0)
        def _():
            st[...] = jnp.zeros_like(st)

        row = i * TR + lax.broadcasted_iota(jnp.int32, (TR, 1), 0)
        q, k, g, _, _ = _hg_gates(ub_ref, lb_ref[...], row)
        qsc[...] = _silu(q)
        ksc[...] = k
        bsc[...] = _mm3(_chunk_tri(TR, False), g)
        tri = lax.broadcasted_iota(jnp.int32, (CHUNK, CHUNK), 1) <= lax.broadcasted_iota(jnp.int32, (CHUNK, CHUNK), 0)

        def chunk(c, carry):
            r0 = pl.multiple_of(c * CHUNK, CHUNK)
            rows = pl.ds(r0, CHUNK)
            _, _, _, _, el, qe, ke, qE, kd = _hg_chunk_terms(bsc[rows, :], qsc[rows, :], ksc[rows, :])
            qe, ke, qE, kd = qe.astype(BF), ke.astype(BF), qE.astype(BF), kd.astype(BF)
            for h in range(HG_HEADS):
                sl = slice(HG_D * h, HG_D * (h + 1))
                v_h = ub_ref[rows, 1024 + HG_D * h:1024 + HG_D * (h + 1)]
                s_in = st[h]
                ss_ref[c, h] = s_in
                a = jnp.where(tri, _nt(qe[:, sl], ke[:, sl]), 0.0)
                o_ref[rows, sl] = _nn(a.astype(BF), v_h) + _nt(qE[:, sl], s_in.astype(BF))
                st[h] = el[:, sl] * s_in + _tn(v_h, kd[:, sl])
            return carry

        lax.fori_loop(0, cpt, chunk, 0, unroll=2)

        gate = ub_ref[:, 1536:2048].astype(F32)
        for h in range(HG_HEADS):
            sl = slice(HG_D * h, HG_D * (h + 1))
            o = o_ref[:, sl]
            r = lax.rsqrt(jnp.mean(o * o, axis=-1, keepdims=True) + EPS)
            yb_ref[:, sl] = (o * r * gn_ref[:, sl] * _silu(gate[:, sl])).astype(BF)

    return pl.pallas_call(
        body, name="hgrn_fwd", grid=(nt,),
        in_specs=[pl.BlockSpec((TR, N_B), lambda i: (i, 0)), pl.BlockSpec((1, 512), lambda i: (0, 0)),
                  pl.BlockSpec((1, 512), lambda i: (0, 0))],
        out_specs=[pl.BlockSpec((TR, 512), lambda i: (i, 0)), pl.BlockSpec((TR, 512), lambda i: (i, 0)),
                   pl.BlockSpec((cpt, HG_HEADS, HG_D, HG_D), lambda i: (i, 0, 0, 0))],
        out_shape=[jax.ShapeDtypeStruct((Lp, 512), BF), jax.ShapeDtypeStruct((Lp, 512), F32),
                   jax.ShapeDtypeStruct((Lp // CHUNK, HG_HEADS, HG_D, HG_D), F32)],
        scratch_shapes=[pltpu.VMEM((HG_HEADS, HG_D, HG_D), F32), pltpu.VMEM((TR, 512), F32),
                        pltpu.VMEM((TR, 512), F32), pltpu.VMEM((TR, 512), F32)],
        compiler_params=_cp(("arbitrary",)),
    )(ub, lb, gn4)


def _hg_bwd(ub, lb, gn4, o_save, s_save, dyb, carry=None):
    Lp = ub.shape[0]
    nt = Lp // TR
    cpt = TR // CHUNK

    def body(ub_ref, lb_ref, gn_ref, o_ref, ss_ref, dy_ref, du_ref, ds_ref,
             dst, bsc, qsc, ksc, dosc, dqsc, dksc, dbsc):
        i = pl.program_id(0)
        t = nt - 1 - i

        @pl.when(i == 0)
        def _():
            dst[...] = jnp.zeros_like(dst)
            ds_ref[...] = jnp.zeros_like(ds_ref)

        lbv = lb_ref[...]
        row = t * TR + lax.broadcasted_iota(jnp.int32, (TR, 1), 0)
        valid = row >= META_PAD
        q, k, g, sig, f = _hg_gates(ub_ref, lbv, row)
        qsc[...] = _silu(q)
        ksc[...] = k
        bsc[...] = _mm3(_chunk_tri(TR, False), g)

        gate = ub_ref[:, 1536:2048].astype(F32)
        dy = dy_ref[...].astype(F32)
        dgn = jnp.zeros((1, 512), F32)
        for h in range(HG_HEADS):
            sl = slice(HG_D * h, HG_D * (h + 1))
            o = o_ref[:, sl]
            r = lax.rsqrt(jnp.mean(o * o, axis=-1, keepdims=True) + EPS)
            ohat = o * r
            don = dy[:, sl] * _silu(gate[:, sl])
            du_ref[:, 1536 + HG_D * h:1536 + HG_D * (h + 1)] = (
                dy[:, sl] * ohat * gn_ref[:, sl] * _dsilu(gate[:, sl])).astype(BF)
            ds_ref[1:2, sl] += jnp.sum(don * ohat, axis=0, keepdims=True)
            gd = don * gn_ref[:, sl]
            dosc[:, sl] = r * (gd - ohat * jnp.mean(gd * ohat, axis=-1, keepdims=True))

        tri = lax.broadcasted_iota(jnp.int32, (CHUNK, CHUNK), 1) <= lax.broadcasted_iota(jnp.int32, (CHUNK, CHUNK), 0)
        last = lax.broadcasted_iota(jnp.int32, (CHUNK, 1), 0) == CHUNK - 1

        def chunk(cc, carry):
            c = cpt - 1 - cc
            r0 = pl.multiple_of(c * CHUNK, CHUNK)
            rows = pl.ds(r0, CHUNK)
            e1, e2, e0, e3, el, qe, ke, qE, kd = _hg_chunk_terms(bsc[rows, :], qsc[rows, :], ksc[rows, :])
            qe_b, ke_b, qE_b, kd_b = qe.astype(BF), ke.astype(BF), qE.astype(BF), kd.astype(BF)
            do_c = dosc[rows, :].astype(BF)
            for h in range(HG_HEADS):
                sl = slice(HG_D * h, HG_D * (h + 1))
                v_h = ub_ref[rows, 1024 + HG_D * h:1024 + HG_D * (h + 1)]
                do_h = do_c[:, sl]
                s_in = ss_ref[c, h]
                d_s = dst[h]
                d_s_b = d_s.astype(BF)
                a = jnp.where(tri, _nt(qe_b[:, sl], ke_b[:, sl]), 0.0).astype(BF)
                da = jnp.where(tri, _nt(do_h, v_h), 0.0).astype(BF)
                dv = _tn(a, do_h) + _nt(kd_b[:, sl], d_s_b)
                dqE = _nn(do_h, s_in.astype(BF))
                dqe = _nn(da, ke_b[:, sl])
                dke = _tn(da, qe_b[:, sl])
                dkd = _nn(v_h, d_s_b)
                del_h = jnp.sum(s_in * d_s, axis=0, keepdims=True)
                dst[h] = el[:, sl] * d_s + _tn(do_h, qE_b[:, sl])
                dqsc[rows, sl] = dqE * e0[:, sl] + dqe * e1[:, sl]
                dksc[rows, sl] = dke * e2[:, sl] + dkd * e3[:, sl]
                tkd = dkd * kd[:, sl]
                dbl = jnp.sum(tkd, axis=0, keepdims=True) + del_h * el[:, sl]
                dbsc[rows, sl] = dqE * qE[:, sl] + dqe * qe[:, sl] - dke * ke[:, sl] - tkd + jnp.where(last, dbl, 0.0)
                du_ref[rows, 1024 + HG_D * h:1024 + HG_D * (h + 1)] = dv.astype(BF)
            return carry

        lax.fori_loop(0, cpt, chunk, 0, unroll=2)

        dg = _mm3(_chunk_tri(TR, True), dbsc[...])
        df = jnp.where(valid & (f > F_FLOOR), dg / f, 0.0)
        dk = jnp.where(valid, dksc[...], 0.0)
        nsig = _sig(-ub_ref[:, 512:1024].astype(F32))
        dsig = (df - dk) * (1.0 - lbv)
        ds_ref[0:1, :] += jnp.sum(df * (1.0 - sig) - dk * nsig, axis=0, keepdims=True)
        du_ref[:, 512:1024] = (dsig * sig * (1.0 - sig)).astype(BF)
        du_ref[:, 0:512] = (dqsc[...] * _dsilu(q)).astype(BF)

    rev = lambda i: (nt - 1 - i, 0)
    in_specs = [pl.BlockSpec((TR, N_B), rev), pl.BlockSpec((1, 512), lambda i: (0, 0)),
                pl.BlockSpec((1, 512), lambda i: (0, 0)), pl.BlockSpec((TR, 512), rev),
                pl.BlockSpec((cpt, HG_HEADS, HG_D, HG_D), lambda i: (nt - 1 - i, 0, 0, 0)),
                pl.BlockSpec((TR, 512), rev)]
    out_specs = [pl.BlockSpec((TR, N_B), rev), pl.BlockSpec((8, 512), lambda i: (0, 0))]
    out_shape = [jax.ShapeDtypeStruct((Lp, N_B), BF), jax.ShapeDtypeStruct((8, 512), F32)]
    scratch = [pltpu.VMEM((HG_HEADS, HG_D, HG_D), F32)] + [pltpu.VMEM((TR, 512), F32)] * 7
    return _call_carrying(body, "hgrn_bwd", nt, in_specs, out_specs, out_shape, scratch,
                          (ub, lb, gn4, o_save, s_save, dyb), carry)


_KCOL = (2 * 512) // 128
_VCOL = _KCOL + 1


def _swa_in_specs(nt, rev):
    tile = (lambda i: nt - 1 - i) if rev else (lambda i: i)
    hpt = TR // HALO
    return [
        pl.BlockSpec((TR, 512), lambda i: (tile(i), 0)),
        pl.BlockSpec((TR, 512), lambda i: (tile(i), 1)),
        pl.BlockSpec((TR, 128), lambda i: (tile(i), _KCOL)),
        pl.BlockSpec((TR, 128), lambda i: (tile(i), _VCOL)),
        pl.BlockSpec((HALO, 128), lambda i: (jnp.maximum(tile(i) * hpt - 1, 0), _KCOL)),
        pl.BlockSpec((HALO, 128), lambda i: (jnp.maximum(tile(i) * hpt - 1, 0), _VCOL)),
        pl.BlockSpec((CHUNK, 128), lambda i: (0, _KCOL)),
        pl.BlockSpec((CHUNK, 128), lambda i: (0, _VCOL)),
        pl.BlockSpec((1, 512), lambda i: (0, 0)),
        pl.BlockSpec((1, 128), lambda i: (0, 0)),
        pl.BlockSpec((1, ATT_Q_HEADS), lambda i: (0, 0)),
    ]


_WROWS = 2 * CHUNK + HALO + TR
_W0 = 2 * CHUNK
_C0 = _W0 + HALO
_SCALE = ATT_HD ** -0.5


def _group_ones(n):
    r = lax.broadcasted_iota(jnp.int32, (n, n), 0)
    c = lax.broadcasted_iota(jnp.int32, (n, n), 1)
    return jnp.where(jnp.right_shift(r, 6) == jnp.right_shift(c, 6), 1.0, 0.0).astype(BF)


def _group_mean(x, ones):
    hi = x.astype(BF)
    lo = (x - hi.astype(F32)).astype(BF)
    return (_nn(hi, ones) + _nn(lo, ones)) * (1.0 / ATT_HD)


def _head_rms(x, ones):
    r = lax.rsqrt(_group_mean(x * x, ones) + EPS)
    return x * r, r


def _swa_windows(kc_ref, vc_ref, kh_ref, vh_ref, km_ref, vm_ref, kg2, ones, kwin, krwin, vwin, vrwin):
    meta = pl.ds(META_PAD, N_META)
    for (k, v, r0, n) in ((km_ref[meta, :], vm_ref[meta, :], 0, N_META), (kh_ref[...], vh_ref[...], _W0, HALO),
                          (kc_ref[...], vc_ref[...], _C0, TR)):
        xhat, _ = _head_rms(k.astype(F32), ones)
        kn = xhat * kg2
        kwin[pl.ds(r0, n), :] = kn.astype(BF)
        krwin[pl.ds(r0, n), :] = pltpu.roll(kn, ATT_HD, 1).astype(BF)
        vwin[pl.ds(r0, n), :] = v
        if vrwin is not None:
            vrwin[pl.ds(r0, n), :] = pltpu.roll(v.astype(F32), ATT_HD, 1).astype(BF)
    zero = jnp.zeros((_W0 - N_META, 128), BF)
    for w in (kwin, krwin, vwin, vrwin):
        if w is not None:
            w[pl.ds(N_META, _W0 - N_META), :] = zero


def _swa_masks_t(t, qb):
    q0 = t * TR + qb * QB
    qc = jnp.right_shift(q0 + lax.broadcasted_iota(jnp.int32, (1, QB), 1), 6)
    kabs = q0 - HALO + lax.broadcasted_iota(jnp.int32, (QB + HALO, 1), 0)
    kc = jnp.right_shift(kabs + HALO, 6) - HALO // CHUNK
    mask_w = (kc <= qc) & (kc >= qc - 2) & (kabs >= META_PAD)
    return qc > 2, mask_w


def _split_heads(x, lane_hi):
    return jnp.where(lane_hi, 0.0, x).astype(BF), jnp.where(lane_hi, x, 0.0).astype(BF)


def _call_carrying(body, name, nt, in_specs, out_specs, out_shape, scratch, args, carry):
    if carry is None:
        return pl.pallas_call(body, name=name, grid=(nt,), in_specs=in_specs, out_specs=out_specs, out_shape=out_shape,
                              scratch_shapes=scratch, compiler_params=_cp(("arbitrary",)))(*args)
    kind, arrs = carry
    n = len(arrs)
    return pl.pallas_call(
        _carry_exchange(body, len(in_specs), len(out_specs), nt, kind, n), name=name + "_" + kind, grid=(nt,),
        in_specs=in_specs + [_ANY] * n, out_specs=out_specs + [_ANY] * n,
        out_shape=out_shape + _exchange_out_shapes(kind, arrs), scratch_shapes=scratch + _exchange_sems(n),
        compiler_params=_cp(("arbitrary",), has_side_effects=True),
    )(*args, *arrs)


def _swa_fwd(uc, qg8, kg2, sinks, carry=None):
    Lp = uc.shape[0]
    nt = Lp // TR
    nqb = TR // QB

    def body(q_ref, g_ref, kc_ref, vc_ref, kh_ref, vh_ref, km_ref, vm_ref, qg_ref, kg_ref, sk_ref,
             yc_ref, o_ref, lse_ref, kwin, krwin, vwin, vt, qlo, qhi, ot):
        t = pl.program_id(0)
        _swa_windows(kc_ref, vc_ref, kh_ref, vh_ref, km_ref, vm_ref, kg_ref[...], _group_ones(128),
                     kwin, krwin, vwin, None)
        vt[...] = vwin[...].T
        xhat, _ = _head_rms(q_ref[...].astype(F32), _group_ones(512))
        lane_hi = (lax.broadcasted_iota(jnp.int32, (1, 512), 1) & ATT_HD) != 0
        lo, hi = _split_heads(xhat * qg_ref[...] * _SCALE, lane_hi)
        qlo[...] = lo
        qhi[...] = hi
        for qb in range(nqb):
            rows = pl.ds(qb * QB, QB)
            wrows = pl.ds(_W0 + qb * QB, QB + HALO)
            mrows = pl.ds(0, N_META)
            mask_m, mask_w = _swa_masks_t(t, qb)
            for j in range(ATT_Q_HEADS):
                p, e = j // 2, j % 2
                kv = j // ATT_GROUP
                ks = kwin if e == kv else krwin
                qp = (qlo, qhi)[e][rows, 128 * p:128 * (p + 1)]
                s_m = jnp.where(mask_m, _nt(ks[mrows, :], qp), NEG)
                s_w = jnp.where(mask_w, _nt(ks[wrows, :], qp), NEG)
                sink = sk_ref[:, j:j + 1]
                m = jnp.maximum(jnp.maximum(jnp.max(s_m, axis=0, keepdims=True),
                                            jnp.max(s_w, axis=0, keepdims=True)), sink)
                p_m = jnp.exp(s_m - m)
                p_w = jnp.exp(s_w - m)
                den = jnp.sum(p_m, axis=0, keepdims=True) + jnp.sum(p_w, axis=0, keepdims=True) + jnp.exp(sink - m)
                vrows = pl.ds(ATT_HD * kv, ATT_HD)
                ot[pl.ds(ATT_HD * j, ATT_HD), pl.ds(qb * QB, QB)] = (
                    _nn(vt[vrows, pl.ds(0, N_META)], p_m.astype(BF))
                    + _nn(vt[vrows, pl.ds(_W0 + qb * QB, QB + HALO)], p_w.astype(BF))) * (1.0 / den)
                lse_ref[j:j + 1, pl.ds(qb * QB, QB)] = m + jnp.log(den)
        o = ot[...].T
        o_ref[...] = o
        yc_ref[...] = (o * _silu(g_ref[...].astype(F32))).astype(BF)

    win = pltpu.VMEM((_WROWS, 128), BF)
    in_specs = _swa_in_specs(nt, False)
    out_specs = [pl.BlockSpec((TR, 512), lambda i: (i, 0)), pl.BlockSpec((TR, 512), lambda i: (i, 0)),
                 pl.BlockSpec((ATT_Q_HEADS, TR), lambda i: (0, i))]
    out_shape = [jax.ShapeDtypeStruct((Lp, 512), BF), jax.ShapeDtypeStruct((Lp, 512), F32),
                 jax.ShapeDtypeStruct((ATT_Q_HEADS, Lp), F32)]
    scratch = [win, win, win, pltpu.VMEM((128, _WROWS), BF), pltpu.VMEM((TR, 512), BF),
               pltpu.VMEM((TR, 512), BF), pltpu.VMEM((512, TR), F32)]
    return _call_carrying(body, "swa_fwd", nt, in_specs, out_specs, out_shape, scratch,
                          (uc, uc, uc, uc, uc, uc, uc, uc, qg8, kg2, sinks), carry)


def _swa_bwd(uc, qg8, kg2, sinks, o_save, lse, dyc):
    Lp = uc.shape[0]
    nt = Lp // TR
    nqb = TR // QB

    def body(q_ref, g_ref, kc_ref, vc_ref, kh_ref, vh_ref, km_ref, vm_ref, qg_ref, kg_ref, sk_ref,
             o_ref, lse_ref, dy_ref, du_ref, dg_ref, dsk_ref,
             kwin, krwin, vwin, vrwin, kt, krt, qlo, qhi, dolo, dohi, dqt, dk_dir, dk_rol, dv_dir, dv_rol,
             carry_k, carry_v, meta_k, meta_v):
        i = pl.program_id(0)
        t = nt - 1 - i

        @pl.when(i == 0)
        def _():
            carry_k[...] = jnp.zeros_like(carry_k)
            carry_v[...] = jnp.zeros_like(carry_v)
            meta_k[...] = jnp.zeros_like(meta_k)
            meta_v[...] = jnp.zeros_like(meta_v)
            dg_ref[...] = jnp.zeros_like(dg_ref)
            dsk_ref[...] = jnp.zeros_like(dsk_ref)

        ones128 = _group_ones(128)
        ones512 = _group_ones(512)
        _swa_windows(kc_ref, vc_ref, kh_ref, vh_ref, km_ref, vm_ref, kg_ref[...], ones128, kwin, krwin, vwin, vrwin)
        kt[...] = kwin[...].T
        krt[...] = krwin[...].T
        xhat_q, r_q = _head_rms(q_ref[...].astype(F32), ones512)
        lane_hi = (lax.broadcasted_iota(jnp.int32, (1, 512), 1) & ATT_HD) != 0
        lo, hi = _split_heads(xhat_q * qg_ref[...] * _SCALE, lane_hi)
        qlo[...] = lo
        qhi[...] = hi
        gate = g_ref[...].astype(F32)
        dy = dy_ref[...].astype(F32)
        do = dy * _silu(gate)
        o = o_ref[...]
        du_ref[:, 512:1024] = (dy * o * _dsilu(gate)).astype(BF)
        lo, hi = _split_heads(do, lane_hi)
        dolo[...] = lo
        dohi[...] = hi
        hsel = jnp.where(jnp.right_shift(lax.broadcasted_iota(jnp.int32, (ATT_Q_HEADS, 512), 1), 6)
                         == lax.broadcasted_iota(jnp.int32, (ATT_Q_HEADS, 512), 0), 1.0, 0.0).astype(BF)
        prod = do * o
        p_hi = prod.astype(BF)
        d_t = _nt(hsel, p_hi) + _nt(hsel, (prod - p_hi.astype(F32)).astype(BF))
        for acc in (dk_dir, dk_rol, dv_dir, dv_rol):
            acc[...] = jnp.zeros_like(acc)

        for qb in range(nqb):
            rows = pl.ds(qb * QB, QB)
            qcols = pl.ds(qb * QB, QB)
            wrows = pl.ds(_W0 + qb * QB, QB + HALO)
            mrows = pl.ds(0, N_META)
            mask_m, mask_w = _swa_masks_t(t, qb)
            for j in range(ATT_Q_HEADS):
                p, e = j // 2, j % 2
                kv = j // ATT_GROUP
                direct = e == kv
                ks, vs, ktr = (kwin, vwin, kt) if direct else (krwin, vrwin, krt)
                dk_acc, dv_acc = (dk_dir, dv_dir) if direct else (dk_rol, dv_rol)
                pair = slice(128 * p, 128 * (p + 1))
                qp = (qlo, qhi)[e][rows, pair]
                dop = (dolo, dohi)[e][rows, pair]
                lse_j = lse_ref[j:j + 1, qcols]
                d_j = d_t[j:j + 1, qb * QB:(qb + 1) * QB]
                p_m = jnp.exp(jnp.where(mask_m, _nt(ks[mrows, :], qp), NEG) - lse_j)
                p_w = jnp.exp(jnp.where(mask_w, _nt(ks[wrows, :], qp), NEG) - lse_j)
                ds_m = (p_m * (_nt(vs[mrows, :], dop) - d_j)).astype(BF)
                ds_w = (p_w * (_nt(vs[wrows, :], dop) - d_j)).astype(BF)
                hrows = pl.ds(ATT_HD * e, ATT_HD)
                dqt[pl.ds(ATT_HD * j, ATT_HD), qcols] = (_nn(ktr[hrows, pl.ds(0, N_META)], ds_m)
                                                         + _nn(ktr[hrows, pl.ds(_W0 + qb * QB, QB + HALO)], ds_w))
                dk_acc[mrows, :] += _nn(ds_m, qp)
                dk_acc[wrows, :] += _nn(ds_w, qp)
                dv_acc[mrows, :] += _nn(p_m.astype(BF), dop)
                dv_acc[wrows, :] += _nn(p_w.astype(BF), dop)
                dsk_ref[j:j + 1, :] -= jnp.exp(sk_ref[:, j:j + 1] - lse_j) * d_j

        dk_dir[...] += pltpu.roll(dk_rol[...], ATT_HD, 1)
        dv_dir[...] += pltpu.roll(dv_rol[...], ATT_HD, 1)
        meta_k[...] += dk_dir[pl.ds(0, N_META), :]
        meta_v[...] += dv_dir[pl.ds(0, N_META), :]
        first = jnp.where(t == 0, 1.0, 0.0)
        dk_dir[pl.ds(_C0 + TR - HALO, HALO), :] += carry_k[...]
        dv_dir[pl.ds(_C0 + TR - HALO, HALO), :] += carry_v[...]
        dk_dir[pl.ds(_C0 + META_PAD, N_META), :] += first * meta_k[...]
        dv_dir[pl.ds(_C0 + META_PAD, N_META), :] += first * meta_v[...]
        carry_k[...] = dk_dir[pl.ds(_W0, HALO), :]
        carry_v[...] = dv_dir[pl.ds(_W0, HALO), :]

        du_ref[:, 1152:1280] = dv_dir[pl.ds(_C0, TR), :].astype(BF)
        xhat_k, r_k = _head_rms(kc_ref[...].astype(F32), ones128)
        dkn = dk_dir[pl.ds(_C0, TR), :]
        dg_ref[1:2, 0:128] += jnp.sum(dkn * xhat_k, axis=0, keepdims=True)
        gd = dkn * kg_ref[...]
        du_ref[:, 1024:1152] = (r_k * (gd - xhat_k * _group_mean(gd * xhat_k, ones128))).astype(BF)
        dqn = dqt[...].T * _SCALE
        dg_ref[0:1, :] += jnp.sum(dqn * xhat_q, axis=0, keepdims=True)
        gd = dqn * qg_ref[...]
        du_ref[:, 0:512] = (r_q * (gd - xhat_q * _group_mean(gd * xhat_q, ones512))).astype(BF)

    rev = lambda i: (nt - 1 - i, 0)
    specs = _swa_in_specs(nt, True)
    win = pltpu.VMEM((_WROWS, 128), BF)
    wint = pltpu.VMEM((128, _WROWS), BF)
    tile_bf = pltpu.VMEM((TR, 512), BF)
    acc = pltpu.VMEM((_WROWS, 128), F32)
    return pl.pallas_call(
        body, name="swa_bwd", grid=(nt,),
        in_specs=specs + [pl.BlockSpec((TR, 512), rev), pl.BlockSpec((ATT_Q_HEADS, TR), lambda i: (0, nt - 1 - i)),
                          pl.BlockSpec((TR, 512), rev)],
        out_specs=[pl.BlockSpec((TR, N_C), rev), pl.BlockSpec((8, 512), lambda i: (0, 0)),
                   pl.BlockSpec((8, 128), lambda i: (0, 0))],
        out_shape=[jax.ShapeDtypeStruct((Lp, N_C), BF), jax.ShapeDtypeStruct((8, 512), F32),
                   jax.ShapeDtypeStruct((8, 128), F32)],
        scratch_shapes=[win, win, win, win, wint, wint, tile_bf, tile_bf, tile_bf, tile_bf,
                        pltpu.VMEM((512, TR), F32), acc, acc, acc, acc,
                        pltpu.VMEM((HALO, 128), F32), pltpu.VMEM((HALO, 128), F32),
                        pltpu.VMEM((N_META, 128), F32), pltpu.VMEM((N_META, 128), F32)],
        compiler_params=_cp(("arbitrary",)),
    )(uc, uc, uc, uc, uc, uc, uc, uc, qg8, kg2, sinks, o_save, lse, dyc)


def _mix_fwd(h, ya, yb, yc, ug, wa, wb, wc, wo):
    Lp = h.shape[0]
    wspec = lambda r: pl.BlockSpec((r, D_MODEL), lambda i: (0, 0))
    yspec = pl.BlockSpec((TRM, 512), lambda i: (i, 0))
    hspec = pl.BlockSpec((TRM, D_MODEL), lambda i: (i, 0))

    def body(h_ref, ya_ref, yb_ref, yc_ref, ug_ref, wa_ref, wb_ref, wc_ref, wo_ref, hn_ref, za_ref, zb_ref, zc_ref):
        mixed = jnp.zeros((TRM, D_MODEL), F32)
        for n, (y_ref, w_ref, z_ref) in enumerate(((ya_ref, wa_ref, za_ref), (yb_ref, wb_ref, zb_ref),
                                                   (yc_ref, wc_ref, zc_ref))):
            z = _nn(y_ref[...], w_ref[...])
            z_ref[...] = z.astype(BF)
            mixed = mixed + _sig(ug_ref[:, D_MODEL * n:D_MODEL * (n + 1)].astype(F32)) * z
        hn_ref[...] = h_ref[...] + _nn(mixed.astype(BF), wo_ref[...])

    return pl.pallas_call(
        body, name="mix_fwd", grid=(Lp // TRM,),
        in_specs=[hspec, yspec, yspec, yspec, pl.BlockSpec((TRM, N_G), lambda i: (i, 0)),
                  wspec(512), wspec(512), wspec(512), wspec(D_MODEL)],
        out_specs=[hspec, hspec, hspec, hspec],
        out_shape=[jax.ShapeDtypeStruct((Lp, D_MODEL), F32)] + [jax.ShapeDtypeStruct((Lp, D_MODEL), BF)] * 3,
        compiler_params=_cp(("parallel",)),
    )(h, ya, yb, yc, ug, wa, wb, wc, wo)


def _mix_bwd(dh, za, zb, zc, ug, wa, wb, wc, wo):
    Lp = dh.shape[0]
    wspec = lambda r: pl.BlockSpec((r, D_MODEL), lambda i: (0, 0))
    yspec = pl.BlockSpec((TRM, 512), lambda i: (i, 0))
    hspec = pl.BlockSpec((TRM, D_MODEL), lambda i: (i, 0))
    gspec = pl.BlockSpec((TRM, N_G), lambda i: (i, 0))

    def body(dh_ref, za_ref, zb_ref, zc_ref, ug_ref, wa_ref, wb_ref, wc_ref, wo_ref,
             dug_ref, mx_ref, dza_ref, dzb_ref, dzc_ref, dya_ref, dyb_ref, dyc_ref):
        dmix = _nt(dh_ref[...].astype(BF), wo_ref[...])
        mixed = jnp.zeros((TRM, D_MODEL), F32)
        for n, (z_ref, w_ref, dz_ref, dy_ref) in enumerate(((za_ref, wa_ref, dza_ref, dya_ref),
                                                            (zb_ref, wb_ref, dzb_ref, dyb_ref),
                                                            (zc_ref, wc_ref, dzc_ref, dyc_ref))):
            sl = slice(D_MODEL * n, D_MODEL * (n + 1))
            z = z_ref[...].astype(F32)
            gt = _sig(ug_ref[:, sl].astype(F32))
            mixed = mixed + gt * z
            dug_ref[:, sl] = (dmix * z * gt * (1.0 - gt)).astype(BF)
            dz = (dmix * gt).astype(BF)
            dz_ref[...] = dz
            dy_ref[...] = _nt(dz, w_ref[...]).astype(BF)
        mx_ref[...] = mixed.astype(BF)

    bf = lambda n: jax.ShapeDtypeStruct((Lp, n), BF)
    return pl.pallas_call(
        body, name="mix_bwd", grid=(Lp // TRM,),
        in_specs=[hspec, hspec, hspec, hspec, gspec, wspec(512), wspec(512), wspec(512), wspec(D_MODEL)],
        out_specs=[gspec, hspec, hspec, hspec, hspec, yspec, yspec, yspec],
        out_shape=[bf(N_G), bf(D_MODEL), bf(D_MODEL), bf(D_MODEL), bf(D_MODEL), bf(512), bf(512), bf(512)],
        compiler_params=_cp(("parallel",)),
    )(dh, za, zb, zc, ug, wa, wb, wc, wo)


def _inproj_bwd(dus, ws, h, dh, g, carry=None):
    Lp = h.shape[0]
    widths = [w.shape[1] for w in ws]

    def body(dg_ref, da_ref, db_ref, dc_ref, wg_ref, wa_ref, wb_ref, wc_ref, h_ref, dh_ref, g_ref, o_ref, gg_ref):
        @pl.when(pl.program_id(0) == 0)
        def _():
            gg_ref[...] = jnp.zeros_like(gg_ref)

        dhn = (_nt(dg_ref[...], wg_ref[...]) + _nt(da_ref[...], wa_ref[...])
               + _nt(db_ref[...], wb_ref[...]) + _nt(dc_ref[...], wc_ref[...]))
        x = h_ref[...]
        r = lax.rsqrt(jnp.mean(x * x, axis=-1, keepdims=True) + EPS)
        xhat = x * r
        gg_ref[0:1, :] += jnp.sum(dhn * xhat, axis=0, keepdims=True)
        gd = dhn * g_ref[...]
        o_ref[...] = dh_ref[...] + r * (gd - xhat * jnp.mean(gd * xhat, axis=-1, keepdims=True))

    hspec = pl.BlockSpec((TRM, D_MODEL), lambda i: (i, 0))
    in_specs = ([pl.BlockSpec((TRM, n), lambda i: (i, 0)) for n in widths]
                + [pl.BlockSpec((D_MODEL, n), lambda i: (0, 0), pipeline_mode=pl.Buffered(1)) for n in widths]
                + [hspec, hspec, pl.BlockSpec((1, D_MODEL), lambda i: (0, 0))])
    out_specs = [hspec, pl.BlockSpec((8, D_MODEL), lambda i: (0, 0))]
    out_shape = [jax.ShapeDtypeStruct((Lp, D_MODEL), F32), jax.ShapeDtypeStruct((8, D_MODEL), F32)]
    return _call_carrying(body, "inproj_bwd", Lp // TRM, in_specs, out_specs, out_shape, [],
                          (*dus, *ws, h, dh, g), carry)


def _loss_head(h, tgt_pad, seq):
    Lp = h.shape[0]
    nt = Lp // TR

    def body(h_ref, t_ref, dh_ref, l_ref):
        i = pl.program_id(0)

        @pl.when(i == 0)
        def _():
            l_ref[...] = jnp.zeros_like(l_ref)

        row = i * TR + lax.broadcasted_iota(jnp.int32, (TR, 1), 0)
        e = jnp.where((row >= CHUNK) & (row < CHUNK + seq), h_ref[...] - t_ref[...], 0.0)
        dh_ref[...] = e * (1.0 / D_MODEL)
        l_ref[...] += (0.5 / D_MODEL) * jnp.sum(jnp.sum(e * e, axis=0, keepdims=True), axis=1, keepdims=True)

    hspec = pl.BlockSpec((TR, D_MODEL), lambda i: (i, 0))
    return pl.pallas_call(
        body, name="loss_head", grid=(nt,), in_specs=[hspec, hspec],
        out_specs=[hspec, pl.BlockSpec((8, 128), lambda i: (0, 0))],
        out_shape=[jax.ShapeDtypeStruct((Lp, D_MODEL), F32), jax.ShapeDtypeStruct((8, 128), F32)],
        compiler_params=_cp(("arbitrary",)),
    )(h, tgt_pad)


def _lb_softmax(lb_ref):
    x = lb_ref[...]
    e = jnp.exp(x - jnp.max(x, axis=0, keepdims=True))
    return e / jnp.sum(e, axis=0, keepdims=True)


def _lb_fwd(hg_lb):
    def body(lb_ref, o_ref):
        sm = _lb_softmax(lb_ref)
        acc = jnp.zeros((1, 512), F32)
        for l in range(DEPTH):
            if l > 0:
                acc = acc + sm[l:l + 1, :]
            o_ref[l:l + 1, :] = jnp.clip(acc, 0.0, 1.0)

    return pl.pallas_call(body, name="lb_fwd", out_shape=jax.ShapeDtypeStruct((DEPTH, 512), F32))(hg_lb)


def _lb_bwd(hg_lb, dlb_all):
    def body(lb_ref, d_ref, o_ref):
        sm = _lb_softmax(lb_ref)
        acc = jnp.zeros((1, 512), F32)
        gm = []
        for l in range(DEPTH):
            if l > 0:
                acc = acc + sm[l:l + 1, :]
            gm.append(jnp.where((acc >= 0.0) & (acc <= 1.0), d_ref[l:l + 1, :], 0.0))
        dsm = [jnp.zeros((1, 512), F32)]
        for j in range(1, DEPTH):
            s = gm[j]
            for l in range(j + 1, DEPTH):
                s = s + gm[l]
            dsm.append(s)
        dot = dsm[0] * sm[0:1, :]
        for j in range(1, DEPTH):
            dot = dot + dsm[j] * sm[j:j + 1, :]
        for j in range(DEPTH):
            o_ref[j:j + 1, :] = sm[j:j + 1, :] * (dsm[j] - dot)

    return pl.pallas_call(body, name="lb_bwd", out_shape=jax.ShapeDtypeStruct((DEPTH, 512), F32))(hg_lb, dlb_all)


_ANY = pl.BlockSpec(memory_space=pl.ANY)


def _chip_peers():
    x, y, c = lax.axis_index("x"), lax.axis_index("y"), lax.axis_index("c")
    return (x, y, c), [(1 - x, y, c), (x, 1 - y, c), (1 - x, 1 - y, c)]


def _exchange(kind, ins, outs, send, recv, loc):
    (x, y, c), peers = _chip_peers()
    me = 2 * x + y
    ds = []
    for a in range(len(ins)):
        if kind == "gather":
            ds.append(pltpu.make_async_copy(ins[a], outs[a].at[me], loc.at[a]))
        else:
            ds.append(pltpu.make_async_copy(ins[a].at[me], outs[a].at[0], loc.at[a]))
        for p, (px, py, pc) in enumerate(peers):
            src, dst = (ins[a], outs[a].at[me]) if kind == "gather" else (ins[a].at[2 * px + py], outs[a].at[1 + p])
            ds.append(pltpu.make_async_remote_copy(src_ref=src, dst_ref=dst, send_sem=send.at[a, p],
                                                   recv_sem=recv.at[a, p], device_id=(px, py, pc), device_id_type=MESH))
    return ds


def _exchange_out_shapes(kind, arrs):
    if kind == "gather":
        return [jax.ShapeDtypeStruct((4,) + a.shape, a.dtype) for a in arrs]
    return [jax.ShapeDtypeStruct(a.shape, a.dtype) for a in arrs]


def _exchange_sems(n):
    return [pltpu.SemaphoreType.DMA((n, 3)), pltpu.SemaphoreType.DMA((n, 3)), pltpu.SemaphoreType.DMA((n,))]


def _exchange_chips(kind, arrs):
    n = len(arrs)

    def body(*refs):
        ds = _exchange(kind, refs[:n], refs[n:2 * n], *refs[2 * n:])
        for d in ds:
            d.start()
        for d in ds:
            d.wait()

    return pl.pallas_call(
        body, name=kind + "_chips", in_specs=[_ANY] * n, out_specs=[_ANY] * n,
        out_shape=_exchange_out_shapes(kind, arrs), scratch_shapes=_exchange_sems(n),
        compiler_params=pltpu.CompilerParams(has_side_effects=True),
    )(*arrs)


def _carry_exchange(body, n_in, n_out, n_steps, kind, n):
    def wrapped(*refs):
        ins, cin = refs[:n_in], refs[n_in:n_in + n]
        outs, cout = refs[n_in + n:n_in + n + n_out], refs[n_in + n + n_out:n_in + 2 * n + n_out]
        scr, sems = refs[n_in + 2 * n + n_out:-3], refs[-3:]
        i = pl.program_id(0)

        @pl.when(i == 0)
        def _():
            for d in _exchange(kind, cin, cout, *sems):
                d.start()

        body(*ins, *outs, *scr)

        @pl.when(i == n_steps - 1)
        def _():
            for d in _exchange(kind, cin, cout, *sems):
                d.wait()

    return wrapped


def _swap_cores(arrs):
    n = len(arrs)

    def body(*refs):
        ins, outs = refs[:n], refs[n:2 * n]
        send, recv = refs[2 * n:]
        x, y, c = lax.axis_index("x"), lax.axis_index("y"), lax.axis_index("c")
        rdmas = []
        for a in range(n):
            r = pltpu.make_async_remote_copy(src_ref=ins[a], dst_ref=outs[a], send_sem=send.at[a], recv_sem=recv.at[a],
                                             device_id=(x, y, 1 - c), device_id_type=MESH)
            r.start()
            rdmas.append(r)
        for r in rdmas:
            r.wait()

    return pl.pallas_call(
        body, name="swap_cores", in_specs=[_ANY] * n, out_specs=[_ANY] * n,
        out_shape=[jax.ShapeDtypeStruct(a.shape, a.dtype) for a in arrs],
        scratch_shapes=[pltpu.SemaphoreType.DMA((n,)), pltpu.SemaphoreType.DMA((n,))],
        compiler_params=pltpu.CompilerParams(has_side_effects=True),
    )(*arrs)


def _allsum_small(p):
    R = p.shape[0]

    def body(p_ref, o_ref, buf, send, recv):
        x, y, c = lax.axis_index("x"), lax.axis_index("y"), lax.axis_index("c")
        me = 4 * x + 2 * y + c
        buf[me] = p_ref[...]
        rdmas = []
        for k in range(1, 8):
            peer = (x ^ (k >> 2), y ^ ((k >> 1) & 1), c ^ (k & 1))
            r = pltpu.make_async_remote_copy(src_ref=p_ref, dst_ref=buf.at[me], send_sem=send.at[k - 1],
                                             recv_sem=recv.at[k - 1], device_id=peer, device_id_type=MESH)
            r.start()
            rdmas.append(r)
        for r in rdmas:
            r.wait()
        acc = buf[0]
        for d in range(1, 8):
            acc = acc + buf[d]
        o_ref[...] = acc

    return pl.pallas_call(
        body, name="allsum_small", out_shape=jax.ShapeDtypeStruct((R, 512), F32),
        in_specs=[pl.BlockSpec(memory_space=pltpu.VMEM)], out_specs=pl.BlockSpec(memory_space=pltpu.VMEM),
        scratch_shapes=[pltpu.VMEM((8, R, 512), F32), pltpu.SemaphoreType.DMA((7,)), pltpu.SemaphoreType.DMA((7,))],
        compiler_params=_cp(has_side_effects=True),
    )(p)


def _sum4(parts, name):
    _, R, C = parts.shape
    tr = 256 if R % 256 == 0 else R

    def body(p_ref, o_ref):
        o_ref[...] = ((p_ref[0] + p_ref[1]) + p_ref[2]) + p_ref[3]

    return pl.pallas_call(
        body, name=name, grid=(R // tr,), in_specs=[pl.BlockSpec((4, tr, C), lambda i: (0, i, 0))],
        out_specs=pl.BlockSpec((tr, C), lambda i: (i, 0)), out_shape=jax.ShapeDtypeStruct((R, C), F32),
        compiler_params=_cp(("parallel",)),
    )(parts)


def _adamw(w, m, v, g0, g1, name):
    R, C = w.shape
    tr = 256 if R % 256 == 0 else R
    two = g1 is not None
    c1 = 1.0 / (1.0 - ADAM_B1 ** ADAM_STEP)
    c2 = 1.0 / (1.0 - ADAM_B2 ** ADAM_STEP)

    def body(*refs):
        if two:
            w_ref, m_ref, v_ref, a_ref, b_ref, g_ref, d_ref, nm_ref, nv_ref = refs
            g = a_ref[...] + b_ref[...]
        else:
            w_ref, m_ref, v_ref, a_ref, g_ref, d_ref, nm_ref, nv_ref = refs
            g = a_ref[...]
        g_ref[...] = g
        m = ADAM_B1 * m_ref[...] + (1.0 - ADAM_B1) * g
        v = ADAM_B2 * v_ref[...] + (1.0 - ADAM_B2) * (g * g)
        nm_ref[...] = m
        nv_ref[...] = v
        d_ref[...] = -ADAM_LR * ((m * c1) / (jnp.sqrt(v * c2) + ADAM_EPS) + ADAM_WD * w_ref[...])

    spec = pl.BlockSpec((tr, C), lambda i: (i, 0))
    n_in = 5 if two else 4
    ins = (w, m, v, g0, g1) if two else (w, m, v, g0)
    return pl.pallas_call(
        body, name=name, grid=(R // tr,), in_specs=[spec] * n_in, out_specs=[spec] * 4,
        out_shape=[jax.ShapeDtypeStruct((R, C), F32)] * 4, compiler_params=_cp(("parallel",)),
    )(*ins)


def _pad8(a):
    r = (-a.shape[0]) % 8
    return a if r == 0 else jnp.pad(a, ((0, r), (0, 0)))


def _local_step(x, tgt, meta, P, shards=None, prep=None, pack=None):
    seq = x.shape[0]
    assert seq % TR == 0
    Lp = seq + TR
    h = jnp.concatenate([jnp.zeros((META_PAD, D_MODEL), F32), meta, x, jnp.zeros((TAIL_PAD, D_MODEL), F32)], axis=0)
    tgt_pad = jnp.pad(tgt, ((CHUNK, TAIL_PAD), (0, 0)))

    P = list(P)
    saved = []
    for l in range(DEPTH):
        p = P[l]
        hn = _rms_fwd(h, p["norm_g"])
        mm = functools.partial(_matmul, out_dtype=BF, tm=TR, tk=D_MODEL, col_major_grid=True)
        ug = mm(hn, p["w_g"], tn=N_G // 2, name="inproj_g")
        ua = mm(hn, p["w_a"], tn=N_A, name="inproj_a")
        ub = mm(hn, p["w_b"], tn=N_B, name="inproj_b")
        uc = mm(hn, p["w_c"], tn=N_C, name="inproj_c")
        ya, yconv = _conv_fwd(ua, p["conv_w"], p["conv_vec"])
        yb, o_hg, s_hg = _hg_fwd(ub, p["lb"], p["gn4"])
        carry = ("gather", shards[l + 1]) if shards is not None and l + 1 < DEPTH else None
        res = _swa_fwd(uc, p["qg"], p["kg"], p["sinks"], carry)
        yc, o_at, lse = res[:3]
        if carry is not None:
            P.append(prep(l + 1, res[3:]))
        h_new, za, zb, zc = _mix_fwd(h, ya, yb, yc, ug, p["w_ao"], p["w_bo"], p["w_co"], p["w_out"])
        saved.append(dict(h=h, hn=hn, ug=ug, ua=ua, ub=ub, uc=uc, ya=ya, yconv=yconv, yb=yb, o_hg=o_hg, s_hg=s_hg,
                          yc=yc, o_at=o_at, lse=lse, za=za, zb=zb, zc=zc))
        h = h_new

    dh, loss8 = _loss_head(h, tgt_pad, seq)

    grads = [None] * DEPTH
    parts = [[None, None] for _ in range(DEPTH)]
    pending = None
    tk_dw = 3 * TR if Lp % (3 * TR) == 0 else TR
    for l in reversed(range(DEPTH)):
        p, s = P[l], saved[l]
        dug, mixed, dza, dzb, dzc, dya, dyb, dyc = _mix_bwd(dh, s["za"], s["zb"], s["zc"], s["ug"],
                                                             p["w_ao"], p["w_bo"], p["w_co"], p["w_out"])
        tnmm = functools.partial(_matmul, ta=True, out_dtype=F32, tk=tk_dw)
        g = {}
        g["w_out"] = tnmm(mixed, dh, tm=D_MODEL, tn=D_MODEL, name="dw_out")
        g["w_ao"] = tnmm(s["ya"], dza, tm=512, tn=D_MODEL, name="dw_ao")
        g["w_bo"] = tnmm(s["yb"], dzb, tm=512, tn=D_MODEL, name="dw_bo")
        g["w_co"] = tnmm(s["yc"], dzc, tm=512, tn=D_MODEL, name="dw_co")
        dua, g["conv_w"], g["conv_vec"] = _conv_bwd(s["ua"], s["yconv"], dya, p["conv_w"], p["conv_vec"])
        carry = ("scatter", pending[1]) if pending is not None else None
        res = _hg_bwd(s["ub"], p["lb"], p["gn4"], s["o_hg"], s["s_hg"], dyb, carry)
        dub, g["hg_small"] = res[:2]
        if carry is not None:
            parts[pending[0]][1] = res[2:]
        duc, g["at_gain"], g["at_sink"] = _swa_bwd(s["uc"], p["qg"], p["kg"], p["sinks"], s["o_at"], s["lse"], dyc)
        g["w_g"] = tnmm(s["hn"], dug, tm=D_MODEL, tn=N_G // 2, name="dw_in_g")
        g["w_a"] = tnmm(s["hn"], dua, tm=D_MODEL, tn=N_A, name="dw_in_a")
        g["w_b"] = tnmm(s["hn"], dub, tm=D_MODEL, tn=N_B, name="dw_in_b")
        g["w_c"] = tnmm(s["hn"], duc, tm=D_MODEL, tn=N_C, name="dw_in_c")
        first, second = pack(g) if pack is not None else (None, None)
        res = _inproj_bwd([dug, dua, dub, duc], [p["w_g"], p["w_a"], p["w_b"], p["w_c"]], s["h"], dh, p["norm_g"],
                          ("scatter", first) if first is not None else None)
        dh, g["norm_g"] = res[:2]
        grads[l] = g
        if pack is not None:
            parts[l][0] = res[2:]
            pending = (l, second)
    if pending is not None:
        parts[pending[0]][1] = _exchange_chips("scatter", pending[1])
    return loss8, dh, grads, parts


def _split_w_in(w):
    return dict(w_a=w[:, 0:1536], w_b=w[:, 1536:3584],
                w_c=jnp.concatenate([w[:, 3584:4096], w[:, 4352:4864], w[:, 4096:4352]], axis=1),
                w_g=w[:, 4864:7936])


def _join_w_in(g):
    c = g["w_c"]
    return jnp.concatenate([g["w_a"], g["w_b"], c[:, 0:512], c[:, 1024:1280], c[:, 512:1024], g["w_g"]], axis=1)


def _attn_small(g):
    return (g["at_gain"][0].reshape(ATT_Q_HEADS, ATT_HD).sum(0),
            g["at_gain"][1, 0:128].reshape(ATT_KV_HEADS, ATT_HD).sum(0), g["at_sink"].sum(1))


_SMALL = (("norm_g", 8), ("meta", 32), ("conv_w", 32 * DEPTH), ("conv_b", 8), ("conv_ln_g", 8), ("conv_ln_b", 8),
          ("lb", 8), ("hg_norm_g", 8), ("q_norm_g", 8), ("k_norm_g", 8), ("sinks", 8))


def _small_offsets():
    off, o = {}, 0
    for name, rows in _SMALL:
        off[name] = (o, rows)
        o += rows
    return off, o


def _pack_small(d):
    parts = []
    for name, rows in _SMALL:
        a = d[name]
        parts.append(jnp.pad(a, ((0, rows - a.shape[0]), (0, 512 - a.shape[1]))))
    return jnp.concatenate(parts, axis=0)


def kernel(x, meta_tokens, norm_g, w_in, conv_w, conv_b, conv_ln_g, conv_ln_b, w_conv_out, hg_lower_bounds, hg_norm_g, w_hg_out, q_norm_g, k_norm_g, attn_sinks, w_att_out, w_out, loss_target, m_meta_tokens, m_norm_g, m_w_in, m_conv_w, m_conv_b, m_conv_ln_g, m_conv_ln_b, m_w_conv_out, m_hg_lower_bounds, m_hg_norm_g, m_w_hg_out, m_q_norm_g, m_k_norm_g, m_attn_sinks, m_w_att_out, m_w_out, v_meta_tokens, v_norm_g, v_w_in, v_conv_w, v_conv_b, v_conv_ln_g, v_conv_ln_b, v_w_conv_out, v_hg_lower_bounds, v_hg_norm_g, v_w_hg_out, v_q_norm_g, v_k_norm_g, v_attn_sinks, v_w_att_out, v_w_out):
    xi, yi = lax.axis_index("x"), lax.axis_index("y")
    chip = 2 * xi + yi
    NS = w_in.shape[2]
    CS = conv_w.shape[2]
    MS = meta_tokens.shape[1]

    shards = [[w_in[l].astype(BF), w_conv_out[l].astype(BF), w_hg_out[l].astype(BF), w_att_out[l].astype(BF),
               w_out[l].astype(BF)] for l in range(DEPTH)]
    *first, g_meta, g_convw = _exchange_chips(
        "gather", shards[0] + [meta_tokens, conv_w.reshape(DEPTH * CONV_WIDTH, CS)])
    cols = lambda g: g.transpose(1, 0, 2).reshape(g.shape[1], -1)
    meta_f = cols(g_meta)
    convw_f = cols(g_convw).reshape(DEPTH, CONV_WIDTH, D_CONV)
    lb_all = _lb_fwd(hg_lower_bounds)

    def prep(l, gathered):
        g_win, g_wao, g_wbo, g_wco, g_wout = gathered
        p = _split_w_in(cols(g_win))
        p.update(w_ao=cols(g_wao), w_bo=cols(g_wbo), w_co=cols(g_wco), w_out=g_wout.reshape(D_MODEL, D_MODEL),
                 norm_g=norm_g[l:l + 1], conv_w=convw_f[l],
                 conv_vec=_pad8(jnp.stack([conv_b[l], conv_ln_g[l], conv_ln_b[l]])),
                 lb=lb_all[l:l + 1], gn4=jnp.tile(hg_norm_g[l:l + 1], (1, HG_HEADS)),
                 qg=jnp.tile(q_norm_g[l:l + 1], (1, ATT_Q_HEADS)), kg=jnp.tile(k_norm_g[l:l + 1], (1, ATT_KV_HEADS)),
                 sinks=attn_sinks[l:l + 1])
        return p

    shard_cols = lambda a: a.reshape(a.shape[0], 4, -1).transpose(1, 0, 2)
    half = D_MODEL // 2

    def pack(g):
        win = shard_cols(_join_w_in(g))
        return [win[:, :half]], [win[:, half:], shard_cols(g["w_ao"]), shard_cols(g["w_bo"]), shard_cols(g["w_co"]),
                                 g["w_out"].reshape(4, MS, D_MODEL)]

    loss8, dh0, grads, parts = _local_step(x[0], loss_target[0], meta_f, [prep(0, first)], shards, prep, pack)
    seq = x.shape[1]
    grad_x = dh0[CHUNK:CHUNK + seq][None]
    loss = lax.psum(loss8[0, 0], ("x", "y", "c"))

    sum4 = functools.partial(_sum4, name="sum_chips")
    mine = [jnp.concatenate([t for l in range(DEPTH) for t in (sum4(parts[l][0][0]), sum4(parts[l][1][0]))], axis=0)]
    mine += [jnp.concatenate([sum4(parts[l][1][a]) for l in range(DEPTH)], axis=0) for a in range(1, 5)]
    theirs = _swap_cores(mine)

    dlb_all = jnp.concatenate([grads[l]["hg_small"][0:1] for l in range(DEPTH)], axis=0)
    small = dict(
        norm_g=jnp.concatenate([grads[l]["norm_g"][0:1] for l in range(DEPTH)], axis=0).reshape(8, 512),
        meta=dh0[META_PAD:CHUNK].reshape(32, 512),
        conv_w=jnp.concatenate([grads[l]["conv_w"] for l in range(DEPTH)], axis=0),
        conv_b=jnp.concatenate([grads[l]["conv_vec"][0:1] for l in range(DEPTH)], axis=0),
        conv_ln_g=jnp.concatenate([grads[l]["conv_vec"][1:2] for l in range(DEPTH)], axis=0),
        conv_ln_b=jnp.concatenate([grads[l]["conv_vec"][2:3] for l in range(DEPTH)], axis=0),
        lb=_lb_bwd(hg_lower_bounds, dlb_all),
        hg_norm_g=jnp.concatenate([grads[l]["hg_small"][1:2].reshape(HG_HEADS, HG_D).sum(0, keepdims=True)
                                   for l in range(DEPTH)], axis=0),
        q_norm_g=jnp.stack([_attn_small(grads[l])[0] for l in range(DEPTH)]),
        k_norm_g=jnp.stack([_attn_small(grads[l])[1] for l in range(DEPTH)]),
        sinks=jnp.stack([_attn_small(grads[l])[2] for l in range(DEPTH)]),
    )
    gsum = _allsum_small(_pack_small(small))
    off, _ = _small_offsets()

    def take(name, rows, cols):
        o, _ = off[name]
        return gsum[o:o + rows, 0:cols]

    g_meta_full = take("meta", 32, 512).reshape(N_META, D_MODEL)
    g_convw_full = take("conv_w", 32 * DEPTH, 512).reshape(DEPTH, 32, 512)[:, :CONV_WIDTH]
    small_grads = dict(
        norm_g=take("norm_g", 8, 512),
        meta=lax.dynamic_slice_in_dim(g_meta_full, chip * MS, MS, axis=1),
        conv_w=lax.dynamic_slice_in_dim(g_convw_full, chip * CS, CS, axis=2).reshape(DEPTH * CONV_WIDTH, CS),
        conv_b=take("conv_b", DEPTH, 512), conv_ln_g=take("conv_ln_g", DEPTH, 512), conv_ln_b=take("conv_ln_b", DEPTH, 512),
        lb=take("lb", DEPTH, 512), hg_norm_g=take("hg_norm_g", DEPTH, HG_D), q_norm_g=take("q_norm_g", DEPTH, ATT_HD),
        k_norm_g=take("k_norm_g", DEPTH, ATT_HD), sinks=take("sinks", DEPTH, ATT_Q_HEADS))

    def big_update(w, m, v, a, b, name):
        shp = w.shape
        r2 = lambda t: t.reshape(-1, shp[-1])
        outs = _adamw(r2(w), r2(m), r2(v), a, b, name)
        return [o.reshape(shp) for o in outs]

    res = {}
    res["w_in"] = big_update(w_in, m_w_in, v_w_in, mine[0], theirs[0], "adamw_w_in")
    res["w_conv_out"] = big_update(w_conv_out, m_w_conv_out, v_w_conv_out, mine[1], theirs[1], "adamw_w_ao")
    res["w_hg_out"] = big_update(w_hg_out, m_w_hg_out, v_w_hg_out, mine[2], theirs[2], "adamw_w_bo")
    res["w_att_out"] = big_update(w_att_out, m_w_att_out, v_w_att_out, mine[3], theirs[3], "adamw_w_co")
    res["w_out"] = big_update(w_out, m_w_out, v_w_out, mine[4], theirs[4], "adamw_w_out")

    small_w = dict(meta=(meta_tokens, m_meta_tokens, v_meta_tokens), norm_g=(norm_g, m_norm_g, v_norm_g),
                   conv_w=(conv_w, m_conv_w, v_conv_w), conv_b=(conv_b, m_conv_b, v_conv_b),
                   conv_ln_g=(conv_ln_g, m_conv_ln_g, v_conv_ln_g), conv_ln_b=(conv_ln_b, m_conv_ln_b, v_conv_ln_b),
                   lb=(hg_lower_bounds, m_hg_lower_bounds, v_hg_lower_bounds),
                   hg_norm_g=(hg_norm_g, m_hg_norm_g, v_hg_norm_g), q_norm_g=(q_norm_g, m_q_norm_g, v_q_norm_g),
                   k_norm_g=(k_norm_g, m_k_norm_g, v_k_norm_g), sinks=(attn_sinks, m_attn_sinks, v_attn_sinks))
    view = lambda n, t: t.reshape(-1, 512) if n == "norm_g" else t.reshape(-1, t.shape[-1])
    pw, pm, pv = (_pack_rows([view(n, small_w[n][k]) for n in small_w]) for k in range(3))
    pg = _pack_rows([small_grads[n] for n in small_w])
    packed = _adamw(pw, pm, pv, pg, None, "adamw_small")
    o = 0
    for n in small_w:
        r, cdim = view(n, small_w[n][0]).shape
        res[n] = [t[o:o + r, 0:cdim].reshape(small_w[n][0].shape) for t in packed]
        o += -(-r // 8) * 8

    order = [("meta", None), ("norm_g", None), ("w_in", None), ("conv_w", None), ("conv_b", None), ("conv_ln_g", None),
             ("conv_ln_b", None), ("w_conv_out", None), ("lb", None), ("hg_norm_g", None), ("w_hg_out", None),
             ("q_norm_g", None), ("k_norm_g", None), ("sinks", None), ("w_att_out", None), ("w_out", None)]
    outs = [loss, grad_x]
    for k in range(4):
        outs += [res[n][k] for n, _ in order]
    return tuple(outs)


def _pack_rows(arrs):
    parts = []
    for a in arrs:
        r = (-a.shape[0]) % 8
        parts.append(jnp.pad(a, ((0, r), (0, 512 - a.shape[1]))))
    return jnp.concatenate(parts, axis=0)
```

```python
import functools

import jax
import jax.numpy as jnp
from jax import lax
from jax.experimental import pallas as pl
from jax.experimental.pallas import tpu as pltpu

F32 = jnp.float32
BF = jnp.bfloat16

D_MODEL = 1024
DEPTH = 4
CHUNK = 64
N_META = 16
META_PAD = CHUNK - N_META
D_CONV = 512
CONV_WIDTH = 31
HG_HEADS = 4
HG_D = 128
ATT_Q_HEADS = 8
ATT_KV_HEADS = 2
ATT_HD = 64
ATT_GROUP = ATT_Q_HEADS // ATT_KV_HEADS
EPS = 1e-6
F_FLOOR = 1e-30
NEG = -1e30

ADAM_LR = 0.001
ADAM_B1 = 0.9
ADAM_B2 = 0.999
ADAM_EPS = 1e-08
ADAM_WD = 0.01
ADAM_STEP = 10

TR = 512
TRM = 256
CONV_RB = 32
QB = 128
HALO = 128
TAIL_PAD = TR - CHUNK
VMEM_LIMIT = 56 * 1024 * 1024

N_G, N_A, N_B, N_C = 3 * D_MODEL, 3 * D_CONV, 4 * 512, 2 * 512 + 2 * 128

MESH = pl.DeviceIdType.MESH


def _cp(sem=None, vmem=VMEM_LIMIT, **kw):
    if sem is None:
        return pltpu.CompilerParams(vmem_limit_bytes=vmem, **kw)
    return pltpu.CompilerParams(dimension_semantics=sem, vmem_limit_bytes=vmem, **kw)


def _nn(a, b):
    return lax.dot_general(a, b, (((1,), (0,)), ((), ())), preferred_element_type=F32)


def _nt(a, b):
    return lax.dot_general(a, b, (((1,), (1,)), ((), ())), preferred_element_type=F32)


def _tn(a, b):
    return lax.dot_general(a, b, (((0,), (0,)), ((), ())), preferred_element_type=F32)


def _sig(x):
    return jax.nn.sigmoid(x)


def _silu(x):
    return x * _sig(x)


def _dsilu(x):
    s = _sig(x)
    return s * (1.0 + x * (1.0 - s))


def _split3(x):
    hi = x.astype(BF)
    r1 = x - hi.astype(F32)
    mid = r1.astype(BF)
    lo = (r1 - mid.astype(F32)).astype(BF)
    return hi, mid, lo


def _mm3(t, x):
    hi, mid, lo = _split3(x)
    return _nn(t, hi) + _nn(t, mid) + _nn(t, lo)


def _chunk_tri(n, upper):
    r = lax.broadcasted_iota(jnp.int32, (n, n), 0)
    c = lax.broadcasted_iota(jnp.int32, (n, n), 1)
    same = jnp.right_shift(r, 6) == jnp.right_shift(c, 6)
    tri = (c >= r) if upper else (c <= r)
    return jnp.where(same & tri, 1.0, 0.0).astype(BF)


def _matmul(a, b, *, ta=False, tb=False, out_dtype, tm, tn, tk, name, col_major_grid=False):
    if ta:
        K, M = a.shape
    else:
        M, K = a.shape
    N = b.shape[0] if tb else b.shape[1]
    assert M % tm == 0 and N % tn == 0 and K % tk == 0, (name, M, N, K, tm, tn, tk)
    nk = K // tk
    if col_major_grid:
        grid = (N // tn, M // tm, nk)
        ij = lambda g0, g1: (g1, g0)
    else:
        grid = (M // tm, N // tn, nk)
        ij = lambda g0, g1: (g0, g1)
    if ta:
        a_spec = pl.BlockSpec((tk, tm), lambda g0, g1, k: (k, ij(g0, g1)[0]))
    else:
        a_spec = pl.BlockSpec((tm, tk), lambda g0, g1, k: (ij(g0, g1)[0], k))
    if tb:
        b_spec = pl.BlockSpec((tn, tk), lambda g0, g1, k: (ij(g0, g1)[1], k))
    else:
        b_spec = pl.BlockSpec((tk, tn), lambda g0, g1, k: (k, ij(g0, g1)[1]))
    o_spec = pl.BlockSpec((tm, tn), lambda g0, g1, k: ij(g0, g1))
    dims = (((0 if ta else 1,), (1 if tb else 0,)), ((), ()))
    use_acc = nk > 1 and out_dtype != F32

    def body(a_ref, b_ref, o_ref, *scr):
        k = pl.program_id(2)
        p = lax.dot_general(a_ref[...].astype(BF), b_ref[...].astype(BF), dims, preferred_element_type=F32)
        if nk == 1:
            o_ref[...] = p.astype(out_dtype)
        else:
            acc = scr[0] if use_acc else o_ref

            @pl.when(k == 0)
            def _():
                acc[...] = p

            @pl.when(k > 0)
            def _():
                acc[...] += p

            if use_acc:
                @pl.when(k == nk - 1)
                def _():
                    o_ref[...] = acc[...].astype(out_dtype)

    return pl.pallas_call(
        body, name=name, grid=grid, in_specs=[a_spec, b_spec], out_specs=o_spec,
        out_shape=jax.ShapeDtypeStruct((M, N), out_dtype),
        scratch_shapes=[pltpu.VMEM((tm, tn), F32)] if use_acc else [],
        compiler_params=_cp(("parallel", "parallel", "arbitrary")),
    )(a, b)


def _rms_fwd(h, g):
    Lp = h.shape[0]

    def body(h_ref, g_ref, o_ref):
        x = h_ref[...]
        r = lax.rsqrt(jnp.mean(x * x, axis=-1, keepdims=True) + EPS)
        o_ref[...] = (x * r * g_ref[...]).astype(BF)

    return pl.pallas_call(
        body, name="rms_fwd", grid=(Lp // TR,),
        in_specs=[pl.BlockSpec((TR, D_MODEL), lambda i: (i, 0)), pl.BlockSpec((1, D_MODEL), lambda i: (0, 0))],
        out_specs=pl.BlockSpec((TR, D_MODEL), lambda i: (i, 0)),
        out_shape=jax.ShapeDtypeStruct((Lp, D_MODEL), BF),
        compiler_params=_cp(("parallel",)),
    )(h, g)


def _glu(ua, row):
    a = ua[:, 0:D_CONV].astype(F32)
    gl = ua[:, D_CONV:2 * D_CONV].astype(F32)
    return jnp.where(row >= META_PAD, a * _sig(gl), 0.0)


_SH_ROWS = TR + CHUNK - 8


def _fill_shifts(src, sh):
    for b in range(1, 8):
        sh[b - 1] = src[pl.ds(b, _SH_ROWS), :]


def _shifted(src, sh, start, n):
    b = start % 8
    if b == 0:
        return src[pl.ds(start, n), :]
    return sh[b - 1, pl.ds(start - b, n), :]


def _conv_fwd(ua, cw, cvec):
    Lp = ua.shape[0]
    nt = Lp // TR
    hb = TR // CHUNK

    def body(cur_ref, halo_ref, w_ref, v_ref, ya_ref, yc_ref, ext, sh):
        i = pl.program_id(0)
        row = i * TR + lax.broadcasted_iota(jnp.int32, (TR, 1), 0)
        hrow = i * TR - CHUNK + lax.broadcasted_iota(jnp.int32, (CHUNK, 1), 0)
        ext[pl.ds(0, CHUNK), :] = jnp.where(i > 0, _glu(halo_ref[...], hrow), 0.0)
        ext[pl.ds(CHUNK, TR), :] = _glu(cur_ref[...], row)
        _fill_shifts(ext, sh)
        for rb in range(TR // CONV_RB):
            r0 = rb * CONV_RB
            rows = pl.ds(r0, CONV_RB)
            acc = jnp.zeros((CONV_RB, D_CONV), F32)
            for j in range(CONV_WIDTH):
                acc = acc + _shifted(ext, sh, r0 + CHUNK - (CONV_WIDTH - 1) + j, CONV_RB) * w_ref[j:j + 1, :]
            y = acc + v_ref[0:1, :]
            yc_ref[rows, :] = y
            mu = jnp.mean(y, axis=-1, keepdims=True)
            d = y - mu
            var = jnp.mean(d * d, axis=-1, keepdims=True)
            yn = d * lax.rsqrt(var + EPS) * v_ref[1:2, :] + v_ref[2:3, :]
            ya_ref[rows, :] = (_silu(yn) * _silu(cur_ref[rows, 2 * D_CONV:3 * D_CONV].astype(F32))).astype(BF)

    return pl.pallas_call(
        body, name="conv_fwd", grid=(nt,),
        in_specs=[pl.BlockSpec((TR, N_A), lambda i: (i, 0)),
                  pl.BlockSpec((CHUNK, N_A), lambda i: (jnp.maximum(i * hb - 1, 0), 0)),
                  pl.BlockSpec((CONV_WIDTH, D_CONV), lambda i: (0, 0)),
                  pl.BlockSpec((8, D_CONV), lambda i: (0, 0))],
        out_specs=[pl.BlockSpec((TR, D_CONV), lambda i: (i, 0)), pl.BlockSpec((TR, D_CONV), lambda i: (i, 0))],
        out_shape=[jax.ShapeDtypeStruct((Lp, D_CONV), BF), jax.ShapeDtypeStruct((Lp, D_CONV), F32)],
        scratch_shapes=[pltpu.VMEM((TR + CHUNK, D_CONV), F32), pltpu.VMEM((7, _SH_ROWS, D_CONV), F32)],
        compiler_params=_cp(("parallel",)),
    )(ua, ua, cw, cvec)


def _conv_bwd(ua, yconv, dya, cw, cvec):
    Lp = ua.shape[0]
    nt = Lp // TR
    hb = TR // CHUNK
    nhb = Lp // CHUNK

    def ln_bwd(y, dout, gate, v_ref):
        mu = jnp.mean(y, axis=-1, keepdims=True)
        d = y - mu
        var = jnp.mean(d * d, axis=-1, keepdims=True)
        rstd = lax.rsqrt(var + EPS)
        xhat = d * rstd
        yn = xhat * v_ref[1:2, :] + v_ref[2:3, :]
        dyn = dout * _silu(gate) * _dsilu(yn)
        dxh = dyn * v_ref[1:2, :]
        dyc = rstd * (dxh - jnp.mean(dxh, axis=-1, keepdims=True) - xhat * jnp.mean(dxh * xhat, axis=-1, keepdims=True))
        return dyc, dyn, xhat, yn

    def body(cur_ref, prev_ref, next_ref, yc_ref, ycn_ref, dy_ref, dyn_ref, w_ref, v_ref,
             du_ref, dw_ref, dv_ref, uext, dext, dwacc, ush, dsh):
        i = pl.program_id(0)

        @pl.when(i == 0)
        def _():
            dwacc[...] = jnp.zeros_like(dwacc)
            dv_ref[...] = jnp.zeros_like(dv_ref)

        row = i * TR + lax.broadcasted_iota(jnp.int32, (TR, 1), 0)
        hrow = i * TR - CHUNK + lax.broadcasted_iota(jnp.int32, (CHUNK, 1), 0)
        uext[pl.ds(0, CHUNK), :] = jnp.where(i > 0, _glu(prev_ref[...], hrow), 0.0)
        uext[pl.ds(CHUNK, TR), :] = _glu(cur_ref[...], row)

        s_b = jnp.zeros((1, D_CONV), F32)
        s_g = jnp.zeros((1, D_CONV), F32)
        s_bb = jnp.zeros((1, D_CONV), F32)
        for rb in range(TR // CONV_RB):
            rows = pl.ds(rb * CONV_RB, CONV_RB)
            gate = cur_ref[rows, 2 * D_CONV:3 * D_CONV].astype(F32)
            dout = dy_ref[rows, :].astype(F32)
            dyc, dyn, xhat, yn = ln_bwd(yc_ref[rows, :], dout, gate, v_ref)
            du_ref[rows, 2 * D_CONV:3 * D_CONV] = (dout * _silu(yn) * _dsilu(gate)).astype(BF)
            dext[rows, :] = dyc
            s_b = s_b + jnp.sum(dyc, axis=0, keepdims=True)
            s_g = s_g + jnp.sum(dyn * xhat, axis=0, keepdims=True)
            s_bb = s_bb + jnp.sum(dyn, axis=0, keepdims=True)
        dv_ref[0:1, :] += s_b
        dv_ref[1:2, :] += s_g
        dv_ref[2:3, :] += s_bb
        dyc_n, _, _, _ = ln_bwd(ycn_ref[...], dyn_ref[...].astype(F32),
                                next_ref[:, 2 * D_CONV:3 * D_CONV].astype(F32), v_ref)
        dext[pl.ds(TR, CHUNK), :] = jnp.where(i < nt - 1, dyc_n, 0.0)
        _fill_shifts(uext, ush)
        _fill_shifts(dext, dsh)

        for rb in range(TR // CONV_RB):
            r0 = rb * CONV_RB
            rows = pl.ds(r0, CONV_RB)
            d_blk = dext[rows, :]
            dglu = jnp.zeros((CONV_RB, D_CONV), F32)
            for j in range(CONV_WIDTH):
                dglu = dglu + _shifted(dext, dsh, r0 + CONV_WIDTH - 1 - j, CONV_RB) * w_ref[j:j + 1, :]
                prod = d_blk * _shifted(uext, ush, r0 + CHUNK - (CONV_WIDTH - 1) + j, CONV_RB)
                part = prod[0:8, :]
                for s in range(1, CONV_RB // 8):
                    part = part + prod[8 * s:8 * s + 8, :]
                dwacc[j] += part
            a = cur_ref[rows, 0:D_CONV].astype(F32)
            sg = _sig(cur_ref[rows, D_CONV:2 * D_CONV].astype(F32))
            grow = i * TR + r0 + lax.broadcasted_iota(jnp.int32, (CONV_RB, 1), 0)
            dglu = jnp.where(grow >= META_PAD, dglu, 0.0)
            du_ref[rows, 0:D_CONV] = (dglu * sg).astype(BF)
            du_ref[rows, D_CONV:2 * D_CONV] = (dglu * a * sg * (1.0 - sg)).astype(BF)

        @pl.when(i == nt - 1)
        def _():
            dw_ref[...] = jnp.sum(dwacc[...], axis=1)

    nxt = lambda i: (jnp.minimum(i * hb + hb, nhb - 1), 0)
    return pl.pallas_call(
        body, name="conv_bwd", grid=(nt,),
        in_specs=[pl.BlockSpec((TR, N_A), lambda i: (i, 0)),
                  pl.BlockSpec((CHUNK, N_A), lambda i: (jnp.maximum(i * hb - 1, 0), 0)),
                  pl.BlockSpec((CHUNK, N_A), nxt),
                  pl.BlockSpec((TR, D_CONV), lambda i: (i, 0)),
                  pl.BlockSpec((CHUNK, D_CONV), nxt),
                  pl.BlockSpec((TR, D_CONV), lambda i: (i, 0)),
                  pl.BlockSpec((CHUNK, D_CONV), nxt),
                  pl.BlockSpec((CONV_WIDTH, D_CONV), lambda i: (0, 0)),
                  pl.BlockSpec((8, D_CONV), lambda i: (0, 0))],
        out_specs=[pl.BlockSpec((TR, N_A), lambda i: (i, 0)),
                   pl.BlockSpec((32, D_CONV), lambda i: (0, 0)),
                   pl.BlockSpec((8, D_CONV), lambda i: (0, 0))],
        out_shape=[jax.ShapeDtypeStruct((Lp, N_A), BF), jax.ShapeDtypeStruct((32, D_CONV), F32),
                   jax.ShapeDtypeStruct((8, D_CONV), F32)],
        scratch_shapes=[pltpu.VMEM((TR + CHUNK, D_CONV), F32), pltpu.VMEM((TR + CHUNK, D_CONV), F32),
                        pltpu.VMEM((32, 8, D_CONV), F32), pltpu.VMEM((7, _SH_ROWS, D_CONV), F32),
                        pltpu.VMEM((7, _SH_ROWS, D_CONV), F32)],
        compiler_params=_cp(("arbitrary",)),
    )(ua, ua, ua, yconv, yconv, dya, dya, cw, cvec)


def _hg_gates(ub_ref, lbv, row):
    q = ub_ref[:, 0:512].astype(F32)
    z = ub_ref[:, 512:1024].astype(F32)
    valid = row >= META_PAD
    sig = _sig(z)
    f = lbv + (1.0 - lbv) * sig
    g = jnp.where(valid, jnp.log(jnp.maximum(f, F_FLOOR)), 0.0)
    k = jnp.where(valid, (1.0 - lbv) * _sig(-z), 0.0)
    return q, k, g, sig, f


def _hg_chunk_terms(b_c, q_c, k_c):
    bm = b_c[CHUNK // 2 - 1:CHUNK // 2, :]
    bl = b_c[CHUNK - 1:CHUNK, :]
    e1 = jnp.exp(b_c - bm)
    e2 = jnp.exp(bm - b_c)
    e0 = jnp.exp(b_c)
    e3 = jnp.exp(bl - b_c)
    el = jnp.exp(bl)
    return e1, e2, e0, e3, el, q_c * e1, k_c * e2, q_c * e0, k_c * e3


def _hg_fwd(ub, lb, gn4):
    Lp = ub.shape[0]
    nt = Lp // TR
    cpt = TR // CHUNK

    def body(ub_ref, lb_ref, gn_ref, yb_ref, o_ref, ss_ref, st, bsc, qsc, ksc):
        i = pl.program_id(0)

        @pl.when(i == 0)
        def _():
            st[...] = jnp.zeros_like(st)

        row = i * TR + lax.broadcasted_iota(jnp.int32, (TR, 1), 0)
        q, k, g, _, _ = _hg_gates(ub_ref, lb_ref[...], row)
        qsc[...] = _silu(q)
        ksc[...] = k
        bsc[...] = _mm3(_chunk_tri(TR, False), g)
        tri = lax.broadcasted_iota(jnp.int32, (CHUNK, CHUNK), 1) <= lax.broadcasted_iota(jnp.int32, (CHUNK, CHUNK), 0)

        def chunk(c, carry):
            r0 = pl.multiple_of(c * CHUNK, CHUNK)
            rows = pl.ds(r0, CHUNK)
            _, _, _, _, el, qe, ke, qE, kd = _hg_chunk_terms(bsc[rows, :], qsc[rows, :], ksc[rows, :])
            qe, ke, qE, kd = qe.astype(BF), ke.astype(BF), qE.astype(BF), kd.astype(BF)
            for h in range(HG_HEADS):
                sl = slice(HG_D * h, HG_D * (h + 1))
                v_h = ub_ref[rows, 1024 + HG_D * h:1024 + HG_D * (h + 1)]
                s_in = st[h]
                ss_ref[c, h] = s_in
                a = jnp.where(tri, _nt(qe[:, sl], ke[:, sl]), 0.0)
                o_ref[rows, sl] = _nn(a.astype(BF), v_h) + _nt(qE[:, sl], s_in.astype(BF))
                st[h] = el[:, sl] * s_in + _tn(v_h, kd[:, sl])
            return carry

        lax.fori_loop(0, cpt, chunk, 0, unroll=2)

        gate = ub_ref[:, 1536:2048].astype(F32)
        for h in range(HG_HEADS):
            sl = slice(HG_D * h, HG_D * (h + 1))
            o = o_ref[:, sl]
            r = lax.rsqrt(jnp.mean(o * o, axis=-1, keepdims=True) + EPS)
            yb_ref[:, sl] = (o * r * gn_ref[:, sl] * _silu(gate[:, sl])).astype(BF)

    return pl.pallas_call(
        body, name="hgrn_fwd", grid=(nt,),
        in_specs=[pl.BlockSpec((TR, N_B), lambda i: (i, 0)), pl.BlockSpec((1, 512), lambda i: (0, 0)),
                  pl.BlockSpec((1, 512), lambda i: (0, 0))],
        out_specs=[pl.BlockSpec((TR, 512), lambda i: (i, 0)), pl.BlockSpec((TR, 512), lambda i: (i, 0)),
                   pl.BlockSpec((cpt, HG_HEADS, HG_D, HG_D), lambda i: (i, 0, 0, 0))],
        out_shape=[jax.ShapeDtypeStruct((Lp, 512), BF), jax.ShapeDtypeStruct((Lp, 512), F32),
                   jax.ShapeDtypeStruct((Lp // CHUNK, HG_HEADS, HG_D, HG_D), F32)],
        scratch_shapes=[pltpu.VMEM((HG_HEADS, HG_D, HG_D), F32), pltpu.VMEM((TR, 512), F32),
                        pltpu.VMEM((TR, 512), F32), pltpu.VMEM((TR, 512), F32)],
        compiler_params=_cp(("arbitrary",)),
    )(ub, lb, gn4)


def _hg_bwd(ub, lb, gn4, o_save, s_save, dyb, carry=None):
    Lp = ub.shape[0]
    nt = Lp // TR
    cpt = TR // CHUNK

    def body(ub_ref, lb_ref, gn_ref, o_ref, ss_ref, dy_ref, du_ref, ds_ref,
             dst, bsc, qsc, ksc, dosc, dqsc, dksc, dbsc):
        i = pl.program_id(0)
        t = nt - 1 - i

        @pl.when(i == 0)
        def _():
            dst[...] = jnp.zeros_like(dst)
            ds_ref[...] = jnp.zeros_like(ds_ref)

        lbv = lb_ref[...]
        row = t * TR + lax.broadcasted_iota(jnp.int32, (TR, 1), 0)
        valid = row >= META_PAD
        q, k, g, sig, f = _hg_gates(ub_ref, lbv, row)
        qsc[...] = _silu(q)
        ksc[...] = k
        bsc[...] = _mm3(_chunk_tri(TR, False), g)

        gate = ub_ref[:, 1536:2048].astype(F32)
        dy = dy_ref[...].astype(F32)
        dgn = jnp.zeros((1, 512), F32)
        for h in range(HG_HEADS):
            sl = slice(HG_D * h, HG_D * (h + 1))
            o = o_ref[:, sl]
            r = lax.rsqrt(jnp.mean(o * o, axis=-1, keepdims=True) + EPS)
            ohat = o * r
            don = dy[:, sl] * _silu(gate[:, sl])
            du_ref[:, 1536 + HG_D * h:1536 + HG_D * (h + 1)] = (
                dy[:, sl] * ohat * gn_ref[:, sl] * _dsilu(gate[:, sl])).astype(BF)
            ds_ref[1:2, sl] += jnp.sum(don * ohat, axis=0, keepdims=True)
            gd = don * gn_ref[:, sl]
            dosc[:, sl] = r * (gd - ohat * jnp.mean(gd * ohat, axis=-1, keepdims=True))

        tri = lax.broadcasted_iota(jnp.int32, (CHUNK, CHUNK), 1) <= lax.broadcasted_iota(jnp.int32, (CHUNK, CHUNK), 0)
        last = lax.broadcasted_iota(jnp.int32, (CHUNK, 1), 0) == CHUNK - 1

        def chunk(cc, carry):
            c = cpt - 1 - cc
            r0 = pl.multiple_of(c * CHUNK, CHUNK)
            rows = pl.ds(r0, CHUNK)
            e1, e2, e0, e3, el, qe, ke, qE, kd = _hg_chunk_terms(bsc[rows, :], qsc[rows, :], ksc[rows, :])
            qe_b, ke_b, qE_b, kd_b = qe.astype(BF), ke.astype(BF), qE.astype(BF), kd.astype(BF)
            do_c = dosc[rows, :].astype(BF)
            for h in range(HG_HEADS):
                sl = slice(HG_D * h, HG_D * (h + 1))
                v_h = ub_ref[rows, 1024 + HG_D * h:1024 + HG_D * (h + 1)]
                do_h = do_c[:, sl]
                s_in = ss_ref[c, h]
                d_s = dst[h]
                d_s_b = d_s.astype(BF)
                a = jnp.where(tri, _nt(qe_b[:, sl], ke_b[:, sl]), 0.0).astype(BF)
                da = jnp.where(tri, _nt(do_h, v_h), 0.0).astype(BF)
                dv = _tn(a, do_h) + _nt(kd_b[:, sl], d_s_b)
                dqE = _nn(do_h, s_in.astype(BF))
                dqe = _nn(da, ke_b[:, sl])
                dke = _tn(da, qe_b[:, sl])
                dkd = _nn(v_h, d_s_b)
                del_h = jnp.sum(s_in * d_s, axis=0, keepdims=True)
                dst[h] = el[:, sl] * d_s + _tn(do_h, qE_b[:, sl])
                dqsc[rows, sl] = dqE * e0[:, sl] + dqe * e1[:, sl]
                dksc[rows, sl] = dke * e2[:, sl] + dkd * e3[:, sl]
                tkd = dkd * kd[:, sl]
                dbl = jnp.sum(tkd, axis=0, keepdims=True) + del_h * el[:, sl]
                dbsc[rows, sl] = dqE * qE[:, sl] + dqe * qe[:, sl] - dke * ke[:, sl] - tkd + jnp.where(last, dbl, 0.0)
                du_ref[rows, 1024 + HG_D * h:1024 + HG_D * (h + 1)] = dv.astype(BF)
            return carry

        lax.fori_loop(0, cpt, chunk, 0, unroll=2)

        dg = _mm3(_chunk_tri(TR, True), dbsc[...])
        df = jnp.where(valid & (f > F_FLOOR), dg / f, 0.0)
        dk = jnp.where(valid, dksc[...], 0.0)
        nsig = _sig(-ub_ref[:, 512:1024].astype(F32))
        dsig = (df - dk) * (1.0 - lbv)
        ds_ref[0:1, :] += jnp.sum(df * (1.0 - sig) - dk * nsig, axis=0, keepdims=True)
        du_ref[:, 512:1024] = (dsig * sig * (1.0 - sig)).astype(BF)
        du_ref[:, 0:512] = (dqsc[...] * _dsilu(q)).astype(BF)

    rev = lambda i: (nt - 1 - i, 0)
    in_specs = [pl.BlockSpec((TR, N_B), rev), pl.BlockSpec((1, 512), lambda i: (0, 0)),
                pl.BlockSpec((1, 512), lambda i: (0, 0)), pl.BlockSpec((TR, 512), rev),
                pl.BlockSpec((cpt, HG_HEADS, HG_D, HG_D), lambda i: (nt - 1 - i, 0, 0, 0)),
                pl.BlockSpec((TR, 512), rev)]
    out_specs = [pl.BlockSpec((TR, N_B), rev), pl.BlockSpec((8, 512), lambda i: (0, 0))]
    out_shape = [jax.ShapeDtypeStruct((Lp, N_B), BF), jax.ShapeDtypeStruct((8, 512), F32)]
    scratch = [pltpu.VMEM((HG_HEADS, HG_D, HG_D), F32)] + [pltpu.VMEM((TR, 512), F32)] * 7
    return _call_carrying(body, "hgrn_bwd", nt, in_specs, out_specs, out_shape, scratch,
                          (ub, lb, gn4, o_save, s_save, dyb), carry)


_KCOL = (2 * 512) // 128
_VCOL = _KCOL + 1


def _swa_in_specs(nt, rev):
    tile = (lambda i: nt - 1 - i) if rev else (lambda i: i)
    hpt = TR // HALO
    return [
        pl.BlockSpec((TR, 512), lambda i: (tile(i), 0)),
        pl.BlockSpec((TR, 512), lambda i: (tile(i), 1)),
        pl.BlockSpec((TR, 128), lambda i: (tile(i), _KCOL)),
        pl.BlockSpec((TR, 128), lambda i: (tile(i), _VCOL)),
        pl.BlockSpec((HALO, 128), lambda i: (jnp.maximum(tile(i) * hpt - 1, 0), _KCOL)),
        pl.BlockSpec((HALO, 128), lambda i: (jnp.maximum(tile(i) * hpt - 1, 0), _VCOL)),
        pl.BlockSpec((CHUNK, 128), lambda i: (0, _KCOL)),
        pl.BlockSpec((CHUNK, 128), lambda i: (0, _VCOL)),
        pl.BlockSpec((1, 512), lambda i: (0, 0)),
        pl.BlockSpec((1, 128), lambda i: (0, 0)),
        pl.BlockSpec((1, ATT_Q_HEADS), lambda i: (0, 0)),
    ]


_WROWS = 2 * CHUNK + HALO + TR
_W0 = 2 * CHUNK
_C0 = _W0 + HALO
_SCALE = ATT_HD ** -0.5


def _group_ones(n):
    r = lax.broadcasted_iota(jnp.int32, (n, n), 0)
    c = lax.broadcasted_iota(jnp.int32, (n, n), 1)
    return jnp.where(jnp.right_shift(r, 6) == jnp.right_shift(c, 6), 1.0, 0.0).astype(BF)


def _group_mean(x, ones):
    hi = x.astype(BF)
    lo = (x - hi.astype(F32)).astype(BF)
    return (_nn(hi, ones) + _nn(lo, ones)) * (1.0 / ATT_HD)


def _head_rms(x, ones):
    r = lax.rsqrt(_group_mean(x * x, ones) + EPS)
    return x * r, r


def _swa_windows(kc_ref, vc_ref, kh_ref, vh_ref, km_ref, vm_ref, kg2, ones, kwin, krwin, vwin, vrwin):
    meta = pl.ds(META_PAD, N_META)
    for (k, v, r0, n) in ((km_ref[meta, :], vm_ref[meta, :], 0, N_META), (kh_ref[...], vh_ref[...], _W0, HALO),
                          (kc_ref[...], vc_ref[...], _C0, TR)):
        xhat, _ = _head_rms(k.astype(F32), ones)
        kn = xhat * kg2
        kwin[pl.ds(r0, n), :] = kn.astype(BF)
        krwin[pl.ds(r0, n), :] = pltpu.roll(kn, ATT_HD, 1).astype(BF)
        vwin[pl.ds(r0, n), :] = v
        if vrwin is not None:
            vrwin[pl.ds(r0, n), :] = pltpu.roll(v.astype(F32), ATT_HD, 1).astype(BF)
    zero = jnp.zeros((_W0 - N_META, 128), BF)
    for w in (kwin, krwin, vwin, vrwin):
        if w is not None:
            w[pl.ds(N_META, _W0 - N_META), :] = zero


def _swa_masks_t(t, qb):
    q0 = t * TR + qb * QB
    qc = jnp.right_shift(q0 + lax.broadcasted_iota(jnp.int32, (1, QB), 1), 6)
    kabs = q0 - HALO + lax.broadcasted_iota(jnp.int32, (QB + HALO, 1), 0)
    kc = jnp.right_shift(kabs + HALO, 6) - HALO // CHUNK
    mask_w = (kc <= qc) & (kc >= qc - 2) & (kabs >= META_PAD)
    return qc > 2, mask_w


def _swa_park(dtype):
    return [pltpu.VMEM((ATT_Q_HEADS, N_META, QB), dtype), pltpu.VMEM((ATT_Q_HEADS, QB + HALO, QB), dtype)]


def _split_heads(x, lane_hi):
    return jnp.where(lane_hi, 0.0, x).astype(BF), jnp.where(lane_hi, x, 0.0).astype(BF)


def _call_carrying(body, name, nt, in_specs, out_specs, out_shape, scratch, args, carry):
    if carry is None:
        return pl.pallas_call(body, name=name, grid=(nt,), in_specs=in_specs, out_specs=out_specs, out_shape=out_shape,
                              scratch_shapes=scratch, compiler_params=_cp(("arbitrary",)))(*args)
    kind, arrs = carry
    n = len(arrs)
    return pl.pallas_call(
        _carry_exchange(body, len(in_specs), len(out_specs), nt, kind, n), name=name + "_" + kind, grid=(nt,),
        in_specs=in_specs + [_ANY] * n, out_specs=out_specs + [_ANY] * n,
        out_shape=out_shape + _exchange_out_shapes(kind, arrs), scratch_shapes=scratch + _exchange_sems(n),
        compiler_params=_cp(("arbitrary",), has_side_effects=True),
    )(*args, *arrs)


def _swa_fwd(uc, qg8, kg2, sinks, carry=None):
    Lp = uc.shape[0]
    nt = Lp // TR
    nqb = TR // QB

    def body(q_ref, g_ref, kc_ref, vc_ref, kh_ref, vh_ref, km_ref, vm_ref, qg_ref, kg_ref, sk_ref,
             yc_ref, o_ref, lse_ref, kwin, krwin, vwin, vt, qlo, qhi, ot, s_m, s_w, p_m, p_w):
        t = pl.program_id(0)
        _swa_windows(kc_ref, vc_ref, kh_ref, vh_ref, km_ref, vm_ref, kg_ref[...], _group_ones(128),
                     kwin, krwin, vwin, None)
        vt[...] = vwin[...].T
        xhat, _ = _head_rms(q_ref[...].astype(F32), _group_ones(512))
        lane_hi = (lax.broadcasted_iota(jnp.int32, (1, 512), 1) & ATT_HD) != 0
        lo, hi = _split_heads(xhat * qg_ref[...] * _SCALE, lane_hi)
        qlo[...] = lo
        qhi[...] = hi
        for qb in range(nqb):
            rows = pl.ds(qb * QB, QB)
            wrows = pl.ds(_W0 + qb * QB, QB + HALO)
            mrows = pl.ds(0, N_META)
            mask_m, mask_w = _swa_masks_t(t, qb)
            for j in range(ATT_Q_HEADS):
                p, e = j // 2, j % 2
                ks = kwin if e == j // ATT_GROUP else krwin
                qp = (qlo, qhi)[e][rows, 128 * p:128 * (p + 1)]
                s_m[j] = _nt(ks[mrows, :], qp)
                s_w[j] = _nt(ks[wrows, :], qp)
            inv = []
            for j in range(ATT_Q_HEADS):
                sm = jnp.where(mask_m, s_m[j], NEG)
                sw = jnp.where(mask_w, s_w[j], NEG)
                sink = sk_ref[:, j:j + 1]
                m = jnp.maximum(jnp.maximum(jnp.max(sm, axis=0, keepdims=True),
                                            jnp.max(sw, axis=0, keepdims=True)), sink)
                em = jnp.exp(sm - m)
                ew = jnp.exp(sw - m)
                den = jnp.sum(em, axis=0, keepdims=True) + jnp.sum(ew, axis=0, keepdims=True) + jnp.exp(sink - m)
                p_m[j] = em.astype(BF)
                p_w[j] = ew.astype(BF)
                lse_ref[j:j + 1, pl.ds(qb * QB, QB)] = m + jnp.log(den)
                inv.append(1.0 / den)
            for j in range(ATT_Q_HEADS):
                vrows = pl.ds(ATT_HD * (j // ATT_GROUP), ATT_HD)
                ot[pl.ds(ATT_HD * j, ATT_HD), pl.ds(qb * QB, QB)] = (
                    _nn(vt[vrows, pl.ds(0, N_META)], p_m[j])
                    + _nn(vt[vrows, pl.ds(_W0 + qb * QB, QB + HALO)], p_w[j])) * inv[j]
        o = ot[...].T
        o_ref[...] = o
        yc_ref[...] = (o * _silu(g_ref[...].astype(F32))).astype(BF)

    win = pltpu.VMEM((_WROWS, 128), BF)
    in_specs = _swa_in_specs(nt, False)
    out_specs = [pl.BlockSpec((TR, 512), lambda i: (i, 0)), pl.BlockSpec((TR, 512), lambda i: (i, 0)),
                 pl.BlockSpec((ATT_Q_HEADS, TR), lambda i: (0, i))]
    out_shape = [jax.ShapeDtypeStruct((Lp, 512), BF), jax.ShapeDtypeStruct((Lp, 512), F32),
                 jax.ShapeDtypeStruct((ATT_Q_HEADS, Lp), F32)]
    scratch = [win, win, win, pltpu.VMEM((128, _WROWS), BF), pltpu.VMEM((TR, 512), BF),
               pltpu.VMEM((TR, 512), BF), pltpu.VMEM((512, TR), F32)] + _swa_park(F32) + _swa_park(BF)
    return _call_carrying(body, "swa_fwd", nt, in_specs, out_specs, out_shape, scratch,
                          (uc, uc, uc, uc, uc, uc, uc, uc, qg8, kg2, sinks), carry)


def _swa_bwd(uc, qg8, kg2, sinks, o_save, lse, dyc):
    Lp = uc.shape[0]
    nt = Lp // TR
    nqb = TR // QB

    def body(q_ref, g_ref, kc_ref, vc_ref, kh_ref, vh_ref, km_ref, vm_ref, qg_ref, kg_ref, sk_ref,
             o_ref, lse_ref, dy_ref, du_ref, dg_ref, dsk_ref,
             kwin, krwin, vwin, vrwin, kt, krt, qlo, qhi, dolo, dohi, dqt, dk_dir, dk_rol, dv_dir, dv_rol,
             carry_k, carry_v, meta_k, meta_v, s_m, s_w, dp_m, dp_w, p_m, p_w, ds_m, ds_w):
        i = pl.program_id(0)
        t = nt - 1 - i

        @pl.when(i == 0)
        def _():
            carry_k[...] = jnp.zeros_like(carry_k)
            carry_v[...] = jnp.zeros_like(carry_v)
            meta_k[...] = jnp.zeros_like(meta_k)
            meta_v[...] = jnp.zeros_like(meta_v)
            dg_ref[...] = jnp.zeros_like(dg_ref)
            dsk_ref[...] = jnp.zeros_like(dsk_ref)

        ones128 = _group_ones(128)
        ones512 = _group_ones(512)
        _swa_windows(kc_ref, vc_ref, kh_ref, vh_ref, km_ref, vm_ref, kg_ref[...], ones128, kwin, krwin, vwin, vrwin)
        kt[...] = kwin[...].T
        krt[...] = krwin[...].T
        xhat_q, r_q = _head_rms(q_ref[...].astype(F32), ones512)
        lane_hi = (lax.broadcasted_iota(jnp.int32, (1, 512), 1) & ATT_HD) != 0
        lo, hi = _split_heads(xhat_q * qg_ref[...] * _SCALE, lane_hi)
        qlo[...] = lo
        qhi[...] = hi
        gate = g_ref[...].astype(F32)
        dy = dy_ref[...].astype(F32)
        do = dy * _silu(gate)
        o = o_ref[...]
        du_ref[:, 512:1024] = (dy * o * _dsilu(gate)).astype(BF)
        lo, hi = _split_heads(do, lane_hi)
        dolo[...] = lo
        dohi[...] = hi
        hsel = jnp.where(jnp.right_shift(lax.broadcasted_iota(jnp.int32, (ATT_Q_HEADS, 512), 1), 6)
                         == lax.broadcasted_iota(jnp.int32, (ATT_Q_HEADS, 512), 0), 1.0, 0.0).astype(BF)
        prod = do * o
        p_hi = prod.astype(BF)
        d_t = _nt(hsel, p_hi) + _nt(hsel, (prod - p_hi.astype(F32)).astype(BF))
        for acc in (dk_dir, dk_rol, dv_dir, dv_rol):
            acc[...] = jnp.zeros_like(acc)

        for qb in range(nqb):
            rows = pl.ds(qb * QB, QB)
            qcols = pl.ds(qb * QB, QB)
            wrows = pl.ds(_W0 + qb * QB, QB + HALO)
            mrows = pl.ds(0, N_META)
            mask_m, mask_w = _swa_masks_t(t, qb)
            for j in range(ATT_Q_HEADS):
                p, e = j // 2, j % 2
                ks, vs = (kwin, vwin) if e == j // ATT_GROUP else (krwin, vrwin)
                pair = slice(128 * p, 128 * (p + 1))
                qp = (qlo, qhi)[e][rows, pair]
                dop = (dolo, dohi)[e][rows, pair]
                s_m[j] = _nt(ks[mrows, :], qp)
                s_w[j] = _nt(ks[wrows, :], qp)
                dp_m[j] = _nt(vs[mrows, :], dop)
                dp_w[j] = _nt(vs[wrows, :], dop)
            for j in range(ATT_Q_HEADS):
                lse_j = lse_ref[j:j + 1, qcols]
                d_j = d_t[j:j + 1, qb * QB:(qb + 1) * QB]
                em = jnp.exp(jnp.where(mask_m, s_m[j], NEG) - lse_j)
                ew = jnp.exp(jnp.where(mask_w, s_w[j], NEG) - lse_j)
                p_m[j] = em.astype(BF)
                p_w[j] = ew.astype(BF)
                ds_m[j] = (em * (dp_m[j] - d_j)).astype(BF)
                ds_w[j] = (ew * (dp_w[j] - d_j)).astype(BF)
                dsk_ref[j:j + 1, :] -= jnp.exp(sk_ref[:, j:j + 1] - lse_j) * d_j
            for j in range(ATT_Q_HEADS):
                e = j % 2
                ktr = kt if e == j // ATT_GROUP else krt
                hrows = pl.ds(ATT_HD * e, ATT_HD)
                dqt[pl.ds(ATT_HD * j, ATT_HD), qcols] = (_nn(ktr[hrows, pl.ds(0, N_META)], ds_m[j])
                                                         + _nn(ktr[hrows, pl.ds(_W0 + qb * QB, QB + HALO)], ds_w[j]))
            for direct, dk_acc, dv_acc in ((True, dk_dir, dv_dir), (False, dk_rol, dv_rol)):
                heads = [j for j in range(ATT_Q_HEADS) if (j % 2 == j // ATT_GROUP) == direct]
                q_cat = jnp.concatenate([(qlo, qhi)[j % 2][rows, 128 * (j // 2):128 * (j // 2 + 1)] for j in heads], axis=0)
                do_cat = jnp.concatenate([(dolo, dohi)[j % 2][rows, 128 * (j // 2):128 * (j // 2 + 1)] for j in heads], axis=0)
                dk_acc[mrows, :] += _nn(jnp.concatenate([ds_m[j] for j in heads], axis=1), q_cat)
                dk_acc[wrows, :] += _nn(jnp.concatenate([ds_w[j] for j in heads], axis=1), q_cat)
                dv_acc[mrows, :] += _nn(jnp.concatenate([p_m[j] for j in heads], axis=1), do_cat)
                dv_acc[wrows, :] += _nn(jnp.concatenate([p_w[j] for j in heads], axis=1), do_cat)

        dk_dir[...] += pltpu.roll(dk_rol[...], ATT_HD, 1)
        dv_dir[...] += pltpu.roll(dv_rol[...], ATT_HD, 1)
        meta_k[...] += dk_dir[pl.ds(0, N_META), :]
        meta_v[...] += dv_dir[pl.ds(0, N_META), :]
        first = jnp.where(t == 0, 1.0, 0.0)
        dk_dir[pl.ds(_C0 + TR - HALO, HALO), :] += carry_k[...]
        dv_dir[pl.ds(_C0 + TR - HALO, HALO), :] += carry_v[...]
        dk_dir[pl.ds(_C0 + META_PAD, N_META), :] += first * meta_k[...]
        dv_dir[pl.ds(_C0 + META_PAD, N_META), :] += first * meta_v[...]
        carry_k[...] = dk_dir[pl.ds(_W0, HALO), :]
        carry_v[...] = dv_dir[pl.ds(_W0, HALO), :]

        du_ref[:, 1152:1280] = dv_dir[pl.ds(_C0, TR), :].astype(BF)
        xhat_k, r_k = _head_rms(kc_ref[...].astype(F32), ones128)
        dkn = dk_dir[pl.ds(_C0, TR), :]
        dg_ref[1:2, 0:128] += jnp.sum(dkn * xhat_k, axis=0, keepdims=True)
        gd = dkn * kg_ref[...]
        du_ref[:, 1024:1152] = (r_k * (gd - xhat_k * _group_mean(gd * xhat_k, ones128))).astype(BF)
        dqn = dqt[...].T * _SCALE
        dg_ref[0:1, :] += jnp.sum(dqn * xhat_q, axis=0, keepdims=True)
        gd = dqn * qg_ref[...]
        du_ref[:, 0:512] = (r_q * (gd - xhat_q * _group_mean(gd * xhat_q, ones512))).astype(BF)

    rev = lambda i: (nt - 1 - i, 0)
    specs = _swa_in_specs(nt, True)
    win = pltpu.VMEM((_WROWS, 128), BF)
    wint = pltpu.VMEM((128, _WROWS), BF)
    tile_bf = pltpu.VMEM((TR, 512), BF)
    acc = pltpu.VMEM((_WROWS, 128), F32)
    return pl.pallas_call(
        body, name="swa_bwd", grid=(nt,),
        in_specs=specs + [pl.BlockSpec((TR, 512), rev), pl.BlockSpec((ATT_Q_HEADS, TR), lambda i: (0, nt - 1 - i)),
                          pl.BlockSpec((TR, 512), rev)],
        out_specs=[pl.BlockSpec((TR, N_C), rev), pl.BlockSpec((8, 512), lambda i: (0, 0)),
                   pl.BlockSpec((8, 128), lambda i: (0, 0))],
        out_shape=[jax.ShapeDtypeStruct((Lp, N_C), BF), jax.ShapeDtypeStruct((8, 512), F32),
                   jax.ShapeDtypeStruct((8, 128), F32)],
        scratch_shapes=[win, win, win, win, wint, wint, tile_bf, tile_bf, tile_bf, tile_bf,
                        pltpu.VMEM((512, TR), F32), acc, acc, acc, acc,
                        pltpu.VMEM((HALO, 128), F32), pltpu.VMEM((HALO, 128), F32),
                        pltpu.VMEM((N_META, 128), F32), pltpu.VMEM((N_META, 128), F32)]
        + _swa_park(F32) + _swa_park(F32) + _swa_park(BF) + _swa_park(BF),
        compiler_params=_cp(("arbitrary",)),
    )(uc, uc, uc, uc, uc, uc, uc, uc, qg8, kg2, sinks, o_save, lse, dyc)


def _mix_fwd(h, ya, yb, yc, ug, wa, wb, wc, wo):
    Lp = h.shape[0]
    wspec = lambda r: pl.BlockSpec((r, D_MODEL), lambda i: (0, 0))
    yspec = pl.BlockSpec((TRM, 512), lambda i: (i, 0))
    hspec = pl.BlockSpec((TRM, D_MODEL), lambda i: (i, 0))

    def body(h_ref, ya_ref, yb_ref, yc_ref, ug_ref, wa_ref, wb_ref, wc_ref, wo_ref, hn_ref, za_ref, zb_ref, zc_ref):
        mixed = jnp.zeros((TRM, D_MODEL), F32)
        for n, (y_ref, w_ref, z_ref) in enumerate(((ya_ref, wa_ref, za_ref), (yb_ref, wb_ref, zb_ref),
                                                   (yc_ref, wc_ref, zc_ref))):
            z = _nn(y_ref[...], w_ref[...])
            z_ref[...] = z.astype(BF)
            mixed = mixed + _sig(ug_ref[:, D_MODEL * n:D_MODEL * (n + 1)].astype(F32)) * z
        hn_ref[...] = h_ref[...] + _nn(mixed.astype(BF), wo_ref[...])

    return pl.pallas_call(
        body, name="mix_fwd", grid=(Lp // TRM,),
        in_specs=[hspec, yspec, yspec, yspec, pl.BlockSpec((TRM, N_G), lambda i: (i, 0)),
                  wspec(512), wspec(512), wspec(512), wspec(D_MODEL)],
        out_specs=[hspec, hspec, hspec, hspec],
        out_shape=[jax.ShapeDtypeStruct((Lp, D_MODEL), F32)] + [jax.ShapeDtypeStruct((Lp, D_MODEL), BF)] * 3,
        compiler_params=_cp(("parallel",)),
    )(h, ya, yb, yc, ug, wa, wb, wc, wo)


def _mix_bwd(dh, za, zb, zc, ug, wa, wb, wc, wo):
    Lp = dh.shape[0]
    wspec = lambda r: pl.BlockSpec((r, D_MODEL), lambda i: (0, 0))
    yspec = pl.BlockSpec((TRM, 512), lambda i: (i, 0))
    hspec = pl.BlockSpec((TRM, D_MODEL), lambda i: (i, 0))
    gspec = pl.BlockSpec((TRM, N_G), lambda i: (i, 0))

    def body(dh_ref, za_ref, zb_ref, zc_ref, ug_ref, wa_ref, wb_ref, wc_ref, wo_ref,
             dug_ref, mx_ref, dza_ref, dzb_ref, dzc_ref, dya_ref, dyb_ref, dyc_ref):
        dmix = _nt(dh_ref[...].astype(BF), wo_ref[...])
        mixed = jnp.zeros((TRM, D_MODEL), F32)
        for n, (z_ref, w_ref, dz_ref, dy_ref) in enumerate(((za_ref, wa_ref, dza_ref, dya_ref),
                                                            (zb_ref, wb_ref, dzb_ref, dyb_ref),
                                                            (zc_ref, wc_ref, dzc_ref, dyc_ref))):
            sl = slice(D_MODEL * n, D_MODEL * (n + 1))
            z = z_ref[...].astype(F32)
            gt = _sig(ug_ref[:, sl].astype(F32))
            mixed = mixed + gt * z
            dug_ref[:, sl] = (dmix * z * gt * (1.0 - gt)).astype(BF)
            dz = (dmix * gt).astype(BF)
            dz_ref[...] = dz
            dy_ref[...] = _nt(dz, w_ref[...]).astype(BF)
        mx_ref[...] = mixed.astype(BF)

    bf = lambda n: jax.ShapeDtypeStruct((Lp, n), BF)
    return pl.pallas_call(
        body, name="mix_bwd", grid=(Lp // TRM,),
        in_specs=[hspec, hspec, hspec, hspec, gspec, wspec(512), wspec(512), wspec(512), wspec(D_MODEL)],
        out_specs=[gspec, hspec, hspec, hspec, hspec, yspec, yspec, yspec],
        out_shape=[bf(N_G), bf(D_MODEL), bf(D_MODEL), bf(D_MODEL), bf(D_MODEL), bf(512), bf(512), bf(512)],
        compiler_params=_cp(("parallel",)),
    )(dh, za, zb, zc, ug, wa, wb, wc, wo)


def _inproj_bwd(dus, ws, h, dh, g, carry=None):
    Lp = h.shape[0]
    widths = [w.shape[1] for w in ws]

    def body(dg_ref, da_ref, db_ref, dc_ref, wg_ref, wa_ref, wb_ref, wc_ref, h_ref, dh_ref, g_ref, o_ref, gg_ref):
        @pl.when(pl.program_id(0) == 0)
        def _():
            gg_ref[...] = jnp.zeros_like(gg_ref)

        dhn = (_nt(dg_ref[...], wg_ref[...]) + _nt(da_ref[...], wa_ref[...])
               + _nt(db_ref[...], wb_ref[...]) + _nt(dc_ref[...], wc_ref[...]))
        x = h_ref[...]
        r = lax.rsqrt(jnp.mean(x * x, axis=-1, keepdims=True) + EPS)
        xhat = x * r
        gg_ref[0:1, :] += jnp.sum(dhn * xhat, axis=0, keepdims=True)
        gd = dhn * g_ref[...]
        o_ref[...] = dh_ref[...] + r * (gd - xhat * jnp.mean(gd * xhat, axis=-1, keepdims=True))

    hspec = pl.BlockSpec((TRM, D_MODEL), lambda i: (i, 0))
    in_specs = ([pl.BlockSpec((TRM, n), lambda i: (i, 0)) for n in widths]
                + [pl.BlockSpec((D_MODEL, n), lambda i: (0, 0), pipeline_mode=pl.Buffered(1)) for n in widths]
                + [hspec, hspec, pl.BlockSpec((1, D_MODEL), lambda i: (0, 0))])
    out_specs = [hspec, pl.BlockSpec((8, D_MODEL), lambda i: (0, 0))]
    out_shape = [jax.ShapeDtypeStruct((Lp, D_MODEL), F32), jax.ShapeDtypeStruct((8, D_MODEL), F32)]
    return _call_carrying(body, "inproj_bwd", Lp // TRM, in_specs, out_specs, out_shape, [],
                          (*dus, *ws, h, dh, g), carry)


def _loss_head(h, tgt_pad, seq):
    Lp = h.shape[0]
    nt = Lp // TR

    def body(h_ref, t_ref, dh_ref, l_ref):
        i = pl.program_id(0)

        @pl.when(i == 0)
        def _():
            l_ref[...] = jnp.zeros_like(l_ref)

        row = i * TR + lax.broadcasted_iota(jnp.int32, (TR, 1), 0)
        e = jnp.where((row >= CHUNK) & (row < CHUNK + seq), h_ref[...] - t_ref[...], 0.0)
        dh_ref[...] = e * (1.0 / D_MODEL)
        l_ref[...] += (0.5 / D_MODEL) * jnp.sum(jnp.sum(e * e, axis=0, keepdims=True), axis=1, keepdims=True)

    hspec = pl.BlockSpec((TR, D_MODEL), lambda i: (i, 0))
    return pl.pallas_call(
        body, name="loss_head", grid=(nt,), in_specs=[hspec, hspec],
        out_specs=[hspec, pl.BlockSpec((8, 128), lambda i: (0, 0))],
        out_shape=[jax.ShapeDtypeStruct((Lp, D_MODEL), F32), jax.ShapeDtypeStruct((8, 128), F32)],
        compiler_params=_cp(("arbitrary",)),
    )(h, tgt_pad)


def _lb_softmax(lb_ref):
    x = lb_ref[...]
    e = jnp.exp(x - jnp.max(x, axis=0, keepdims=True))
    return e / jnp.sum(e, axis=0, keepdims=True)


def _lb_fwd(hg_lb):
    def body(lb_ref, o_ref):
        sm = _lb_softmax(lb_ref)
        acc = jnp.zeros((1, 512), F32)
        for l in range(DEPTH):
            if l > 0:
                acc = acc + sm[l:l + 1, :]
            o_ref[l:l + 1, :] = jnp.clip(acc, 0.0, 1.0)

    return pl.pallas_call(body, name="lb_fwd", out_shape=jax.ShapeDtypeStruct((DEPTH, 512), F32))(hg_lb)


def _lb_bwd(hg_lb, dlb_all):
    def body(lb_ref, d_ref, o_ref):
        sm = _lb_softmax(lb_ref)
        acc = jnp.zeros((1, 512), F32)
        gm = []
        for l in range(DEPTH):
            if l > 0:
                acc = acc + sm[l:l + 1, :]
            gm.append(jnp.where((acc >= 0.0) & (acc <= 1.0), d_ref[l:l + 1, :], 0.0))
        dsm = [jnp.zeros((1, 512), F32)]
        for j in range(1, DEPTH):
            s = gm[j]
            for l in range(j + 1, DEPTH):
                s = s + gm[l]
            dsm.append(s)
        dot = dsm[0] * sm[0:1, :]
        for j in range(1, DEPTH):
            dot = dot + dsm[j] * sm[j:j + 1, :]
        for j in range(DEPTH):
            o_ref[j:j + 1, :] = sm[j:j + 1, :] * (dsm[j] - dot)

    return pl.pallas_call(body, name="lb_bwd", out_shape=jax.ShapeDtypeStruct((DEPTH, 512), F32))(hg_lb, dlb_all)


_ANY = pl.BlockSpec(memory_space=pl.ANY)


def _chip_peers():
    x, y, c = lax.axis_index("x"), lax.axis_index("y"), lax.axis_index("c")
    return (x, y, c), [(1 - x, y, c), (x, 1 - y, c), (1 - x, 1 - y, c)]


def _exchange(kind, ins, outs, send, recv, loc):
    (x, y, c), peers = _chip_peers()
    me = 2 * x + y
    ds = []
    for a in range(len(ins)):
        if kind == "gather":
            ds.append(pltpu.make_async_copy(ins[a], outs[a].at[me], loc.at[a]))
        else:
            ds.append(pltpu.make_async_copy(ins[a].at[me], outs[a].at[0], loc.at[a]))
        for p, (px, py, pc) in enumerate(peers):
            src, dst = (ins[a], outs[a].at[me]) if kind == "gather" else (ins[a].at[2 * px + py], outs[a].at[1 + p])
            ds.append(pltpu.make_async_remote_copy(src_ref=src, dst_ref=dst, send_sem=send.at[a, p],
                                                   recv_sem=recv.at[a, p], device_id=(px, py, pc), device_id_type=MESH))
    return ds


def _exchange_out_shapes(kind, arrs):
    if kind == "gather":
        return [jax.ShapeDtypeStruct((4,) + a.shape, a.dtype) for a in arrs]
    return [jax.ShapeDtypeStruct(a.shape, a.dtype) for a in arrs]


def _exchange_sems(n):
    return [pltpu.SemaphoreType.DMA((n, 3)), pltpu.SemaphoreType.DMA((n, 3)), pltpu.SemaphoreType.DMA((n,))]


def _exchange_chips(kind, arrs):
    n = len(arrs)

    def body(*refs):
        ds = _exchange(kind, refs[:n], refs[n:2 * n], *refs[2 * n:])
        for d in ds:
            d.start()
        for d in ds:
            d.wait()

    return pl.pallas_call(
        body, name=kind + "_chips", in_specs=[_ANY] * n, out_specs=[_ANY] * n,
        out_shape=_exchange_out_shapes(kind, arrs), scratch_shapes=_exchange_sems(n),
        compiler_params=pltpu.CompilerParams(has_side_effects=True),
    )(*arrs)


def _carry_exchange(body, n_in, n_out, n_steps, kind, n):
    def wrapped(*refs):
        ins, cin = refs[:n_in], refs[n_in:n_in + n]
        outs, cout = refs[n_in + n:n_in + n + n_out], refs[n_in + n + n_out:n_in + 2 * n + n_out]
        scr, sems = refs[n_in + 2 * n + n_out:-3], refs[-3:]
        i = pl.program_id(0)

        @pl.when(i == 0)
        def _():
            for d in _exchange(kind, cin, cout, *sems):
                d.start()

        body(*ins, *outs, *scr)

        @pl.when(i == n_steps - 1)
        def _():
            for d in _exchange(kind, cin, cout, *sems):
                d.wait()

    return wrapped


def _swap_cores(arrs):
    n = len(arrs)

    def body(*refs):
        ins, outs = refs[:n], refs[n:2 * n]
        send, recv = refs[2 * n:]
        x, y, c = lax.axis_index("x"), lax.axis_index("y"), lax.axis_index("c")
        rdmas = []
        for a in range(n):
            r = pltpu.make_async_remote_copy(src_ref=ins[a], dst_ref=outs[a], send_sem=send.at[a], recv_sem=recv.at[a],
                                             device_id=(x, y, 1 - c), device_id_type=MESH)
            r.start()
            rdmas.append(r)
        for r in rdmas:
            r.wait()

    return pl.pallas_call(
        body, name="swap_cores", in_specs=[_ANY] * n, out_specs=[_ANY] * n,
        out_shape=[jax.ShapeDtypeStruct(a.shape, a.dtype) for a in arrs],
        scratch_shapes=[pltpu.SemaphoreType.DMA((n,)), pltpu.SemaphoreType.DMA((n,))],
        compiler_params=pltpu.CompilerParams(has_side_effects=True),
    )(*arrs)


def _allsum_small(p):
    R = p.shape[0]

    def body(p_ref, o_ref, buf, send, recv):
        x, y, c = lax.axis_index("x"), lax.axis_index("y"), lax.axis_index("c")
        me = 4 * x + 2 * y + c
        buf[me] = p_ref[...]
        rdmas = []
        for k in range(1, 8):
            peer = (x ^ (k >> 2), y ^ ((k >> 1) & 1), c ^ (k & 1))
            r = pltpu.make_async_remote_copy(src_ref=p_ref, dst_ref=buf.at[me], send_sem=send.at[k - 1],
                                             recv_sem=recv.at[k - 1], device_id=peer, device_id_type=MESH)
            r.start()
            rdmas.append(r)
        for r in rdmas:
            r.wait()
        acc = buf[0]
        for d in range(1, 8):
            acc = acc + buf[d]
        o_ref[...] = acc

    return pl.pallas_call(
        body, name="allsum_small", out_shape=jax.ShapeDtypeStruct((R, 512), F32),
        in_specs=[pl.BlockSpec(memory_space=pltpu.VMEM)], out_specs=pl.BlockSpec(memory_space=pltpu.VMEM),
        scratch_shapes=[pltpu.VMEM((8, R, 512), F32), pltpu.SemaphoreType.DMA((7,)), pltpu.SemaphoreType.DMA((7,))],
        compiler_params=_cp(has_side_effects=True),
    )(p)


def _sum4(parts, name):
    _, R, C = parts.shape
    tr = 256 if R % 256 == 0 else R

    def body(p_ref, o_ref):
        o_ref[...] = ((p_ref[0] + p_ref[1]) + p_ref[2]) + p_ref[3]

    return pl.pallas_call(
        body, name=name, grid=(R // tr,), in_specs=[pl.BlockSpec((4, tr, C), lambda i: (0, i, 0))],
        out_specs=pl.BlockSpec((tr, C), lambda i: (i, 0)), out_shape=jax.ShapeDtypeStruct((R, C), F32),
        compiler_params=_cp(("parallel",)),
    )(parts)


def _adamw(w, m, v, g0, g1, name):
    R, C = w.shape
    tr = 256 if R % 256 == 0 else R
    two = g1 is not None
    c1 = 1.0 / (1.0 - ADAM_B1 ** ADAM_STEP)
    c2 = 1.0 / (1.0 - ADAM_B2 ** ADAM_STEP)

    def body(*refs):
        if two:
            w_ref, m_ref, v_ref, a_ref, b_ref, g_ref, d_ref, nm_ref, nv_ref = refs
            g = a_ref[...] + b_ref[...]
        else:
            w_ref, m_ref, v_ref, a_ref, g_ref, d_ref, nm_ref, nv_ref = refs
            g = a_ref[...]
        g_ref[...] = g
        m = ADAM_B1 * m_ref[...] + (1.0 - ADAM_B1) * g
        v = ADAM_B2 * v_ref[...] + (1.0 - ADAM_B2) * (g * g)
        nm_ref[...] = m
        nv_ref[...] = v
        d_ref[...] = -ADAM_LR * ((m * c1) / (jnp.sqrt(v * c2) + ADAM_EPS) + ADAM_WD * w_ref[...])

    spec = pl.BlockSpec((tr, C), lambda i: (i, 0))
    n_in = 5 if two else 4
    ins = (w, m, v, g0, g1) if two else (w, m, v, g0)
    return pl.pallas_call(
        body, name=name, grid=(R // tr,), in_specs=[spec] * n_in, out_specs=[spec] * 4,
        out_shape=[jax.ShapeDtypeStruct((R, C), F32)] * 4, compiler_params=_cp(("parallel",)),
    )(*ins)


def _pad8(a):
    r = (-a.shape[0]) % 8
    return a if r == 0 else jnp.pad(a, ((0, r), (0, 0)))


def _local_step(x, tgt, meta, P, shards=None, prep=None, pack=None):
    seq = x.shape[0]
    assert seq % TR == 0
    Lp = seq + TR
    h = jnp.concatenate([jnp.zeros((META_PAD, D_MODEL), F32), meta, x, jnp.zeros((TAIL_PAD, D_MODEL), F32)], axis=0)
    tgt_pad = jnp.pad(tgt, ((CHUNK, TAIL_PAD), (0, 0)))

    P = list(P)
    saved = []
    for l in range(DEPTH):
        p = P[l]
        hn = _rms_fwd(h, p["norm_g"])
        mm = functools.partial(_matmul, out_dtype=BF, tm=TR, tk=D_MODEL, col_major_grid=True)
        ug = mm(hn, p["w_g"], tn=N_G // 2, name="inproj_g")
        ua = mm(hn, p["w_a"], tn=N_A, name="inproj_a")
        ub = mm(hn, p["w_b"], tn=N_B, name="inproj_b")
        uc = mm(hn, p["w_c"], tn=N_C, name="inproj_c")
        ya, yconv = _conv_fwd(ua, p["conv_w"], p["conv_vec"])
        yb, o_hg, s_hg = _hg_fwd(ub, p["lb"], p["gn4"])
        carry = ("gather", shards[l + 1]) if shards is not None and l + 1 < DEPTH else None
        res = _swa_fwd(uc, p["qg"], p["kg"], p["sinks"], carry)
        yc, o_at, lse = res[:3]
        if carry is not None:
            P.append(prep(l + 1, res[3:]))
        h_new, za, zb, zc = _mix_fwd(h, ya, yb, yc, ug, p["w_ao"], p["w_bo"], p["w_co"], p["w_out"])
        saved.append(dict(h=h, hn=hn, ug=ug, ua=ua, ub=ub, uc=uc, ya=ya, yconv=yconv, yb=yb, o_hg=o_hg, s_hg=s_hg,
                          yc=yc, o_at=o_at, lse=lse, za=za, zb=zb, zc=zc))
        h = h_new

    dh, loss8 = _loss_head(h, tgt_pad, seq)

    grads = [None] * DEPTH
    parts = [[None, None] for _ in range(DEPTH)]
    pending = None
    tk_dw = 3 * TR if Lp % (3 * TR) == 0 else TR
    for l in reversed(range(DEPTH)):
        p, s = P[l], saved[l]
        dug, mixed, dza, dzb, dzc, dya, dyb, dyc = _mix_bwd(dh, s["za"], s["zb"], s["zc"], s["ug"],
                                                             p["w_ao"], p["w_bo"], p["w_co"], p["w_out"])
        tnmm = functools.partial(_matmul, ta=True, out_dtype=F32, tk=tk_dw)
        g = {}
        g["w_out"] = tnmm(mixed, dh, tm=D_MODEL, tn=D_MODEL, name="dw_out")
        g["w_ao"] = tnmm(s["ya"], dza, tm=512, tn=D_MODEL, name="dw_ao")
        g["w_bo"] = tnmm(s["yb"], dzb, tm=512, tn=D_MODEL, name="dw_bo")
        g["w_co"] = tnmm(s["yc"], dzc, tm=512, tn=D_MODEL, name="dw_co")
        dua, g["conv_w"], g["conv_vec"] = _conv_bwd(s["ua"], s["yconv"], dya, p["conv_w"], p["conv_vec"])
        carry = ("scatter", pending[1]) if pending is not None else None
        res = _hg_bwd(s["ub"], p["lb"], p["gn4"], s["o_hg"], s["s_hg"], dyb, carry)
        dub, g["hg_small"] = res[:2]
        if carry is not None:
            parts[pending[0]][1] = res[2:]
        duc, g["at_gain"], g["at_sink"] = _swa_bwd(s["uc"], p["qg"], p["kg"], p["sinks"], s["o_at"], s["lse"], dyc)
        g["w_g"] = tnmm(s["hn"], dug, tm=D_MODEL, tn=N_G // 2, name="dw_in_g")
        g["w_a"] = tnmm(s["hn"], dua, tm=D_MODEL, tn=N_A, name="dw_in_a")
        g["w_b"] = tnmm(s["hn"], dub, tm=D_MODEL, tn=N_B, name="dw_in_b")
        g["w_c"] = tnmm(s["hn"], duc, tm=D_MODEL, tn=N_C, name="dw_in_c")
        first, second = pack(g) if pack is not None else (None, None)
        res = _inproj_bwd([dug, dua, dub, duc], [p["w_g"], p["w_a"], p["w_b"], p["w_c"]], s["h"], dh, p["norm_g"],
                          ("scatter", first) if first is not None else None)
        dh, g["norm_g"] = res[:2]
        grads[l] = g
        if pack is not None:
            parts[l][0] = res[2:]
            pending = (l, second)
    if pending is not None:
        parts[pending[0]][1] = _exchange_chips("scatter", pending[1])
    return loss8, dh, grads, parts


def _split_w_in(w):
    return dict(w_a=w[:, 0:1536], w_b=w[:, 1536:3584],
                w_c=jnp.concatenate([w[:, 3584:4096], w[:, 4352:4864], w[:, 4096:4352]], axis=1),
                w_g=w[:, 4864:7936])


def _join_w_in(g):
    c = g["w_c"]
    return jnp.concatenate([g["w_a"], g["w_b"], c[:, 0:512], c[:, 1024:1280], c[:, 512:1024], g["w_g"]], axis=1)


def _attn_small(g):
    return (g["at_gain"][0].reshape(ATT_Q_HEADS, ATT_HD).sum(0),
            g["at_gain"][1, 0:128].reshape(ATT_KV_HEADS, ATT_HD).sum(0), g["at_sink"].sum(1))


_SMALL = (("norm_g", 8), ("meta", 32), ("conv_w", 32 * DEPTH), ("conv_b", 8), ("conv_ln_g", 8), ("conv_ln_b", 8),
          ("lb", 8), ("hg_norm_g", 8), ("q_norm_g", 8), ("k_norm_g", 8), ("sinks", 8))


def _small_offsets():
    off, o = {}, 0
    for name, rows in _SMALL:
        off[name] = (o, rows)
        o += rows
    return off, o


def _pack_small(d):
    parts = []
    for name, rows in _SMALL:
        a = d[name]
        parts.append(jnp.pad(a, ((0, rows - a.shape[0]), (0, 512 - a.shape[1]))))
    return jnp.concatenate(parts, axis=0)


def kernel(x, meta_tokens, norm_g, w_in, conv_w, conv_b, conv_ln_g, conv_ln_b, w_conv_out, hg_lower_bounds, hg_norm_g, w_hg_out, q_norm_g, k_norm_g, attn_sinks, w_att_out, w_out, loss_target, m_meta_tokens, m_norm_g, m_w_in, m_conv_w, m_conv_b, m_conv_ln_g, m_conv_ln_b, m_w_conv_out, m_hg_lower_bounds, m_hg_norm_g, m_w_hg_out, m_q_norm_g, m_k_norm_g, m_attn_sinks, m_w_att_out, m_w_out, v_meta_tokens, v_norm_g, v_w_in, v_conv_w, v_conv_b, v_conv_ln_g, v_conv_ln_b, v_w_conv_out, v_hg_lower_bounds, v_hg_norm_g, v_w_hg_out, v_q_norm_g, v_k_norm_g, v_attn_sinks, v_w_att_out, v_w_out):
    xi, yi = lax.axis_index("x"), lax.axis_index("y")
    chip = 2 * xi + yi
    NS = w_in.shape[2]
    CS = conv_w.shape[2]
    MS = meta_tokens.shape[1]

    shards = [[w_in[l].astype(BF), w_conv_out[l].astype(BF), w_hg_out[l].astype(BF), w_att_out[l].astype(BF),
               w_out[l].astype(BF)] for l in range(DEPTH)]
    *first, g_meta, g_convw = _exchange_chips(
        "gather", shards[0] + [meta_tokens, conv_w.reshape(DEPTH * CONV_WIDTH, CS)])
    cols = lambda g: g.transpose(1, 0, 2).reshape(g.shape[1], -1)
    meta_f = cols(g_meta)
    convw_f = cols(g_convw).reshape(DEPTH, CONV_WIDTH, D_CONV)
    lb_all = _lb_fwd(hg_lower_bounds)

    def prep(l, gathered):
        g_win, g_wao, g_wbo, g_wco, g_wout = gathered
        p = _split_w_in(cols(g_win))
        p.update(w_ao=cols(g_wao), w_bo=cols(g_wbo), w_co=cols(g_wco), w_out=g_wout.reshape(D_MODEL, D_MODEL),
                 norm_g=norm_g[l:l + 1], conv_w=convw_f[l],
                 conv_vec=_pad8(jnp.stack([conv_b[l], conv_ln_g[l], conv_ln_b[l]])),
                 lb=lb_all[l:l + 1], gn4=jnp.tile(hg_norm_g[l:l + 1], (1, HG_HEADS)),
                 qg=jnp.tile(q_norm_g[l:l + 1], (1, ATT_Q_HEADS)), kg=jnp.tile(k_norm_g[l:l + 1], (1, ATT_KV_HEADS)),
                 sinks=attn_sinks[l:l + 1])
        return p

    shard_cols = lambda a: a.reshape(a.shape[0], 4, -1).transpose(1, 0, 2)
    half = D_MODEL // 2

    def pack(g):
        win = shard_cols(_join_w_in(g))
        return [win[:, :half]], [win[:, half:], shard_cols(g["w_ao"]), shard_cols(g["w_bo"]), shard_cols(g["w_co"]),
                                 g["w_out"].reshape(4, MS, D_MODEL)]

    loss8, dh0, grads, parts = _local_step(x[0], loss_target[0], meta_f, [prep(0, first)], shards, prep, pack)
    seq = x.shape[1]
    grad_x = dh0[CHUNK:CHUNK + seq][None]
    loss = lax.psum(loss8[0, 0], ("x", "y", "c"))

    sum4 = functools.partial(_sum4, name="sum_chips")
    mine = [jnp.concatenate([t for l in range(DEPTH) for t in (sum4(parts[l][0][0]), sum4(parts[l][1][0]))], axis=0)]
    mine += [jnp.concatenate([sum4(parts[l][1][a]) for l in range(DEPTH)], axis=0) for a in range(1, 5)]
    theirs = _swap_cores(mine)

    dlb_all = jnp.concatenate([grads[l]["hg_small"][0:1] for l in range(DEPTH)], axis=0)
    small = dict(
        norm_g=jnp.concatenate([grads[l]["norm_g"][0:1] for l in range(DEPTH)], axis=0).reshape(8, 512),
        meta=dh0[META_PAD:CHUNK].reshape(32, 512),
        conv_w=jnp.concatenate([grads[l]["conv_w"] for l in range(DEPTH)], axis=0),
        conv_b=jnp.concatenate([grads[l]["conv_vec"][0:1] for l in range(DEPTH)], axis=0),
        conv_ln_g=jnp.concatenate([grads[l]["conv_vec"][1:2] for l in range(DEPTH)], axis=0),
        conv_ln_b=jnp.concatenate([grads[l]["conv_vec"][2:3] for l in range(DEPTH)], axis=0),
        lb=_lb_bwd(hg_lower_bounds, dlb_all),
        hg_norm_g=jnp.concatenate([grads[l]["hg_small"][1:2].reshape(HG_HEADS, HG_D).sum(0, keepdims=True)
                                   for l in range(DEPTH)], axis=0),
        q_norm_g=jnp.stack([_attn_small(grads[l])[0] for l in range(DEPTH)]),
        k_norm_g=jnp.stack([_attn_small(grads[l])[1] for l in range(DEPTH)]),
        sinks=jnp.stack([_attn_small(grads[l])[2] for l in range(DEPTH)]),
    )
    gsum = _allsum_small(_pack_small(small))
    off, _ = _small_offsets()

    def take(name, rows, cols):
        o, _ = off[name]
        return gsum[o:o + rows, 0:cols]

    g_meta_full = take("meta", 32, 512).reshape(N_META, D_MODEL)
    g_convw_full = take("conv_w", 32 * DEPTH, 512).reshape(DEPTH, 32, 512)[:, :CONV_WIDTH]
    small_grads = dict(
        norm_g=take("norm_g", 8, 512),
        meta=lax.dynamic_slice_in_dim(g_meta_full, chip * MS, MS, axis=1),
        conv_w=lax.dynamic_slice_in_dim(g_convw_full, chip * CS, CS, axis=2).reshape(DEPTH * CONV_WIDTH, CS),
        conv_b=take("conv_b", DEPTH, 512), conv_ln_g=take("conv_ln_g", DEPTH, 512), conv_ln_b=take("conv_ln_b", DEPTH, 512),
        lb=take("lb", DEPTH, 512), hg_norm_g=take("hg_norm_g", DEPTH, HG_D), q_norm_g=take("q_norm_g", DEPTH, ATT_HD),
        k_norm_g=take("k_norm_g", DEPTH, ATT_HD), sinks=take("sinks", DEPTH, ATT_Q_HEADS))

    def big_update(w, m, v, a, b, name):
        shp = w.shape
        r2 = lambda t: t.reshape(-1, shp[-1])
        outs = _adamw(r2(w), r2(m), r2(v), a, b, name)
        return [o.reshape(shp) for o in outs]

    res = {}
    res["w_in"] = big_update(w_in, m_w_in, v_w_in, mine[0], theirs[0], "adamw_w_in")
    res["w_conv_out"] = big_update(w_conv_out, m_w_conv_out, v_w_conv_out, mine[1], theirs[1], "adamw_w_ao")
    res["w_hg_out"] = big_update(w_hg_out, m_w_hg_out, v_w_hg_out, mine[2], theirs[2], "adamw_w_bo")
    res["w_att_out"] = big_update(w_att_out, m_w_att_out, v_w_att_out, mine[3], theirs[3], "adamw_w_co")
    res["w_out"] = big_update(w_out, m_w_out, v_w_out, mine[4], theirs[4], "adamw_w_out")

    small_w = dict(meta=(meta_tokens, m_meta_tokens, v_meta_tokens), norm_g=(norm_g, m_norm_g, v_norm_g),
                   conv_w=(conv_w, m_conv_w, v_conv_w), conv_b=(conv_b, m_conv_b, v_conv_b),
                   conv_ln_g=(conv_ln_g, m_conv_ln_g, v_conv_ln_g), conv_ln_b=(conv_ln_b, m_conv_ln_b, v_conv_ln_b),
                   lb=(hg_lower_bounds, m_hg_lower_bounds, v_hg_lower_bounds),
                   hg_norm_g=(hg_norm_g, m_hg_norm_g, v_hg_norm_g), q_norm_g=(q_norm_g, m_q_norm_g, v_q_norm_g),
                   k_norm_g=(k_norm_g, m_k_norm_g, v_k_norm_g), sinks=(attn_sinks, m_attn_sinks, v_attn_sinks))
    view = lambda n, t: t.reshape(-1, 512) if n == "norm_g" else t.reshape(-1, t.shape[-1])
    pw, pm, pv = (_pack_rows([view(n, small_w[n][k]) for n in small_w]) for k in range(3))
    pg = _pack_rows([small_grads[n] for n in small_w])
    packed = _adamw(pw, pm, pv, pg, None, "adamw_small")
    o = 0
    for n in small_w:
        r, cdim = view(n, small_w[n][0]).shape
        res[n] = [t[o:o + r, 0:cdim].reshape(small_w[n][0].shape) for t in packed]
        o += -(-r // 8) * 8

    order = [("meta", None), ("norm_g", None), ("w_in", None), ("conv_w", None), ("conv_b", None), ("conv_ln_g", None),
             ("conv_ln_b", None), ("w_conv_out", None), ("lb", None), ("hg_norm_g", None), ("w_hg_out", None),
             ("q_norm_g", None), ("k_norm_g", None), ("sinks", None), ("w_att_out", None), ("w_out", None)]
    outs = [loss, grad_x]
    for k in range(4):
        outs += [res[n][k] for n, _ in order]
    return tuple(outs)


def _pack_rows(arrs):
    parts = []
    for a in arrs:
        r = (-a.shape[0]) % 8
        parts.append(jnp.pad(a, ((0, r), (0, 512 - a.shape[1]))))
    return jnp.concatenate(parts, axis=0)
```

```python
import functools

import jax
import jax.numpy as jnp
from jax import lax
from jax.experimental import pallas as pl
from jax.experimental.pallas import tpu as pltpu

F32 = jnp.float32
BF = jnp.bfloat16

D_MODEL = 1024
DEPTH = 4
CHUNK = 64
N_META = 16
META_PAD = CHUNK - N_META
D_CONV = 512
CONV_WIDTH = 31
HG_HEADS = 4
HG_D = 128
ATT_Q_HEADS = 8
ATT_KV_HEADS = 2
ATT_HD = 64
ATT_GROUP = ATT_Q_HEADS // ATT_KV_HEADS
EPS = 1e-6
F_FLOOR = 1e-30
NEG = -1e30

ADAM_LR = 0.001
ADAM_B1 = 0.9
ADAM_B2 = 0.999
ADAM_EPS = 1e-08
ADAM_WD = 0.01
ADAM_STEP = 10

TR = 512
TRM = 256
CONV_RB = 32
QB = 128
HALO = 128
TAIL_PAD = TR - CHUNK
VMEM_LIMIT = 56 * 1024 * 1024

N_G, N_A, N_B, N_C = 3 * D_MODEL, 3 * D_CONV, 4 * 512, 2 * 512 + 2 * 128

MESH = pl.DeviceIdType.MESH


def _cp(sem=None, vmem=VMEM_LIMIT, **kw):
    if sem is None:
        return pltpu.CompilerParams(vmem_limit_bytes=vmem, **kw)
    return pltpu.CompilerParams(dimension_semantics=sem, vmem_limit_bytes=vmem, **kw)


def _nn(a, b):
    return lax.dot_general(a, b, (((1,), (0,)), ((), ())), preferred_element_type=F32)


def _nt(a, b):
    return lax.dot_general(a, b, (((1,), (1,)), ((), ())), preferred_element_type=F32)


def _tn(a, b):
    return lax.dot_general(a, b, (((0,), (0,)), ((), ())), preferred_element_type=F32)


def _sig(x):
    return jax.nn.sigmoid(x)


def _silu(x):
    return x * _sig(x)


def _dsilu(x):
    s = _sig(x)
    return s * (1.0 + x * (1.0 - s))


def _split3(x):
    hi = x.astype(BF)
    r1 = x - hi.astype(F32)
    mid = r1.astype(BF)
    lo = (r1 - mid.astype(F32)).astype(BF)
    return hi, mid, lo


def _mm3(t, x):
    hi, mid, lo = _split3(x)
    return _nn(t, hi) + _nn(t, mid) + _nn(t, lo)


def _chunk_tri(n, upper):
    r = lax.broadcasted_iota(jnp.int32, (n, n), 0)
    c = lax.broadcasted_iota(jnp.int32, (n, n), 1)
    same = jnp.right_shift(r, 6) == jnp.right_shift(c, 6)
    tri = (c >= r) if upper else (c <= r)
    return jnp.where(same & tri, 1.0, 0.0).astype(BF)


def _matmul(a, b, *, ta=False, tb=False, out_dtype, tm, tn, tk, name, col_major_grid=False):
    if ta:
        K, M = a.shape
    else:
        M, K = a.shape
    N = b.shape[0] if tb else b.shape[1]
    assert M % tm == 0 and N % tn == 0 and K % tk == 0, (name, M, N, K, tm, tn, tk)
    nk = K // tk
    if col_major_grid:
        grid = (N // tn, M // tm, nk)
        ij = lambda g0, g1: (g1, g0)
    else:
        grid = (M // tm, N // tn, nk)
        ij = lambda g0, g1: (g0, g1)
    if ta:
        a_spec = pl.BlockSpec((tk, tm), lambda g0, g1, k: (k, ij(g0, g1)[0]))
    else:
        a_spec = pl.BlockSpec((tm, tk), lambda g0, g1, k: (ij(g0, g1)[0], k))
    if tb:
        b_spec = pl.BlockSpec((tn, tk), lambda g0, g1, k: (ij(g0, g1)[1], k))
    else:
        b_spec = pl.BlockSpec((tk, tn), lambda g0, g1, k: (k, ij(g0, g1)[1]))
    o_spec = pl.BlockSpec((tm, tn), lambda g0, g1, k: ij(g0, g1))
    dims = (((0 if ta else 1,), (1 if tb else 0,)), ((), ()))
    use_acc = nk > 1 and out_dtype != F32

    def body(a_ref, b_ref, o_ref, *scr):
        k = pl.program_id(2)
        p = lax.dot_general(a_ref[...].astype(BF), b_ref[...].astype(BF), dims, preferred_element_type=F32)
        if nk == 1:
            o_ref[...] = p.astype(out_dtype)
        else:
            acc = scr[0] if use_acc else o_ref

            @pl.when(k == 0)
            def _():
                acc[...] = p

            @pl.when(k > 0)
            def _():
                acc[...] += p

            if use_acc:
                @pl.when(k == nk - 1)
                def _():
                    o_ref[...] = acc[...].astype(out_dtype)

    return pl.pallas_call(
        body, name=name, grid=grid, in_specs=[a_spec, b_spec], out_specs=o_spec,
        out_shape=jax.ShapeDtypeStruct((M, N), out_dtype),
        scratch_shapes=[pltpu.VMEM((tm, tn), F32)] if use_acc else [],
        compiler_params=_cp(("parallel", "parallel", "arbitrary")),
    )(a, b)


def _rms_fwd(h, g):
    Lp = h.shape[0]

    def body(h_ref, g_ref, o_ref):
        x = h_ref[...]
        r = lax.rsqrt(jnp.mean(x * x, axis=-1, keepdims=True) + EPS)
        o_ref[...] = (x * r * g_ref[...]).astype(BF)

    return pl.pallas_call(
        body, name="rms_fwd", grid=(Lp // TR,),
        in_specs=[pl.BlockSpec((TR, D_MODEL), lambda i: (i, 0)), pl.BlockSpec((1, D_MODEL), lambda i: (0, 0))],
        out_specs=pl.BlockSpec((TR, D_MODEL), lambda i: (i, 0)),
        out_shape=jax.ShapeDtypeStruct((Lp, D_MODEL), BF),
        compiler_params=_cp(("parallel",)),
    )(h, g)


def _glu(ua, row):
    a = ua[:, 0:D_CONV].astype(F32)
    gl = ua[:, D_CONV:2 * D_CONV].astype(F32)
    return jnp.where(row >= META_PAD, a * _sig(gl), 0.0)


_SH_ROWS = TR + CHUNK - 8


def _fill_shifts(src, sh):
    for b in range(1, 8):
        sh[b - 1] = src[pl.ds(b, _SH_ROWS), :]


def _shifted(src, sh, start, n):
    b = start % 8
    if b == 0:
        return src[pl.ds(start, n), :]
    return sh[b - 1, pl.ds(start - b, n), :]


def _conv_fwd(ua, cw, cvec, carry=None):
    Lp = ua.shape[0]
    nt = Lp // TR
    hb = TR // CHUNK

    def body(cur_ref, halo_ref, w_ref, v_ref, ya_ref, yc_ref, ext, sh):
        i = pl.program_id(0)
        row = i * TR + lax.broadcasted_iota(jnp.int32, (TR, 1), 0)
        hrow = i * TR - CHUNK + lax.broadcasted_iota(jnp.int32, (CHUNK, 1), 0)
        ext[pl.ds(0, CHUNK), :] = jnp.where(i > 0, _glu(halo_ref[...], hrow), 0.0)
        ext[pl.ds(CHUNK, TR), :] = _glu(cur_ref[...], row)
        _fill_shifts(ext, sh)
        for rb in range(TR // CONV_RB):
            r0 = rb * CONV_RB
            rows = pl.ds(r0, CONV_RB)
            acc = jnp.zeros((CONV_RB, D_CONV), F32)
            for j in range(CONV_WIDTH):
                acc = acc + _shifted(ext, sh, r0 + CHUNK - (CONV_WIDTH - 1) + j, CONV_RB) * w_ref[j:j + 1, :]
            y = acc + v_ref[0:1, :]
            yc_ref[rows, :] = y
            mu = jnp.mean(y, axis=-1, keepdims=True)
            d = y - mu
            var = jnp.mean(d * d, axis=-1, keepdims=True)
            yn = d * lax.rsqrt(var + EPS) * v_ref[1:2, :] + v_ref[2:3, :]
            ya_ref[rows, :] = (_silu(yn) * _silu(cur_ref[rows, 2 * D_CONV:3 * D_CONV].astype(F32))).astype(BF)

    in_specs = [pl.BlockSpec((TR, N_A), lambda i: (i, 0)),
                pl.BlockSpec((CHUNK, N_A), lambda i: (jnp.maximum(i * hb - 1, 0), 0)),
                pl.BlockSpec((CONV_WIDTH, D_CONV), lambda i: (0, 0)),
                pl.BlockSpec((8, D_CONV), lambda i: (0, 0))]
    out_specs = [pl.BlockSpec((TR, D_CONV), lambda i: (i, 0)), pl.BlockSpec((TR, D_CONV), lambda i: (i, 0))]
    out_shape = [jax.ShapeDtypeStruct((Lp, D_CONV), BF), jax.ShapeDtypeStruct((Lp, D_CONV), F32)]
    scratch = [pltpu.VMEM((TR + CHUNK, D_CONV), F32), pltpu.VMEM((7, _SH_ROWS, D_CONV), F32)]
    return _call_carrying(body, "conv_fwd", nt, in_specs, out_specs, out_shape, scratch, (ua, ua, cw, cvec), carry)


def _conv_bwd(ua, yconv, dya, cw, cvec):
    Lp = ua.shape[0]
    nt = Lp // TR
    hb = TR // CHUNK
    nhb = Lp // CHUNK

    def ln_bwd(y, dout, gate, v_ref):
        mu = jnp.mean(y, axis=-1, keepdims=True)
        d = y - mu
        var = jnp.mean(d * d, axis=-1, keepdims=True)
        rstd = lax.rsqrt(var + EPS)
        xhat = d * rstd
        yn = xhat * v_ref[1:2, :] + v_ref[2:3, :]
        dyn = dout * _silu(gate) * _dsilu(yn)
        dxh = dyn * v_ref[1:2, :]
        dyc = rstd * (dxh - jnp.mean(dxh, axis=-1, keepdims=True) - xhat * jnp.mean(dxh * xhat, axis=-1, keepdims=True))
        return dyc, dyn, xhat, yn

    def body(cur_ref, prev_ref, next_ref, yc_ref, ycn_ref, dy_ref, dyn_ref, w_ref, v_ref,
             du_ref, dw_ref, dv_ref, uext, dext, dwacc, ush, dsh):
        i = pl.program_id(0)

        @pl.when(i == 0)
        def _():
            dwacc[...] = jnp.zeros_like(dwacc)
            dv_ref[...] = jnp.zeros_like(dv_ref)

        row = i * TR + lax.broadcasted_iota(jnp.int32, (TR, 1), 0)
        hrow = i * TR - CHUNK + lax.broadcasted_iota(jnp.int32, (CHUNK, 1), 0)
        uext[pl.ds(0, CHUNK), :] = jnp.where(i > 0, _glu(prev_ref[...], hrow), 0.0)
        uext[pl.ds(CHUNK, TR), :] = _glu(cur_ref[...], row)

        s_b = jnp.zeros((1, D_CONV), F32)
        s_g = jnp.zeros((1, D_CONV), F32)
        s_bb = jnp.zeros((1, D_CONV), F32)
        for rb in range(TR // CONV_RB):
            rows = pl.ds(rb * CONV_RB, CONV_RB)
            gate = cur_ref[rows, 2 * D_CONV:3 * D_CONV].astype(F32)
            dout = dy_ref[rows, :].astype(F32)
            dyc, dyn, xhat, yn = ln_bwd(yc_ref[rows, :], dout, gate, v_ref)
            du_ref[rows, 2 * D_CONV:3 * D_CONV] = (dout * _silu(yn) * _dsilu(gate)).astype(BF)
            dext[rows, :] = dyc
            s_b = s_b + jnp.sum(dyc, axis=0, keepdims=True)
            s_g = s_g + jnp.sum(dyn * xhat, axis=0, keepdims=True)
            s_bb = s_bb + jnp.sum(dyn, axis=0, keepdims=True)
        dv_ref[0:1, :] += s_b
        dv_ref[1:2, :] += s_g
        dv_ref[2:3, :] += s_bb
        dyc_n, _, _, _ = ln_bwd(ycn_ref[...], dyn_ref[...].astype(F32),
                                next_ref[:, 2 * D_CONV:3 * D_CONV].astype(F32), v_ref)
        dext[pl.ds(TR, CHUNK), :] = jnp.where(i < nt - 1, dyc_n, 0.0)
        _fill_shifts(uext, ush)
        _fill_shifts(dext, dsh)

        for rb in range(TR // CONV_RB):
            r0 = rb * CONV_RB
            rows = pl.ds(r0, CONV_RB)
            d_blk = dext[rows, :]
            dglu = jnp.zeros((CONV_RB, D_CONV), F32)
            for j in range(CONV_WIDTH):
                dglu = dglu + _shifted(dext, dsh, r0 + CONV_WIDTH - 1 - j, CONV_RB) * w_ref[j:j + 1, :]
                prod = d_blk * _shifted(uext, ush, r0 + CHUNK - (CONV_WIDTH - 1) + j, CONV_RB)
                part = prod[0:8, :]
                for s in range(1, CONV_RB // 8):
                    part = part + prod[8 * s:8 * s + 8, :]
                dwacc[j] += part
            a = cur_ref[rows, 0:D_CONV].astype(F32)
            sg = _sig(cur_ref[rows, D_CONV:2 * D_CONV].astype(F32))
            grow = i * TR + r0 + lax.broadcasted_iota(jnp.int32, (CONV_RB, 1), 0)
            dglu = jnp.where(grow >= META_PAD, dglu, 0.0)
            du_ref[rows, 0:D_CONV] = (dglu * sg).astype(BF)
            du_ref[rows, D_CONV:2 * D_CONV] = (dglu * a * sg * (1.0 - sg)).astype(BF)

        @pl.when(i == nt - 1)
        def _():
            dw_ref[...] = jnp.sum(dwacc[...], axis=1)

    nxt = lambda i: (jnp.minimum(i * hb + hb, nhb - 1), 0)
    return pl.pallas_call(
        body, name="conv_bwd", grid=(nt,),
        in_specs=[pl.BlockSpec((TR, N_A), lambda i: (i, 0)),
                  pl.BlockSpec((CHUNK, N_A), lambda i: (jnp.maximum(i * hb - 1, 0), 0)),
                  pl.BlockSpec((CHUNK, N_A), nxt),
                  pl.BlockSpec((TR, D_CONV), lambda i: (i, 0)),
                  pl.BlockSpec((CHUNK, D_CONV), nxt),
                  pl.BlockSpec((TR, D_CONV), lambda i: (i, 0)),
                  pl.BlockSpec((CHUNK, D_CONV), nxt),
                  pl.BlockSpec((CONV_WIDTH, D_CONV), lambda i: (0, 0)),
                  pl.BlockSpec((8, D_CONV), lambda i: (0, 0))],
        out_specs=[pl.BlockSpec((TR, N_A), lambda i: (i, 0)),
                   pl.BlockSpec((32, D_CONV), lambda i: (0, 0)),
                   pl.BlockSpec((8, D_CONV), lambda i: (0, 0))],
        out_shape=[jax.ShapeDtypeStruct((Lp, N_A), BF), jax.ShapeDtypeStruct((32, D_CONV), F32),
                   jax.ShapeDtypeStruct((8, D_CONV), F32)],
        scratch_shapes=[pltpu.VMEM((TR + CHUNK, D_CONV), F32), pltpu.VMEM((TR + CHUNK, D_CONV), F32),
                        pltpu.VMEM((32, 8, D_CONV), F32), pltpu.VMEM((7, _SH_ROWS, D_CONV), F32),
                        pltpu.VMEM((7, _SH_ROWS, D_CONV), F32)],
        compiler_params=_cp(("arbitrary",)),
    )(ua, ua, ua, yconv, yconv, dya, dya, cw, cvec)


def _hg_gates(ub_ref, lbv, row):
    q = ub_ref[:, 0:512].astype(F32)
    z = ub_ref[:, 512:1024].astype(F32)
    valid = row >= META_PAD
    sig = _sig(z)
    f = lbv + (1.0 - lbv) * sig
    g = jnp.where(valid, jnp.log(jnp.maximum(f, F_FLOOR)), 0.0)
    k = jnp.where(valid, (1.0 - lbv) * _sig(-z), 0.0)
    return q, k, g, sig, f


def _hg_chunk_terms(b_c, q_c, k_c):
    bm = b_c[CHUNK // 2 - 1:CHUNK // 2, :]
    bl = b_c[CHUNK - 1:CHUNK, :]
    e1 = jnp.exp(b_c - bm)
    e2 = jnp.exp(bm - b_c)
    e0 = jnp.exp(b_c)
    e3 = jnp.exp(bl - b_c)
    el = jnp.exp(bl)
    return e1, e2, e0, e3, el, q_c * e1, k_c * e2, q_c * e0, k_c * e3


def _hg_fwd(ub, lb, gn4, carry=None):
    Lp = ub.shape[0]
    nt = Lp // TR
    cpt = TR // CHUNK

    def body(ub_ref, lb_ref, gn_ref, yb_ref, o_ref, ss_ref, st, bsc, qsc, ksc, qes, els, ust):
        i = pl.program_id(0)

        @pl.when(i == 0)
        def _():
            st[...] = jnp.zeros_like(st)

        row = i * TR + lax.broadcasted_iota(jnp.int32, (TR, 1), 0)
        q, k, g, _, _ = _hg_gates(ub_ref, lb_ref[...], row)
        qsc[...] = _silu(q)
        ksc[...] = k
        bsc[...] = _mm3(_chunk_tri(TR, False), g)
        tri = lax.broadcasted_iota(jnp.int32, (CHUNK, CHUNK), 1) <= lax.broadcasted_iota(jnp.int32, (CHUNK, CHUNK), 0)

        def intra(c, carry):
            rows = pl.ds(pl.multiple_of(c * CHUNK, CHUNK), CHUNK)
            _, _, _, _, el, qe, ke, qE, kd = _hg_chunk_terms(bsc[rows, :], qsc[rows, :], ksc[rows, :])
            qe, ke, kd = qe.astype(BF), ke.astype(BF), kd.astype(BF)
            qes[rows, :] = qE.astype(BF)
            els[c] = jnp.broadcast_to(el, (8, 512))
            sls = [slice(HG_D * h, HG_D * (h + 1)) for h in range(HG_HEADS)]
            v = [ub_ref[rows, 1024 + HG_D * h:1024 + HG_D * (h + 1)] for h in range(HG_HEADS)]
            a = [_nt(qe[:, sl], ke[:, sl]) for sl in sls]
            u = [_tn(v[h], kd[:, sls[h]]) for h in range(HG_HEADS)]
            a = [jnp.where(tri, x, 0.0).astype(BF) for x in a]
            oi = [_nn(a[h], v[h]) for h in range(HG_HEADS)]
            for h in range(HG_HEADS):
                ust[c, h] = u[h]
                o_ref[rows, sls[h]] = oi[h]
            return carry

        lax.fori_loop(0, cpt, intra, 0, unroll=2)

        for h in range(HG_HEADS):
            sl = slice(HG_D * h, HG_D * (h + 1))
            s = st[h]
            for c in range(cpt):
                ss_ref[c, h] = s
                s = els[c, 0:1, sl] * s + ust[c, h]
            st[h] = s

        def inter(c, carry):
            rows = pl.ds(pl.multiple_of(c * CHUNK, CHUNK), CHUNK)
            for h in range(HG_HEADS):
                sl = slice(HG_D * h, HG_D * (h + 1))
                o_ref[rows, sl] += _nt(qes[rows, sl], ss_ref[c, h].astype(BF))
            return carry

        lax.fori_loop(0, cpt, inter, 0, unroll=2)

        gate = ub_ref[:, 1536:2048].astype(F32)
        for h in range(HG_HEADS):
            sl = slice(HG_D * h, HG_D * (h + 1))
            o = o_ref[:, sl]
            r = lax.rsqrt(jnp.mean(o * o, axis=-1, keepdims=True) + EPS)
            yb_ref[:, sl] = (o * r * gn_ref[:, sl] * _silu(gate[:, sl])).astype(BF)

    in_specs = [pl.BlockSpec((TR, N_B), lambda i: (i, 0)), pl.BlockSpec((1, 512), lambda i: (0, 0)),
                pl.BlockSpec((1, 512), lambda i: (0, 0))]
    out_specs = [pl.BlockSpec((TR, 512), lambda i: (i, 0)), pl.BlockSpec((TR, 512), lambda i: (i, 0)),
                 pl.BlockSpec((cpt, HG_HEADS, HG_D, HG_D), lambda i: (i, 0, 0, 0))]
    out_shape = [jax.ShapeDtypeStruct((Lp, 512), BF), jax.ShapeDtypeStruct((Lp, 512), F32),
                 jax.ShapeDtypeStruct((Lp // CHUNK, HG_HEADS, HG_D, HG_D), F32)]
    scratch = [pltpu.VMEM((HG_HEADS, HG_D, HG_D), F32), pltpu.VMEM((TR, 512), F32),
               pltpu.VMEM((TR, 512), F32), pltpu.VMEM((TR, 512), F32), pltpu.VMEM((TR, 512), BF),
               pltpu.VMEM((cpt, 8, 512), F32), pltpu.VMEM((cpt, HG_HEADS, HG_D, HG_D), F32)]
    return _call_carrying(body, "hgrn_fwd", nt, in_specs, out_specs, out_shape, scratch, (ub, lb, gn4), carry)


def _hg_bwd(ub, lb, gn4, o_save, s_save, dyb, carry=None):
    Lp = ub.shape[0]
    nt = Lp // TR
    cpt = TR // CHUNK

    def body(ub_ref, lb_ref, gn_ref, o_ref, ss_ref, dy_ref, du_ref, ds_ref,
             dst, bsc, qsc, ksc, dosc, dqsc, dksc, dbsc, els, ust, dss):
        i = pl.program_id(0)
        t = nt - 1 - i

        @pl.when(i == 0)
        def _():
            dst[...] = jnp.zeros_like(dst)
            ds_ref[...] = jnp.zeros_like(ds_ref)

        lbv = lb_ref[...]
        row = t * TR + lax.broadcasted_iota(jnp.int32, (TR, 1), 0)
        valid = row >= META_PAD
        q, k, g, sig, f = _hg_gates(ub_ref, lbv, row)
        qsc[...] = _silu(q)
        ksc[...] = k
        bsc[...] = _mm3(_chunk_tri(TR, False), g)

        gate = ub_ref[:, 1536:2048].astype(F32)
        dy = dy_ref[...].astype(F32)
        dgn = jnp.zeros((1, 512), F32)
        for h in range(HG_HEADS):
            sl = slice(HG_D * h, HG_D * (h + 1))
            o = o_ref[:, sl]
            r = lax.rsqrt(jnp.mean(o * o, axis=-1, keepdims=True) + EPS)
            ohat = o * r
            don = dy[:, sl] * _silu(gate[:, sl])
            du_ref[:, 1536 + HG_D * h:1536 + HG_D * (h + 1)] = (
                dy[:, sl] * ohat * gn_ref[:, sl] * _dsilu(gate[:, sl])).astype(BF)
            ds_ref[1:2, sl] += jnp.sum(don * ohat, axis=0, keepdims=True)
            gd = don * gn_ref[:, sl]
            dosc[:, sl] = r * (gd - ohat * jnp.mean(gd * ohat, axis=-1, keepdims=True))

        tri = lax.broadcasted_iota(jnp.int32, (CHUNK, CHUNK), 1) <= lax.broadcasted_iota(jnp.int32, (CHUNK, CHUNK), 0)
        last = lax.broadcasted_iota(jnp.int32, (CHUNK, 1), 0) == CHUNK - 1

        def incr(c, carry):
            rows = pl.ds(pl.multiple_of(c * CHUNK, CHUNK), CHUNK)
            b_c = bsc[rows, :]
            qE_b = (qsc[rows, :] * jnp.exp(b_c)).astype(BF)
            els[c] = jnp.broadcast_to(jnp.exp(b_c[CHUNK - 1:CHUNK, :]), (8, 512))
            do_c = dosc[rows, :].astype(BF)
            for h in range(HG_HEADS):
                sl = slice(HG_D * h, HG_D * (h + 1))
                ust[c, h] = _tn(do_c[:, sl], qE_b[:, sl])
            return carry

        lax.fori_loop(0, cpt, incr, 0, unroll=2)

        for h in range(HG_HEADS):
            sl = slice(HG_D * h, HG_D * (h + 1))
            d_s = dst[h]
            for c in reversed(range(cpt)):
                dss[c, h] = d_s
                d_s = els[c, 0:1, sl] * d_s + ust[c, h]
            dst[h] = d_s

        def chunk(c, carry):
            r0 = pl.multiple_of(c * CHUNK, CHUNK)
            rows = pl.ds(r0, CHUNK)
            e1, e2, e0, e3, el, qe, ke, qE, kd = _hg_chunk_terms(bsc[rows, :], qsc[rows, :], ksc[rows, :])
            qe_b, ke_b, kd_b = qe.astype(BF), ke.astype(BF), kd.astype(BF)
            do_c = dosc[rows, :].astype(BF)
            hs = range(HG_HEADS)
            sls = [slice(HG_D * h, HG_D * (h + 1)) for h in hs]
            v = [ub_ref[rows, 1024 + HG_D * h:1024 + HG_D * (h + 1)] for h in hs]
            do = [do_c[:, sl] for sl in sls]
            a = [_nt(qe_b[:, sl], ke_b[:, sl]) for sl in sls]
            da = [_nt(do[h], v[h]) for h in hs]
            dqE = [_nn(do[h], ss_ref[c, h].astype(BF)) for h in hs]
            dkd = [_nn(v[h], dss[c, h].astype(BF)) for h in hs]
            dv2 = [_nt(kd_b[:, sls[h]], dss[c, h].astype(BF)) for h in hs]
            a = [jnp.where(tri, x, 0.0).astype(BF) for x in a]
            da = [jnp.where(tri, x, 0.0).astype(BF) for x in da]
            dv = [_tn(a[h], do[h]) + dv2[h] for h in hs]
            dqe = [_nn(da[h], ke_b[:, sls[h]]) for h in hs]
            dke = [_tn(da[h], qe_b[:, sls[h]]) for h in hs]
            for h in hs:
                sl = sls[h]
                del_h = jnp.sum(ss_ref[c, h] * dss[c, h], axis=0, keepdims=True)
                dqsc[rows, sl] = dqE[h] * e0[:, sl] + dqe[h] * e1[:, sl]
                dksc[rows, sl] = dke[h] * e2[:, sl] + dkd[h] * e3[:, sl]
                tkd = dkd[h] * kd[:, sl]
                dbl = jnp.sum(tkd, axis=0, keepdims=True) + del_h * el[:, sl]
                dbsc[rows, sl] = (dqE[h] * qE[:, sl] + dqe[h] * qe[:, sl] - dke[h] * ke[:, sl] - tkd
                                  + jnp.where(last, dbl, 0.0))
                du_ref[rows, 1024 + HG_D * h:1024 + HG_D * (h + 1)] = dv[h].astype(BF)
            return carry

        lax.fori_loop(0, cpt, chunk, 0, unroll=2)

        dg = _mm3(_chunk_tri(TR, True), dbsc[...])
        df = jnp.where(valid & (f > F_FLOOR), dg / f, 0.0)
        dk = jnp.where(valid, dksc[...], 0.0)
        nsig = _sig(-ub_ref[:, 512:1024].astype(F32))
        dsig = (df - dk) * (1.0 - lbv)
        ds_ref[0:1, :] += jnp.sum(df * (1.0 - sig) - dk * nsig, axis=0, keepdims=True)
        du_ref[:, 512:1024] = (dsig * sig * (1.0 - sig)).astype(BF)
        du_ref[:, 0:512] = (dqsc[...] * _dsilu(q)).astype(BF)

    rev = lambda i: (nt - 1 - i, 0)
    in_specs = [pl.BlockSpec((TR, N_B), rev), pl.BlockSpec((1, 512), lambda i: (0, 0)),
                pl.BlockSpec((1, 512), lambda i: (0, 0)), pl.BlockSpec((TR, 512), rev),
                pl.BlockSpec((cpt, HG_HEADS, HG_D, HG_D), lambda i: (nt - 1 - i, 0, 0, 0)),
                pl.BlockSpec((TR, 512), rev)]
    out_specs = [pl.BlockSpec((TR, N_B), rev), pl.BlockSpec((8, 512), lambda i: (0, 0))]
    out_shape = [jax.ShapeDtypeStruct((Lp, N_B), BF), jax.ShapeDtypeStruct((8, 512), F32)]
    states = pltpu.VMEM((cpt, HG_HEADS, HG_D, HG_D), F32)
    scratch = ([pltpu.VMEM((HG_HEADS, HG_D, HG_D), F32)] + [pltpu.VMEM((TR, 512), F32)] * 7
               + [pltpu.VMEM((cpt, 8, 512), F32), states, states])
    return _call_carrying(body, "hgrn_bwd", nt, in_specs, out_specs, out_shape, scratch,
                          (ub, lb, gn4, o_save, s_save, dyb), carry)


_KCOL = (2 * 512) // 128
_VCOL = _KCOL + 1


def _swa_in_specs(nt, rev):
    tile = (lambda i: nt - 1 - i) if rev else (lambda i: i)
    hpt = TR // HALO
    return [
        pl.BlockSpec((TR, 512), lambda i: (tile(i), 0)),
        pl.BlockSpec((TR, 512), lambda i: (tile(i), 1)),
        pl.BlockSpec((TR, 128), lambda i: (tile(i), _KCOL)),
        pl.BlockSpec((TR, 128), lambda i: (tile(i), _VCOL)),
        pl.BlockSpec((HALO, 128), lambda i: (jnp.maximum(tile(i) * hpt - 1, 0), _KCOL)),
        pl.BlockSpec((HALO, 128), lambda i: (jnp.maximum(tile(i) * hpt - 1, 0), _VCOL)),
        pl.BlockSpec((CHUNK, 128), lambda i: (0, _KCOL)),
        pl.BlockSpec((CHUNK, 128), lambda i: (0, _VCOL)),
        pl.BlockSpec((1, 512), lambda i: (0, 0)),
        pl.BlockSpec((1, 128), lambda i: (0, 0)),
        pl.BlockSpec((1, ATT_Q_HEADS), lambda i: (0, 0)),
    ]


_WROWS = 2 * CHUNK + HALO + TR
_W0 = 2 * CHUNK
_C0 = _W0 + HALO
_SCALE = ATT_HD ** -0.5


def _group_ones(n):
    r = lax.broadcasted_iota(jnp.int32, (n, n), 0)
    c = lax.broadcasted_iota(jnp.int32, (n, n), 1)
    return jnp.where(jnp.right_shift(r, 6) == jnp.right_shift(c, 6), 1.0, 0.0).astype(BF)


def _group_mean(x, ones):
    hi = x.astype(BF)
    lo = (x - hi.astype(F32)).astype(BF)
    return (_nn(hi, ones) + _nn(lo, ones)) * (1.0 / ATT_HD)


def _head_rms(x, ones):
    r = lax.rsqrt(_group_mean(x * x, ones) + EPS)
    return x * r, r


def _swa_windows(kc_ref, vc_ref, kh_ref, vh_ref, km_ref, vm_ref, kg2, ones, kwin, krwin, vwin, vrwin):
    meta = pl.ds(META_PAD, N_META)
    for (k, v, r0, n) in ((km_ref[meta, :], vm_ref[meta, :], 0, N_META), (kh_ref[...], vh_ref[...], _W0, HALO),
                          (kc_ref[...], vc_ref[...], _C0, TR)):
        xhat, _ = _head_rms(k.astype(F32), ones)
        kn = xhat * kg2
        kwin[pl.ds(r0, n), :] = kn.astype(BF)
        krwin[pl.ds(r0, n), :] = pltpu.roll(kn, ATT_HD, 1).astype(BF)
        vwin[pl.ds(r0, n), :] = v
        if vrwin is not None:
            vrwin[pl.ds(r0, n), :] = pltpu.roll(v.astype(F32), ATT_HD, 1).astype(BF)
    zero = jnp.zeros((_W0 - N_META, 128), BF)
    for w in (kwin, krwin, vwin, vrwin):
        if w is not None:
            w[pl.ds(N_META, _W0 - N_META), :] = zero


def _swa_masks_t(t, qb):
    q0 = t * TR + qb * QB
    qc = jnp.right_shift(q0 + lax.broadcasted_iota(jnp.int32, (1, QB), 1), 6)
    kabs = q0 - HALO + lax.broadcasted_iota(jnp.int32, (QB + HALO, 1), 0)
    kc = jnp.right_shift(kabs + HALO, 6) - HALO // CHUNK
    mask_w = (kc <= qc) & (kc >= qc - 2) & (kabs >= META_PAD)
    return qc > 2, mask_w


def _swa_park(dtype):
    return [pltpu.VMEM((ATT_Q_HEADS, N_META, QB), dtype), pltpu.VMEM((ATT_Q_HEADS, QB + HALO, QB), dtype)]


def _split_heads(x, lane_hi):
    return jnp.where(lane_hi, 0.0, x).astype(BF), jnp.where(lane_hi, x, 0.0).astype(BF)


def _call_carrying(body, name, nt, in_specs, out_specs, out_shape, scratch, args, carry):
    if carry is None:
        return pl.pallas_call(body, name=name, grid=(nt,), in_specs=in_specs, out_specs=out_specs, out_shape=out_shape,
                              scratch_shapes=scratch, compiler_params=_cp(("arbitrary",)))(*args)
    kind, arrs = carry
    n = len(arrs)
    return pl.pallas_call(
        _carry_exchange(body, len(in_specs), len(out_specs), nt, kind, n), name=name + "_" + kind, grid=(nt,),
        in_specs=in_specs + [_ANY] * n, out_specs=out_specs + [_ANY] * n,
        out_shape=out_shape + _exchange_out_shapes(kind, arrs), scratch_shapes=scratch + _exchange_sems(n),
        compiler_params=_cp(("arbitrary",), has_side_effects=True),
    )(*args, *arrs)


def _swa_fwd(uc, qg8, kg2, sinks, carry=None):
    Lp = uc.shape[0]
    nt = Lp // TR
    nqb = TR // QB

    def body(q_ref, g_ref, kc_ref, vc_ref, kh_ref, vh_ref, km_ref, vm_ref, qg_ref, kg_ref, sk_ref,
             yc_ref, o_ref, lse_ref, kwin, krwin, vwin, vt, qlo, qhi, ot, s_m, s_w, p_m, p_w):
        t = pl.program_id(0)
        _swa_windows(kc_ref, vc_ref, kh_ref, vh_ref, km_ref, vm_ref, kg_ref[...], _group_ones(128),
                     kwin, krwin, vwin, None)
        vt[...] = vwin[...].T
        xhat, _ = _head_rms(q_ref[...].astype(F32), _group_ones(512))
        lane_hi = (lax.broadcasted_iota(jnp.int32, (1, 512), 1) & ATT_HD) != 0
        lo, hi = _split_heads(xhat * qg_ref[...] * _SCALE, lane_hi)
        qlo[...] = lo
        qhi[...] = hi
        for qb in range(nqb):
            rows = pl.ds(qb * QB, QB)
            wrows = pl.ds(_W0 + qb * QB, QB + HALO)
            mrows = pl.ds(0, N_META)
            mask_m, mask_w = _swa_masks_t(t, qb)
            for j in range(ATT_Q_HEADS):
                p, e = j // 2, j % 2
                ks = kwin if e == j // ATT_GROUP else krwin
                qp = (qlo, qhi)[e][rows, 128 * p:128 * (p + 1)]
                s_m[j] = _nt(ks[mrows, :], qp)
                s_w[j] = _nt(ks[wrows, :], qp)
            inv = []
            for j in range(ATT_Q_HEADS):
                sm = jnp.where(mask_m, s_m[j], NEG)
                sw = jnp.where(mask_w, s_w[j], NEG)
                sink = sk_ref[:, j:j + 1]
                m = jnp.maximum(jnp.maximum(jnp.max(sm, axis=0, keepdims=True),
                                            jnp.max(sw, axis=0, keepdims=True)), sink)
                em = jnp.exp(sm - m)
                ew = jnp.exp(sw - m)
                den = jnp.sum(em, axis=0, keepdims=True) + jnp.sum(ew, axis=0, keepdims=True) + jnp.exp(sink - m)
                p_m[j] = em.astype(BF)
                p_w[j] = ew.astype(BF)
                lse_ref[j:j + 1, pl.ds(qb * QB, QB)] = m + jnp.log(den)
                inv.append(1.0 / den)
            for j in range(ATT_Q_HEADS):
                vrows = pl.ds(ATT_HD * (j // ATT_GROUP), ATT_HD)
                ot[pl.ds(ATT_HD * j, ATT_HD), pl.ds(qb * QB, QB)] = (
                    _nn(vt[vrows, pl.ds(0, N_META)], p_m[j])
                    + _nn(vt[vrows, pl.ds(_W0 + qb * QB, QB + HALO)], p_w[j])) * inv[j]
        o = ot[...].T
        o_ref[...] = o
        yc_ref[...] = (o * _silu(g_ref[...].astype(F32))).astype(BF)

    win = pltpu.VMEM((_WROWS, 128), BF)
    in_specs = _swa_in_specs(nt, False)
    out_specs = [pl.BlockSpec((TR, 512), lambda i: (i, 0)), pl.BlockSpec((TR, 512), lambda i: (i, 0)),
                 pl.BlockSpec((ATT_Q_HEADS, TR), lambda i: (0, i))]
    out_shape = [jax.ShapeDtypeStruct((Lp, 512), BF), jax.ShapeDtypeStruct((Lp, 512), F32),
                 jax.ShapeDtypeStruct((ATT_Q_HEADS, Lp), F32)]
    scratch = [win, win, win, pltpu.VMEM((128, _WROWS), BF), pltpu.VMEM((TR, 512), BF),
               pltpu.VMEM((TR, 512), BF), pltpu.VMEM((512, TR), F32)] + _swa_park(F32) + _swa_park(BF)
    return _call_carrying(body, "swa_fwd", nt, in_specs, out_specs, out_shape, scratch,
                          (uc, uc, uc, uc, uc, uc, uc, uc, qg8, kg2, sinks), carry)


def _swa_bwd(uc, qg8, kg2, sinks, o_save, lse, dyc):
    Lp = uc.shape[0]
    nt = Lp // TR
    nqb = TR // QB

    def body(q_ref, g_ref, kc_ref, vc_ref, kh_ref, vh_ref, km_ref, vm_ref, qg_ref, kg_ref, sk_ref,
             o_ref, lse_ref, dy_ref, du_ref, dg_ref, dsk_ref,
             kwin, krwin, vwin, vrwin, kt, krt, qlo, qhi, dolo, dohi, dqt, dk_dir, dk_rol, dv_dir, dv_rol,
             carry_k, carry_v, meta_k, meta_v, s_m, s_w, dp_m, dp_w, p_m, p_w, ds_m, ds_w):
        i = pl.program_id(0)
        t = nt - 1 - i

        @pl.when(i == 0)
        def _():
            carry_k[...] = jnp.zeros_like(carry_k)
            carry_v[...] = jnp.zeros_like(carry_v)
            meta_k[...] = jnp.zeros_like(meta_k)
            meta_v[...] = jnp.zeros_like(meta_v)
            dg_ref[...] = jnp.zeros_like(dg_ref)
            dsk_ref[...] = jnp.zeros_like(dsk_ref)

        ones128 = _group_ones(128)
        ones512 = _group_ones(512)
        _swa_windows(kc_ref, vc_ref, kh_ref, vh_ref, km_ref, vm_ref, kg_ref[...], ones128, kwin, krwin, vwin, vrwin)
        kt[...] = kwin[...].T
        krt[...] = krwin[...].T
        xhat_q, r_q = _head_rms(q_ref[...].astype(F32), ones512)
        lane_hi = (lax.broadcasted_iota(jnp.int32, (1, 512), 1) & ATT_HD) != 0
        lo, hi = _split_heads(xhat_q * qg_ref[...] * _SCALE, lane_hi)
        qlo[...] = lo
        qhi[...] = hi
        gate = g_ref[...].astype(F32)
        dy = dy_ref[...].astype(F32)
        do = dy * _silu(gate)
        o = o_ref[...]
        du_ref[:, 512:1024] = (dy * o * _dsilu(gate)).astype(BF)
        lo, hi = _split_heads(do, lane_hi)
        dolo[...] = lo
        dohi[...] = hi
        hsel = jnp.where(jnp.right_shift(lax.broadcasted_iota(jnp.int32, (ATT_Q_HEADS, 512), 1), 6)
                         == lax.broadcasted_iota(jnp.int32, (ATT_Q_HEADS, 512), 0), 1.0, 0.0).astype(BF)
        prod = do * o
        p_hi = prod.astype(BF)
        d_t = _nt(hsel, p_hi) + _nt(hsel, (prod - p_hi.astype(F32)).astype(BF))
        for acc in (dk_dir, dk_rol, dv_dir, dv_rol):
            acc[...] = jnp.zeros_like(acc)

        for qb in range(nqb):
            rows = pl.ds(qb * QB, QB)
            qcols = pl.ds(qb * QB, QB)
            wrows = pl.ds(_W0 + qb * QB, QB + HALO)
            mrows = pl.ds(0, N_META)
            mask_m, mask_w = _swa_masks_t(t, qb)
            for j in range(ATT_Q_HEADS):
                p, e = j // 2, j % 2
                ks, vs = (kwin, vwin) if e == j // ATT_GROUP else (krwin, vrwin)
                pair = slice(128 * p, 128 * (p + 1))
                qp = (qlo, qhi)[e][rows, pair]
                dop = (dolo, dohi)[e][rows, pair]
                s_m[j] = _nt(ks[mrows, :], qp)
                s_w[j] = _nt(ks[wrows, :], qp)
                dp_m[j] = _nt(vs[mrows, :], dop)
                dp_w[j] = _nt(vs[wrows, :], dop)
            for j in range(ATT_Q_HEADS):
                lse_j = lse_ref[j:j + 1, qcols]
                d_j = d_t[j:j + 1, qb * QB:(qb + 1) * QB]
                em = jnp.exp(jnp.where(mask_m, s_m[j], NEG) - lse_j)
                ew = jnp.exp(jnp.where(mask_w, s_w[j], NEG) - lse_j)
                p_m[j] = em.astype(BF)
                p_w[j] = ew.astype(BF)
                ds_m[j] = (em * (dp_m[j] - d_j)).astype(BF)
                ds_w[j] = (ew * (dp_w[j] - d_j)).astype(BF)
                dsk_ref[j:j + 1, :] -= jnp.exp(sk_ref[:, j:j + 1] - lse_j) * d_j
            for j in range(ATT_Q_HEADS):
                e = j % 2
                ktr = kt if e == j // ATT_GROUP else krt
                hrows = pl.ds(ATT_HD * e, ATT_HD)
                dqt[pl.ds(ATT_HD * j, ATT_HD), qcols] = (_nn(ktr[hrows, pl.ds(0, N_META)], ds_m[j])
                                                         + _nn(ktr[hrows, pl.ds(_W0 + qb * QB, QB + HALO)], ds_w[j]))
            for direct, dk_acc, dv_acc in ((True, dk_dir, dv_dir), (False, dk_rol, dv_rol)):
                heads = [j for j in range(ATT_Q_HEADS) if (j % 2 == j // ATT_GROUP) == direct]
                q_cat = jnp.concatenate([(qlo, qhi)[j % 2][rows, 128 * (j // 2):128 * (j // 2 + 1)] for j in heads], axis=0)
                do_cat = jnp.concatenate([(dolo, dohi)[j % 2][rows, 128 * (j // 2):128 * (j // 2 + 1)] for j in heads], axis=0)
                dk_acc[mrows, :] += _nn(jnp.concatenate([ds_m[j] for j in heads], axis=1), q_cat)
                dk_acc[wrows, :] += _nn(jnp.concatenate([ds_w[j] for j in heads], axis=1), q_cat)
                dv_acc[mrows, :] += _nn(jnp.concatenate([p_m[j] for j in heads], axis=1), do_cat)
                dv_acc[wrows, :] += _nn(jnp.concatenate([p_w[j] for j in heads], axis=1), do_cat)

        dk_dir[...] += pltpu.roll(dk_rol[...], ATT_HD, 1)
        dv_dir[...] += pltpu.roll(dv_rol[...], ATT_HD, 1)
        meta_k[...] += dk_dir[pl.ds(0, N_META), :]
        meta_v[...] += dv_dir[pl.ds(0, N_META), :]
        first = jnp.where(t == 0, 1.0, 0.0)
        dk_dir[pl.ds(_C0 + TR - HALO, HALO), :] += carry_k[...]
        dv_dir[pl.ds(_C0 + TR - HALO, HALO), :] += carry_v[...]
        dk_dir[pl.ds(_C0 + META_PAD, N_META), :] += first * meta_k[...]
        dv_dir[pl.ds(_C0 + META_PAD, N_META), :] += first * meta_v[...]
        carry_k[...] = dk_dir[pl.ds(_W0, HALO), :]
        carry_v[...] = dv_dir[pl.ds(_W0, HALO), :]

        du_ref[:, 1152:1280] = dv_dir[pl.ds(_C0, TR), :].astype(BF)
        xhat_k, r_k = _head_rms(kc_ref[...].astype(F32), ones128)
        dkn = dk_dir[pl.ds(_C0, TR), :]
        dg_ref[1:2, 0:128] += jnp.sum(dkn * xhat_k, axis=0, keepdims=True)
        gd = dkn * kg_ref[...]
        du_ref[:, 1024:1152] = (r_k * (gd - xhat_k * _group_mean(gd * xhat_k, ones128))).astype(BF)
        dqn = dqt[...].T * _SCALE
        dg_ref[0:1, :] += jnp.sum(dqn * xhat_q, axis=0, keepdims=True)
        gd = dqn * qg_ref[...]
        du_ref[:, 0:512] = (r_q * (gd - xhat_q * _group_mean(gd * xhat_q, ones512))).astype(BF)

    rev = lambda i: (nt - 1 - i, 0)
    specs = _swa_in_specs(nt, True)
    win = pltpu.VMEM((_WROWS, 128), BF)
    wint = pltpu.VMEM((128, _WROWS), BF)
    tile_bf = pltpu.VMEM((TR, 512), BF)
    acc = pltpu.VMEM((_WROWS, 128), F32)
    return pl.pallas_call(
        body, name="swa_bwd", grid=(nt,),
        in_specs=specs + [pl.BlockSpec((TR, 512), rev), pl.BlockSpec((ATT_Q_HEADS, TR), lambda i: (0, nt - 1 - i)),
                          pl.BlockSpec((TR, 512), rev)],
        out_specs=[pl.BlockSpec((TR, N_C), rev), pl.BlockSpec((8, 512), lambda i: (0, 0)),
                   pl.BlockSpec((8, 128), lambda i: (0, 0))],
        out_shape=[jax.ShapeDtypeStruct((Lp, N_C), BF), jax.ShapeDtypeStruct((8, 512), F32),
                   jax.ShapeDtypeStruct((8, 128), F32)],
        scratch_shapes=[win, win, win, win, wint, wint, tile_bf, tile_bf, tile_bf, tile_bf,
                        pltpu.VMEM((512, TR), F32), acc, acc, acc, acc,
                        pltpu.VMEM((HALO, 128), F32), pltpu.VMEM((HALO, 128), F32),
                        pltpu.VMEM((N_META, 128), F32), pltpu.VMEM((N_META, 128), F32)]
        + _swa_park(F32) + _swa_park(F32) + _swa_park(BF) + _swa_park(BF),
        compiler_params=_cp(("arbitrary",)),
    )(uc, uc, uc, uc, uc, uc, uc, uc, qg8, kg2, sinks, o_save, lse, dyc)


def _mix_fwd(h, ya, yb, yc, ug, wa, wb, wc, wo):
    Lp = h.shape[0]
    wspec = lambda r: pl.BlockSpec((r, D_MODEL), lambda i: (0, 0))
    yspec = pl.BlockSpec((TRM, 512), lambda i: (i, 0))
    hspec = pl.BlockSpec((TRM, D_MODEL), lambda i: (i, 0))

    def body(h_ref, ya_ref, yb_ref, yc_ref, ug_ref, wa_ref, wb_ref, wc_ref, wo_ref, hn_ref, za_ref, zb_ref, zc_ref):
        mixed = jnp.zeros((TRM, D_MODEL), F32)
        for n, (y_ref, w_ref, z_ref) in enumerate(((ya_ref, wa_ref, za_ref), (yb_ref, wb_ref, zb_ref),
                                                   (yc_ref, wc_ref, zc_ref))):
            z = _nn(y_ref[...], w_ref[...])
            z_ref[...] = z.astype(BF)
            mixed = mixed + _sig(ug_ref[:, D_MODEL * n:D_MODEL * (n + 1)].astype(F32)) * z
        hn_ref[...] = h_ref[...] + _nn(mixed.astype(BF), wo_ref[...])

    return pl.pallas_call(
        body, name="mix_fwd", grid=(Lp // TRM,),
        in_specs=[hspec, yspec, yspec, yspec, pl.BlockSpec((TRM, N_G), lambda i: (i, 0)),
                  wspec(512), wspec(512), wspec(512), wspec(D_MODEL)],
        out_specs=[hspec, hspec, hspec, hspec],
        out_shape=[jax.ShapeDtypeStruct((Lp, D_MODEL), F32)] + [jax.ShapeDtypeStruct((Lp, D_MODEL), BF)] * 3,
        compiler_params=_cp(("parallel",)),
    )(h, ya, yb, yc, ug, wa, wb, wc, wo)


def _mix_bwd(dh, za, zb, zc, ug, wa, wb, wc, wo):
    Lp = dh.shape[0]
    wspec = lambda r: pl.BlockSpec((r, D_MODEL), lambda i: (0, 0))
    yspec = pl.BlockSpec((TRM, 512), lambda i: (i, 0))
    hspec = pl.BlockSpec((TRM, D_MODEL), lambda i: (i, 0))
    gspec = pl.BlockSpec((TRM, N_G), lambda i: (i, 0))

    def body(dh_ref, za_ref, zb_ref, zc_ref, ug_ref, wa_ref, wb_ref, wc_ref, wo_ref,
             dug_ref, mx_ref, dza_ref, dzb_ref, dzc_ref, dya_ref, dyb_ref, dyc_ref):
        dmix = _nt(dh_ref[...].astype(BF), wo_ref[...])
        mixed = jnp.zeros((TRM, D_MODEL), F32)
        for n, (z_ref, w_ref, dz_ref, dy_ref) in enumerate(((za_ref, wa_ref, dza_ref, dya_ref),
                                                            (zb_ref, wb_ref, dzb_ref, dyb_ref),
                                                            (zc_ref, wc_ref, dzc_ref, dyc_ref))):
            sl = slice(D_MODEL * n, D_MODEL * (n + 1))
            z = z_ref[...].astype(F32)
            gt = _sig(ug_ref[:, sl].astype(F32))
            mixed = mixed + gt * z
            dug_ref[:, sl] = (dmix * z * gt * (1.0 - gt)).astype(BF)
            dz = (dmix * gt).astype(BF)
            dz_ref[...] = dz
            dy_ref[...] = _nt(dz, w_ref[...]).astype(BF)
        mx_ref[...] = mixed.astype(BF)

    bf = lambda n: jax.ShapeDtypeStruct((Lp, n), BF)
    return pl.pallas_call(
        body, name="mix_bwd", grid=(Lp // TRM,),
        in_specs=[hspec, hspec, hspec, hspec, gspec, wspec(512), wspec(512), wspec(512), wspec(D_MODEL)],
        out_specs=[gspec, hspec, hspec, hspec, hspec, yspec, yspec, yspec],
        out_shape=[bf(N_G), bf(D_MODEL), bf(D_MODEL), bf(D_MODEL), bf(D_MODEL), bf(512), bf(512), bf(512)],
        compiler_params=_cp(("parallel",)),
    )(dh, za, zb, zc, ug, wa, wb, wc, wo)


def _inproj_bwd(dus, ws, h, dh, g, carry=None):
    Lp = h.shape[0]
    widths = [w.shape[1] for w in ws]

    def body(dg_ref, da_ref, db_ref, dc_ref, wg_ref, wa_ref, wb_ref, wc_ref, h_ref, dh_ref, g_ref, o_ref, gg_ref):
        @pl.when(pl.program_id(0) == 0)
        def _():
            gg_ref[...] = jnp.zeros_like(gg_ref)

        dhn = (_nt(dg_ref[...], wg_ref[...]) + _nt(da_ref[...], wa_ref[...])
               + _nt(db_ref[...], wb_ref[...]) + _nt(dc_ref[...], wc_ref[...]))
        x = h_ref[...]
        r = lax.rsqrt(jnp.mean(x * x, axis=-1, keepdims=True) + EPS)
        xhat = x * r
        gg_ref[0:1, :] += jnp.sum(dhn * xhat, axis=0, keepdims=True)
        gd = dhn * g_ref[...]
        o_ref[...] = dh_ref[...] + r * (gd - xhat * jnp.mean(gd * xhat, axis=-1, keepdims=True))

    hspec = pl.BlockSpec((TRM, D_MODEL), lambda i: (i, 0))
    in_specs = ([pl.BlockSpec((TRM, n), lambda i: (i, 0)) for n in widths]
                + [pl.BlockSpec((D_MODEL, n), lambda i: (0, 0), pipeline_mode=pl.Buffered(1)) for n in widths]
                + [hspec, hspec, pl.BlockSpec((1, D_MODEL), lambda i: (0, 0))])
    out_specs = [hspec, pl.BlockSpec((8, D_MODEL), lambda i: (0, 0))]
    out_shape = [jax.ShapeDtypeStruct((Lp, D_MODEL), F32), jax.ShapeDtypeStruct((8, D_MODEL), F32)]
    return _call_carrying(body, "inproj_bwd", Lp // TRM, in_specs, out_specs, out_shape, [],
                          (*dus, *ws, h, dh, g), carry)


def _loss_head(h, tgt_pad, seq):
    Lp = h.shape[0]
    nt = Lp // TR

    def body(h_ref, t_ref, dh_ref, l_ref):
        i = pl.program_id(0)

        @pl.when(i == 0)
        def _():
            l_ref[...] = jnp.zeros_like(l_ref)

        row = i * TR + lax.broadcasted_iota(jnp.int32, (TR, 1), 0)
        e = jnp.where((row >= CHUNK) & (row < CHUNK + seq), h_ref[...] - t_ref[...], 0.0)
        dh_ref[...] = e * (1.0 / D_MODEL)
        l_ref[...] += (0.5 / D_MODEL) * jnp.sum(jnp.sum(e * e, axis=0, keepdims=True), axis=1, keepdims=True)

    hspec = pl.BlockSpec((TR, D_MODEL), lambda i: (i, 0))
    return pl.pallas_call(
        body, name="loss_head", grid=(nt,), in_specs=[hspec, hspec],
        out_specs=[hspec, pl.BlockSpec((8, 128), lambda i: (0, 0))],
        out_shape=[jax.ShapeDtypeStruct((Lp, D_MODEL), F32), jax.ShapeDtypeStruct((8, 128), F32)],
        compiler_params=_cp(("arbitrary",)),
    )(h, tgt_pad)


def _lb_softmax(lb_ref):
    x = lb_ref[...]
    e = jnp.exp(x - jnp.max(x, axis=0, keepdims=True))
    return e / jnp.sum(e, axis=0, keepdims=True)


def _lb_fwd(hg_lb):
    def body(lb_ref, o_ref):
        sm = _lb_softmax(lb_ref)
        acc = jnp.zeros((1, 512), F32)
        for l in range(DEPTH):
            if l > 0:
                acc = acc + sm[l:l + 1, :]
            o_ref[l:l + 1, :] = jnp.clip(acc, 0.0, 1.0)

    return pl.pallas_call(body, name="lb_fwd", out_shape=jax.ShapeDtypeStruct((DEPTH, 512), F32))(hg_lb)


def _lb_bwd(hg_lb, dlb_all):
    def body(lb_ref, d_ref, o_ref):
        sm = _lb_softmax(lb_ref)
        acc = jnp.zeros((1, 512), F32)
        gm = []
        for l in range(DEPTH):
            if l > 0:
                acc = acc + sm[l:l + 1, :]
            gm.append(jnp.where((acc >= 0.0) & (acc <= 1.0), d_ref[l:l + 1, :], 0.0))
        dsm = [jnp.zeros((1, 512), F32)]
        for j in range(1, DEPTH):
            s = gm[j]
            for l in range(j + 1, DEPTH):
                s = s + gm[l]
            dsm.append(s)
        dot = dsm[0] * sm[0:1, :]
        for j in range(1, DEPTH):
            dot = dot + dsm[j] * sm[j:j + 1, :]
        for j in range(DEPTH):
            o_ref[j:j + 1, :] = sm[j:j + 1, :] * (dsm[j] - dot)

    return pl.pallas_call(body, name="lb_bwd", out_shape=jax.ShapeDtypeStruct((DEPTH, 512), F32))(hg_lb, dlb_all)


_ANY = pl.BlockSpec(memory_space=pl.ANY)


def _chip_peers():
    x, y, c = lax.axis_index("x"), lax.axis_index("y"), lax.axis_index("c")
    return (x, y, c), [(1 - x, y, c), (x, 1 - y, c), (1 - x, 1 - y, c)]


def _exchange(kind, ins, outs, send, recv, loc):
    (x, y, c), peers = _chip_peers()
    me = 2 * x + y
    ds = []
    for a in range(len(ins)):
        if kind == "gather":
            ds.append(pltpu.make_async_copy(ins[a], outs[a].at[me], loc.at[a]))
        else:
            ds.append(pltpu.make_async_copy(ins[a].at[me], outs[a].at[0], loc.at[a]))
        for p, (px, py, pc) in enumerate(peers):
            src, dst = (ins[a], outs[a].at[me]) if kind == "gather" else (ins[a].at[2 * px + py], outs[a].at[1 + p])
            ds.append(pltpu.make_async_remote_copy(src_ref=src, dst_ref=dst, send_sem=send.at[a, p],
                                                   recv_sem=recv.at[a, p], device_id=(px, py, pc), device_id_type=MESH))
    return ds


def _exchange_out_shapes(kind, arrs):
    if kind == "gather":
        return [jax.ShapeDtypeStruct((4,) + a.shape, a.dtype) for a in arrs]
    return [jax.ShapeDtypeStruct(a.shape, a.dtype) for a in arrs]


def _exchange_sems(n):
    return [pltpu.SemaphoreType.DMA((n, 3)), pltpu.SemaphoreType.DMA((n, 3)), pltpu.SemaphoreType.DMA((n,))]


def _exchange_chips(kind, arrs):
    n = len(arrs)

    def body(*refs):
        ds = _exchange(kind, refs[:n], refs[n:2 * n], *refs[2 * n:])
        for d in ds:
            d.start()
        for d in ds:
            d.wait()

    return pl.pallas_call(
        body, name=kind + "_chips", in_specs=[_ANY] * n, out_specs=[_ANY] * n,
        out_shape=_exchange_out_shapes(kind, arrs), scratch_shapes=_exchange_sems(n),
        compiler_params=pltpu.CompilerParams(has_side_effects=True),
    )(*arrs)


def _carry_exchange(body, n_in, n_out, n_steps, kind, n):
    def wrapped(*refs):
        ins, cin = refs[:n_in], refs[n_in:n_in + n]
        outs, cout = refs[n_in + n:n_in + n + n_out], refs[n_in + n + n_out:n_in + 2 * n + n_out]
        scr, sems = refs[n_in + 2 * n + n_out:-3], refs[-3:]
        i = pl.program_id(0)

        @pl.when(i == 0)
        def _():
            for d in _exchange(kind, cin, cout, *sems):
                d.start()

        body(*ins, *outs, *scr)

        @pl.when(i == n_steps - 1)
        def _():
            for d in _exchange(kind, cin, cout, *sems):
                d.wait()

    return wrapped


def _swap_cores(arrs):
    n = len(arrs)

    def body(*refs):
        ins, outs = refs[:n], refs[n:2 * n]
        send, recv = refs[2 * n:]
        x, y, c = lax.axis_index("x"), lax.axis_index("y"), lax.axis_index("c")
        rdmas = []
        for a in range(n):
            r = pltpu.make_async_remote_copy(src_ref=ins[a], dst_ref=outs[a], send_sem=send.at[a], recv_sem=recv.at[a],
                                             device_id=(x, y, 1 - c), device_id_type=MESH)
            r.start()
            rdmas.append(r)
        for r in rdmas:
            r.wait()

    return pl.pallas_call(
        body, name="swap_cores", in_specs=[_ANY] * n, out_specs=[_ANY] * n,
        out_shape=[jax.ShapeDtypeStruct(a.shape, a.dtype) for a in arrs],
        scratch_shapes=[pltpu.SemaphoreType.DMA((n,)), pltpu.SemaphoreType.DMA((n,))],
        compiler_params=pltpu.CompilerParams(has_side_effects=True),
    )(*arrs)


def _allsum_small(p):
    R = p.shape[0]

    def body(p_ref, o_ref, buf, send, recv):
        x, y, c = lax.axis_index("x"), lax.axis_index("y"), lax.axis_index("c")
        me = 4 * x + 2 * y + c
        buf[me] = p_ref[...]
        rdmas = []
        for k in range(1, 8):
            peer = (x ^ (k >> 2), y ^ ((k >> 1) & 1), c ^ (k & 1))
            r = pltpu.make_async_remote_copy(src_ref=p_ref, dst_ref=buf.at[me], send_sem=send.at[k - 1],
                                             recv_sem=recv.at[k - 1], device_id=peer, device_id_type=MESH)
            r.start()
            rdmas.append(r)
        for r in rdmas:
            r.wait()
        acc = buf[0]
        for d in range(1, 8):
            acc = acc + buf[d]
        o_ref[...] = acc

    return pl.pallas_call(
        body, name="allsum_small", out_shape=jax.ShapeDtypeStruct((R, 512), F32),
        in_specs=[pl.BlockSpec(memory_space=pltpu.VMEM)], out_specs=pl.BlockSpec(memory_space=pltpu.VMEM),
        scratch_shapes=[pltpu.VMEM((8, R, 512), F32), pltpu.SemaphoreType.DMA((7,)), pltpu.SemaphoreType.DMA((7,))],
        compiler_params=_cp(has_side_effects=True),
    )(p)


def _sum4(parts, name):
    _, R, C = parts.shape
    tr = 256 if R % 256 == 0 else R

    def body(p_ref, o_ref):
        o_ref[...] = ((p_ref[0] + p_ref[1]) + p_ref[2]) + p_ref[3]

    return pl.pallas_call(
        body, name=name, grid=(R // tr,), in_specs=[pl.BlockSpec((4, tr, C), lambda i: (0, i, 0))],
        out_specs=pl.BlockSpec((tr, C), lambda i: (i, 0)), out_shape=jax.ShapeDtypeStruct((R, C), F32),
        compiler_params=_cp(("parallel",)),
    )(parts)


def _adamw(w, m, v, g0, g1, name):
    R, C = w.shape
    tr = 256 if R % 256 == 0 else R
    two = g1 is not None
    c1 = 1.0 / (1.0 - ADAM_B1 ** ADAM_STEP)
    c2 = 1.0 / (1.0 - ADAM_B2 ** ADAM_STEP)

    def body(*refs):
        if two:
            w_ref, m_ref, v_ref, a_ref, b_ref, g_ref, d_ref, nm_ref, nv_ref = refs
            g = a_ref[...] + b_ref[...]
        else:
            w_ref, m_ref, v_ref, a_ref, g_ref, d_ref, nm_ref, nv_ref = refs
            g = a_ref[...]
        g_ref[...] = g
        m = ADAM_B1 * m_ref[...] + (1.0 - ADAM_B1) * g
        v = ADAM_B2 * v_ref[...] + (1.0 - ADAM_B2) * (g * g)
        nm_ref[...] = m
        nv_ref[...] = v
        d_ref[...] = -ADAM_LR * ((m * c1) / (jnp.sqrt(v * c2) + ADAM_EPS) + ADAM_WD * w_ref[...])

    spec = pl.BlockSpec((tr, C), lambda i: (i, 0))
    n_in = 5 if two else 4
    ins = (w, m, v, g0, g1) if two else (w, m, v, g0)
    return pl.pallas_call(
        body, name=name, grid=(R // tr,), in_specs=[spec] * n_in, out_specs=[spec] * 4,
        out_shape=[jax.ShapeDtypeStruct((R, C), F32)] * 4, compiler_params=_cp(("parallel",)),
    )(*ins)


def _pad8(a):
    r = (-a.shape[0]) % 8
    return a if r == 0 else jnp.pad(a, ((0, r), (0, 0)))


def _local_step(x, tgt, meta, P, shards=None, prep=None, pack=None):
    seq = x.shape[0]
    assert seq % TR == 0
    Lp = seq + TR
    h = jnp.concatenate([jnp.zeros((META_PAD, D_MODEL), F32), meta, x, jnp.zeros((TAIL_PAD, D_MODEL), F32)], axis=0)
    tgt_pad = jnp.pad(tgt, ((CHUNK, TAIL_PAD), (0, 0)))

    P = list(P)
    saved = []
    for l in range(DEPTH):
        p = P[l]
        hn = _rms_fwd(h, p["norm_g"])
        mm = functools.partial(_matmul, out_dtype=BF, tm=TR, tk=D_MODEL, col_major_grid=True)
        ug = mm(hn, p["w_g"], tn=N_G // 2, name="inproj_g")
        ua = mm(hn, p["w_a"], tn=N_A, name="inproj_a")
        ub = mm(hn, p["w_b"], tn=N_B, name="inproj_b")
        uc = mm(hn, p["w_c"], tn=N_C, name="inproj_c")
        nxt = shards[l + 1] if shards is not None and l + 1 < DEPTH else None
        carry = (lambda part: ("gather", part)) if nxt is not None else (lambda part: None)
        res_a = _conv_fwd(ua, p["conv_w"], p["conv_vec"], carry(nxt and nxt[1:2]))
        res_b = _hg_fwd(ub, p["lb"], p["gn4"], carry(nxt and nxt[0:1]))
        res_c = _swa_fwd(uc, p["qg"], p["kg"], p["sinks"], carry(nxt and nxt[2:]))
        (ya, yconv), (yb, o_hg, s_hg), (yc, o_at, lse) = res_a[:2], res_b[:3], res_c[:3]
        if nxt is not None:
            P.append(prep(l + 1, [*res_b[3:], *res_a[2:], *res_c[3:]]))
        h_new, za, zb, zc = _mix_fwd(h, ya, yb, yc, ug, p["w_ao"], p["w_bo"], p["w_co"], p["w_out"])
        saved.append(dict(h=h, hn=hn, ug=ug, ua=ua, ub=ub, uc=uc, ya=ya, yconv=yconv, yb=yb, o_hg=o_hg, s_hg=s_hg,
                          yc=yc, o_at=o_at, lse=lse, za=za, zb=zb, zc=zc))
        h = h_new

    dh, loss8 = _loss_head(h, tgt_pad, seq)

    grads = [None] * DEPTH
    parts = [[None, None] for _ in range(DEPTH)]
    pending = None
    tk_dw = 3 * TR if Lp % (3 * TR) == 0 else TR
    for l in reversed(range(DEPTH)):
        p, s = P[l], saved[l]
        dug, mixed, dza, dzb, dzc, dya, dyb, dyc = _mix_bwd(dh, s["za"], s["zb"], s["zc"], s["ug"],
                                                             p["w_ao"], p["w_bo"], p["w_co"], p["w_out"])
        tnmm = functools.partial(_matmul, ta=True, out_dtype=F32, tk=tk_dw)
        g = {}
        g["w_out"] = tnmm(mixed, dh, tm=D_MODEL, tn=D_MODEL, name="dw_out")
        g["w_ao"] = tnmm(s["ya"], dza, tm=512, tn=D_MODEL, name="dw_ao")
        g["w_bo"] = tnmm(s["yb"], dzb, tm=512, tn=D_MODEL, name="dw_bo")
        g["w_co"] = tnmm(s["yc"], dzc, tm=512, tn=D_MODEL, name="dw_co")
        dua, g["conv_w"], g["conv_vec"] = _conv_bwd(s["ua"], s["yconv"], dya, p["conv_w"], p["conv_vec"])
        carry = ("scatter", pending[1]) if pending is not None else None
        res = _hg_bwd(s["ub"], p["lb"], p["gn4"], s["o_hg"], s["s_hg"], dyb, carry)
        dub, g["hg_small"] = res[:2]
        if carry is not None:
            parts[pending[0]][1] = res[2:]
        duc, g["at_gain"], g["at_sink"] = _swa_bwd(s["uc"], p["qg"], p["kg"], p["sinks"], s["o_at"], s["lse"], dyc)
        g["w_g"] = tnmm(s["hn"], dug, tm=D_MODEL, tn=N_G // 2, name="dw_in_g")
        g["w_a"] = tnmm(s["hn"], dua, tm=D_MODEL, tn=N_A, name="dw_in_a")
        g["w_b"] = tnmm(s["hn"], dub, tm=D_MODEL, tn=N_B, name="dw_in_b")
        g["w_c"] = tnmm(s["hn"], duc, tm=D_MODEL, tn=N_C, name="dw_in_c")
        first, second = pack(g) if pack is not None else (None, None)
        res = _inproj_bwd([dug, dua, dub, duc], [p["w_g"], p["w_a"], p["w_b"], p["w_c"]], s["h"], dh, p["norm_g"],
                          ("scatter", first) if first is not None else None)
        dh, g["norm_g"] = res[:2]
        grads[l] = g
        if pack is not None:
            parts[l][0] = res[2:]
            pending = (l, second)
    if pending is not None:
        parts[pending[0]][1] = _exchange_chips("scatter", pending[1])
    return loss8, dh, grads, parts


def _split_w_in(w):
    return dict(w_a=w[:, 0:1536], w_b=w[:, 1536:3584],
                w_c=jnp.concatenate([w[:, 3584:4096], w[:, 4352:4864], w[:, 4096:4352]], axis=1),
                w_g=w[:, 4864:7936])


def _join_w_in(g):
    c = g["w_c"]
    return jnp.concatenate([g["w_a"], g["w_b"], c[:, 0:512], c[:, 1024:1280], c[:, 512:1024], g["w_g"]], axis=1)


def _attn_small(g):
    return (g["at_gain"][0].reshape(ATT_Q_HEADS, ATT_HD).sum(0),
            g["at_gain"][1, 0:128].reshape(ATT_KV_HEADS, ATT_HD).sum(0), g["at_sink"].sum(1))


_SMALL = (("norm_g", 8), ("meta", 32), ("conv_w", 32 * DEPTH), ("conv_b", 8), ("conv_ln_g", 8), ("conv_ln_b", 8),
          ("lb", 8), ("hg_norm_g", 8), ("q_norm_g", 8), ("k_norm_g", 8), ("sinks", 8))


def _small_offsets():
    off, o = {}, 0
    for name, rows in _SMALL:
        off[name] = (o, rows)
        o += rows
    return off, o


def _pack_small(d):
    parts = []
    for name, rows in _SMALL:
        a = d[name]
        parts.append(jnp.pad(a, ((0, rows - a.shape[0]), (0, 512 - a.shape[1]))))
    return jnp.concatenate(parts, axis=0)


def kernel(x, meta_tokens, norm_g, w_in, conv_w, conv_b, conv_ln_g, conv_ln_b, w_conv_out, hg_lower_bounds, hg_norm_g, w_hg_out, q_norm_g, k_norm_g, attn_sinks, w_att_out, w_out, loss_target, m_meta_tokens, m_norm_g, m_w_in, m_conv_w, m_conv_b, m_conv_ln_g, m_conv_ln_b, m_w_conv_out, m_hg_lower_bounds, m_hg_norm_g, m_w_hg_out, m_q_norm_g, m_k_norm_g, m_attn_sinks, m_w_att_out, m_w_out, v_meta_tokens, v_norm_g, v_w_in, v_conv_w, v_conv_b, v_conv_ln_g, v_conv_ln_b, v_w_conv_out, v_hg_lower_bounds, v_hg_norm_g, v_w_hg_out, v_q_norm_g, v_k_norm_g, v_attn_sinks, v_w_att_out, v_w_out):
    xi, yi = lax.axis_index("x"), lax.axis_index("y")
    chip = 2 * xi + yi
    NS = w_in.shape[2]
    CS = conv_w.shape[2]
    MS = meta_tokens.shape[1]

    half = D_MODEL // 2
    shards = [[w_in[l, :half].astype(BF), w_in[l, half:].astype(BF), w_conv_out[l].astype(BF), w_hg_out[l].astype(BF),
               w_att_out[l].astype(BF), w_out[l].astype(BF)] for l in range(DEPTH)]
    *first, g_meta, g_convw = _exchange_chips(
        "gather", shards[0] + [meta_tokens, conv_w.reshape(DEPTH * CONV_WIDTH, CS)])
    cols = lambda g: g.transpose(1, 0, 2).reshape(g.shape[1], -1)
    meta_f = cols(g_meta)
    convw_f = cols(g_convw).reshape(DEPTH, CONV_WIDTH, D_CONV)
    lb_all = _lb_fwd(hg_lower_bounds)

    def prep(l, gathered):
        g_win_top, g_win_bot, g_wao, g_wbo, g_wco, g_wout = gathered
        p = _split_w_in(jnp.concatenate([cols(g_win_top), cols(g_win_bot)], axis=0))
        p.update(w_ao=cols(g_wao), w_bo=cols(g_wbo), w_co=cols(g_wco), w_out=g_wout.reshape(D_MODEL, D_MODEL),
                 norm_g=norm_g[l:l + 1], conv_w=convw_f[l],
                 conv_vec=_pad8(jnp.stack([conv_b[l], conv_ln_g[l], conv_ln_b[l]])),
                 lb=lb_all[l:l + 1], gn4=jnp.tile(hg_norm_g[l:l + 1], (1, HG_HEADS)),
                 qg=jnp.tile(q_norm_g[l:l + 1], (1, ATT_Q_HEADS)), kg=jnp.tile(k_norm_g[l:l + 1], (1, ATT_KV_HEADS)),
                 sinks=attn_sinks[l:l + 1])
        return p

    shard_cols = lambda a: a.reshape(a.shape[0], 4, -1).transpose(1, 0, 2)
    def pack(g):
        win = shard_cols(_join_w_in(g))
        return [win[:, :half]], [win[:, half:], shard_cols(g["w_ao"]), shard_cols(g["w_bo"]), shard_cols(g["w_co"]),
                                 g["w_out"].reshape(4, MS, D_MODEL)]

    loss8, dh0, grads, parts = _local_step(x[0], loss_target[0], meta_f, [prep(0, first)], shards, prep, pack)
    seq = x.shape[1]
    grad_x = dh0[CHUNK:CHUNK + seq][None]
    loss = lax.psum(loss8[0, 0], ("x", "y", "c"))

    sum4 = functools.partial(_sum4, name="sum_chips")
    mine = [jnp.concatenate([t for l in range(DEPTH) for t in (sum4(parts[l][0][0]), sum4(parts[l][1][0]))], axis=0)]
    mine += [jnp.concatenate([sum4(parts[l][1][a]) for l in range(DEPTH)], axis=0) for a in range(1, 5)]
    theirs = _swap_cores(mine)

    dlb_all = jnp.concatenate([grads[l]["hg_small"][0:1] for l in range(DEPTH)], axis=0)
    small = dict(
        norm_g=jnp.concatenate([grads[l]["norm_g"][0:1] for l in range(DEPTH)], axis=0).reshape(8, 512),
        meta=dh0[META_PAD:CHUNK].reshape(32, 512),
        conv_w=jnp.concatenate([grads[l]["conv_w"] for l in range(DEPTH)], axis=0),
        conv_b=jnp.concatenate([grads[l]["conv_vec"][0:1] for l in range(DEPTH)], axis=0),
        conv_ln_g=jnp.concatenate([grads[l]["conv_vec"][1:2] for l in range(DEPTH)], axis=0),
        conv_ln_b=jnp.concatenate([grads[l]["conv_vec"][2:3] for l in range(DEPTH)], axis=0),
        lb=_lb_bwd(hg_lower_bounds, dlb_all),
        hg_norm_g=jnp.concatenate([grads[l]["hg_small"][1:2].reshape(HG_HEADS, HG_D).sum(0, keepdims=True)
                                   for l in range(DEPTH)], axis=0),
        q_norm_g=jnp.stack([_attn_small(grads[l])[0] for l in range(DEPTH)]),
        k_norm_g=jnp.stack([_attn_small(grads[l])[1] for l in range(DEPTH)]),
        sinks=jnp.stack([_attn_small(grads[l])[2] for l in range(DEPTH)]),
    )
    gsum = _allsum_small(_pack_small(small))
    off, _ = _small_offsets()

    def take(name, rows, cols):
        o, _ = off[name]
        return gsum[o:o + rows, 0:cols]

    g_meta_full = take("meta", 32, 512).reshape(N_META, D_MODEL)
    g_convw_full = take("conv_w", 32 * DEPTH, 512).reshape(DEPTH, 32, 512)[:, :CONV_WIDTH]
    small_grads = dict(
        norm_g=take("norm_g", 8, 512),
        meta=lax.dynamic_slice_in_dim(g_meta_full, chip * MS, MS, axis=1),
        conv_w=lax.dynamic_slice_in_dim(g_convw_full, chip * CS, CS, axis=2).reshape(DEPTH * CONV_WIDTH, CS),
        conv_b=take("conv_b", DEPTH, 512), conv_ln_g=take("conv_ln_g", DEPTH, 512), conv_ln_b=take("conv_ln_b", DEPTH, 512),
        lb=take("lb", DEPTH, 512), hg_norm_g=take("hg_norm_g", DEPTH, HG_D), q_norm_g=take("q_norm_g", DEPTH, ATT_HD),
        k_norm_g=take("k_norm_g", DEPTH, ATT_HD), sinks=take("sinks", DEPTH, ATT_Q_HEADS))

    def big_update(w, m, v, a, b, name):
        shp = w.shape
        r2 = lambda t: t.reshape(-1, shp[-1])
        outs = _adamw(r2(w), r2(m), r2(v), a, b, name)
        return [o.reshape(shp) for o in outs]

    res = {}
    res["w_in"] = big_update(w_in, m_w_in, v_w_in, mine[0], theirs[0], "adamw_w_in")
    res["w_conv_out"] = big_update(w_conv_out, m_w_conv_out, v_w_conv_out, mine[1], theirs[1], "adamw_w_ao")
    res["w_hg_out"] = big_update(w_hg_out, m_w_hg_out, v_w_hg_out, mine[2], theirs[2], "adamw_w_bo")
    res["w_att_out"] = big_update(w_att_out, m_w_att_out, v_w_att_out, mine[3], theirs[3], "adamw_w_co")
    res["w_out"] = big_update(w_out, m_w_out, v_w_out, mine[4], theirs[4], "adamw_w_out")

    small_w = dict(meta=(meta_tokens, m_meta_tokens, v_meta_tokens), norm_g=(norm_g, m_norm_g, v_norm_g),
                   conv_w=(conv_w, m_conv_w, v_conv_w), conv_b=(conv_b, m_conv_b, v_conv_b),
                   conv_ln_g=(conv_ln_g, m_conv_ln_g, v_conv_ln_g), conv_ln_b=(conv_ln_b, m_conv_ln_b, v_conv_ln_b),
                   lb=(hg_lower_bounds, m_hg_lower_bounds, v_hg_lower_bounds),
                   hg_norm_g=(hg_norm_g, m_hg_norm_g, v_hg_norm_g), q_norm_g=(q_norm_g, m_q_norm_g, v_q_norm_g),
                   k_norm_g=(k_norm_g, m_k_norm_g, v_k_norm_g), sinks=(attn_sinks, m_attn_sinks, v_attn_sinks))
    view = lambda n, t: t.reshape(-1, 512) if n == "norm_g" else t.reshape(-1, t.shape[-1])
    pw, pm, pv = (_pack_rows([view(n, small_w[n][k]) for n in small_w]) for k in range(3))
    pg = _pack_rows([small_grads[n] for n in small_w])
    packed = _adamw(pw, pm, pv, pg, None, "adamw_small")
    o = 0
    for n in small_w:
        r, cdim = view(n, small_w[n][0]).shape
        res[n] = [t[o:o + r, 0:cdim].reshape(small_w[n][0].shape) for t in packed]
        o += -(-r // 8) * 8

    order = [("meta", None), ("norm_g", None), ("w_in", None), ("conv_w", None), ("conv_b", None), ("conv_ln_g", None),
             ("conv_ln_b", None), ("w_conv_out", None), ("lb", None), ("hg_norm_g", None), ("w_hg_out", None),
             ("q_norm_g", None), ("k_norm_g", None), ("sinks", None), ("w_att_out", None), ("w_out", None)]
    outs = [loss, grad_x]
    for k in range(4):
        outs += [res[n][k] for n, _ in order]
    return tuple(outs)


def _pack_rows(arrs):
    parts = []
    for a in arrs:
        r = (-a.shape[0]) % 8
        parts.append(jnp.pad(a, ((0, r), (0, 512 - a.shape[1]))))
    return jnp.concatenate(parts, axis=0)
```

```python
import functools

import jax
import jax.numpy as jnp
from jax import lax
from jax.experimental import pallas as pl
from jax.experimental.pallas import tpu as pltpu

F32 = jnp.float32
BF = jnp.bfloat16

D_MODEL = 1024
DEPTH = 4
CHUNK = 64
N_META = 16
META_PAD = CHUNK - N_META
D_CONV = 512
CONV_WIDTH = 31
HG_HEADS = 4
HG_D = 128
ATT_Q_HEADS = 8
ATT_KV_HEADS = 2
ATT_HD = 64
ATT_GROUP = ATT_Q_HEADS // ATT_KV_HEADS
EPS = 1e-6
F_FLOOR = 1e-30
NEG = -1e30

ADAM_LR = 0.001
ADAM_B1 = 0.9
ADAM_B2 = 0.999
ADAM_EPS = 1e-08
ADAM_WD = 0.01
ADAM_STEP = 10

TR = 640
TRM = TR // 2
CONV_RB = 32
QB = 128
HALO = 128
VMEM_LIMIT = 56 * 1024 * 1024

N_G, N_A, N_B, N_C = 3 * D_MODEL, 3 * D_CONV, 4 * 512, 2 * 512 + 2 * 128

MESH = pl.DeviceIdType.MESH


def _cp(sem=None, vmem=VMEM_LIMIT, **kw):
    if sem is None:
        return pltpu.CompilerParams(vmem_limit_bytes=vmem, **kw)
    return pltpu.CompilerParams(dimension_semantics=sem, vmem_limit_bytes=vmem, **kw)


def _nn(a, b):
    return lax.dot_general(a, b, (((1,), (0,)), ((), ())), preferred_element_type=F32)


def _nt(a, b):
    return lax.dot_general(a, b, (((1,), (1,)), ((), ())), preferred_element_type=F32)


def _tn(a, b):
    return lax.dot_general(a, b, (((0,), (0,)), ((), ())), preferred_element_type=F32)


def _sig(x):
    return jax.nn.sigmoid(x)


def _silu(x):
    return x * _sig(x)


def _dsilu(x):
    s = _sig(x)
    return s * (1.0 + x * (1.0 - s))


def _split3(x):
    hi = x.astype(BF)
    r1 = x - hi.astype(F32)
    mid = r1.astype(BF)
    lo = (r1 - mid.astype(F32)).astype(BF)
    return hi, mid, lo


def _mm3(t, x):
    hi, mid, lo = _split3(x)
    return _nn(t, hi) + _nn(t, mid) + _nn(t, lo)


def _chunk_tri(n, upper):
    r = lax.broadcasted_iota(jnp.int32, (n, n), 0)
    c = lax.broadcasted_iota(jnp.int32, (n, n), 1)
    same = jnp.right_shift(r, 6) == jnp.right_shift(c, 6)
    tri = (c >= r) if upper else (c <= r)
    return jnp.where(same & tri, 1.0, 0.0).astype(BF)


def _matmul(a, b, *, ta=False, tb=False, out_dtype, tm, tn, tk, name, col_major_grid=False):
    if ta:
        K, M = a.shape
    else:
        M, K = a.shape
    N = b.shape[0] if tb else b.shape[1]
    assert M % tm == 0 and N % tn == 0 and K % tk == 0, (name, M, N, K, tm, tn, tk)
    nk = K // tk
    if col_major_grid:
        grid = (N // tn, M // tm, nk)
        ij = lambda g0, g1: (g1, g0)
    else:
        grid = (M // tm, N // tn, nk)
        ij = lambda g0, g1: (g0, g1)
    if ta:
        a_spec = pl.BlockSpec((tk, tm), lambda g0, g1, k: (k, ij(g0, g1)[0]))
    else:
        a_spec = pl.BlockSpec((tm, tk), lambda g0, g1, k: (ij(g0, g1)[0], k))
    if tb:
        b_spec = pl.BlockSpec((tn, tk), lambda g0, g1, k: (ij(g0, g1)[1], k))
    else:
        b_spec = pl.BlockSpec((tk, tn), lambda g0, g1, k: (k, ij(g0, g1)[1]))
    o_spec = pl.BlockSpec((tm, tn), lambda g0, g1, k: ij(g0, g1))
    dims = (((0 if ta else 1,), (1 if tb else 0,)), ((), ()))
    use_acc = nk > 1 and out_dtype != F32

    def body(a_ref, b_ref, o_ref, *scr):
        k = pl.program_id(2)
        p = lax.dot_general(a_ref[...].astype(BF), b_ref[...].astype(BF), dims, preferred_element_type=F32)
        if nk == 1:
            o_ref[...] = p.astype(out_dtype)
        else:
            acc = scr[0] if use_acc else o_ref

            @pl.when(k == 0)
            def _():
                acc[...] = p

            @pl.when(k > 0)
            def _():
                acc[...] += p

            if use_acc:
                @pl.when(k == nk - 1)
                def _():
                    o_ref[...] = acc[...].astype(out_dtype)

    return pl.pallas_call(
        body, name=name, grid=grid, in_specs=[a_spec, b_spec], out_specs=o_spec,
        out_shape=jax.ShapeDtypeStruct((M, N), out_dtype),
        scratch_shapes=[pltpu.VMEM((tm, tn), F32)] if use_acc else [],
        compiler_params=_cp(("parallel", "parallel", "arbitrary")),
    )(a, b)


def _rms_fwd(h, g):
    Lp = h.shape[0]

    def body(h_ref, g_ref, o_ref):
        x = h_ref[...]
        r = lax.rsqrt(jnp.mean(x * x, axis=-1, keepdims=True) + EPS)
        o_ref[...] = (x * r * g_ref[...]).astype(BF)

    return pl.pallas_call(
        body, name="rms_fwd", grid=(Lp // TR,),
        in_specs=[pl.BlockSpec((TR, D_MODEL), lambda i: (i, 0)), pl.BlockSpec((1, D_MODEL), lambda i: (0, 0))],
        out_specs=pl.BlockSpec((TR, D_MODEL), lambda i: (i, 0)),
        out_shape=jax.ShapeDtypeStruct((Lp, D_MODEL), BF),
        compiler_params=_cp(("parallel",)),
    )(h, g)


def _glu(ua, row):
    a = ua[:, 0:D_CONV].astype(F32)
    gl = ua[:, D_CONV:2 * D_CONV].astype(F32)
    return jnp.where(row >= META_PAD, a * _sig(gl), 0.0)


_SH_ROWS = TR + CHUNK - 8


def _fill_shifts(src, sh):
    for b in range(1, 8):
        sh[b - 1] = src[pl.ds(b, _SH_ROWS), :]


def _shifted(src, sh, start, n):
    b = start % 8
    if b == 0:
        return src[pl.ds(start, n), :]
    return sh[b - 1, pl.ds(start - b, n), :]


def _conv_fwd(ua, cw, cvec, carry=None):
    Lp = ua.shape[0]
    nt = Lp // TR
    hb = TR // CHUNK

    def body(cur_ref, halo_ref, w_ref, v_ref, ya_ref, yc_ref, ext, sh):
        i = pl.program_id(0)
        row = i * TR + lax.broadcasted_iota(jnp.int32, (TR, 1), 0)
        hrow = i * TR - CHUNK + lax.broadcasted_iota(jnp.int32, (CHUNK, 1), 0)
        ext[pl.ds(0, CHUNK), :] = jnp.where(i > 0, _glu(halo_ref[...], hrow), 0.0)
        ext[pl.ds(CHUNK, TR), :] = _glu(cur_ref[...], row)
        _fill_shifts(ext, sh)
        for rb in range(TR // CONV_RB):
            r0 = rb * CONV_RB
            rows = pl.ds(r0, CONV_RB)
            acc = jnp.zeros((CONV_RB, D_CONV), F32)
            for j in range(CONV_WIDTH):
                acc = acc + _shifted(ext, sh, r0 + CHUNK - (CONV_WIDTH - 1) + j, CONV_RB) * w_ref[j:j + 1, :]
            y = acc + v_ref[0:1, :]
            yc_ref[rows, :] = y
            mu = jnp.mean(y, axis=-1, keepdims=True)
            d = y - mu
            var = jnp.mean(d * d, axis=-1, keepdims=True)
            yn = d * lax.rsqrt(var + EPS) * v_ref[1:2, :] + v_ref[2:3, :]
            ya_ref[rows, :] = (_silu(yn) * _silu(cur_ref[rows, 2 * D_CONV:3 * D_CONV].astype(F32))).astype(BF)

    in_specs = [pl.BlockSpec((TR, N_A), lambda i: (i, 0)),
                pl.BlockSpec((CHUNK, N_A), lambda i: (jnp.maximum(i * hb - 1, 0), 0)),
                pl.BlockSpec((CONV_WIDTH, D_CONV), lambda i: (0, 0)),
                pl.BlockSpec((8, D_CONV), lambda i: (0, 0))]
    out_specs = [pl.BlockSpec((TR, D_CONV), lambda i: (i, 0)), pl.BlockSpec((TR, D_CONV), lambda i: (i, 0))]
    out_shape = [jax.ShapeDtypeStruct((Lp, D_CONV), BF), jax.ShapeDtypeStruct((Lp, D_CONV), F32)]
    scratch = [pltpu.VMEM((TR + CHUNK, D_CONV), F32), pltpu.VMEM((7, _SH_ROWS, D_CONV), F32)]
    return _call_carrying(body, "conv_fwd", nt, in_specs, out_specs, out_shape, scratch, (ua, ua, cw, cvec), carry)


def _conv_bwd(ua, yconv, dya, cw, cvec):
    Lp = ua.shape[0]
    nt = Lp // TR
    hb = TR // CHUNK
    nhb = Lp // CHUNK

    def ln_bwd(y, dout, gate, v_ref):
        mu = jnp.mean(y, axis=-1, keepdims=True)
        d = y - mu
        var = jnp.mean(d * d, axis=-1, keepdims=True)
        rstd = lax.rsqrt(var + EPS)
        xhat = d * rstd
        yn = xhat * v_ref[1:2, :] + v_ref[2:3, :]
        dyn = dout * _silu(gate) * _dsilu(yn)
        dxh = dyn * v_ref[1:2, :]
        dyc = rstd * (dxh - jnp.mean(dxh, axis=-1, keepdims=True) - xhat * jnp.mean(dxh * xhat, axis=-1, keepdims=True))
        return dyc, dyn, xhat, yn

    def body(cur_ref, prev_ref, next_ref, yc_ref, ycn_ref, dy_ref, dyn_ref, w_ref, v_ref,
             du_ref, dw_ref, dv_ref, uext, dext, dwacc, ush, dsh):
        i = pl.program_id(0)

        @pl.when(i == 0)
        def _():
            dwacc[...] = jnp.zeros_like(dwacc)
            dv_ref[...] = jnp.zeros_like(dv_ref)

        row = i * TR + lax.broadcasted_iota(jnp.int32, (TR, 1), 0)
        hrow = i * TR - CHUNK + lax.broadcasted_iota(jnp.int32, (CHUNK, 1), 0)
        uext[pl.ds(0, CHUNK), :] = jnp.where(i > 0, _glu(prev_ref[...], hrow), 0.0)
        uext[pl.ds(CHUNK, TR), :] = _glu(cur_ref[...], row)

        s_b = jnp.zeros((1, D_CONV), F32)
        s_g = jnp.zeros((1, D_CONV), F32)
        s_bb = jnp.zeros((1, D_CONV), F32)
        for rb in range(TR // CONV_RB):
            rows = pl.ds(rb * CONV_RB, CONV_RB)
            gate = cur_ref[rows, 2 * D_CONV:3 * D_CONV].astype(F32)
            dout = dy_ref[rows, :].astype(F32)
            dyc, dyn, xhat, yn = ln_bwd(yc_ref[rows, :], dout, gate, v_ref)
            du_ref[rows, 2 * D_CONV:3 * D_CONV] = (dout * _silu(yn) * _dsilu(gate)).astype(BF)
            dext[rows, :] = dyc
            s_b = s_b + jnp.sum(dyc, axis=0, keepdims=True)
            s_g = s_g + jnp.sum(dyn * xhat, axis=0, keepdims=True)
            s_bb = s_bb + jnp.sum(dyn, axis=0, keepdims=True)
        dv_ref[0:1, :] += s_b
        dv_ref[1:2, :] += s_g
        dv_ref[2:3, :] += s_bb
        dyc_n, _, _, _ = ln_bwd(ycn_ref[...], dyn_ref[...].astype(F32),
                                next_ref[:, 2 * D_CONV:3 * D_CONV].astype(F32), v_ref)
        dext[pl.ds(TR, CHUNK), :] = jnp.where(i < nt - 1, dyc_n, 0.0)
        _fill_shifts(uext, ush)
        _fill_shifts(dext, dsh)

        for rb in range(TR // CONV_RB):
            r0 = rb * CONV_RB
            rows = pl.ds(r0, CONV_RB)
            d_blk = dext[rows, :]
            dglu = jnp.zeros((CONV_RB, D_CONV), F32)
            for j in range(CONV_WIDTH):
                dglu = dglu + _shifted(dext, dsh, r0 + CONV_WIDTH - 1 - j, CONV_RB) * w_ref[j:j + 1, :]
                prod = d_blk * _shifted(uext, ush, r0 + CHUNK - (CONV_WIDTH - 1) + j, CONV_RB)
                part = prod[0:8, :]
                for s in range(1, CONV_RB // 8):
                    part = part + prod[8 * s:8 * s + 8, :]
                dwacc[j] += part
            a = cur_ref[rows, 0:D_CONV].astype(F32)
            sg = _sig(cur_ref[rows, D_CONV:2 * D_CONV].astype(F32))
            grow = i * TR + r0 + lax.broadcasted_iota(jnp.int32, (CONV_RB, 1), 0)
            dglu = jnp.where(grow >= META_PAD, dglu, 0.0)
            du_ref[rows, 0:D_CONV] = (dglu * sg).astype(BF)
            du_ref[rows, D_CONV:2 * D_CONV] = (dglu * a * sg * (1.0 - sg)).astype(BF)

        @pl.when(i == nt - 1)
        def _():
            dw_ref[...] = jnp.sum(dwacc[...], axis=1)

    nxt = lambda i: (jnp.minimum(i * hb + hb, nhb - 1), 0)
    return pl.pallas_call(
        body, name="conv_bwd", grid=(nt,),
        in_specs=[pl.BlockSpec((TR, N_A), lambda i: (i, 0)),
                  pl.BlockSpec((CHUNK, N_A), lambda i: (jnp.maximum(i * hb - 1, 0), 0)),
                  pl.BlockSpec((CHUNK, N_A), nxt),
                  pl.BlockSpec((TR, D_CONV), lambda i: (i, 0)),
                  pl.BlockSpec((CHUNK, D_CONV), nxt),
                  pl.BlockSpec((TR, D_CONV), lambda i: (i, 0)),
                  pl.BlockSpec((CHUNK, D_CONV), nxt),
                  pl.BlockSpec((CONV_WIDTH, D_CONV), lambda i: (0, 0)),
                  pl.BlockSpec((8, D_CONV), lambda i: (0, 0))],
        out_specs=[pl.BlockSpec((TR, N_A), lambda i: (i, 0)),
                   pl.BlockSpec((32, D_CONV), lambda i: (0, 0)),
                   pl.BlockSpec((8, D_CONV), lambda i: (0, 0))],
        out_shape=[jax.ShapeDtypeStruct((Lp, N_A), BF), jax.ShapeDtypeStruct((32, D_CONV), F32),
                   jax.ShapeDtypeStruct((8, D_CONV), F32)],
        scratch_shapes=[pltpu.VMEM((TR + CHUNK, D_CONV), F32), pltpu.VMEM((TR + CHUNK, D_CONV), F32),
                        pltpu.VMEM((32, 8, D_CONV), F32), pltpu.VMEM((7, _SH_ROWS, D_CONV), F32),
                        pltpu.VMEM((7, _SH_ROWS, D_CONV), F32)],
        compiler_params=_cp(("arbitrary",)),
    )(ua, ua, ua, yconv, yconv, dya, dya, cw, cvec)


def _hg_gates(ub_ref, lbv, row):
    q = ub_ref[:, 0:512].astype(F32)
    z = ub_ref[:, 512:1024].astype(F32)
    valid = row >= META_PAD
    sig = _sig(z)
    f = lbv + (1.0 - lbv) * sig
    g = jnp.where(valid, jnp.log(jnp.maximum(f, F_FLOOR)), 0.0)
    k = jnp.where(valid, (1.0 - lbv) * (1.0 - sig), 0.0)
    return q, k, g, sig, f


def _hg_chunk_terms(b_c, q_c, k_c):
    bm = b_c[CHUNK // 2 - 1:CHUNK // 2, :]
    bl = b_c[CHUNK - 1:CHUNK, :]
    e1 = jnp.exp(b_c - bm)
    e2 = jnp.exp(bm - b_c)
    e0 = jnp.exp(b_c)
    e3 = jnp.exp(bl - b_c)
    el = jnp.exp(bl)
    return e1, e2, e0, e3, el, q_c * e1, k_c * e2, q_c * e0, k_c * e3


def _hg_fwd(ub, lb, gn4, carry=None):
    Lp = ub.shape[0]
    nt = Lp // TR
    cpt = TR // CHUNK

    def body(ub_ref, lb_ref, gn_ref, yb_ref, o_ref, ss_ref, st, bsc, qsc, ksc, qes, els, ust):
        i = pl.program_id(0)

        @pl.when(i == 0)
        def _():
            st[...] = jnp.zeros_like(st)

        row = i * TR + lax.broadcasted_iota(jnp.int32, (TR, 1), 0)
        q, k, g, _, _ = _hg_gates(ub_ref, lb_ref[...], row)
        qsc[...] = _silu(q)
        ksc[...] = k
        bsc[...] = _mm3(_chunk_tri(TR, False), g)
        tri = lax.broadcasted_iota(jnp.int32, (CHUNK, CHUNK), 1) <= lax.broadcasted_iota(jnp.int32, (CHUNK, CHUNK), 0)

        def intra(c, carry):
            rows = pl.ds(pl.multiple_of(c * CHUNK, CHUNK), CHUNK)
            _, _, _, _, el, qe, ke, qE, kd = _hg_chunk_terms(bsc[rows, :], qsc[rows, :], ksc[rows, :])
            qe, ke, kd = qe.astype(BF), ke.astype(BF), kd.astype(BF)
            qes[rows, :] = qE.astype(BF)
            els[c] = jnp.broadcast_to(el, (8, 512))
            sls = [slice(HG_D * h, HG_D * (h + 1)) for h in range(HG_HEADS)]
            v = [ub_ref[rows, 1024 + HG_D * h:1024 + HG_D * (h + 1)] for h in range(HG_HEADS)]
            a = [_nt(qe[:, sl], ke[:, sl]) for sl in sls]
            u = [_tn(v[h], kd[:, sls[h]]) for h in range(HG_HEADS)]
            a = [jnp.where(tri, x, 0.0).astype(BF) for x in a]
            oi = [_nn(a[h], v[h]) for h in range(HG_HEADS)]
            for h in range(HG_HEADS):
                ust[c, h] = u[h]
                o_ref[rows, sls[h]] = oi[h]
            return carry

        lax.fori_loop(0, cpt, intra, 0, unroll=2)

        for h in range(HG_HEADS):
            sl = slice(HG_D * h, HG_D * (h + 1))
            s = st[h]
            for c in range(cpt):
                ss_ref[c, h] = s
                s = els[c, 0:1, sl] * s + ust[c, h]
            st[h] = s

        def inter(c, carry):
            rows = pl.ds(pl.multiple_of(c * CHUNK, CHUNK), CHUNK)
            for h in range(HG_HEADS):
                sl = slice(HG_D * h, HG_D * (h + 1))
                o_ref[rows, sl] += _nt(qes[rows, sl], ss_ref[c, h].astype(BF))
            return carry

        lax.fori_loop(0, cpt, inter, 0, unroll=2)

        gate = ub_ref[:, 1536:2048].astype(F32)
        for h in range(HG_HEADS):
            sl = slice(HG_D * h, HG_D * (h + 1))
            o = o_ref[:, sl]
            r = lax.rsqrt(jnp.mean(o * o, axis=-1, keepdims=True) + EPS)
            yb_ref[:, sl] = (o * r * gn_ref[:, sl] * _silu(gate[:, sl])).astype(BF)

    in_specs = [pl.BlockSpec((TR, N_B), lambda i: (i, 0)), pl.BlockSpec((1, 512), lambda i: (0, 0)),
                pl.BlockSpec((1, 512), lambda i: (0, 0))]
    out_specs = [pl.BlockSpec((TR, 512), lambda i: (i, 0)), pl.BlockSpec((TR, 512), lambda i: (i, 0)),
                 pl.BlockSpec((cpt, HG_HEADS, HG_D, HG_D), lambda i: (i, 0, 0, 0))]
    out_shape = [jax.ShapeDtypeStruct((Lp, 512), BF), jax.ShapeDtypeStruct((Lp, 512), F32),
                 jax.ShapeDtypeStruct((Lp // CHUNK, HG_HEADS, HG_D, HG_D), F32)]
    scratch = [pltpu.VMEM((HG_HEADS, HG_D, HG_D), F32), pltpu.VMEM((TR, 512), F32),
               pltpu.VMEM((TR, 512), F32), pltpu.VMEM((TR, 512), F32), pltpu.VMEM((TR, 512), BF),
               pltpu.VMEM((cpt, 8, 512), F32), pltpu.VMEM((cpt, HG_HEADS, HG_D, HG_D), F32)]
    return _call_carrying(body, "hgrn_fwd", nt, in_specs, out_specs, out_shape, scratch, (ub, lb, gn4), carry)


def _hg_bwd(ub, lb, gn4, o_save, s_save, dyb, carry=None):
    Lp = ub.shape[0]
    nt = Lp // TR
    cpt = TR // CHUNK

    def body(ub_ref, lb_ref, gn_ref, o_ref, ss_ref, dy_ref, du_ref, ds_ref,
             dst, bsc, qsc, ksc, dosc, dqsc, dksc, dbsc, els, ust, dss):
        i = pl.program_id(0)
        t = nt - 1 - i

        @pl.when(i == 0)
        def _():
            dst[...] = jnp.zeros_like(dst)
            ds_ref[...] = jnp.zeros_like(ds_ref)

        lbv = lb_ref[...]
        row = t * TR + lax.broadcasted_iota(jnp.int32, (TR, 1), 0)
        valid = row >= META_PAD
        q, k, g, sig, f = _hg_gates(ub_ref, lbv, row)
        qsc[...] = _silu(q)
        ksc[...] = k
        bsc[...] = _mm3(_chunk_tri(TR, False), g)

        gate = ub_ref[:, 1536:2048].astype(F32)
        dy = dy_ref[...].astype(F32)
        dgn = jnp.zeros((1, 512), F32)
        for h in range(HG_HEADS):
            sl = slice(HG_D * h, HG_D * (h + 1))
            o = o_ref[:, sl]
            r = lax.rsqrt(jnp.mean(o * o, axis=-1, keepdims=True) + EPS)
            ohat = o * r
            don = dy[:, sl] * _silu(gate[:, sl])
            du_ref[:, 1536 + HG_D * h:1536 + HG_D * (h + 1)] = (
                dy[:, sl] * ohat * gn_ref[:, sl] * _dsilu(gate[:, sl])).astype(BF)
            ds_ref[1:2, sl] += jnp.sum(don * ohat, axis=0, keepdims=True)
            gd = don * gn_ref[:, sl]
            dosc[:, sl] = r * (gd - ohat * jnp.mean(gd * ohat, axis=-1, keepdims=True))

        tri = lax.broadcasted_iota(jnp.int32, (CHUNK, CHUNK), 1) <= lax.broadcasted_iota(jnp.int32, (CHUNK, CHUNK), 0)
        last = lax.broadcasted_iota(jnp.int32, (CHUNK, 1), 0) == CHUNK - 1

        def incr(c, carry):
            rows = pl.ds(pl.multiple_of(c * CHUNK, CHUNK), CHUNK)
            b_c = bsc[rows, :]
            qE_b = (qsc[rows, :] * jnp.exp(b_c)).astype(BF)
            els[c] = jnp.broadcast_to(jnp.exp(b_c[CHUNK - 1:CHUNK, :]), (8, 512))
            do_c = dosc[rows, :].astype(BF)
            for h in range(HG_HEADS):
                sl = slice(HG_D * h, HG_D * (h + 1))
                ust[c, h] = _tn(do_c[:, sl], qE_b[:, sl])
            return carry

        lax.fori_loop(0, cpt, incr, 0, unroll=2)

        for h in range(HG_HEADS):
            sl = slice(HG_D * h, HG_D * (h + 1))
            d_s = dst[h]
            for c in reversed(range(cpt)):
                dss[c, h] = d_s
                d_s = els[c, 0:1, sl] * d_s + ust[c, h]
            dst[h] = d_s

        def chunk(c, carry):
            r0 = pl.multiple_of(c * CHUNK, CHUNK)
            rows = pl.ds(r0, CHUNK)
            e1, e2, e0, e3, el, qe, ke, qE, kd = _hg_chunk_terms(bsc[rows, :], qsc[rows, :], ksc[rows, :])
            qe_b, ke_b, kd_b = qe.astype(BF), ke.astype(BF), kd.astype(BF)
            do_c = dosc[rows, :].astype(BF)
            hs = range(HG_HEADS)
            sls = [slice(HG_D * h, HG_D * (h + 1)) for h in hs]
            v = [ub_ref[rows, 1024 + HG_D * h:1024 + HG_D * (h + 1)] for h in hs]
            do = [do_c[:, sl] for sl in sls]
            a = [_nt(qe_b[:, sl], ke_b[:, sl]) for sl in sls]
            da = [_nt(do[h], v[h]) for h in hs]
            dqE = [_nn(do[h], ss_ref[c, h].astype(BF)) for h in hs]
            dkd = [_nn(v[h], dss[c, h].astype(BF)) for h in hs]
            dv2 = [_nt(kd_b[:, sls[h]], dss[c, h].astype(BF)) for h in hs]
            a = [jnp.where(tri, x, 0.0).astype(BF) for x in a]
            da = [jnp.where(tri, x, 0.0).astype(BF) for x in da]
            dv = [_tn(a[h], do[h]) + dv2[h] for h in hs]
            dqe = [_nn(da[h], ke_b[:, sls[h]]) for h in hs]
            dke = [_tn(da[h], qe_b[:, sls[h]]) for h in hs]
            for h in hs:
                sl = sls[h]
                del_h = jnp.sum(ss_ref[c, h] * dss[c, h], axis=0, keepdims=True)
                dqsc[rows, sl] = dqE[h] * e0[:, sl] + dqe[h] * e1[:, sl]
                dksc[rows, sl] = dke[h] * e2[:, sl] + dkd[h] * e3[:, sl]
                tkd = dkd[h] * kd[:, sl]
                dbl = jnp.sum(tkd, axis=0, keepdims=True) + del_h * el[:, sl]
                dbsc[rows, sl] = (dqE[h] * qE[:, sl] + dqe[h] * qe[:, sl] - dke[h] * ke[:, sl] - tkd
                                  + jnp.where(last, dbl, 0.0))
                du_ref[rows, 1024 + HG_D * h:1024 + HG_D * (h + 1)] = dv[h].astype(BF)
            return carry

        lax.fori_loop(0, cpt, chunk, 0, unroll=2)

        dg = _mm3(_chunk_tri(TR, True), dbsc[...])
        df = jnp.where(valid & (f > F_FLOOR), dg / f, 0.0)
        dk = jnp.where(valid, dksc[...], 0.0)
        dsig = (df - dk) * (1.0 - lbv)
        ds_ref[0:1, :] += jnp.sum((df - dk) * (1.0 - sig), axis=0, keepdims=True)
        du_ref[:, 512:1024] = (dsig * sig * (1.0 - sig)).astype(BF)
        du_ref[:, 0:512] = (dqsc[...] * _dsilu(q)).astype(BF)

    rev = lambda i: (nt - 1 - i, 0)
    in_specs = [pl.BlockSpec((TR, N_B), rev), pl.BlockSpec((1, 512), lambda i: (0, 0)),
                pl.BlockSpec((1, 512), lambda i: (0, 0)), pl.BlockSpec((TR, 512), rev),
                pl.BlockSpec((cpt, HG_HEADS, HG_D, HG_D), lambda i: (nt - 1 - i, 0, 0, 0)),
                pl.BlockSpec((TR, 512), rev)]
    out_specs = [pl.BlockSpec((TR, N_B), rev), pl.BlockSpec((8, 512), lambda i: (0, 0))]
    out_shape = [jax.ShapeDtypeStruct((Lp, N_B), BF), jax.ShapeDtypeStruct((8, 512), F32)]
    states = pltpu.VMEM((cpt, HG_HEADS, HG_D, HG_D), F32)
    scratch = ([pltpu.VMEM((HG_HEADS, HG_D, HG_D), F32)] + [pltpu.VMEM((TR, 512), F32)] * 7
               + [pltpu.VMEM((cpt, 8, 512), F32), states, states])
    return _call_carrying(body, "hgrn_bwd", nt, in_specs, out_specs, out_shape, scratch,
                          (ub, lb, gn4, o_save, s_save, dyb), carry)


_KCOL = (2 * 512) // 128
_VCOL = _KCOL + 1


def _swa_in_specs(nt, rev):
    tile = (lambda i: nt - 1 - i) if rev else (lambda i: i)
    hpt = TR // HALO
    return [
        pl.BlockSpec((TR, 512), lambda i: (tile(i), 0)),
        pl.BlockSpec((TR, 512), lambda i: (tile(i), 1)),
        pl.BlockSpec((TR, 128), lambda i: (tile(i), _KCOL)),
        pl.BlockSpec((TR, 128), lambda i: (tile(i), _VCOL)),
        pl.BlockSpec((HALO, 128), lambda i: (jnp.maximum(tile(i) * hpt - 1, 0), _KCOL)),
        pl.BlockSpec((HALO, 128), lambda i: (jnp.maximum(tile(i) * hpt - 1, 0), _VCOL)),
        pl.BlockSpec((CHUNK, 128), lambda i: (0, _KCOL)),
        pl.BlockSpec((CHUNK, 128), lambda i: (0, _VCOL)),
        pl.BlockSpec((1, 512), lambda i: (0, 0)),
        pl.BlockSpec((1, 128), lambda i: (0, 0)),
        pl.BlockSpec((1, ATT_Q_HEADS), lambda i: (0, 0)),
    ]


_WROWS = 2 * CHUNK + HALO + TR
_W0 = 2 * CHUNK
_C0 = _W0 + HALO
_SCALE = ATT_HD ** -0.5


def _group_ones(n):
    r = lax.broadcasted_iota(jnp.int32, (n, n), 0)
    c = lax.broadcasted_iota(jnp.int32, (n, n), 1)
    return jnp.where(jnp.right_shift(r, 6) == jnp.right_shift(c, 6), 1.0, 0.0).astype(BF)


def _group_mean(x, ones):
    hi = x.astype(BF)
    lo = (x - hi.astype(F32)).astype(BF)
    return (_nn(hi, ones) + _nn(lo, ones)) * (1.0 / ATT_HD)


def _head_rms(x, ones):
    r = lax.rsqrt(_group_mean(x * x, ones) + EPS)
    return x * r, r


def _swa_windows(kc_ref, vc_ref, kh_ref, vh_ref, km_ref, vm_ref, kg2, ones, kwin, krwin, vwin, vrwin):
    meta = pl.ds(META_PAD, N_META)
    for (k, v, r0, n) in ((km_ref[meta, :], vm_ref[meta, :], 0, N_META), (kh_ref[...], vh_ref[...], _W0, HALO),
                          (kc_ref[...], vc_ref[...], _C0, TR)):
        xhat, _ = _head_rms(k.astype(F32), ones)
        kn = xhat * kg2
        kwin[pl.ds(r0, n), :] = kn.astype(BF)
        krwin[pl.ds(r0, n), :] = pltpu.roll(kn, ATT_HD, 1).astype(BF)
        vwin[pl.ds(r0, n), :] = v
        if vrwin is not None:
            vrwin[pl.ds(r0, n), :] = pltpu.roll(v.astype(F32), ATT_HD, 1).astype(BF)
    zero = jnp.zeros((_W0 - N_META, 128), BF)
    for w in (kwin, krwin, vwin, vrwin):
        if w is not None:
            w[pl.ds(N_META, _W0 - N_META), :] = zero


def _swa_masks_t(t, qb):
    q0 = t * TR + qb * QB
    qc = jnp.right_shift(q0 + lax.broadcasted_iota(jnp.int32, (1, QB), 1), 6)
    kabs = q0 - HALO + lax.broadcasted_iota(jnp.int32, (QB + HALO, 1), 0)
    kc = jnp.right_shift(kabs + HALO, 6) - HALO // CHUNK
    mask_w = (kc <= qc) & (kc >= qc - 2) & (kabs >= META_PAD)
    return qc > 2, mask_w


def _swa_park(dtype):
    return [pltpu.VMEM((ATT_Q_HEADS, N_META, QB), dtype), pltpu.VMEM((ATT_Q_HEADS, QB + HALO, QB), dtype)]


def _split_heads(x, lane_hi):
    return jnp.where(lane_hi, 0.0, x).astype(BF), jnp.where(lane_hi, x, 0.0).astype(BF)


def _call_carrying(body, name, nt, in_specs, out_specs, out_shape, scratch, args, carry):
    if carry is None:
        return pl.pallas_call(body, name=name, grid=(nt,), in_specs=in_specs, out_specs=out_specs, out_shape=out_shape,
                              scratch_shapes=scratch, compiler_params=_cp(("arbitrary",)))(*args)
    kind, arrs = carry
    n = len(arrs)
    return pl.pallas_call(
        _carry_exchange(body, len(in_specs), len(out_specs), nt, kind, n), name=name + "_" + kind, grid=(nt,),
        in_specs=in_specs + [_ANY] * n, out_specs=out_specs + [_ANY] * n,
        out_shape=out_shape + _exchange_out_shapes(kind, arrs), scratch_shapes=scratch + _exchange_sems(n),
        compiler_params=_cp(("arbitrary",), has_side_effects=True),
    )(*args, *arrs)


def _swa_fwd(uc, qg8, kg2, sinks, carry=None):
    Lp = uc.shape[0]
    nt = Lp // TR
    nqb = TR // QB

    def body(q_ref, g_ref, kc_ref, vc_ref, kh_ref, vh_ref, km_ref, vm_ref, qg_ref, kg_ref, sk_ref,
             yc_ref, o_ref, lse_ref, kwin, krwin, vwin, vt, qlo, qhi, ot, s_m, s_w, p_m, p_w):
        t = pl.program_id(0)
        _swa_windows(kc_ref, vc_ref, kh_ref, vh_ref, km_ref, vm_ref, kg_ref[...], _group_ones(128),
                     kwin, krwin, vwin, None)
        vt[...] = vwin[...].T
        xhat, _ = _head_rms(q_ref[...].astype(F32), _group_ones(512))
        lane_hi = (lax.broadcasted_iota(jnp.int32, (1, 512), 1) & ATT_HD) != 0
        lo, hi = _split_heads(xhat * qg_ref[...] * _SCALE, lane_hi)
        qlo[...] = lo
        qhi[...] = hi
        for qb in range(nqb):
            rows = pl.ds(qb * QB, QB)
            wrows = pl.ds(_W0 + qb * QB, QB + HALO)
            mrows = pl.ds(0, N_META)
            mask_m, mask_w = _swa_masks_t(t, qb)
            for j in range(ATT_Q_HEADS):
                p, e = j // 2, j % 2
                ks = kwin if e == j // ATT_GROUP else krwin
                qp = (qlo, qhi)[e][rows, 128 * p:128 * (p + 1)]
                s_m[j] = _nt(ks[mrows, :], qp)
                s_w[j] = _nt(ks[wrows, :], qp)
            inv = []
            for j in range(ATT_Q_HEADS):
                sm = jnp.where(mask_m, s_m[j], NEG)
                sw = jnp.where(mask_w, s_w[j], NEG)
                sink = sk_ref[:, j:j + 1]
                m = jnp.maximum(jnp.maximum(jnp.max(sm, axis=0, keepdims=True),
                                            jnp.max(sw, axis=0, keepdims=True)), sink)
                em = jnp.exp(sm - m)
                ew = jnp.exp(sw - m)
                den = jnp.sum(em, axis=0, keepdims=True) + jnp.sum(ew, axis=0, keepdims=True) + jnp.exp(sink - m)
                p_m[j] = em.astype(BF)
                p_w[j] = ew.astype(BF)
                lse_ref[j:j + 1, pl.ds(qb * QB, QB)] = m + jnp.log(den)
                inv.append(1.0 / den)
            for j in range(ATT_Q_HEADS):
                vrows = pl.ds(ATT_HD * (j // ATT_GROUP), ATT_HD)
                ot[pl.ds(ATT_HD * j, ATT_HD), pl.ds(qb * QB, QB)] = (
                    _nn(vt[vrows, pl.ds(0, N_META)], p_m[j])
                    + _nn(vt[vrows, pl.ds(_W0 + qb * QB, QB + HALO)], p_w[j])) * inv[j]
        o = ot[...].T
        o_ref[...] = o
        yc_ref[...] = (o * _silu(g_ref[...].astype(F32))).astype(BF)

    win = pltpu.VMEM((_WROWS, 128), BF)
    in_specs = _swa_in_specs(nt, False)
    out_specs = [pl.BlockSpec((TR, 512), lambda i: (i, 0)), pl.BlockSpec((TR, 512), lambda i: (i, 0)),
                 pl.BlockSpec((ATT_Q_HEADS, TR), lambda i: (0, i))]
    out_shape = [jax.ShapeDtypeStruct((Lp, 512), BF), jax.ShapeDtypeStruct((Lp, 512), F32),
                 jax.ShapeDtypeStruct((ATT_Q_HEADS, Lp), F32)]
    scratch = [win, win, win, pltpu.VMEM((128, _WROWS), BF), pltpu.VMEM((TR, 512), BF),
               pltpu.VMEM((TR, 512), BF), pltpu.VMEM((512, TR), F32)] + _swa_park(F32) + _swa_park(BF)
    return _call_carrying(body, "swa_fwd", nt, in_specs, out_specs, out_shape, scratch,
                          (uc, uc, uc, uc, uc, uc, uc, uc, qg8, kg2, sinks), carry)


def _swa_bwd(uc, qg8, kg2, sinks, o_save, lse, dyc):
    Lp = uc.shape[0]
    nt = Lp // TR
    nqb = TR // QB

    def body(q_ref, g_ref, kc_ref, vc_ref, kh_ref, vh_ref, km_ref, vm_ref, qg_ref, kg_ref, sk_ref,
             o_ref, lse_ref, dy_ref, du_ref, dg_ref, dsk_ref,
             kwin, krwin, vwin, vrwin, kt, krt, qlo, qhi, dolo, dohi, dqt, dk_dir, dk_rol, dv_dir, dv_rol,
             carry_k, carry_v, meta_k, meta_v, s_m, s_w, dp_m, dp_w, p_m, p_w, ds_m, ds_w):
        i = pl.program_id(0)
        t = nt - 1 - i

        @pl.when(i == 0)
        def _():
            carry_k[...] = jnp.zeros_like(carry_k)
            carry_v[...] = jnp.zeros_like(carry_v)
            meta_k[...] = jnp.zeros_like(meta_k)
            meta_v[...] = jnp.zeros_like(meta_v)
            dg_ref[...] = jnp.zeros_like(dg_ref)
            dsk_ref[...] = jnp.zeros_like(dsk_ref)

        ones128 = _group_ones(128)
        ones512 = _group_ones(512)
        _swa_windows(kc_ref, vc_ref, kh_ref, vh_ref, km_ref, vm_ref, kg_ref[...], ones128, kwin, krwin, vwin, vrwin)
        kt[...] = kwin[...].T
        krt[...] = krwin[...].T
        xhat_q, r_q = _head_rms(q_ref[...].astype(F32), ones512)
        lane_hi = (lax.broadcasted_iota(jnp.int32, (1, 512), 1) & ATT_HD) != 0
        lo, hi = _split_heads(xhat_q * qg_ref[...] * _SCALE, lane_hi)
        qlo[...] = lo
        qhi[...] = hi
        gate = g_ref[...].astype(F32)
        dy = dy_ref[...].astype(F32)
        do = dy * _silu(gate)
        o = o_ref[...]
        du_ref[:, 512:1024] = (dy * o * _dsilu(gate)).astype(BF)
        lo, hi = _split_heads(do, lane_hi)
        dolo[...] = lo
        dohi[...] = hi
        hsel = jnp.where(jnp.right_shift(lax.broadcasted_iota(jnp.int32, (ATT_Q_HEADS, 512), 1), 6)
                         == lax.broadcasted_iota(jnp.int32, (ATT_Q_HEADS, 512), 0), 1.0, 0.0).astype(BF)
        prod = do * o
        p_hi = prod.astype(BF)
        d_t = _nt(hsel, p_hi) + _nt(hsel, (prod - p_hi.astype(F32)).astype(BF))
        for acc in (dk_dir, dk_rol, dv_dir, dv_rol):
            acc[...] = jnp.zeros_like(acc)

        for qb in range(nqb):
            rows = pl.ds(qb * QB, QB)
            qcols = pl.ds(qb * QB, QB)
            wrows = pl.ds(_W0 + qb * QB, QB + HALO)
            mrows = pl.ds(0, N_META)
            mask_m, mask_w = _swa_masks_t(t, qb)
            for j in range(ATT_Q_HEADS):
                p, e = j // 2, j % 2
                ks, vs = (kwin, vwin) if e == j // ATT_GROUP else (krwin, vrwin)
                pair = slice(128 * p, 128 * (p + 1))
                qp = (qlo, qhi)[e][rows, pair]
                dop = (dolo, dohi)[e][rows, pair]
                s_m[j] = _nt(ks[mrows, :], qp)
                s_w[j] = _nt(ks[wrows, :], qp)
                dp_m[j] = _nt(vs[mrows, :], dop)
                dp_w[j] = _nt(vs[wrows, :], dop)
            for j in range(ATT_Q_HEADS):
                lse_j = lse_ref[j:j + 1, qcols]
                d_j = d_t[j:j + 1, qb * QB:(qb + 1) * QB]
                em = jnp.exp(jnp.where(mask_m, s_m[j], NEG) - lse_j)
                ew = jnp.exp(jnp.where(mask_w, s_w[j], NEG) - lse_j)
                p_m[j] = em.astype(BF)
                p_w[j] = ew.astype(BF)
                ds_m[j] = (em * (dp_m[j] - d_j)).astype(BF)
                ds_w[j] = (ew * (dp_w[j] - d_j)).astype(BF)
                dsk_ref[j:j + 1, :] -= jnp.exp(sk_ref[:, j:j + 1] - lse_j) * d_j
            for j in range(ATT_Q_HEADS):
                e = j % 2
                ktr = kt if e == j // ATT_GROUP else krt
                hrows = pl.ds(ATT_HD * e, ATT_HD)
                dqt[pl.ds(ATT_HD * j, ATT_HD), qcols] = (_nn(ktr[hrows, pl.ds(0, N_META)], ds_m[j])
                                                         + _nn(ktr[hrows, pl.ds(_W0 + qb * QB, QB + HALO)], ds_w[j]))
            for direct, dk_acc, dv_acc in ((True, dk_dir, dv_dir), (False, dk_rol, dv_rol)):
                heads = [j for j in range(ATT_Q_HEADS) if (j % 2 == j // ATT_GROUP) == direct]
                q_cat = jnp.concatenate([(qlo, qhi)[j % 2][rows, 128 * (j // 2):128 * (j // 2 + 1)] for j in heads], axis=0)
                do_cat = jnp.concatenate([(dolo, dohi)[j % 2][rows, 128 * (j // 2):128 * (j // 2 + 1)] for j in heads], axis=0)
                dk_acc[mrows, :] += _nn(jnp.concatenate([ds_m[j] for j in heads], axis=1), q_cat)
                dk_acc[wrows, :] += _nn(jnp.concatenate([ds_w[j] for j in heads], axis=1), q_cat)
                dv_acc[mrows, :] += _nn(jnp.concatenate([p_m[j] for j in heads], axis=1), do_cat)
                dv_acc[wrows, :] += _nn(jnp.concatenate([p_w[j] for j in heads], axis=1), do_cat)

        dk_dir[...] += pltpu.roll(dk_rol[...], ATT_HD, 1)
        dv_dir[...] += pltpu.roll(dv_rol[...], ATT_HD, 1)
        meta_k[...] += dk_dir[pl.ds(0, N_META), :]
        meta_v[...] += dv_dir[pl.ds(0, N_META), :]
        first = jnp.where(t == 0, 1.0, 0.0)
        dk_dir[pl.ds(_C0 + TR - HALO, HALO), :] += carry_k[...]
        dv_dir[pl.ds(_C0 + TR - HALO, HALO), :] += carry_v[...]
        dk_dir[pl.ds(_C0 + META_PAD, N_META), :] += first * meta_k[...]
        dv_dir[pl.ds(_C0 + META_PAD, N_META), :] += first * meta_v[...]
        carry_k[...] = dk_dir[pl.ds(_W0, HALO), :]
        carry_v[...] = dv_dir[pl.ds(_W0, HALO), :]

        du_ref[:, 1152:1280] = dv_dir[pl.ds(_C0, TR), :].astype(BF)
        xhat_k, r_k = _head_rms(kc_ref[...].astype(F32), ones128)
        dkn = dk_dir[pl.ds(_C0, TR), :]
        dg_ref[1:2, 0:128] += jnp.sum(dkn * xhat_k, axis=0, keepdims=True)
        gd = dkn * kg_ref[...]
        du_ref[:, 1024:1152] = (r_k * (gd - xhat_k * _group_mean(gd * xhat_k, ones128))).astype(BF)
        dqn = dqt[...].T * _SCALE
        dg_ref[0:1, :] += jnp.sum(dqn * xhat_q, axis=0, keepdims=True)
        gd = dqn * qg_ref[...]
        du_ref[:, 0:512] = (r_q * (gd - xhat_q * _group_mean(gd * xhat_q, ones512))).astype(BF)

    rev = lambda i: (nt - 1 - i, 0)
    specs = _swa_in_specs(nt, True)
    win = pltpu.VMEM((_WROWS, 128), BF)
    wint = pltpu.VMEM((128, _WROWS), BF)
    tile_bf = pltpu.VMEM((TR, 512), BF)
    acc = pltpu.VMEM((_WROWS, 128), F32)
    return pl.pallas_call(
        body, name="swa_bwd", grid=(nt,),
        in_specs=specs + [pl.BlockSpec((TR, 512), rev), pl.BlockSpec((ATT_Q_HEADS, TR), lambda i: (0, nt - 1 - i)),
                          pl.BlockSpec((TR, 512), rev)],
        out_specs=[pl.BlockSpec((TR, N_C), rev), pl.BlockSpec((8, 512), lambda i: (0, 0)),
                   pl.BlockSpec((8, 128), lambda i: (0, 0))],
        out_shape=[jax.ShapeDtypeStruct((Lp, N_C), BF), jax.ShapeDtypeStruct((8, 512), F32),
                   jax.ShapeDtypeStruct((8, 128), F32)],
        scratch_shapes=[win, win, win, win, wint, wint, tile_bf, tile_bf, tile_bf, tile_bf,
                        pltpu.VMEM((512, TR), F32), acc, acc, acc, acc,
                        pltpu.VMEM((HALO, 128), F32), pltpu.VMEM((HALO, 128), F32),
                        pltpu.VMEM((N_META, 128), F32), pltpu.VMEM((N_META, 128), F32)]
        + _swa_park(F32) + _swa_park(F32) + _swa_park(BF) + _swa_park(BF),
        compiler_params=_cp(("arbitrary",)),
    )(uc, uc, uc, uc, uc, uc, uc, uc, qg8, kg2, sinks, o_save, lse, dyc)


def _mix_fwd(h, ya, yb, yc, ug, wa, wb, wc, wo):
    Lp = h.shape[0]
    wspec = lambda r: pl.BlockSpec((r, D_MODEL), lambda i: (0, 0))
    yspec = pl.BlockSpec((TRM, 512), lambda i: (i, 0))
    hspec = pl.BlockSpec((TRM, D_MODEL), lambda i: (i, 0))

    def body(h_ref, ya_ref, yb_ref, yc_ref, ug_ref, wa_ref, wb_ref, wc_ref, wo_ref, hn_ref, za_ref, zb_ref, zc_ref):
        mixed = jnp.zeros((TRM, D_MODEL), F32)
        for n, (y_ref, w_ref, z_ref) in enumerate(((ya_ref, wa_ref, za_ref), (yb_ref, wb_ref, zb_ref),
                                                   (yc_ref, wc_ref, zc_ref))):
            z = _nn(y_ref[...], w_ref[...])
            z_ref[...] = z.astype(BF)
            mixed = mixed + _sig(ug_ref[:, D_MODEL * n:D_MODEL * (n + 1)].astype(F32)) * z
        hn_ref[...] = h_ref[...] + _nn(mixed.astype(BF), wo_ref[...])

    return pl.pallas_call(
        body, name="mix_fwd", grid=(Lp // TRM,),
        in_specs=[hspec, yspec, yspec, yspec, pl.BlockSpec((TRM, N_G), lambda i: (i, 0)),
                  wspec(512), wspec(512), wspec(512), wspec(D_MODEL)],
        out_specs=[hspec, hspec, hspec, hspec],
        out_shape=[jax.ShapeDtypeStruct((Lp, D_MODEL), F32)] + [jax.ShapeDtypeStruct((Lp, D_MODEL), BF)] * 3,
        compiler_params=_cp(("parallel",)),
    )(h, ya, yb, yc, ug, wa, wb, wc, wo)


def _mix_bwd(dh, za, zb, zc, ug, wa, wb, wc, wo):
    Lp = dh.shape[0]
    wspec = lambda r: pl.BlockSpec((r, D_MODEL), lambda i: (0, 0))
    yspec = pl.BlockSpec((TRM, 512), lambda i: (i, 0))
    hspec = pl.BlockSpec((TRM, D_MODEL), lambda i: (i, 0))
    gspec = pl.BlockSpec((TRM, N_G), lambda i: (i, 0))

    def body(dh_ref, za_ref, zb_ref, zc_ref, ug_ref, wa_ref, wb_ref, wc_ref, wo_ref,
             dug_ref, mx_ref, dza_ref, dzb_ref, dzc_ref, dya_ref, dyb_ref, dyc_ref):
        dmix = _nt(dh_ref[...].astype(BF), wo_ref[...])
        mixed = jnp.zeros((TRM, D_MODEL), F32)
        for n, (z_ref, w_ref, dz_ref, dy_ref) in enumerate(((za_ref, wa_ref, dza_ref, dya_ref),
                                                            (zb_ref, wb_ref, dzb_ref, dyb_ref),
                                                            (zc_ref, wc_ref, dzc_ref, dyc_ref))):
            sl = slice(D_MODEL * n, D_MODEL * (n + 1))
            z = z_ref[...].astype(F32)
            gt = _sig(ug_ref[:, sl].astype(F32))
            mixed = mixed + gt * z
            dug_ref[:, sl] = (dmix * z * gt * (1.0 - gt)).astype(BF)
            dz = (dmix * gt).astype(BF)
            dz_ref[...] = dz
            dy_ref[...] = _nt(dz, w_ref[...]).astype(BF)
        mx_ref[...] = mixed.astype(BF)

    bf = lambda n: jax.ShapeDtypeStruct((Lp, n), BF)
    return pl.pallas_call(
        body, name="mix_bwd", grid=(Lp // TRM,),
        in_specs=[hspec, hspec, hspec, hspec, gspec, wspec(512), wspec(512), wspec(512), wspec(D_MODEL)],
        out_specs=[gspec, hspec, hspec, hspec, hspec, yspec, yspec, yspec],
        out_shape=[bf(N_G), bf(D_MODEL), bf(D_MODEL), bf(D_MODEL), bf(D_MODEL), bf(512), bf(512), bf(512)],
        compiler_params=_cp(("parallel",)),
    )(dh, za, zb, zc, ug, wa, wb, wc, wo)


def _inproj_bwd(dus, ws, h, dh, g, carry=None):
    Lp = h.shape[0]
    widths = [w.shape[1] for w in ws]

    def body(dg_ref, da_ref, db_ref, dc_ref, wg_ref, wa_ref, wb_ref, wc_ref, h_ref, dh_ref, g_ref, o_ref, gg_ref):
        @pl.when(pl.program_id(0) == 0)
        def _():
            gg_ref[...] = jnp.zeros_like(gg_ref)

        dhn = (_nt(dg_ref[...], wg_ref[...]) + _nt(da_ref[...], wa_ref[...])
               + _nt(db_ref[...], wb_ref[...]) + _nt(dc_ref[...], wc_ref[...]))
        x = h_ref[...]
        r = lax.rsqrt(jnp.mean(x * x, axis=-1, keepdims=True) + EPS)
        xhat = x * r
        gg_ref[0:1, :] += jnp.sum(dhn * xhat, axis=0, keepdims=True)
        gd = dhn * g_ref[...]
        o_ref[...] = dh_ref[...] + r * (gd - xhat * jnp.mean(gd * xhat, axis=-1, keepdims=True))

    hspec = pl.BlockSpec((TRM, D_MODEL), lambda i: (i, 0))
    in_specs = ([pl.BlockSpec((TRM, n), lambda i: (i, 0)) for n in widths]
                + [pl.BlockSpec((D_MODEL, n), lambda i: (0, 0), pipeline_mode=pl.Buffered(1)) for n in widths]
                + [hspec, hspec, pl.BlockSpec((1, D_MODEL), lambda i: (0, 0))])
    out_specs = [hspec, pl.BlockSpec((8, D_MODEL), lambda i: (0, 0))]
    out_shape = [jax.ShapeDtypeStruct((Lp, D_MODEL), F32), jax.ShapeDtypeStruct((8, D_MODEL), F32)]
    return _call_carrying(body, "inproj_bwd", Lp // TRM, in_specs, out_specs, out_shape, [],
                          (*dus, *ws, h, dh, g), carry)


def _loss_head(h, tgt_pad, seq):
    Lp = h.shape[0]
    nt = Lp // TR

    def body(h_ref, t_ref, dh_ref, l_ref):
        i = pl.program_id(0)

        @pl.when(i == 0)
        def _():
            l_ref[...] = jnp.zeros_like(l_ref)

        row = i * TR + lax.broadcasted_iota(jnp.int32, (TR, 1), 0)
        e = jnp.where((row >= CHUNK) & (row < CHUNK + seq), h_ref[...] - t_ref[...], 0.0)
        dh_ref[...] = e * (1.0 / D_MODEL)
        l_ref[...] += (0.5 / D_MODEL) * jnp.sum(jnp.sum(e * e, axis=0, keepdims=True), axis=1, keepdims=True)

    hspec = pl.BlockSpec((TR, D_MODEL), lambda i: (i, 0))
    return pl.pallas_call(
        body, name="loss_head", grid=(nt,), in_specs=[hspec, hspec],
        out_specs=[hspec, pl.BlockSpec((8, 128), lambda i: (0, 0))],
        out_shape=[jax.ShapeDtypeStruct((Lp, D_MODEL), F32), jax.ShapeDtypeStruct((8, 128), F32)],
        compiler_params=_cp(("arbitrary",)),
    )(h, tgt_pad)


def _lb_softmax(lb_ref):
    x = lb_ref[...]
    e = jnp.exp(x - jnp.max(x, axis=0, keepdims=True))
    return e / jnp.sum(e, axis=0, keepdims=True)


def _lb_fwd(hg_lb):
    def body(lb_ref, o_ref):
        sm = _lb_softmax(lb_ref)
        acc = jnp.zeros((1, 512), F32)
        for l in range(DEPTH):
            if l > 0:
                acc = acc + sm[l:l + 1, :]
            o_ref[l:l + 1, :] = jnp.clip(acc, 0.0, 1.0)

    return pl.pallas_call(body, name="lb_fwd", out_shape=jax.ShapeDtypeStruct((DEPTH, 512), F32))(hg_lb)


def _lb_bwd(hg_lb, dlb_all):
    def body(lb_ref, d_ref, o_ref):
        sm = _lb_softmax(lb_ref)
        acc = jnp.zeros((1, 512), F32)
        gm = []
        for l in range(DEPTH):
            if l > 0:
                acc = acc + sm[l:l + 1, :]
            gm.append(jnp.where((acc >= 0.0) & (acc <= 1.0), d_ref[l:l + 1, :], 0.0))
        dsm = [jnp.zeros((1, 512), F32)]
        for j in range(1, DEPTH):
            s = gm[j]
            for l in range(j + 1, DEPTH):
                s = s + gm[l]
            dsm.append(s)
        dot = dsm[0] * sm[0:1, :]
        for j in range(1, DEPTH):
            dot = dot + dsm[j] * sm[j:j + 1, :]
        for j in range(DEPTH):
            o_ref[j:j + 1, :] = sm[j:j + 1, :] * (dsm[j] - dot)

    return pl.pallas_call(body, name="lb_bwd", out_shape=jax.ShapeDtypeStruct((DEPTH, 512), F32))(hg_lb, dlb_all)


_ANY = pl.BlockSpec(memory_space=pl.ANY)


def _chip_peers():
    x, y, c = lax.axis_index("x"), lax.axis_index("y"), lax.axis_index("c")
    return (x, y, c), [(1 - x, y, c), (x, 1 - y, c), (1 - x, 1 - y, c)]


def _exchange(kind, ins, outs, send, recv, loc):
    (x, y, c), peers = _chip_peers()
    me = 2 * x + y
    ds = []
    for a in range(len(ins)):
        if kind == "gather":
            ds.append(pltpu.make_async_copy(ins[a], outs[a].at[me], loc.at[a]))
        else:
            ds.append(pltpu.make_async_copy(ins[a].at[me], outs[a].at[0], loc.at[a]))
        for p, (px, py, pc) in enumerate(peers):
            src, dst = (ins[a], outs[a].at[me]) if kind == "gather" else (ins[a].at[2 * px + py], outs[a].at[1 + p])
            ds.append(pltpu.make_async_remote_copy(src_ref=src, dst_ref=dst, send_sem=send.at[a, p],
                                                   recv_sem=recv.at[a, p], device_id=(px, py, pc), device_id_type=MESH))
    return ds


def _exchange_out_shapes(kind, arrs):
    if kind == "gather":
        return [jax.ShapeDtypeStruct((4,) + a.shape, a.dtype) for a in arrs]
    return [jax.ShapeDtypeStruct(a.shape, a.dtype) for a in arrs]


def _exchange_sems(n):
    return [pltpu.SemaphoreType.DMA((n, 3)), pltpu.SemaphoreType.DMA((n, 3)), pltpu.SemaphoreType.DMA((n,))]


def _exchange_chips(kind, arrs):
    n = len(arrs)

    def body(*refs):
        ds = _exchange(kind, refs[:n], refs[n:2 * n], *refs[2 * n:])
        for d in ds:
            d.start()
        for d in ds:
            d.wait()

    return pl.pallas_call(
        body, name=kind + "_chips", in_specs=[_ANY] * n, out_specs=[_ANY] * n,
        out_shape=_exchange_out_shapes(kind, arrs), scratch_shapes=_exchange_sems(n),
        compiler_params=pltpu.CompilerParams(has_side_effects=True),
    )(*arrs)


def _carry_exchange(body, n_in, n_out, n_steps, kind, n):
    def wrapped(*refs):
        ins, cin = refs[:n_in], refs[n_in:n_in + n]
        outs, cout = refs[n_in + n:n_in + n + n_out], refs[n_in + n + n_out:n_in + 2 * n + n_out]
        scr, sems = refs[n_in + 2 * n + n_out:-3], refs[-3:]
        i = pl.program_id(0)

        @pl.when(i == 0)
        def _():
            for d in _exchange(kind, cin, cout, *sems):
                d.start()

        body(*ins, *outs, *scr)

        @pl.when(i == n_steps - 1)
        def _():
            for d in _exchange(kind, cin, cout, *sems):
                d.wait()

    return wrapped


def _swap_cores(arrs):
    n = len(arrs)

    def body(*refs):
        ins, outs = refs[:n], refs[n:2 * n]
        send, recv = refs[2 * n:]
        x, y, c = lax.axis_index("x"), lax.axis_index("y"), lax.axis_index("c")
        rdmas = []
        for a in range(n):
            r = pltpu.make_async_remote_copy(src_ref=ins[a], dst_ref=outs[a], send_sem=send.at[a], recv_sem=recv.at[a],
                                             device_id=(x, y, 1 - c), device_id_type=MESH)
            r.start()
            rdmas.append(r)
        for r in rdmas:
            r.wait()

    return pl.pallas_call(
        body, name="swap_cores", in_specs=[_ANY] * n, out_specs=[_ANY] * n,
        out_shape=[jax.ShapeDtypeStruct(a.shape, a.dtype) for a in arrs],
        scratch_shapes=[pltpu.SemaphoreType.DMA((n,)), pltpu.SemaphoreType.DMA((n,))],
        compiler_params=pltpu.CompilerParams(has_side_effects=True),
    )(*arrs)


def _allsum_small(p):
    R = p.shape[0]

    def body(p_ref, o_ref, buf, send, recv):
        x, y, c = lax.axis_index("x"), lax.axis_index("y"), lax.axis_index("c")
        me = 4 * x + 2 * y + c
        buf[me] = p_ref[...]
        rdmas = []
        for k in range(1, 8):
            peer = (x ^ (k >> 2), y ^ ((k >> 1) & 1), c ^ (k & 1))
            r = pltpu.make_async_remote_copy(src_ref=p_ref, dst_ref=buf.at[me], send_sem=send.at[k - 1],
                                             recv_sem=recv.at[k - 1], device_id=peer, device_id_type=MESH)
            r.start()
            rdmas.append(r)
        for r in rdmas:
            r.wait()
        acc = buf[0]
        for d in range(1, 8):
            acc = acc + buf[d]
        o_ref[...] = acc

    return pl.pallas_call(
        body, name="allsum_small", out_shape=jax.ShapeDtypeStruct((R, 512), F32),
        in_specs=[pl.BlockSpec(memory_space=pltpu.VMEM)], out_specs=pl.BlockSpec(memory_space=pltpu.VMEM),
        scratch_shapes=[pltpu.VMEM((8, R, 512), F32), pltpu.SemaphoreType.DMA((7,)), pltpu.SemaphoreType.DMA((7,))],
        compiler_params=_cp(has_side_effects=True),
    )(p)


def _sum4(parts, name):
    _, R, C = parts.shape
    tr = 256 if R % 256 == 0 else R

    def body(p_ref, o_ref):
        o_ref[...] = ((p_ref[0] + p_ref[1]) + p_ref[2]) + p_ref[3]

    return pl.pallas_call(
        body, name=name, grid=(R // tr,), in_specs=[pl.BlockSpec((4, tr, C), lambda i: (0, i, 0))],
        out_specs=pl.BlockSpec((tr, C), lambda i: (i, 0)), out_shape=jax.ShapeDtypeStruct((R, C), F32),
        compiler_params=_cp(("parallel",)),
    )(parts)


def _adamw(w, m, v, g0, g1, name):
    R, C = w.shape
    tr = 256 if R % 256 == 0 else R
    two = g1 is not None
    c1 = 1.0 / (1.0 - ADAM_B1 ** ADAM_STEP)
    c2 = 1.0 / (1.0 - ADAM_B2 ** ADAM_STEP)

    def body(*refs):
        if two:
            w_ref, m_ref, v_ref, a_ref, b_ref, g_ref, d_ref, nm_ref, nv_ref = refs
            g = a_ref[...] + b_ref[...]
        else:
            w_ref, m_ref, v_ref, a_ref, g_ref, d_ref, nm_ref, nv_ref = refs
            g = a_ref[...]
        g_ref[...] = g
        m = ADAM_B1 * m_ref[...] + (1.0 - ADAM_B1) * g
        v = ADAM_B2 * v_ref[...] + (1.0 - ADAM_B2) * (g * g)
        nm_ref[...] = m
        nv_ref[...] = v
        d_ref[...] = -ADAM_LR * ((m * c1) / (jnp.sqrt(v * c2) + ADAM_EPS) + ADAM_WD * w_ref[...])

    spec = pl.BlockSpec((tr, C), lambda i: (i, 0))
    n_in = 5 if two else 4
    ins = (w, m, v, g0, g1) if two else (w, m, v, g0)
    return pl.pallas_call(
        body, name=name, grid=(R // tr,), in_specs=[spec] * n_in, out_specs=[spec] * 4,
        out_shape=[jax.ShapeDtypeStruct((R, C), F32)] * 4, compiler_params=_cp(("parallel",)),
    )(*ins)


def _pad8(a):
    r = (-a.shape[0]) % 8
    return a if r == 0 else jnp.pad(a, ((0, r), (0, 0)))


def _local_step(x, tgt, meta, P, shards=None, prep=None, pack=None):
    seq = x.shape[0]
    Lp = -(-(seq + CHUNK) // TR) * TR
    tail = Lp - seq - CHUNK
    h = jnp.concatenate([jnp.zeros((META_PAD, D_MODEL), F32), meta, x, jnp.zeros((tail, D_MODEL), F32)], axis=0)
    tgt_pad = jnp.pad(tgt, ((CHUNK, tail), (0, 0)))

    P = list(P)
    saved = []
    for l in range(DEPTH):
        p = P[l]
        hn = _rms_fwd(h, p["norm_g"])
        mm = functools.partial(_matmul, out_dtype=BF, tm=TR, tk=D_MODEL, col_major_grid=True)
        ug = mm(hn, p["w_g"], tn=N_G // 2, name="inproj_g")
        ua = mm(hn, p["w_a"], tn=N_A, name="inproj_a")
        ub = mm(hn, p["w_b"], tn=N_B, name="inproj_b")
        uc = mm(hn, p["w_c"], tn=N_C, name="inproj_c")
        nxt = shards[l + 1] if shards is not None and l + 1 < DEPTH else None
        carry = (lambda part: ("gather", part)) if nxt is not None else (lambda part: None)
        res_a = _conv_fwd(ua, p["conv_w"], p["conv_vec"], carry(nxt and nxt[1:2]))
        res_b = _hg_fwd(ub, p["lb"], p["gn4"], carry(nxt and nxt[0:1]))
        res_c = _swa_fwd(uc, p["qg"], p["kg"], p["sinks"], carry(nxt and nxt[2:]))
        (ya, yconv), (yb, o_hg, s_hg), (yc, o_at, lse) = res_a[:2], res_b[:3], res_c[:3]
        if nxt is not None:
            P.append(prep(l + 1, [*res_b[3:], *res_a[2:], *res_c[3:]]))
        h_new, za, zb, zc = _mix_fwd(h, ya, yb, yc, ug, p["w_ao"], p["w_bo"], p["w_co"], p["w_out"])
        saved.append(dict(h=h, hn=hn, ug=ug, ua=ua, ub=ub, uc=uc, ya=ya, yconv=yconv, yb=yb, o_hg=o_hg, s_hg=s_hg,
                          yc=yc, o_at=o_at, lse=lse, za=za, zb=zb, zc=zc))
        h = h_new

    dh, loss8 = _loss_head(h, tgt_pad, seq)

    grads = [None] * DEPTH
    parts = [[None, None] for _ in range(DEPTH)]
    pending = None
    tk_dw = 2 * TR if Lp % (2 * TR) == 0 else TR
    for l in reversed(range(DEPTH)):
        p, s = P[l], saved[l]
        dug, mixed, dza, dzb, dzc, dya, dyb, dyc = _mix_bwd(dh, s["za"], s["zb"], s["zc"], s["ug"],
                                                             p["w_ao"], p["w_bo"], p["w_co"], p["w_out"])
        tnmm = functools.partial(_matmul, ta=True, out_dtype=F32, tk=tk_dw)
        g = {}
        g["w_out"] = tnmm(mixed, dh, tm=D_MODEL, tn=D_MODEL, name="dw_out")
        g["w_ao"] = tnmm(s["ya"], dza, tm=512, tn=D_MODEL, name="dw_ao")
        g["w_bo"] = tnmm(s["yb"], dzb, tm=512, tn=D_MODEL, name="dw_bo")
        g["w_co"] = tnmm(s["yc"], dzc, tm=512, tn=D_MODEL, name="dw_co")
        dua, g["conv_w"], g["conv_vec"] = _conv_bwd(s["ua"], s["yconv"], dya, p["conv_w"], p["conv_vec"])
        carry = ("scatter", pending[1]) if pending is not None else None
        res = _hg_bwd(s["ub"], p["lb"], p["gn4"], s["o_hg"], s["s_hg"], dyb, carry)
        dub, g["hg_small"] = res[:2]
        if carry is not None:
            parts[pending[0]][1] = res[2:]
        duc, g["at_gain"], g["at_sink"] = _swa_bwd(s["uc"], p["qg"], p["kg"], p["sinks"], s["o_at"], s["lse"], dyc)
        g["w_g"] = tnmm(s["hn"], dug, tm=D_MODEL, tn=N_G // 2, name="dw_in_g")
        g["w_a"] = tnmm(s["hn"], dua, tm=D_MODEL, tn=N_A, name="dw_in_a")
        g["w_b"] = tnmm(s["hn"], dub, tm=D_MODEL, tn=N_B, name="dw_in_b")
        g["w_c"] = tnmm(s["hn"], duc, tm=D_MODEL, tn=N_C, name="dw_in_c")
        first, second = pack(g) if pack is not None else (None, None)
        if first is not None and l == 0:
            first, second = first + second, []
        res = _inproj_bwd([dug, dua, dub, duc], [p["w_g"], p["w_a"], p["w_b"], p["w_c"]], s["h"], dh, p["norm_g"],
                          ("scatter", first) if first is not None else None)
        dh, g["norm_g"] = res[:2]
        grads[l] = g
        if pack is not None:
            parts[l] = [res[2:3], res[3:]] if l == 0 else [res[2:], None]
            pending = (l, second) if l > 0 else None
    return loss8, dh, grads, parts


def _split_w_in(w):
    return dict(w_a=w[:, 0:1536], w_b=w[:, 1536:3584],
                w_c=jnp.concatenate([w[:, 3584:4096], w[:, 4352:4864], w[:, 4096:4352]], axis=1),
                w_g=w[:, 4864:7936])


def _join_w_in(g):
    c = g["w_c"]
    return jnp.concatenate([g["w_a"], g["w_b"], c[:, 0:512], c[:, 1024:1280], c[:, 512:1024], g["w_g"]], axis=1)


def _attn_small(g):
    return (g["at_gain"][0].reshape(ATT_Q_HEADS, ATT_HD).sum(0),
            g["at_gain"][1, 0:128].reshape(ATT_KV_HEADS, ATT_HD).sum(0), g["at_sink"].sum(1))


_SMALL = (("norm_g", 8), ("meta", 32), ("conv_w", 32 * DEPTH), ("conv_b", 8), ("conv_ln_g", 8), ("conv_ln_b", 8),
          ("lb", 8), ("hg_norm_g", 8), ("q_norm_g", 8), ("k_norm_g", 8), ("sinks", 8))


def _small_offsets():
    off, o = {}, 0
    for name, rows in _SMALL:
        off[name] = (o, rows)
        o += rows
    return off, o


def _pack_small(d):
    parts = []
    for name, rows in _SMALL:
        a = d[name]
        parts.append(jnp.pad(a, ((0, rows - a.shape[0]), (0, 512 - a.shape[1]))))
    return jnp.concatenate(parts, axis=0)


def kernel(x, meta_tokens, norm_g, w_in, conv_w, conv_b, conv_ln_g, conv_ln_b, w_conv_out, hg_lower_bounds, hg_norm_g, w_hg_out, q_norm_g, k_norm_g, attn_sinks, w_att_out, w_out, loss_target, m_meta_tokens, m_norm_g, m_w_in, m_conv_w, m_conv_b, m_conv_ln_g, m_conv_ln_b, m_w_conv_out, m_hg_lower_bounds, m_hg_norm_g, m_w_hg_out, m_q_norm_g, m_k_norm_g, m_attn_sinks, m_w_att_out, m_w_out, v_meta_tokens, v_norm_g, v_w_in, v_conv_w, v_conv_b, v_conv_ln_g, v_conv_ln_b, v_w_conv_out, v_hg_lower_bounds, v_hg_norm_g, v_w_hg_out, v_q_norm_g, v_k_norm_g, v_attn_sinks, v_w_att_out, v_w_out):
    xi, yi = lax.axis_index("x"), lax.axis_index("y")
    chip = 2 * xi + yi
    NS = w_in.shape[2]
    CS = conv_w.shape[2]
    MS = meta_tokens.shape[1]

    half = D_MODEL // 2
    shards = [[w_in[l, :half].astype(BF), w_in[l, half:].astype(BF), w_conv_out[l].astype(BF), w_hg_out[l].astype(BF),
               w_att_out[l].astype(BF), w_out[l].astype(BF)] for l in range(DEPTH)]
    *first, g_meta, g_convw = _exchange_chips(
        "gather", shards[0] + [meta_tokens, conv_w.reshape(DEPTH * CONV_WIDTH, CS)])
    cols = lambda g: g.transpose(1, 0, 2).reshape(g.shape[1], -1)
    meta_f = cols(g_meta)
    convw_f = cols(g_convw).reshape(DEPTH, CONV_WIDTH, D_CONV)
    lb_all = _lb_fwd(hg_lower_bounds)

    def prep(l, gathered):
        g_win_top, g_win_bot, g_wao, g_wbo, g_wco, g_wout = gathered
        p = _split_w_in(jnp.concatenate([cols(g_win_top), cols(g_win_bot)], axis=0))
        p.update(w_ao=cols(g_wao), w_bo=cols(g_wbo), w_co=cols(g_wco), w_out=g_wout.reshape(D_MODEL, D_MODEL),
                 norm_g=norm_g[l:l + 1], conv_w=convw_f[l],
                 conv_vec=_pad8(jnp.stack([conv_b[l], conv_ln_g[l], conv_ln_b[l]])),
                 lb=lb_all[l:l + 1], gn4=jnp.tile(hg_norm_g[l:l + 1], (1, HG_HEADS)),
                 qg=jnp.tile(q_norm_g[l:l + 1], (1, ATT_Q_HEADS)), kg=jnp.tile(k_norm_g[l:l + 1], (1, ATT_KV_HEADS)),
                 sinks=attn_sinks[l:l + 1])
        return p

    shard_cols = lambda a: a.reshape(a.shape[0], 4, -1).transpose(1, 0, 2)
    def pack(g):
        win = shard_cols(_join_w_in(g))
        return [win[:, :half]], [win[:, half:], shard_cols(g["w_ao"]), shard_cols(g["w_bo"]), shard_cols(g["w_co"]),
                                 g["w_out"].reshape(4, MS, D_MODEL)]

    loss8, dh0, grads, parts = _local_step(x[0], loss_target[0], meta_f, [prep(0, first)], shards, prep, pack)
    seq = x.shape[1]
    grad_x = dh0[CHUNK:CHUNK + seq][None]
    loss = lax.psum(loss8[0, 0], ("x", "y", "c"))

    sum4 = functools.partial(_sum4, name="sum_chips")
    mine = [jnp.concatenate([t for l in range(DEPTH) for t in (sum4(parts[l][0][0]), sum4(parts[l][1][0]))], axis=0)]
    mine += [jnp.concatenate([sum4(parts[l][1][a]) for l in range(DEPTH)], axis=0) for a in range(1, 5)]
    theirs = _swap_cores(mine)

    dlb_all = jnp.concatenate([grads[l]["hg_small"][0:1] for l in range(DEPTH)], axis=0)
    small = dict(
        norm_g=jnp.concatenate([grads[l]["norm_g"][0:1] for l in range(DEPTH)], axis=0).reshape(8, 512),
        meta=dh0[META_PAD:CHUNK].reshape(32, 512),
        conv_w=jnp.concatenate([grads[l]["conv_w"] for l in range(DEPTH)], axis=0),
        conv_b=jnp.concatenate([grads[l]["conv_vec"][0:1] for l in range(DEPTH)], axis=0),
        conv_ln_g=jnp.concatenate([grads[l]["conv_vec"][1:2] for l in range(DEPTH)], axis=0),
        conv_ln_b=jnp.concatenate([grads[l]["conv_vec"][2:3] for l in range(DEPTH)], axis=0),
        lb=_lb_bwd(hg_lower_bounds, dlb_all),
        hg_norm_g=jnp.concatenate([grads[l]["hg_small"][1:2].reshape(HG_HEADS, HG_D).sum(0, keepdims=True)
                                   for l in range(DEPTH)], axis=0),
        q_norm_g=jnp.stack([_attn_small(grads[l])[0] for l in range(DEPTH)]),
        k_norm_g=jnp.stack([_attn_small(grads[l])[1] for l in range(DEPTH)]),
        sinks=jnp.stack([_attn_small(grads[l])[2] for l in range(DEPTH)]),
    )
    gsum = _allsum_small(_pack_small(small))
    off, _ = _small_offsets()

    def take(name, rows, cols):
        o, _ = off[name]
        return gsum[o:o + rows, 0:cols]

    g_meta_full = take("meta", 32, 512).reshape(N_META, D_MODEL)
    g_convw_full = take("conv_w", 32 * DEPTH, 512).reshape(DEPTH, 32, 512)[:, :CONV_WIDTH]
    small_grads = dict(
        norm_g=take("norm_g", 8, 512),
        meta=lax.dynamic_slice_in_dim(g_meta_full, chip * MS, MS, axis=1),
        conv_w=lax.dynamic_slice_in_dim(g_convw_full, chip * CS, CS, axis=2).reshape(DEPTH * CONV_WIDTH, CS),
        conv_b=take("conv_b", DEPTH, 512), conv_ln_g=take("conv_ln_g", DEPTH, 512), conv_ln_b=take("conv_ln_b", DEPTH, 512),
        lb=take("lb", DEPTH, 512), hg_norm_g=take("hg_norm_g", DEPTH, HG_D), q_norm_g=take("q_norm_g", DEPTH, ATT_HD),
        k_norm_g=take("k_norm_g", DEPTH, ATT_HD), sinks=take("sinks", DEPTH, ATT_Q_HEADS))

    def big_update(w, m, v, a, b, name):
        shp = w.shape
        r2 = lambda t: t.reshape(-1, shp[-1])
        outs = _adamw(r2(w), r2(m), r2(v), a, b, name)
        return [o.reshape(shp) for o in outs]

    res = {}
    res["w_in"] = big_update(w_in, m_w_in, v_w_in, mine[0], theirs[0], "adamw_w_in")
    res["w_conv_out"] = big_update(w_conv_out, m_w_conv_out, v_w_conv_out, mine[1], theirs[1], "adamw_w_ao")
    res["w_hg_out"] = big_update(w_hg_out, m_w_hg_out, v_w_hg_out, mine[2], theirs[2], "adamw_w_bo")
    res["w_att_out"] = big_update(w_att_out, m_w_att_out, v_w_att_out, mine[3], theirs[3], "adamw_w_co")
    res["w_out"] = big_update(w_out, m_w_out, v_w_out, mine[4], theirs[4], "adamw_w_out")

    small_w = dict(meta=(meta_tokens, m_meta_tokens, v_meta_tokens), norm_g=(norm_g, m_norm_g, v_norm_g),
                   conv_w=(conv_w, m_conv_w, v_conv_w), conv_b=(conv_b, m_conv_b, v_conv_b),
                   conv_ln_g=(conv_ln_g, m_conv_ln_g, v_conv_ln_g), conv_ln_b=(conv_ln_b, m_conv_ln_b, v_conv_ln_b),
                   lb=(hg_lower_bounds, m_hg_lower_bounds, v_hg_lower_bounds),
                   hg_norm_g=(hg_norm_g, m_hg_norm_g, v_hg_norm_g), q_norm_g=(q_norm_g, m_q_norm_g, v_q_norm_g),
                   k_norm_g=(k_norm_g, m_k_norm_g, v_k_norm_g), sinks=(attn_sinks, m_attn_sinks, v_attn_sinks))
    view = lambda n, t: t.reshape(-1, 512) if n == "norm_g" else t.reshape(-1, t.shape[-1])
    pw, pm, pv = (_pack_rows([view(n, small_w[n][k]) for n in small_w]) for k in range(3))
    pg = _pack_rows([small_grads[n] for n in small_w])
    packed = _adamw(pw, pm, pv, pg, None, "adamw_small")
    o = 0
    for n in small_w:
        r, cdim = view(n, small_w[n][0]).shape
        res[n] = [t[o:o + r, 0:cdim].reshape(small_w[n][0].shape) for t in packed]
        o += -(-r // 8) * 8

    order = [("meta", None), ("norm_g", None), ("w_in", None), ("conv_w", None), ("conv_b", None), ("conv_ln_g", None),
             ("conv_ln_b", None), ("w_conv_out", None), ("lb", None), ("hg_norm_g", None), ("w_hg_out", None),
             ("q_norm_g", None), ("k_norm_g", None), ("sinks", None), ("w_att_out", None), ("w_out", None)]
    outs = [loss, grad_x]
    for k in range(4):
        outs += [res[n][k] for n, _ in order]
    return tuple(outs)


def _pack_rows(arrs):
    parts = []
    for a in arrs:
        r = (-a.shape[0]) % 8
        parts.append(jnp.pad(a, ((0, r), (0, 512 - a.shape[1]))))
    return jnp.concatenate(parts, axis=0)
```

```python
import functools

import jax
import jax.numpy as jnp
from jax import lax
from jax.experimental import pallas as pl
from jax.experimental.pallas import tpu as pltpu

F32 = jnp.float32
BF = jnp.bfloat16

D_MODEL = 1024
DEPTH = 4
CHUNK = 64
N_META = 16
META_PAD = CHUNK - N_META
D_CONV = 512
CONV_WIDTH = 31
HG_HEADS = 4
HG_D = 128
ATT_Q_HEADS = 8
ATT_KV_HEADS = 2
ATT_HD = 64
ATT_GROUP = ATT_Q_HEADS // ATT_KV_HEADS
EPS = 1e-6
F_FLOOR = 1e-30
NEG = -1e30

ADAM_LR = 0.001
ADAM_B1 = 0.9
ADAM_B2 = 0.999
ADAM_EPS = 1e-08
ADAM_WD = 0.01
ADAM_STEP = 10

TR = 640
TRM = TR // 2
CONV_RB = 32
QB = 128
HALO = 128
VMEM_LIMIT = 56 * 1024 * 1024

N_G, N_A, N_B, N_C = 3 * D_MODEL, 3 * D_CONV, 4 * 512, 2 * 512 + 2 * 128

MESH = pl.DeviceIdType.MESH


def _cp(sem=None, vmem=VMEM_LIMIT, **kw):
    if sem is None:
        return pltpu.CompilerParams(vmem_limit_bytes=vmem, **kw)
    return pltpu.CompilerParams(dimension_semantics=sem, vmem_limit_bytes=vmem, **kw)


def _nn(a, b):
    return lax.dot_general(a, b, (((1,), (0,)), ((), ())), preferred_element_type=F32)


def _nt(a, b):
    return lax.dot_general(a, b, (((1,), (1,)), ((), ())), preferred_element_type=F32)


def _tn(a, b):
    return lax.dot_general(a, b, (((0,), (0,)), ((), ())), preferred_element_type=F32)


def _sig(x):
    return jax.nn.sigmoid(x)


def _silu(x):
    return x * _sig(x)


def _dsilu(x):
    s = _sig(x)
    return s * (1.0 + x * (1.0 - s))


def _mm_split(t, x):
    hi = x.astype(BF)
    lo = (x - hi.astype(F32)).astype(BF)
    return _nn(t, hi) + _nn(t, lo)


def _chunk_tri(n, upper):
    r = lax.broadcasted_iota(jnp.int32, (n, n), 0)
    c = lax.broadcasted_iota(jnp.int32, (n, n), 1)
    same = jnp.right_shift(r, 6) == jnp.right_shift(c, 6)
    tri = (c >= r) if upper else (c <= r)
    return jnp.where(same & tri, 1.0, 0.0).astype(BF)


def _matmul(a, b, *, ta=False, tb=False, out_dtype, tm, tn, tk, name, col_major_grid=False):
    if ta:
        K, M = a.shape
    else:
        M, K = a.shape
    N = b.shape[0] if tb else b.shape[1]
    assert M % tm == 0 and N % tn == 0 and K % tk == 0, (name, M, N, K, tm, tn, tk)
    nk = K // tk
    if col_major_grid:
        grid = (N // tn, M // tm, nk)
        ij = lambda g0, g1: (g1, g0)
    else:
        grid = (M // tm, N // tn, nk)
        ij = lambda g0, g1: (g0, g1)
    if ta:
        a_spec = pl.BlockSpec((tk, tm), lambda g0, g1, k: (k, ij(g0, g1)[0]))
    else:
        a_spec = pl.BlockSpec((tm, tk), lambda g0, g1, k: (ij(g0, g1)[0], k))
    if tb:
        b_spec = pl.BlockSpec((tn, tk), lambda g0, g1, k: (ij(g0, g1)[1], k))
    else:
        b_spec = pl.BlockSpec((tk, tn), lambda g0, g1, k: (k, ij(g0, g1)[1]))
    o_spec = pl.BlockSpec((tm, tn), lambda g0, g1, k: ij(g0, g1))
    dims = (((0 if ta else 1,), (1 if tb else 0,)), ((), ()))
    use_acc = nk > 1 and out_dtype != F32

    def body(a_ref, b_ref, o_ref, *scr):
        k = pl.program_id(2)
        p = lax.dot_general(a_ref[...].astype(BF), b_ref[...].astype(BF), dims, preferred_element_type=F32)
        if nk == 1:
            o_ref[...] = p.astype(out_dtype)
        else:
            acc = scr[0] if use_acc else o_ref

            @pl.when(k == 0)
            def _():
                acc[...] = p

            @pl.when(k > 0)
            def _():
                acc[...] += p

            if use_acc:
                @pl.when(k == nk - 1)
                def _():
                    o_ref[...] = acc[...].astype(out_dtype)

    return pl.pallas_call(
        body, name=name, grid=grid, in_specs=[a_spec, b_spec], out_specs=o_spec,
        out_shape=jax.ShapeDtypeStruct((M, N), out_dtype),
        scratch_shapes=[pltpu.VMEM((tm, tn), F32)] if use_acc else [],
        compiler_params=_cp(("parallel", "parallel", "arbitrary")),
    )(a, b)


def _rms_fwd(h, g):
    Lp = h.shape[0]

    def body(h_ref, g_ref, o_ref):
        x = h_ref[...]
        r = lax.rsqrt(jnp.mean(x * x, axis=-1, keepdims=True) + EPS)
        o_ref[...] = (x * r * g_ref[...]).astype(BF)

    return pl.pallas_call(
        body, name="rms_fwd", grid=(Lp // TR,),
        in_specs=[pl.BlockSpec((TR, D_MODEL), lambda i: (i, 0)), pl.BlockSpec((1, D_MODEL), lambda i: (0, 0))],
        out_specs=pl.BlockSpec((TR, D_MODEL), lambda i: (i, 0)),
        out_shape=jax.ShapeDtypeStruct((Lp, D_MODEL), BF),
        compiler_params=_cp(("parallel",)),
    )(h, g)


def _glu(ua, row):
    a = ua[:, 0:D_CONV].astype(F32)
    gl = ua[:, D_CONV:2 * D_CONV].astype(F32)
    return jnp.where(row >= META_PAD, a * _sig(gl), 0.0)


_SH_ROWS = TR + CHUNK - 8


def _fill_shifts(src, sh):
    for b in range(1, 8):
        sh[b - 1] = src[pl.ds(b, _SH_ROWS), :]


def _shifted(src, sh, start, n):
    b = start % 8
    if b == 0:
        return src[pl.ds(start, n), :]
    return sh[b - 1, pl.ds(start - b, n), :]


def _conv_fwd(ua, cw, cvec, carry=None):
    Lp = ua.shape[0]
    nt = Lp // TR
    hb = TR // CHUNK

    def body(cur_ref, halo_ref, w_ref, v_ref, ya_ref, yc_ref, ext, sh):
        i = pl.program_id(0)
        row = i * TR + lax.broadcasted_iota(jnp.int32, (TR, 1), 0)
        hrow = i * TR - CHUNK + lax.broadcasted_iota(jnp.int32, (CHUNK, 1), 0)
        ext[pl.ds(0, CHUNK), :] = jnp.where(i > 0, _glu(halo_ref[...], hrow), 0.0)
        ext[pl.ds(CHUNK, TR), :] = _glu(cur_ref[...], row)
        _fill_shifts(ext, sh)
        for rb in range(TR // CONV_RB):
            r0 = rb * CONV_RB
            rows = pl.ds(r0, CONV_RB)
            acc = jnp.zeros((CONV_RB, D_CONV), F32)
            for j in range(CONV_WIDTH):
                acc = acc + _shifted(ext, sh, r0 + CHUNK - (CONV_WIDTH - 1) + j, CONV_RB) * w_ref[j:j + 1, :]
            y = acc + v_ref[0:1, :]
            yc_ref[rows, :] = y
            mu = jnp.mean(y, axis=-1, keepdims=True)
            d = y - mu
            var = jnp.mean(d * d, axis=-1, keepdims=True)
            yn = d * lax.rsqrt(var + EPS) * v_ref[1:2, :] + v_ref[2:3, :]
            ya_ref[rows, :] = (_silu(yn) * _silu(cur_ref[rows, 2 * D_CONV:3 * D_CONV].astype(F32))).astype(BF)

    in_specs = [pl.BlockSpec((TR, N_A), lambda i: (i, 0)),
                pl.BlockSpec((CHUNK, N_A), lambda i: (jnp.maximum(i * hb - 1, 0), 0)),
                pl.BlockSpec((CONV_WIDTH, D_CONV), lambda i: (0, 0)),
                pl.BlockSpec((8, D_CONV), lambda i: (0, 0))]
    out_specs = [pl.BlockSpec((TR, D_CONV), lambda i: (i, 0)), pl.BlockSpec((TR, D_CONV), lambda i: (i, 0))]
    out_shape = [jax.ShapeDtypeStruct((Lp, D_CONV), BF), jax.ShapeDtypeStruct((Lp, D_CONV), F32)]
    scratch = [pltpu.VMEM((TR + CHUNK, D_CONV), F32), pltpu.VMEM((7, _SH_ROWS, D_CONV), F32)]
    return _call_carrying(body, "conv_fwd", nt, in_specs, out_specs, out_shape, scratch, (ua, ua, cw, cvec), carry)


def _conv_bwd(ua, yconv, dya, cw, cvec):
    Lp = ua.shape[0]
    nt = Lp // TR
    hb = TR // CHUNK
    nhb = Lp // CHUNK

    def ln_bwd(y, dout, gate, v_ref):
        mu = jnp.mean(y, axis=-1, keepdims=True)
        d = y - mu
        var = jnp.mean(d * d, axis=-1, keepdims=True)
        rstd = lax.rsqrt(var + EPS)
        xhat = d * rstd
        yn = xhat * v_ref[1:2, :] + v_ref[2:3, :]
        dyn = dout * _silu(gate) * _dsilu(yn)
        dxh = dyn * v_ref[1:2, :]
        dyc = rstd * (dxh - jnp.mean(dxh, axis=-1, keepdims=True) - xhat * jnp.mean(dxh * xhat, axis=-1, keepdims=True))
        return dyc, dyn, xhat, yn

    def body(cur_ref, prev_ref, next_ref, yc_ref, ycn_ref, dy_ref, dyn_ref, w_ref, v_ref,
             du_ref, dw_ref, dv_ref, uext, dext, dwacc, ush, dsh):
        i = pl.program_id(0)

        @pl.when(i == 0)
        def _():
            dwacc[...] = jnp.zeros_like(dwacc)
            dv_ref[...] = jnp.zeros_like(dv_ref)

        row = i * TR + lax.broadcasted_iota(jnp.int32, (TR, 1), 0)
        hrow = i * TR - CHUNK + lax.broadcasted_iota(jnp.int32, (CHUNK, 1), 0)
        uext[pl.ds(0, CHUNK), :] = jnp.where(i > 0, _glu(prev_ref[...], hrow), 0.0)
        uext[pl.ds(CHUNK, TR), :] = _glu(cur_ref[...], row)

        s_b = jnp.zeros((1, D_CONV), F32)
        s_g = jnp.zeros((1, D_CONV), F32)
        s_bb = jnp.zeros((1, D_CONV), F32)
        for rb in range(TR // CONV_RB):
            rows = pl.ds(rb * CONV_RB, CONV_RB)
            gate = cur_ref[rows, 2 * D_CONV:3 * D_CONV].astype(F32)
            dout = dy_ref[rows, :].astype(F32)
            dyc, dyn, xhat, yn = ln_bwd(yc_ref[rows, :], dout, gate, v_ref)
            du_ref[rows, 2 * D_CONV:3 * D_CONV] = (dout * _silu(yn) * _dsilu(gate)).astype(BF)
            dext[rows, :] = dyc
            s_b = s_b + jnp.sum(dyc, axis=0, keepdims=True)
            s_g = s_g + jnp.sum(dyn * xhat, axis=0, keepdims=True)
            s_bb = s_bb + jnp.sum(dyn, axis=0, keepdims=True)
        dv_ref[0:1, :] += s_b
        dv_ref[1:2, :] += s_g
        dv_ref[2:3, :] += s_bb
        dyc_n, _, _, _ = ln_bwd(ycn_ref[...], dyn_ref[...].astype(F32),
                                next_ref[:, 2 * D_CONV:3 * D_CONV].astype(F32), v_ref)
        dext[pl.ds(TR, CHUNK), :] = jnp.where(i < nt - 1, dyc_n, 0.0)
        _fill_shifts(uext, ush)
        _fill_shifts(dext, dsh)

        for rb in range(TR // CONV_RB):
            r0 = rb * CONV_RB
            rows = pl.ds(r0, CONV_RB)
            d_blk = dext[rows, :]
            dglu = jnp.zeros((CONV_RB, D_CONV), F32)
            for j in range(CONV_WIDTH):
                dglu = dglu + _shifted(dext, dsh, r0 + CONV_WIDTH - 1 - j, CONV_RB) * w_ref[j:j + 1, :]
                prod = d_blk * _shifted(uext, ush, r0 + CHUNK - (CONV_WIDTH - 1) + j, CONV_RB)
                part = prod[0:8, :]
                for s in range(1, CONV_RB // 8):
                    part = part + prod[8 * s:8 * s + 8, :]
                dwacc[j] += part
            a = cur_ref[rows, 0:D_CONV].astype(F32)
            sg = _sig(cur_ref[rows, D_CONV:2 * D_CONV].astype(F32))
            grow = i * TR + r0 + lax.broadcasted_iota(jnp.int32, (CONV_RB, 1), 0)
            dglu = jnp.where(grow >= META_PAD, dglu, 0.0)
            du_ref[rows, 0:D_CONV] = (dglu * sg).astype(BF)
            du_ref[rows, D_CONV:2 * D_CONV] = (dglu * a * sg * (1.0 - sg)).astype(BF)

        @pl.when(i == nt - 1)
        def _():
            dw_ref[...] = jnp.sum(dwacc[...], axis=1)

    nxt = lambda i: (jnp.minimum(i * hb + hb, nhb - 1), 0)
    return pl.pallas_call(
        body, name="conv_bwd", grid=(nt,),
        in_specs=[pl.BlockSpec((TR, N_A), lambda i: (i, 0)),
                  pl.BlockSpec((CHUNK, N_A), lambda i: (jnp.maximum(i * hb - 1, 0), 0)),
                  pl.BlockSpec((CHUNK, N_A), nxt),
                  pl.BlockSpec((TR, D_CONV), lambda i: (i, 0)),
                  pl.BlockSpec((CHUNK, D_CONV), nxt),
                  pl.BlockSpec((TR, D_CONV), lambda i: (i, 0)),
                  pl.BlockSpec((CHUNK, D_CONV), nxt),
                  pl.BlockSpec((CONV_WIDTH, D_CONV), lambda i: (0, 0)),
                  pl.BlockSpec((8, D_CONV), lambda i: (0, 0))],
        out_specs=[pl.BlockSpec((TR, N_A), lambda i: (i, 0)),
                   pl.BlockSpec((32, D_CONV), lambda i: (0, 0)),
                   pl.BlockSpec((8, D_CONV), lambda i: (0, 0))],
        out_shape=[jax.ShapeDtypeStruct((Lp, N_A), BF), jax.ShapeDtypeStruct((32, D_CONV), F32),
                   jax.ShapeDtypeStruct((8, D_CONV), F32)],
        scratch_shapes=[pltpu.VMEM((TR + CHUNK, D_CONV), F32), pltpu.VMEM((TR + CHUNK, D_CONV), F32),
                        pltpu.VMEM((32, 8, D_CONV), F32), pltpu.VMEM((7, _SH_ROWS, D_CONV), F32),
                        pltpu.VMEM((7, _SH_ROWS, D_CONV), F32)],
        compiler_params=_cp(("arbitrary",)),
    )(ua, ua, ua, yconv, yconv, dya, dya, cw, cvec)


def _hg_gates(ub_ref, lbv, row):
    q = ub_ref[:, 0:512].astype(F32)
    z = ub_ref[:, 512:1024].astype(F32)
    valid = row >= META_PAD
    sig = _sig(z)
    f = lbv + (1.0 - lbv) * sig
    g = jnp.where(valid, jnp.log(jnp.maximum(f, F_FLOOR)), 0.0)
    k = jnp.where(valid, (1.0 - lbv) * (1.0 - sig), 0.0)
    return q, k, g, sig, f


def _hg_chunk_terms(b_c, q_c, k_c):
    bm = b_c[CHUNK // 2 - 1:CHUNK // 2, :]
    bl = b_c[CHUNK - 1:CHUNK, :]
    e1 = jnp.exp(b_c - bm)
    e2 = jnp.exp(bm - b_c)
    e0 = jnp.exp(b_c)
    e3 = jnp.exp(bl - b_c)
    el = jnp.exp(bl)
    return e1, e2, e0, e3, el, q_c * e1, k_c * e2, q_c * e0, k_c * e3


def _hg_fwd(ub, lb, gn4, carry=None):
    Lp = ub.shape[0]
    nt = Lp // TR
    cpt = TR // CHUNK

    def body(ub_ref, lb_ref, gn_ref, yb_ref, o_ref, ss_ref, st, bsc, qsc, ksc, qes, els, ust):
        i = pl.program_id(0)

        @pl.when(i == 0)
        def _():
            st[...] = jnp.zeros_like(st)

        row = i * TR + lax.broadcasted_iota(jnp.int32, (TR, 1), 0)
        q, k, g, _, _ = _hg_gates(ub_ref, lb_ref[...], row)
        qsc[...] = _silu(q)
        ksc[...] = k
        bsc[...] = _mm_split(_chunk_tri(TR, False), g)
        tri = lax.broadcasted_iota(jnp.int32, (CHUNK, CHUNK), 1) <= lax.broadcasted_iota(jnp.int32, (CHUNK, CHUNK), 0)

        def intra(c, carry):
            rows = pl.ds(pl.multiple_of(c * CHUNK, CHUNK), CHUNK)
            _, _, _, _, el, qe, ke, qE, kd = _hg_chunk_terms(bsc[rows, :], qsc[rows, :], ksc[rows, :])
            qe, ke, kd = qe.astype(BF), ke.astype(BF), kd.astype(BF)
            qes[rows, :] = qE.astype(BF)
            els[c] = jnp.broadcast_to(el, (8, 512))
            sls = [slice(HG_D * h, HG_D * (h + 1)) for h in range(HG_HEADS)]
            v = [ub_ref[rows, 1024 + HG_D * h:1024 + HG_D * (h + 1)] for h in range(HG_HEADS)]
            a = [_nt(qe[:, sl], ke[:, sl]) for sl in sls]
            u = [_tn(v[h], kd[:, sls[h]]) for h in range(HG_HEADS)]
            a = [jnp.where(tri, x, 0.0).astype(BF) for x in a]
            oi = [_nn(a[h], v[h]) for h in range(HG_HEADS)]
            for h in range(HG_HEADS):
                ust[c, h] = u[h]
                o_ref[rows, sls[h]] = oi[h]
            return carry

        lax.fori_loop(0, cpt, intra, 0, unroll=2)

        for h in range(HG_HEADS):
            sl = slice(HG_D * h, HG_D * (h + 1))
            s = st[h]
            for c in range(cpt):
                ss_ref[c, h] = s
                s = els[c, 0:1, sl] * s + ust[c, h]
            st[h] = s

        def inter(c, carry):
            rows = pl.ds(pl.multiple_of(c * CHUNK, CHUNK), CHUNK)
            for h in range(HG_HEADS):
                sl = slice(HG_D * h, HG_D * (h + 1))
                o_ref[rows, sl] += _nt(qes[rows, sl], ss_ref[c, h].astype(BF))
            return carry

        lax.fori_loop(0, cpt, inter, 0, unroll=2)

        gate = ub_ref[:, 1536:2048].astype(F32)
        for h in range(HG_HEADS):
            sl = slice(HG_D * h, HG_D * (h + 1))
            o = o_ref[:, sl]
            r = lax.rsqrt(jnp.mean(o * o, axis=-1, keepdims=True) + EPS)
            yb_ref[:, sl] = (o * r * gn_ref[:, sl] * _silu(gate[:, sl])).astype(BF)

    in_specs = [pl.BlockSpec((TR, N_B), lambda i: (i, 0)), pl.BlockSpec((1, 512), lambda i: (0, 0)),
                pl.BlockSpec((1, 512), lambda i: (0, 0))]
    out_specs = [pl.BlockSpec((TR, 512), lambda i: (i, 0)), pl.BlockSpec((TR, 512), lambda i: (i, 0)),
                 pl.BlockSpec((cpt, HG_HEADS, HG_D, HG_D), lambda i: (i, 0, 0, 0))]
    out_shape = [jax.ShapeDtypeStruct((Lp, 512), BF), jax.ShapeDtypeStruct((Lp, 512), F32),
                 jax.ShapeDtypeStruct((Lp // CHUNK, HG_HEADS, HG_D, HG_D), F32)]
    scratch = [pltpu.VMEM((HG_HEADS, HG_D, HG_D), F32), pltpu.VMEM((TR, 512), F32),
               pltpu.VMEM((TR, 512), F32), pltpu.VMEM((TR, 512), F32), pltpu.VMEM((TR, 512), BF),
               pltpu.VMEM((cpt, 8, 512), F32), pltpu.VMEM((cpt, HG_HEADS, HG_D, HG_D), F32)]
    return _call_carrying(body, "hgrn_fwd", nt, in_specs, out_specs, out_shape, scratch, (ub, lb, gn4), carry)


def _hg_bwd(ub, lb, gn4, o_save, s_save, dyb, carry=None):
    Lp = ub.shape[0]
    nt = Lp // TR
    cpt = TR // CHUNK

    def body(ub_ref, lb_ref, gn_ref, o_ref, ss_ref, dy_ref, du_ref, ds_ref,
             dst, bsc, qsc, ksc, dosc, dqsc, dksc, dbsc, els, ust, dss):
        i = pl.program_id(0)
        t = nt - 1 - i

        @pl.when(i == 0)
        def _():
            dst[...] = jnp.zeros_like(dst)
            ds_ref[...] = jnp.zeros_like(ds_ref)

        lbv = lb_ref[...]
        row = t * TR + lax.broadcasted_iota(jnp.int32, (TR, 1), 0)
        valid = row >= META_PAD
        q, k, g, sig, f = _hg_gates(ub_ref, lbv, row)
        qsc[...] = _silu(q)
        ksc[...] = k
        bsc[...] = _mm_split(_chunk_tri(TR, False), g)

        gate = ub_ref[:, 1536:2048].astype(F32)
        dy = dy_ref[...].astype(F32)
        dgn = jnp.zeros((1, 512), F32)
        for h in range(HG_HEADS):
            sl = slice(HG_D * h, HG_D * (h + 1))
            o = o_ref[:, sl]
            r = lax.rsqrt(jnp.mean(o * o, axis=-1, keepdims=True) + EPS)
            ohat = o * r
            don = dy[:, sl] * _silu(gate[:, sl])
            du_ref[:, 1536 + HG_D * h:1536 + HG_D * (h + 1)] = (
                dy[:, sl] * ohat * gn_ref[:, sl] * _dsilu(gate[:, sl])).astype(BF)
            ds_ref[1:2, sl] += jnp.sum(don * ohat, axis=0, keepdims=True)
            gd = don * gn_ref[:, sl]
            dosc[:, sl] = r * (gd - ohat * jnp.mean(gd * ohat, axis=-1, keepdims=True))

        tri = lax.broadcasted_iota(jnp.int32, (CHUNK, CHUNK), 1) <= lax.broadcasted_iota(jnp.int32, (CHUNK, CHUNK), 0)
        last = lax.broadcasted_iota(jnp.int32, (CHUNK, 1), 0) == CHUNK - 1

        def incr(c, carry):
            rows = pl.ds(pl.multiple_of(c * CHUNK, CHUNK), CHUNK)
            b_c = bsc[rows, :]
            qE_b = (qsc[rows, :] * jnp.exp(b_c)).astype(BF)
            els[c] = jnp.broadcast_to(jnp.exp(b_c[CHUNK - 1:CHUNK, :]), (8, 512))
            do_c = dosc[rows, :].astype(BF)
            for h in range(HG_HEADS):
                sl = slice(HG_D * h, HG_D * (h + 1))
                ust[c, h] = _tn(do_c[:, sl], qE_b[:, sl])
            return carry

        lax.fori_loop(0, cpt, incr, 0, unroll=2)

        for h in range(HG_HEADS):
            sl = slice(HG_D * h, HG_D * (h + 1))
            d_s = dst[h]
            for c in reversed(range(cpt)):
                dss[c, h] = d_s
                d_s = els[c, 0:1, sl] * d_s + ust[c, h]
            dst[h] = d_s

        def chunk(c, carry):
            r0 = pl.multiple_of(c * CHUNK, CHUNK)
            rows = pl.ds(r0, CHUNK)
            e1, e2, e0, e3, el, qe, ke, qE, kd = _hg_chunk_terms(bsc[rows, :], qsc[rows, :], ksc[rows, :])
            qe_b, ke_b, kd_b = qe.astype(BF), ke.astype(BF), kd.astype(BF)
            do_c = dosc[rows, :].astype(BF)
            hs = range(HG_HEADS)
            sls = [slice(HG_D * h, HG_D * (h + 1)) for h in hs]
            v = [ub_ref[rows, 1024 + HG_D * h:1024 + HG_D * (h + 1)] for h in hs]
            do = [do_c[:, sl] for sl in sls]
            a = [_nt(qe_b[:, sl], ke_b[:, sl]) for sl in sls]
            da = [_nt(do[h], v[h]) for h in hs]
            dqE = [_nn(do[h], ss_ref[c, h].astype(BF)) for h in hs]
            dkd = [_nn(v[h], dss[c, h].astype(BF)) for h in hs]
            dv2 = [_nt(kd_b[:, sls[h]], dss[c, h].astype(BF)) for h in hs]
            a = [jnp.where(tri, x, 0.0).astype(BF) for x in a]
            da = [jnp.where(tri, x, 0.0).astype(BF) for x in da]
            dv = [_tn(a[h], do[h]) + dv2[h] for h in hs]
            dqe = [_nn(da[h], ke_b[:, sls[h]]) for h in hs]
            dke = [_tn(da[h], qe_b[:, sls[h]]) for h in hs]
            for h in hs:
                sl = sls[h]
                del_h = jnp.sum(ss_ref[c, h] * dss[c, h], axis=0, keepdims=True)
                dqsc[rows, sl] = dqE[h] * e0[:, sl] + dqe[h] * e1[:, sl]
                dksc[rows, sl] = dke[h] * e2[:, sl] + dkd[h] * e3[:, sl]
                tkd = dkd[h] * kd[:, sl]
                dbl = jnp.sum(tkd, axis=0, keepdims=True) + del_h * el[:, sl]
                dbsc[rows, sl] = (dqE[h] * qE[:, sl] + dqe[h] * qe[:, sl] - dke[h] * ke[:, sl] - tkd
                                  + jnp.where(last, dbl, 0.0))
                du_ref[rows, 1024 + HG_D * h:1024 + HG_D * (h + 1)] = dv[h].astype(BF)
            return carry

        lax.fori_loop(0, cpt, chunk, 0, unroll=2)

        dg = _mm_split(_chunk_tri(TR, True), dbsc[...])
        df = jnp.where(valid & (f > F_FLOOR), dg / f, 0.0)
        dk = jnp.where(valid, dksc[...], 0.0)
        dsig = (df - dk) * (1.0 - lbv)
        ds_ref[0:1, :] += jnp.sum((df - dk) * (1.0 - sig), axis=0, keepdims=True)
        du_ref[:, 512:1024] = (dsig * sig * (1.0 - sig)).astype(BF)
        du_ref[:, 0:512] = (dqsc[...] * _dsilu(q)).astype(BF)

    rev = lambda i: (nt - 1 - i, 0)
    in_specs = [pl.BlockSpec((TR, N_B), rev), pl.BlockSpec((1, 512), lambda i: (0, 0)),
                pl.BlockSpec((1, 512), lambda i: (0, 0)), pl.BlockSpec((TR, 512), rev),
                pl.BlockSpec((cpt, HG_HEADS, HG_D, HG_D), lambda i: (nt - 1 - i, 0, 0, 0)),
                pl.BlockSpec((TR, 512), rev)]
    out_specs = [pl.BlockSpec((TR, N_B), rev), pl.BlockSpec((8, 512), lambda i: (0, 0))]
    out_shape = [jax.ShapeDtypeStruct((Lp, N_B), BF), jax.ShapeDtypeStruct((8, 512), F32)]
    states = pltpu.VMEM((cpt, HG_HEADS, HG_D, HG_D), F32)
    scratch = ([pltpu.VMEM((HG_HEADS, HG_D, HG_D), F32)] + [pltpu.VMEM((TR, 512), F32)] * 7
               + [pltpu.VMEM((cpt, 8, 512), F32), states, states])
    return _call_carrying(body, "hgrn_bwd", nt, in_specs, out_specs, out_shape, scratch,
                          (ub, lb, gn4, o_save, s_save, dyb), carry)


_KCOL = (2 * 512) // 128
_VCOL = _KCOL + 1


def _swa_in_specs(nt, rev):
    tile = (lambda i: nt - 1 - i) if rev else (lambda i: i)
    hpt = TR // HALO
    return [
        pl.BlockSpec((TR, 512), lambda i: (tile(i), 0)),
        pl.BlockSpec((TR, 512), lambda i: (tile(i), 1)),
        pl.BlockSpec((TR, 128), lambda i: (tile(i), _KCOL)),
        pl.BlockSpec((TR, 128), lambda i: (tile(i), _VCOL)),
        pl.BlockSpec((HALO, 128), lambda i: (jnp.maximum(tile(i) * hpt - 1, 0), _KCOL)),
        pl.BlockSpec((HALO, 128), lambda i: (jnp.maximum(tile(i) * hpt - 1, 0), _VCOL)),
        pl.BlockSpec((CHUNK, 128), lambda i: (0, _KCOL)),
        pl.BlockSpec((CHUNK, 128), lambda i: (0, _VCOL)),
        pl.BlockSpec((1, 512), lambda i: (0, 0)),
        pl.BlockSpec((1, 128), lambda i: (0, 0)),
        pl.BlockSpec((1, ATT_Q_HEADS), lambda i: (0, 0)),
    ]


_WROWS = 2 * CHUNK + HALO + TR
_W0 = 2 * CHUNK
_C0 = _W0 + HALO
_SCALE = ATT_HD ** -0.5


def _group_ones(n):
    r = lax.broadcasted_iota(jnp.int32, (n, n), 0)
    c = lax.broadcasted_iota(jnp.int32, (n, n), 1)
    return jnp.where(jnp.right_shift(r, 6) == jnp.right_shift(c, 6), 1.0, 0.0).astype(BF)


def _group_mean(x, ones):
    hi = x.astype(BF)
    lo = (x - hi.astype(F32)).astype(BF)
    return (_nn(hi, ones) + _nn(lo, ones)) * (1.0 / ATT_HD)


def _head_rms(x, ones):
    r = lax.rsqrt(_group_mean(x * x, ones) + EPS)
    return x * r, r


def _swa_windows(kc_ref, vc_ref, kh_ref, vh_ref, km_ref, vm_ref, kg2, ones, kwin, krwin, vwin, vrwin):
    meta = pl.ds(META_PAD, N_META)
    for (k, v, r0, n) in ((km_ref[meta, :], vm_ref[meta, :], 0, N_META), (kh_ref[...], vh_ref[...], _W0, HALO),
                          (kc_ref[...], vc_ref[...], _C0, TR)):
        xhat, _ = _head_rms(k.astype(F32), ones)
        kn = xhat * kg2
        kwin[pl.ds(r0, n), :] = kn.astype(BF)
        krwin[pl.ds(r0, n), :] = pltpu.roll(kn, ATT_HD, 1).astype(BF)
        vwin[pl.ds(r0, n), :] = v
        if vrwin is not None:
            vrwin[pl.ds(r0, n), :] = pltpu.roll(v.astype(F32), ATT_HD, 1).astype(BF)
    zero = jnp.zeros((_W0 - N_META, 128), BF)
    for w in (kwin, krwin, vwin, vrwin):
        if w is not None:
            w[pl.ds(N_META, _W0 - N_META), :] = zero


def _swa_masks_t(t, qb):
    q0 = t * TR + qb * QB
    qc = jnp.right_shift(q0 + lax.broadcasted_iota(jnp.int32, (1, QB), 1), 6)
    kabs = q0 - HALO + lax.broadcasted_iota(jnp.int32, (QB + HALO, 1), 0)
    kc = jnp.right_shift(kabs + HALO, 6) - HALO // CHUNK
    mask_w = (kc <= qc) & (kc >= qc - 2) & (kabs >= META_PAD)
    return qc > 2, mask_w


def _swa_park(dtype):
    return [pltpu.VMEM((ATT_Q_HEADS, N_META, QB), dtype), pltpu.VMEM((ATT_Q_HEADS, QB + HALO, QB), dtype)]


def _split_heads(x, lane_hi):
    return jnp.where(lane_hi, 0.0, x).astype(BF), jnp.where(lane_hi, x, 0.0).astype(BF)


def _call_carrying(body, name, nt, in_specs, out_specs, out_shape, scratch, args, carry):
    if carry is None:
        return pl.pallas_call(body, name=name, grid=(nt,), in_specs=in_specs, out_specs=out_specs, out_shape=out_shape,
                              scratch_shapes=scratch, compiler_params=_cp(("arbitrary",)))(*args)
    kind, arrs = carry
    n = len(arrs)
    return pl.pallas_call(
        _carry_exchange(body, len(in_specs), len(out_specs), nt, kind, n), name=name + "_" + kind, grid=(nt,),
        in_specs=in_specs + [_ANY] * n, out_specs=out_specs + [_ANY] * n,
        out_shape=out_shape + _exchange_out_shapes(kind, arrs), scratch_shapes=scratch + _exchange_sems(n),
        compiler_params=_cp(("arbitrary",), has_side_effects=True),
    )(*args, *arrs)


def _swa_fwd(uc, qg8, kg2, sinks, carry=None):
    Lp = uc.shape[0]
    nt = Lp // TR
    nqb = TR // QB

    def body(q_ref, g_ref, kc_ref, vc_ref, kh_ref, vh_ref, km_ref, vm_ref, qg_ref, kg_ref, sk_ref,
             yc_ref, o_ref, lse_ref, kwin, krwin, vwin, vt, qlo, qhi, ot, s_m, s_w, p_m, p_w):
        t = pl.program_id(0)
        _swa_windows(kc_ref, vc_ref, kh_ref, vh_ref, km_ref, vm_ref, kg_ref[...], _group_ones(128),
                     kwin, krwin, vwin, None)
        vt[...] = vwin[...].T
        xhat, _ = _head_rms(q_ref[...].astype(F32), _group_ones(512))
        lane_hi = (lax.broadcasted_iota(jnp.int32, (1, 512), 1) & ATT_HD) != 0
        lo, hi = _split_heads(xhat * qg_ref[...] * _SCALE, lane_hi)
        qlo[...] = lo
        qhi[...] = hi
        for qb in range(nqb):
            rows = pl.ds(qb * QB, QB)
            wrows = pl.ds(_W0 + qb * QB, QB + HALO)
            mrows = pl.ds(0, N_META)
            mask_m, mask_w = _swa_masks_t(t, qb)
            for j in range(ATT_Q_HEADS):
                p, e = j // 2, j % 2
                ks = kwin if e == j // ATT_GROUP else krwin
                qp = (qlo, qhi)[e][rows, 128 * p:128 * (p + 1)]
                s_m[j] = _nt(ks[mrows, :], qp)
                s_w[j] = _nt(ks[wrows, :], qp)
            inv = []
            for j in range(ATT_Q_HEADS):
                sm = jnp.where(mask_m, s_m[j], NEG)
                sw = jnp.where(mask_w, s_w[j], NEG)
                sink = sk_ref[:, j:j + 1]
                m = jnp.maximum(jnp.maximum(jnp.max(sm, axis=0, keepdims=True),
                                            jnp.max(sw, axis=0, keepdims=True)), sink)
                em = jnp.exp(sm - m)
                ew = jnp.exp(sw - m)
                den = jnp.sum(em, axis=0, keepdims=True) + jnp.sum(ew, axis=0, keepdims=True) + jnp.exp(sink - m)
                p_m[j] = em.astype(BF)
                p_w[j] = ew.astype(BF)
                lse_ref[j:j + 1, pl.ds(qb * QB, QB)] = m + jnp.log(den)
                inv.append(1.0 / den)
            for j in range(ATT_Q_HEADS):
                vrows = pl.ds(ATT_HD * (j // ATT_GROUP), ATT_HD)
                ot[pl.ds(ATT_HD * j, ATT_HD), pl.ds(qb * QB, QB)] = (
                    _nn(vt[vrows, pl.ds(0, N_META)], p_m[j])
                    + _nn(vt[vrows, pl.ds(_W0 + qb * QB, QB + HALO)], p_w[j])) * inv[j]
        o = ot[...].T
        o_ref[...] = o
        yc_ref[...] = (o * _silu(g_ref[...].astype(F32))).astype(BF)

    win = pltpu.VMEM((_WROWS, 128), BF)
    in_specs = _swa_in_specs(nt, False)
    out_specs = [pl.BlockSpec((TR, 512), lambda i: (i, 0)), pl.BlockSpec((TR, 512), lambda i: (i, 0)),
                 pl.BlockSpec((ATT_Q_HEADS, TR), lambda i: (0, i))]
    out_shape = [jax.ShapeDtypeStruct((Lp, 512), BF), jax.ShapeDtypeStruct((Lp, 512), F32),
                 jax.ShapeDtypeStruct((ATT_Q_HEADS, Lp), F32)]
    scratch = [win, win, win, pltpu.VMEM((128, _WROWS), BF), pltpu.VMEM((TR, 512), BF),
               pltpu.VMEM((TR, 512), BF), pltpu.VMEM((512, TR), F32)] + _swa_park(F32) + _swa_park(BF)
    return _call_carrying(body, "swa_fwd", nt, in_specs, out_specs, out_shape, scratch,
                          (uc, uc, uc, uc, uc, uc, uc, uc, qg8, kg2, sinks), carry)


def _swa_bwd(uc, qg8, kg2, sinks, o_save, lse, dyc):
    Lp = uc.shape[0]
    nt = Lp // TR
    nqb = TR // QB

    def body(q_ref, g_ref, kc_ref, vc_ref, kh_ref, vh_ref, km_ref, vm_ref, qg_ref, kg_ref, sk_ref,
             o_ref, lse_ref, dy_ref, du_ref, dg_ref, dsk_ref,
             kwin, krwin, vwin, vrwin, kt, krt, qlo, qhi, dolo, dohi, dqt, dk_dir, dk_rol, dv_dir, dv_rol,
             carry_k, carry_v, meta_k, meta_v, s_m, s_w, dp_m, dp_w, p_m, p_w, ds_m, ds_w):
        i = pl.program_id(0)
        t = nt - 1 - i

        @pl.when(i == 0)
        def _():
            carry_k[...] = jnp.zeros_like(carry_k)
            carry_v[...] = jnp.zeros_like(carry_v)
            meta_k[...] = jnp.zeros_like(meta_k)
            meta_v[...] = jnp.zeros_like(meta_v)
            dg_ref[...] = jnp.zeros_like(dg_ref)
            dsk_ref[...] = jnp.zeros_like(dsk_ref)

        ones128 = _group_ones(128)
        ones512 = _group_ones(512)
        _swa_windows(kc_ref, vc_ref, kh_ref, vh_ref, km_ref, vm_ref, kg_ref[...], ones128, kwin, krwin, vwin, vrwin)
        kt[...] = kwin[...].T
        krt[...] = krwin[...].T
        xhat_q, r_q = _head_rms(q_ref[...].astype(F32), ones512)
        lane_hi = (lax.broadcasted_iota(jnp.int32, (1, 512), 1) & ATT_HD) != 0
        lo, hi = _split_heads(xhat_q * qg_ref[...] * _SCALE, lane_hi)
        qlo[...] = lo
        qhi[...] = hi
        gate = g_ref[...].astype(F32)
        dy = dy_ref[...].astype(F32)
        do = dy * _silu(gate)
        o = o_ref[...]
        du_ref[:, 512:1024] = (dy * o * _dsilu(gate)).astype(BF)
        lo, hi = _split_heads(do, lane_hi)
        dolo[...] = lo
        dohi[...] = hi
        hsel = jnp.where(jnp.right_shift(lax.broadcasted_iota(jnp.int32, (ATT_Q_HEADS, 512), 1), 6)
                         == lax.broadcasted_iota(jnp.int32, (ATT_Q_HEADS, 512), 0), 1.0, 0.0).astype(BF)
        prod = do * o
        p_hi = prod.astype(BF)
        d_t = _nt(hsel, p_hi) + _nt(hsel, (prod - p_hi.astype(F32)).astype(BF))
        for acc in (dk_dir, dk_rol, dv_dir, dv_rol):
            acc[...] = jnp.zeros_like(acc)

        for qb in range(nqb):
            rows = pl.ds(qb * QB, QB)
            qcols = pl.ds(qb * QB, QB)
            wrows = pl.ds(_W0 + qb * QB, QB + HALO)
            mrows = pl.ds(0, N_META)
            mask_m, mask_w = _swa_masks_t(t, qb)
            for j in range(ATT_Q_HEADS):
                p, e = j // 2, j % 2
                ks, vs = (kwin, vwin) if e == j // ATT_GROUP else (krwin, vrwin)
                pair = slice(128 * p, 128 * (p + 1))
                qp = (qlo, qhi)[e][rows, pair]
                dop = (dolo, dohi)[e][rows, pair]
                s_m[j] = _nt(ks[mrows, :], qp)
                s_w[j] = _nt(ks[wrows, :], qp)
                dp_m[j] = _nt(vs[mrows, :], dop)
                dp_w[j] = _nt(vs[wrows, :], dop)
            for j in range(ATT_Q_HEADS):
                lse_j = lse_ref[j:j + 1, qcols]
                d_j = d_t[j:j + 1, qb * QB:(qb + 1) * QB]
                em = jnp.exp(jnp.where(mask_m, s_m[j], NEG) - lse_j)
                ew = jnp.exp(jnp.where(mask_w, s_w[j], NEG) - lse_j)
                p_m[j] = em.astype(BF)
                p_w[j] = ew.astype(BF)
                ds_m[j] = (em * (dp_m[j] - d_j)).astype(BF)
                ds_w[j] = (ew * (dp_w[j] - d_j)).astype(BF)
                dsk_ref[j:j + 1, :] -= jnp.exp(sk_ref[:, j:j + 1] - lse_j) * d_j
            for j in range(ATT_Q_HEADS):
                e = j % 2
                ktr = kt if e == j // ATT_GROUP else krt
                hrows = pl.ds(ATT_HD * e, ATT_HD)
                dqt[pl.ds(ATT_HD * j, ATT_HD), qcols] = (_nn(ktr[hrows, pl.ds(0, N_META)], ds_m[j])
                                                         + _nn(ktr[hrows, pl.ds(_W0 + qb * QB, QB + HALO)], ds_w[j]))
            for direct, dk_acc, dv_acc in ((True, dk_dir, dv_dir), (False, dk_rol, dv_rol)):
                heads = [j for j in range(ATT_Q_HEADS) if (j % 2 == j // ATT_GROUP) == direct]
                q_cat = jnp.concatenate([(qlo, qhi)[j % 2][rows, 128 * (j // 2):128 * (j // 2 + 1)] for j in heads], axis=0)
                do_cat = jnp.concatenate([(dolo, dohi)[j % 2][rows, 128 * (j // 2):128 * (j // 2 + 1)] for j in heads], axis=0)
                dk_acc[mrows, :] += _nn(jnp.concatenate([ds_m[j] for j in heads], axis=1), q_cat)
                dk_acc[wrows, :] += _nn(jnp.concatenate([ds_w[j] for j in heads], axis=1), q_cat)
                dv_acc[mrows, :] += _nn(jnp.concatenate([p_m[j] for j in heads], axis=1), do_cat)
                dv_acc[wrows, :] += _nn(jnp.concatenate([p_w[j] for j in heads], axis=1), do_cat)

        dk_dir[...] += pltpu.roll(dk_rol[...], ATT_HD, 1)
        dv_dir[...] += pltpu.roll(dv_rol[...], ATT_HD, 1)
        meta_k[...] += dk_dir[pl.ds(0, N_META), :]
        meta_v[...] += dv_dir[pl.ds(0, N_META), :]
        first = jnp.where(t == 0, 1.0, 0.0)
        dk_dir[pl.ds(_C0 + TR - HALO, HALO), :] += carry_k[...]
        dv_dir[pl.ds(_C0 + TR - HALO, HALO), :] += carry_v[...]
        dk_dir[pl.ds(_C0 + META_PAD, N_META), :] += first * meta_k[...]
        dv_dir[pl.ds(_C0 + META_PAD, N_META), :] += first * meta_v[...]
        carry_k[...] = dk_dir[pl.ds(_W0, HALO), :]
        carry_v[...] = dv_dir[pl.ds(_W0, HALO), :]

        du_ref[:, 1152:1280] = dv_dir[pl.ds(_C0, TR), :].astype(BF)
        xhat_k, r_k = _head_rms(kc_ref[...].astype(F32), ones128)
        dkn = dk_dir[pl.ds(_C0, TR), :]
        dg_ref[1:2, 0:128] += jnp.sum(dkn * xhat_k, axis=0, keepdims=True)
        gd = dkn * kg_ref[...]
        du_ref[:, 1024:1152] = (r_k * (gd - xhat_k * _group_mean(gd * xhat_k, ones128))).astype(BF)
        dqn = dqt[...].T * _SCALE
        dg_ref[0:1, :] += jnp.sum(dqn * xhat_q, axis=0, keepdims=True)
        gd = dqn * qg_ref[...]
        du_ref[:, 0:512] = (r_q * (gd - xhat_q * _group_mean(gd * xhat_q, ones512))).astype(BF)

    rev = lambda i: (nt - 1 - i, 0)
    specs = _swa_in_specs(nt, True)
    win = pltpu.VMEM((_WROWS, 128), BF)
    wint = pltpu.VMEM((128, _WROWS), BF)
    tile_bf = pltpu.VMEM((TR, 512), BF)
    acc = pltpu.VMEM((_WROWS, 128), F32)
    return pl.pallas_call(
        body, name="swa_bwd", grid=(nt,),
        in_specs=specs + [pl.BlockSpec((TR, 512), rev), pl.BlockSpec((ATT_Q_HEADS, TR), lambda i: (0, nt - 1 - i)),
                          pl.BlockSpec((TR, 512), rev)],
        out_specs=[pl.BlockSpec((TR, N_C), rev), pl.BlockSpec((8, 512), lambda i: (0, 0)),
                   pl.BlockSpec((8, 128), lambda i: (0, 0))],
        out_shape=[jax.ShapeDtypeStruct((Lp, N_C), BF), jax.ShapeDtypeStruct((8, 512), F32),
                   jax.ShapeDtypeStruct((8, 128), F32)],
        scratch_shapes=[win, win, win, win, wint, wint, tile_bf, tile_bf, tile_bf, tile_bf,
                        pltpu.VMEM((512, TR), F32), acc, acc, acc, acc,
                        pltpu.VMEM((HALO, 128), F32), pltpu.VMEM((HALO, 128), F32),
                        pltpu.VMEM((N_META, 128), F32), pltpu.VMEM((N_META, 128), F32)]
        + _swa_park(F32) + _swa_park(F32) + _swa_park(BF) + _swa_park(BF),
        compiler_params=_cp(("arbitrary",)),
    )(uc, uc, uc, uc, uc, uc, uc, uc, qg8, kg2, sinks, o_save, lse, dyc)


def _mix_fwd(h, ya, yb, yc, ug, wa, wb, wc, wo):
    Lp = h.shape[0]
    wspec = lambda r: pl.BlockSpec((r, D_MODEL), lambda i: (0, 0))
    yspec = pl.BlockSpec((TRM, 512), lambda i: (i, 0))
    hspec = pl.BlockSpec((TRM, D_MODEL), lambda i: (i, 0))

    def body(h_ref, ya_ref, yb_ref, yc_ref, ug_ref, wa_ref, wb_ref, wc_ref, wo_ref,
             hn_ref, za_ref, zb_ref, zc_ref, mx_ref):
        mixed = jnp.zeros((TRM, D_MODEL), F32)
        for n, (y_ref, w_ref, z_ref) in enumerate(((ya_ref, wa_ref, za_ref), (yb_ref, wb_ref, zb_ref),
                                                   (yc_ref, wc_ref, zc_ref))):
            z = _nn(y_ref[...], w_ref[...])
            z_ref[...] = z.astype(BF)
            mixed = mixed + _sig(ug_ref[:, D_MODEL * n:D_MODEL * (n + 1)].astype(F32)) * z
        mixed = mixed.astype(BF)
        mx_ref[...] = mixed
        hn_ref[...] = h_ref[...] + _nn(mixed, wo_ref[...])

    return pl.pallas_call(
        body, name="mix_fwd", grid=(Lp // TRM,),
        in_specs=[hspec, yspec, yspec, yspec, pl.BlockSpec((TRM, N_G), lambda i: (i, 0)),
                  wspec(512), wspec(512), wspec(512), wspec(D_MODEL)],
        out_specs=[hspec, hspec, hspec, hspec, hspec],
        out_shape=[jax.ShapeDtypeStruct((Lp, D_MODEL), F32)] + [jax.ShapeDtypeStruct((Lp, D_MODEL), BF)] * 4,
        compiler_params=_cp(("parallel",)),
    )(h, ya, yb, yc, ug, wa, wb, wc, wo)


def _mix_bwd(dh, za, zb, zc, ug, wa, wb, wc, wo):
    Lp = dh.shape[0]
    wspec = lambda r: pl.BlockSpec((r, D_MODEL), lambda i: (0, 0))
    yspec = pl.BlockSpec((TRM, 512), lambda i: (i, 0))
    hspec = pl.BlockSpec((TRM, D_MODEL), lambda i: (i, 0))
    gspec = pl.BlockSpec((TRM, N_G), lambda i: (i, 0))

    def body(dh_ref, za_ref, zb_ref, zc_ref, ug_ref, wa_ref, wb_ref, wc_ref, wo_ref,
             dug_ref, dza_ref, dzb_ref, dzc_ref, dya_ref, dyb_ref, dyc_ref):
        dmix = _nt(dh_ref[...].astype(BF), wo_ref[...])
        for n, (z_ref, w_ref, dz_ref, dy_ref) in enumerate(((za_ref, wa_ref, dza_ref, dya_ref),
                                                            (zb_ref, wb_ref, dzb_ref, dyb_ref),
                                                            (zc_ref, wc_ref, dzc_ref, dyc_ref))):
            sl = slice(D_MODEL * n, D_MODEL * (n + 1))
            gt = _sig(ug_ref[:, sl].astype(F32))
            dz = dmix * gt
            dug_ref[:, sl] = (dz * z_ref[...].astype(F32) * (1.0 - gt)).astype(BF)
            dz = dz.astype(BF)
            dz_ref[...] = dz
            dy_ref[...] = _nt(dz, w_ref[...]).astype(BF)

    bf = lambda n: jax.ShapeDtypeStruct((Lp, n), BF)
    return pl.pallas_call(
        body, name="mix_bwd", grid=(Lp // TRM,),
        in_specs=[hspec, hspec, hspec, hspec, gspec, wspec(512), wspec(512), wspec(512), wspec(D_MODEL)],
        out_specs=[gspec, hspec, hspec, hspec, yspec, yspec, yspec],
        out_shape=[bf(N_G), bf(D_MODEL), bf(D_MODEL), bf(D_MODEL), bf(512), bf(512), bf(512)],
        compiler_params=_cp(("parallel",)),
    )(dh, za, zb, zc, ug, wa, wb, wc, wo)


def _inproj_bwd(dus, ws, h, dh, g, carry=None):
    Lp = h.shape[0]
    widths = [w.shape[1] for w in ws]

    def body(dg_ref, da_ref, db_ref, dc_ref, wg_ref, wa_ref, wb_ref, wc_ref, h_ref, dh_ref, g_ref, o_ref, gg_ref):
        @pl.when(pl.program_id(0) == 0)
        def _():
            gg_ref[...] = jnp.zeros_like(gg_ref)

        dhn = (_nt(dg_ref[...], wg_ref[...]) + _nt(da_ref[...], wa_ref[...])
               + _nt(db_ref[...], wb_ref[...]) + _nt(dc_ref[...], wc_ref[...]))
        x = h_ref[...]
        r = lax.rsqrt(jnp.mean(x * x, axis=-1, keepdims=True) + EPS)
        xhat = x * r
        gg_ref[0:1, :] += jnp.sum(dhn * xhat, axis=0, keepdims=True)
        gd = dhn * g_ref[...]
        o_ref[...] = dh_ref[...] + r * (gd - xhat * jnp.mean(gd * xhat, axis=-1, keepdims=True))

    hspec = pl.BlockSpec((TRM, D_MODEL), lambda i: (i, 0))
    in_specs = ([pl.BlockSpec((TRM, n), lambda i: (i, 0)) for n in widths]
                + [pl.BlockSpec((D_MODEL, n), lambda i: (0, 0), pipeline_mode=pl.Buffered(1)) for n in widths]
                + [hspec, hspec, pl.BlockSpec((1, D_MODEL), lambda i: (0, 0))])
    out_specs = [hspec, pl.BlockSpec((8, D_MODEL), lambda i: (0, 0))]
    out_shape = [jax.ShapeDtypeStruct((Lp, D_MODEL), F32), jax.ShapeDtypeStruct((8, D_MODEL), F32)]
    return _call_carrying(body, "inproj_bwd", Lp // TRM, in_specs, out_specs, out_shape, [],
                          (*dus, *ws, h, dh, g), carry)


def _loss_head(h, tgt_pad, seq):
    Lp = h.shape[0]
    nt = Lp // TR

    def body(h_ref, t_ref, dh_ref, l_ref):
        i = pl.program_id(0)

        @pl.when(i == 0)
        def _():
            l_ref[...] = jnp.zeros_like(l_ref)

        row = i * TR + lax.broadcasted_iota(jnp.int32, (TR, 1), 0)
        e = jnp.where((row >= CHUNK) & (row < CHUNK + seq), h_ref[...] - t_ref[...], 0.0)
        dh_ref[...] = e * (1.0 / D_MODEL)
        l_ref[...] += (0.5 / D_MODEL) * jnp.sum(jnp.sum(e * e, axis=0, keepdims=True), axis=1, keepdims=True)

    hspec = pl.BlockSpec((TR, D_MODEL), lambda i: (i, 0))
    return pl.pallas_call(
        body, name="loss_head", grid=(nt,), in_specs=[hspec, hspec],
        out_specs=[hspec, pl.BlockSpec((8, 128), lambda i: (0, 0))],
        out_shape=[jax.ShapeDtypeStruct((Lp, D_MODEL), F32), jax.ShapeDtypeStruct((8, 128), F32)],
        compiler_params=_cp(("arbitrary",)),
    )(h, tgt_pad)


def _lb_softmax(lb_ref):
    x = lb_ref[...]
    e = jnp.exp(x - jnp.max(x, axis=0, keepdims=True))
    return e / jnp.sum(e, axis=0, keepdims=True)


def _lb_fwd(hg_lb):
    def body(lb_ref, o_ref):
        sm = _lb_softmax(lb_ref)
        acc = jnp.zeros((1, 512), F32)
        for l in range(DEPTH):
            if l > 0:
                acc = acc + sm[l:l + 1, :]
            o_ref[l:l + 1, :] = jnp.clip(acc, 0.0, 1.0)

    return pl.pallas_call(body, name="lb_fwd", out_shape=jax.ShapeDtypeStruct((DEPTH, 512), F32))(hg_lb)


def _lb_bwd(hg_lb, dlb_all):
    def body(lb_ref, d_ref, o_ref):
        sm = _lb_softmax(lb_ref)
        acc = jnp.zeros((1, 512), F32)
        gm = []
        for l in range(DEPTH):
            if l > 0:
                acc = acc + sm[l:l + 1, :]
            gm.append(jnp.where((acc >= 0.0) & (acc <= 1.0), d_ref[l:l + 1, :], 0.0))
        dsm = [jnp.zeros((1, 512), F32)]
        for j in range(1, DEPTH):
            s = gm[j]
            for l in range(j + 1, DEPTH):
                s = s + gm[l]
            dsm.append(s)
        dot = dsm[0] * sm[0:1, :]
        for j in range(1, DEPTH):
            dot = dot + dsm[j] * sm[j:j + 1, :]
        for j in range(DEPTH):
            o_ref[j:j + 1, :] = sm[j:j + 1, :] * (dsm[j] - dot)

    return pl.pallas_call(body, name="lb_bwd", out_shape=jax.ShapeDtypeStruct((DEPTH, 512), F32))(hg_lb, dlb_all)


_ANY = pl.BlockSpec(memory_space=pl.ANY)


def _chip_peers():
    x, y, c = lax.axis_index("x"), lax.axis_index("y"), lax.axis_index("c")
    return (x, y, c), [(1 - x, y, c), (x, 1 - y, c), (1 - x, 1 - y, c)]


def _exchange(kind, ins, outs, send, recv, loc):
    (x, y, c), peers = _chip_peers()
    me = 2 * x + y
    ds = []
    for a in range(len(ins)):
        if kind == "gather":
            ds.append(pltpu.make_async_copy(ins[a], outs[a].at[me], loc.at[a]))
        else:
            ds.append(pltpu.make_async_copy(ins[a].at[me], outs[a].at[0], loc.at[a]))
        for p, (px, py, pc) in enumerate(peers):
            src, dst = (ins[a], outs[a].at[me]) if kind == "gather" else (ins[a].at[2 * px + py], outs[a].at[1 + p])
            ds.append(pltpu.make_async_remote_copy(src_ref=src, dst_ref=dst, send_sem=send.at[a, p],
                                                   recv_sem=recv.at[a, p], device_id=(px, py, pc), device_id_type=MESH))
    return ds


def _exchange_out_shapes(kind, arrs):
    if kind == "gather":
        return [jax.ShapeDtypeStruct((4,) + a.shape, a.dtype) for a in arrs]
    return [jax.ShapeDtypeStruct(a.shape, a.dtype) for a in arrs]


def _exchange_sems(n):
    return [pltpu.SemaphoreType.DMA((n, 3)), pltpu.SemaphoreType.DMA((n, 3)), pltpu.SemaphoreType.DMA((n,))]


def _exchange_chips(kind, arrs):
    n = len(arrs)

    def body(*refs):
        ds = _exchange(kind, refs[:n], refs[n:2 * n], *refs[2 * n:])
        for d in ds:
            d.start()
        for d in ds:
            d.wait()

    return pl.pallas_call(
        body, name=kind + "_chips", in_specs=[_ANY] * n, out_specs=[_ANY] * n,
        out_shape=_exchange_out_shapes(kind, arrs), scratch_shapes=_exchange_sems(n),
        compiler_params=pltpu.CompilerParams(has_side_effects=True),
    )(*arrs)


def _carry_exchange(body, n_in, n_out, n_steps, kind, n):
    def wrapped(*refs):
        ins, cin = refs[:n_in], refs[n_in:n_in + n]
        outs, cout = refs[n_in + n:n_in + n + n_out], refs[n_in + n + n_out:n_in + 2 * n + n_out]
        scr, sems = refs[n_in + 2 * n + n_out:-3], refs[-3:]
        i = pl.program_id(0)

        @pl.when(i == 0)
        def _():
            for d in _exchange(kind, cin, cout, *sems):
                d.start()

        body(*ins, *outs, *scr)

        @pl.when(i == n_steps - 1)
        def _():
            for d in _exchange(kind, cin, cout, *sems):
                d.wait()

    return wrapped


def _swap_cores(arrs):
    n = len(arrs)

    def body(*refs):
        ins, outs = refs[:n], refs[n:2 * n]
        send, recv = refs[2 * n:]
        x, y, c = lax.axis_index("x"), lax.axis_index("y"), lax.axis_index("c")
        rdmas = []
        for a in range(n):
            r = pltpu.make_async_remote_copy(src_ref=ins[a], dst_ref=outs[a], send_sem=send.at[a], recv_sem=recv.at[a],
                                             device_id=(x, y, 1 - c), device_id_type=MESH)
            r.start()
            rdmas.append(r)
        for r in rdmas:
            r.wait()

    return pl.pallas_call(
        body, name="swap_cores", in_specs=[_ANY] * n, out_specs=[_ANY] * n,
        out_shape=[jax.ShapeDtypeStruct(a.shape, a.dtype) for a in arrs],
        scratch_shapes=[pltpu.SemaphoreType.DMA((n,)), pltpu.SemaphoreType.DMA((n,))],
        compiler_params=pltpu.CompilerParams(has_side_effects=True),
    )(*arrs)


def _allsum_small(p):
    R = p.shape[0]

    def body(p_ref, o_ref, buf, send, recv):
        x, y, c = lax.axis_index("x"), lax.axis_index("y"), lax.axis_index("c")
        me = 4 * x + 2 * y + c
        buf[me] = p_ref[...]
        rdmas = []
        for k in range(1, 8):
            peer = (x ^ (k >> 2), y ^ ((k >> 1) & 1), c ^ (k & 1))
            r = pltpu.make_async_remote_copy(src_ref=p_ref, dst_ref=buf.at[me], send_sem=send.at[k - 1],
                                             recv_sem=recv.at[k - 1], device_id=peer, device_id_type=MESH)
            r.start()
            rdmas.append(r)
        for r in rdmas:
            r.wait()
        acc = buf[0]
        for d in range(1, 8):
            acc = acc + buf[d]
        o_ref[...] = acc

    return pl.pallas_call(
        body, name="allsum_small", out_shape=jax.ShapeDtypeStruct((R, 512), F32),
        in_specs=[pl.BlockSpec(memory_space=pltpu.VMEM)], out_specs=pl.BlockSpec(memory_space=pltpu.VMEM),
        scratch_shapes=[pltpu.VMEM((8, R, 512), F32), pltpu.SemaphoreType.DMA((7,)), pltpu.SemaphoreType.DMA((7,))],
        compiler_params=_cp(has_side_effects=True),
    )(p)


def _sum4(parts, name):
    _, R, C = parts.shape
    tr = 256 if R % 256 == 0 else R

    def body(p_ref, o_ref):
        o_ref[...] = ((p_ref[0] + p_ref[1]) + p_ref[2]) + p_ref[3]

    return pl.pallas_call(
        body, name=name, grid=(R // tr,), in_specs=[pl.BlockSpec((4, tr, C), lambda i: (0, i, 0))],
        out_specs=pl.BlockSpec((tr, C), lambda i: (i, 0)), out_shape=jax.ShapeDtypeStruct((R, C), F32),
        compiler_params=_cp(("parallel",)),
    )(parts)


def _adamw(w, m, v, g0, g1, name):
    R, C = w.shape
    tr = 256 if R % 256 == 0 else R
    two = g1 is not None
    c1 = 1.0 / (1.0 - ADAM_B1 ** ADAM_STEP)
    c2 = 1.0 / (1.0 - ADAM_B2 ** ADAM_STEP)

    def body(*refs):
        if two:
            w_ref, m_ref, v_ref, a_ref, b_ref, g_ref, d_ref, nm_ref, nv_ref = refs
            g = a_ref[...] + b_ref[...]
        else:
            w_ref, m_ref, v_ref, a_ref, g_ref, d_ref, nm_ref, nv_ref = refs
            g = a_ref[...]
        g_ref[...] = g
        m = ADAM_B1 * m_ref[...] + (1.0 - ADAM_B1) * g
        v = ADAM_B2 * v_ref[...] + (1.0 - ADAM_B2) * (g * g)
        nm_ref[...] = m
        nv_ref[...] = v
        d_ref[...] = -ADAM_LR * ((m * c1) / (jnp.sqrt(v * c2) + ADAM_EPS) + ADAM_WD * w_ref[...])

    spec = pl.BlockSpec((tr, C), lambda i: (i, 0))
    n_in = 5 if two else 4
    ins = (w, m, v, g0, g1) if two else (w, m, v, g0)
    return pl.pallas_call(
        body, name=name, grid=(R // tr,), in_specs=[spec] * n_in, out_specs=[spec] * 4,
        out_shape=[jax.ShapeDtypeStruct((R, C), F32)] * 4, compiler_params=_cp(("parallel",)),
    )(*ins)


def _pad8(a):
    r = (-a.shape[0]) % 8
    return a if r == 0 else jnp.pad(a, ((0, r), (0, 0)))


def _local_step(x, tgt, meta, P, shards=None, prep=None, pack=None):
    seq = x.shape[0]
    Lp = -(-(seq + CHUNK) // TR) * TR
    tail = Lp - seq - CHUNK
    h = jnp.concatenate([jnp.zeros((META_PAD, D_MODEL), F32), meta, x, jnp.zeros((tail, D_MODEL), F32)], axis=0)
    tgt_pad = jnp.pad(tgt, ((CHUNK, tail), (0, 0)))

    P = list(P)
    saved = []
    for l in range(DEPTH):
        p = P[l]
        hn = _rms_fwd(h, p["norm_g"])
        mm = functools.partial(_matmul, out_dtype=BF, tm=TR, tk=D_MODEL, col_major_grid=True)
        ug = mm(hn, p["w_g"], tn=N_G // 2, name="inproj_g")
        ua = mm(hn, p["w_a"], tn=N_A, name="inproj_a")
        ub = mm(hn, p["w_b"], tn=N_B, name="inproj_b")
        uc = mm(hn, p["w_c"], tn=N_C, name="inproj_c")
        nxt = shards[l + 1] if shards is not None and l + 1 < DEPTH else None
        carry = (lambda part: ("gather", part)) if nxt is not None else (lambda part: None)
        res_a = _conv_fwd(ua, p["conv_w"], p["conv_vec"], carry(nxt and nxt[1:2]))
        res_b = _hg_fwd(ub, p["lb"], p["gn4"], carry(nxt and nxt[0:1]))
        res_c = _swa_fwd(uc, p["qg"], p["kg"], p["sinks"], carry(nxt and nxt[2:]))
        (ya, yconv), (yb, o_hg, s_hg), (yc, o_at, lse) = res_a[:2], res_b[:3], res_c[:3]
        if nxt is not None:
            P.append(prep(l + 1, [*res_b[3:], *res_a[2:], *res_c[3:]]))
        h_new, za, zb, zc, mixed = _mix_fwd(h, ya, yb, yc, ug, p["w_ao"], p["w_bo"], p["w_co"], p["w_out"])
        saved.append(dict(h=h, hn=hn, ug=ug, ua=ua, ub=ub, uc=uc, ya=ya, yconv=yconv, yb=yb, o_hg=o_hg, s_hg=s_hg,
                          yc=yc, o_at=o_at, lse=lse, za=za, zb=zb, zc=zc, mixed=mixed))
        h = h_new

    dh, loss8 = _loss_head(h, tgt_pad, seq)

    grads = [None] * DEPTH
    parts = [[None, None] for _ in range(DEPTH)]
    pending = None
    tk_dw = 2 * TR if Lp % (2 * TR) == 0 else TR
    for l in reversed(range(DEPTH)):
        p, s = P[l], saved[l]
        dug, dza, dzb, dzc, dya, dyb, dyc = _mix_bwd(dh, s["za"], s["zb"], s["zc"], s["ug"],
                                                      p["w_ao"], p["w_bo"], p["w_co"], p["w_out"])
        tnmm = functools.partial(_matmul, ta=True, out_dtype=F32, tk=tk_dw)
        g = {}
        g["w_out"] = tnmm(s["mixed"], dh, tm=D_MODEL, tn=D_MODEL, name="dw_out")
        g["w_ao"] = tnmm(s["ya"], dza, tm=512, tn=D_MODEL, name="dw_ao")
        g["w_bo"] = tnmm(s["yb"], dzb, tm=512, tn=D_MODEL, name="dw_bo")
        g["w_co"] = tnmm(s["yc"], dzc, tm=512, tn=D_MODEL, name="dw_co")
        dua, g["conv_w"], g["conv_vec"] = _conv_bwd(s["ua"], s["yconv"], dya, p["conv_w"], p["conv_vec"])
        carry = ("scatter", pending[1]) if pending is not None else None
        res = _hg_bwd(s["ub"], p["lb"], p["gn4"], s["o_hg"], s["s_hg"], dyb, carry)
        dub, g["hg_small"] = res[:2]
        if carry is not None:
            parts[pending[0]][1] = res[2:]
        duc, g["at_gain"], g["at_sink"] = _swa_bwd(s["uc"], p["qg"], p["kg"], p["sinks"], s["o_at"], s["lse"], dyc)
        g["w_g"] = tnmm(s["hn"], dug, tm=D_MODEL, tn=N_G // 2, name="dw_in_g")
        g["w_a"] = tnmm(s["hn"], dua, tm=D_MODEL, tn=N_A, name="dw_in_a")
        g["w_b"] = tnmm(s["hn"], dub, tm=D_MODEL, tn=N_B, name="dw_in_b")
        g["w_c"] = tnmm(s["hn"], duc, tm=D_MODEL, tn=N_C, name="dw_in_c")
        first, second = pack(g) if pack is not None else (None, None)
        if first is not None and l == 0:
            first, second = first + second, []
        res = _inproj_bwd([dug, dua, dub, duc], [p["w_g"], p["w_a"], p["w_b"], p["w_c"]], s["h"], dh, p["norm_g"],
                          ("scatter", first) if first is not None else None)
        dh, g["norm_g"] = res[:2]
        grads[l] = g
        if pack is not None:
            parts[l] = [res[2:3], res[3:]] if l == 0 else [res[2:], None]
            pending = (l, second) if l > 0 else None
    return loss8, dh, grads, parts


def _split_w_in(w):
    return dict(w_a=w[:, 0:1536], w_b=w[:, 1536:3584],
                w_c=jnp.concatenate([w[:, 3584:4096], w[:, 4352:4864], w[:, 4096:4352]], axis=1),
                w_g=w[:, 4864:7936])


def _join_w_in(g):
    c = g["w_c"]
    return jnp.concatenate([g["w_a"], g["w_b"], c[:, 0:512], c[:, 1024:1280], c[:, 512:1024], g["w_g"]], axis=1)


def _attn_small(g):
    return (g["at_gain"][0].reshape(ATT_Q_HEADS, ATT_HD).sum(0),
            g["at_gain"][1, 0:128].reshape(ATT_KV_HEADS, ATT_HD).sum(0), g["at_sink"].sum(1))


_SMALL = (("norm_g", 8), ("meta", 32), ("conv_w", 32 * DEPTH), ("conv_b", 8), ("conv_ln_g", 8), ("conv_ln_b", 8),
          ("lb", 8), ("hg_norm_g", 8), ("q_norm_g", 8), ("k_norm_g", 8), ("sinks", 8))


def _small_offsets():
    off, o = {}, 0
    for name, rows in _SMALL:
        off[name] = (o, rows)
        o += rows
    return off, o


def _pack_small(d):
    parts = []
    for name, rows in _SMALL:
        a = d[name]
        parts.append(jnp.pad(a, ((0, rows - a.shape[0]), (0, 512 - a.shape[1]))))
    return jnp.concatenate(parts, axis=0)


def kernel(x, meta_tokens, norm_g, w_in, conv_w, conv_b, conv_ln_g, conv_ln_b, w_conv_out, hg_lower_bounds, hg_norm_g, w_hg_out, q_norm_g, k_norm_g, attn_sinks, w_att_out, w_out, loss_target, m_meta_tokens, m_norm_g, m_w_in, m_conv_w, m_conv_b, m_conv_ln_g, m_conv_ln_b, m_w_conv_out, m_hg_lower_bounds, m_hg_norm_g, m_w_hg_out, m_q_norm_g, m_k_norm_g, m_attn_sinks, m_w_att_out, m_w_out, v_meta_tokens, v_norm_g, v_w_in, v_conv_w, v_conv_b, v_conv_ln_g, v_conv_ln_b, v_w_conv_out, v_hg_lower_bounds, v_hg_norm_g, v_w_hg_out, v_q_norm_g, v_k_norm_g, v_attn_sinks, v_w_att_out, v_w_out):
    xi, yi = lax.axis_index("x"), lax.axis_index("y")
    chip = 2 * xi + yi
    NS = w_in.shape[2]
    CS = conv_w.shape[2]
    MS = meta_tokens.shape[1]

    half = D_MODEL // 2
    shards = [[w_in[l, :half].astype(BF), w_in[l, half:].astype(BF), w_conv_out[l].astype(BF), w_hg_out[l].astype(BF),
               w_att_out[l].astype(BF), w_out[l].astype(BF)] for l in range(DEPTH)]
    *first, g_meta, g_convw = _exchange_chips(
        "gather", shards[0] + [meta_tokens, conv_w.reshape(DEPTH * CONV_WIDTH, CS)])
    cols = lambda g: g.transpose(1, 0, 2).reshape(g.shape[1], -1)
    meta_f = cols(g_meta)
    convw_f = cols(g_convw).reshape(DEPTH, CONV_WIDTH, D_CONV)
    lb_all = _lb_fwd(hg_lower_bounds)

    def prep(l, gathered):
        g_win_top, g_win_bot, g_wao, g_wbo, g_wco, g_wout = gathered
        p = _split_w_in(jnp.concatenate([cols(g_win_top), cols(g_win_bot)], axis=0))
        p.update(w_ao=cols(g_wao), w_bo=cols(g_wbo), w_co=cols(g_wco), w_out=g_wout.reshape(D_MODEL, D_MODEL),
                 norm_g=norm_g[l:l + 1], conv_w=convw_f[l],
                 conv_vec=_pad8(jnp.stack([conv_b[l], conv_ln_g[l], conv_ln_b[l]])),
                 lb=lb_all[l:l + 1], gn4=jnp.tile(hg_norm_g[l:l + 1], (1, HG_HEADS)),
                 qg=jnp.tile(q_norm_g[l:l + 1], (1, ATT_Q_HEADS)), kg=jnp.tile(k_norm_g[l:l + 1], (1, ATT_KV_HEADS)),
                 sinks=attn_sinks[l:l + 1])
        return p

    shard_cols = lambda a: a.reshape(a.shape[0], 4, -1).transpose(1, 0, 2)
    def pack(g):
        win = shard_cols(_join_w_in(g))
        return [win[:, :half]], [win[:, half:], shard_cols(g["w_ao"]), shard_cols(g["w_bo"]), shard_cols(g["w_co"]),
                                 g["w_out"].reshape(4, MS, D_MODEL)]

    loss8, dh0, grads, parts = _local_step(x[0], loss_target[0], meta_f, [prep(0, first)], shards, prep, pack)
    seq = x.shape[1]
    grad_x = dh0[CHUNK:CHUNK + seq][None]
    loss = lax.psum(loss8[0, 0], ("x", "y", "c"))

    sum4 = functools.partial(_sum4, name="sum_chips")
    mine = [jnp.concatenate([t for l in range(DEPTH) for t in (sum4(parts[l][0][0]), sum4(parts[l][1][0]))], axis=0)]
    mine += [jnp.concatenate([sum4(parts[l][1][a]) for l in range(DEPTH)], axis=0) for a in range(1, 5)]
    theirs = _swap_cores(mine)

    dlb_all = jnp.concatenate([grads[l]["hg_small"][0:1] for l in range(DEPTH)], axis=0)
    small = dict(
        norm_g=jnp.concatenate([grads[l]["norm_g"][0:1] for l in range(DEPTH)], axis=0).reshape(8, 512),
        meta=dh0[META_PAD:CHUNK].reshape(32, 512),
        conv_w=jnp.concatenate([grads[l]["conv_w"] for l in range(DEPTH)], axis=0),
        conv_b=jnp.concatenate([grads[l]["conv_vec"][0:1] for l in range(DEPTH)], axis=0),
        conv_ln_g=jnp.concatenate([grads[l]["conv_vec"][1:2] for l in range(DEPTH)], axis=0),
        conv_ln_b=jnp.concatenate([grads[l]["conv_vec"][2:3] for l in range(DEPTH)], axis=0),
        lb=_lb_bwd(hg_lower_bounds, dlb_all),
        hg_norm_g=jnp.concatenate([grads[l]["hg_small"][1:2].reshape(HG_HEADS, HG_D).sum(0, keepdims=True)
                                   for l in range(DEPTH)], axis=0),
        q_norm_g=jnp.stack([_attn_small(grads[l])[0] for l in range(DEPTH)]),
        k_norm_g=jnp.stack([_attn_small(grads[l])[1] for l in range(DEPTH)]),
        sinks=jnp.stack([_attn_small(grads[l])[2] for l in range(DEPTH)]),
    )
    gsum = _allsum_small(_pack_small(small))
    off, _ = _small_offsets()

    def take(name, rows, cols):
        o, _ = off[name]
        return gsum[o:o + rows, 0:cols]

    g_meta_full = take("meta", 32, 512).reshape(N_META, D_MODEL)
    g_convw_full = take("conv_w", 32 * DEPTH, 512).reshape(DEPTH, 32, 512)[:, :CONV_WIDTH]
    small_grads = dict(
        norm_g=take("norm_g", 8, 512),
        meta=lax.dynamic_slice_in_dim(g_meta_full, chip * MS, MS, axis=1),
        conv_w=lax.dynamic_slice_in_dim(g_convw_full, chip * CS, CS, axis=2).reshape(DEPTH * CONV_WIDTH, CS),
        conv_b=take("conv_b", DEPTH, 512), conv_ln_g=take("conv_ln_g", DEPTH, 512), conv_ln_b=take("conv_ln_b", DEPTH, 512),
        lb=take("lb", DEPTH, 512), hg_norm_g=take("hg_norm_g", DEPTH, HG_D), q_norm_g=take("q_norm_g", DEPTH, ATT_HD),
        k_norm_g=take("k_norm_g", DEPTH, ATT_HD), sinks=take("sinks", DEPTH, ATT_Q_HEADS))

    def big_update(w, m, v, a, b, name):
        shp = w.shape
        r2 = lambda t: t.reshape(-1, shp[-1])
        outs = _adamw(r2(w), r2(m), r2(v), a, b, name)
        return [o.reshape(shp) for o in outs]

    res = {}
    res["w_in"] = big_update(w_in, m_w_in, v_w_in, mine[0], theirs[0], "adamw_w_in")
    res["w_conv_out"] = big_update(w_conv_out, m_w_conv_out, v_w_conv_out, mine[1], theirs[1], "adamw_w_ao")
    res["w_hg_out"] = big_update(w_hg_out, m_w_hg_out, v_w_hg_out, mine[2], theirs[2], "adamw_w_bo")
    res["w_att_out"] = big_update(w_att_out, m_w_att_out, v_w_att_out, mine[3], theirs[3], "adamw_w_co")
    res["w_out"] = big_update(w_out, m_w_out, v_w_out, mine[4], theirs[4], "adamw_w_out")

    small_w = dict(meta=(meta_tokens, m_meta_tokens, v_meta_tokens), norm_g=(norm_g, m_norm_g, v_norm_g),
                   conv_w=(conv_w, m_conv_w, v_conv_w), conv_b=(conv_b, m_conv_b, v_conv_b),
                   conv_ln_g=(conv_ln_g, m_conv_ln_g, v_conv_ln_g), conv_ln_b=(conv_ln_b, m_conv_ln_b, v_conv_ln_b),
                   lb=(hg_lower_bounds, m_hg_lower_bounds, v_hg_lower_bounds),
                   hg_norm_g=(hg_norm_g, m_hg_norm_g, v_hg_norm_g), q_norm_g=(q_norm_g, m_q_norm_g, v_q_norm_g),
                   k_norm_g=(k_norm_g, m_k_norm_g, v_k_norm_g), sinks=(attn_sinks, m_attn_sinks, v_attn_sinks))
    view = lambda n, t: t.reshape(-1, 512) if n == "norm_g" else t.reshape(-1, t.shape[-1])
    pw, pm, pv = (_pack_rows([view(n, small_w[n][k]) for n in small_w]) for k in range(3))
    pg = _pack_rows([small_grads[n] for n in small_w])
    packed = _adamw(pw, pm, pv, pg, None, "adamw_small")
    o = 0
    for n in small_w:
        r, cdim = view(n, small_w[n][0]).shape
        res[n] = [t[o:o + r, 0:cdim].reshape(small_w[n][0].shape) for t in packed]
        o += -(-r // 8) * 8

    order = [("meta", None), ("norm_g", None), ("w_in", None), ("conv_w", None), ("conv_b", None), ("conv_ln_g", None),
             ("conv_ln_b", None), ("w_conv_out", None), ("lb", None), ("hg_norm_g", None), ("w_hg_out", None),
             ("q_norm_g", None), ("k_norm_g", None), ("sinks", None), ("w_att_out", None), ("w_out", None)]
    outs = [loss, grad_x]
    for k in range(4):
        outs += [res[n][k] for n, _ in order]
    return tuple(outs)


def _pack_rows(arrs):
    parts = []
    for a in arrs:
        r = (-a.shape[0]) % 8
        parts.append(jnp.pad(a, ((0, r), (0, 512 - a.shape[1]))))
    return jnp.concatenate(parts, axis=0)
```

```python
import functools

import jax
import jax.numpy as jnp
from jax import lax
from jax.experimental import pallas as pl
from jax.experimental.pallas import tpu as pltpu

F32 = jnp.float32
BF = jnp.bfloat16

D_MODEL = 1024
DEPTH = 4
CHUNK = 64
N_META = 16
META_PAD = CHUNK - N_META
D_CONV = 512
CONV_WIDTH = 31
HG_HEADS = 4
HG_D = 128
ATT_Q_HEADS = 8
ATT_KV_HEADS = 2
ATT_HD = 64
ATT_GROUP = ATT_Q_HEADS // ATT_KV_HEADS
EPS = 1e-6
F_FLOOR = 1e-30
NEG = -1e30

ADAM_LR = 0.001
ADAM_B1 = 0.9
ADAM_B2 = 0.999
ADAM_EPS = 1e-08
ADAM_WD = 0.01
ADAM_STEP = 10

TR = 640
TRM = TR // 2
CONV_RB = 32
QB = 128
HALO = 128
VMEM_LIMIT = 56 * 1024 * 1024

N_G, N_A, N_B, N_C = 3 * D_MODEL, 3 * D_CONV, 4 * 512, 2 * 512 + 2 * 128

MESH = pl.DeviceIdType.MESH


def _cp(sem=None, vmem=VMEM_LIMIT, **kw):
    if sem is None:
        return pltpu.CompilerParams(vmem_limit_bytes=vmem, **kw)
    return pltpu.CompilerParams(dimension_semantics=sem, vmem_limit_bytes=vmem, **kw)


def _nn(a, b):
    return lax.dot_general(a, b, (((1,), (0,)), ((), ())), preferred_element_type=F32)


def _nt(a, b):
    return lax.dot_general(a, b, (((1,), (1,)), ((), ())), preferred_element_type=F32)


def _tn(a, b):
    return lax.dot_general(a, b, (((0,), (0,)), ((), ())), preferred_element_type=F32)


def _sig(x):
    return jax.nn.sigmoid(x)


def _silu(x):
    return x * _sig(x)


def _dsilu(x):
    s = _sig(x)
    return s * (1.0 + x * (1.0 - s))


def _mm_split(t, x):
    hi = x.astype(BF)
    lo = (x - hi.astype(F32)).astype(BF)
    return _nn(t, hi) + _nn(t, lo)


def _chunk_tri(n, upper):
    r = lax.broadcasted_iota(jnp.int32, (n, n), 0)
    c = lax.broadcasted_iota(jnp.int32, (n, n), 1)
    same = jnp.right_shift(r, 6) == jnp.right_shift(c, 6)
    tri = (c >= r) if upper else (c <= r)
    return jnp.where(same & tri, 1.0, 0.0).astype(BF)


def _matmul(a, b, *, ta=False, tb=False, out_dtype, tm, tn, tk, name, col_major_grid=False):
    if ta:
        K, M = a.shape
    else:
        M, K = a.shape
    N = b.shape[0] if tb else b.shape[1]
    assert M % tm == 0 and N % tn == 0 and K % tk == 0, (name, M, N, K, tm, tn, tk)
    nk = K // tk
    if col_major_grid:
        grid = (N // tn, M // tm, nk)
        ij = lambda g0, g1: (g1, g0)
    else:
        grid = (M // tm, N // tn, nk)
        ij = lambda g0, g1: (g0, g1)
    if ta:
        a_spec = pl.BlockSpec((tk, tm), lambda g0, g1, k: (k, ij(g0, g1)[0]))
    else:
        a_spec = pl.BlockSpec((tm, tk), lambda g0, g1, k: (ij(g0, g1)[0], k))
    if tb:
        b_spec = pl.BlockSpec((tn, tk), lambda g0, g1, k: (ij(g0, g1)[1], k))
    else:
        b_spec = pl.BlockSpec((tk, tn), lambda g0, g1, k: (k, ij(g0, g1)[1]))
    o_spec = pl.BlockSpec((tm, tn), lambda g0, g1, k: ij(g0, g1))
    dims = (((0 if ta else 1,), (1 if tb else 0,)), ((), ()))
    use_acc = nk > 1 and out_dtype != F32

    def body(a_ref, b_ref, o_ref, *scr):
        k = pl.program_id(2)
        p = lax.dot_general(a_ref[...].astype(BF), b_ref[...].astype(BF), dims, preferred_element_type=F32)
        if nk == 1:
            o_ref[...] = p.astype(out_dtype)
        else:
            acc = scr[0] if use_acc else o_ref

            @pl.when(k == 0)
            def _():
                acc[...] = p

            @pl.when(k > 0)
            def _():
                acc[...] += p

            if use_acc:
                @pl.when(k == nk - 1)
                def _():
                    o_ref[...] = acc[...].astype(out_dtype)

    return pl.pallas_call(
        body, name=name, grid=grid, in_specs=[a_spec, b_spec], out_specs=o_spec,
        out_shape=jax.ShapeDtypeStruct((M, N), out_dtype),
        scratch_shapes=[pltpu.VMEM((tm, tn), F32)] if use_acc else [],
        compiler_params=_cp(("parallel", "parallel", "arbitrary")),
    )(a, b)


def _rms_fwd(h, g):
    Lp = h.shape[0]

    def body(h_ref, g_ref, o_ref):
        x = h_ref[...]
        r = lax.rsqrt(jnp.mean(x * x, axis=-1, keepdims=True) + EPS)
        o_ref[...] = (x * r * g_ref[...]).astype(BF)

    return pl.pallas_call(
        body, name="rms_fwd", grid=(Lp // TR,),
        in_specs=[pl.BlockSpec((TR, D_MODEL), lambda i: (i, 0)), pl.BlockSpec((1, D_MODEL), lambda i: (0, 0))],
        out_specs=pl.BlockSpec((TR, D_MODEL), lambda i: (i, 0)),
        out_shape=jax.ShapeDtypeStruct((Lp, D_MODEL), BF),
        compiler_params=_cp(("parallel",)),
    )(h, g)


def _glu(ua, row):
    a = ua[:, 0:D_CONV].astype(F32)
    gl = ua[:, D_CONV:2 * D_CONV].astype(F32)
    return jnp.where(row >= META_PAD, a * _sig(gl), 0.0)


_SH_ROWS = TR + CHUNK - 8


def _fill_shifts(src, sh):
    for b in range(1, 8):
        sh[b - 1] = src[pl.ds(b, _SH_ROWS), :]


def _shifted(src, sh, start, n):
    b = start % 8
    if b == 0:
        return src[pl.ds(start, n), :]
    return sh[b - 1, pl.ds(start - b, n), :]


def _conv_fwd(ua, cw, cvec, carry=None):
    Lp = ua.shape[0]
    nt = Lp // TR
    hb = TR // CHUNK

    def body(cur_ref, halo_ref, w_ref, v_ref, ya_ref, yc_ref, ext, sh):
        i = pl.program_id(0)
        row = i * TR + lax.broadcasted_iota(jnp.int32, (TR, 1), 0)
        hrow = i * TR - CHUNK + lax.broadcasted_iota(jnp.int32, (CHUNK, 1), 0)
        ext[pl.ds(0, CHUNK), :] = jnp.where(i > 0, _glu(halo_ref[...], hrow), 0.0)
        ext[pl.ds(CHUNK, TR), :] = _glu(cur_ref[...], row)
        _fill_shifts(ext, sh)
        for rb in range(TR // CONV_RB):
            r0 = rb * CONV_RB
            rows = pl.ds(r0, CONV_RB)
            acc = jnp.zeros((CONV_RB, D_CONV), F32)
            for j in range(CONV_WIDTH):
                acc = acc + _shifted(ext, sh, r0 + CHUNK - (CONV_WIDTH - 1) + j, CONV_RB) * w_ref[j:j + 1, :]
            y = acc + v_ref[0:1, :]
            yc_ref[rows, :] = y
            mu = jnp.mean(y, axis=-1, keepdims=True)
            d = y - mu
            var = jnp.mean(d * d, axis=-1, keepdims=True)
            yn = d * lax.rsqrt(var + EPS) * v_ref[1:2, :] + v_ref[2:3, :]
            ya_ref[rows, :] = (_silu(yn) * _silu(cur_ref[rows, 2 * D_CONV:3 * D_CONV].astype(F32))).astype(BF)

    in_specs = [pl.BlockSpec((TR, N_A), lambda i: (i, 0)),
                pl.BlockSpec((CHUNK, N_A), lambda i: (jnp.maximum(i * hb - 1, 0), 0)),
                pl.BlockSpec((CONV_WIDTH, D_CONV), lambda i: (0, 0)),
                pl.BlockSpec((8, D_CONV), lambda i: (0, 0))]
    out_specs = [pl.BlockSpec((TR, D_CONV), lambda i: (i, 0)), pl.BlockSpec((TR, D_CONV), lambda i: (i, 0))]
    out_shape = [jax.ShapeDtypeStruct((Lp, D_CONV), BF), jax.ShapeDtypeStruct((Lp, D_CONV), F32)]
    scratch = [pltpu.VMEM((TR + CHUNK, D_CONV), F32), pltpu.VMEM((7, _SH_ROWS, D_CONV), F32)]
    return _call_carrying(body, "conv_fwd", nt, in_specs, out_specs, out_shape, scratch, (ua, ua, cw, cvec), carry)


def _conv_bwd(ua, yconv, dya, cw, cvec):
    Lp = ua.shape[0]
    nt = Lp // TR
    hb = TR // CHUNK
    nhb = Lp // CHUNK

    def ln_bwd(y, dout, gate, v_ref):
        mu = jnp.mean(y, axis=-1, keepdims=True)
        d = y - mu
        var = jnp.mean(d * d, axis=-1, keepdims=True)
        rstd = lax.rsqrt(var + EPS)
        xhat = d * rstd
        yn = xhat * v_ref[1:2, :] + v_ref[2:3, :]
        dyn = dout * _silu(gate) * _dsilu(yn)
        dxh = dyn * v_ref[1:2, :]
        dyc = rstd * (dxh - jnp.mean(dxh, axis=-1, keepdims=True) - xhat * jnp.mean(dxh * xhat, axis=-1, keepdims=True))
        return dyc, dyn, xhat, yn

    def body(cur_ref, prev_ref, next_ref, yc_ref, ycn_ref, dy_ref, dyn_ref, w_ref, v_ref,
             du_ref, dw_ref, dv_ref, uext, dext, dwacc, ush, dsh):
        i = pl.program_id(0)

        @pl.when(i == 0)
        def _():
            dwacc[...] = jnp.zeros_like(dwacc)
            dv_ref[...] = jnp.zeros_like(dv_ref)

        row = i * TR + lax.broadcasted_iota(jnp.int32, (TR, 1), 0)
        hrow = i * TR - CHUNK + lax.broadcasted_iota(jnp.int32, (CHUNK, 1), 0)
        uext[pl.ds(0, CHUNK), :] = jnp.where(i > 0, _glu(prev_ref[...], hrow), 0.0)
        uext[pl.ds(CHUNK, TR), :] = _glu(cur_ref[...], row)

        s_b = jnp.zeros((1, D_CONV), F32)
        s_g = jnp.zeros((1, D_CONV), F32)
        s_bb = jnp.zeros((1, D_CONV), F32)
        for rb in range(TR // CONV_RB):
            rows = pl.ds(rb * CONV_RB, CONV_RB)
            gate = cur_ref[rows, 2 * D_CONV:3 * D_CONV].astype(F32)
            dout = dy_ref[rows, :].astype(F32)
            dyc, dyn, xhat, yn = ln_bwd(yc_ref[rows, :], dout, gate, v_ref)
            du_ref[rows, 2 * D_CONV:3 * D_CONV] = (dout * _silu(yn) * _dsilu(gate)).astype(BF)
            dext[rows, :] = dyc
            s_b = s_b + jnp.sum(dyc, axis=0, keepdims=True)
            s_g = s_g + jnp.sum(dyn * xhat, axis=0, keepdims=True)
            s_bb = s_bb + jnp.sum(dyn, axis=0, keepdims=True)
        dv_ref[0:1, :] += s_b
        dv_ref[1:2, :] += s_g
        dv_ref[2:3, :] += s_bb
        dyc_n, _, _, _ = ln_bwd(ycn_ref[...], dyn_ref[...].astype(F32),
                                next_ref[:, 2 * D_CONV:3 * D_CONV].astype(F32), v_ref)
        dext[pl.ds(TR, CHUNK), :] = jnp.where(i < nt - 1, dyc_n, 0.0)
        _fill_shifts(uext, ush)
        _fill_shifts(dext, dsh)

        for rb in range(TR // CONV_RB):
            r0 = rb * CONV_RB
            rows = pl.ds(r0, CONV_RB)
            d_blk = dext[rows, :]
            dglu = jnp.zeros((CONV_RB, D_CONV), F32)
            for j in range(CONV_WIDTH):
                dglu = dglu + _shifted(dext, dsh, r0 + CONV_WIDTH - 1 - j, CONV_RB) * w_ref[j:j + 1, :]
                prod = d_blk * _shifted(uext, ush, r0 + CHUNK - (CONV_WIDTH - 1) + j, CONV_RB)
                part = prod[0:8, :]
                for s in range(1, CONV_RB // 8):
                    part = part + prod[8 * s:8 * s + 8, :]
                dwacc[j] += part
            a = cur_ref[rows, 0:D_CONV].astype(F32)
            sg = _sig(cur_ref[rows, D_CONV:2 * D_CONV].astype(F32))
            grow = i * TR + r0 + lax.broadcasted_iota(jnp.int32, (CONV_RB, 1), 0)
            dglu = jnp.where(grow >= META_PAD, dglu, 0.0)
            du_ref[rows, 0:D_CONV] = (dglu * sg).astype(BF)
            du_ref[rows, D_CONV:2 * D_CONV] = (dglu * a * sg * (1.0 - sg)).astype(BF)

        @pl.when(i == nt - 1)
        def _():
            dw_ref[...] = jnp.sum(dwacc[...], axis=1)

    nxt = lambda i: (jnp.minimum(i * hb + hb, nhb - 1), 0)
    return pl.pallas_call(
        body, name="conv_bwd", grid=(nt,),
        in_specs=[pl.BlockSpec((TR, N_A), lambda i: (i, 0)),
                  pl.BlockSpec((CHUNK, N_A), lambda i: (jnp.maximum(i * hb - 1, 0), 0)),
                  pl.BlockSpec((CHUNK, N_A), nxt),
                  pl.BlockSpec((TR, D_CONV), lambda i: (i, 0)),
                  pl.BlockSpec((CHUNK, D_CONV), nxt),
                  pl.BlockSpec((TR, D_CONV), lambda i: (i, 0)),
                  pl.BlockSpec((CHUNK, D_CONV), nxt),
                  pl.BlockSpec((CONV_WIDTH, D_CONV), lambda i: (0, 0)),
                  pl.BlockSpec((8, D_CONV), lambda i: (0, 0))],
        out_specs=[pl.BlockSpec((TR, N_A), lambda i: (i, 0)),
                   pl.BlockSpec((32, D_CONV), lambda i: (0, 0)),
                   pl.BlockSpec((8, D_CONV), lambda i: (0, 0))],
        out_shape=[jax.ShapeDtypeStruct((Lp, N_A), BF), jax.ShapeDtypeStruct((32, D_CONV), F32),
                   jax.ShapeDtypeStruct((8, D_CONV), F32)],
        scratch_shapes=[pltpu.VMEM((TR + CHUNK, D_CONV), F32), pltpu.VMEM((TR + CHUNK, D_CONV), F32),
                        pltpu.VMEM((32, 8, D_CONV), F32), pltpu.VMEM((7, _SH_ROWS, D_CONV), F32),
                        pltpu.VMEM((7, _SH_ROWS, D_CONV), F32)],
        compiler_params=_cp(("arbitrary",)),
    )(ua, ua, ua, yconv, yconv, dya, dya, cw, cvec)


def _hg_gates(ub_ref, lbv, row):
    q = ub_ref[:, 0:512].astype(F32)
    z = ub_ref[:, 512:1024].astype(F32)
    valid = row >= META_PAD
    sig = _sig(z)
    f = lbv + (1.0 - lbv) * sig
    g = jnp.where(valid, jnp.log(jnp.maximum(f, F_FLOOR)), 0.0)
    k = jnp.where(valid, (1.0 - lbv) * (1.0 - sig), 0.0)
    return q, k, g, sig, f


def _hg_chunk_terms(b_c, q_c, k_c):
    bm = b_c[CHUNK // 2 - 1:CHUNK // 2, :]
    bl = b_c[CHUNK - 1:CHUNK, :]
    e1 = jnp.exp(b_c - bm)
    e2 = jnp.exp(bm - b_c)
    e0 = jnp.exp(b_c)
    e3 = jnp.exp(bl - b_c)
    el = jnp.exp(bl)
    return e1, e2, e0, e3, el, q_c * e1, k_c * e2, q_c * e0, k_c * e3


def _hg_fwd(ub, lb, gn4, carry=None):
    Lp = ub.shape[0]
    nt = Lp // TR
    cpt = TR // CHUNK

    def body(ub_ref, lb_ref, gn_ref, yb_ref, o_ref, ss_ref, st, bsc, qsc, ksc, qes, els, ust, tlo):
        i = pl.program_id(0)

        @pl.when(i == 0)
        def _():
            st[...] = jnp.zeros_like(st)
            tlo[...] = _chunk_tri(TR, False)

        row = i * TR + lax.broadcasted_iota(jnp.int32, (TR, 1), 0)
        q, k, g, _, _ = _hg_gates(ub_ref, lb_ref[...], row)
        qsc[...] = _silu(q)
        ksc[...] = k
        bsc[...] = _mm_split(tlo[...], g)
        tri = lax.broadcasted_iota(jnp.int32, (CHUNK, CHUNK), 1) <= lax.broadcasted_iota(jnp.int32, (CHUNK, CHUNK), 0)

        def intra(c, carry):
            rows = pl.ds(pl.multiple_of(c * CHUNK, CHUNK), CHUNK)
            _, _, _, _, el, qe, ke, qE, kd = _hg_chunk_terms(bsc[rows, :], qsc[rows, :], ksc[rows, :])
            qe, ke, kd = qe.astype(BF), ke.astype(BF), kd.astype(BF)
            qes[rows, :] = qE.astype(BF)
            els[c] = jnp.broadcast_to(el, (8, 512))
            sls = [slice(HG_D * h, HG_D * (h + 1)) for h in range(HG_HEADS)]
            v = [ub_ref[rows, 1024 + HG_D * h:1024 + HG_D * (h + 1)] for h in range(HG_HEADS)]
            a = [_nt(qe[:, sl], ke[:, sl]) for sl in sls]
            u = [_tn(v[h], kd[:, sls[h]]) for h in range(HG_HEADS)]
            a = [jnp.where(tri, x, 0.0).astype(BF) for x in a]
            oi = [_nn(a[h], v[h]) for h in range(HG_HEADS)]
            for h in range(HG_HEADS):
                ust[c, h] = u[h]
                o_ref[rows, sls[h]] = oi[h]
            return carry

        lax.fori_loop(0, cpt, intra, 0, unroll=2)

        for h in range(HG_HEADS):
            sl = slice(HG_D * h, HG_D * (h + 1))
            s = st[h]
            for c in range(cpt):
                ss_ref[c, h] = s
                s = els[c, 0:1, sl] * s + ust[c, h]
            st[h] = s

        def inter(c, carry):
            rows = pl.ds(pl.multiple_of(c * CHUNK, CHUNK), CHUNK)
            for h in range(HG_HEADS):
                sl = slice(HG_D * h, HG_D * (h + 1))
                o_ref[rows, sl] += _nt(qes[rows, sl], ss_ref[c, h].astype(BF))
            return carry

        lax.fori_loop(0, cpt, inter, 0, unroll=2)

        gate = ub_ref[:, 1536:2048].astype(F32)
        for h in range(HG_HEADS):
            sl = slice(HG_D * h, HG_D * (h + 1))
            o = o_ref[:, sl]
            r = lax.rsqrt(jnp.mean(o * o, axis=-1, keepdims=True) + EPS)
            yb_ref[:, sl] = (o * r * gn_ref[:, sl] * _silu(gate[:, sl])).astype(BF)

    in_specs = [pl.BlockSpec((TR, N_B), lambda i: (i, 0)), pl.BlockSpec((1, 512), lambda i: (0, 0)),
                pl.BlockSpec((1, 512), lambda i: (0, 0))]
    out_specs = [pl.BlockSpec((TR, 512), lambda i: (i, 0)), pl.BlockSpec((TR, 512), lambda i: (i, 0)),
                 pl.BlockSpec((cpt, HG_HEADS, HG_D, HG_D), lambda i: (i, 0, 0, 0))]
    out_shape = [jax.ShapeDtypeStruct((Lp, 512), BF), jax.ShapeDtypeStruct((Lp, 512), F32),
                 jax.ShapeDtypeStruct((Lp // CHUNK, HG_HEADS, HG_D, HG_D), F32)]
    scratch = [pltpu.VMEM((HG_HEADS, HG_D, HG_D), F32), pltpu.VMEM((TR, 512), F32),
               pltpu.VMEM((TR, 512), F32), pltpu.VMEM((TR, 512), F32), pltpu.VMEM((TR, 512), BF),
               pltpu.VMEM((cpt, 8, 512), F32), pltpu.VMEM((cpt, HG_HEADS, HG_D, HG_D), F32), pltpu.VMEM((TR, TR), BF)]
    return _call_carrying(body, "hgrn_fwd", nt, in_specs, out_specs, out_shape, scratch, (ub, lb, gn4), carry)


def _hg_bwd(ub, lb, gn4, o_save, s_save, dyb, carry=None):
    Lp = ub.shape[0]
    nt = Lp // TR
    cpt = TR // CHUNK

    def body(ub_ref, lb_ref, gn_ref, o_ref, ss_ref, dy_ref, du_ref, ds_ref,
             dst, bsc, qsc, ksc, dosc, dqsc, dksc, dbsc, els, ust, dss, tlo, tup):
        i = pl.program_id(0)
        t = nt - 1 - i

        @pl.when(i == 0)
        def _():
            dst[...] = jnp.zeros_like(dst)
            ds_ref[...] = jnp.zeros_like(ds_ref)
            tlo[...] = _chunk_tri(TR, False)
            tup[...] = _chunk_tri(TR, True)

        lbv = lb_ref[...]
        row = t * TR + lax.broadcasted_iota(jnp.int32, (TR, 1), 0)
        valid = row >= META_PAD
        q, k, g, sig, f = _hg_gates(ub_ref, lbv, row)
        qsc[...] = _silu(q)
        ksc[...] = k
        bsc[...] = _mm_split(tlo[...], g)

        gate = ub_ref[:, 1536:2048].astype(F32)
        dy = dy_ref[...].astype(F32)
        dgn = jnp.zeros((1, 512), F32)
        for h in range(HG_HEADS):
            sl = slice(HG_D * h, HG_D * (h + 1))
            o = o_ref[:, sl]
            r = lax.rsqrt(jnp.mean(o * o, axis=-1, keepdims=True) + EPS)
            ohat = o * r
            don = dy[:, sl] * _silu(gate[:, sl])
            du_ref[:, 1536 + HG_D * h:1536 + HG_D * (h + 1)] = (
                dy[:, sl] * ohat * gn_ref[:, sl] * _dsilu(gate[:, sl])).astype(BF)
            ds_ref[1:2, sl] += jnp.sum(don * ohat, axis=0, keepdims=True)
            gd = don * gn_ref[:, sl]
            dosc[:, sl] = r * (gd - ohat * jnp.mean(gd * ohat, axis=-1, keepdims=True))

        tri = lax.broadcasted_iota(jnp.int32, (CHUNK, CHUNK), 1) <= lax.broadcasted_iota(jnp.int32, (CHUNK, CHUNK), 0)
        last = lax.broadcasted_iota(jnp.int32, (CHUNK, 1), 0) == CHUNK - 1

        def incr(c, carry):
            rows = pl.ds(pl.multiple_of(c * CHUNK, CHUNK), CHUNK)
            b_c = bsc[rows, :]
            qE_b = (qsc[rows, :] * jnp.exp(b_c)).astype(BF)
            els[c] = jnp.broadcast_to(jnp.exp(b_c[CHUNK - 1:CHUNK, :]), (8, 512))
            do_c = dosc[rows, :].astype(BF)
            for h in range(HG_HEADS):
                sl = slice(HG_D * h, HG_D * (h + 1))
                ust[c, h] = _tn(do_c[:, sl], qE_b[:, sl])
            return carry

        lax.fori_loop(0, cpt, incr, 0, unroll=2)

        for h in range(HG_HEADS):
            sl = slice(HG_D * h, HG_D * (h + 1))
            d_s = dst[h]
            for c in reversed(range(cpt)):
                dss[c, h] = d_s
                d_s = els[c, 0:1, sl] * d_s + ust[c, h]
            dst[h] = d_s

        def chunk(c, carry):
            r0 = pl.multiple_of(c * CHUNK, CHUNK)
            rows = pl.ds(r0, CHUNK)
            e1, e2, e0, e3, el, qe, ke, qE, kd = _hg_chunk_terms(bsc[rows, :], qsc[rows, :], ksc[rows, :])
            qe_b, ke_b, kd_b = qe.astype(BF), ke.astype(BF), kd.astype(BF)
            do_c = dosc[rows, :].astype(BF)
            hs = range(HG_HEADS)
            sls = [slice(HG_D * h, HG_D * (h + 1)) for h in hs]
            v = [ub_ref[rows, 1024 + HG_D * h:1024 + HG_D * (h + 1)] for h in hs]
            do = [do_c[:, sl] for sl in sls]
            a = [_nt(qe_b[:, sl], ke_b[:, sl]) for sl in sls]
            da = [_nt(do[h], v[h]) for h in hs]
            dqE = [_nn(do[h], ss_ref[c, h].astype(BF)) for h in hs]
            dkd = [_nn(v[h], dss[c, h].astype(BF)) for h in hs]
            dv2 = [_nt(kd_b[:, sls[h]], dss[c, h].astype(BF)) for h in hs]
            a = [jnp.where(tri, x, 0.0).astype(BF) for x in a]
            da = [jnp.where(tri, x, 0.0).astype(BF) for x in da]
            dv = [_tn(a[h], do[h]) + dv2[h] for h in hs]
            dqe = [_nn(da[h], ke_b[:, sls[h]]) for h in hs]
            dke = [_tn(da[h], qe_b[:, sls[h]]) for h in hs]
            for h in hs:
                sl = sls[h]
                del_h = jnp.sum(ss_ref[c, h] * dss[c, h], axis=0, keepdims=True)
                dqsc[rows, sl] = dqE[h] * e0[:, sl] + dqe[h] * e1[:, sl]
                dksc[rows, sl] = dke[h] * e2[:, sl] + dkd[h] * e3[:, sl]
                tkd = dkd[h] * kd[:, sl]
                dbl = jnp.sum(tkd, axis=0, keepdims=True) + del_h * el[:, sl]
                dbsc[rows, sl] = (dqE[h] * qE[:, sl] + dqe[h] * qe[:, sl] - dke[h] * ke[:, sl] - tkd
                                  + jnp.where(last, dbl, 0.0))
                du_ref[rows, 1024 + HG_D * h:1024 + HG_D * (h + 1)] = dv[h].astype(BF)
            return carry

        lax.fori_loop(0, cpt, chunk, 0, unroll=2)

        dg = _mm_split(tup[...], dbsc[...])
        df = jnp.where(valid & (f > F_FLOOR), dg / f, 0.0)
        dk = jnp.where(valid, dksc[...], 0.0)
        dsig = (df - dk) * (1.0 - lbv)
        ds_ref[0:1, :] += jnp.sum((df - dk) * (1.0 - sig), axis=0, keepdims=True)
        du_ref[:, 512:1024] = (dsig * sig * (1.0 - sig)).astype(BF)
        du_ref[:, 0:512] = (dqsc[...] * _dsilu(q)).astype(BF)

    rev = lambda i: (nt - 1 - i, 0)
    in_specs = [pl.BlockSpec((TR, N_B), rev), pl.BlockSpec((1, 512), lambda i: (0, 0)),
                pl.BlockSpec((1, 512), lambda i: (0, 0)), pl.BlockSpec((TR, 512), rev),
                pl.BlockSpec((cpt, HG_HEADS, HG_D, HG_D), lambda i: (nt - 1 - i, 0, 0, 0)),
                pl.BlockSpec((TR, 512), rev)]
    out_specs = [pl.BlockSpec((TR, N_B), rev), pl.BlockSpec((8, 512), lambda i: (0, 0))]
    out_shape = [jax.ShapeDtypeStruct((Lp, N_B), BF), jax.ShapeDtypeStruct((8, 512), F32)]
    states = pltpu.VMEM((cpt, HG_HEADS, HG_D, HG_D), F32)
    scratch = ([pltpu.VMEM((HG_HEADS, HG_D, HG_D), F32)] + [pltpu.VMEM((TR, 512), F32)] * 7
               + [pltpu.VMEM((cpt, 8, 512), F32), states, states, pltpu.VMEM((TR, TR), BF), pltpu.VMEM((TR, TR), BF)])
    return _call_carrying(body, "hgrn_bwd", nt, in_specs, out_specs, out_shape, scratch,
                          (ub, lb, gn4, o_save, s_save, dyb), carry)


_KCOL = (2 * 512) // 128
_VCOL = _KCOL + 1


def _swa_in_specs(nt, rev):
    tile = (lambda i: nt - 1 - i) if rev else (lambda i: i)
    hpt = TR // HALO
    return [
        pl.BlockSpec((TR, 512), lambda i: (tile(i), 0)),
        pl.BlockSpec((TR, 512), lambda i: (tile(i), 1)),
        pl.BlockSpec((TR, 128), lambda i: (tile(i), _KCOL)),
        pl.BlockSpec((TR, 128), lambda i: (tile(i), _VCOL)),
        pl.BlockSpec((HALO, 128), lambda i: (jnp.maximum(tile(i) * hpt - 1, 0), _KCOL)),
        pl.BlockSpec((HALO, 128), lambda i: (jnp.maximum(tile(i) * hpt - 1, 0), _VCOL)),
        pl.BlockSpec((CHUNK, 128), lambda i: (0, _KCOL)),
        pl.BlockSpec((CHUNK, 128), lambda i: (0, _VCOL)),
        pl.BlockSpec((1, 512), lambda i: (0, 0)),
        pl.BlockSpec((1, 128), lambda i: (0, 0)),
        pl.BlockSpec((1, ATT_Q_HEADS), lambda i: (0, 0)),
    ]


_WROWS = 2 * CHUNK + HALO + TR
_W0 = 2 * CHUNK
_C0 = _W0 + HALO
_SCALE = ATT_HD ** -0.5


def _group_ones(n):
    r = lax.broadcasted_iota(jnp.int32, (n, n), 0)
    c = lax.broadcasted_iota(jnp.int32, (n, n), 1)
    return jnp.where(jnp.right_shift(r, 6) == jnp.right_shift(c, 6), 1.0, 0.0).astype(BF)


def _group_mean(x, ones):
    hi = x.astype(BF)
    lo = (x - hi.astype(F32)).astype(BF)
    return (_nn(hi, ones) + _nn(lo, ones)) * (1.0 / ATT_HD)


def _head_rms(x, ones):
    r = lax.rsqrt(_group_mean(x * x, ones) + EPS)
    return x * r, r


def _swa_windows(kc_ref, vc_ref, kh_ref, vh_ref, km_ref, vm_ref, kg2, ones, kwin, krwin, vwin, vrwin):
    meta = pl.ds(META_PAD, N_META)
    for (k, v, r0, n) in ((km_ref[meta, :], vm_ref[meta, :], 0, N_META), (kh_ref[...], vh_ref[...], _W0, HALO),
                          (kc_ref[...], vc_ref[...], _C0, TR)):
        xhat, _ = _head_rms(k.astype(F32), ones)
        kn = xhat * kg2
        kwin[pl.ds(r0, n), :] = kn.astype(BF)
        krwin[pl.ds(r0, n), :] = pltpu.roll(kn, ATT_HD, 1).astype(BF)
        vwin[pl.ds(r0, n), :] = v
        if vrwin is not None:
            vrwin[pl.ds(r0, n), :] = pltpu.roll(v.astype(F32), ATT_HD, 1).astype(BF)
    zero = jnp.zeros((_W0 - N_META, 128), BF)
    for w in (kwin, krwin, vwin, vrwin):
        if w is not None:
            w[pl.ds(N_META, _W0 - N_META), :] = zero


def _swa_masks_t(t, qb):
    q0 = t * TR + qb * QB
    qc = jnp.right_shift(q0 + lax.broadcasted_iota(jnp.int32, (1, QB), 1), 6)
    kabs = q0 - HALO + lax.broadcasted_iota(jnp.int32, (QB + HALO, 1), 0)
    kc = jnp.right_shift(kabs + HALO, 6) - HALO // CHUNK
    mask_w = (kc <= qc) & (kc >= qc - 2) & (kabs >= META_PAD)
    return qc > 2, mask_w


def _swa_park(dtype):
    return [pltpu.VMEM((ATT_Q_HEADS, N_META, QB), dtype), pltpu.VMEM((ATT_Q_HEADS, QB + HALO, QB), dtype)]


def _split_heads(x, lane_hi):
    return jnp.where(lane_hi, 0.0, x).astype(BF), jnp.where(lane_hi, x, 0.0).astype(BF)


def _call_carrying(body, name, nt, in_specs, out_specs, out_shape, scratch, args, carry):
    if carry is None:
        return pl.pallas_call(body, name=name, grid=(nt,), in_specs=in_specs, out_specs=out_specs, out_shape=out_shape,
                              scratch_shapes=scratch, compiler_params=_cp(("arbitrary",)))(*args)
    kind, arrs = carry
    n = len(arrs)
    return pl.pallas_call(
        _carry_exchange(body, len(in_specs), len(out_specs), nt, kind, n), name=name + "_" + kind, grid=(nt,),
        in_specs=in_specs + [_ANY] * n, out_specs=out_specs + [_ANY] * n,
        out_shape=out_shape + _exchange_out_shapes(kind, arrs), scratch_shapes=scratch + _exchange_sems(n),
        compiler_params=_cp(("arbitrary",), has_side_effects=True),
    )(*args, *arrs)


def _swa_fwd(uc, qg8, kg2, sinks, carry=None):
    Lp = uc.shape[0]
    nt = Lp // TR
    nqb = TR // QB

    def body(q_ref, g_ref, kc_ref, vc_ref, kh_ref, vh_ref, km_ref, vm_ref, qg_ref, kg_ref, sk_ref,
             yc_ref, o_ref, lse_ref, kwin, krwin, vwin, vt, qlo, qhi, ot, s_m, s_w, p_m, p_w):
        t = pl.program_id(0)
        _swa_windows(kc_ref, vc_ref, kh_ref, vh_ref, km_ref, vm_ref, kg_ref[...], _group_ones(128),
                     kwin, krwin, vwin, None)
        vt[...] = vwin[...].T
        xhat, _ = _head_rms(q_ref[...].astype(F32), _group_ones(512))
        lane_hi = (lax.broadcasted_iota(jnp.int32, (1, 512), 1) & ATT_HD) != 0
        lo, hi = _split_heads(xhat * qg_ref[...] * _SCALE, lane_hi)
        qlo[...] = lo
        qhi[...] = hi
        for qb in range(nqb):
            rows = pl.ds(qb * QB, QB)
            wrows = pl.ds(_W0 + qb * QB, QB + HALO)
            mrows = pl.ds(0, N_META)
            mask_m, mask_w = _swa_masks_t(t, qb)
            for j in range(ATT_Q_HEADS):
                p, e = j // 2, j % 2
                ks = kwin if e == j // ATT_GROUP else krwin
                qp = (qlo, qhi)[e][rows, 128 * p:128 * (p + 1)]
                s_m[j] = _nt(ks[mrows, :], qp)
                s_w[j] = _nt(ks[wrows, :], qp)
            inv = []
            for j in range(ATT_Q_HEADS):
                sm = jnp.where(mask_m, s_m[j], NEG)
                sw = jnp.where(mask_w, s_w[j], NEG)
                sink = sk_ref[:, j:j + 1]
                m = jnp.maximum(jnp.maximum(jnp.max(sm, axis=0, keepdims=True),
                                            jnp.max(sw, axis=0, keepdims=True)), sink)
                em = jnp.exp(sm - m)
                ew = jnp.exp(sw - m)
                den = jnp.sum(em, axis=0, keepdims=True) + jnp.sum(ew, axis=0, keepdims=True) + jnp.exp(sink - m)
                p_m[j] = em.astype(BF)
                p_w[j] = ew.astype(BF)
                lse_ref[j:j + 1, pl.ds(qb * QB, QB)] = m + jnp.log(den)
                inv.append(1.0 / den)
            for j in range(ATT_Q_HEADS):
                vrows = pl.ds(ATT_HD * (j // ATT_GROUP), ATT_HD)
                ot[pl.ds(ATT_HD * j, ATT_HD), pl.ds(qb * QB, QB)] = (
                    _nn(vt[vrows, pl.ds(0, N_META)], p_m[j])
                    + _nn(vt[vrows, pl.ds(_W0 + qb * QB, QB + HALO)], p_w[j])) * inv[j]
        o = ot[...].T
        o_ref[...] = o
        yc_ref[...] = (o * _silu(g_ref[...].astype(F32))).astype(BF)

    win = pltpu.VMEM((_WROWS, 128), BF)
    in_specs = _swa_in_specs(nt, False)
    out_specs = [pl.BlockSpec((TR, 512), lambda i: (i, 0)), pl.BlockSpec((TR, 512), lambda i: (i, 0)),
                 pl.BlockSpec((ATT_Q_HEADS, TR), lambda i: (0, i))]
    out_shape = [jax.ShapeDtypeStruct((Lp, 512), BF), jax.ShapeDtypeStruct((Lp, 512), F32),
                 jax.ShapeDtypeStruct((ATT_Q_HEADS, Lp), F32)]
    scratch = [win, win, win, pltpu.VMEM((128, _WROWS), BF), pltpu.VMEM((TR, 512), BF),
               pltpu.VMEM((TR, 512), BF), pltpu.VMEM((512, TR), F32)] + _swa_park(F32) + _swa_park(BF)
    return _call_carrying(body, "swa_fwd", nt, in_specs, out_specs, out_shape, scratch,
                          (uc, uc, uc, uc, uc, uc, uc, uc, qg8, kg2, sinks), carry)


def _swa_bwd(uc, qg8, kg2, sinks, o_save, lse, dyc):
    Lp = uc.shape[0]
    nt = Lp // TR
    nqb = TR // QB

    def body(q_ref, g_ref, kc_ref, vc_ref, kh_ref, vh_ref, km_ref, vm_ref, qg_ref, kg_ref, sk_ref,
             o_ref, lse_ref, dy_ref, du_ref, dg_ref, dsk_ref,
             kwin, krwin, vwin, vrwin, kt, krt, qlo, qhi, dolo, dohi, dqt, dk_dir, dk_rol, dv_dir, dv_rol,
             carry_k, carry_v, meta_k, meta_v, s_m, s_w, dp_m, dp_w, p_m, p_w, ds_m, ds_w):
        i = pl.program_id(0)
        t = nt - 1 - i

        @pl.when(i == 0)
        def _():
            carry_k[...] = jnp.zeros_like(carry_k)
            carry_v[...] = jnp.zeros_like(carry_v)
            meta_k[...] = jnp.zeros_like(meta_k)
            meta_v[...] = jnp.zeros_like(meta_v)
            dg_ref[...] = jnp.zeros_like(dg_ref)
            dsk_ref[...] = jnp.zeros_like(dsk_ref)

        ones128 = _group_ones(128)
        ones512 = _group_ones(512)
        _swa_windows(kc_ref, vc_ref, kh_ref, vh_ref, km_ref, vm_ref, kg_ref[...], ones128, kwin, krwin, vwin, vrwin)
        kt[...] = kwin[...].T
        krt[...] = krwin[...].T
        xhat_q, r_q = _head_rms(q_ref[...].astype(F32), ones512)
        lane_hi = (lax.broadcasted_iota(jnp.int32, (1, 512), 1) & ATT_HD) != 0
        lo, hi = _split_heads(xhat_q * qg_ref[...] * _SCALE, lane_hi)
        qlo[...] = lo
        qhi[...] = hi
        gate = g_ref[...].astype(F32)
        dy = dy_ref[...].astype(F32)
        do = dy * _silu(gate)
        o = o_ref[...]
        du_ref[:, 512:1024] = (dy * o * _dsilu(gate)).astype(BF)
        lo, hi = _split_heads(do, lane_hi)
        dolo[...] = lo
        dohi[...] = hi
        hsel = jnp.where(jnp.right_shift(lax.broadcasted_iota(jnp.int32, (ATT_Q_HEADS, 512), 1), 6)
                         == lax.broadcasted_iota(jnp.int32, (ATT_Q_HEADS, 512), 0), 1.0, 0.0).astype(BF)
        prod = do * o
        p_hi = prod.astype(BF)
        d_t = _nt(hsel, p_hi) + _nt(hsel, (prod - p_hi.astype(F32)).astype(BF))
        for acc in (dk_dir, dk_rol, dv_dir, dv_rol):
            acc[...] = jnp.zeros_like(acc)

        for qb in range(nqb):
            rows = pl.ds(qb * QB, QB)
            qcols = pl.ds(qb * QB, QB)
            wrows = pl.ds(_W0 + qb * QB, QB + HALO)
            mrows = pl.ds(0, N_META)
            mask_m, mask_w = _swa_masks_t(t, qb)
            for j in range(ATT_Q_HEADS):
                p, e = j // 2, j % 2
                ks, vs = (kwin, vwin) if e == j // ATT_GROUP else (krwin, vrwin)
                pair = slice(128 * p, 128 * (p + 1))
                qp = (qlo, qhi)[e][rows, pair]
                dop = (dolo, dohi)[e][rows, pair]
                s_m[j] = _nt(ks[mrows, :], qp)
                s_w[j] = _nt(ks[wrows, :], qp)
                dp_m[j] = _nt(vs[mrows, :], dop)
                dp_w[j] = _nt(vs[wrows, :], dop)
            for j in range(ATT_Q_HEADS):
                lse_j = lse_ref[j:j + 1, qcols]
                d_j = d_t[j:j + 1, qb * QB:(qb + 1) * QB]
                em = jnp.exp(jnp.where(mask_m, s_m[j], NEG) - lse_j)
                ew = jnp.exp(jnp.where(mask_w, s_w[j], NEG) - lse_j)
                p_m[j] = em.astype(BF)
                p_w[j] = ew.astype(BF)
                ds_m[j] = (em * (dp_m[j] - d_j)).astype(BF)
                ds_w[j] = (ew * (dp_w[j] - d_j)).astype(BF)
                dsk_ref[j:j + 1, :] -= jnp.exp(sk_ref[:, j:j + 1] - lse_j) * d_j
            for j in range(ATT_Q_HEADS):
                e = j % 2
                ktr = kt if e == j // ATT_GROUP else krt
                hrows = pl.ds(ATT_HD * e, ATT_HD)
                dqt[pl.ds(ATT_HD * j, ATT_HD), qcols] = (_nn(ktr[hrows, pl.ds(0, N_META)], ds_m[j])
                                                         + _nn(ktr[hrows, pl.ds(_W0 + qb * QB, QB + HALO)], ds_w[j]))
            for direct, dk_acc, dv_acc in ((True, dk_dir, dv_dir), (False, dk_rol, dv_rol)):
                heads = [j for j in range(ATT_Q_HEADS) if (j % 2 == j // ATT_GROUP) == direct]
                q_cat = jnp.concatenate([(qlo, qhi)[j % 2][rows, 128 * (j // 2):128 * (j // 2 + 1)] for j in heads], axis=0)
                do_cat = jnp.concatenate([(dolo, dohi)[j % 2][rows, 128 * (j // 2):128 * (j // 2 + 1)] for j in heads], axis=0)
                dk_acc[mrows, :] += _nn(jnp.concatenate([ds_m[j] for j in heads], axis=1), q_cat)
                dk_acc[wrows, :] += _nn(jnp.concatenate([ds_w[j] for j in heads], axis=1), q_cat)
                dv_acc[mrows, :] += _nn(jnp.concatenate([p_m[j] for j in heads], axis=1), do_cat)
                dv_acc[wrows, :] += _nn(jnp.concatenate([p_w[j] for j in heads], axis=1), do_cat)

        dk_dir[...] += pltpu.roll(dk_rol[...], ATT_HD, 1)
        dv_dir[...] += pltpu.roll(dv_rol[...], ATT_HD, 1)
        meta_k[...] += dk_dir[pl.ds(0, N_META), :]
        meta_v[...] += dv_dir[pl.ds(0, N_META), :]
        first = jnp.where(t == 0, 1.0, 0.0)
        dk_dir[pl.ds(_C0 + TR - HALO, HALO), :] += carry_k[...]
        dv_dir[pl.ds(_C0 + TR - HALO, HALO), :] += carry_v[...]
        dk_dir[pl.ds(_C0 + META_PAD, N_META), :] += first * meta_k[...]
        dv_dir[pl.ds(_C0 + META_PAD, N_META), :] += first * meta_v[...]
        carry_k[...] = dk_dir[pl.ds(_W0, HALO), :]
        carry_v[...] = dv_dir[pl.ds(_W0, HALO), :]

        du_ref[:, 1152:1280] = dv_dir[pl.ds(_C0, TR), :].astype(BF)
        xhat_k, r_k = _head_rms(kc_ref[...].astype(F32), ones128)
        dkn = dk_dir[pl.ds(_C0, TR), :]
        dg_ref[1:2, 0:128] += jnp.sum(dkn * xhat_k, axis=0, keepdims=True)
        gd = dkn * kg_ref[...]
        du_ref[:, 1024:1152] = (r_k * (gd - xhat_k * _group_mean(gd * xhat_k, ones128))).astype(BF)
        dqn = dqt[...].T * _SCALE
        dg_ref[0:1, :] += jnp.sum(dqn * xhat_q, axis=0, keepdims=True)
        gd = dqn * qg_ref[...]
        du_ref[:, 0:512] = (r_q * (gd - xhat_q * _group_mean(gd * xhat_q, ones512))).astype(BF)

    rev = lambda i: (nt - 1 - i, 0)
    specs = _swa_in_specs(nt, True)
    win = pltpu.VMEM((_WROWS, 128), BF)
    wint = pltpu.VMEM((128, _WROWS), BF)
    tile_bf = pltpu.VMEM((TR, 512), BF)
    acc = pltpu.VMEM((_WROWS, 128), F32)
    return pl.pallas_call(
        body, name="swa_bwd", grid=(nt,),
        in_specs=specs + [pl.BlockSpec((TR, 512), rev), pl.BlockSpec((ATT_Q_HEADS, TR), lambda i: (0, nt - 1 - i)),
                          pl.BlockSpec((TR, 512), rev)],
        out_specs=[pl.BlockSpec((TR, N_C), rev), pl.BlockSpec((8, 512), lambda i: (0, 0)),
                   pl.BlockSpec((8, 128), lambda i: (0, 0))],
        out_shape=[jax.ShapeDtypeStruct((Lp, N_C), BF), jax.ShapeDtypeStruct((8, 512), F32),
                   jax.ShapeDtypeStruct((8, 128), F32)],
        scratch_shapes=[win, win, win, win, wint, wint, tile_bf, tile_bf, tile_bf, tile_bf,
                        pltpu.VMEM((512, TR), F32), acc, acc, acc, acc,
                        pltpu.VMEM((HALO, 128), F32), pltpu.VMEM((HALO, 128), F32),
                        pltpu.VMEM((N_META, 128), F32), pltpu.VMEM((N_META, 128), F32)]
        + _swa_park(F32) + _swa_park(F32) + _swa_park(BF) + _swa_park(BF),
        compiler_params=_cp(("arbitrary",)),
    )(uc, uc, uc, uc, uc, uc, uc, uc, qg8, kg2, sinks, o_save, lse, dyc)


def _mix_fwd(h, ya, yb, yc, ug, wa, wb, wc, wo):
    Lp = h.shape[0]
    wspec = lambda r: pl.BlockSpec((r, D_MODEL), lambda i: (0, 0))
    yspec = pl.BlockSpec((TRM, 512), lambda i: (i, 0))
    hspec = pl.BlockSpec((TRM, D_MODEL), lambda i: (i, 0))

    def body(h_ref, ya_ref, yb_ref, yc_ref, ug_ref, wa_ref, wb_ref, wc_ref, wo_ref,
             hn_ref, za_ref, zb_ref, zc_ref, mx_ref):
        mixed = jnp.zeros((TRM, D_MODEL), F32)
        for n, (y_ref, w_ref, z_ref) in enumerate(((ya_ref, wa_ref, za_ref), (yb_ref, wb_ref, zb_ref),
                                                   (yc_ref, wc_ref, zc_ref))):
            z = _nn(y_ref[...], w_ref[...])
            z_ref[...] = z.astype(BF)
            mixed = mixed + _sig(ug_ref[:, D_MODEL * n:D_MODEL * (n + 1)].astype(F32)) * z
        mixed = mixed.astype(BF)
        mx_ref[...] = mixed
        hn_ref[...] = h_ref[...] + _nn(mixed, wo_ref[...])

    return pl.pallas_call(
        body, name="mix_fwd", grid=(Lp // TRM,),
        in_specs=[hspec, yspec, yspec, yspec, pl.BlockSpec((TRM, N_G), lambda i: (i, 0)),
                  wspec(512), wspec(512), wspec(512), wspec(D_MODEL)],
        out_specs=[hspec, hspec, hspec, hspec, hspec],
        out_shape=[jax.ShapeDtypeStruct((Lp, D_MODEL), F32)] + [jax.ShapeDtypeStruct((Lp, D_MODEL), BF)] * 4,
        compiler_params=_cp(("parallel",)),
    )(h, ya, yb, yc, ug, wa, wb, wc, wo)


def _mix_bwd(dh, za, zb, zc, ug, wa, wb, wc, wo):
    Lp = dh.shape[0]
    wspec = lambda r: pl.BlockSpec((r, D_MODEL), lambda i: (0, 0))
    yspec = pl.BlockSpec((TRM, 512), lambda i: (i, 0))
    hspec = pl.BlockSpec((TRM, D_MODEL), lambda i: (i, 0))
    gspec = pl.BlockSpec((TRM, N_G), lambda i: (i, 0))

    def body(dh_ref, za_ref, zb_ref, zc_ref, ug_ref, wa_ref, wb_ref, wc_ref, wo_ref,
             dug_ref, dza_ref, dzb_ref, dzc_ref, dya_ref, dyb_ref, dyc_ref):
        dmix = _nt(dh_ref[...].astype(BF), wo_ref[...])
        for n, (z_ref, w_ref, dz_ref, dy_ref) in enumerate(((za_ref, wa_ref, dza_ref, dya_ref),
                                                            (zb_ref, wb_ref, dzb_ref, dyb_ref),
                                                            (zc_ref, wc_ref, dzc_ref, dyc_ref))):
            sl = slice(D_MODEL * n, D_MODEL * (n + 1))
            gt = _sig(ug_ref[:, sl].astype(F32))
            dz = dmix * gt
            dug_ref[:, sl] = (dz * z_ref[...].astype(F32) * (1.0 - gt)).astype(BF)
            dz = dz.astype(BF)
            dz_ref[...] = dz
            dy_ref[...] = _nt(dz, w_ref[...]).astype(BF)

    bf = lambda n: jax.ShapeDtypeStruct((Lp, n), BF)
    return pl.pallas_call(
        body, name="mix_bwd", grid=(Lp // TRM,),
        in_specs=[hspec, hspec, hspec, hspec, gspec, wspec(512), wspec(512), wspec(512), wspec(D_MODEL)],
        out_specs=[gspec, hspec, hspec, hspec, yspec, yspec, yspec],
        out_shape=[bf(N_G), bf(D_MODEL), bf(D_MODEL), bf(D_MODEL), bf(512), bf(512), bf(512)],
        compiler_params=_cp(("parallel",)),
    )(dh, za, zb, zc, ug, wa, wb, wc, wo)


def _inproj_bwd(dus, ws, h, dh, g, carry=None):
    Lp = h.shape[0]
    widths = [w.shape[1] for w in ws]

    def body(dg_ref, da_ref, db_ref, dc_ref, wg_ref, wa_ref, wb_ref, wc_ref, h_ref, dh_ref, g_ref, o_ref, gg_ref):
        @pl.when(pl.program_id(0) == 0)
        def _():
            gg_ref[...] = jnp.zeros_like(gg_ref)

        dhn = (_nt(dg_ref[...], wg_ref[...]) + _nt(da_ref[...], wa_ref[...])
               + _nt(db_ref[...], wb_ref[...]) + _nt(dc_ref[...], wc_ref[...]))
        x = h_ref[...]
        r = lax.rsqrt(jnp.mean(x * x, axis=-1, keepdims=True) + EPS)
        xhat = x * r
        gg_ref[0:1, :] += jnp.sum(dhn * xhat, axis=0, keepdims=True)
        gd = dhn * g_ref[...]
        o_ref[...] = dh_ref[...] + r * (gd - xhat * jnp.mean(gd * xhat, axis=-1, keepdims=True))

    hspec = pl.BlockSpec((TRM, D_MODEL), lambda i: (i, 0))
    in_specs = ([pl.BlockSpec((TRM, n), lambda i: (i, 0)) for n in widths]
                + [pl.BlockSpec((D_MODEL, n), lambda i: (0, 0), pipeline_mode=pl.Buffered(1)) for n in widths]
                + [hspec, hspec, pl.BlockSpec((1, D_MODEL), lambda i: (0, 0))])
    out_specs = [hspec, pl.BlockSpec((8, D_MODEL), lambda i: (0, 0))]
    out_shape = [jax.ShapeDtypeStruct((Lp, D_MODEL), F32), jax.ShapeDtypeStruct((8, D_MODEL), F32)]
    return _call_carrying(body, "inproj_bwd", Lp // TRM, in_specs, out_specs, out_shape, [],
                          (*dus, *ws, h, dh, g), carry)


def _loss_head(h, tgt_pad, seq):
    Lp = h.shape[0]
    nt = Lp // TR

    def body(h_ref, t_ref, dh_ref, l_ref):
        i = pl.program_id(0)

        @pl.when(i == 0)
        def _():
            l_ref[...] = jnp.zeros_like(l_ref)

        row = i * TR + lax.broadcasted_iota(jnp.int32, (TR, 1), 0)
        e = jnp.where((row >= CHUNK) & (row < CHUNK + seq), h_ref[...] - t_ref[...], 0.0)
        dh_ref[...] = e * (1.0 / D_MODEL)
        l_ref[...] += (0.5 / D_MODEL) * jnp.sum(jnp.sum(e * e, axis=0, keepdims=True), axis=1, keepdims=True)

    hspec = pl.BlockSpec((TR, D_MODEL), lambda i: (i, 0))
    return pl.pallas_call(
        body, name="loss_head", grid=(nt,), in_specs=[hspec, hspec],
        out_specs=[hspec, pl.BlockSpec((8, 128), lambda i: (0, 0))],
        out_shape=[jax.ShapeDtypeStruct((Lp, D_MODEL), F32), jax.ShapeDtypeStruct((8, 128), F32)],
        compiler_params=_cp(("arbitrary",)),
    )(h, tgt_pad)


def _lb_softmax(lb_ref):
    x = lb_ref[...]
    e = jnp.exp(x - jnp.max(x, axis=0, keepdims=True))
    return e / jnp.sum(e, axis=0, keepdims=True)


def _lb_fwd(hg_lb):
    def body(lb_ref, o_ref):
        sm = _lb_softmax(lb_ref)
        acc = jnp.zeros((1, 512), F32)
        for l in range(DEPTH):
            if l > 0:
                acc = acc + sm[l:l + 1, :]
            o_ref[l:l + 1, :] = jnp.clip(acc, 0.0, 1.0)

    return pl.pallas_call(body, name="lb_fwd", out_shape=jax.ShapeDtypeStruct((DEPTH, 512), F32))(hg_lb)


def _lb_bwd(hg_lb, dlb_all):
    def body(lb_ref, d_ref, o_ref):
        sm = _lb_softmax(lb_ref)
        acc = jnp.zeros((1, 512), F32)
        gm = []
        for l in range(DEPTH):
            if l > 0:
                acc = acc + sm[l:l + 1, :]
            gm.append(jnp.where((acc >= 0.0) & (acc <= 1.0), d_ref[l:l + 1, :], 0.0))
        dsm = [jnp.zeros((1, 512), F32)]
        for j in range(1, DEPTH):
            s = gm[j]
            for l in range(j + 1, DEPTH):
                s = s + gm[l]
            dsm.append(s)
        dot = dsm[0] * sm[0:1, :]
        for j in range(1, DEPTH):
            dot = dot + dsm[j] * sm[j:j + 1, :]
        for j in range(DEPTH):
            o_ref[j:j + 1, :] = sm[j:j + 1, :] * (dsm[j] - dot)

    return pl.pallas_call(body, name="lb_bwd", out_shape=jax.ShapeDtypeStruct((DEPTH, 512), F32))(hg_lb, dlb_all)


_ANY = pl.BlockSpec(memory_space=pl.ANY)


def _chip_peers():
    x, y, c = lax.axis_index("x"), lax.axis_index("y"), lax.axis_index("c")
    return (x, y, c), [(1 - x, y, c), (x, 1 - y, c), (1 - x, 1 - y, c)]


def _exchange(kind, ins, outs, send, recv, loc):
    (x, y, c), peers = _chip_peers()
    me = 2 * x + y
    ds = []
    for a in range(len(ins)):
        if kind == "gather":
            ds.append(pltpu.make_async_copy(ins[a], outs[a].at[me], loc.at[a]))
        else:
            ds.append(pltpu.make_async_copy(ins[a].at[me], outs[a].at[0], loc.at[a]))
        for p, (px, py, pc) in enumerate(peers):
            src, dst = (ins[a], outs[a].at[me]) if kind == "gather" else (ins[a].at[2 * px + py], outs[a].at[1 + p])
            ds.append(pltpu.make_async_remote_copy(src_ref=src, dst_ref=dst, send_sem=send.at[a, p],
                                                   recv_sem=recv.at[a, p], device_id=(px, py, pc), device_id_type=MESH))
    return ds


def _exchange_out_shapes(kind, arrs):
    if kind == "gather":
        return [jax.ShapeDtypeStruct((4,) + a.shape, a.dtype) for a in arrs]
    return [jax.ShapeDtypeStruct(a.shape, a.dtype) for a in arrs]


def _exchange_sems(n):
    return [pltpu.SemaphoreType.DMA((n, 3)), pltpu.SemaphoreType.DMA((n, 3)), pltpu.SemaphoreType.DMA((n,))]


def _exchange_chips(kind, arrs):
    n = len(arrs)

    def body(*refs):
        ds = _exchange(kind, refs[:n], refs[n:2 * n], *refs[2 * n:])
        for d in ds:
            d.start()
        for d in ds:
            d.wait()

    return pl.pallas_call(
        body, name=kind + "_chips", in_specs=[_ANY] * n, out_specs=[_ANY] * n,
        out_shape=_exchange_out_shapes(kind, arrs), scratch_shapes=_exchange_sems(n),
        compiler_params=pltpu.CompilerParams(has_side_effects=True),
    )(*arrs)


def _carry_exchange(body, n_in, n_out, n_steps, kind, n):
    def wrapped(*refs):
        ins, cin = refs[:n_in], refs[n_in:n_in + n]
        outs, cout = refs[n_in + n:n_in + n + n_out], refs[n_in + n + n_out:n_in + 2 * n + n_out]
        scr, sems = refs[n_in + 2 * n + n_out:-3], refs[-3:]
        i = pl.program_id(0)

        @pl.when(i == 0)
        def _():
            for d in _exchange(kind, cin, cout, *sems):
                d.start()

        body(*ins, *outs, *scr)

        @pl.when(i == n_steps - 1)
        def _():
            for d in _exchange(kind, cin, cout, *sems):
                d.wait()

    return wrapped


def _swap_cores(arrs):
    n = len(arrs)

    def body(*refs):
        ins, outs = refs[:n], refs[n:2 * n]
        send, recv = refs[2 * n:]
        x, y, c = lax.axis_index("x"), lax.axis_index("y"), lax.axis_index("c")
        rdmas = []
        for a in range(n):
            r = pltpu.make_async_remote_copy(src_ref=ins[a], dst_ref=outs[a], send_sem=send.at[a], recv_sem=recv.at[a],
                                             device_id=(x, y, 1 - c), device_id_type=MESH)
            r.start()
            rdmas.append(r)
        for r in rdmas:
            r.wait()

    return pl.pallas_call(
        body, name="swap_cores", in_specs=[_ANY] * n, out_specs=[_ANY] * n,
        out_shape=[jax.ShapeDtypeStruct(a.shape, a.dtype) for a in arrs],
        scratch_shapes=[pltpu.SemaphoreType.DMA((n,)), pltpu.SemaphoreType.DMA((n,))],
        compiler_params=pltpu.CompilerParams(has_side_effects=True),
    )(*arrs)


def _allsum_small(p):
    R = p.shape[0]

    def body(p_ref, o_ref, buf, send, recv):
        x, y, c = lax.axis_index("x"), lax.axis_index("y"), lax.axis_index("c")
        me = 4 * x + 2 * y + c
        buf[me] = p_ref[...]
        rdmas = []
        for k in range(1, 8):
            peer = (x ^ (k >> 2), y ^ ((k >> 1) & 1), c ^ (k & 1))
            r = pltpu.make_async_remote_copy(src_ref=p_ref, dst_ref=buf.at[me], send_sem=send.at[k - 1],
                                             recv_sem=recv.at[k - 1], device_id=peer, device_id_type=MESH)
            r.start()
            rdmas.append(r)
        for r in rdmas:
            r.wait()
        acc = buf[0]
        for d in range(1, 8):
            acc = acc + buf[d]
        o_ref[...] = acc

    return pl.pallas_call(
        body, name="allsum_small", out_shape=jax.ShapeDtypeStruct((R, 512), F32),
        in_specs=[pl.BlockSpec(memory_space=pltpu.VMEM)], out_specs=pl.BlockSpec(memory_space=pltpu.VMEM),
        scratch_shapes=[pltpu.VMEM((8, R, 512), F32), pltpu.SemaphoreType.DMA((7,)), pltpu.SemaphoreType.DMA((7,))],
        compiler_params=_cp(has_side_effects=True),
    )(p)


def _sum4(parts, name):
    _, R, C = parts.shape
    tr = 256 if R % 256 == 0 else R

    def body(p_ref, o_ref):
        p = [p_ref[k].astype(F32) for k in range(4)]
        o_ref[...] = ((p[0] + p[1]) + p[2]) + p[3]

    return pl.pallas_call(
        body, name=name, grid=(R // tr,), in_specs=[pl.BlockSpec((4, tr, C), lambda i: (0, i, 0))],
        out_specs=pl.BlockSpec((tr, C), lambda i: (i, 0)), out_shape=jax.ShapeDtypeStruct((R, C), F32),
        compiler_params=_cp(("parallel",)),
    )(parts)


def _adamw(w, m, v, g0, g1, name):
    R, C = w.shape
    tr = 256 if R % 256 == 0 else R
    two = g1 is not None
    c1 = 1.0 / (1.0 - ADAM_B1 ** ADAM_STEP)
    c2 = 1.0 / (1.0 - ADAM_B2 ** ADAM_STEP)

    def body(*refs):
        if two:
            w_ref, m_ref, v_ref, a_ref, b_ref, g_ref, d_ref, nm_ref, nv_ref = refs
            g = a_ref[...] + b_ref[...]
        else:
            w_ref, m_ref, v_ref, a_ref, g_ref, d_ref, nm_ref, nv_ref = refs
            g = a_ref[...]
        g_ref[...] = g
        m = ADAM_B1 * m_ref[...] + (1.0 - ADAM_B1) * g
        v = ADAM_B2 * v_ref[...] + (1.0 - ADAM_B2) * (g * g)
        nm_ref[...] = m
        nv_ref[...] = v
        d_ref[...] = -ADAM_LR * ((m * c1) / (jnp.sqrt(v * c2) + ADAM_EPS) + ADAM_WD * w_ref[...])

    spec = pl.BlockSpec((tr, C), lambda i: (i, 0))
    n_in = 5 if two else 4
    ins = (w, m, v, g0, g1) if two else (w, m, v, g0)
    return pl.pallas_call(
        body, name=name, grid=(R // tr,), in_specs=[spec] * n_in, out_specs=[spec] * 4,
        out_shape=[jax.ShapeDtypeStruct((R, C), F32)] * 4, compiler_params=_cp(("parallel",)),
    )(*ins)


def _pad8(a):
    r = (-a.shape[0]) % 8
    return a if r == 0 else jnp.pad(a, ((0, r), (0, 0)))


def _local_step(x, tgt, meta, P, shards=None, prep=None, pack=None):
    seq = x.shape[0]
    Lp = -(-(seq + CHUNK) // TR) * TR
    tail = Lp - seq - CHUNK
    h = jnp.concatenate([jnp.zeros((META_PAD, D_MODEL), F32), meta, x, jnp.zeros((tail, D_MODEL), F32)], axis=0)
    tgt_pad = jnp.pad(tgt, ((CHUNK, tail), (0, 0)))

    P = list(P)
    saved = []
    for l in range(DEPTH):
        p = P[l]
        hn = _rms_fwd(h, p["norm_g"])
        mm = functools.partial(_matmul, out_dtype=BF, tm=TR, tk=D_MODEL, col_major_grid=True)
        ug = mm(hn, p["w_g"], tn=N_G // 2, name="inproj_g")
        ua = mm(hn, p["w_a"], tn=N_A, name="inproj_a")
        ub = mm(hn, p["w_b"], tn=N_B, name="inproj_b")
        uc = mm(hn, p["w_c"], tn=N_C, name="inproj_c")
        nxt = shards[l + 1] if shards is not None and l + 1 < DEPTH else None
        carry = (lambda part: ("gather", part)) if nxt is not None else (lambda part: None)
        res_a = _conv_fwd(ua, p["conv_w"], p["conv_vec"], carry(nxt and nxt[1:2]))
        res_b = _hg_fwd(ub, p["lb"], p["gn4"], carry(nxt and nxt[0:1]))
        res_c = _swa_fwd(uc, p["qg"], p["kg"], p["sinks"], carry(nxt and nxt[2:]))
        (ya, yconv), (yb, o_hg, s_hg), (yc, o_at, lse) = res_a[:2], res_b[:3], res_c[:3]
        if nxt is not None:
            P.append(prep(l + 1, [*res_b[3:], *res_a[2:], *res_c[3:]]))
        h_new, za, zb, zc, mixed = _mix_fwd(h, ya, yb, yc, ug, p["w_ao"], p["w_bo"], p["w_co"], p["w_out"])
        saved.append(dict(h=h, hn=hn, ug=ug, ua=ua, ub=ub, uc=uc, ya=ya, yconv=yconv, yb=yb, o_hg=o_hg, s_hg=s_hg,
                          yc=yc, o_at=o_at, lse=lse, za=za, zb=zb, zc=zc, mixed=mixed))
        h = h_new

    dh, loss8 = _loss_head(h, tgt_pad, seq)

    grads = [None] * DEPTH
    parts = [[None, None] for _ in range(DEPTH)]
    pending = None
    tk_dw = 2 * TR if Lp % (2 * TR) == 0 else TR
    for l in reversed(range(DEPTH)):
        p, s = P[l], saved[l]
        dug, dza, dzb, dzc, dya, dyb, dyc = _mix_bwd(dh, s["za"], s["zb"], s["zc"], s["ug"],
                                                      p["w_ao"], p["w_bo"], p["w_co"], p["w_out"])
        tnmm = functools.partial(_matmul, ta=True, out_dtype=F32, tk=tk_dw)
        g = {}
        g["w_out"] = tnmm(s["mixed"], dh, tm=D_MODEL, tn=D_MODEL, name="dw_out")
        g["w_ao"] = tnmm(s["ya"], dza, tm=512, tn=D_MODEL, name="dw_ao")
        g["w_bo"] = tnmm(s["yb"], dzb, tm=512, tn=D_MODEL, name="dw_bo")
        g["w_co"] = tnmm(s["yc"], dzc, tm=512, tn=D_MODEL, name="dw_co")
        dua, g["conv_w"], g["conv_vec"] = _conv_bwd(s["ua"], s["yconv"], dya, p["conv_w"], p["conv_vec"])
        carry = ("scatter", pending[1]) if pending is not None else None
        res = _hg_bwd(s["ub"], p["lb"], p["gn4"], s["o_hg"], s["s_hg"], dyb, carry)
        dub, g["hg_small"] = res[:2]
        if carry is not None:
            parts[pending[0]][1] = res[2:]
        duc, g["at_gain"], g["at_sink"] = _swa_bwd(s["uc"], p["qg"], p["kg"], p["sinks"], s["o_at"], s["lse"], dyc)
        g["w_g"] = tnmm(s["hn"], dug, tm=D_MODEL, tn=N_G // 2, name="dw_in_g")
        g["w_a"] = tnmm(s["hn"], dua, tm=D_MODEL, tn=N_A, name="dw_in_a")
        g["w_b"] = tnmm(s["hn"], dub, tm=D_MODEL, tn=N_B, name="dw_in_b")
        g["w_c"] = tnmm(s["hn"], duc, tm=D_MODEL, tn=N_C, name="dw_in_c")
        first, second = pack(g) if pack is not None else (None, None)
        if first is not None and l == 0:
            first, second = first + second, []
        res = _inproj_bwd([dug, dua, dub, duc], [p["w_g"], p["w_a"], p["w_b"], p["w_c"]], s["h"], dh, p["norm_g"],
                          ("scatter", first) if first is not None else None)
        dh, g["norm_g"] = res[:2]
        grads[l] = g
        if pack is not None:
            parts[l] = [res[2:3], res[3:]] if l == 0 else [res[2:], None]
            pending = (l, second) if l > 0 else None
    return loss8, dh, grads, parts


def _split_w_in(w):
    return dict(w_a=w[:, 0:1536], w_b=w[:, 1536:3584],
                w_c=jnp.concatenate([w[:, 3584:4096], w[:, 4352:4864], w[:, 4096:4352]], axis=1),
                w_g=w[:, 4864:7936])


def _join_w_in(g):
    c = g["w_c"]
    return jnp.concatenate([g["w_a"], g["w_b"], c[:, 0:512], c[:, 1024:1280], c[:, 512:1024], g["w_g"]], axis=1)


def _attn_small(g):
    return (g["at_gain"][0].reshape(ATT_Q_HEADS, ATT_HD).sum(0),
            g["at_gain"][1, 0:128].reshape(ATT_KV_HEADS, ATT_HD).sum(0), g["at_sink"].sum(1))


_SMALL = (("norm_g", 8), ("meta", 32), ("conv_w", 32 * DEPTH), ("conv_b", 8), ("conv_ln_g", 8), ("conv_ln_b", 8),
          ("lb", 8), ("hg_norm_g", 8), ("q_norm_g", 8), ("k_norm_g", 8), ("sinks", 8))


def _small_offsets():
    off, o = {}, 0
    for name, rows in _SMALL:
        off[name] = (o, rows)
        o += rows
    return off, o


def _pack_small(d):
    parts = []
    for name, rows in _SMALL:
        a = d[name]
        parts.append(jnp.pad(a, ((0, rows - a.shape[0]), (0, 512 - a.shape[1]))))
    return jnp.concatenate(parts, axis=0)


def kernel(x, meta_tokens, norm_g, w_in, conv_w, conv_b, conv_ln_g, conv_ln_b, w_conv_out, hg_lower_bounds, hg_norm_g, w_hg_out, q_norm_g, k_norm_g, attn_sinks, w_att_out, w_out, loss_target, m_meta_tokens, m_norm_g, m_w_in, m_conv_w, m_conv_b, m_conv_ln_g, m_conv_ln_b, m_w_conv_out, m_hg_lower_bounds, m_hg_norm_g, m_w_hg_out, m_q_norm_g, m_k_norm_g, m_attn_sinks, m_w_att_out, m_w_out, v_meta_tokens, v_norm_g, v_w_in, v_conv_w, v_conv_b, v_conv_ln_g, v_conv_ln_b, v_w_conv_out, v_hg_lower_bounds, v_hg_norm_g, v_w_hg_out, v_q_norm_g, v_k_norm_g, v_attn_sinks, v_w_att_out, v_w_out):
    xi, yi = lax.axis_index("x"), lax.axis_index("y")
    chip = 2 * xi + yi
    NS = w_in.shape[2]
    CS = conv_w.shape[2]
    MS = meta_tokens.shape[1]

    half = D_MODEL // 2
    shards = [[w_in[l, :half].astype(BF), w_in[l, half:].astype(BF), w_conv_out[l].astype(BF), w_hg_out[l].astype(BF),
               w_att_out[l].astype(BF), w_out[l].astype(BF)] for l in range(DEPTH)]
    *first, g_meta, g_convw = _exchange_chips(
        "gather", shards[0] + [meta_tokens, conv_w.reshape(DEPTH * CONV_WIDTH, CS)])
    cols = lambda g: g.transpose(1, 0, 2).reshape(g.shape[1], -1)
    meta_f = cols(g_meta)
    convw_f = cols(g_convw).reshape(DEPTH, CONV_WIDTH, D_CONV)
    lb_all = _lb_fwd(hg_lower_bounds)

    def prep(l, gathered):
        g_win_top, g_win_bot, g_wao, g_wbo, g_wco, g_wout = gathered
        p = _split_w_in(jnp.concatenate([cols(g_win_top), cols(g_win_bot)], axis=0))
        p.update(w_ao=cols(g_wao), w_bo=cols(g_wbo), w_co=cols(g_wco), w_out=g_wout.reshape(D_MODEL, D_MODEL),
                 norm_g=norm_g[l:l + 1], conv_w=convw_f[l],
                 conv_vec=_pad8(jnp.stack([conv_b[l], conv_ln_g[l], conv_ln_b[l]])),
                 lb=lb_all[l:l + 1], gn4=jnp.tile(hg_norm_g[l:l + 1], (1, HG_HEADS)),
                 qg=jnp.tile(q_norm_g[l:l + 1], (1, ATT_Q_HEADS)), kg=jnp.tile(k_norm_g[l:l + 1], (1, ATT_KV_HEADS)),
                 sinks=attn_sinks[l:l + 1])
        return p

    shard_cols = lambda a: a.reshape(a.shape[0], 4, -1).transpose(1, 0, 2)
    def pack(g):
        win = shard_cols(_join_w_in(g)).astype(BF)
        return [win[:, :half]], [win[:, half:], shard_cols(g["w_ao"]).astype(BF), shard_cols(g["w_bo"]).astype(BF),
                                 shard_cols(g["w_co"]).astype(BF), g["w_out"].reshape(4, MS, D_MODEL).astype(BF)]

    loss8, dh0, grads, parts = _local_step(x[0], loss_target[0], meta_f, [prep(0, first)], shards, prep, pack)
    seq = x.shape[1]
    grad_x = dh0[CHUNK:CHUNK + seq][None]
    loss = lax.psum(loss8[0, 0], ("x", "y", "c"))

    sum4 = functools.partial(_sum4, name="sum_chips")
    mine = [jnp.concatenate([t for l in range(DEPTH) for t in (sum4(parts[l][0][0]), sum4(parts[l][1][0]))], axis=0)]
    mine += [jnp.concatenate([sum4(parts[l][1][a]) for l in range(DEPTH)], axis=0) for a in range(1, 5)]
    theirs = _swap_cores(mine)

    dlb_all = jnp.concatenate([grads[l]["hg_small"][0:1] for l in range(DEPTH)], axis=0)
    small = dict(
        norm_g=jnp.concatenate([grads[l]["norm_g"][0:1] for l in range(DEPTH)], axis=0).reshape(8, 512),
        meta=dh0[META_PAD:CHUNK].reshape(32, 512),
        conv_w=jnp.concatenate([grads[l]["conv_w"] for l in range(DEPTH)], axis=0),
        conv_b=jnp.concatenate([grads[l]["conv_vec"][0:1] for l in range(DEPTH)], axis=0),
        conv_ln_g=jnp.concatenate([grads[l]["conv_vec"][1:2] for l in range(DEPTH)], axis=0),
        conv_ln_b=jnp.concatenate([grads[l]["conv_vec"][2:3] for l in range(DEPTH)], axis=0),
        lb=_lb_bwd(hg_lower_bounds, dlb_all),
        hg_norm_g=jnp.concatenate([grads[l]["hg_small"][1:2].reshape(HG_HEADS, HG_D).sum(0, keepdims=True)
                                   for l in range(DEPTH)], axis=0),
        q_norm_g=jnp.stack([_attn_small(grads[l])[0] for l in range(DEPTH)]),
        k_norm_g=jnp.stack([_attn_small(grads[l])[1] for l in range(DEPTH)]),
        sinks=jnp.stack([_attn_small(grads[l])[2] for l in range(DEPTH)]),
    )
    gsum = _allsum_small(_pack_small(small))
    off, _ = _small_offsets()

    def take(name, rows, cols):
        o, _ = off[name]
        return gsum[o:o + rows, 0:cols]

    g_meta_full = take("meta", 32, 512).reshape(N_META, D_MODEL)
    g_convw_full = take("conv_w", 32 * DEPTH, 512).reshape(DEPTH, 32, 512)[:, :CONV_WIDTH]
    small_grads = dict(
        norm_g=take("norm_g", 8, 512),
        meta=lax.dynamic_slice_in_dim(g_meta_full, chip * MS, MS, axis=1),
        conv_w=lax.dynamic_slice_in_dim(g_convw_full, chip * CS, CS, axis=2).reshape(DEPTH * CONV_WIDTH, CS),
        conv_b=take("conv_b", DEPTH, 512), conv_ln_g=take("conv_ln_g", DEPTH, 512), conv_ln_b=take("conv_ln_b", DEPTH, 512),
        lb=take("lb", DEPTH, 512), hg_norm_g=take("hg_norm_g", DEPTH, HG_D), q_norm_g=take("q_norm_g", DEPTH, ATT_HD),
        k_norm_g=take("k_norm_g", DEPTH, ATT_HD), sinks=take("sinks", DEPTH, ATT_Q_HEADS))

    def big_update(w, m, v, a, b, name):
        shp = w.shape
        r2 = lambda t: t.reshape(-1, shp[-1])
        outs = _adamw(r2(w), r2(m), r2(v), a, b, name)
        return [o.reshape(shp) for o in outs]

    res = {}
    res["w_in"] = big_update(w_in, m_w_in, v_w_in, mine[0], theirs[0], "adamw_w_in")
    res["w_conv_out"] = big_update(w_conv_out, m_w_conv_out, v_w_conv_out, mine[1], theirs[1], "adamw_w_ao")
    res["w_hg_out"] = big_update(w_hg_out, m_w_hg_out, v_w_hg_out, mine[2], theirs[2], "adamw_w_bo")
    res["w_att_out"] = big_update(w_att_out, m_w_att_out, v_w_att_out, mine[3], theirs[3], "adamw_w_co")
    res["w_out"] = big_update(w_out, m_w_out, v_w_out, mine[4], theirs[4], "adamw_w_out")

    small_w = dict(meta=(meta_tokens, m_meta_tokens, v_meta_tokens), norm_g=(norm_g, m_norm_g, v_norm_g),
                   conv_w=(conv_w, m_conv_w, v_conv_w), conv_b=(conv_b, m_conv_b, v_conv_b),
                   conv_ln_g=(conv_ln_g, m_conv_ln_g, v_conv_ln_g), conv_ln_b=(conv_ln_b, m_conv_ln_b, v_conv_ln_b),
                   lb=(hg_lower_bounds, m_hg_lower_bounds, v_hg_lower_bounds),
                   hg_norm_g=(hg_norm_g, m_hg_norm_g, v_hg_norm_g), q_norm_g=(q_norm_g, m_q_norm_g, v_q_norm_g),
                   k_norm_g=(k_norm_g, m_k_norm_g, v_k_norm_g), sinks=(attn_sinks, m_attn_sinks, v_attn_sinks))
    view = lambda n, t: t.reshape(-1, 512) if n == "norm_g" else t.reshape(-1, t.shape[-1])
    pw, pm, pv = (_pack_rows([view(n, small_w[n][k]) for n in small_w]) for k in range(3))
    pg = _pack_rows([small_grads[n] for n in small_w])
    packed = _adamw(pw, pm, pv, pg, None, "adamw_small")
    o = 0
    for n in small_w:
        r, cdim = view(n, small_w[n][0]).shape
        res[n] = [t[o:o + r, 0:cdim].reshape(small_w[n][0].shape) for t in packed]
        o += -(-r // 8) * 8

    order = [("meta", None), ("norm_g", None), ("w_in", None), ("conv_w", None), ("conv_b", None), ("conv_ln_g", None),
             ("conv_ln_b", None), ("w_conv_out", None), ("lb", None), ("hg_norm_g", None), ("w_hg_out", None),
             ("q_norm_g", None), ("k_norm_g", None), ("sinks", None), ("w_att_out", None), ("w_out", None)]
    outs = [loss, grad_x]
    for k in range(4):
        outs += [res[n][k] for n, _ in order]
    return tuple(outs)


def _pack_rows(arrs):
    parts = []
    for a in arrs:
        r = (-a.shape[0]) % 8
        parts.append(jnp.pad(a, ((0, r), (0, 512 - a.shape[1]))))
    return jnp.concatenate(parts, axis=0)
```

```python
import functools

import jax
import jax.numpy as jnp
from jax import lax
from jax.experimental import pallas as pl
from jax.experimental.pallas import tpu as pltpu

F32 = jnp.float32
BF = jnp.bfloat16

D_MODEL = 1024
DEPTH = 4
CHUNK = 64
N_META = 16
META_PAD = CHUNK - N_META
D_CONV = 512
CONV_WIDTH = 31
HG_HEADS = 4
HG_D = 128
ATT_Q_HEADS = 8
ATT_KV_HEADS = 2
ATT_HD = 64
ATT_GROUP = ATT_Q_HEADS // ATT_KV_HEADS
EPS = 1e-6
F_FLOOR = 1e-30
NEG = -1e30

ADAM_LR = 0.001
ADAM_B1 = 0.9
ADAM_B2 = 0.999
ADAM_EPS = 1e-08
ADAM_WD = 0.01
ADAM_STEP = 10

TR = 640
TRM = TR // 2
CONV_RB = 32
QB = 128
HALO = 128
VMEM_LIMIT = 56 * 1024 * 1024

N_G, N_A, N_B, N_C = 3 * D_MODEL, 3 * D_CONV, 4 * 512, 2 * 512 + 2 * 128

MESH = pl.DeviceIdType.MESH


def _cp(sem=None, vmem=VMEM_LIMIT, **kw):
    if sem is None:
        return pltpu.CompilerParams(vmem_limit_bytes=vmem, **kw)
    return pltpu.CompilerParams(dimension_semantics=sem, vmem_limit_bytes=vmem, **kw)


def _nn(a, b):
    return lax.dot_general(a, b, (((1,), (0,)), ((), ())), preferred_element_type=F32)


def _nt(a, b):
    return lax.dot_general(a, b, (((1,), (1,)), ((), ())), preferred_element_type=F32)


def _tn(a, b):
    return lax.dot_general(a, b, (((0,), (0,)), ((), ())), preferred_element_type=F32)


def _sig(x):
    return jax.nn.sigmoid(x)


def _silu(x):
    return x * _sig(x)


def _dsilu(x):
    s = _sig(x)
    return s * (1.0 + x * (1.0 - s))


def _mm_split(t, x):
    hi = x.astype(BF)
    lo = (x - hi.astype(F32)).astype(BF)
    return _nn(t, hi) + _nn(t, lo)


def _chunk_tri(n, upper):
    r = lax.broadcasted_iota(jnp.int32, (n, n), 0)
    c = lax.broadcasted_iota(jnp.int32, (n, n), 1)
    same = jnp.right_shift(r, 6) == jnp.right_shift(c, 6)
    tri = (c >= r) if upper else (c <= r)
    return jnp.where(same & tri, 1.0, 0.0).astype(BF)


def _matmul(a, b, *, ta=False, tb=False, out_dtype, tm, tn, tk, name, col_major_grid=False):
    if ta:
        K, M = a.shape
    else:
        M, K = a.shape
    N = b.shape[0] if tb else b.shape[1]
    assert M % tm == 0 and N % tn == 0 and K % tk == 0, (name, M, N, K, tm, tn, tk)
    nk = K // tk
    if col_major_grid:
        grid = (N // tn, M // tm, nk)
        ij = lambda g0, g1: (g1, g0)
    else:
        grid = (M // tm, N // tn, nk)
        ij = lambda g0, g1: (g0, g1)
    if ta:
        a_spec = pl.BlockSpec((tk, tm), lambda g0, g1, k: (k, ij(g0, g1)[0]))
    else:
        a_spec = pl.BlockSpec((tm, tk), lambda g0, g1, k: (ij(g0, g1)[0], k))
    if tb:
        b_spec = pl.BlockSpec((tn, tk), lambda g0, g1, k: (ij(g0, g1)[1], k))
    else:
        b_spec = pl.BlockSpec((tk, tn), lambda g0, g1, k: (k, ij(g0, g1)[1]))
    o_spec = pl.BlockSpec((tm, tn), lambda g0, g1, k: ij(g0, g1))
    dims = (((0 if ta else 1,), (1 if tb else 0,)), ((), ()))
    use_acc = nk > 1 and out_dtype != F32

    def body(a_ref, b_ref, o_ref, *scr):
        k = pl.program_id(2)
        p = lax.dot_general(a_ref[...].astype(BF), b_ref[...].astype(BF), dims, preferred_element_type=F32)
        if nk == 1:
            o_ref[...] = p.astype(out_dtype)
        else:
            acc = scr[0] if use_acc else o_ref

            @pl.when(k == 0)
            def _():
                acc[...] = p

            @pl.when(k > 0)
            def _():
                acc[...] += p

            if use_acc:
                @pl.when(k == nk - 1)
                def _():
                    o_ref[...] = acc[...].astype(out_dtype)

    return pl.pallas_call(
        body, name=name, grid=grid, in_specs=[a_spec, b_spec], out_specs=o_spec,
        out_shape=jax.ShapeDtypeStruct((M, N), out_dtype),
        scratch_shapes=[pltpu.VMEM((tm, tn), F32)] if use_acc else [],
        compiler_params=_cp(("parallel", "parallel", "arbitrary")),
    )(a, b)


def _rms_fwd(h, g):
    Lp = h.shape[0]

    def body(h_ref, g_ref, o_ref):
        x = h_ref[...]
        r = lax.rsqrt(jnp.mean(x * x, axis=-1, keepdims=True) + EPS)
        o_ref[...] = (x * r * g_ref[...]).astype(BF)

    return pl.pallas_call(
        body, name="rms_fwd", grid=(Lp // TR,),
        in_specs=[pl.BlockSpec((TR, D_MODEL), lambda i: (i, 0)), pl.BlockSpec((1, D_MODEL), lambda i: (0, 0))],
        out_specs=pl.BlockSpec((TR, D_MODEL), lambda i: (i, 0)),
        out_shape=jax.ShapeDtypeStruct((Lp, D_MODEL), BF),
        compiler_params=_cp(("parallel",)),
    )(h, g)


def _glu(ua, row):
    a = ua[:, 0:D_CONV].astype(F32)
    gl = ua[:, D_CONV:2 * D_CONV].astype(F32)
    return jnp.where(row >= META_PAD, a * _sig(gl), 0.0)


_SH_ROWS = TR + CHUNK - 8


def _fill_shifts(src, sh):
    for b in range(1, 8):
        sh[b - 1] = src[pl.ds(b, _SH_ROWS), :]


def _shifted(src, sh, start, n):
    b = start % 8
    if b == 0:
        return src[pl.ds(start, n), :]
    return sh[b - 1, pl.ds(start - b, n), :]


def _conv_fwd(ua, cw, cvec, carry=None):
    Lp = ua.shape[0]
    nt = Lp // TR
    hb = TR // CHUNK

    def body(cur_ref, halo_ref, w_ref, v_ref, ya_ref, yc_ref, ext, sh):
        i = pl.program_id(0)
        row = i * TR + lax.broadcasted_iota(jnp.int32, (TR, 1), 0)
        hrow = i * TR - CHUNK + lax.broadcasted_iota(jnp.int32, (CHUNK, 1), 0)
        ext[pl.ds(0, CHUNK), :] = jnp.where(i > 0, _glu(halo_ref[...], hrow), 0.0)
        ext[pl.ds(CHUNK, TR), :] = _glu(cur_ref[...], row)
        _fill_shifts(ext, sh)
        for rb in range(TR // CONV_RB):
            r0 = rb * CONV_RB
            rows = pl.ds(r0, CONV_RB)
            acc = jnp.zeros((CONV_RB, D_CONV), F32)
            for j in range(CONV_WIDTH):
                acc = acc + _shifted(ext, sh, r0 + CHUNK - (CONV_WIDTH - 1) + j, CONV_RB) * w_ref[j:j + 1, :]
            y = acc + v_ref[0:1, :]
            yc_ref[rows, :] = y
            mu = jnp.mean(y, axis=-1, keepdims=True)
            d = y - mu
            var = jnp.mean(d * d, axis=-1, keepdims=True)
            yn = d * lax.rsqrt(var + EPS) * v_ref[1:2, :] + v_ref[2:3, :]
            ya_ref[rows, :] = (_silu(yn) * _silu(cur_ref[rows, 2 * D_CONV:3 * D_CONV].astype(F32))).astype(BF)

    in_specs = [pl.BlockSpec((TR, N_A), lambda i: (i, 0)),
                pl.BlockSpec((CHUNK, N_A), lambda i: (jnp.maximum(i * hb - 1, 0), 0)),
                pl.BlockSpec((CONV_WIDTH, D_CONV), lambda i: (0, 0)),
                pl.BlockSpec((8, D_CONV), lambda i: (0, 0))]
    out_specs = [pl.BlockSpec((TR, D_CONV), lambda i: (i, 0)), pl.BlockSpec((TR, D_CONV), lambda i: (i, 0))]
    out_shape = [jax.ShapeDtypeStruct((Lp, D_CONV), BF), jax.ShapeDtypeStruct((Lp, D_CONV), F32)]
    scratch = [pltpu.VMEM((TR + CHUNK, D_CONV), F32), pltpu.VMEM((7, _SH_ROWS, D_CONV), F32)]
    return _call_carrying(body, "conv_fwd", nt, in_specs, out_specs, out_shape, scratch, (ua, ua, cw, cvec), carry)


def _conv_bwd(ua, yconv, dya, cw, cvec):
    Lp = ua.shape[0]
    nt = Lp // TR
    hb = TR // CHUNK
    nhb = Lp // CHUNK

    def ln_bwd(y, dout, gate, v_ref):
        mu = jnp.mean(y, axis=-1, keepdims=True)
        d = y - mu
        var = jnp.mean(d * d, axis=-1, keepdims=True)
        rstd = lax.rsqrt(var + EPS)
        xhat = d * rstd
        yn = xhat * v_ref[1:2, :] + v_ref[2:3, :]
        dyn = dout * _silu(gate) * _dsilu(yn)
        dxh = dyn * v_ref[1:2, :]
        dyc = rstd * (dxh - jnp.mean(dxh, axis=-1, keepdims=True) - xhat * jnp.mean(dxh * xhat, axis=-1, keepdims=True))
        return dyc, dyn, xhat, yn

    def body(cur_ref, prev_ref, next_ref, yc_ref, ycn_ref, dy_ref, dyn_ref, w_ref, v_ref,
             du_ref, dw_ref, dv_ref, uext, dext, dwacc, ush, dsh):
        i = pl.program_id(0)

        @pl.when(i == 0)
        def _():
            dwacc[...] = jnp.zeros_like(dwacc)
            dv_ref[...] = jnp.zeros_like(dv_ref)

        row = i * TR + lax.broadcasted_iota(jnp.int32, (TR, 1), 0)
        hrow = i * TR - CHUNK + lax.broadcasted_iota(jnp.int32, (CHUNK, 1), 0)
        uext[pl.ds(0, CHUNK), :] = jnp.where(i > 0, _glu(prev_ref[...], hrow), 0.0)
        uext[pl.ds(CHUNK, TR), :] = _glu(cur_ref[...], row)

        s_b = jnp.zeros((1, D_CONV), F32)
        s_g = jnp.zeros((1, D_CONV), F32)
        s_bb = jnp.zeros((1, D_CONV), F32)
        for rb in range(TR // CONV_RB):
            rows = pl.ds(rb * CONV_RB, CONV_RB)
            gate = cur_ref[rows, 2 * D_CONV:3 * D_CONV].astype(F32)
            dout = dy_ref[rows, :].astype(F32)
            dyc, dyn, xhat, yn = ln_bwd(yc_ref[rows, :], dout, gate, v_ref)
            du_ref[rows, 2 * D_CONV:3 * D_CONV] = (dout * _silu(yn) * _dsilu(gate)).astype(BF)
            dext[rows, :] = dyc
            s_b = s_b + jnp.sum(dyc, axis=0, keepdims=True)
            s_g = s_g + jnp.sum(dyn * xhat, axis=0, keepdims=True)
            s_bb = s_bb + jnp.sum(dyn, axis=0, keepdims=True)
        dv_ref[0:1, :] += s_b
        dv_ref[1:2, :] += s_g
        dv_ref[2:3, :] += s_bb
        dyc_n, _, _, _ = ln_bwd(ycn_ref[...], dyn_ref[...].astype(F32),
                                next_ref[:, 2 * D_CONV:3 * D_CONV].astype(F32), v_ref)
        dext[pl.ds(TR, CHUNK), :] = jnp.where(i < nt - 1, dyc_n, 0.0)
        _fill_shifts(uext, ush)
        _fill_shifts(dext, dsh)

        for rb in range(TR // CONV_RB):
            r0 = rb * CONV_RB
            rows = pl.ds(r0, CONV_RB)
            d_blk = dext[rows, :]
            dglu = jnp.zeros((CONV_RB, D_CONV), F32)
            for j in range(CONV_WIDTH):
                dglu = dglu + _shifted(dext, dsh, r0 + CONV_WIDTH - 1 - j, CONV_RB) * w_ref[j:j + 1, :]
                prod = d_blk * _shifted(uext, ush, r0 + CHUNK - (CONV_WIDTH - 1) + j, CONV_RB)
                part = prod[0:8, :]
                for s in range(1, CONV_RB // 8):
                    part = part + prod[8 * s:8 * s + 8, :]
                dwacc[j] += part
            a = cur_ref[rows, 0:D_CONV].astype(F32)
            sg = _sig(cur_ref[rows, D_CONV:2 * D_CONV].astype(F32))
            grow = i * TR + r0 + lax.broadcasted_iota(jnp.int32, (CONV_RB, 1), 0)
            dglu = jnp.where(grow >= META_PAD, dglu, 0.0)
            du_ref[rows, 0:D_CONV] = (dglu * sg).astype(BF)
            du_ref[rows, D_CONV:2 * D_CONV] = (dglu * a * sg * (1.0 - sg)).astype(BF)

        @pl.when(i == nt - 1)
        def _():
            dw_ref[...] = jnp.sum(dwacc[...], axis=1)

    nxt = lambda i: (jnp.minimum(i * hb + hb, nhb - 1), 0)
    return pl.pallas_call(
        body, name="conv_bwd", grid=(nt,),
        in_specs=[pl.BlockSpec((TR, N_A), lambda i: (i, 0)),
                  pl.BlockSpec((CHUNK, N_A), lambda i: (jnp.maximum(i * hb - 1, 0), 0)),
                  pl.BlockSpec((CHUNK, N_A), nxt),
                  pl.BlockSpec((TR, D_CONV), lambda i: (i, 0)),
                  pl.BlockSpec((CHUNK, D_CONV), nxt),
                  pl.BlockSpec((TR, D_CONV), lambda i: (i, 0)),
                  pl.BlockSpec((CHUNK, D_CONV), nxt),
                  pl.BlockSpec((CONV_WIDTH, D_CONV), lambda i: (0, 0)),
                  pl.BlockSpec((8, D_CONV), lambda i: (0, 0))],
        out_specs=[pl.BlockSpec((TR, N_A), lambda i: (i, 0)),
                   pl.BlockSpec((32, D_CONV), lambda i: (0, 0)),
                   pl.BlockSpec((8, D_CONV), lambda i: (0, 0))],
        out_shape=[jax.ShapeDtypeStruct((Lp, N_A), BF), jax.ShapeDtypeStruct((32, D_CONV), F32),
                   jax.ShapeDtypeStruct((8, D_CONV), F32)],
        scratch_shapes=[pltpu.VMEM((TR + CHUNK, D_CONV), F32), pltpu.VMEM((TR + CHUNK, D_CONV), F32),
                        pltpu.VMEM((32, 8, D_CONV), F32), pltpu.VMEM((7, _SH_ROWS, D_CONV), F32),
                        pltpu.VMEM((7, _SH_ROWS, D_CONV), F32)],
        compiler_params=_cp(("arbitrary",)),
    )(ua, ua, ua, yconv, yconv, dya, dya, cw, cvec)


def _hg_gates(ub_ref, lbv, row):
    q = ub_ref[:, 0:512].astype(F32)
    z = ub_ref[:, 512:1024].astype(F32)
    valid = row >= META_PAD
    sig = _sig(z)
    f = lbv + (1.0 - lbv) * sig
    g = jnp.where(valid, jnp.log(jnp.maximum(f, F_FLOOR)), 0.0)
    k = jnp.where(valid, (1.0 - lbv) * (1.0 - sig), 0.0)
    return q, k, g, sig, f


def _hg_chunk_terms(b_c, q_c, k_c):
    bm = b_c[CHUNK // 2 - 1:CHUNK // 2, :]
    bl = b_c[CHUNK - 1:CHUNK, :]
    e1 = jnp.exp(b_c - bm)
    e2 = jnp.exp(bm - b_c)
    e0 = jnp.exp(b_c)
    e3 = jnp.exp(bl - b_c)
    el = jnp.exp(bl)
    return e1, e2, e0, e3, el, q_c * e1, k_c * e2, q_c * e0, k_c * e3


def _hg_fwd(ub, lb, gn4, carry=None):
    Lp = ub.shape[0]
    nt = Lp // TR
    cpt = TR // CHUNK

    def body(ub_ref, lb_ref, gn_ref, yb_ref, o_ref, ss_ref, st, bsc, qsc, ksc, qes, els, ust, tlo):
        i = pl.program_id(0)

        @pl.when(i == 0)
        def _():
            st[...] = jnp.zeros_like(st)
            tlo[...] = _chunk_tri(TR, False)

        row = i * TR + lax.broadcasted_iota(jnp.int32, (TR, 1), 0)
        q, k, g, _, _ = _hg_gates(ub_ref, lb_ref[...], row)
        qsc[...] = _silu(q)
        ksc[...] = k
        bsc[...] = _mm_split(tlo[...], g)
        tri = lax.broadcasted_iota(jnp.int32, (CHUNK, CHUNK), 1) <= lax.broadcasted_iota(jnp.int32, (CHUNK, CHUNK), 0)

        def intra(c, carry):
            rows = pl.ds(pl.multiple_of(c * CHUNK, CHUNK), CHUNK)
            _, _, _, _, el, qe, ke, qE, kd = _hg_chunk_terms(bsc[rows, :], qsc[rows, :], ksc[rows, :])
            qe, ke, kd = qe.astype(BF), ke.astype(BF), kd.astype(BF)
            qes[rows, :] = qE.astype(BF)
            els[c] = jnp.broadcast_to(el, (8, 512))
            sls = [slice(HG_D * h, HG_D * (h + 1)) for h in range(HG_HEADS)]
            v = [ub_ref[rows, 1024 + HG_D * h:1024 + HG_D * (h + 1)] for h in range(HG_HEADS)]
            a = [_nt(qe[:, sl], ke[:, sl]) for sl in sls]
            u = [_tn(v[h], kd[:, sls[h]]) for h in range(HG_HEADS)]
            a = [jnp.where(tri, x, 0.0).astype(BF) for x in a]
            oi = [_nn(a[h], v[h]) for h in range(HG_HEADS)]
            for h in range(HG_HEADS):
                ust[c, h] = u[h]
                o_ref[rows, sls[h]] = oi[h]
            return carry

        lax.fori_loop(0, cpt, intra, 0, unroll=2)

        for h in range(HG_HEADS):
            sl = slice(HG_D * h, HG_D * (h + 1))
            s = st[h]
            for c in range(cpt):
                ss_ref[c, h] = s
                s = els[c, 0:1, sl] * s + ust[c, h]
            st[h] = s

        def inter(c, carry):
            rows = pl.ds(pl.multiple_of(c * CHUNK, CHUNK), CHUNK)
            for h in range(HG_HEADS):
                sl = slice(HG_D * h, HG_D * (h + 1))
                o_ref[rows, sl] += _nt(qes[rows, sl], ss_ref[c, h].astype(BF))
            return carry

        lax.fori_loop(0, cpt, inter, 0, unroll=2)

        gate = ub_ref[:, 1536:2048].astype(F32)
        for h in range(HG_HEADS):
            sl = slice(HG_D * h, HG_D * (h + 1))
            o = o_ref[:, sl]
            r = lax.rsqrt(jnp.mean(o * o, axis=-1, keepdims=True) + EPS)
            yb_ref[:, sl] = (o * r * gn_ref[:, sl] * _silu(gate[:, sl])).astype(BF)

    in_specs = [pl.BlockSpec((TR, N_B), lambda i: (i, 0)), pl.BlockSpec((1, 512), lambda i: (0, 0)),
                pl.BlockSpec((1, 512), lambda i: (0, 0))]
    out_specs = [pl.BlockSpec((TR, 512), lambda i: (i, 0)), pl.BlockSpec((TR, 512), lambda i: (i, 0)),
                 pl.BlockSpec((cpt, HG_HEADS, HG_D, HG_D), lambda i: (i, 0, 0, 0))]
    out_shape = [jax.ShapeDtypeStruct((Lp, 512), BF), jax.ShapeDtypeStruct((Lp, 512), F32),
                 jax.ShapeDtypeStruct((Lp // CHUNK, HG_HEADS, HG_D, HG_D), F32)]
    scratch = [pltpu.VMEM((HG_HEADS, HG_D, HG_D), F32), pltpu.VMEM((TR, 512), F32),
               pltpu.VMEM((TR, 512), F32), pltpu.VMEM((TR, 512), F32), pltpu.VMEM((TR, 512), BF),
               pltpu.VMEM((cpt, 8, 512), F32), pltpu.VMEM((cpt, HG_HEADS, HG_D, HG_D), F32), pltpu.VMEM((TR, TR), BF)]
    return _call_carrying(body, "hgrn_fwd", nt, in_specs, out_specs, out_shape, scratch, (ub, lb, gn4), carry)


def _hg_bwd(ub, lb, gn4, o_save, s_save, dyb, carry=None):
    Lp = ub.shape[0]
    nt = Lp // TR
    cpt = TR // CHUNK

    def body(ub_ref, lb_ref, gn_ref, o_ref, ss_ref, dy_ref, du_ref, ds_ref,
             dst, bsc, qsc, ksc, dosc, dqsc, dksc, dbsc, els, ust, dss, tlo, tup):
        i = pl.program_id(0)
        t = nt - 1 - i

        @pl.when(i == 0)
        def _():
            dst[...] = jnp.zeros_like(dst)
            ds_ref[...] = jnp.zeros_like(ds_ref)
            tlo[...] = _chunk_tri(TR, False)
            tup[...] = _chunk_tri(TR, True)

        lbv = lb_ref[...]
        row = t * TR + lax.broadcasted_iota(jnp.int32, (TR, 1), 0)
        valid = row >= META_PAD
        q, k, g, sig, f = _hg_gates(ub_ref, lbv, row)
        qsc[...] = _silu(q)
        ksc[...] = k
        bsc[...] = _mm_split(tlo[...], g)

        gate = ub_ref[:, 1536:2048].astype(F32)
        dy = dy_ref[...].astype(F32)
        dgn = jnp.zeros((1, 512), F32)
        for h in range(HG_HEADS):
            sl = slice(HG_D * h, HG_D * (h + 1))
            o = o_ref[:, sl]
            r = lax.rsqrt(jnp.mean(o * o, axis=-1, keepdims=True) + EPS)
            ohat = o * r
            don = dy[:, sl] * _silu(gate[:, sl])
            du_ref[:, 1536 + HG_D * h:1536 + HG_D * (h + 1)] = (
                dy[:, sl] * ohat * gn_ref[:, sl] * _dsilu(gate[:, sl])).astype(BF)
            ds_ref[1:2, sl] += jnp.sum(don * ohat, axis=0, keepdims=True)
            gd = don * gn_ref[:, sl]
            dosc[:, sl] = r * (gd - ohat * jnp.mean(gd * ohat, axis=-1, keepdims=True))

        tri = lax.broadcasted_iota(jnp.int32, (CHUNK, CHUNK), 1) <= lax.broadcasted_iota(jnp.int32, (CHUNK, CHUNK), 0)
        last = lax.broadcasted_iota(jnp.int32, (CHUNK, 1), 0) == CHUNK - 1

        def incr(c, carry):
            rows = pl.ds(pl.multiple_of(c * CHUNK, CHUNK), CHUNK)
            b_c = bsc[rows, :]
            qE_b = (qsc[rows, :] * jnp.exp(b_c)).astype(BF)
            els[c] = jnp.broadcast_to(jnp.exp(b_c[CHUNK - 1:CHUNK, :]), (8, 512))
            do_c = dosc[rows, :].astype(BF)
            for h in range(HG_HEADS):
                sl = slice(HG_D * h, HG_D * (h + 1))
                ust[c, h] = _tn(do_c[:, sl], qE_b[:, sl])
            return carry

        lax.fori_loop(0, cpt, incr, 0, unroll=2)

        for h in range(HG_HEADS):
            sl = slice(HG_D * h, HG_D * (h + 1))
            d_s = dst[h]
            for c in reversed(range(cpt)):
                dss[c, h] = d_s
                d_s = els[c, 0:1, sl] * d_s + ust[c, h]
            dst[h] = d_s

        def chunk(c, carry):
            r0 = pl.multiple_of(c * CHUNK, CHUNK)
            rows = pl.ds(r0, CHUNK)
            e1, e2, e0, e3, el, qe, ke, qE, kd = _hg_chunk_terms(bsc[rows, :], qsc[rows, :], ksc[rows, :])
            qe_b, ke_b, kd_b = qe.astype(BF), ke.astype(BF), kd.astype(BF)
            do_c = dosc[rows, :].astype(BF)
            hs = range(HG_HEADS)
            sls = [slice(HG_D * h, HG_D * (h + 1)) for h in hs]
            v = [ub_ref[rows, 1024 + HG_D * h:1024 + HG_D * (h + 1)] for h in hs]
            do = [do_c[:, sl] for sl in sls]
            a = [_nt(qe_b[:, sl], ke_b[:, sl]) for sl in sls]
            da = [_nt(do[h], v[h]) for h in hs]
            dqE = [_nn(do[h], ss_ref[c, h].astype(BF)) for h in hs]
            dkd = [_nn(v[h], dss[c, h].astype(BF)) for h in hs]
            dv2 = [_nt(kd_b[:, sls[h]], dss[c, h].astype(BF)) for h in hs]
            a = [jnp.where(tri, x, 0.0).astype(BF) for x in a]
            da = [jnp.where(tri, x, 0.0).astype(BF) for x in da]
            dv = [_tn(a[h], do[h]) + dv2[h] for h in hs]
            dqe = [_nn(da[h], ke_b[:, sls[h]]) for h in hs]
            dke = [_tn(da[h], qe_b[:, sls[h]]) for h in hs]
            for h in hs:
                sl = sls[h]
                del_h = jnp.sum(ss_ref[c, h] * dss[c, h], axis=0, keepdims=True)
                dqsc[rows, sl] = dqE[h] * e0[:, sl] + dqe[h] * e1[:, sl]
                dksc[rows, sl] = dke[h] * e2[:, sl] + dkd[h] * e3[:, sl]
                tkd = dkd[h] * kd[:, sl]
                dbl = jnp.sum(tkd, axis=0, keepdims=True) + del_h * el[:, sl]
                dbsc[rows, sl] = (dqE[h] * qE[:, sl] + dqe[h] * qe[:, sl] - dke[h] * ke[:, sl] - tkd
                                  + jnp.where(last, dbl, 0.0))
                du_ref[rows, 1024 + HG_D * h:1024 + HG_D * (h + 1)] = dv[h].astype(BF)
            return carry

        lax.fori_loop(0, cpt, chunk, 0, unroll=2)

        dg = _mm_split(tup[...], dbsc[...])
        df = jnp.where(valid & (f > F_FLOOR), dg / f, 0.0)
        dk = jnp.where(valid, dksc[...], 0.0)
        dsig = (df - dk) * (1.0 - lbv)
        ds_ref[0:1, :] += jnp.sum((df - dk) * (1.0 - sig), axis=0, keepdims=True)
        du_ref[:, 512:1024] = (dsig * sig * (1.0 - sig)).astype(BF)
        du_ref[:, 0:512] = (dqsc[...] * _dsilu(q)).astype(BF)

    rev = lambda i: (nt - 1 - i, 0)
    in_specs = [pl.BlockSpec((TR, N_B), rev), pl.BlockSpec((1, 512), lambda i: (0, 0)),
                pl.BlockSpec((1, 512), lambda i: (0, 0)), pl.BlockSpec((TR, 512), rev),
                pl.BlockSpec((cpt, HG_HEADS, HG_D, HG_D), lambda i: (nt - 1 - i, 0, 0, 0)),
                pl.BlockSpec((TR, 512), rev)]
    out_specs = [pl.BlockSpec((TR, N_B), rev), pl.BlockSpec((8, 512), lambda i: (0, 0))]
    out_shape = [jax.ShapeDtypeStruct((Lp, N_B), BF), jax.ShapeDtypeStruct((8, 512), F32)]
    states = pltpu.VMEM((cpt, HG_HEADS, HG_D, HG_D), F32)
    scratch = ([pltpu.VMEM((HG_HEADS, HG_D, HG_D), F32)] + [pltpu.VMEM((TR, 512), F32)] * 7
               + [pltpu.VMEM((cpt, 8, 512), F32), states, states, pltpu.VMEM((TR, TR), BF), pltpu.VMEM((TR, TR), BF)])
    return _call_carrying(body, "hgrn_bwd", nt, in_specs, out_specs, out_shape, scratch,
                          (ub, lb, gn4, o_save, s_save, dyb), carry)


_KCOL = (2 * 512) // 128
_VCOL = _KCOL + 1


def _swa_in_specs(nt, rev):
    tile = (lambda i: nt - 1 - i) if rev else (lambda i: i)
    hpt = TR // HALO
    return [
        pl.BlockSpec((TR, 512), lambda i: (tile(i), 0)),
        pl.BlockSpec((TR, 512), lambda i: (tile(i), 1)),
        pl.BlockSpec((TR, 128), lambda i: (tile(i), _KCOL)),
        pl.BlockSpec((TR, 128), lambda i: (tile(i), _VCOL)),
        pl.BlockSpec((HALO, 128), lambda i: (jnp.maximum(tile(i) * hpt - 1, 0), _KCOL)),
        pl.BlockSpec((HALO, 128), lambda i: (jnp.maximum(tile(i) * hpt - 1, 0), _VCOL)),
        pl.BlockSpec((CHUNK, 128), lambda i: (0, _KCOL)),
        pl.BlockSpec((CHUNK, 128), lambda i: (0, _VCOL)),
        pl.BlockSpec((1, 512), lambda i: (0, 0)),
        pl.BlockSpec((1, 128), lambda i: (0, 0)),
        pl.BlockSpec((1, ATT_Q_HEADS), lambda i: (0, 0)),
    ]


_WROWS = 2 * CHUNK + HALO + TR
_W0 = 2 * CHUNK
_C0 = _W0 + HALO
_SCALE = ATT_HD ** -0.5


def _group_ones(n):
    r = lax.broadcasted_iota(jnp.int32, (n, n), 0)
    c = lax.broadcasted_iota(jnp.int32, (n, n), 1)
    return jnp.where(jnp.right_shift(r, 6) == jnp.right_shift(c, 6), 1.0, 0.0).astype(BF)


def _group_mean(x, ones):
    hi = x.astype(BF)
    lo = (x - hi.astype(F32)).astype(BF)
    return (_nn(hi, ones) + _nn(lo, ones)) * (1.0 / ATT_HD)


def _head_rms(x, ones):
    r = lax.rsqrt(_group_mean(x * x, ones) + EPS)
    return x * r, r


def _swa_windows(kc_ref, vc_ref, kh_ref, vh_ref, km_ref, vm_ref, kg2, ones, kwin, krwin, vwin, vrwin):
    meta = pl.ds(META_PAD, N_META)
    for (k, v, r0, n) in ((km_ref[meta, :], vm_ref[meta, :], 0, N_META), (kh_ref[...], vh_ref[...], _W0, HALO),
                          (kc_ref[...], vc_ref[...], _C0, TR)):
        xhat, _ = _head_rms(k.astype(F32), ones)
        kn = xhat * kg2
        kwin[pl.ds(r0, n), :] = kn.astype(BF)
        krwin[pl.ds(r0, n), :] = pltpu.roll(kn, ATT_HD, 1).astype(BF)
        vwin[pl.ds(r0, n), :] = v
        if vrwin is not None:
            vrwin[pl.ds(r0, n), :] = pltpu.roll(v.astype(F32), ATT_HD, 1).astype(BF)
    zero = jnp.zeros((_W0 - N_META, 128), BF)
    for w in (kwin, krwin, vwin, vrwin):
        if w is not None:
            w[pl.ds(N_META, _W0 - N_META), :] = zero


def _swa_masks_t(t, qb):
    q0 = t * TR + qb * QB
    qc = jnp.right_shift(q0 + lax.broadcasted_iota(jnp.int32, (1, QB), 1), 6)
    kabs = q0 - HALO + lax.broadcasted_iota(jnp.int32, (QB + HALO, 1), 0)
    kc = jnp.right_shift(kabs + HALO, 6) - HALO // CHUNK
    mask_w = (kc <= qc) & (kc >= qc - 2) & (kabs >= META_PAD)
    return qc > 2, mask_w


def _swa_park(dtype):
    return [pltpu.VMEM((ATT_Q_HEADS, N_META, QB), dtype), pltpu.VMEM((ATT_Q_HEADS, QB + HALO, QB), dtype)]


def _split_heads(x, lane_hi):
    return jnp.where(lane_hi, 0.0, x).astype(BF), jnp.where(lane_hi, x, 0.0).astype(BF)


def _call_carrying(body, name, nt, in_specs, out_specs, out_shape, scratch, args, carry):
    if carry is None:
        return pl.pallas_call(body, name=name, grid=(nt,), in_specs=in_specs, out_specs=out_specs, out_shape=out_shape,
                              scratch_shapes=scratch, compiler_params=_cp(("arbitrary",)))(*args)
    kind, arrs = carry
    n = len(arrs)
    return pl.pallas_call(
        _carry_exchange(body, len(in_specs), len(out_specs), nt, kind, n), name=name + "_" + kind, grid=(nt,),
        in_specs=in_specs + [_ANY] * n, out_specs=out_specs + [_ANY] * n,
        out_shape=out_shape + _exchange_out_shapes(kind, arrs), scratch_shapes=scratch + _exchange_sems(n),
        compiler_params=_cp(("arbitrary",), has_side_effects=True),
    )(*args, *arrs)


def _swa_fwd(uc, qg8, kg2, sinks, carry=None):
    Lp = uc.shape[0]
    nt = Lp // TR
    nqb = TR // QB

    def body(q_ref, g_ref, kc_ref, vc_ref, kh_ref, vh_ref, km_ref, vm_ref, qg_ref, kg_ref, sk_ref,
             yc_ref, o_ref, lse_ref, kwin, krwin, vwin, vt, qlo, qhi, ot, s_m, s_w, p_m, p_w):
        t = pl.program_id(0)
        _swa_windows(kc_ref, vc_ref, kh_ref, vh_ref, km_ref, vm_ref, kg_ref[...], _group_ones(128),
                     kwin, krwin, vwin, None)
        vt[...] = vwin[...].T
        xhat, _ = _head_rms(q_ref[...].astype(F32), _group_ones(512))
        lane_hi = (lax.broadcasted_iota(jnp.int32, (1, 512), 1) & ATT_HD) != 0
        lo, hi = _split_heads(xhat * qg_ref[...] * _SCALE, lane_hi)
        qlo[...] = lo
        qhi[...] = hi
        for qb in range(nqb):
            rows = pl.ds(qb * QB, QB)
            wrows = pl.ds(_W0 + qb * QB, QB + HALO)
            mrows = pl.ds(0, N_META)
            mask_m, mask_w = _swa_masks_t(t, qb)
            for j in range(ATT_Q_HEADS):
                p, e = j // 2, j % 2
                ks = kwin if e == j // ATT_GROUP else krwin
                qp = (qlo, qhi)[e][rows, 128 * p:128 * (p + 1)]
                s_m[j] = _nt(ks[mrows, :], qp)
                s_w[j] = _nt(ks[wrows, :], qp)
            inv = []
            for j in range(ATT_Q_HEADS):
                sm = jnp.where(mask_m, s_m[j], NEG)
                sw = jnp.where(mask_w, s_w[j], NEG)
                sink = sk_ref[:, j:j + 1]
                m = jnp.maximum(jnp.maximum(jnp.max(sm, axis=0, keepdims=True),
                                            jnp.max(sw, axis=0, keepdims=True)), sink)
                em = jnp.exp(sm - m)
                ew = jnp.exp(sw - m)
                den = jnp.sum(em, axis=0, keepdims=True) + jnp.sum(ew, axis=0, keepdims=True) + jnp.exp(sink - m)
                p_m[j] = em.astype(BF)
                p_w[j] = ew.astype(BF)
                lse_ref[j:j + 1, pl.ds(qb * QB, QB)] = m + jnp.log(den)
                inv.append(1.0 / den)
            for j in range(ATT_Q_HEADS):
                vrows = pl.ds(ATT_HD * (j // ATT_GROUP), ATT_HD)
                ot[pl.ds(ATT_HD * j, ATT_HD), pl.ds(qb * QB, QB)] = (
                    _nn(vt[vrows, pl.ds(0, N_META)], p_m[j])
                    + _nn(vt[vrows, pl.ds(_W0 + qb * QB, QB + HALO)], p_w[j])) * inv[j]
        o = ot[...].T
        o_ref[...] = o
        yc_ref[...] = (o * _silu(g_ref[...].astype(F32))).astype(BF)

    win = pltpu.VMEM((_WROWS, 128), BF)
    in_specs = _swa_in_specs(nt, False)
    out_specs = [pl.BlockSpec((TR, 512), lambda i: (i, 0)), pl.BlockSpec((TR, 512), lambda i: (i, 0)),
                 pl.BlockSpec((ATT_Q_HEADS, TR), lambda i: (0, i))]
    out_shape = [jax.ShapeDtypeStruct((Lp, 512), BF), jax.ShapeDtypeStruct((Lp, 512), F32),
                 jax.ShapeDtypeStruct((ATT_Q_HEADS, Lp), F32)]
    scratch = [win, win, win, pltpu.VMEM((128, _WROWS), BF), pltpu.VMEM((TR, 512), BF),
               pltpu.VMEM((TR, 512), BF), pltpu.VMEM((512, TR), F32)] + _swa_park(F32) + _swa_park(BF)
    return _call_carrying(body, "swa_fwd", nt, in_specs, out_specs, out_shape, scratch,
                          (uc, uc, uc, uc, uc, uc, uc, uc, qg8, kg2, sinks), carry)


def _swa_bwd(uc, qg8, kg2, sinks, o_save, lse, dyc):
    Lp = uc.shape[0]
    nt = Lp // TR
    nqb = TR // QB

    def body(q_ref, g_ref, kc_ref, vc_ref, kh_ref, vh_ref, km_ref, vm_ref, qg_ref, kg_ref, sk_ref,
             o_ref, lse_ref, dy_ref, du_ref, dg_ref, dsk_ref,
             kwin, krwin, vwin, vrwin, kt, krt, qlo, qhi, dolo, dohi, dqt, dk_dir, dk_rol, dv_dir, dv_rol,
             carry_k, carry_v, meta_k, meta_v, s_m, s_w, dp_m, dp_w, p_m, p_w, ds_m, ds_w):
        i = pl.program_id(0)
        t = nt - 1 - i

        @pl.when(i == 0)
        def _():
            carry_k[...] = jnp.zeros_like(carry_k)
            carry_v[...] = jnp.zeros_like(carry_v)
            meta_k[...] = jnp.zeros_like(meta_k)
            meta_v[...] = jnp.zeros_like(meta_v)
            dg_ref[...] = jnp.zeros_like(dg_ref)
            dsk_ref[...] = jnp.zeros_like(dsk_ref)

        ones128 = _group_ones(128)
        ones512 = _group_ones(512)
        _swa_windows(kc_ref, vc_ref, kh_ref, vh_ref, km_ref, vm_ref, kg_ref[...], ones128, kwin, krwin, vwin, vrwin)
        kt[...] = kwin[...].T
        krt[...] = krwin[...].T
        xhat_q, r_q = _head_rms(q_ref[...].astype(F32), ones512)
        lane_hi = (lax.broadcasted_iota(jnp.int32, (1, 512), 1) & ATT_HD) != 0
        lo, hi = _split_heads(xhat_q * qg_ref[...] * _SCALE, lane_hi)
        qlo[...] = lo
        qhi[...] = hi
        gate = g_ref[...].astype(F32)
        dy = dy_ref[...].astype(F32)
        do = dy * _silu(gate)
        o = o_ref[...]
        du_ref[:, 512:1024] = (dy * o * _dsilu(gate)).astype(BF)
        lo, hi = _split_heads(do, lane_hi)
        dolo[...] = lo
        dohi[...] = hi
        hsel = jnp.where(jnp.right_shift(lax.broadcasted_iota(jnp.int32, (ATT_Q_HEADS, 512), 1), 6)
                         == lax.broadcasted_iota(jnp.int32, (ATT_Q_HEADS, 512), 0), 1.0, 0.0).astype(BF)
        prod = do * o
        p_hi = prod.astype(BF)
        d_t = _nt(hsel, p_hi) + _nt(hsel, (prod - p_hi.astype(F32)).astype(BF))
        for acc in (dk_dir, dk_rol, dv_dir, dv_rol):
            acc[...] = jnp.zeros_like(acc)

        for qb in range(nqb):
            rows = pl.ds(qb * QB, QB)
            qcols = pl.ds(qb * QB, QB)
            wrows = pl.ds(_W0 + qb * QB, QB + HALO)
            mrows = pl.ds(0, N_META)
            mask_m, mask_w = _swa_masks_t(t, qb)
            for j in range(ATT_Q_HEADS):
                p, e = j // 2, j % 2
                ks, vs = (kwin, vwin) if e == j // ATT_GROUP else (krwin, vrwin)
                pair = slice(128 * p, 128 * (p + 1))
                qp = (qlo, qhi)[e][rows, pair]
                dop = (dolo, dohi)[e][rows, pair]
                s_m[j] = _nt(ks[mrows, :], qp)
                s_w[j] = _nt(ks[wrows, :], qp)
                dp_m[j] = _nt(vs[mrows, :], dop)
                dp_w[j] = _nt(vs[wrows, :], dop)
            for j in range(ATT_Q_HEADS):
                lse_j = lse_ref[j:j + 1, qcols]
                d_j = d_t[j:j + 1, qb * QB:(qb + 1) * QB]
                em = jnp.exp(jnp.where(mask_m, s_m[j], NEG) - lse_j)
                ew = jnp.exp(jnp.where(mask_w, s_w[j], NEG) - lse_j)
                p_m[j] = em.astype(BF)
                p_w[j] = ew.astype(BF)
                ds_m[j] = (em * (dp_m[j] - d_j)).astype(BF)
                ds_w[j] = (ew * (dp_w[j] - d_j)).astype(BF)
                dsk_ref[j:j + 1, :] -= jnp.exp(sk_ref[:, j:j + 1] - lse_j) * d_j
            for j in range(ATT_Q_HEADS):
                e = j % 2
                ktr = kt if e == j // ATT_GROUP else krt
                hrows = pl.ds(ATT_HD * e, ATT_HD)
                dqt[pl.ds(ATT_HD * j, ATT_HD), qcols] = (_nn(ktr[hrows, pl.ds(0, N_META)], ds_m[j])
                                                         + _nn(ktr[hrows, pl.ds(_W0 + qb * QB, QB + HALO)], ds_w[j]))
            for direct, dk_acc, dv_acc in ((True, dk_dir, dv_dir), (False, dk_rol, dv_rol)):
                heads = [j for j in range(ATT_Q_HEADS) if (j % 2 == j // ATT_GROUP) == direct]
                q_cat = jnp.concatenate([(qlo, qhi)[j % 2][rows, 128 * (j // 2):128 * (j // 2 + 1)] for j in heads], axis=0)
                do_cat = jnp.concatenate([(dolo, dohi)[j % 2][rows, 128 * (j // 2):128 * (j // 2 + 1)] for j in heads], axis=0)
                dk_acc[mrows, :] += _nn(jnp.concatenate([ds_m[j] for j in heads], axis=1), q_cat)
                dk_acc[wrows, :] += _nn(jnp.concatenate([ds_w[j] for j in heads], axis=1), q_cat)
                dv_acc[mrows, :] += _nn(jnp.concatenate([p_m[j] for j in heads], axis=1), do_cat)
                dv_acc[wrows, :] += _nn(jnp.concatenate([p_w[j] for j in heads], axis=1), do_cat)

        dk_dir[...] += pltpu.roll(dk_rol[...], ATT_HD, 1)
        dv_dir[...] += pltpu.roll(dv_rol[...], ATT_HD, 1)
        meta_k[...] += dk_dir[pl.ds(0, N_META), :]
        meta_v[...] += dv_dir[pl.ds(0, N_META), :]
        first = jnp.where(t == 0, 1.0, 0.0)
        dk_dir[pl.ds(_C0 + TR - HALO, HALO), :] += carry_k[...]
        dv_dir[pl.ds(_C0 + TR - HALO, HALO), :] += carry_v[...]
        dk_dir[pl.ds(_C0 + META_PAD, N_META), :] += first * meta_k[...]
        dv_dir[pl.ds(_C0 + META_PAD, N_META), :] += first * meta_v[...]
        carry_k[...] = dk_dir[pl.ds(_W0, HALO), :]
        carry_v[...] = dv_dir[pl.ds(_W0, HALO), :]

        du_ref[:, 1152:1280] = dv_dir[pl.ds(_C0, TR), :].astype(BF)
        xhat_k, r_k = _head_rms(kc_ref[...].astype(F32), ones128)
        dkn = dk_dir[pl.ds(_C0, TR), :]
        dg_ref[1:2, 0:128] += jnp.sum(dkn * xhat_k, axis=0, keepdims=True)
        gd = dkn * kg_ref[...]
        du_ref[:, 1024:1152] = (r_k * (gd - xhat_k * _group_mean(gd * xhat_k, ones128))).astype(BF)
        dqn = dqt[...].T * _SCALE
        dg_ref[0:1, :] += jnp.sum(dqn * xhat_q, axis=0, keepdims=True)
        gd = dqn * qg_ref[...]
        du_ref[:, 0:512] = (r_q * (gd - xhat_q * _group_mean(gd * xhat_q, ones512))).astype(BF)

    rev = lambda i: (nt - 1 - i, 0)
    specs = _swa_in_specs(nt, True)
    win = pltpu.VMEM((_WROWS, 128), BF)
    wint = pltpu.VMEM((128, _WROWS), BF)
    tile_bf = pltpu.VMEM((TR, 512), BF)
    acc = pltpu.VMEM((_WROWS, 128), F32)
    return pl.pallas_call(
        body, name="swa_bwd", grid=(nt,),
        in_specs=specs + [pl.BlockSpec((TR, 512), rev), pl.BlockSpec((ATT_Q_HEADS, TR), lambda i: (0, nt - 1 - i)),
                          pl.BlockSpec((TR, 512), rev)],
        out_specs=[pl.BlockSpec((TR, N_C), rev), pl.BlockSpec((8, 512), lambda i: (0, 0)),
                   pl.BlockSpec((8, 128), lambda i: (0, 0))],
        out_shape=[jax.ShapeDtypeStruct((Lp, N_C), BF), jax.ShapeDtypeStruct((8, 512), F32),
                   jax.ShapeDtypeStruct((8, 128), F32)],
        scratch_shapes=[win, win, win, win, wint, wint, tile_bf, tile_bf, tile_bf, tile_bf,
                        pltpu.VMEM((512, TR), F32), acc, acc, acc, acc,
                        pltpu.VMEM((HALO, 128), F32), pltpu.VMEM((HALO, 128), F32),
                        pltpu.VMEM((N_META, 128), F32), pltpu.VMEM((N_META, 128), F32)]
        + _swa_park(F32) + _swa_park(F32) + _swa_park(BF) + _swa_park(BF),
        compiler_params=_cp(("arbitrary",)),
    )(uc, uc, uc, uc, uc, uc, uc, uc, qg8, kg2, sinks, o_save, lse, dyc)


def _mix_fwd(h, ya, yb, yc, ug, wa, wb, wc, wo):
    Lp = h.shape[0]
    wspec = lambda r: pl.BlockSpec((r, D_MODEL), lambda i: (0, 0))
    yspec = pl.BlockSpec((TRM, 512), lambda i: (i, 0))
    hspec = pl.BlockSpec((TRM, D_MODEL), lambda i: (i, 0))

    def body(h_ref, ya_ref, yb_ref, yc_ref, ug_ref, wa_ref, wb_ref, wc_ref, wo_ref,
             hn_ref, za_ref, zb_ref, zc_ref, mx_ref):
        mixed = jnp.zeros((TRM, D_MODEL), F32)
        for n, (y_ref, w_ref, z_ref) in enumerate(((ya_ref, wa_ref, za_ref), (yb_ref, wb_ref, zb_ref),
                                                   (yc_ref, wc_ref, zc_ref))):
            z = _nn(y_ref[...], w_ref[...])
            z_ref[...] = z.astype(BF)
            mixed = mixed + _sig(ug_ref[:, D_MODEL * n:D_MODEL * (n + 1)].astype(F32)) * z
        mixed = mixed.astype(BF)
        mx_ref[...] = mixed
        hn_ref[...] = h_ref[...] + _nn(mixed, wo_ref[...])

    return pl.pallas_call(
        body, name="mix_fwd", grid=(Lp // TRM,),
        in_specs=[hspec, yspec, yspec, yspec, pl.BlockSpec((TRM, N_G), lambda i: (i, 0)),
                  wspec(512), wspec(512), wspec(512), wspec(D_MODEL)],
        out_specs=[hspec, hspec, hspec, hspec, hspec],
        out_shape=[jax.ShapeDtypeStruct((Lp, D_MODEL), F32)] + [jax.ShapeDtypeStruct((Lp, D_MODEL), BF)] * 4,
        compiler_params=_cp(("parallel",)),
    )(h, ya, yb, yc, ug, wa, wb, wc, wo)


def _mix_bwd(dh, za, zb, zc, ug, wa, wb, wc, wo):
    Lp = dh.shape[0]
    wspec = lambda r: pl.BlockSpec((r, D_MODEL), lambda i: (0, 0))
    yspec = pl.BlockSpec((TRM, 512), lambda i: (i, 0))
    hspec = pl.BlockSpec((TRM, D_MODEL), lambda i: (i, 0))
    gspec = pl.BlockSpec((TRM, N_G), lambda i: (i, 0))

    def body(dh_ref, za_ref, zb_ref, zc_ref, ug_ref, wa_ref, wb_ref, wc_ref, wo_ref,
             dug_ref, dza_ref, dzb_ref, dzc_ref, dya_ref, dyb_ref, dyc_ref):
        dmix = _nt(dh_ref[...].astype(BF), wo_ref[...])
        for n, (z_ref, w_ref, dz_ref, dy_ref) in enumerate(((za_ref, wa_ref, dza_ref, dya_ref),
                                                            (zb_ref, wb_ref, dzb_ref, dyb_ref),
                                                            (zc_ref, wc_ref, dzc_ref, dyc_ref))):
            sl = slice(D_MODEL * n, D_MODEL * (n + 1))
            gt = _sig(ug_ref[:, sl].astype(F32))
            dz = dmix * gt
            dug_ref[:, sl] = (dz * z_ref[...].astype(F32) * (1.0 - gt)).astype(BF)
            dz = dz.astype(BF)
            dz_ref[...] = dz
            dy_ref[...] = _nt(dz, w_ref[...]).astype(BF)

    bf = lambda n: jax.ShapeDtypeStruct((Lp, n), BF)
    return pl.pallas_call(
        body, name="mix_bwd", grid=(Lp // TRM,),
        in_specs=[hspec, hspec, hspec, hspec, gspec, wspec(512), wspec(512), wspec(512), wspec(D_MODEL)],
        out_specs=[gspec, hspec, hspec, hspec, yspec, yspec, yspec],
        out_shape=[bf(N_G), bf(D_MODEL), bf(D_MODEL), bf(D_MODEL), bf(512), bf(512), bf(512)],
        compiler_params=_cp(("parallel",)),
    )(dh, za, zb, zc, ug, wa, wb, wc, wo)


def _inproj_bwd(dus, ws, h, dh, g, carry=None):
    Lp = h.shape[0]
    widths = [w.shape[0] for w in ws]

    def body(dg_ref, da_ref, db_ref, dc_ref, wg_ref, wa_ref, wb_ref, wc_ref, h_ref, dh_ref, g_ref, o_ref, gg_ref):
        @pl.when(pl.program_id(0) == 0)
        def _():
            gg_ref[...] = jnp.zeros_like(gg_ref)

        dhn = (_nn(dg_ref[...], wg_ref[...]) + _nn(da_ref[...], wa_ref[...])
               + _nn(db_ref[...], wb_ref[...]) + _nn(dc_ref[...], wc_ref[...]))
        x = h_ref[...]
        r = lax.rsqrt(jnp.mean(x * x, axis=-1, keepdims=True) + EPS)
        xhat = x * r
        gg_ref[0:1, :] += jnp.sum(dhn * xhat, axis=0, keepdims=True)
        gd = dhn * g_ref[...]
        o_ref[...] = dh_ref[...] + r * (gd - xhat * jnp.mean(gd * xhat, axis=-1, keepdims=True))

    hspec = pl.BlockSpec((TRM, D_MODEL), lambda i: (i, 0))
    in_specs = ([pl.BlockSpec((TRM, n), lambda i: (i, 0)) for n in widths]
                + [pl.BlockSpec((n, D_MODEL), lambda i: (0, 0), pipeline_mode=pl.Buffered(1)) for n in widths]
                + [hspec, hspec, pl.BlockSpec((1, D_MODEL), lambda i: (0, 0))])
    out_specs = [hspec, pl.BlockSpec((8, D_MODEL), lambda i: (0, 0))]
    out_shape = [jax.ShapeDtypeStruct((Lp, D_MODEL), F32), jax.ShapeDtypeStruct((8, D_MODEL), F32)]
    return _call_carrying(body, "inproj_bwd", Lp // TRM, in_specs, out_specs, out_shape, [],
                          (*dus, *ws, h, dh, g), carry)


def _loss_head(h, tgt_pad, seq):
    Lp = h.shape[0]
    nt = Lp // TR

    def body(h_ref, t_ref, dh_ref, l_ref):
        i = pl.program_id(0)

        @pl.when(i == 0)
        def _():
            l_ref[...] = jnp.zeros_like(l_ref)

        row = i * TR + lax.broadcasted_iota(jnp.int32, (TR, 1), 0)
        e = jnp.where((row >= CHUNK) & (row < CHUNK + seq), h_ref[...] - t_ref[...], 0.0)
        dh_ref[...] = e * (1.0 / D_MODEL)
        l_ref[...] += (0.5 / D_MODEL) * jnp.sum(jnp.sum(e * e, axis=0, keepdims=True), axis=1, keepdims=True)

    hspec = pl.BlockSpec((TR, D_MODEL), lambda i: (i, 0))
    return pl.pallas_call(
        body, name="loss_head", grid=(nt,), in_specs=[hspec, hspec],
        out_specs=[hspec, pl.BlockSpec((8, 128), lambda i: (0, 0))],
        out_shape=[jax.ShapeDtypeStruct((Lp, D_MODEL), F32), jax.ShapeDtypeStruct((8, 128), F32)],
        compiler_params=_cp(("arbitrary",)),
    )(h, tgt_pad)


def _lb_softmax(lb_ref):
    x = lb_ref[...]
    e = jnp.exp(x - jnp.max(x, axis=0, keepdims=True))
    return e / jnp.sum(e, axis=0, keepdims=True)


def _lb_fwd(hg_lb):
    def body(lb_ref, o_ref):
        sm = _lb_softmax(lb_ref)
        acc = jnp.zeros((1, 512), F32)
        for l in range(DEPTH):
            if l > 0:
                acc = acc + sm[l:l + 1, :]
            o_ref[l:l + 1, :] = jnp.clip(acc, 0.0, 1.0)

    return pl.pallas_call(body, name="lb_fwd", out_shape=jax.ShapeDtypeStruct((DEPTH, 512), F32))(hg_lb)


def _lb_bwd(hg_lb, dlb_all):
    def body(lb_ref, d_ref, o_ref):
        sm = _lb_softmax(lb_ref)
        acc = jnp.zeros((1, 512), F32)
        gm = []
        for l in range(DEPTH):
            if l > 0:
                acc = acc + sm[l:l + 1, :]
            gm.append(jnp.where((acc >= 0.0) & (acc <= 1.0), d_ref[l:l + 1, :], 0.0))
        dsm = [jnp.zeros((1, 512), F32)]
        for j in range(1, DEPTH):
            s = gm[j]
            for l in range(j + 1, DEPTH):
                s = s + gm[l]
            dsm.append(s)
        dot = dsm[0] * sm[0:1, :]
        for j in range(1, DEPTH):
            dot = dot + dsm[j] * sm[j:j + 1, :]
        for j in range(DEPTH):
            o_ref[j:j + 1, :] = sm[j:j + 1, :] * (dsm[j] - dot)

    return pl.pallas_call(body, name="lb_bwd", out_shape=jax.ShapeDtypeStruct((DEPTH, 512), F32))(hg_lb, dlb_all)


_ANY = pl.BlockSpec(memory_space=pl.ANY)


def _chip_peers():
    x, y, c = lax.axis_index("x"), lax.axis_index("y"), lax.axis_index("c")
    return (x, y, c), [(1 - x, y, c), (x, 1 - y, c), (1 - x, 1 - y, c)]


def _exchange(kind, ins, outs, send, recv, loc):
    (x, y, c), peers = _chip_peers()
    me = 2 * x + y
    ds = []
    for a in range(len(ins)):
        if kind == "gather":
            ds.append(pltpu.make_async_copy(ins[a], outs[a].at[me], loc.at[a]))
        else:
            ds.append(pltpu.make_async_copy(ins[a].at[me], outs[a].at[0], loc.at[a]))
        for p, (px, py, pc) in enumerate(peers):
            src, dst = (ins[a], outs[a].at[me]) if kind == "gather" else (ins[a].at[2 * px + py], outs[a].at[1 + p])
            ds.append(pltpu.make_async_remote_copy(src_ref=src, dst_ref=dst, send_sem=send.at[a, p],
                                                   recv_sem=recv.at[a, p], device_id=(px, py, pc), device_id_type=MESH))
    return ds


def _exchange_out_shapes(kind, arrs):
    if kind == "gather":
        return [jax.ShapeDtypeStruct((4,) + a.shape, a.dtype) for a in arrs]
    return [jax.ShapeDtypeStruct(a.shape, a.dtype) for a in arrs]


def _exchange_sems(n):
    return [pltpu.SemaphoreType.DMA((n, 3)), pltpu.SemaphoreType.DMA((n, 3)), pltpu.SemaphoreType.DMA((n,))]


def _exchange_chips(kind, arrs):
    n = len(arrs)

    def body(*refs):
        ds = _exchange(kind, refs[:n], refs[n:2 * n], *refs[2 * n:])
        for d in ds:
            d.start()
        for d in ds:
            d.wait()

    return pl.pallas_call(
        body, name=kind + "_chips", in_specs=[_ANY] * n, out_specs=[_ANY] * n,
        out_shape=_exchange_out_shapes(kind, arrs), scratch_shapes=_exchange_sems(n),
        compiler_params=pltpu.CompilerParams(has_side_effects=True),
    )(*arrs)


def _carry_exchange(body, n_in, n_out, n_steps, kind, n):
    def wrapped(*refs):
        ins, cin = refs[:n_in], refs[n_in:n_in + n]
        outs, cout = refs[n_in + n:n_in + n + n_out], refs[n_in + n + n_out:n_in + 2 * n + n_out]
        scr, sems = refs[n_in + 2 * n + n_out:-3], refs[-3:]
        i = pl.program_id(0)

        @pl.when(i == 0)
        def _():
            for d in _exchange(kind, cin, cout, *sems):
                d.start()

        body(*ins, *outs, *scr)

        @pl.when(i == n_steps - 1)
        def _():
            for d in _exchange(kind, cin, cout, *sems):
                d.wait()

    return wrapped


def _swap_cores(arrs):
    n = len(arrs)

    def body(*refs):
        ins, outs = refs[:n], refs[n:2 * n]
        send, recv = refs[2 * n:]
        x, y, c = lax.axis_index("x"), lax.axis_index("y"), lax.axis_index("c")
        rdmas = []
        for a in range(n):
            r = pltpu.make_async_remote_copy(src_ref=ins[a], dst_ref=outs[a], send_sem=send.at[a], recv_sem=recv.at[a],
                                             device_id=(x, y, 1 - c), device_id_type=MESH)
            r.start()
            rdmas.append(r)
        for r in rdmas:
            r.wait()

    return pl.pallas_call(
        body, name="swap_cores", in_specs=[_ANY] * n, out_specs=[_ANY] * n,
        out_shape=[jax.ShapeDtypeStruct(a.shape, a.dtype) for a in arrs],
        scratch_shapes=[pltpu.SemaphoreType.DMA((n,)), pltpu.SemaphoreType.DMA((n,))],
        compiler_params=pltpu.CompilerParams(has_side_effects=True),
    )(*arrs)


def _allsum_small(p):
    R = p.shape[0]

    def body(p_ref, o_ref, buf, send, recv):
        x, y, c = lax.axis_index("x"), lax.axis_index("y"), lax.axis_index("c")
        me = 4 * x + 2 * y + c
        buf[me] = p_ref[...]
        rdmas = []
        for k in range(1, 8):
            peer = (x ^ (k >> 2), y ^ ((k >> 1) & 1), c ^ (k & 1))
            r = pltpu.make_async_remote_copy(src_ref=p_ref, dst_ref=buf.at[me], send_sem=send.at[k - 1],
                                             recv_sem=recv.at[k - 1], device_id=peer, device_id_type=MESH)
            r.start()
            rdmas.append(r)
        for r in rdmas:
            r.wait()
        acc = buf[0]
        for d in range(1, 8):
            acc = acc + buf[d]
        o_ref[...] = acc

    return pl.pallas_call(
        body, name="allsum_small", out_shape=jax.ShapeDtypeStruct((R, 512), F32),
        in_specs=[pl.BlockSpec(memory_space=pltpu.VMEM)], out_specs=pl.BlockSpec(memory_space=pltpu.VMEM),
        scratch_shapes=[pltpu.VMEM((8, R, 512), F32), pltpu.SemaphoreType.DMA((7,)), pltpu.SemaphoreType.DMA((7,))],
        compiler_params=_cp(has_side_effects=True),
    )(p)


def _row_block(rows):
    return max((d for d in range(16, 513, 16) if rows % d == 0), default=rows)


def _sum4(parts, name):
    _, R, C = parts.shape
    tr = _row_block(R)

    def body(p_ref, o_ref):
        p = [p_ref[k].astype(F32) for k in range(4)]
        o_ref[...] = ((p[0] + p[1]) + p[2]) + p[3]

    return pl.pallas_call(
        body, name=name, grid=(R // tr,), in_specs=[pl.BlockSpec((4, tr, C), lambda i: (0, i, 0))],
        out_specs=pl.BlockSpec((tr, C), lambda i: (i, 0)), out_shape=jax.ShapeDtypeStruct((R, C), F32),
        compiler_params=_cp(("parallel",)),
    )(parts)


def _adamw(w, m, v, g0, g1, name):
    L, R, C = w.shape
    tr = _row_block(R)
    two = g1 is not None
    c1 = 1.0 / (1.0 - ADAM_B1 ** ADAM_STEP)
    c2 = 1.0 / (1.0 - ADAM_B2 ** ADAM_STEP)

    def body(*refs):
        if two:
            w_ref, m_ref, v_ref, a_ref, b_ref, g_ref, d_ref, nm_ref, nv_ref = refs
            g = a_ref[...] + b_ref[...]
        else:
            w_ref, m_ref, v_ref, a_ref, g_ref, d_ref, nm_ref, nv_ref = refs
            g = a_ref[...]
        g_ref[...] = g
        m = ADAM_B1 * m_ref[...] + (1.0 - ADAM_B1) * g
        v = ADAM_B2 * v_ref[...] + (1.0 - ADAM_B2) * (g * g)
        nm_ref[...] = m
        nv_ref[...] = v
        d_ref[...] = -ADAM_LR * ((m * c1) / (jnp.sqrt(v * c2) + ADAM_EPS) + ADAM_WD * w_ref[...])

    spec = pl.BlockSpec((1, tr, C), lambda l, i: (l, i, 0))
    n_in = 5 if two else 4
    ins = (w, m, v, g0, g1) if two else (w, m, v, g0)
    return pl.pallas_call(
        body, name=name, grid=(L, R // tr), in_specs=[spec] * n_in, out_specs=[spec] * 4,
        out_shape=[jax.ShapeDtypeStruct((L, R, C), F32)] * 4, compiler_params=_cp(("parallel", "parallel")),
    )(*ins)


def _pad8(a):
    r = (-a.shape[0]) % 8
    return a if r == 0 else jnp.pad(a, ((0, r), (0, 0)))


def _local_step(x, tgt, meta, P, shards=None, prep=None, pack=None):
    seq = x.shape[0]
    Lp = -(-(seq + CHUNK) // TR) * TR
    tail = Lp - seq - CHUNK
    h = jnp.concatenate([jnp.zeros((META_PAD, D_MODEL), F32), meta, x, jnp.zeros((tail, D_MODEL), F32)], axis=0)
    tgt_pad = jnp.pad(tgt, ((CHUNK, tail), (0, 0)))

    P = list(P)
    saved = []
    for l in range(DEPTH):
        p = P[l]
        hn = _rms_fwd(h, p["norm_g"])
        mm = functools.partial(_matmul, tb=True, out_dtype=BF, tm=TR, tk=D_MODEL, col_major_grid=True)
        ug = mm(hn, p["w_g"], tn=N_G // 2, name="inproj_g")
        ua = mm(hn, p["w_a"], tn=N_A, name="inproj_a")
        ub = mm(hn, p["w_b"], tn=N_B, name="inproj_b")
        uc = mm(hn, p["w_c"], tn=N_C, name="inproj_c")
        nxt = shards[l + 1] if shards is not None and l + 1 < DEPTH else None
        carry = (lambda part: ("gather", part)) if nxt is not None else (lambda part: None)
        res_a = _conv_fwd(ua, p["conv_w"], p["conv_vec"], carry(nxt and nxt[1:2]))
        res_b = _hg_fwd(ub, p["lb"], p["gn4"], carry(nxt and nxt[0:1]))
        res_c = _swa_fwd(uc, p["qg"], p["kg"], p["sinks"], carry(nxt and nxt[2:]))
        (ya, yconv), (yb, o_hg, s_hg), (yc, o_at, lse) = res_a[:2], res_b[:3], res_c[:3]
        if nxt is not None:
            P.append(prep(l + 1, [*res_b[3:], *res_a[2:], *res_c[3:]]))
        h_new, za, zb, zc, mixed = _mix_fwd(h, ya, yb, yc, ug, p["w_ao"], p["w_bo"], p["w_co"], p["w_out"])
        saved.append(dict(h=h, hn=hn, ug=ug, ua=ua, ub=ub, uc=uc, ya=ya, yconv=yconv, yb=yb, o_hg=o_hg, s_hg=s_hg,
                          yc=yc, o_at=o_at, lse=lse, za=za, zb=zb, zc=zc, mixed=mixed))
        h = h_new

    dh, loss8 = _loss_head(h, tgt_pad, seq)

    grads = [None] * DEPTH
    parts = [[None, None] for _ in range(DEPTH)]
    pending = None
    tk_dw = 2 * TR if Lp % (2 * TR) == 0 else TR
    for l in reversed(range(DEPTH)):
        p, s = P[l], saved[l]
        dug, dza, dzb, dzc, dya, dyb, dyc = _mix_bwd(dh, s["za"], s["zb"], s["zc"], s["ug"],
                                                      p["w_ao"], p["w_bo"], p["w_co"], p["w_out"])
        tnmm = functools.partial(_matmul, ta=True, out_dtype=F32, tk=tk_dw)
        g = {}
        g["w_out"] = tnmm(s["mixed"], dh, tm=D_MODEL, tn=D_MODEL, name="dw_out")
        g["w_ao"] = tnmm(s["ya"], dza, tm=512, tn=D_MODEL, name="dw_ao")
        g["w_bo"] = tnmm(s["yb"], dzb, tm=512, tn=D_MODEL, name="dw_bo")
        g["w_co"] = tnmm(s["yc"], dzc, tm=512, tn=D_MODEL, name="dw_co")
        dua, g["conv_w"], g["conv_vec"] = _conv_bwd(s["ua"], s["yconv"], dya, p["conv_w"], p["conv_vec"])
        carry = ("scatter", pending[1]) if pending is not None else None
        res = _hg_bwd(s["ub"], p["lb"], p["gn4"], s["o_hg"], s["s_hg"], dyb, carry)
        dub, g["hg_small"] = res[:2]
        if carry is not None:
            parts[pending[0]][1] = res[2:]
        duc, g["at_gain"], g["at_sink"] = _swa_bwd(s["uc"], p["qg"], p["kg"], p["sinks"], s["o_at"], s["lse"], dyc)
        g["w_g"] = tnmm(dug, s["hn"], tm=N_G // 2, tn=D_MODEL, name="dw_in_g")
        g["w_a"] = tnmm(dua, s["hn"], tm=N_A, tn=D_MODEL, name="dw_in_a")
        g["w_b"] = tnmm(dub, s["hn"], tm=N_B, tn=D_MODEL, name="dw_in_b")
        g["w_c"] = tnmm(duc, s["hn"], tm=N_C, tn=D_MODEL, name="dw_in_c")
        first, second = pack(g) if pack is not None else (None, None)
        if first is not None and l == 0:
            first, second = first + second, []
        res = _inproj_bwd([dug, dua, dub, duc], [p["w_g"], p["w_a"], p["w_b"], p["w_c"]], s["h"], dh, p["norm_g"],
                          ("scatter", first) if first is not None else None)
        dh, g["norm_g"] = res[:2]
        grads[l] = g
        if pack is not None:
            parts[l] = [res[2:3], res[3:]] if l == 0 else [res[2:], None]
            pending = (l, second) if l > 0 else None
    return loss8, dh, grads, parts


def _split_w_in(wt):
    return dict(w_a=wt[0:1536], w_b=wt[1536:3584],
                w_c=jnp.concatenate([wt[3584:4096], wt[4352:4864], wt[4096:4352]], axis=0), w_g=wt[4864:7936])


def _join_w_in(g):
    c = g["w_c"]
    return jnp.concatenate([g["w_a"], g["w_b"], c[0:512], c[1024:1280], c[512:1024], g["w_g"]], axis=0)


def _attn_small(g):
    return (g["at_gain"][0].reshape(ATT_Q_HEADS, ATT_HD).sum(0),
            g["at_gain"][1, 0:128].reshape(ATT_KV_HEADS, ATT_HD).sum(0), g["at_sink"].sum(1))


_SMALL = (("norm_g", 8), ("meta", 32), ("conv_w", 32 * DEPTH), ("conv_b", 8), ("conv_ln_g", 8), ("conv_ln_b", 8),
          ("lb", 8), ("hg_norm_g", 8), ("q_norm_g", 8), ("k_norm_g", 8), ("sinks", 8))


def _small_offsets():
    off, o = {}, 0
    for name, rows in _SMALL:
        off[name] = (o, rows)
        o += rows
    return off, o


def _pack_small(d):
    parts = []
    for name, rows in _SMALL:
        a = d[name]
        parts.append(jnp.pad(a, ((0, rows - a.shape[0]), (0, 512 - a.shape[1]))))
    return jnp.concatenate(parts, axis=0)


def kernel(x, meta_tokens, norm_g, w_in, conv_w, conv_b, conv_ln_g, conv_ln_b, w_conv_out, hg_lower_bounds, hg_norm_g, w_hg_out, q_norm_g, k_norm_g, attn_sinks, w_att_out, w_out, loss_target, m_meta_tokens, m_norm_g, m_w_in, m_conv_w, m_conv_b, m_conv_ln_g, m_conv_ln_b, m_w_conv_out, m_hg_lower_bounds, m_hg_norm_g, m_w_hg_out, m_q_norm_g, m_k_norm_g, m_attn_sinks, m_w_att_out, m_w_out, v_meta_tokens, v_norm_g, v_w_in, v_conv_w, v_conv_b, v_conv_ln_g, v_conv_ln_b, v_w_conv_out, v_hg_lower_bounds, v_hg_norm_g, v_w_hg_out, v_q_norm_g, v_k_norm_g, v_attn_sinks, v_w_att_out, v_w_out):
    xi, yi = lax.axis_index("x"), lax.axis_index("y")
    chip = 2 * xi + yi
    NS = w_in.shape[2]
    CS = conv_w.shape[2]
    MS = meta_tokens.shape[1]

    half = NS // 2
    w_in_t, m_w_in_t, v_w_in_t = (jnp.swapaxes(t, 1, 2) for t in (w_in, m_w_in, v_w_in))
    shards = [[w_in_t[l, :half].astype(BF), w_in_t[l, half:].astype(BF), w_conv_out[l].astype(BF),
               w_hg_out[l].astype(BF), w_att_out[l].astype(BF), w_out[l].astype(BF)] for l in range(DEPTH)]
    *first, g_meta, g_convw = _exchange_chips(
        "gather", shards[0] + [meta_tokens, conv_w.reshape(DEPTH * CONV_WIDTH, CS)])
    cols = lambda g: g.transpose(1, 0, 2).reshape(g.shape[1], -1)
    meta_f = cols(g_meta)
    convw_f = cols(g_convw).reshape(DEPTH, CONV_WIDTH, D_CONV)
    lb_all = _lb_fwd(hg_lower_bounds)

    def prep(l, gathered):
        g_win_top, g_win_bot, g_wao, g_wbo, g_wco, g_wout = gathered
        p = _split_w_in(jnp.concatenate([g_win_top, g_win_bot], axis=1).reshape(4 * NS, D_MODEL))
        p.update(w_ao=cols(g_wao), w_bo=cols(g_wbo), w_co=cols(g_wco), w_out=g_wout.reshape(D_MODEL, D_MODEL),
                 norm_g=norm_g[l:l + 1], conv_w=convw_f[l],
                 conv_vec=_pad8(jnp.stack([conv_b[l], conv_ln_g[l], conv_ln_b[l]])),
                 lb=lb_all[l:l + 1], gn4=jnp.tile(hg_norm_g[l:l + 1], (1, HG_HEADS)),
                 qg=jnp.tile(q_norm_g[l:l + 1], (1, ATT_Q_HEADS)), kg=jnp.tile(k_norm_g[l:l + 1], (1, ATT_KV_HEADS)),
                 sinks=attn_sinks[l:l + 1])
        return p

    shard_cols = lambda a: a.reshape(a.shape[0], 4, -1).transpose(1, 0, 2)
    def pack(g):
        win = _join_w_in(g).reshape(4, NS, D_MODEL).astype(BF)
        return [win[:, :half]], [win[:, half:], shard_cols(g["w_ao"]).astype(BF), shard_cols(g["w_bo"]).astype(BF),
                                 shard_cols(g["w_co"]).astype(BF), g["w_out"].reshape(4, MS, D_MODEL).astype(BF)]

    loss8, dh0, grads, parts = _local_step(x[0], loss_target[0], meta_f, [prep(0, first)], shards, prep, pack)
    seq = x.shape[1]
    grad_x = dh0[CHUNK:CHUNK + seq][None]
    loss = lax.psum(loss8[0, 0], ("x", "y", "c"))

    sum4 = functools.partial(_sum4, name="sum_chips")
    mine = [jnp.concatenate([t for l in range(DEPTH) for t in (sum4(parts[l][0][0]), sum4(parts[l][1][0]))], axis=0)]
    mine += [jnp.concatenate([sum4(parts[l][1][a]) for l in range(DEPTH)], axis=0) for a in range(1, 5)]
    theirs = _swap_cores(mine)

    dlb_all = jnp.concatenate([grads[l]["hg_small"][0:1] for l in range(DEPTH)], axis=0)
    small = dict(
        norm_g=jnp.concatenate([grads[l]["norm_g"][0:1] for l in range(DEPTH)], axis=0).reshape(8, 512),
        meta=dh0[META_PAD:CHUNK].reshape(32, 512),
        conv_w=jnp.concatenate([grads[l]["conv_w"] for l in range(DEPTH)], axis=0),
        conv_b=jnp.concatenate([grads[l]["conv_vec"][0:1] for l in range(DEPTH)], axis=0),
        conv_ln_g=jnp.concatenate([grads[l]["conv_vec"][1:2] for l in range(DEPTH)], axis=0),
        conv_ln_b=jnp.concatenate([grads[l]["conv_vec"][2:3] for l in range(DEPTH)], axis=0),
        lb=_lb_bwd(hg_lower_bounds, dlb_all),
        hg_norm_g=jnp.concatenate([grads[l]["hg_small"][1:2].reshape(HG_HEADS, HG_D).sum(0, keepdims=True)
                                   for l in range(DEPTH)], axis=0),
        q_norm_g=jnp.stack([_attn_small(grads[l])[0] for l in range(DEPTH)]),
        k_norm_g=jnp.stack([_attn_small(grads[l])[1] for l in range(DEPTH)]),
        sinks=jnp.stack([_attn_small(grads[l])[2] for l in range(DEPTH)]),
    )
    gsum = _allsum_small(_pack_small(small))
    off, _ = _small_offsets()

    def take(name, rows, cols):
        o, _ = off[name]
        return gsum[o:o + rows, 0:cols]

    g_meta_full = take("meta", 32, 512).reshape(N_META, D_MODEL)
    g_convw_full = take("conv_w", 32 * DEPTH, 512).reshape(DEPTH, 32, 512)[:, :CONV_WIDTH]
    small_grads = dict(
        norm_g=take("norm_g", 8, 512),
        meta=lax.dynamic_slice_in_dim(g_meta_full, chip * MS, MS, axis=1),
        conv_w=lax.dynamic_slice_in_dim(g_convw_full, chip * CS, CS, axis=2).reshape(DEPTH * CONV_WIDTH, CS),
        conv_b=take("conv_b", DEPTH, 512), conv_ln_g=take("conv_ln_g", DEPTH, 512), conv_ln_b=take("conv_ln_b", DEPTH, 512),
        lb=take("lb", DEPTH, 512), hg_norm_g=take("hg_norm_g", DEPTH, HG_D), q_norm_g=take("q_norm_g", DEPTH, ATT_HD),
        k_norm_g=take("k_norm_g", DEPTH, ATT_HD), sinks=take("sinks", DEPTH, ATT_Q_HEADS))

    def big_update(w, m, v, a, b, name):
        return _adamw(w, m, v, a.reshape(w.shape), b.reshape(w.shape), name)

    res = {}
    res["w_in"] = [jnp.swapaxes(t, 1, 2) for t in big_update(w_in_t, m_w_in_t, v_w_in_t, mine[0], theirs[0], "adamw_w_in")]
    res["w_conv_out"] = big_update(w_conv_out, m_w_conv_out, v_w_conv_out, mine[1], theirs[1], "adamw_w_ao")
    res["w_hg_out"] = big_update(w_hg_out, m_w_hg_out, v_w_hg_out, mine[2], theirs[2], "adamw_w_bo")
    res["w_att_out"] = big_update(w_att_out, m_w_att_out, v_w_att_out, mine[3], theirs[3], "adamw_w_co")
    res["w_out"] = big_update(w_out, m_w_out, v_w_out, mine[4], theirs[4], "adamw_w_out")

    small_w = dict(meta=(meta_tokens, m_meta_tokens, v_meta_tokens), norm_g=(norm_g, m_norm_g, v_norm_g),
                   conv_w=(conv_w, m_conv_w, v_conv_w), conv_b=(conv_b, m_conv_b, v_conv_b),
                   conv_ln_g=(conv_ln_g, m_conv_ln_g, v_conv_ln_g), conv_ln_b=(conv_ln_b, m_conv_ln_b, v_conv_ln_b),
                   lb=(hg_lower_bounds, m_hg_lower_bounds, v_hg_lower_bounds),
                   hg_norm_g=(hg_norm_g, m_hg_norm_g, v_hg_norm_g), q_norm_g=(q_norm_g, m_q_norm_g, v_q_norm_g),
                   k_norm_g=(k_norm_g, m_k_norm_g, v_k_norm_g), sinks=(attn_sinks, m_attn_sinks, v_attn_sinks))
    view = lambda n, t: t.reshape(-1, 512) if n == "norm_g" else t.reshape(-1, t.shape[-1])
    pw, pm, pv = (_pack_rows([view(n, small_w[n][k]) for n in small_w]) for k in range(3))
    pg = _pack_rows([small_grads[n] for n in small_w])
    packed = [t[0] for t in _adamw(pw[None], pm[None], pv[None], pg[None], None, "adamw_small")]
    o = 0
    for n in small_w:
        r, cdim = view(n, small_w[n][0]).shape
        res[n] = [t[o:o + r, 0:cdim].reshape(small_w[n][0].shape) for t in packed]
        o += -(-r // 8) * 8

    order = [("meta", None), ("norm_g", None), ("w_in", None), ("conv_w", None), ("conv_b", None), ("conv_ln_g", None),
             ("conv_ln_b", None), ("w_conv_out", None), ("lb", None), ("hg_norm_g", None), ("w_hg_out", None),
             ("q_norm_g", None), ("k_norm_g", None), ("sinks", None), ("w_att_out", None), ("w_out", None)]
    outs = [loss, grad_x]
    for k in range(4):
        outs += [res[n][k] for n, _ in order]
    return tuple(outs)


def _pack_rows(arrs):
    parts = []
    for a in arrs:
        r = (-a.shape[0]) % 8
        parts.append(jnp.pad(a, ((0, r), (0, 512 - a.shape[1]))))
    return jnp.concatenate(parts, axis=0)
```

```python
import functools

import jax
import jax.numpy as jnp
from jax import lax
from jax.experimental import pallas as pl
from jax.experimental.pallas import tpu as pltpu

F32 = jnp.float32
BF = jnp.bfloat16

D_MODEL = 1024
DEPTH = 4
CHUNK = 64
N_META = 16
META_PAD = CHUNK - N_META
D_CONV = 512
CONV_WIDTH = 31
HG_HEADS = 4
HG_D = 128
ATT_Q_HEADS = 8
ATT_KV_HEADS = 2
ATT_HD = 64
ATT_GROUP = ATT_Q_HEADS // ATT_KV_HEADS
EPS = 1e-6
F_FLOOR = 1e-30
NEG = -1e30

ADAM_LR = 0.001
ADAM_B1 = 0.9
ADAM_B2 = 0.999
ADAM_EPS = 1e-08
ADAM_WD = 0.01
ADAM_STEP = 10

TR = 640
TRM = TR // 2
CONV_RB = 32
QB = 128
HALO = 128
VMEM_LIMIT = 56 * 1024 * 1024

N_G, N_A, N_B, N_C = 3 * D_MODEL, 3 * D_CONV, 4 * 512, 2 * 512 + 2 * 128

MESH = pl.DeviceIdType.MESH


def _cp(sem=None, vmem=VMEM_LIMIT, **kw):
    if sem is None:
        return pltpu.CompilerParams(vmem_limit_bytes=vmem, **kw)
    return pltpu.CompilerParams(dimension_semantics=sem, vmem_limit_bytes=vmem, **kw)


def _nn(a, b):
    return lax.dot_general(a, b, (((1,), (0,)), ((), ())), preferred_element_type=F32)


def _nt(a, b):
    return lax.dot_general(a, b, (((1,), (1,)), ((), ())), preferred_element_type=F32)


def _tn(a, b):
    return lax.dot_general(a, b, (((0,), (0,)), ((), ())), preferred_element_type=F32)


def _sig(x):
    return jax.nn.sigmoid(x)


def _silu(x):
    return x * _sig(x)


def _dsilu(x):
    s = _sig(x)
    return s * (1.0 + x * (1.0 - s))


def _mm_split(t, x):
    hi = x.astype(BF)
    lo = (x - hi.astype(F32)).astype(BF)
    return _nn(t, hi) + _nn(t, lo)


def _chunk_tri(n, upper):
    r = lax.broadcasted_iota(jnp.int32, (n, n), 0)
    c = lax.broadcasted_iota(jnp.int32, (n, n), 1)
    same = jnp.right_shift(r, 6) == jnp.right_shift(c, 6)
    tri = (c >= r) if upper else (c <= r)
    return jnp.where(same & tri, 1.0, 0.0).astype(BF)


def _matmul(a, b, *, ta=False, tb=False, out_dtype, tm, tn, tk, name, col_major_grid=False):
    if ta:
        K, M = a.shape
    else:
        M, K = a.shape
    N = b.shape[0] if tb else b.shape[1]
    assert M % tm == 0 and N % tn == 0 and K % tk == 0, (name, M, N, K, tm, tn, tk)
    nk = K // tk
    if col_major_grid:
        grid = (N // tn, M // tm, nk)
        ij = lambda g0, g1: (g1, g0)
    else:
        grid = (M // tm, N // tn, nk)
        ij = lambda g0, g1: (g0, g1)
    if ta:
        a_spec = pl.BlockSpec((tk, tm), lambda g0, g1, k: (k, ij(g0, g1)[0]))
    else:
        a_spec = pl.BlockSpec((tm, tk), lambda g0, g1, k: (ij(g0, g1)[0], k))
    if tb:
        b_spec = pl.BlockSpec((tn, tk), lambda g0, g1, k: (ij(g0, g1)[1], k))
    else:
        b_spec = pl.BlockSpec((tk, tn), lambda g0, g1, k: (k, ij(g0, g1)[1]))
    o_spec = pl.BlockSpec((tm, tn), lambda g0, g1, k: ij(g0, g1))
    dims = (((0 if ta else 1,), (1 if tb else 0,)), ((), ()))
    use_acc = nk > 1 and out_dtype != F32

    def body(a_ref, b_ref, o_ref, *scr):
        k = pl.program_id(2)
        p = lax.dot_general(a_ref[...].astype(BF), b_ref[...].astype(BF), dims, preferred_element_type=F32)
        if nk == 1:
            o_ref[...] = p.astype(out_dtype)
        else:
            acc = scr[0] if use_acc else o_ref

            @pl.when(k == 0)
            def _():
                acc[...] = p

            @pl.when(k > 0)
            def _():
                acc[...] += p

            if use_acc:
                @pl.when(k == nk - 1)
                def _():
                    o_ref[...] = acc[...].astype(out_dtype)

    return pl.pallas_call(
        body, name=name, grid=grid, in_specs=[a_spec, b_spec], out_specs=o_spec,
        out_shape=jax.ShapeDtypeStruct((M, N), out_dtype),
        scratch_shapes=[pltpu.VMEM((tm, tn), F32)] if use_acc else [],
        compiler_params=_cp(("parallel", "parallel", "arbitrary")),
    )(a, b)


def _rms_fwd(h, g):
    Lp = h.shape[0]

    def body(h_ref, g_ref, o_ref):
        x = h_ref[...]
        r = lax.rsqrt(jnp.mean(x * x, axis=-1, keepdims=True) + EPS)
        o_ref[...] = (x * r * g_ref[...]).astype(BF)

    return pl.pallas_call(
        body, name="rms_fwd", grid=(Lp // TR,),
        in_specs=[pl.BlockSpec((TR, D_MODEL), lambda i: (i, 0)), pl.BlockSpec((1, D_MODEL), lambda i: (0, 0))],
        out_specs=pl.BlockSpec((TR, D_MODEL), lambda i: (i, 0)),
        out_shape=jax.ShapeDtypeStruct((Lp, D_MODEL), BF),
        compiler_params=_cp(("parallel",)),
    )(h, g)


def _glu(ua, row):
    a = ua[:, 0:D_CONV].astype(F32)
    gl = ua[:, D_CONV:2 * D_CONV].astype(F32)
    return jnp.where(row >= META_PAD, a * _sig(gl), 0.0)


_SH_ROWS = TR + CHUNK - 8


def _fill_shifts(src, sh):
    for b in range(1, 8):
        sh[b - 1] = src[pl.ds(b, _SH_ROWS), :]


def _shifted(src, sh, start, n):
    b = start % 8
    if b == 0:
        return src[pl.ds(start, n), :]
    return sh[b - 1, pl.ds(start - b, n), :]


def _conv_fwd(ua, cw, cvec, carry=None):
    Lp = ua.shape[0]
    nt = Lp // TR
    hb = TR // CHUNK

    def body(cur_ref, halo_ref, w_ref, v_ref, ya_ref, yc_ref, ext, sh):
        i = pl.program_id(0)
        row = i * TR + lax.broadcasted_iota(jnp.int32, (TR, 1), 0)
        hrow = i * TR - CHUNK + lax.broadcasted_iota(jnp.int32, (CHUNK, 1), 0)
        ext[pl.ds(0, CHUNK), :] = jnp.where(i > 0, _glu(halo_ref[...], hrow), 0.0)
        ext[pl.ds(CHUNK, TR), :] = _glu(cur_ref[...], row)
        _fill_shifts(ext, sh)
        for rb in range(TR // CONV_RB):
            r0 = rb * CONV_RB
            rows = pl.ds(r0, CONV_RB)
            acc = jnp.zeros((CONV_RB, D_CONV), F32)
            for j in range(CONV_WIDTH):
                acc = acc + _shifted(ext, sh, r0 + CHUNK - (CONV_WIDTH - 1) + j, CONV_RB) * w_ref[j:j + 1, :]
            y = acc + v_ref[0:1, :]
            yc_ref[rows, :] = y
            mu = jnp.mean(y, axis=-1, keepdims=True)
            d = y - mu
            var = jnp.mean(d * d, axis=-1, keepdims=True)
            yn = d * lax.rsqrt(var + EPS) * v_ref[1:2, :] + v_ref[2:3, :]
            ya_ref[rows, :] = (_silu(yn) * _silu(cur_ref[rows, 2 * D_CONV:3 * D_CONV].astype(F32))).astype(BF)

    in_specs = [pl.BlockSpec((TR, N_A), lambda i: (i, 0)),
                pl.BlockSpec((CHUNK, N_A), lambda i: (jnp.maximum(i * hb - 1, 0), 0)),
                pl.BlockSpec((CONV_WIDTH, D_CONV), lambda i: (0, 0)),
                pl.BlockSpec((8, D_CONV), lambda i: (0, 0))]
    out_specs = [pl.BlockSpec((TR, D_CONV), lambda i: (i, 0)), pl.BlockSpec((TR, D_CONV), lambda i: (i, 0))]
    out_shape = [jax.ShapeDtypeStruct((Lp, D_CONV), BF), jax.ShapeDtypeStruct((Lp, D_CONV), F32)]
    scratch = [pltpu.VMEM((TR + CHUNK, D_CONV), F32), pltpu.VMEM((7, _SH_ROWS, D_CONV), F32)]
    return _call_carrying(body, "conv_fwd", nt, in_specs, out_specs, out_shape, scratch, (ua, ua, cw, cvec), carry)


def _conv_bwd(ua, yconv, dya, cw, cvec):
    Lp = ua.shape[0]
    nt = Lp // TR
    hb = TR // CHUNK
    nhb = Lp // CHUNK

    def ln_bwd(y, dout, gate, v_ref):
        mu = jnp.mean(y, axis=-1, keepdims=True)
        d = y - mu
        var = jnp.mean(d * d, axis=-1, keepdims=True)
        rstd = lax.rsqrt(var + EPS)
        xhat = d * rstd
        yn = xhat * v_ref[1:2, :] + v_ref[2:3, :]
        dyn = dout * _silu(gate) * _dsilu(yn)
        dxh = dyn * v_ref[1:2, :]
        dyc = rstd * (dxh - jnp.mean(dxh, axis=-1, keepdims=True) - xhat * jnp.mean(dxh * xhat, axis=-1, keepdims=True))
        return dyc, dyn, xhat, yn

    def body(cur_ref, prev_ref, next_ref, yc_ref, ycn_ref, dy_ref, dyn_ref, w_ref, v_ref,
             du_ref, dw_ref, dv_ref, uext, dext, dwacc, ush, dsh):
        i = pl.program_id(0)

        @pl.when(i == 0)
        def _():
            dwacc[...] = jnp.zeros_like(dwacc)
            dv_ref[...] = jnp.zeros_like(dv_ref)

        row = i * TR + lax.broadcasted_iota(jnp.int32, (TR, 1), 0)
        hrow = i * TR - CHUNK + lax.broadcasted_iota(jnp.int32, (CHUNK, 1), 0)
        uext[pl.ds(0, CHUNK), :] = jnp.where(i > 0, _glu(prev_ref[...], hrow), 0.0)
        uext[pl.ds(CHUNK, TR), :] = _glu(cur_ref[...], row)

        s_b = jnp.zeros((1, D_CONV), F32)
        s_g = jnp.zeros((1, D_CONV), F32)
        s_bb = jnp.zeros((1, D_CONV), F32)
        for rb in range(TR // CONV_RB):
            rows = pl.ds(rb * CONV_RB, CONV_RB)
            gate = cur_ref[rows, 2 * D_CONV:3 * D_CONV].astype(F32)
            dout = dy_ref[rows, :].astype(F32)
            dyc, dyn, xhat, yn = ln_bwd(yc_ref[rows, :], dout, gate, v_ref)
            du_ref[rows, 2 * D_CONV:3 * D_CONV] = (dout * _silu(yn) * _dsilu(gate)).astype(BF)
            dext[rows, :] = dyc
            s_b = s_b + jnp.sum(dyc, axis=0, keepdims=True)
            s_g = s_g + jnp.sum(dyn * xhat, axis=0, keepdims=True)
            s_bb = s_bb + jnp.sum(dyn, axis=0, keepdims=True)
        dv_ref[0:1, :] += s_b
        dv_ref[1:2, :] += s_g
        dv_ref[2:3, :] += s_bb
        dyc_n, _, _, _ = ln_bwd(ycn_ref[...], dyn_ref[...].astype(F32),
                                next_ref[:, 2 * D_CONV:3 * D_CONV].astype(F32), v_ref)
        dext[pl.ds(TR, CHUNK), :] = jnp.where(i < nt - 1, dyc_n, 0.0)
        _fill_shifts(uext, ush)
        _fill_shifts(dext, dsh)

        for rb in range(TR // CONV_RB):
            r0 = rb * CONV_RB
            rows = pl.ds(r0, CONV_RB)
            d_blk = dext[rows, :]
            dglu = jnp.zeros((CONV_RB, D_CONV), F32)
            for j in range(CONV_WIDTH):
                dglu = dglu + _shifted(dext, dsh, r0 + CONV_WIDTH - 1 - j, CONV_RB) * w_ref[j:j + 1, :]
                prod = d_blk * _shifted(uext, ush, r0 + CHUNK - (CONV_WIDTH - 1) + j, CONV_RB)
                part = prod[0:8, :]
                for s in range(1, CONV_RB // 8):
                    part = part + prod[8 * s:8 * s + 8, :]
                dwacc[j] += part
            a = cur_ref[rows, 0:D_CONV].astype(F32)
            sg = _sig(cur_ref[rows, D_CONV:2 * D_CONV].astype(F32))
            grow = i * TR + r0 + lax.broadcasted_iota(jnp.int32, (CONV_RB, 1), 0)
            dglu = jnp.where(grow >= META_PAD, dglu, 0.0)
            du_ref[rows, 0:D_CONV] = (dglu * sg).astype(BF)
            du_ref[rows, D_CONV:2 * D_CONV] = (dglu * a * sg * (1.0 - sg)).astype(BF)

        @pl.when(i == nt - 1)
        def _():
            dw_ref[...] = jnp.sum(dwacc[...], axis=1)

    nxt = lambda i: (jnp.minimum(i * hb + hb, nhb - 1), 0)
    return pl.pallas_call(
        body, name="conv_bwd", grid=(nt,),
        in_specs=[pl.BlockSpec((TR, N_A), lambda i: (i, 0)),
                  pl.BlockSpec((CHUNK, N_A), lambda i: (jnp.maximum(i * hb - 1, 0), 0)),
                  pl.BlockSpec((CHUNK, N_A), nxt),
                  pl.BlockSpec((TR, D_CONV), lambda i: (i, 0)),
                  pl.BlockSpec((CHUNK, D_CONV), nxt),
                  pl.BlockSpec((TR, D_CONV), lambda i: (i, 0)),
                  pl.BlockSpec((CHUNK, D_CONV), nxt),
                  pl.BlockSpec((CONV_WIDTH, D_CONV), lambda i: (0, 0)),
                  pl.BlockSpec((8, D_CONV), lambda i: (0, 0))],
        out_specs=[pl.BlockSpec((TR, N_A), lambda i: (i, 0)),
                   pl.BlockSpec((32, D_CONV), lambda i: (0, 0)),
                   pl.BlockSpec((8, D_CONV), lambda i: (0, 0))],
        out_shape=[jax.ShapeDtypeStruct((Lp, N_A), BF), jax.ShapeDtypeStruct((32, D_CONV), F32),
                   jax.ShapeDtypeStruct((8, D_CONV), F32)],
        scratch_shapes=[pltpu.VMEM((TR + CHUNK, D_CONV), F32), pltpu.VMEM((TR + CHUNK, D_CONV), F32),
                        pltpu.VMEM((32, 8, D_CONV), F32), pltpu.VMEM((7, _SH_ROWS, D_CONV), F32),
                        pltpu.VMEM((7, _SH_ROWS, D_CONV), F32)],
        compiler_params=_cp(("arbitrary",)),
    )(ua, ua, ua, yconv, yconv, dya, dya, cw, cvec)


def _hg_gates(ub_ref, lbv, row):
    q = ub_ref[:, 0:512].astype(F32)
    z = ub_ref[:, 512:1024].astype(F32)
    valid = row >= META_PAD
    sig = _sig(z)
    f = lbv + (1.0 - lbv) * sig
    g = jnp.where(valid, jnp.log(jnp.maximum(f, F_FLOOR)), 0.0)
    k = jnp.where(valid, (1.0 - lbv) * (1.0 - sig), 0.0)
    return q, k, g, sig, f


def _hg_chunk_terms(b_c, q_c, k_c):
    bm = b_c[CHUNK // 2 - 1:CHUNK // 2, :]
    bl = b_c[CHUNK - 1:CHUNK, :]
    e1 = jnp.exp(b_c - bm)
    e2 = jnp.exp(bm - b_c)
    e0 = jnp.exp(b_c)
    e3 = jnp.exp(bl - b_c)
    el = jnp.exp(bl)
    return e1, e2, e0, e3, el, q_c * e1, k_c * e2, q_c * e0, k_c * e3


def _hg_fwd(ub, lb, gn4, carry=None):
    Lp = ub.shape[0]
    nt = Lp // TR
    cpt = TR // CHUNK

    def body(ub_ref, lb_ref, gn_ref, yb_ref, o_ref, ss_ref, st, bsc, qsc, ksc, qes, els, ust, tlo):
        i = pl.program_id(0)

        @pl.when(i == 0)
        def _():
            st[...] = jnp.zeros_like(st)
            tlo[...] = _chunk_tri(TR, False)

        row = i * TR + lax.broadcasted_iota(jnp.int32, (TR, 1), 0)
        q, k, g, _, _ = _hg_gates(ub_ref, lb_ref[...], row)
        qsc[...] = _silu(q)
        ksc[...] = k
        bsc[...] = _mm_split(tlo[...], g)
        tri = lax.broadcasted_iota(jnp.int32, (CHUNK, CHUNK), 1) <= lax.broadcasted_iota(jnp.int32, (CHUNK, CHUNK), 0)

        def intra(c, carry):
            rows = pl.ds(pl.multiple_of(c * CHUNK, CHUNK), CHUNK)
            _, _, _, _, el, qe, ke, qE, kd = _hg_chunk_terms(bsc[rows, :], qsc[rows, :], ksc[rows, :])
            qe, ke, kd = qe.astype(BF), ke.astype(BF), kd.astype(BF)
            qes[rows, :] = qE.astype(BF)
            els[c] = jnp.broadcast_to(el, (8, 512))
            sls = [slice(HG_D * h, HG_D * (h + 1)) for h in range(HG_HEADS)]
            v = [ub_ref[rows, 1024 + HG_D * h:1024 + HG_D * (h + 1)] for h in range(HG_HEADS)]
            a = [_nt(qe[:, sl], ke[:, sl]) for sl in sls]
            u = [_tn(v[h], kd[:, sls[h]]) for h in range(HG_HEADS)]
            a = [jnp.where(tri, x, 0.0).astype(BF) for x in a]
            oi = [_nn(a[h], v[h]) for h in range(HG_HEADS)]
            for h in range(HG_HEADS):
                ust[c, h] = u[h]
                o_ref[rows, sls[h]] = oi[h]
            return carry

        lax.fori_loop(0, cpt, intra, 0, unroll=2)

        for h in range(HG_HEADS):
            sl = slice(HG_D * h, HG_D * (h + 1))
            s = st[h]
            for c in range(cpt):
                ss_ref[c, h] = s
                s = els[c, 0:1, sl] * s + ust[c, h]
            st[h] = s

        def inter(c, carry):
            rows = pl.ds(pl.multiple_of(c * CHUNK, CHUNK), CHUNK)
            for h in range(HG_HEADS):
                sl = slice(HG_D * h, HG_D * (h + 1))
                o_ref[rows, sl] += _nt(qes[rows, sl], ss_ref[c, h].astype(BF))
            return carry

        lax.fori_loop(0, cpt, inter, 0, unroll=2)

        gate = ub_ref[:, 1536:2048].astype(F32)
        for h in range(HG_HEADS):
            sl = slice(HG_D * h, HG_D * (h + 1))
            o = o_ref[:, sl]
            r = lax.rsqrt(jnp.mean(o * o, axis=-1, keepdims=True) + EPS)
            yb_ref[:, sl] = (o * r * gn_ref[:, sl] * _silu(gate[:, sl])).astype(BF)

    in_specs = [pl.BlockSpec((TR, N_B), lambda i: (i, 0)), pl.BlockSpec((1, 512), lambda i: (0, 0)),
                pl.BlockSpec((1, 512), lambda i: (0, 0))]
    out_specs = [pl.BlockSpec((TR, 512), lambda i: (i, 0)), pl.BlockSpec((TR, 512), lambda i: (i, 0)),
                 pl.BlockSpec((cpt, HG_HEADS, HG_D, HG_D), lambda i: (i, 0, 0, 0))]
    out_shape = [jax.ShapeDtypeStruct((Lp, 512), BF), jax.ShapeDtypeStruct((Lp, 512), F32),
                 jax.ShapeDtypeStruct((Lp // CHUNK, HG_HEADS, HG_D, HG_D), F32)]
    scratch = [pltpu.VMEM((HG_HEADS, HG_D, HG_D), F32), pltpu.VMEM((TR, 512), F32),
               pltpu.VMEM((TR, 512), F32), pltpu.VMEM((TR, 512), F32), pltpu.VMEM((TR, 512), BF),
               pltpu.VMEM((cpt, 8, 512), F32), pltpu.VMEM((cpt, HG_HEADS, HG_D, HG_D), F32), pltpu.VMEM((TR, TR), BF)]
    return _call_carrying(body, "hgrn_fwd", nt, in_specs, out_specs, out_shape, scratch, (ub, lb, gn4), carry)


def _hg_bwd(ub, lb, gn4, o_save, s_save, dyb, carry=None):
    Lp = ub.shape[0]
    nt = Lp // TR
    cpt = TR // CHUNK

    def body(ub_ref, lb_ref, gn_ref, o_ref, ss_ref, dy_ref, du_ref, ds_ref,
             dst, bsc, qsc, ksc, dosc, dqsc, dksc, dbsc, els, ust, dss, tlo, tup):
        i = pl.program_id(0)
        t = nt - 1 - i

        @pl.when(i == 0)
        def _():
            dst[...] = jnp.zeros_like(dst)
            ds_ref[...] = jnp.zeros_like(ds_ref)
            tlo[...] = _chunk_tri(TR, False)
            tup[...] = _chunk_tri(TR, True)

        lbv = lb_ref[...]
        row = t * TR + lax.broadcasted_iota(jnp.int32, (TR, 1), 0)
        valid = row >= META_PAD
        q, k, g, sig, f = _hg_gates(ub_ref, lbv, row)
        qsc[...] = _silu(q)
        ksc[...] = k
        bsc[...] = _mm_split(tlo[...], g)

        gate = ub_ref[:, 1536:2048].astype(F32)
        dy = dy_ref[...].astype(F32)
        dgn = jnp.zeros((1, 512), F32)
        for h in range(HG_HEADS):
            sl = slice(HG_D * h, HG_D * (h + 1))
            o = o_ref[:, sl]
            r = lax.rsqrt(jnp.mean(o * o, axis=-1, keepdims=True) + EPS)
            ohat = o * r
            don = dy[:, sl] * _silu(gate[:, sl])
            du_ref[:, 1536 + HG_D * h:1536 + HG_D * (h + 1)] = (
                dy[:, sl] * ohat * gn_ref[:, sl] * _dsilu(gate[:, sl])).astype(BF)
            ds_ref[1:2, sl] += jnp.sum(don * ohat, axis=0, keepdims=True)
            gd = don * gn_ref[:, sl]
            dosc[:, sl] = r * (gd - ohat * jnp.mean(gd * ohat, axis=-1, keepdims=True))

        tri = lax.broadcasted_iota(jnp.int32, (CHUNK, CHUNK), 1) <= lax.broadcasted_iota(jnp.int32, (CHUNK, CHUNK), 0)
        last = lax.broadcasted_iota(jnp.int32, (CHUNK, 1), 0) == CHUNK - 1

        def incr(c, carry):
            rows = pl.ds(pl.multiple_of(c * CHUNK, CHUNK), CHUNK)
            b_c = bsc[rows, :]
            qE_b = (qsc[rows, :] * jnp.exp(b_c)).astype(BF)
            els[c] = jnp.broadcast_to(jnp.exp(b_c[CHUNK - 1:CHUNK, :]), (8, 512))
            do_c = dosc[rows, :].astype(BF)
            for h in range(HG_HEADS):
                sl = slice(HG_D * h, HG_D * (h + 1))
                ust[c, h] = _tn(do_c[:, sl], qE_b[:, sl])
            return carry

        lax.fori_loop(0, cpt, incr, 0, unroll=2)

        for h in range(HG_HEADS):
            sl = slice(HG_D * h, HG_D * (h + 1))
            d_s = dst[h]
            for c in reversed(range(cpt)):
                dss[c, h] = d_s
                d_s = els[c, 0:1, sl] * d_s + ust[c, h]
            dst[h] = d_s

        def chunk(c, carry):
            r0 = pl.multiple_of(c * CHUNK, CHUNK)
            rows = pl.ds(r0, CHUNK)
            e1, e2, e0, e3, el, qe, ke, qE, kd = _hg_chunk_terms(bsc[rows, :], qsc[rows, :], ksc[rows, :])
            qe_b, ke_b, kd_b = qe.astype(BF), ke.astype(BF), kd.astype(BF)
            do_c = dosc[rows, :].astype(BF)
            hs = range(HG_HEADS)
            sls = [slice(HG_D * h, HG_D * (h + 1)) for h in hs]
            v = [ub_ref[rows, 1024 + HG_D * h:1024 + HG_D * (h + 1)] for h in hs]
            do = [do_c[:, sl] for sl in sls]
            a = [_nt(qe_b[:, sl], ke_b[:, sl]) for sl in sls]
            da = [_nt(do[h], v[h]) for h in hs]
            dqE = [_nn(do[h], ss_ref[c, h].astype(BF)) for h in hs]
            dkd = [_nn(v[h], dss[c, h].astype(BF)) for h in hs]
            dv2 = [_nt(kd_b[:, sls[h]], dss[c, h].astype(BF)) for h in hs]
            a = [jnp.where(tri, x, 0.0).astype(BF) for x in a]
            da = [jnp.where(tri, x, 0.0).astype(BF) for x in da]
            dv = [_tn(a[h], do[h]) + dv2[h] for h in hs]
            dqe = [_nn(da[h], ke_b[:, sls[h]]) for h in hs]
            dke = [_tn(da[h], qe_b[:, sls[h]]) for h in hs]
            for h in hs:
                sl = sls[h]
                del_h = jnp.sum(ss_ref[c, h] * dss[c, h], axis=0, keepdims=True)
                dqsc[rows, sl] = dqE[h] * e0[:, sl] + dqe[h] * e1[:, sl]
                dksc[rows, sl] = dke[h] * e2[:, sl] + dkd[h] * e3[:, sl]
                tkd = dkd[h] * kd[:, sl]
                dbl = jnp.sum(tkd, axis=0, keepdims=True) + del_h * el[:, sl]
                dbsc[rows, sl] = (dqE[h] * qE[:, sl] + dqe[h] * qe[:, sl] - dke[h] * ke[:, sl] - tkd
                                  + jnp.where(last, dbl, 0.0))
                du_ref[rows, 1024 + HG_D * h:1024 + HG_D * (h + 1)] = dv[h].astype(BF)
            return carry

        lax.fori_loop(0, cpt, chunk, 0, unroll=2)

        dg = _mm_split(tup[...], dbsc[...])
        df = jnp.where(valid & (f > F_FLOOR), dg / f, 0.0)
        dk = jnp.where(valid, dksc[...], 0.0)
        dsig = (df - dk) * (1.0 - lbv)
        ds_ref[0:1, :] += jnp.sum((df - dk) * (1.0 - sig), axis=0, keepdims=True)
        du_ref[:, 512:1024] = (dsig * sig * (1.0 - sig)).astype(BF)
        du_ref[:, 0:512] = (dqsc[...] * _dsilu(q)).astype(BF)

    rev = lambda i: (nt - 1 - i, 0)
    in_specs = [pl.BlockSpec((TR, N_B), rev), pl.BlockSpec((1, 512), lambda i: (0, 0)),
                pl.BlockSpec((1, 512), lambda i: (0, 0)), pl.BlockSpec((TR, 512), rev),
                pl.BlockSpec((cpt, HG_HEADS, HG_D, HG_D), lambda i: (nt - 1 - i, 0, 0, 0)),
                pl.BlockSpec((TR, 512), rev)]
    out_specs = [pl.BlockSpec((TR, N_B), rev), pl.BlockSpec((8, 512), lambda i: (0, 0))]
    out_shape = [jax.ShapeDtypeStruct((Lp, N_B), BF), jax.ShapeDtypeStruct((8, 512), F32)]
    states = pltpu.VMEM((cpt, HG_HEADS, HG_D, HG_D), F32)
    scratch = ([pltpu.VMEM((HG_HEADS, HG_D, HG_D), F32)] + [pltpu.VMEM((TR, 512), F32)] * 7
               + [pltpu.VMEM((cpt, 8, 512), F32), states, states, pltpu.VMEM((TR, TR), BF), pltpu.VMEM((TR, TR), BF)])
    return _call_carrying(body, "hgrn_bwd", nt, in_specs, out_specs, out_shape, scratch,
                          (ub, lb, gn4, o_save, s_save, dyb), carry)


_KCOL = (2 * 512) // 128
_VCOL = _KCOL + 1


def _swa_in_specs(nt, rev):
    tile = (lambda i: nt - 1 - i) if rev else (lambda i: i)
    hpt = TR // HALO
    return [
        pl.BlockSpec((TR, 512), lambda i: (tile(i), 0)),
        pl.BlockSpec((TR, 512), lambda i: (tile(i), 1)),
        pl.BlockSpec((TR, 128), lambda i: (tile(i), _KCOL)),
        pl.BlockSpec((TR, 128), lambda i: (tile(i), _VCOL)),
        pl.BlockSpec((HALO, 128), lambda i: (jnp.maximum(tile(i) * hpt - 1, 0), _KCOL)),
        pl.BlockSpec((HALO, 128), lambda i: (jnp.maximum(tile(i) * hpt - 1, 0), _VCOL)),
        pl.BlockSpec((CHUNK, 128), lambda i: (0, _KCOL)),
        pl.BlockSpec((CHUNK, 128), lambda i: (0, _VCOL)),
        pl.BlockSpec((1, 512), lambda i: (0, 0)),
        pl.BlockSpec((1, 128), lambda i: (0, 0)),
        pl.BlockSpec((1, ATT_Q_HEADS), lambda i: (0, 0)),
    ]


_WROWS = 2 * CHUNK + HALO + TR
_W0 = 2 * CHUNK
_C0 = _W0 + HALO
_SCALE = ATT_HD ** -0.5


def _group_ones(n):
    r = lax.broadcasted_iota(jnp.int32, (n, n), 0)
    c = lax.broadcasted_iota(jnp.int32, (n, n), 1)
    return jnp.where(jnp.right_shift(r, 6) == jnp.right_shift(c, 6), 1.0, 0.0).astype(BF)


def _group_mean(x, ones):
    hi = x.astype(BF)
    lo = (x - hi.astype(F32)).astype(BF)
    return (_nn(hi, ones) + _nn(lo, ones)) * (1.0 / ATT_HD)


def _head_rms(x, ones):
    r = lax.rsqrt(_group_mean(x * x, ones) + EPS)
    return x * r, r


def _swa_windows(kc_ref, vc_ref, kh_ref, vh_ref, km_ref, vm_ref, kg2, ones, kwin, krwin, vwin, vrwin):
    meta = pl.ds(META_PAD, N_META)
    for (k, v, r0, n) in ((km_ref[meta, :], vm_ref[meta, :], 0, N_META), (kh_ref[...], vh_ref[...], _W0, HALO),
                          (kc_ref[...], vc_ref[...], _C0, TR)):
        xhat, _ = _head_rms(k.astype(F32), ones)
        kn = xhat * kg2
        kwin[pl.ds(r0, n), :] = kn.astype(BF)
        krwin[pl.ds(r0, n), :] = pltpu.roll(kn, ATT_HD, 1).astype(BF)
        vwin[pl.ds(r0, n), :] = v
        if vrwin is not None:
            vrwin[pl.ds(r0, n), :] = pltpu.roll(v.astype(F32), ATT_HD, 1).astype(BF)
    zero = jnp.zeros((_W0 - N_META, 128), BF)
    for w in (kwin, krwin, vwin, vrwin):
        if w is not None:
            w[pl.ds(N_META, _W0 - N_META), :] = zero


def _swa_masks_t(t, qb):
    q0 = t * TR + qb * QB
    qc = jnp.right_shift(q0 + lax.broadcasted_iota(jnp.int32, (1, QB), 1), 6)
    kabs = q0 - HALO + lax.broadcasted_iota(jnp.int32, (QB + HALO, 1), 0)
    kc = jnp.right_shift(kabs + HALO, 6) - HALO // CHUNK
    mask_w = (kc <= qc) & (kc >= qc - 2) & (kabs >= META_PAD)
    return qc > 2, mask_w


def _swa_park(dtype):
    return [pltpu.VMEM((ATT_Q_HEADS, N_META, QB), dtype), pltpu.VMEM((ATT_Q_HEADS, QB + HALO, QB), dtype)]


def _split_heads(x, lane_hi):
    return jnp.where(lane_hi, 0.0, x).astype(BF), jnp.where(lane_hi, x, 0.0).astype(BF)


def _call_carrying(body, name, nt, in_specs, out_specs, out_shape, scratch, args, carry):
    if carry is None:
        return pl.pallas_call(body, name=name, grid=(nt,), in_specs=in_specs, out_specs=out_specs, out_shape=out_shape,
                              scratch_shapes=scratch, compiler_params=_cp(("arbitrary",)))(*args)
    kind, arrs = carry
    n = len(arrs)
    return pl.pallas_call(
        _carry_exchange(body, len(in_specs), len(out_specs), nt, kind, n), name=name + "_" + kind, grid=(nt,),
        in_specs=in_specs + [_ANY] * n, out_specs=out_specs + [_ANY] * n,
        out_shape=out_shape + _exchange_out_shapes(kind, arrs), scratch_shapes=scratch + _exchange_sems(n),
        compiler_params=_cp(("arbitrary",), has_side_effects=True),
    )(*args, *arrs)


def _swa_fwd(uc, qg8, kg2, sinks, carry=None):
    Lp = uc.shape[0]
    nt = Lp // TR
    nqb = TR // QB

    def body(q_ref, g_ref, kc_ref, vc_ref, kh_ref, vh_ref, km_ref, vm_ref, qg_ref, kg_ref, sk_ref,
             yc_ref, o_ref, lse_ref, kwin, krwin, vwin, vt, qlo, qhi, ot, s_m, s_w, p_m, p_w):
        t = pl.program_id(0)
        _swa_windows(kc_ref, vc_ref, kh_ref, vh_ref, km_ref, vm_ref, kg_ref[...], _group_ones(128),
                     kwin, krwin, vwin, None)
        vt[...] = vwin[...].T
        xhat, _ = _head_rms(q_ref[...].astype(F32), _group_ones(512))
        lane_hi = (lax.broadcasted_iota(jnp.int32, (1, 512), 1) & ATT_HD) != 0
        lo, hi = _split_heads(xhat * qg_ref[...] * _SCALE, lane_hi)
        qlo[...] = lo
        qhi[...] = hi
        for qb in range(nqb):
            rows = pl.ds(qb * QB, QB)
            wrows = pl.ds(_W0 + qb * QB, QB + HALO)
            mrows = pl.ds(0, N_META)
            mask_m, mask_w = _swa_masks_t(t, qb)
            for j in range(ATT_Q_HEADS):
                p, e = j // 2, j % 2
                ks = kwin if e == j // ATT_GROUP else krwin
                qp = (qlo, qhi)[e][rows, 128 * p:128 * (p + 1)]
                s_m[j] = _nt(ks[mrows, :], qp)
                s_w[j] = _nt(ks[wrows, :], qp)
            inv = []
            for j in range(ATT_Q_HEADS):
                sm = jnp.where(mask_m, s_m[j], NEG)
                sw = jnp.where(mask_w, s_w[j], NEG)
                sink = sk_ref[:, j:j + 1]
                m = jnp.maximum(jnp.maximum(jnp.max(sm, axis=0, keepdims=True),
                                            jnp.max(sw, axis=0, keepdims=True)), sink)
                em = jnp.exp(sm - m)
                ew = jnp.exp(sw - m)
                den = jnp.sum(em, axis=0, keepdims=True) + jnp.sum(ew, axis=0, keepdims=True) + jnp.exp(sink - m)
                p_m[j] = em.astype(BF)
                p_w[j] = ew.astype(BF)
                lse_ref[j:j + 1, pl.ds(qb * QB, QB)] = m + jnp.log(den)
                inv.append(1.0 / den)
            for j in range(ATT_Q_HEADS):
                vrows = pl.ds(ATT_HD * (j // ATT_GROUP), ATT_HD)
                ot[pl.ds(ATT_HD * j, ATT_HD), pl.ds(qb * QB, QB)] = (
                    _nn(vt[vrows, pl.ds(0, N_META)], p_m[j])
                    + _nn(vt[vrows, pl.ds(_W0 + qb * QB, QB + HALO)], p_w[j])) * inv[j]
        o = ot[...].T
        o_ref[...] = o
        yc_ref[...] = (o * _silu(g_ref[...].astype(F32))).astype(BF)

    win = pltpu.VMEM((_WROWS, 128), BF)
    in_specs = _swa_in_specs(nt, False)
    out_specs = [pl.BlockSpec((TR, 512), lambda i: (i, 0)), pl.BlockSpec((TR, 512), lambda i: (i, 0)),
                 pl.BlockSpec((ATT_Q_HEADS, TR), lambda i: (0, i))]
    out_shape = [jax.ShapeDtypeStruct((Lp, 512), BF), jax.ShapeDtypeStruct((Lp, 512), F32),
                 jax.ShapeDtypeStruct((ATT_Q_HEADS, Lp), F32)]
    scratch = [win, win, win, pltpu.VMEM((128, _WROWS), BF), pltpu.VMEM((TR, 512), BF),
               pltpu.VMEM((TR, 512), BF), pltpu.VMEM((512, TR), F32)] + _swa_park(F32) + _swa_park(BF)
    return _call_carrying(body, "swa_fwd", nt, in_specs, out_specs, out_shape, scratch,
                          (uc, uc, uc, uc, uc, uc, uc, uc, qg8, kg2, sinks), carry)


def _swa_bwd(uc, qg8, kg2, sinks, o_save, lse, dyc):
    Lp = uc.shape[0]
    nt = Lp // TR
    nqb = TR // QB

    def body(q_ref, g_ref, kc_ref, vc_ref, kh_ref, vh_ref, km_ref, vm_ref, qg_ref, kg_ref, sk_ref,
             o_ref, lse_ref, dy_ref, du_ref, dg_ref, dsk_ref,
             kwin, krwin, vwin, vrwin, kt, krt, qlo, qhi, dolo, dohi, dqt, dk_dir, dk_rol, dv_dir, dv_rol,
             carry_k, carry_v, meta_k, meta_v, s_m, s_w, dp_m, dp_w, p_m, p_w, ds_m, ds_w):
        i = pl.program_id(0)
        t = nt - 1 - i

        @pl.when(i == 0)
        def _():
            carry_k[...] = jnp.zeros_like(carry_k)
            carry_v[...] = jnp.zeros_like(carry_v)
            meta_k[...] = jnp.zeros_like(meta_k)
            meta_v[...] = jnp.zeros_like(meta_v)
            dg_ref[...] = jnp.zeros_like(dg_ref)
            dsk_ref[...] = jnp.zeros_like(dsk_ref)

        ones128 = _group_ones(128)
        ones512 = _group_ones(512)
        _swa_windows(kc_ref, vc_ref, kh_ref, vh_ref, km_ref, vm_ref, kg_ref[...], ones128, kwin, krwin, vwin, vrwin)
        kt[...] = kwin[...].T
        krt[...] = krwin[...].T
        xhat_q, r_q = _head_rms(q_ref[...].astype(F32), ones512)
        lane_hi = (lax.broadcasted_iota(jnp.int32, (1, 512), 1) & ATT_HD) != 0
        lo, hi = _split_heads(xhat_q * qg_ref[...] * _SCALE, lane_hi)
        qlo[...] = lo
        qhi[...] = hi
        gate = g_ref[...].astype(F32)
        dy = dy_ref[...].astype(F32)
        do = dy * _silu(gate)
        o = o_ref[...]
        du_ref[:, 512:1024] = (dy * o * _dsilu(gate)).astype(BF)
        lo, hi = _split_heads(do, lane_hi)
        dolo[...] = lo
        dohi[...] = hi
        hsel = jnp.where(jnp.right_shift(lax.broadcasted_iota(jnp.int32, (ATT_Q_HEADS, 512), 1), 6)
                         == lax.broadcasted_iota(jnp.int32, (ATT_Q_HEADS, 512), 0), 1.0, 0.0).astype(BF)
        prod = do * o
        p_hi = prod.astype(BF)
        d_t = _nt(hsel, p_hi) + _nt(hsel, (prod - p_hi.astype(F32)).astype(BF))
        for acc in (dk_dir, dk_rol, dv_dir, dv_rol):
            acc[...] = jnp.zeros_like(acc)

        for qb in range(nqb):
            rows = pl.ds(qb * QB, QB)
            qcols = pl.ds(qb * QB, QB)
            wrows = pl.ds(_W0 + qb * QB, QB + HALO)
            mrows = pl.ds(0, N_META)
            mask_m, mask_w = _swa_masks_t(t, qb)
            for j in range(ATT_Q_HEADS):
                p, e = j // 2, j % 2
                ks, vs = (kwin, vwin) if e == j // ATT_GROUP else (krwin, vrwin)
                pair = slice(128 * p, 128 * (p + 1))
                qp = (qlo, qhi)[e][rows, pair]
                dop = (dolo, dohi)[e][rows, pair]
                s_m[j] = _nt(ks[mrows, :], qp)
                s_w[j] = _nt(ks[wrows, :], qp)
                dp_m[j] = _nt(vs[mrows, :], dop)
                dp_w[j] = _nt(vs[wrows, :], dop)
            for j in range(ATT_Q_HEADS):
                lse_j = lse_ref[j:j + 1, qcols]
                d_j = d_t[j:j + 1, qb * QB:(qb + 1) * QB]
                em = jnp.exp(jnp.where(mask_m, s_m[j], NEG) - lse_j)
                ew = jnp.exp(jnp.where(mask_w, s_w[j], NEG) - lse_j)
                p_m[j] = em.astype(BF)
                p_w[j] = ew.astype(BF)
                ds_m[j] = (em * (dp_m[j] - d_j)).astype(BF)
                ds_w[j] = (ew * (dp_w[j] - d_j)).astype(BF)
                dsk_ref[j:j + 1, :] -= jnp.exp(sk_ref[:, j:j + 1] - lse_j) * d_j
            for j in range(ATT_Q_HEADS):
                e = j % 2
                ktr = kt if e == j // ATT_GROUP else krt
                hrows = pl.ds(ATT_HD * e, ATT_HD)
                dqt[pl.ds(ATT_HD * j, ATT_HD), qcols] = (_nn(ktr[hrows, pl.ds(0, N_META)], ds_m[j])
                                                         + _nn(ktr[hrows, pl.ds(_W0 + qb * QB, QB + HALO)], ds_w[j]))
            for direct, dk_acc, dv_acc in ((True, dk_dir, dv_dir), (False, dk_rol, dv_rol)):
                heads = [j for j in range(ATT_Q_HEADS) if (j % 2 == j // ATT_GROUP) == direct]
                q_cat = jnp.concatenate([(qlo, qhi)[j % 2][rows, 128 * (j // 2):128 * (j // 2 + 1)] for j in heads], axis=0)
                do_cat = jnp.concatenate([(dolo, dohi)[j % 2][rows, 128 * (j // 2):128 * (j // 2 + 1)] for j in heads], axis=0)
                dk_acc[mrows, :] += _nn(jnp.concatenate([ds_m[j] for j in heads], axis=1), q_cat)
                dk_acc[wrows, :] += _nn(jnp.concatenate([ds_w[j] for j in heads], axis=1), q_cat)
                dv_acc[mrows, :] += _nn(jnp.concatenate([p_m[j] for j in heads], axis=1), do_cat)
                dv_acc[wrows, :] += _nn(jnp.concatenate([p_w[j] for j in heads], axis=1), do_cat)

        dk_dir[...] += pltpu.roll(dk_rol[...], ATT_HD, 1)
        dv_dir[...] += pltpu.roll(dv_rol[...], ATT_HD, 1)
        meta_k[...] += dk_dir[pl.ds(0, N_META), :]
        meta_v[...] += dv_dir[pl.ds(0, N_META), :]
        first = jnp.where(t == 0, 1.0, 0.0)
        dk_dir[pl.ds(_C0 + TR - HALO, HALO), :] += carry_k[...]
        dv_dir[pl.ds(_C0 + TR - HALO, HALO), :] += carry_v[...]
        dk_dir[pl.ds(_C0 + META_PAD, N_META), :] += first * meta_k[...]
        dv_dir[pl.ds(_C0 + META_PAD, N_META), :] += first * meta_v[...]
        carry_k[...] = dk_dir[pl.ds(_W0, HALO), :]
        carry_v[...] = dv_dir[pl.ds(_W0, HALO), :]

        du_ref[:, 1152:1280] = dv_dir[pl.ds(_C0, TR), :].astype(BF)
        xhat_k, r_k = _head_rms(kc_ref[...].astype(F32), ones128)
        dkn = dk_dir[pl.ds(_C0, TR), :]
        dg_ref[1:2, 0:128] += jnp.sum(dkn * xhat_k, axis=0, keepdims=True)
        gd = dkn * kg_ref[...]
        du_ref[:, 1024:1152] = (r_k * (gd - xhat_k * _group_mean(gd * xhat_k, ones128))).astype(BF)
        dqn = dqt[...].T * _SCALE
        dg_ref[0:1, :] += jnp.sum(dqn * xhat_q, axis=0, keepdims=True)
        gd = dqn * qg_ref[...]
        du_ref[:, 0:512] = (r_q * (gd - xhat_q * _group_mean(gd * xhat_q, ones512))).astype(BF)

    rev = lambda i: (nt - 1 - i, 0)
    specs = _swa_in_specs(nt, True)
    win = pltpu.VMEM((_WROWS, 128), BF)
    wint = pltpu.VMEM((128, _WROWS), BF)
    tile_bf = pltpu.VMEM((TR, 512), BF)
    acc = pltpu.VMEM((_WROWS, 128), F32)
    return pl.pallas_call(
        body, name="swa_bwd", grid=(nt,),
        in_specs=specs + [pl.BlockSpec((TR, 512), rev), pl.BlockSpec((ATT_Q_HEADS, TR), lambda i: (0, nt - 1 - i)),
                          pl.BlockSpec((TR, 512), rev)],
        out_specs=[pl.BlockSpec((TR, N_C), rev), pl.BlockSpec((8, 512), lambda i: (0, 0)),
                   pl.BlockSpec((8, 128), lambda i: (0, 0))],
        out_shape=[jax.ShapeDtypeStruct((Lp, N_C), BF), jax.ShapeDtypeStruct((8, 512), F32),
                   jax.ShapeDtypeStruct((8, 128), F32)],
        scratch_shapes=[win, win, win, win, wint, wint, tile_bf, tile_bf, tile_bf, tile_bf,
                        pltpu.VMEM((512, TR), F32), acc, acc, acc, acc,
                        pltpu.VMEM((HALO, 128), F32), pltpu.VMEM((HALO, 128), F32),
                        pltpu.VMEM((N_META, 128), F32), pltpu.VMEM((N_META, 128), F32)]
        + _swa_park(F32) + _swa_park(F32) + _swa_park(BF) + _swa_park(BF),
        compiler_params=_cp(("arbitrary",)),
    )(uc, uc, uc, uc, uc, uc, uc, uc, qg8, kg2, sinks, o_save, lse, dyc)


def _mix_fwd(h, ya, yb, yc, ug, wa, wb, wc, wo):
    Lp = h.shape[0]
    wspec = lambda r: pl.BlockSpec((r, D_MODEL), lambda i: (0, 0))
    yspec = pl.BlockSpec((TRM, 512), lambda i: (i, 0))
    hspec = pl.BlockSpec((TRM, D_MODEL), lambda i: (i, 0))

    def body(h_ref, ya_ref, yb_ref, yc_ref, ug_ref, wa_ref, wb_ref, wc_ref, wo_ref,
             hn_ref, za_ref, zb_ref, zc_ref, mx_ref):
        mixed = jnp.zeros((TRM, D_MODEL), F32)
        for n, (y_ref, w_ref, z_ref) in enumerate(((ya_ref, wa_ref, za_ref), (yb_ref, wb_ref, zb_ref),
                                                   (yc_ref, wc_ref, zc_ref))):
            z = _nn(y_ref[...], w_ref[...])
            z_ref[...] = z.astype(BF)
            mixed = mixed + _sig(ug_ref[:, D_MODEL * n:D_MODEL * (n + 1)].astype(F32)) * z
        mixed = mixed.astype(BF)
        mx_ref[...] = mixed
        hn_ref[...] = h_ref[...] + _nn(mixed, wo_ref[...])

    return pl.pallas_call(
        body, name="mix_fwd", grid=(Lp // TRM,),
        in_specs=[hspec, yspec, yspec, yspec, pl.BlockSpec((TRM, N_G), lambda i: (i, 0)),
                  wspec(512), wspec(512), wspec(512), wspec(D_MODEL)],
        out_specs=[hspec, hspec, hspec, hspec, hspec],
        out_shape=[jax.ShapeDtypeStruct((Lp, D_MODEL), F32)] + [jax.ShapeDtypeStruct((Lp, D_MODEL), BF)] * 4,
        compiler_params=_cp(("parallel",)),
    )(h, ya, yb, yc, ug, wa, wb, wc, wo)


def _mix_bwd(dh, za, zb, zc, ug, wa, wb, wc, wo):
    Lp = dh.shape[0]
    wspec = lambda r: pl.BlockSpec((r, D_MODEL), lambda i: (0, 0))
    yspec = pl.BlockSpec((TRM, 512), lambda i: (i, 0))
    hspec = pl.BlockSpec((TRM, D_MODEL), lambda i: (i, 0))
    gspec = pl.BlockSpec((TRM, N_G), lambda i: (i, 0))

    def body(dh_ref, za_ref, zb_ref, zc_ref, ug_ref, wa_ref, wb_ref, wc_ref, wo_ref,
             dug_ref, dza_ref, dzb_ref, dzc_ref, dya_ref, dyb_ref, dyc_ref):
        dmix = _nt(dh_ref[...].astype(BF), wo_ref[...])
        for n, (z_ref, w_ref, dz_ref, dy_ref) in enumerate(((za_ref, wa_ref, dza_ref, dya_ref),
                                                            (zb_ref, wb_ref, dzb_ref, dyb_ref),
                                                            (zc_ref, wc_ref, dzc_ref, dyc_ref))):
            sl = slice(D_MODEL * n, D_MODEL * (n + 1))
            gt = _sig(ug_ref[:, sl].astype(F32))
            dz = dmix * gt
            dug_ref[:, sl] = (dz * z_ref[...].astype(F32) * (1.0 - gt)).astype(BF)
            dz = dz.astype(BF)
            dz_ref[...] = dz
            dy_ref[...] = _nt(dz, w_ref[...]).astype(BF)

    bf = lambda n: jax.ShapeDtypeStruct((Lp, n), BF)
    return pl.pallas_call(
        body, name="mix_bwd", grid=(Lp // TRM,),
        in_specs=[hspec, hspec, hspec, hspec, gspec, wspec(512), wspec(512), wspec(512), wspec(D_MODEL)],
        out_specs=[gspec, hspec, hspec, hspec, yspec, yspec, yspec],
        out_shape=[bf(N_G), bf(D_MODEL), bf(D_MODEL), bf(D_MODEL), bf(512), bf(512), bf(512)],
        compiler_params=_cp(("parallel",)),
    )(dh, za, zb, zc, ug, wa, wb, wc, wo)


def _inproj_bwd(dus, ws, h, dh, g, carry=None):
    Lp = h.shape[0]
    widths = [w.shape[0] for w in ws]

    def body(dg_ref, da_ref, db_ref, dc_ref, wg_ref, wa_ref, wb_ref, wc_ref, h_ref, dh_ref, g_ref, o_ref, gg_ref):
        @pl.when(pl.program_id(0) == 0)
        def _():
            gg_ref[...] = jnp.zeros_like(gg_ref)

        dhn = (_nn(dg_ref[...], wg_ref[...]) + _nn(da_ref[...], wa_ref[...])
               + _nn(db_ref[...], wb_ref[...]) + _nn(dc_ref[...], wc_ref[...]))
        x = h_ref[...]
        r = lax.rsqrt(jnp.mean(x * x, axis=-1, keepdims=True) + EPS)
        xhat = x * r
        gg_ref[0:1, :] += jnp.sum(dhn * xhat, axis=0, keepdims=True)
        gd = dhn * g_ref[...]
        o_ref[...] = dh_ref[...] + r * (gd - xhat * jnp.mean(gd * xhat, axis=-1, keepdims=True))

    hspec = pl.BlockSpec((TRM, D_MODEL), lambda i: (i, 0))
    in_specs = ([pl.BlockSpec((TRM, n), lambda i: (i, 0)) for n in widths]
                + [pl.BlockSpec((n, D_MODEL), lambda i: (0, 0), pipeline_mode=pl.Buffered(1)) for n in widths]
                + [hspec, hspec, pl.BlockSpec((1, D_MODEL), lambda i: (0, 0))])
    out_specs = [hspec, pl.BlockSpec((8, D_MODEL), lambda i: (0, 0))]
    out_shape = [jax.ShapeDtypeStruct((Lp, D_MODEL), F32), jax.ShapeDtypeStruct((8, D_MODEL), F32)]
    return _call_carrying(body, "inproj_bwd", Lp // TRM, in_specs, out_specs, out_shape, [],
                          (*dus, *ws, h, dh, g), carry)


def _loss_head(h, tgt_pad, seq):
    Lp = h.shape[0]
    nt = Lp // TR

    def body(h_ref, t_ref, dh_ref, l_ref):
        i = pl.program_id(0)

        @pl.when(i == 0)
        def _():
            l_ref[...] = jnp.zeros_like(l_ref)

        row = i * TR + lax.broadcasted_iota(jnp.int32, (TR, 1), 0)
        e = jnp.where((row >= CHUNK) & (row < CHUNK + seq), h_ref[...] - t_ref[...], 0.0)
        dh_ref[...] = e * (1.0 / D_MODEL)
        l_ref[...] += (0.5 / D_MODEL) * jnp.sum(jnp.sum(e * e, axis=0, keepdims=True), axis=1, keepdims=True)

    hspec = pl.BlockSpec((TR, D_MODEL), lambda i: (i, 0))
    return pl.pallas_call(
        body, name="loss_head", grid=(nt,), in_specs=[hspec, hspec],
        out_specs=[hspec, pl.BlockSpec((8, 128), lambda i: (0, 0))],
        out_shape=[jax.ShapeDtypeStruct((Lp, D_MODEL), F32), jax.ShapeDtypeStruct((8, 128), F32)],
        compiler_params=_cp(("arbitrary",)),
    )(h, tgt_pad)


def _lb_softmax(lb_ref):
    x = lb_ref[...]
    e = jnp.exp(x - jnp.max(x, axis=0, keepdims=True))
    return e / jnp.sum(e, axis=0, keepdims=True)


def _lb_fwd(hg_lb):
    def body(lb_ref, o_ref):
        sm = _lb_softmax(lb_ref)
        acc = jnp.zeros((1, 512), F32)
        for l in range(DEPTH):
            if l > 0:
                acc = acc + sm[l:l + 1, :]
            o_ref[l:l + 1, :] = jnp.clip(acc, 0.0, 1.0)

    return pl.pallas_call(body, name="lb_fwd", out_shape=jax.ShapeDtypeStruct((DEPTH, 512), F32))(hg_lb)


def _lb_bwd(hg_lb, dlb_all):
    def body(lb_ref, d_ref, o_ref):
        sm = _lb_softmax(lb_ref)
        acc = jnp.zeros((1, 512), F32)
        gm = []
        for l in range(DEPTH):
            if l > 0:
                acc = acc + sm[l:l + 1, :]
            gm.append(jnp.where((acc >= 0.0) & (acc <= 1.0), d_ref[l:l + 1, :], 0.0))
        dsm = [jnp.zeros((1, 512), F32)]
        for j in range(1, DEPTH):
            s = gm[j]
            for l in range(j + 1, DEPTH):
                s = s + gm[l]
            dsm.append(s)
        dot = dsm[0] * sm[0:1, :]
        for j in range(1, DEPTH):
            dot = dot + dsm[j] * sm[j:j + 1, :]
        for j in range(DEPTH):
            o_ref[j:j + 1, :] = sm[j:j + 1, :] * (dsm[j] - dot)

    return pl.pallas_call(body, name="lb_bwd", out_shape=jax.ShapeDtypeStruct((DEPTH, 512), F32))(hg_lb, dlb_all)


_ANY = pl.BlockSpec(memory_space=pl.ANY)


def _chip_peers():
    x, y, c = lax.axis_index("x"), lax.axis_index("y"), lax.axis_index("c")
    return (x, y, c), [(1 - x, y, c), (x, 1 - y, c), (1 - x, 1 - y, c)]


def _exchange(kind, ins, outs, send, recv, loc):
    (x, y, c), peers = _chip_peers()
    me = 2 * x + y
    ds = []
    for a in range(len(ins)):
        if kind == "gather":
            ds.append(pltpu.make_async_copy(ins[a], outs[a].at[me], loc.at[a]))
        else:
            ds.append(pltpu.make_async_copy(ins[a].at[me], outs[a].at[0], loc.at[a]))
        for p, (px, py, pc) in enumerate(peers):
            src, dst = (ins[a], outs[a].at[me]) if kind == "gather" else (ins[a].at[2 * px + py], outs[a].at[1 + p])
            ds.append(pltpu.make_async_remote_copy(src_ref=src, dst_ref=dst, send_sem=send.at[a, p],
                                                   recv_sem=recv.at[a, p], device_id=(px, py, pc), device_id_type=MESH))
    return ds


def _exchange_out_shapes(kind, arrs):
    if kind == "gather":
        return [jax.ShapeDtypeStruct((4,) + a.shape, a.dtype) for a in arrs]
    return [jax.ShapeDtypeStruct(a.shape, a.dtype) for a in arrs]


def _exchange_sems(n):
    return [pltpu.SemaphoreType.DMA((n, 3)), pltpu.SemaphoreType.DMA((n, 3)), pltpu.SemaphoreType.DMA((n,))]


def _gather_first(arrs, split):
    n = len(arrs)

    def body(*refs):
        ins, outs = refs[:n], refs[n:2 * n]
        send, recv, loc, fsend, frecv = refs[2 * n:]
        (x, y, c), peers = _chip_peers()
        me = 2 * x + y

        def half(a):
            hr = arrs[a].shape[0] // 2
            return pl.ds(pl.multiple_of(c * hr, 16), hr)

        local = [pltpu.make_async_copy(ins[a], outs[a].at[me], loc.at[a]) for a in range(n)]
        far, fwd = {}, {}
        for a in range(n):
            for p, (px, py, pc) in enumerate(peers):
                src, dst = (ins[a].at[half(a)], outs[a].at[me, half(a)]) if split[a] else (ins[a], outs[a].at[me])
                far[a, p] = pltpu.make_async_remote_copy(src_ref=src, dst_ref=dst, send_sem=send.at[a, p],
                                                         recv_sem=recv.at[a, p], device_id=(px, py, pc), device_id_type=MESH)
                if split[a]:
                    landed = outs[a].at[2 * px + py, half(a)]
                    fwd[a, p] = pltpu.make_async_remote_copy(src_ref=landed, dst_ref=landed, send_sem=fsend.at[a, p],
                                                             recv_sem=frecv.at[a, p], device_id=(x, y, 1 - c),
                                                             device_id_type=MESH)
        for d in local + list(far.values()):
            d.start()
        for key, d in far.items():
            d.wait_recv()
            if key in fwd:
                fwd[key].start()
        for d in fwd.values():
            d.wait()
        for d in far.values():
            d.wait_send()
        for d in local:
            d.wait()

    sems = pltpu.SemaphoreType.DMA((n, 3))
    return pl.pallas_call(
        body, name="gather_first", in_specs=[_ANY] * n, out_specs=[_ANY] * n,
        out_shape=_exchange_out_shapes("gather", arrs),
        scratch_shapes=[sems, sems, pltpu.SemaphoreType.DMA((n,)), sems, sems],
        compiler_params=pltpu.CompilerParams(has_side_effects=True),
    )(*arrs)


def _carry_exchange(body, n_in, n_out, n_steps, kind, n):
    def wrapped(*refs):
        ins, cin = refs[:n_in], refs[n_in:n_in + n]
        outs, cout = refs[n_in + n:n_in + n + n_out], refs[n_in + n + n_out:n_in + 2 * n + n_out]
        scr, sems = refs[n_in + 2 * n + n_out:-3], refs[-3:]
        i = pl.program_id(0)

        @pl.when(i == 0)
        def _():
            for d in _exchange(kind, cin, cout, *sems):
                d.start()

        body(*ins, *outs, *scr)

        @pl.when(i == n_steps - 1)
        def _():
            for d in _exchange(kind, cin, cout, *sems):
                d.wait()

    return wrapped


def _swap_cores(arrs):
    n = len(arrs)

    def body(*refs):
        ins, outs = refs[:n], refs[n:2 * n]
        send, recv = refs[2 * n:]
        x, y, c = lax.axis_index("x"), lax.axis_index("y"), lax.axis_index("c")
        rdmas = []
        for a in range(n):
            r = pltpu.make_async_remote_copy(src_ref=ins[a], dst_ref=outs[a], send_sem=send.at[a], recv_sem=recv.at[a],
                                             device_id=(x, y, 1 - c), device_id_type=MESH)
            r.start()
            rdmas.append(r)
        for r in rdmas:
            r.wait()

    return pl.pallas_call(
        body, name="swap_cores", in_specs=[_ANY] * n, out_specs=[_ANY] * n,
        out_shape=[jax.ShapeDtypeStruct(a.shape, a.dtype) for a in arrs],
        scratch_shapes=[pltpu.SemaphoreType.DMA((n,)), pltpu.SemaphoreType.DMA((n,))],
        compiler_params=pltpu.CompilerParams(has_side_effects=True),
    )(*arrs)


def _allsum_small(p):
    R = p.shape[0]

    def body(p_ref, o_ref, buf, send, recv):
        x, y, c = lax.axis_index("x"), lax.axis_index("y"), lax.axis_index("c")
        me = 4 * x + 2 * y + c
        buf[me] = p_ref[...]
        rdmas = []
        for k in range(1, 8):
            peer = (x ^ (k >> 2), y ^ ((k >> 1) & 1), c ^ (k & 1))
            r = pltpu.make_async_remote_copy(src_ref=p_ref, dst_ref=buf.at[me], send_sem=send.at[k - 1],
                                             recv_sem=recv.at[k - 1], device_id=peer, device_id_type=MESH)
            r.start()
            rdmas.append(r)
        for r in rdmas:
            r.wait()
        acc = buf[0]
        for d in range(1, 8):
            acc = acc + buf[d]
        o_ref[...] = acc

    return pl.pallas_call(
        body, name="allsum_small", out_shape=jax.ShapeDtypeStruct((R, 512), F32),
        in_specs=[pl.BlockSpec(memory_space=pltpu.VMEM)], out_specs=pl.BlockSpec(memory_space=pltpu.VMEM),
        scratch_shapes=[pltpu.VMEM((8, R, 512), F32), pltpu.SemaphoreType.DMA((7,)), pltpu.SemaphoreType.DMA((7,))],
        compiler_params=_cp(has_side_effects=True),
    )(p)


def _row_block(rows):
    return max((d for d in range(16, 513, 16) if rows % d == 0), default=rows)


def _sum4(parts, name):
    _, R, C = parts.shape
    tr = _row_block(R)

    def body(p_ref, o_ref):
        p = [p_ref[k].astype(F32) for k in range(4)]
        o_ref[...] = ((p[0] + p[1]) + p[2]) + p[3]

    return pl.pallas_call(
        body, name=name, grid=(R // tr,), in_specs=[pl.BlockSpec((4, tr, C), lambda i: (0, i, 0))],
        out_specs=pl.BlockSpec((tr, C), lambda i: (i, 0)), out_shape=jax.ShapeDtypeStruct((R, C), F32),
        compiler_params=_cp(("parallel",)),
    )(parts)


def _adamw(w, m, v, g0, g1, name):
    L, R, C = w.shape
    tr = _row_block(R)
    two = g1 is not None
    c1 = 1.0 / (1.0 - ADAM_B1 ** ADAM_STEP)
    c2 = 1.0 / (1.0 - ADAM_B2 ** ADAM_STEP)

    def body(*refs):
        if two:
            w_ref, m_ref, v_ref, a_ref, b_ref, g_ref, d_ref, nm_ref, nv_ref = refs
            g = a_ref[...] + b_ref[...]
        else:
            w_ref, m_ref, v_ref, a_ref, g_ref, d_ref, nm_ref, nv_ref = refs
            g = a_ref[...]
        g_ref[...] = g
        m = ADAM_B1 * m_ref[...] + (1.0 - ADAM_B1) * g
        v = ADAM_B2 * v_ref[...] + (1.0 - ADAM_B2) * (g * g)
        nm_ref[...] = m
        nv_ref[...] = v
        d_ref[...] = -ADAM_LR * ((m * c1) / (jnp.sqrt(v * c2) + ADAM_EPS) + ADAM_WD * w_ref[...])

    spec = pl.BlockSpec((1, tr, C), lambda l, i: (l, i, 0))
    n_in = 5 if two else 4
    ins = (w, m, v, g0, g1) if two else (w, m, v, g0)
    return pl.pallas_call(
        body, name=name, grid=(L, R // tr), in_specs=[spec] * n_in, out_specs=[spec] * 4,
        out_shape=[jax.ShapeDtypeStruct((L, R, C), F32)] * 4, compiler_params=_cp(("parallel", "parallel")),
    )(*ins)


def _pad8(a):
    r = (-a.shape[0]) % 8
    return a if r == 0 else jnp.pad(a, ((0, r), (0, 0)))


def _local_step(x, tgt, meta, P, shards=None, prep=None, pack=None):
    seq = x.shape[0]
    Lp = -(-(seq + CHUNK) // TR) * TR
    tail = Lp - seq - CHUNK
    h = jnp.concatenate([jnp.zeros((META_PAD, D_MODEL), F32), meta, x, jnp.zeros((tail, D_MODEL), F32)], axis=0)
    tgt_pad = jnp.pad(tgt, ((CHUNK, tail), (0, 0)))

    P = list(P)
    saved = []
    for l in range(DEPTH):
        p = P[l]
        hn = _rms_fwd(h, p["norm_g"])
        mm = functools.partial(_matmul, tb=True, out_dtype=BF, tm=TR, tk=D_MODEL, col_major_grid=True)
        ug = mm(hn, p["w_g"], tn=N_G // 2, name="inproj_g")
        ua = mm(hn, p["w_a"], tn=N_A, name="inproj_a")
        ub = mm(hn, p["w_b"], tn=N_B, name="inproj_b")
        uc = mm(hn, p["w_c"], tn=N_C, name="inproj_c")
        nxt = shards[l + 1] if shards is not None and l + 1 < DEPTH else None
        carry = (lambda part: ("gather", part)) if nxt is not None else (lambda part: None)
        res_a = _conv_fwd(ua, p["conv_w"], p["conv_vec"], carry(nxt and nxt[1:2]))
        res_b = _hg_fwd(ub, p["lb"], p["gn4"], carry(nxt and nxt[0:1]))
        res_c = _swa_fwd(uc, p["qg"], p["kg"], p["sinks"], carry(nxt and nxt[2:]))
        (ya, yconv), (yb, o_hg, s_hg), (yc, o_at, lse) = res_a[:2], res_b[:3], res_c[:3]
        if nxt is not None:
            P.append(prep(l + 1, [*res_b[3:], *res_a[2:], *res_c[3:]]))
        h_new, za, zb, zc, mixed = _mix_fwd(h, ya, yb, yc, ug, p["w_ao"], p["w_bo"], p["w_co"], p["w_out"])
        saved.append(dict(h=h, hn=hn, ug=ug, ua=ua, ub=ub, uc=uc, ya=ya, yconv=yconv, yb=yb, o_hg=o_hg, s_hg=s_hg,
                          yc=yc, o_at=o_at, lse=lse, za=za, zb=zb, zc=zc, mixed=mixed))
        h = h_new

    dh, loss8 = _loss_head(h, tgt_pad, seq)

    grads = [None] * DEPTH
    parts = [[None, None] for _ in range(DEPTH)]
    pending = None
    tk_dw = 2 * TR if Lp % (2 * TR) == 0 else TR
    for l in reversed(range(DEPTH)):
        p, s = P[l], saved[l]
        dug, dza, dzb, dzc, dya, dyb, dyc = _mix_bwd(dh, s["za"], s["zb"], s["zc"], s["ug"],
                                                      p["w_ao"], p["w_bo"], p["w_co"], p["w_out"])
        tnmm = functools.partial(_matmul, ta=True, out_dtype=F32, tk=tk_dw)
        g = {}
        g["w_out"] = tnmm(s["mixed"], dh, tm=D_MODEL, tn=D_MODEL, name="dw_out")
        g["w_ao"] = tnmm(s["ya"], dza, tm=512, tn=D_MODEL, name="dw_ao")
        g["w_bo"] = tnmm(s["yb"], dzb, tm=512, tn=D_MODEL, name="dw_bo")
        g["w_co"] = tnmm(s["yc"], dzc, tm=512, tn=D_MODEL, name="dw_co")
        dua, g["conv_w"], g["conv_vec"] = _conv_bwd(s["ua"], s["yconv"], dya, p["conv_w"], p["conv_vec"])
        carry = ("scatter", pending[1]) if pending is not None else None
        res = _hg_bwd(s["ub"], p["lb"], p["gn4"], s["o_hg"], s["s_hg"], dyb, carry)
        dub, g["hg_small"] = res[:2]
        if carry is not None:
            parts[pending[0]][1] = res[2:]
        duc, g["at_gain"], g["at_sink"] = _swa_bwd(s["uc"], p["qg"], p["kg"], p["sinks"], s["o_at"], s["lse"], dyc)
        g["w_g"] = tnmm(dug, s["hn"], tm=N_G // 2, tn=D_MODEL, name="dw_in_g")
        g["w_a"] = tnmm(dua, s["hn"], tm=N_A, tn=D_MODEL, name="dw_in_a")
        g["w_b"] = tnmm(dub, s["hn"], tm=N_B, tn=D_MODEL, name="dw_in_b")
        g["w_c"] = tnmm(duc, s["hn"], tm=N_C, tn=D_MODEL, name="dw_in_c")
        first, second = pack(g) if pack is not None else (None, None)
        if first is not None and l == 0:
            first, second = first + second, []
        res = _inproj_bwd([dug, dua, dub, duc], [p["w_g"], p["w_a"], p["w_b"], p["w_c"]], s["h"], dh, p["norm_g"],
                          ("scatter", first) if first is not None else None)
        dh, g["norm_g"] = res[:2]
        grads[l] = g
        if pack is not None:
            parts[l] = [res[2:3], res[3:]] if l == 0 else [res[2:], None]
            pending = (l, second) if l > 0 else None
    return loss8, dh, grads, parts


def _split_w_in(wt):
    return dict(w_a=wt[0:1536], w_b=wt[1536:3584],
                w_c=jnp.concatenate([wt[3584:4096], wt[4352:4864], wt[4096:4352]], axis=0), w_g=wt[4864:7936])


def _join_w_in(g):
    c = g["w_c"]
    return jnp.concatenate([g["w_a"], g["w_b"], c[0:512], c[1024:1280], c[512:1024], g["w_g"]], axis=0)


def _attn_small(g):
    return (g["at_gain"][0].reshape(ATT_Q_HEADS, ATT_HD).sum(0),
            g["at_gain"][1, 0:128].reshape(ATT_KV_HEADS, ATT_HD).sum(0), g["at_sink"].sum(1))


_SMALL = (("norm_g", 8), ("meta", 32), ("conv_w", 32 * DEPTH), ("conv_b", 8), ("conv_ln_g", 8), ("conv_ln_b", 8),
          ("lb", 8), ("hg_norm_g", 8), ("q_norm_g", 8), ("k_norm_g", 8), ("sinks", 8))


def _small_offsets():
    off, o = {}, 0
    for name, rows in _SMALL:
        off[name] = (o, rows)
        o += rows
    return off, o


def _pack_small(d):
    parts = []
    for name, rows in _SMALL:
        a = d[name]
        parts.append(jnp.pad(a, ((0, rows - a.shape[0]), (0, 512 - a.shape[1]))))
    return jnp.concatenate(parts, axis=0)


def kernel(x, meta_tokens, norm_g, w_in, conv_w, conv_b, conv_ln_g, conv_ln_b, w_conv_out, hg_lower_bounds, hg_norm_g, w_hg_out, q_norm_g, k_norm_g, attn_sinks, w_att_out, w_out, loss_target, m_meta_tokens, m_norm_g, m_w_in, m_conv_w, m_conv_b, m_conv_ln_g, m_conv_ln_b, m_w_conv_out, m_hg_lower_bounds, m_hg_norm_g, m_w_hg_out, m_q_norm_g, m_k_norm_g, m_attn_sinks, m_w_att_out, m_w_out, v_meta_tokens, v_norm_g, v_w_in, v_conv_w, v_conv_b, v_conv_ln_g, v_conv_ln_b, v_w_conv_out, v_hg_lower_bounds, v_hg_norm_g, v_w_hg_out, v_q_norm_g, v_k_norm_g, v_attn_sinks, v_w_att_out, v_w_out):
    xi, yi = lax.axis_index("x"), lax.axis_index("y")
    chip = 2 * xi + yi
    NS = w_in.shape[2]
    CS = conv_w.shape[2]
    MS = meta_tokens.shape[1]

    half = NS // 2
    w_in_t, m_w_in_t, v_w_in_t = (jnp.swapaxes(t, 1, 2) for t in (w_in, m_w_in, v_w_in))
    shards = [[w_in_t[l, :half].astype(BF), w_in_t[l, half:].astype(BF), w_conv_out[l].astype(BF),
               w_hg_out[l].astype(BF), w_att_out[l].astype(BF), w_out[l].astype(BF)] for l in range(DEPTH)]
    *first, g_meta, g_convw = _gather_first(shards[0] + [meta_tokens, conv_w.reshape(DEPTH * CONV_WIDTH, CS)],
                                            [True] * len(shards[0]) + [False, False])
    cols = lambda g: g.transpose(1, 0, 2).reshape(g.shape[1], -1)
    meta_f = cols(g_meta)
    convw_f = cols(g_convw).reshape(DEPTH, CONV_WIDTH, D_CONV)
    lb_all = _lb_fwd(hg_lower_bounds)

    def prep(l, gathered):
        g_win_top, g_win_bot, g_wao, g_wbo, g_wco, g_wout = gathered
        p = _split_w_in(jnp.concatenate([g_win_top, g_win_bot], axis=1).reshape(4 * NS, D_MODEL))
        p.update(w_ao=cols(g_wao), w_bo=cols(g_wbo), w_co=cols(g_wco), w_out=g_wout.reshape(D_MODEL, D_MODEL),
                 norm_g=norm_g[l:l + 1], conv_w=convw_f[l],
                 conv_vec=_pad8(jnp.stack([conv_b[l], conv_ln_g[l], conv_ln_b[l]])),
                 lb=lb_all[l:l + 1], gn4=jnp.tile(hg_norm_g[l:l + 1], (1, HG_HEADS)),
                 qg=jnp.tile(q_norm_g[l:l + 1], (1, ATT_Q_HEADS)), kg=jnp.tile(k_norm_g[l:l + 1], (1, ATT_KV_HEADS)),
                 sinks=attn_sinks[l:l + 1])
        return p

    shard_cols = lambda a: a.reshape(a.shape[0], 4, -1).transpose(1, 0, 2)
    def pack(g):
        win = _join_w_in(g).reshape(4, NS, D_MODEL).astype(BF)
        return [win[:, :half]], [win[:, half:], shard_cols(g["w_ao"]).astype(BF), shard_cols(g["w_bo"]).astype(BF),
                                 shard_cols(g["w_co"]).astype(BF), g["w_out"].reshape(4, MS, D_MODEL).astype(BF)]

    loss8, dh0, grads, parts = _local_step(x[0], loss_target[0], meta_f, [prep(0, first)], shards, prep, pack)
    seq = x.shape[1]
    grad_x = dh0[CHUNK:CHUNK + seq][None]
    loss = lax.psum(loss8[0, 0], ("x", "y", "c"))

    sum4 = functools.partial(_sum4, name="sum_chips")
    mine = [jnp.concatenate([t for l in range(DEPTH) for t in (sum4(parts[l][0][0]), sum4(parts[l][1][0]))], axis=0)]
    mine += [jnp.concatenate([sum4(parts[l][1][a]) for l in range(DEPTH)], axis=0) for a in range(1, 5)]
    theirs = _swap_cores(mine)

    dlb_all = jnp.concatenate([grads[l]["hg_small"][0:1] for l in range(DEPTH)], axis=0)
    small = dict(
        norm_g=jnp.concatenate([grads[l]["norm_g"][0:1] for l in range(DEPTH)], axis=0).reshape(8, 512),
        meta=dh0[META_PAD:CHUNK].reshape(32, 512),
        conv_w=jnp.concatenate([grads[l]["conv_w"] for l in range(DEPTH)], axis=0),
        conv_b=jnp.concatenate([grads[l]["conv_vec"][0:1] for l in range(DEPTH)], axis=0),
        conv_ln_g=jnp.concatenate([grads[l]["conv_vec"][1:2] for l in range(DEPTH)], axis=0),
        conv_ln_b=jnp.concatenate([grads[l]["conv_vec"][2:3] for l in range(DEPTH)], axis=0),
        lb=_lb_bwd(hg_lower_bounds, dlb_all),
        hg_norm_g=jnp.concatenate([grads[l]["hg_small"][1:2].reshape(HG_HEADS, HG_D).sum(0, keepdims=True)
                                   for l in range(DEPTH)], axis=0),
        q_norm_g=jnp.stack([_attn_small(grads[l])[0] for l in range(DEPTH)]),
        k_norm_g=jnp.stack([_attn_small(grads[l])[1] for l in range(DEPTH)]),
        sinks=jnp.stack([_attn_small(grads[l])[2] for l in range(DEPTH)]),
    )
    gsum = _allsum_small(_pack_small(small))
    off, _ = _small_offsets()

    def take(name, rows, cols):
        o, _ = off[name]
        return gsum[o:o + rows, 0:cols]

    g_meta_full = take("meta", 32, 512).reshape(N_META, D_MODEL)
    g_convw_full = take("conv_w", 32 * DEPTH, 512).reshape(DEPTH, 32, 512)[:, :CONV_WIDTH]
    small_grads = dict(
        norm_g=take("norm_g", 8, 512),
        meta=lax.dynamic_slice_in_dim(g_meta_full, chip * MS, MS, axis=1),
        conv_w=lax.dynamic_slice_in_dim(g_convw_full, chip * CS, CS, axis=2).reshape(DEPTH * CONV_WIDTH, CS),
        conv_b=take("conv_b", DEPTH, 512), conv_ln_g=take("conv_ln_g", DEPTH, 512), conv_ln_b=take("conv_ln_b", DEPTH, 512),
        lb=take("lb", DEPTH, 512), hg_norm_g=take("hg_norm_g", DEPTH, HG_D), q_norm_g=take("q_norm_g", DEPTH, ATT_HD),
        k_norm_g=take("k_norm_g", DEPTH, ATT_HD), sinks=take("sinks", DEPTH, ATT_Q_HEADS))

    def big_update(w, m, v, a, b, name):
        return _adamw(w, m, v, a.reshape(w.shape), b.reshape(w.shape), name)

    res = {}
    res["w_in"] = [jnp.swapaxes(t, 1, 2) for t in big_update(w_in_t, m_w_in_t, v_w_in_t, mine[0], theirs[0], "adamw_w_in")]
    res["w_conv_out"] = big_update(w_conv_out, m_w_conv_out, v_w_conv_out, mine[1], theirs[1], "adamw_w_ao")
    res["w_hg_out"] = big_update(w_hg_out, m_w_hg_out, v_w_hg_out, mine[2], theirs[2], "adamw_w_bo")
    res["w_att_out"] = big_update(w_att_out, m_w_att_out, v_w_att_out, mine[3], theirs[3], "adamw_w_co")
    res["w_out"] = big_update(w_out, m_w_out, v_w_out, mine[4], theirs[4], "adamw_w_out")

    small_w = dict(meta=(meta_tokens, m_meta_tokens, v_meta_tokens), norm_g=(norm_g, m_norm_g, v_norm_g),
                   conv_w=(conv_w, m_conv_w, v_conv_w), conv_b=(conv_b, m_conv_b, v_conv_b),
                   conv_ln_g=(conv_ln_g, m_conv_ln_g, v_conv_ln_g), conv_ln_b=(conv_ln_b, m_conv_ln_b, v_conv_ln_b),
                   lb=(hg_lower_bounds, m_hg_lower_bounds, v_hg_lower_bounds),
                   hg_norm_g=(hg_norm_g, m_hg_norm_g, v_hg_norm_g), q_norm_g=(q_norm_g, m_q_norm_g, v_q_norm_g),
                   k_norm_g=(k_norm_g, m_k_norm_g, v_k_norm_g), sinks=(attn_sinks, m_attn_sinks, v_attn_sinks))
    view = lambda n, t: t.reshape(-1, 512) if n == "norm_g" else t.reshape(-1, t.shape[-1])
    pw, pm, pv = (_pack_rows([view(n, small_w[n][k]) for n in small_w]) for k in range(3))
    pg = _pack_rows([small_grads[n] for n in small_w])
    packed = [t[0] for t in _adamw(pw[None], pm[None], pv[None], pg[None], None, "adamw_small")]
    o = 0
    for n in small_w:
        r, cdim = view(n, small_w[n][0]).shape
        res[n] = [t[o:o + r, 0:cdim].reshape(small_w[n][0].shape) for t in packed]
        o += -(-r // 8) * 8

    order = [("meta", None), ("norm_g", None), ("w_in", None), ("conv_w", None), ("conv_b", None), ("conv_ln_g", None),
             ("conv_ln_b", None), ("w_conv_out", None), ("lb", None), ("hg_norm_g", None), ("w_hg_out", None),
             ("q_norm_g", None), ("k_norm_g", None), ("sinks", None), ("w_att_out", None), ("w_out", None)]
    outs = [loss, grad_x]
    for k in range(4):
        outs += [res[n][k] for n, _ in order]
    return tuple(outs)


def _pack_rows(arrs):
    parts = []
    for a in arrs:
        r = (-a.shape[0]) % 8
        parts.append(jnp.pad(a, ((0, r), (0, 512 - a.shape[1]))))
    return jnp.concatenate(parts, axis=0)
```

```python
import functools

import jax
import jax.numpy as jnp
from jax import lax
from jax.experimental import pallas as pl
from jax.experimental.pallas import tpu as pltpu

F32 = jnp.float32
BF = jnp.bfloat16

D_MODEL = 1024
DEPTH = 4
CHUNK = 64
N_META = 16
META_PAD = CHUNK - N_META
D_CONV = 512
CONV_WIDTH = 31
HG_HEADS = 4
HG_D = 128
ATT_Q_HEADS = 8
ATT_KV_HEADS = 2
ATT_HD = 64
ATT_GROUP = ATT_Q_HEADS // ATT_KV_HEADS
EPS = 1e-6
F_FLOOR = 1e-30
NEG = -1e30

ADAM_LR = 0.001
ADAM_B1 = 0.9
ADAM_B2 = 0.999
ADAM_EPS = 1e-08
ADAM_WD = 0.01
ADAM_STEP = 10

TR = 640
TRM = TR // 2
CONV_RB = 32
QB = 128
HALO = 128
VMEM_LIMIT = 56 * 1024 * 1024

N_G, N_A, N_B, N_C = 3 * D_MODEL, 3 * D_CONV, 4 * 512, 2 * 512 + 2 * 128

MESH = pl.DeviceIdType.MESH


def _cp(sem=None, vmem=VMEM_LIMIT, **kw):
    if sem is None:
        return pltpu.CompilerParams(vmem_limit_bytes=vmem, **kw)
    return pltpu.CompilerParams(dimension_semantics=sem, vmem_limit_bytes=vmem, **kw)


def _nn(a, b):
    return lax.dot_general(a, b, (((1,), (0,)), ((), ())), preferred_element_type=F32)


def _nt(a, b):
    return lax.dot_general(a, b, (((1,), (1,)), ((), ())), preferred_element_type=F32)


def _tn(a, b):
    return lax.dot_general(a, b, (((0,), (0,)), ((), ())), preferred_element_type=F32)


def _sig(x):
    return jax.nn.sigmoid(x)


def _silu(x):
    return x * _sig(x)


def _silu_pair(x):
    s = _sig(x)
    return x * s, s * (1.0 + x * (1.0 - s))


def _mm_split(t, x):
    hi = x.astype(BF)
    lo = (x - hi.astype(F32)).astype(BF)
    return _nn(t, hi) + _nn(t, lo)


def _chunk_tri(n, upper):
    r = lax.broadcasted_iota(jnp.int32, (n, n), 0)
    c = lax.broadcasted_iota(jnp.int32, (n, n), 1)
    same = jnp.right_shift(r, 6) == jnp.right_shift(c, 6)
    tri = (c >= r) if upper else (c <= r)
    return jnp.where(same & tri, 1.0, 0.0).astype(BF)


def _matmul(a, b, *, ta=False, tb=False, out_dtype, tm, tn, tk, name, col_major_grid=False):
    if ta:
        K, M = a.shape
    else:
        M, K = a.shape
    N = b.shape[0] if tb else b.shape[1]
    assert M % tm == 0 and N % tn == 0 and K % tk == 0, (name, M, N, K, tm, tn, tk)
    nk = K // tk
    if col_major_grid:
        grid = (N // tn, M // tm, nk)
        ij = lambda g0, g1: (g1, g0)
    else:
        grid = (M // tm, N // tn, nk)
        ij = lambda g0, g1: (g0, g1)
    if ta:
        a_spec = pl.BlockSpec((tk, tm), lambda g0, g1, k: (k, ij(g0, g1)[0]))
    else:
        a_spec = pl.BlockSpec((tm, tk), lambda g0, g1, k: (ij(g0, g1)[0], k))
    if tb:
        b_spec = pl.BlockSpec((tn, tk), lambda g0, g1, k: (ij(g0, g1)[1], k))
    else:
        b_spec = pl.BlockSpec((tk, tn), lambda g0, g1, k: (k, ij(g0, g1)[1]))
    o_spec = pl.BlockSpec((tm, tn), lambda g0, g1, k: ij(g0, g1))
    dims = (((0 if ta else 1,), (1 if tb else 0,)), ((), ()))
    use_acc = nk > 1 and out_dtype != F32

    def body(a_ref, b_ref, o_ref, *scr):
        k = pl.program_id(2)
        p = lax.dot_general(a_ref[...].astype(BF), b_ref[...].astype(BF), dims, preferred_element_type=F32)
        if nk == 1:
            o_ref[...] = p.astype(out_dtype)
        else:
            acc = scr[0] if use_acc else o_ref

            @pl.when(k == 0)
            def _():
                acc[...] = p

            @pl.when(k > 0)
            def _():
                acc[...] += p

            if use_acc:
                @pl.when(k == nk - 1)
                def _():
                    o_ref[...] = acc[...].astype(out_dtype)

    return pl.pallas_call(
        body, name=name, grid=grid, in_specs=[a_spec, b_spec], out_specs=o_spec,
        out_shape=jax.ShapeDtypeStruct((M, N), out_dtype),
        scratch_shapes=[pltpu.VMEM((tm, tn), F32)] if use_acc else [],
        compiler_params=_cp(("parallel", "parallel", "arbitrary")),
    )(a, b)


def _rms_fwd(h, g):
    Lp = h.shape[0]

    def body(h_ref, g_ref, o_ref):
        x = h_ref[...]
        r = lax.rsqrt(jnp.mean(x * x, axis=-1, keepdims=True) + EPS)
        o_ref[...] = (x * r * g_ref[...]).astype(BF)

    return pl.pallas_call(
        body, name="rms_fwd", grid=(Lp // TR,),
        in_specs=[pl.BlockSpec((TR, D_MODEL), lambda i: (i, 0)), pl.BlockSpec((1, D_MODEL), lambda i: (0, 0))],
        out_specs=pl.BlockSpec((TR, D_MODEL), lambda i: (i, 0)),
        out_shape=jax.ShapeDtypeStruct((Lp, D_MODEL), BF),
        compiler_params=_cp(("parallel",)),
    )(h, g)


def _glu(ua, row):
    a = ua[:, 0:D_CONV].astype(F32)
    gl = ua[:, D_CONV:2 * D_CONV].astype(F32)
    return jnp.where(row >= META_PAD, a * _sig(gl), 0.0)


_SH_ROWS = TR + CHUNK - 8


def _fill_shifts(src, sh):
    for b in range(1, 8):
        sh[b - 1] = src[pl.ds(b, _SH_ROWS), :]


def _shifted(src, sh, start, n):
    b = start % 8
    if b == 0:
        return src[pl.ds(start, n), :]
    return sh[b - 1, pl.ds(start - b, n), :]


def _conv_fwd(ua, cw, cvec, carry=None):
    Lp = ua.shape[0]
    nt = Lp // TR
    hb = TR // CHUNK

    def body(cur_ref, halo_ref, w_ref, v_ref, ya_ref, yc_ref, ext, sh):
        i = pl.program_id(0)
        row = i * TR + lax.broadcasted_iota(jnp.int32, (TR, 1), 0)
        hrow = i * TR - CHUNK + lax.broadcasted_iota(jnp.int32, (CHUNK, 1), 0)
        ext[pl.ds(0, CHUNK), :] = jnp.where(i > 0, _glu(halo_ref[...], hrow), 0.0)
        ext[pl.ds(CHUNK, TR), :] = _glu(cur_ref[...], row)
        _fill_shifts(ext, sh)
        for rb in range(TR // CONV_RB):
            r0 = rb * CONV_RB
            rows = pl.ds(r0, CONV_RB)
            acc = jnp.zeros((CONV_RB, D_CONV), F32)
            for j in range(CONV_WIDTH):
                acc = acc + _shifted(ext, sh, r0 + CHUNK - (CONV_WIDTH - 1) + j, CONV_RB) * w_ref[j:j + 1, :]
            y = acc + v_ref[0:1, :]
            yc_ref[rows, :] = y
            mu = jnp.mean(y, axis=-1, keepdims=True)
            d = y - mu
            var = jnp.mean(d * d, axis=-1, keepdims=True)
            yn = d * lax.rsqrt(var + EPS) * v_ref[1:2, :] + v_ref[2:3, :]
            ya_ref[rows, :] = (_silu(yn) * _silu(cur_ref[rows, 2 * D_CONV:3 * D_CONV].astype(F32))).astype(BF)

    in_specs = [pl.BlockSpec((TR, N_A), lambda i: (i, 0)),
                pl.BlockSpec((CHUNK, N_A), lambda i: (jnp.maximum(i * hb - 1, 0), 0)),
                pl.BlockSpec((CONV_WIDTH, D_CONV), lambda i: (0, 0)),
                pl.BlockSpec((8, D_CONV), lambda i: (0, 0))]
    out_specs = [pl.BlockSpec((TR, D_CONV), lambda i: (i, 0)), pl.BlockSpec((TR, D_CONV), lambda i: (i, 0))]
    out_shape = [jax.ShapeDtypeStruct((Lp, D_CONV), BF), jax.ShapeDtypeStruct((Lp, D_CONV), F32)]
    scratch = [pltpu.VMEM((TR + CHUNK, D_CONV), F32), pltpu.VMEM((7, _SH_ROWS, D_CONV), F32)]
    return _call_carrying(body, "conv_fwd", nt, in_specs, out_specs, out_shape, scratch, (ua, ua, cw, cvec), carry)


def _conv_bwd(ua, yconv, dya, cw, cvec):
    Lp = ua.shape[0]
    nt = Lp // TR
    hb = TR // CHUNK
    nhb = Lp // CHUNK

    def ln_bwd(y, dout, gate, v_ref):
        mu = jnp.mean(y, axis=-1, keepdims=True)
        d = y - mu
        var = jnp.mean(d * d, axis=-1, keepdims=True)
        rstd = lax.rsqrt(var + EPS)
        xhat = d * rstd
        yn = xhat * v_ref[1:2, :] + v_ref[2:3, :]
        s_gate, ds_gate = _silu_pair(gate)
        s_yn, ds_yn = _silu_pair(yn)
        dyn = dout * s_gate * ds_yn
        dxh = dyn * v_ref[1:2, :]
        dyc = rstd * (dxh - jnp.mean(dxh, axis=-1, keepdims=True) - xhat * jnp.mean(dxh * xhat, axis=-1, keepdims=True))
        return dyc, dyn, xhat, dout * s_yn * ds_gate

    def body(cur_ref, prev_ref, next_ref, yc_ref, ycn_ref, dy_ref, dyn_ref, w_ref, v_ref,
             du_ref, dw_ref, dv_ref, uext, dext, dwacc, ush, dsh):
        i = pl.program_id(0)

        @pl.when(i == 0)
        def _():
            dwacc[...] = jnp.zeros_like(dwacc)
            dv_ref[...] = jnp.zeros_like(dv_ref)

        row = i * TR + lax.broadcasted_iota(jnp.int32, (TR, 1), 0)
        hrow = i * TR - CHUNK + lax.broadcasted_iota(jnp.int32, (CHUNK, 1), 0)
        uext[pl.ds(0, CHUNK), :] = jnp.where(i > 0, _glu(prev_ref[...], hrow), 0.0)
        uext[pl.ds(CHUNK, TR), :] = _glu(cur_ref[...], row)

        s_b = jnp.zeros((1, D_CONV), F32)
        s_g = jnp.zeros((1, D_CONV), F32)
        s_bb = jnp.zeros((1, D_CONV), F32)
        for rb in range(TR // CONV_RB):
            rows = pl.ds(rb * CONV_RB, CONV_RB)
            gate = cur_ref[rows, 2 * D_CONV:3 * D_CONV].astype(F32)
            dout = dy_ref[rows, :].astype(F32)
            dyc, dyn, xhat, dgate = ln_bwd(yc_ref[rows, :], dout, gate, v_ref)
            du_ref[rows, 2 * D_CONV:3 * D_CONV] = dgate.astype(BF)
            dext[rows, :] = dyc
            s_b = s_b + jnp.sum(dyc, axis=0, keepdims=True)
            s_g = s_g + jnp.sum(dyn * xhat, axis=0, keepdims=True)
            s_bb = s_bb + jnp.sum(dyn, axis=0, keepdims=True)
        dv_ref[0:1, :] += s_b
        dv_ref[1:2, :] += s_g
        dv_ref[2:3, :] += s_bb
        dyc_n, _, _, _ = ln_bwd(ycn_ref[...], dyn_ref[...].astype(F32),
                                next_ref[:, 2 * D_CONV:3 * D_CONV].astype(F32), v_ref)
        dext[pl.ds(TR, CHUNK), :] = jnp.where(i < nt - 1, dyc_n, 0.0)
        _fill_shifts(uext, ush)
        _fill_shifts(dext, dsh)

        for rb in range(TR // CONV_RB):
            r0 = rb * CONV_RB
            rows = pl.ds(r0, CONV_RB)
            d_blk = dext[rows, :]
            dglu = jnp.zeros((CONV_RB, D_CONV), F32)
            for j in range(CONV_WIDTH):
                dglu = dglu + _shifted(dext, dsh, r0 + CONV_WIDTH - 1 - j, CONV_RB) * w_ref[j:j + 1, :]
                prod = d_blk * _shifted(uext, ush, r0 + CHUNK - (CONV_WIDTH - 1) + j, CONV_RB)
                part = prod[0:8, :]
                for s in range(1, CONV_RB // 8):
                    part = part + prod[8 * s:8 * s + 8, :]
                dwacc[j] += part
            a = cur_ref[rows, 0:D_CONV].astype(F32)
            sg = _sig(cur_ref[rows, D_CONV:2 * D_CONV].astype(F32))
            grow = i * TR + r0 + lax.broadcasted_iota(jnp.int32, (CONV_RB, 1), 0)
            dglu = jnp.where(grow >= META_PAD, dglu, 0.0)
            du_ref[rows, 0:D_CONV] = (dglu * sg).astype(BF)
            du_ref[rows, D_CONV:2 * D_CONV] = (dglu * a * sg * (1.0 - sg)).astype(BF)

        @pl.when(i == nt - 1)
        def _():
            dw_ref[...] = jnp.sum(dwacc[...], axis=1)

    nxt = lambda i: (jnp.minimum(i * hb + hb, nhb - 1), 0)
    return pl.pallas_call(
        body, name="conv_bwd", grid=(nt,),
        in_specs=[pl.BlockSpec((TR, N_A), lambda i: (i, 0)),
                  pl.BlockSpec((CHUNK, N_A), lambda i: (jnp.maximum(i * hb - 1, 0), 0)),
                  pl.BlockSpec((CHUNK, N_A), nxt),
                  pl.BlockSpec((TR, D_CONV), lambda i: (i, 0)),
                  pl.BlockSpec((CHUNK, D_CONV), nxt),
                  pl.BlockSpec((TR, D_CONV), lambda i: (i, 0)),
                  pl.BlockSpec((CHUNK, D_CONV), nxt),
                  pl.BlockSpec((CONV_WIDTH, D_CONV), lambda i: (0, 0)),
                  pl.BlockSpec((8, D_CONV), lambda i: (0, 0))],
        out_specs=[pl.BlockSpec((TR, N_A), lambda i: (i, 0)),
                   pl.BlockSpec((32, D_CONV), lambda i: (0, 0)),
                   pl.BlockSpec((8, D_CONV), lambda i: (0, 0))],
        out_shape=[jax.ShapeDtypeStruct((Lp, N_A), BF), jax.ShapeDtypeStruct((32, D_CONV), F32),
                   jax.ShapeDtypeStruct((8, D_CONV), F32)],
        scratch_shapes=[pltpu.VMEM((TR + CHUNK, D_CONV), F32), pltpu.VMEM((TR + CHUNK, D_CONV), F32),
                        pltpu.VMEM((32, 8, D_CONV), F32), pltpu.VMEM((7, _SH_ROWS, D_CONV), F32),
                        pltpu.VMEM((7, _SH_ROWS, D_CONV), F32)],
        compiler_params=_cp(("arbitrary",)),
    )(ua, ua, ua, yconv, yconv, dya, dya, cw, cvec)


def _hg_gates(ub_ref, lbv, row):
    q = ub_ref[:, 0:512].astype(F32)
    z = ub_ref[:, 512:1024].astype(F32)
    valid = row >= META_PAD
    sig = _sig(z)
    f = lbv + (1.0 - lbv) * sig
    g = jnp.where(valid, jnp.log(jnp.maximum(f, F_FLOOR)), 0.0)
    k = jnp.where(valid, (1.0 - lbv) * (1.0 - sig), 0.0)
    return q, k, g, sig, f


def _hg_chunk_terms(b_c, q_c, k_c):
    bm = b_c[CHUNK // 2 - 1:CHUNK // 2, :]
    bl = b_c[CHUNK - 1:CHUNK, :]
    e1 = jnp.exp(b_c - bm)
    e2 = jnp.exp(bm - b_c)
    e0 = jnp.exp(b_c)
    e3 = jnp.exp(bl - b_c)
    el = jnp.exp(bl)
    return e1, e2, e0, e3, el, q_c * e1, k_c * e2, q_c * e0, k_c * e3


def _hg_fwd(ub, lb, gn4, carry=None):
    Lp = ub.shape[0]
    nt = Lp // TR
    cpt = TR // CHUNK

    def body(ub_ref, lb_ref, gn_ref, yb_ref, o_ref, ss_ref, st, bsc, qsc, ksc, qes, els, ust, tlo):
        i = pl.program_id(0)

        @pl.when(i == 0)
        def _():
            st[...] = jnp.zeros_like(st)
            tlo[...] = _chunk_tri(TR, False)

        row = i * TR + lax.broadcasted_iota(jnp.int32, (TR, 1), 0)
        q, k, g, _, _ = _hg_gates(ub_ref, lb_ref[...], row)
        qsc[...] = _silu(q)
        ksc[...] = k
        bsc[...] = _mm_split(tlo[...], g)
        tri = lax.broadcasted_iota(jnp.int32, (CHUNK, CHUNK), 1) <= lax.broadcasted_iota(jnp.int32, (CHUNK, CHUNK), 0)

        def intra(c, carry):
            rows = pl.ds(pl.multiple_of(c * CHUNK, CHUNK), CHUNK)
            _, _, _, _, el, qe, ke, qE, kd = _hg_chunk_terms(bsc[rows, :], qsc[rows, :], ksc[rows, :])
            qe, ke, kd = qe.astype(BF), ke.astype(BF), kd.astype(BF)
            qes[rows, :] = qE.astype(BF)
            els[c] = jnp.broadcast_to(el, (8, 512))
            sls = [slice(HG_D * h, HG_D * (h + 1)) for h in range(HG_HEADS)]
            v = [ub_ref[rows, 1024 + HG_D * h:1024 + HG_D * (h + 1)] for h in range(HG_HEADS)]
            a = [_nt(qe[:, sl], ke[:, sl]) for sl in sls]
            u = [_tn(v[h], kd[:, sls[h]]) for h in range(HG_HEADS)]
            a = [jnp.where(tri, x, 0.0).astype(BF) for x in a]
            oi = [_nn(a[h], v[h]) for h in range(HG_HEADS)]
            for h in range(HG_HEADS):
                ust[c, h] = u[h]
                o_ref[rows, sls[h]] = oi[h]
            return carry

        lax.fori_loop(0, cpt, intra, 0, unroll=2)

        for h in range(HG_HEADS):
            sl = slice(HG_D * h, HG_D * (h + 1))
            s = st[h]
            for c in range(cpt):
                ss_ref[c, h] = s
                s = els[c, 0:1, sl] * s + ust[c, h]
            st[h] = s

        def inter(c, carry):
            rows = pl.ds(pl.multiple_of(c * CHUNK, CHUNK), CHUNK)
            for h in range(HG_HEADS):
                sl = slice(HG_D * h, HG_D * (h + 1))
                o_ref[rows, sl] += _nt(qes[rows, sl], ss_ref[c, h].astype(BF))
            return carry

        lax.fori_loop(0, cpt, inter, 0, unroll=2)

        gate = ub_ref[:, 1536:2048].astype(F32)
        for h in range(HG_HEADS):
            sl = slice(HG_D * h, HG_D * (h + 1))
            o = o_ref[:, sl]
            r = lax.rsqrt(jnp.mean(o * o, axis=-1, keepdims=True) + EPS)
            yb_ref[:, sl] = (o * r * gn_ref[:, sl] * _silu(gate[:, sl])).astype(BF)

    in_specs = [pl.BlockSpec((TR, N_B), lambda i: (i, 0)), pl.BlockSpec((1, 512), lambda i: (0, 0)),
                pl.BlockSpec((1, 512), lambda i: (0, 0))]
    out_specs = [pl.BlockSpec((TR, 512), lambda i: (i, 0)), pl.BlockSpec((TR, 512), lambda i: (i, 0)),
                 pl.BlockSpec((cpt, HG_HEADS, HG_D, HG_D), lambda i: (i, 0, 0, 0))]
    out_shape = [jax.ShapeDtypeStruct((Lp, 512), BF), jax.ShapeDtypeStruct((Lp, 512), F32),
                 jax.ShapeDtypeStruct((Lp // CHUNK, HG_HEADS, HG_D, HG_D), F32)]
    scratch = [pltpu.VMEM((HG_HEADS, HG_D, HG_D), F32), pltpu.VMEM((TR, 512), F32),
               pltpu.VMEM((TR, 512), F32), pltpu.VMEM((TR, 512), F32), pltpu.VMEM((TR, 512), BF),
               pltpu.VMEM((cpt, 8, 512), F32), pltpu.VMEM((cpt, HG_HEADS, HG_D, HG_D), F32), pltpu.VMEM((TR, TR), BF)]
    return _call_carrying(body, "hgrn_fwd", nt, in_specs, out_specs, out_shape, scratch, (ub, lb, gn4), carry)


def _hg_bwd(ub, lb, gn4, o_save, s_save, dyb, carry=None):
    Lp = ub.shape[0]
    nt = Lp // TR
    cpt = TR // CHUNK

    def body(ub_ref, lb_ref, gn_ref, o_ref, ss_ref, dy_ref, du_ref, ds_ref,
             dst, bsc, qsc, ksc, dosc, dqsc, dksc, dbsc, els, ust, dss, tlo, tup):
        i = pl.program_id(0)
        t = nt - 1 - i

        @pl.when(i == 0)
        def _():
            dst[...] = jnp.zeros_like(dst)
            ds_ref[...] = jnp.zeros_like(ds_ref)
            tlo[...] = _chunk_tri(TR, False)
            tup[...] = _chunk_tri(TR, True)

        lbv = lb_ref[...]
        row = t * TR + lax.broadcasted_iota(jnp.int32, (TR, 1), 0)
        valid = row >= META_PAD
        q, k, g, sig, f = _hg_gates(ub_ref, lbv, row)
        silu_q, dsilu_q = _silu_pair(q)
        qsc[...] = silu_q
        ksc[...] = k
        bsc[...] = _mm_split(tlo[...], g)

        gate = ub_ref[:, 1536:2048].astype(F32)
        dy = dy_ref[...].astype(F32)
        for h in range(HG_HEADS):
            sl = slice(HG_D * h, HG_D * (h + 1))
            o = o_ref[:, sl]
            r = lax.rsqrt(jnp.mean(o * o, axis=-1, keepdims=True) + EPS)
            ohat = o * r
            silu_g, dsilu_g = _silu_pair(gate[:, sl])
            don = dy[:, sl] * silu_g
            du_ref[:, 1536 + HG_D * h:1536 + HG_D * (h + 1)] = (dy[:, sl] * ohat * gn_ref[:, sl] * dsilu_g).astype(BF)
            ds_ref[1:2, sl] += jnp.sum(don * ohat, axis=0, keepdims=True)
            gd = don * gn_ref[:, sl]
            dosc[:, sl] = r * (gd - ohat * jnp.mean(gd * ohat, axis=-1, keepdims=True))

        tri = lax.broadcasted_iota(jnp.int32, (CHUNK, CHUNK), 1) <= lax.broadcasted_iota(jnp.int32, (CHUNK, CHUNK), 0)
        last = lax.broadcasted_iota(jnp.int32, (CHUNK, 1), 0) == CHUNK - 1

        def incr(c, carry):
            rows = pl.ds(pl.multiple_of(c * CHUNK, CHUNK), CHUNK)
            b_c = bsc[rows, :]
            qE_b = (qsc[rows, :] * jnp.exp(b_c)).astype(BF)
            els[c] = jnp.broadcast_to(jnp.exp(b_c[CHUNK - 1:CHUNK, :]), (8, 512))
            do_c = dosc[rows, :].astype(BF)
            for h in range(HG_HEADS):
                sl = slice(HG_D * h, HG_D * (h + 1))
                ust[c, h] = _tn(do_c[:, sl], qE_b[:, sl])
            return carry

        lax.fori_loop(0, cpt, incr, 0, unroll=2)

        for h in range(HG_HEADS):
            sl = slice(HG_D * h, HG_D * (h + 1))
            d_s = dst[h]
            for c in reversed(range(cpt)):
                dss[c, h] = d_s
                d_s = els[c, 0:1, sl] * d_s + ust[c, h]
            dst[h] = d_s

        def chunk(c, carry):
            r0 = pl.multiple_of(c * CHUNK, CHUNK)
            rows = pl.ds(r0, CHUNK)
            e1, e2, e0, e3, el, qe, ke, qE, kd = _hg_chunk_terms(bsc[rows, :], qsc[rows, :], ksc[rows, :])
            qe_b, ke_b, kd_b = qe.astype(BF), ke.astype(BF), kd.astype(BF)
            do_c = dosc[rows, :].astype(BF)
            hs = range(HG_HEADS)
            sls = [slice(HG_D * h, HG_D * (h + 1)) for h in hs]
            v = [ub_ref[rows, 1024 + HG_D * h:1024 + HG_D * (h + 1)] for h in hs]
            do = [do_c[:, sl] for sl in sls]
            a = [_nt(qe_b[:, sl], ke_b[:, sl]) for sl in sls]
            da = [_nt(do[h], v[h]) for h in hs]
            dqE = [_nn(do[h], ss_ref[c, h].astype(BF)) for h in hs]
            dkd = [_nn(v[h], dss[c, h].astype(BF)) for h in hs]
            dv2 = [_nt(kd_b[:, sls[h]], dss[c, h].astype(BF)) for h in hs]
            a = [jnp.where(tri, x, 0.0).astype(BF) for x in a]
            da = [jnp.where(tri, x, 0.0).astype(BF) for x in da]
            dv = [_tn(a[h], do[h]) + dv2[h] for h in hs]
            dqe = [_nn(da[h], ke_b[:, sls[h]]) for h in hs]
            dke = [_tn(da[h], qe_b[:, sls[h]]) for h in hs]
            for h in hs:
                sl = sls[h]
                del_h = jnp.sum(ss_ref[c, h] * dss[c, h], axis=0, keepdims=True)
                dqsc[rows, sl] = dqE[h] * e0[:, sl] + dqe[h] * e1[:, sl]
                dksc[rows, sl] = dke[h] * e2[:, sl] + dkd[h] * e3[:, sl]
                tkd = dkd[h] * kd[:, sl]
                dbl = jnp.sum(tkd, axis=0, keepdims=True) + del_h * el[:, sl]
                dbsc[rows, sl] = (dqE[h] * qE[:, sl] + dqe[h] * qe[:, sl] - dke[h] * ke[:, sl] - tkd
                                  + jnp.where(last, dbl, 0.0))
                du_ref[rows, 1024 + HG_D * h:1024 + HG_D * (h + 1)] = dv[h].astype(BF)
            return carry

        lax.fori_loop(0, cpt, chunk, 0, unroll=2)

        dg = _mm_split(tup[...], dbsc[...])
        df = jnp.where(valid & (f > F_FLOOR), dg / f, 0.0)
        dk = jnp.where(valid, dksc[...], 0.0)
        dsig = (df - dk) * (1.0 - lbv)
        ds_ref[0:1, :] += jnp.sum((df - dk) * (1.0 - sig), axis=0, keepdims=True)
        du_ref[:, 512:1024] = (dsig * sig * (1.0 - sig)).astype(BF)
        du_ref[:, 0:512] = (dqsc[...] * dsilu_q).astype(BF)

    rev = lambda i: (nt - 1 - i, 0)
    in_specs = [pl.BlockSpec((TR, N_B), rev), pl.BlockSpec((1, 512), lambda i: (0, 0)),
                pl.BlockSpec((1, 512), lambda i: (0, 0)), pl.BlockSpec((TR, 512), rev),
                pl.BlockSpec((cpt, HG_HEADS, HG_D, HG_D), lambda i: (nt - 1 - i, 0, 0, 0)),
                pl.BlockSpec((TR, 512), rev)]
    out_specs = [pl.BlockSpec((TR, N_B), rev), pl.BlockSpec((8, 512), lambda i: (0, 0))]
    out_shape = [jax.ShapeDtypeStruct((Lp, N_B), BF), jax.ShapeDtypeStruct((8, 512), F32)]
    states = pltpu.VMEM((cpt, HG_HEADS, HG_D, HG_D), F32)
    scratch = ([pltpu.VMEM((HG_HEADS, HG_D, HG_D), F32)] + [pltpu.VMEM((TR, 512), F32)] * 7
               + [pltpu.VMEM((cpt, 8, 512), F32), states, states, pltpu.VMEM((TR, TR), BF), pltpu.VMEM((TR, TR), BF)])
    return _call_carrying(body, "hgrn_bwd", nt, in_specs, out_specs, out_shape, scratch,
                          (ub, lb, gn4, o_save, s_save, dyb), carry)


_KCOL = (2 * 512) // 128
_VCOL = _KCOL + 1


def _swa_in_specs(nt, rev):
    tile = (lambda i: nt - 1 - i) if rev else (lambda i: i)
    hpt = TR // HALO
    return [
        pl.BlockSpec((TR, 512), lambda i: (tile(i), 0)),
        pl.BlockSpec((TR, 512), lambda i: (tile(i), 1)),
        pl.BlockSpec((TR, 128), lambda i: (tile(i), _KCOL)),
        pl.BlockSpec((TR, 128), lambda i: (tile(i), _VCOL)),
        pl.BlockSpec((HALO, 128), lambda i: (jnp.maximum(tile(i) * hpt - 1, 0), _KCOL)),
        pl.BlockSpec((HALO, 128), lambda i: (jnp.maximum(tile(i) * hpt - 1, 0), _VCOL)),
        pl.BlockSpec((CHUNK, 128), lambda i: (0, _KCOL)),
        pl.BlockSpec((CHUNK, 128), lambda i: (0, _VCOL)),
        pl.BlockSpec((1, 512), lambda i: (0, 0)),
        pl.BlockSpec((1, 128), lambda i: (0, 0)),
        pl.BlockSpec((1, ATT_Q_HEADS), lambda i: (0, 0)),
    ]


_WROWS = 2 * CHUNK + HALO + TR
_W0 = 2 * CHUNK
_C0 = _W0 + HALO
_SCALE = ATT_HD ** -0.5


def _group_ones(n):
    r = lax.broadcasted_iota(jnp.int32, (n, n), 0)
    c = lax.broadcasted_iota(jnp.int32, (n, n), 1)
    return jnp.where(jnp.right_shift(r, 6) == jnp.right_shift(c, 6), 1.0, 0.0).astype(BF)


def _group_mean(x, ones):
    hi = x.astype(BF)
    lo = (x - hi.astype(F32)).astype(BF)
    return (_nn(hi, ones) + _nn(lo, ones)) * (1.0 / ATT_HD)


def _head_rms(x, ones):
    r = lax.rsqrt(_group_mean(x * x, ones) + EPS)
    return x * r, r


def _swa_windows(kc_ref, vc_ref, kh_ref, vh_ref, km_ref, vm_ref, kg2, ones, kwin, krwin, vwin, vrwin):
    meta = pl.ds(META_PAD, N_META)
    for (k, v, r0, n) in ((km_ref[meta, :], vm_ref[meta, :], 0, N_META), (kh_ref[...], vh_ref[...], _W0, HALO),
                          (kc_ref[...], vc_ref[...], _C0, TR)):
        xhat, _ = _head_rms(k.astype(F32), ones)
        kn = xhat * kg2
        kwin[pl.ds(r0, n), :] = kn.astype(BF)
        krwin[pl.ds(r0, n), :] = pltpu.roll(kn, ATT_HD, 1).astype(BF)
        vwin[pl.ds(r0, n), :] = v
        if vrwin is not None:
            vrwin[pl.ds(r0, n), :] = pltpu.roll(v.astype(F32), ATT_HD, 1).astype(BF)
    zero = jnp.zeros((_W0 - N_META, 128), BF)
    for w in (kwin, krwin, vwin, vrwin):
        if w is not None:
            w[pl.ds(N_META, _W0 - N_META), :] = zero


def _swa_masks_t(t, qb):
    q0 = t * TR + qb * QB
    qc = jnp.right_shift(q0 + lax.broadcasted_iota(jnp.int32, (1, QB), 1), 6)
    kabs = q0 - HALO + lax.broadcasted_iota(jnp.int32, (QB + HALO, 1), 0)
    kc = jnp.right_shift(kabs + HALO, 6) - HALO // CHUNK
    mask_w = (kc <= qc) & (kc >= qc - 2) & (kabs >= META_PAD)
    return qc > 2, mask_w


def _swa_park(dtype):
    return [pltpu.VMEM((ATT_Q_HEADS, N_META, QB), dtype), pltpu.VMEM((ATT_Q_HEADS, QB + HALO, QB), dtype)]


def _swa_ones():
    return [pltpu.VMEM((128, 128), BF), pltpu.VMEM((512, 512), BF)]


def _split_heads(x, lane_hi):
    return jnp.where(lane_hi, 0.0, x).astype(BF), jnp.where(lane_hi, x, 0.0).astype(BF)


def _call_carrying(body, name, nt, in_specs, out_specs, out_shape, scratch, args, carry):
    if carry is None:
        return pl.pallas_call(body, name=name, grid=(nt,), in_specs=in_specs, out_specs=out_specs, out_shape=out_shape,
                              scratch_shapes=scratch, compiler_params=_cp(("arbitrary",)))(*args)
    kind, arrs = carry
    n = len(arrs)
    return pl.pallas_call(
        _carry_exchange(body, len(in_specs), len(out_specs), nt, kind, n), name=name + "_" + kind, grid=(nt,),
        in_specs=in_specs + [_ANY] * n, out_specs=out_specs + [_ANY] * n,
        out_shape=out_shape + _exchange_out_shapes(kind, arrs), scratch_shapes=scratch + _exchange_sems(n),
        compiler_params=_cp(("arbitrary",), has_side_effects=True),
    )(*args, *arrs)


def _swa_fwd(uc, qg8, kg2, sinks, carry=None):
    Lp = uc.shape[0]
    nt = Lp // TR
    nqb = TR // QB

    def body(q_ref, g_ref, kc_ref, vc_ref, kh_ref, vh_ref, km_ref, vm_ref, qg_ref, kg_ref, sk_ref,
             yc_ref, o_ref, lse_ref, kwin, krwin, vwin, vt, qlo, qhi, ot, s_m, s_w, p_m, p_w, g128, g512):
        t = pl.program_id(0)

        @pl.when(t == 0)
        def _():
            g128[...] = _group_ones(128)
            g512[...] = _group_ones(512)

        _swa_windows(kc_ref, vc_ref, kh_ref, vh_ref, km_ref, vm_ref, kg_ref[...], g128[...], kwin, krwin, vwin, None)
        vt[...] = vwin[...].T
        xhat, _ = _head_rms(q_ref[...].astype(F32), g512[...])
        lane_hi = (lax.broadcasted_iota(jnp.int32, (1, 512), 1) & ATT_HD) != 0
        lo, hi = _split_heads(xhat * qg_ref[...] * _SCALE, lane_hi)
        qlo[...] = lo
        qhi[...] = hi
        for qb in range(nqb):
            rows = pl.ds(qb * QB, QB)
            wrows = pl.ds(_W0 + qb * QB, QB + HALO)
            mrows = pl.ds(0, N_META)
            mask_m, mask_w = _swa_masks_t(t, qb)
            for j in range(ATT_Q_HEADS):
                p, e = j // 2, j % 2
                ks = kwin if e == j // ATT_GROUP else krwin
                qp = (qlo, qhi)[e][rows, 128 * p:128 * (p + 1)]
                s_m[j] = _nt(ks[mrows, :], qp)
                s_w[j] = _nt(ks[wrows, :], qp)
            inv = []
            for j in range(ATT_Q_HEADS):
                sm = jnp.where(mask_m, s_m[j], NEG)
                sw = jnp.where(mask_w, s_w[j], NEG)
                sink = sk_ref[:, j:j + 1]
                m = jnp.maximum(jnp.maximum(jnp.max(sm, axis=0, keepdims=True),
                                            jnp.max(sw, axis=0, keepdims=True)), sink)
                em = jnp.exp(sm - m)
                ew = jnp.exp(sw - m)
                den = jnp.sum(em, axis=0, keepdims=True) + jnp.sum(ew, axis=0, keepdims=True) + jnp.exp(sink - m)
                p_m[j] = em.astype(BF)
                p_w[j] = ew.astype(BF)
                lse_ref[j:j + 1, pl.ds(qb * QB, QB)] = m + jnp.log(den)
                inv.append(1.0 / den)
            for j in range(ATT_Q_HEADS):
                vrows = pl.ds(ATT_HD * (j // ATT_GROUP), ATT_HD)
                ot[pl.ds(ATT_HD * j, ATT_HD), pl.ds(qb * QB, QB)] = (
                    _nn(vt[vrows, pl.ds(0, N_META)], p_m[j])
                    + _nn(vt[vrows, pl.ds(_W0 + qb * QB, QB + HALO)], p_w[j])) * inv[j]
        o = ot[...].T
        o_ref[...] = o
        yc_ref[...] = (o * _silu(g_ref[...].astype(F32))).astype(BF)

    win = pltpu.VMEM((_WROWS, 128), BF)
    in_specs = _swa_in_specs(nt, False)
    out_specs = [pl.BlockSpec((TR, 512), lambda i: (i, 0)), pl.BlockSpec((TR, 512), lambda i: (i, 0)),
                 pl.BlockSpec((ATT_Q_HEADS, TR), lambda i: (0, i))]
    out_shape = [jax.ShapeDtypeStruct((Lp, 512), BF), jax.ShapeDtypeStruct((Lp, 512), F32),
                 jax.ShapeDtypeStruct((ATT_Q_HEADS, Lp), F32)]
    scratch = [win, win, win, pltpu.VMEM((128, _WROWS), BF), pltpu.VMEM((TR, 512), BF),
               pltpu.VMEM((TR, 512), BF), pltpu.VMEM((512, TR), F32)] + _swa_park(F32) + _swa_park(BF) + _swa_ones()
    return _call_carrying(body, "swa_fwd", nt, in_specs, out_specs, out_shape, scratch,
                          (uc, uc, uc, uc, uc, uc, uc, uc, qg8, kg2, sinks), carry)


def _swa_bwd(uc, qg8, kg2, sinks, o_save, lse, dyc):
    Lp = uc.shape[0]
    nt = Lp // TR
    nqb = TR // QB

    def body(q_ref, g_ref, kc_ref, vc_ref, kh_ref, vh_ref, km_ref, vm_ref, qg_ref, kg_ref, sk_ref,
             o_ref, lse_ref, dy_ref, du_ref, dg_ref, dsk_ref,
             kwin, krwin, vwin, vrwin, kt, krt, qlo, qhi, dolo, dohi, dqt, dk_dir, dk_rol, dv_dir, dv_rol,
             carry_k, carry_v, meta_k, meta_v, s_m, s_w, dp_m, dp_w, p_m, p_w, ds_m, ds_w, g128, g512):
        i = pl.program_id(0)
        t = nt - 1 - i

        @pl.when(i == 0)
        def _():
            carry_k[...] = jnp.zeros_like(carry_k)
            carry_v[...] = jnp.zeros_like(carry_v)
            meta_k[...] = jnp.zeros_like(meta_k)
            meta_v[...] = jnp.zeros_like(meta_v)
            dg_ref[...] = jnp.zeros_like(dg_ref)
            dsk_ref[...] = jnp.zeros_like(dsk_ref)
            g128[...] = _group_ones(128)
            g512[...] = _group_ones(512)

        ones128 = g128[...]
        ones512 = g512[...]
        _swa_windows(kc_ref, vc_ref, kh_ref, vh_ref, km_ref, vm_ref, kg_ref[...], ones128, kwin, krwin, vwin, vrwin)
        kt[...] = kwin[...].T
        krt[...] = krwin[...].T
        xhat_q, r_q = _head_rms(q_ref[...].astype(F32), ones512)
        lane_hi = (lax.broadcasted_iota(jnp.int32, (1, 512), 1) & ATT_HD) != 0
        lo, hi = _split_heads(xhat_q * qg_ref[...] * _SCALE, lane_hi)
        qlo[...] = lo
        qhi[...] = hi
        gate = g_ref[...].astype(F32)
        dy = dy_ref[...].astype(F32)
        silu_g, dsilu_g = _silu_pair(gate)
        do = dy * silu_g
        o = o_ref[...]
        du_ref[:, 512:1024] = (dy * o * dsilu_g).astype(BF)
        lo, hi = _split_heads(do, lane_hi)
        dolo[...] = lo
        dohi[...] = hi
        hsel = jnp.where(jnp.right_shift(lax.broadcasted_iota(jnp.int32, (ATT_Q_HEADS, 512), 1), 6)
                         == lax.broadcasted_iota(jnp.int32, (ATT_Q_HEADS, 512), 0), 1.0, 0.0).astype(BF)
        prod = do * o
        p_hi = prod.astype(BF)
        d_t = _nt(hsel, p_hi) + _nt(hsel, (prod - p_hi.astype(F32)).astype(BF))
        for acc in (dk_dir, dk_rol, dv_dir, dv_rol):
            acc[...] = jnp.zeros_like(acc)

        for qb in range(nqb):
            rows = pl.ds(qb * QB, QB)
            qcols = pl.ds(qb * QB, QB)
            wrows = pl.ds(_W0 + qb * QB, QB + HALO)
            mrows = pl.ds(0, N_META)
            mask_m, mask_w = _swa_masks_t(t, qb)
            for j in range(ATT_Q_HEADS):
                p, e = j // 2, j % 2
                ks, vs = (kwin, vwin) if e == j // ATT_GROUP else (krwin, vrwin)
                pair = slice(128 * p, 128 * (p + 1))
                qp = (qlo, qhi)[e][rows, pair]
                dop = (dolo, dohi)[e][rows, pair]
                s_m[j] = _nt(ks[mrows, :], qp)
                s_w[j] = _nt(ks[wrows, :], qp)
                dp_m[j] = _nt(vs[mrows, :], dop)
                dp_w[j] = _nt(vs[wrows, :], dop)
            for j in range(ATT_Q_HEADS):
                lse_j = lse_ref[j:j + 1, qcols]
                d_j = d_t[j:j + 1, qb * QB:(qb + 1) * QB]
                em = jnp.exp(jnp.where(mask_m, s_m[j], NEG) - lse_j)
                ew = jnp.exp(jnp.where(mask_w, s_w[j], NEG) - lse_j)
                p_m[j] = em.astype(BF)
                p_w[j] = ew.astype(BF)
                ds_m[j] = (em * (dp_m[j] - d_j)).astype(BF)
                ds_w[j] = (ew * (dp_w[j] - d_j)).astype(BF)
                dsk_ref[j:j + 1, :] -= jnp.exp(sk_ref[:, j:j + 1] - lse_j) * d_j
            for j in range(ATT_Q_HEADS):
                e = j % 2
                ktr = kt if e == j // ATT_GROUP else krt
                hrows = pl.ds(ATT_HD * e, ATT_HD)
                dqt[pl.ds(ATT_HD * j, ATT_HD), qcols] = (_nn(ktr[hrows, pl.ds(0, N_META)], ds_m[j])
                                                         + _nn(ktr[hrows, pl.ds(_W0 + qb * QB, QB + HALO)], ds_w[j]))
            for direct, dk_acc, dv_acc in ((True, dk_dir, dv_dir), (False, dk_rol, dv_rol)):
                heads = [j for j in range(ATT_Q_HEADS) if (j % 2 == j // ATT_GROUP) == direct]
                q_cat = jnp.concatenate([(qlo, qhi)[j % 2][rows, 128 * (j // 2):128 * (j // 2 + 1)] for j in heads], axis=0)
                do_cat = jnp.concatenate([(dolo, dohi)[j % 2][rows, 128 * (j // 2):128 * (j // 2 + 1)] for j in heads], axis=0)
                dk_acc[mrows, :] += _nn(jnp.concatenate([ds_m[j] for j in heads], axis=1), q_cat)
                dk_acc[wrows, :] += _nn(jnp.concatenate([ds_w[j] for j in heads], axis=1), q_cat)
                dv_acc[mrows, :] += _nn(jnp.concatenate([p_m[j] for j in heads], axis=1), do_cat)
                dv_acc[wrows, :] += _nn(jnp.concatenate([p_w[j] for j in heads], axis=1), do_cat)

        dk_dir[...] += pltpu.roll(dk_rol[...], ATT_HD, 1)
        dv_dir[...] += pltpu.roll(dv_rol[...], ATT_HD, 1)
        meta_k[...] += dk_dir[pl.ds(0, N_META), :]
        meta_v[...] += dv_dir[pl.ds(0, N_META), :]
        first = jnp.where(t == 0, 1.0, 0.0)
        dk_dir[pl.ds(_C0 + TR - HALO, HALO), :] += carry_k[...]
        dv_dir[pl.ds(_C0 + TR - HALO, HALO), :] += carry_v[...]
        dk_dir[pl.ds(_C0 + META_PAD, N_META), :] += first * meta_k[...]
        dv_dir[pl.ds(_C0 + META_PAD, N_META), :] += first * meta_v[...]
        carry_k[...] = dk_dir[pl.ds(_W0, HALO), :]
        carry_v[...] = dv_dir[pl.ds(_W0, HALO), :]

        du_ref[:, 1152:1280] = dv_dir[pl.ds(_C0, TR), :].astype(BF)
        xhat_k, r_k = _head_rms(kc_ref[...].astype(F32), ones128)
        dkn = dk_dir[pl.ds(_C0, TR), :]
        dg_ref[1:2, 0:128] += jnp.sum(dkn * xhat_k, axis=0, keepdims=True)
        gd = dkn * kg_ref[...]
        du_ref[:, 1024:1152] = (r_k * (gd - xhat_k * _group_mean(gd * xhat_k, ones128))).astype(BF)
        dqn = dqt[...].T * _SCALE
        dg_ref[0:1, :] += jnp.sum(dqn * xhat_q, axis=0, keepdims=True)
        gd = dqn * qg_ref[...]
        du_ref[:, 0:512] = (r_q * (gd - xhat_q * _group_mean(gd * xhat_q, ones512))).astype(BF)

    rev = lambda i: (nt - 1 - i, 0)
    specs = _swa_in_specs(nt, True)
    win = pltpu.VMEM((_WROWS, 128), BF)
    wint = pltpu.VMEM((128, _WROWS), BF)
    tile_bf = pltpu.VMEM((TR, 512), BF)
    acc = pltpu.VMEM((_WROWS, 128), F32)
    return pl.pallas_call(
        body, name="swa_bwd", grid=(nt,),
        in_specs=specs + [pl.BlockSpec((TR, 512), rev), pl.BlockSpec((ATT_Q_HEADS, TR), lambda i: (0, nt - 1 - i)),
                          pl.BlockSpec((TR, 512), rev)],
        out_specs=[pl.BlockSpec((TR, N_C), rev), pl.BlockSpec((8, 512), lambda i: (0, 0)),
                   pl.BlockSpec((8, 128), lambda i: (0, 0))],
        out_shape=[jax.ShapeDtypeStruct((Lp, N_C), BF), jax.ShapeDtypeStruct((8, 512), F32),
                   jax.ShapeDtypeStruct((8, 128), F32)],
        scratch_shapes=[win, win, win, win, wint, wint, tile_bf, tile_bf, tile_bf, tile_bf,
                        pltpu.VMEM((512, TR), F32), acc, acc, acc, acc,
                        pltpu.VMEM((HALO, 128), F32), pltpu.VMEM((HALO, 128), F32),
                        pltpu.VMEM((N_META, 128), F32), pltpu.VMEM((N_META, 128), F32)]
        + _swa_park(F32) + _swa_park(F32) + _swa_park(BF) + _swa_park(BF) + _swa_ones(),
        compiler_params=_cp(("arbitrary",)),
    )(uc, uc, uc, uc, uc, uc, uc, uc, qg8, kg2, sinks, o_save, lse, dyc)


def _mix_fwd(h, ya, yb, yc, ug, wa, wb, wc, wo, g_next):
    Lp = h.shape[0]
    wspec = lambda r: pl.BlockSpec((r, D_MODEL), lambda i: (0, 0))
    yspec = pl.BlockSpec((TRM, 512), lambda i: (i, 0))
    hspec = pl.BlockSpec((TRM, D_MODEL), lambda i: (i, 0))

    def body(h_ref, ya_ref, yb_ref, yc_ref, ug_ref, wa_ref, wb_ref, wc_ref, wo_ref, gn_ref,
             hn_ref, za_ref, zb_ref, zc_ref, mx_ref, nx_ref):
        mixed = jnp.zeros((TRM, D_MODEL), F32)
        for n, (y_ref, w_ref, z_ref) in enumerate(((ya_ref, wa_ref, za_ref), (yb_ref, wb_ref, zb_ref),
                                                   (yc_ref, wc_ref, zc_ref))):
            z = _nn(y_ref[...], w_ref[...])
            z_ref[...] = z.astype(BF)
            mixed = mixed + _sig(ug_ref[:, D_MODEL * n:D_MODEL * (n + 1)].astype(F32)) * z
        mixed = mixed.astype(BF)
        mx_ref[...] = mixed
        x = h_ref[...] + _nn(mixed, wo_ref[...])
        hn_ref[...] = x
        nx_ref[...] = (x * lax.rsqrt(jnp.mean(x * x, axis=-1, keepdims=True) + EPS) * gn_ref[...]).astype(BF)

    return pl.pallas_call(
        body, name="mix_fwd", grid=(Lp // TRM,),
        in_specs=[hspec, yspec, yspec, yspec, pl.BlockSpec((TRM, N_G), lambda i: (i, 0)),
                  wspec(512), wspec(512), wspec(512), wspec(D_MODEL), wspec(1)],
        out_specs=[hspec, hspec, hspec, hspec, hspec, hspec],
        out_shape=[jax.ShapeDtypeStruct((Lp, D_MODEL), F32)] + [jax.ShapeDtypeStruct((Lp, D_MODEL), BF)] * 5,
        compiler_params=_cp(("parallel",)),
    )(h, ya, yb, yc, ug, wa, wb, wc, wo, g_next)


def _mix_bwd(dh, za, zb, zc, ug, wa, wb, wc, wo):
    Lp = dh.shape[0]
    wspec = lambda r: pl.BlockSpec((r, D_MODEL), lambda i: (0, 0))
    yspec = pl.BlockSpec((TRM, 512), lambda i: (i, 0))
    hspec = pl.BlockSpec((TRM, D_MODEL), lambda i: (i, 0))
    gspec = pl.BlockSpec((TRM, N_G), lambda i: (i, 0))

    def body(dh_ref, za_ref, zb_ref, zc_ref, ug_ref, wa_ref, wb_ref, wc_ref, wo_ref,
             dug_ref, dza_ref, dzb_ref, dzc_ref, dya_ref, dyb_ref, dyc_ref):
        dmix = _nt(dh_ref[...].astype(BF), wo_ref[...])
        for n, (z_ref, w_ref, dz_ref, dy_ref) in enumerate(((za_ref, wa_ref, dza_ref, dya_ref),
                                                            (zb_ref, wb_ref, dzb_ref, dyb_ref),
                                                            (zc_ref, wc_ref, dzc_ref, dyc_ref))):
            sl = slice(D_MODEL * n, D_MODEL * (n + 1))
            gt = _sig(ug_ref[:, sl].astype(F32))
            dz = dmix * gt
            dug_ref[:, sl] = (dz * z_ref[...].astype(F32) * (1.0 - gt)).astype(BF)
            dz = dz.astype(BF)
            dz_ref[...] = dz
            dy_ref[...] = _nt(dz, w_ref[...]).astype(BF)

    bf = lambda n: jax.ShapeDtypeStruct((Lp, n), BF)
    return pl.pallas_call(
        body, name="mix_bwd", grid=(Lp // TRM,),
        in_specs=[hspec, hspec, hspec, hspec, gspec, wspec(512), wspec(512), wspec(512), wspec(D_MODEL)],
        out_specs=[gspec, hspec, hspec, hspec, yspec, yspec, yspec],
        out_shape=[bf(N_G), bf(D_MODEL), bf(D_MODEL), bf(D_MODEL), bf(512), bf(512), bf(512)],
        compiler_params=_cp(("parallel",)),
    )(dh, za, zb, zc, ug, wa, wb, wc, wo)


def _inproj_bwd(dus, ws, h, dh, g, carry=None):
    Lp = h.shape[0]
    widths = [w.shape[0] for w in ws]

    def body(dg_ref, da_ref, db_ref, dc_ref, wg_ref, wa_ref, wb_ref, wc_ref, h_ref, dh_ref, g_ref, o_ref, gg_ref):
        @pl.when(pl.program_id(0) == 0)
        def _():
            gg_ref[...] = jnp.zeros_like(gg_ref)

        dhn = (_nn(dg_ref[...], wg_ref[...]) + _nn(da_ref[...], wa_ref[...])
               + _nn(db_ref[...], wb_ref[...]) + _nn(dc_ref[...], wc_ref[...]))
        x = h_ref[...]
        r = lax.rsqrt(jnp.mean(x * x, axis=-1, keepdims=True) + EPS)
        xhat = x * r
        gg_ref[0:1, :] += jnp.sum(dhn * xhat, axis=0, keepdims=True)
        gd = dhn * g_ref[...]
        o_ref[...] = dh_ref[...] + r * (gd - xhat * jnp.mean(gd * xhat, axis=-1, keepdims=True))

    hspec = pl.BlockSpec((TRM, D_MODEL), lambda i: (i, 0))
    in_specs = ([pl.BlockSpec((TRM, n), lambda i: (i, 0)) for n in widths]
                + [pl.BlockSpec((n, D_MODEL), lambda i: (0, 0), pipeline_mode=pl.Buffered(1)) for n in widths]
                + [hspec, hspec, pl.BlockSpec((1, D_MODEL), lambda i: (0, 0))])
    out_specs = [hspec, pl.BlockSpec((8, D_MODEL), lambda i: (0, 0))]
    out_shape = [jax.ShapeDtypeStruct((Lp, D_MODEL), F32), jax.ShapeDtypeStruct((8, D_MODEL), F32)]
    return _call_carrying(body, "inproj_bwd", Lp // TRM, in_specs, out_specs, out_shape, [],
                          (*dus, *ws, h, dh, g), carry)


def _loss_head(h, tgt_pad, seq):
    Lp = h.shape[0]
    nt = Lp // TR

    def body(h_ref, t_ref, dh_ref, l_ref):
        i = pl.program_id(0)

        @pl.when(i == 0)
        def _():
            l_ref[...] = jnp.zeros_like(l_ref)

        row = i * TR + lax.broadcasted_iota(jnp.int32, (TR, 1), 0)
        e = jnp.where((row >= CHUNK) & (row < CHUNK + seq), h_ref[...] - t_ref[...], 0.0)
        dh_ref[...] = e * (1.0 / D_MODEL)
        l_ref[...] += (0.5 / D_MODEL) * jnp.sum(jnp.sum(e * e, axis=0, keepdims=True), axis=1, keepdims=True)

    hspec = pl.BlockSpec((TR, D_MODEL), lambda i: (i, 0))
    return pl.pallas_call(
        body, name="loss_head", grid=(nt,), in_specs=[hspec, hspec],
        out_specs=[hspec, pl.BlockSpec((8, 128), lambda i: (0, 0))],
        out_shape=[jax.ShapeDtypeStruct((Lp, D_MODEL), F32), jax.ShapeDtypeStruct((8, 128), F32)],
        compiler_params=_cp(("arbitrary",)),
    )(h, tgt_pad)


def _lb_softmax(lb_ref):
    x = lb_ref[...]
    e = jnp.exp(x - jnp.max(x, axis=0, keepdims=True))
    return e / jnp.sum(e, axis=0, keepdims=True)


def _lb_fwd(hg_lb):
    def body(lb_ref, o_ref):
        sm = _lb_softmax(lb_ref)
        acc = jnp.zeros((1, 512), F32)
        for l in range(DEPTH):
            if l > 0:
                acc = acc + sm[l:l + 1, :]
            o_ref[l:l + 1, :] = jnp.clip(acc, 0.0, 1.0)

    return pl.pallas_call(body, name="lb_fwd", out_shape=jax.ShapeDtypeStruct((DEPTH, 512), F32))(hg_lb)


def _lb_bwd(hg_lb, dlb_all):
    def body(lb_ref, d_ref, o_ref):
        sm = _lb_softmax(lb_ref)
        acc = jnp.zeros((1, 512), F32)
        gm = []
        for l in range(DEPTH):
            if l > 0:
                acc = acc + sm[l:l + 1, :]
            gm.append(jnp.where((acc >= 0.0) & (acc <= 1.0), d_ref[l:l + 1, :], 0.0))
        dsm = [jnp.zeros((1, 512), F32)]
        for j in range(1, DEPTH):
            s = gm[j]
            for l in range(j + 1, DEPTH):
                s = s + gm[l]
            dsm.append(s)
        dot = dsm[0] * sm[0:1, :]
        for j in range(1, DEPTH):
            dot = dot + dsm[j] * sm[j:j + 1, :]
        for j in range(DEPTH):
            o_ref[j:j + 1, :] = sm[j:j + 1, :] * (dsm[j] - dot)

    return pl.pallas_call(body, name="lb_bwd", out_shape=jax.ShapeDtypeStruct((DEPTH, 512), F32))(hg_lb, dlb_all)


_ANY = pl.BlockSpec(memory_space=pl.ANY)


def _chip_peers():
    x, y, c = lax.axis_index("x"), lax.axis_index("y"), lax.axis_index("c")
    return (x, y, c), [(1 - x, y, c), (x, 1 - y, c), (1 - x, 1 - y, c)]


def _exchange(kind, ins, outs, send, recv, loc):
    (x, y, c), peers = _chip_peers()
    me = 2 * x + y
    ds = []
    for a in range(len(ins)):
        if kind == "gather":
            ds.append(pltpu.make_async_copy(ins[a], outs[a].at[me], loc.at[a]))
        else:
            ds.append(pltpu.make_async_copy(ins[a].at[me], outs[a].at[0], loc.at[a]))
        for p, (px, py, pc) in enumerate(peers):
            src, dst = (ins[a], outs[a].at[me]) if kind == "gather" else (ins[a].at[2 * px + py], outs[a].at[1 + p])
            ds.append(pltpu.make_async_remote_copy(src_ref=src, dst_ref=dst, send_sem=send.at[a, p],
                                                   recv_sem=recv.at[a, p], device_id=(px, py, pc), device_id_type=MESH))
    return ds


def _exchange_out_shapes(kind, arrs):
    if kind == "gather":
        return [jax.ShapeDtypeStruct((4,) + a.shape, a.dtype) for a in arrs]
    return [jax.ShapeDtypeStruct(a.shape, a.dtype) for a in arrs]


def _exchange_sems(n):
    return [pltpu.SemaphoreType.DMA((n, 3)), pltpu.SemaphoreType.DMA((n, 3)), pltpu.SemaphoreType.DMA((n,))]


def _gather_first(arrs, split):
    n = len(arrs)

    def body(*refs):
        ins, outs = refs[:n], refs[n:2 * n]
        send, recv, loc, fsend, frecv = refs[2 * n:]
        (x, y, c), peers = _chip_peers()
        me = 2 * x + y

        def half(a):
            hr = arrs[a].shape[0] // 2
            return pl.ds(pl.multiple_of(c * hr, 16), hr)

        local = [pltpu.make_async_copy(ins[a], outs[a].at[me], loc.at[a]) for a in range(n)]
        far, fwd = {}, {}
        for a in range(n):
            for p, (px, py, pc) in enumerate(peers):
                src, dst = (ins[a].at[half(a)], outs[a].at[me, half(a)]) if split[a] else (ins[a], outs[a].at[me])
                far[a, p] = pltpu.make_async_remote_copy(src_ref=src, dst_ref=dst, send_sem=send.at[a, p],
                                                         recv_sem=recv.at[a, p], device_id=(px, py, pc), device_id_type=MESH)
                if split[a]:
                    landed = outs[a].at[2 * px + py, half(a)]
                    fwd[a, p] = pltpu.make_async_remote_copy(src_ref=landed, dst_ref=landed, send_sem=fsend.at[a, p],
                                                             recv_sem=frecv.at[a, p], device_id=(x, y, 1 - c),
                                                             device_id_type=MESH)
        for d in local + list(far.values()):
            d.start()
        for key, d in far.items():
            d.wait_recv()
            if key in fwd:
                fwd[key].start()
        for d in fwd.values():
            d.wait()
        for d in far.values():
            d.wait_send()
        for d in local:
            d.wait()

    sems = pltpu.SemaphoreType.DMA((n, 3))
    return pl.pallas_call(
        body, name="gather_first", in_specs=[_ANY] * n, out_specs=[_ANY] * n,
        out_shape=_exchange_out_shapes("gather", arrs),
        scratch_shapes=[sems, sems, pltpu.SemaphoreType.DMA((n,)), sems, sems],
        compiler_params=pltpu.CompilerParams(has_side_effects=True),
    )(*arrs)


def _carry_exchange(body, n_in, n_out, n_steps, kind, n):
    def wrapped(*refs):
        ins, cin = refs[:n_in], refs[n_in:n_in + n]
        outs, cout = refs[n_in + n:n_in + n + n_out], refs[n_in + n + n_out:n_in + 2 * n + n_out]
        scr, sems = refs[n_in + 2 * n + n_out:-3], refs[-3:]
        i = pl.program_id(0)

        @pl.when(i == 0)
        def _():
            for d in _exchange(kind, cin, cout, *sems):
                d.start()

        body(*ins, *outs, *scr)

        @pl.when(i == n_steps - 1)
        def _():
            for d in _exchange(kind, cin, cout, *sems):
                d.wait()

    return wrapped


def _swap_cores(arrs):
    n = len(arrs)

    def body(*refs):
        ins, outs = refs[:n], refs[n:2 * n]
        send, recv = refs[2 * n:]
        x, y, c = lax.axis_index("x"), lax.axis_index("y"), lax.axis_index("c")
        rdmas = []
        for a in range(n):
            r = pltpu.make_async_remote_copy(src_ref=ins[a], dst_ref=outs[a], send_sem=send.at[a], recv_sem=recv.at[a],
                                             device_id=(x, y, 1 - c), device_id_type=MESH)
            r.start()
            rdmas.append(r)
        for r in rdmas:
            r.wait()

    return pl.pallas_call(
        body, name="swap_cores", in_specs=[_ANY] * n, out_specs=[_ANY] * n,
        out_shape=[jax.ShapeDtypeStruct(a.shape, a.dtype) for a in arrs],
        scratch_shapes=[pltpu.SemaphoreType.DMA((n,)), pltpu.SemaphoreType.DMA((n,))],
        compiler_params=pltpu.CompilerParams(has_side_effects=True),
    )(*arrs)


def _allsum_small(p):
    R = p.shape[0]

    def body(p_ref, o_ref, buf, send, recv):
        x, y, c = lax.axis_index("x"), lax.axis_index("y"), lax.axis_index("c")
        me = 4 * x + 2 * y + c
        buf[me] = p_ref[...]
        rdmas = []
        for k in range(1, 8):
            peer = (x ^ (k >> 2), y ^ ((k >> 1) & 1), c ^ (k & 1))
            r = pltpu.make_async_remote_copy(src_ref=p_ref, dst_ref=buf.at[me], send_sem=send.at[k - 1],
                                             recv_sem=recv.at[k - 1], device_id=peer, device_id_type=MESH)
            r.start()
            rdmas.append(r)
        for r in rdmas:
            r.wait()
        acc = buf[0]
        for d in range(1, 8):
            acc = acc + buf[d]
        o_ref[...] = acc

    return pl.pallas_call(
        body, name="allsum_small", out_shape=jax.ShapeDtypeStruct((R, 512), F32),
        in_specs=[pl.BlockSpec(memory_space=pltpu.VMEM)], out_specs=pl.BlockSpec(memory_space=pltpu.VMEM),
        scratch_shapes=[pltpu.VMEM((8, R, 512), F32), pltpu.SemaphoreType.DMA((7,)), pltpu.SemaphoreType.DMA((7,))],
        compiler_params=_cp(has_side_effects=True),
    )(p)


def _row_block(rows):
    return max((d for d in range(16, 513, 16) if rows % d == 0), default=rows)


def _sum4(parts, name):
    _, R, C = parts.shape
    tr = _row_block(R)

    def body(p_ref, o_ref):
        p = [p_ref[k].astype(F32) for k in range(4)]
        o_ref[...] = ((p[0] + p[1]) + p[2]) + p[3]

    return pl.pallas_call(
        body, name=name, grid=(R // tr,), in_specs=[pl.BlockSpec((4, tr, C), lambda i: (0, i, 0))],
        out_specs=pl.BlockSpec((tr, C), lambda i: (i, 0)), out_shape=jax.ShapeDtypeStruct((R, C), F32),
        compiler_params=_cp(("parallel",)),
    )(parts)


def _adamw(w, m, v, g0, g1, name):
    L, R, C = w.shape
    tr = _row_block(R)
    two = g1 is not None
    c1 = 1.0 / (1.0 - ADAM_B1 ** ADAM_STEP)
    c2 = 1.0 / (1.0 - ADAM_B2 ** ADAM_STEP)

    def body(*refs):
        if two:
            w_ref, m_ref, v_ref, a_ref, b_ref, g_ref, d_ref, nm_ref, nv_ref = refs
            g = a_ref[...] + b_ref[...]
        else:
            w_ref, m_ref, v_ref, a_ref, g_ref, d_ref, nm_ref, nv_ref = refs
            g = a_ref[...]
        g_ref[...] = g
        m = ADAM_B1 * m_ref[...] + (1.0 - ADAM_B1) * g
        v = ADAM_B2 * v_ref[...] + (1.0 - ADAM_B2) * (g * g)
        nm_ref[...] = m
        nv_ref[...] = v
        d_ref[...] = -ADAM_LR * ((m * c1) / (jnp.sqrt(v * c2) + ADAM_EPS) + ADAM_WD * w_ref[...])

    spec = pl.BlockSpec((1, tr, C), lambda l, i: (l, i, 0))
    n_in = 5 if two else 4
    ins = (w, m, v, g0, g1) if two else (w, m, v, g0)
    return pl.pallas_call(
        body, name=name, grid=(L, R // tr), in_specs=[spec] * n_in, out_specs=[spec] * 4,
        out_shape=[jax.ShapeDtypeStruct((L, R, C), F32)] * 4, compiler_params=_cp(("parallel", "parallel")),
    )(*ins)


def _pad8(a):
    r = (-a.shape[0]) % 8
    return a if r == 0 else jnp.pad(a, ((0, r), (0, 0)))


def _local_step(x, tgt, meta, P, shards=None, prep=None, pack=None):
    seq = x.shape[0]
    Lp = -(-(seq + CHUNK) // TR) * TR
    tail = Lp - seq - CHUNK
    h = jnp.concatenate([jnp.zeros((META_PAD, D_MODEL), F32), meta, x, jnp.zeros((tail, D_MODEL), F32)], axis=0)
    tgt_pad = jnp.pad(tgt, ((CHUNK, tail), (0, 0)))

    P = list(P)
    saved = []
    hn = _rms_fwd(h, P[0]["norm_g"])
    for l in range(DEPTH):
        p = P[l]
        mm = functools.partial(_matmul, tb=True, out_dtype=BF, tm=TR, tk=D_MODEL, col_major_grid=True)
        ug = mm(hn, p["w_g"], tn=N_G // 2, name="inproj_g")
        ua = mm(hn, p["w_a"], tn=N_A, name="inproj_a")
        ub = mm(hn, p["w_b"], tn=N_B, name="inproj_b")
        uc = mm(hn, p["w_c"], tn=N_C, name="inproj_c")
        nxt = shards[l + 1] if shards is not None and l + 1 < DEPTH else None
        carry = (lambda part: ("gather", part)) if nxt is not None else (lambda part: None)
        res_a = _conv_fwd(ua, p["conv_w"], p["conv_vec"], carry(nxt and nxt[1:2]))
        res_b = _hg_fwd(ub, p["lb"], p["gn4"], carry(nxt and nxt[0:1]))
        res_c = _swa_fwd(uc, p["qg"], p["kg"], p["sinks"], carry(nxt and nxt[2:]))
        (ya, yconv), (yb, o_hg, s_hg), (yc, o_at, lse) = res_a[:2], res_b[:3], res_c[:3]
        if nxt is not None:
            P.append(prep(l + 1, [*res_b[3:], *res_a[2:], *res_c[3:]]))
        g_next = P[min(l + 1, DEPTH - 1)]["norm_g"]
        h_new, za, zb, zc, mixed, hn_next = _mix_fwd(h, ya, yb, yc, ug, p["w_ao"], p["w_bo"], p["w_co"], p["w_out"], g_next)
        saved.append(dict(h=h, hn=hn, ug=ug, ua=ua, ub=ub, uc=uc, ya=ya, yconv=yconv, yb=yb, o_hg=o_hg, s_hg=s_hg,
                          yc=yc, o_at=o_at, lse=lse, za=za, zb=zb, zc=zc, mixed=mixed))
        h, hn = h_new, hn_next

    dh, loss8 = _loss_head(h, tgt_pad, seq)

    grads = [None] * DEPTH
    parts = [[None, None] for _ in range(DEPTH)]
    pending = None
    tk_dw = 2 * TR if Lp % (2 * TR) == 0 else TR
    for l in reversed(range(DEPTH)):
        p, s = P[l], saved[l]
        dug, dza, dzb, dzc, dya, dyb, dyc = _mix_bwd(dh, s["za"], s["zb"], s["zc"], s["ug"],
                                                      p["w_ao"], p["w_bo"], p["w_co"], p["w_out"])
        tnmm = functools.partial(_matmul, ta=True, out_dtype=F32, tk=tk_dw)
        g = {}
        g["w_out"] = tnmm(s["mixed"], dh, tm=D_MODEL, tn=D_MODEL, name="dw_out")
        g["w_ao"] = tnmm(s["ya"], dza, tm=512, tn=D_MODEL, name="dw_ao")
        g["w_bo"] = tnmm(s["yb"], dzb, tm=512, tn=D_MODEL, name="dw_bo")
        g["w_co"] = tnmm(s["yc"], dzc, tm=512, tn=D_MODEL, name="dw_co")
        dua, g["conv_w"], g["conv_vec"] = _conv_bwd(s["ua"], s["yconv"], dya, p["conv_w"], p["conv_vec"])
        carry = ("scatter", pending[1]) if pending is not None else None
        res = _hg_bwd(s["ub"], p["lb"], p["gn4"], s["o_hg"], s["s_hg"], dyb, carry)
        dub, g["hg_small"] = res[:2]
        if carry is not None:
            parts[pending[0]][1] = res[2:]
        duc, g["at_gain"], g["at_sink"] = _swa_bwd(s["uc"], p["qg"], p["kg"], p["sinks"], s["o_at"], s["lse"], dyc)
        g["w_g"] = tnmm(dug, s["hn"], tm=N_G // 2, tn=D_MODEL, name="dw_in_g")
        g["w_a"] = tnmm(dua, s["hn"], tm=N_A, tn=D_MODEL, name="dw_in_a")
        g["w_b"] = tnmm(dub, s["hn"], tm=N_B, tn=D_MODEL, name="dw_in_b")
        g["w_c"] = tnmm(duc, s["hn"], tm=N_C, tn=D_MODEL, name="dw_in_c")
        first, second = pack(g) if pack is not None else (None, None)
        if first is not None and l == 0:
            first, second = first + second, []
        res = _inproj_bwd([dug, dua, dub, duc], [p["w_g"], p["w_a"], p["w_b"], p["w_c"]], s["h"], dh, p["norm_g"],
                          ("scatter", first) if first is not None else None)
        dh, g["norm_g"] = res[:2]
        grads[l] = g
        if pack is not None:
            parts[l] = [res[2:3], res[3:]] if l == 0 else [res[2:], None]
            pending = (l, second) if l > 0 else None
    return loss8, dh, grads, parts


def _split_w_in(wt):
    return dict(w_a=wt[0:1536], w_b=wt[1536:3584],
                w_c=jnp.concatenate([wt[3584:4096], wt[4352:4864], wt[4096:4352]], axis=0), w_g=wt[4864:7936])


def _join_w_in(g):
    c = g["w_c"]
    return jnp.concatenate([g["w_a"], g["w_b"], c[0:512], c[1024:1280], c[512:1024], g["w_g"]], axis=0)


def _attn_small(g):
    return (g["at_gain"][0].reshape(ATT_Q_HEADS, ATT_HD).sum(0),
            g["at_gain"][1, 0:128].reshape(ATT_KV_HEADS, ATT_HD).sum(0), g["at_sink"].sum(1))


_SMALL = (("norm_g", 8), ("meta", 32), ("conv_w", 32 * DEPTH), ("conv_b", 8), ("conv_ln_g", 8), ("conv_ln_b", 8),
          ("lb", 8), ("hg_norm_g", 8), ("q_norm_g", 8), ("k_norm_g", 8), ("sinks", 8))


def _small_offsets():
    off, o = {}, 0
    for name, rows in _SMALL:
        off[name] = (o, rows)
        o += rows
    return off, o


def _pack_small(d):
    parts = []
    for name, rows in _SMALL:
        a = d[name]
        parts.append(jnp.pad(a, ((0, rows - a.shape[0]), (0, 512 - a.shape[1]))))
    return jnp.concatenate(parts, axis=0)


def kernel(x, meta_tokens, norm_g, w_in, conv_w, conv_b, conv_ln_g, conv_ln_b, w_conv_out, hg_lower_bounds, hg_norm_g, w_hg_out, q_norm_g, k_norm_g, attn_sinks, w_att_out, w_out, loss_target, m_meta_tokens, m_norm_g, m_w_in, m_conv_w, m_conv_b, m_conv_ln_g, m_conv_ln_b, m_w_conv_out, m_hg_lower_bounds, m_hg_norm_g, m_w_hg_out, m_q_norm_g, m_k_norm_g, m_attn_sinks, m_w_att_out, m_w_out, v_meta_tokens, v_norm_g, v_w_in, v_conv_w, v_conv_b, v_conv_ln_g, v_conv_ln_b, v_w_conv_out, v_hg_lower_bounds, v_hg_norm_g, v_w_hg_out, v_q_norm_g, v_k_norm_g, v_attn_sinks, v_w_att_out, v_w_out):
    xi, yi = lax.axis_index("x"), lax.axis_index("y")
    chip = 2 * xi + yi
    NS = w_in.shape[2]
    CS = conv_w.shape[2]
    MS = meta_tokens.shape[1]

    half = NS // 2
    w_in_t, m_w_in_t, v_w_in_t = (jnp.swapaxes(t, 1, 2) for t in (w_in, m_w_in, v_w_in))
    shards = [[w_in_t[l, :half].astype(BF), w_in_t[l, half:].astype(BF), w_conv_out[l].astype(BF),
               w_hg_out[l].astype(BF), w_att_out[l].astype(BF), w_out[l].astype(BF)] for l in range(DEPTH)]
    *first, g_meta, g_convw = _gather_first(shards[0] + [meta_tokens, conv_w.reshape(DEPTH * CONV_WIDTH, CS)],
                                            [True] * len(shards[0]) + [False, False])
    cols = lambda g: g.transpose(1, 0, 2).reshape(g.shape[1], -1)
    meta_f = cols(g_meta)
    convw_f = cols(g_convw).reshape(DEPTH, CONV_WIDTH, D_CONV)
    lb_all = _lb_fwd(hg_lower_bounds)

    def prep(l, gathered):
        g_win_top, g_win_bot, g_wao, g_wbo, g_wco, g_wout = gathered
        p = _split_w_in(jnp.concatenate([g_win_top, g_win_bot], axis=1).reshape(4 * NS, D_MODEL))
        p.update(w_ao=cols(g_wao), w_bo=cols(g_wbo), w_co=cols(g_wco), w_out=g_wout.reshape(D_MODEL, D_MODEL),
                 norm_g=norm_g[l:l + 1], conv_w=convw_f[l],
                 conv_vec=_pad8(jnp.stack([conv_b[l], conv_ln_g[l], conv_ln_b[l]])),
                 lb=lb_all[l:l + 1], gn4=jnp.tile(hg_norm_g[l:l + 1], (1, HG_HEADS)),
                 qg=jnp.tile(q_norm_g[l:l + 1], (1, ATT_Q_HEADS)), kg=jnp.tile(k_norm_g[l:l + 1], (1, ATT_KV_HEADS)),
                 sinks=attn_sinks[l:l + 1])
        return p

    shard_cols = lambda a: a.reshape(a.shape[0], 4, -1).transpose(1, 0, 2)
    def pack(g):
        win = _join_w_in(g).reshape(4, NS, D_MODEL).astype(BF)
        return [win[:, :half]], [win[:, half:], shard_cols(g["w_ao"]).astype(BF), shard_cols(g["w_bo"]).astype(BF),
                                 shard_cols(g["w_co"]).astype(BF), g["w_out"].reshape(4, MS, D_MODEL).astype(BF)]

    loss8, dh0, grads, parts = _local_step(x[0], loss_target[0], meta_f, [prep(0, first)], shards, prep, pack)
    seq = x.shape[1]
    grad_x = dh0[CHUNK:CHUNK + seq][None]
    loss = lax.psum(loss8[0, 0], ("x", "y", "c"))

    sum4 = functools.partial(_sum4, name="sum_chips")
    mine = [jnp.concatenate([t for l in range(DEPTH) for t in (sum4(parts[l][0][0]), sum4(parts[l][1][0]))], axis=0)]
    mine += [jnp.concatenate([sum4(parts[l][1][a]) for l in range(DEPTH)], axis=0) for a in range(1, 5)]
    theirs = _swap_cores(mine)

    dlb_all = jnp.concatenate([grads[l]["hg_small"][0:1] for l in range(DEPTH)], axis=0)
    small = dict(
        norm_g=jnp.concatenate([grads[l]["norm_g"][0:1] for l in range(DEPTH)], axis=0).reshape(8, 512),
        meta=dh0[META_PAD:CHUNK].reshape(32, 512),
        conv_w=jnp.concatenate([grads[l]["conv_w"] for l in range(DEPTH)], axis=0),
        conv_b=jnp.concatenate([grads[l]["conv_vec"][0:1] for l in range(DEPTH)], axis=0),
        conv_ln_g=jnp.concatenate([grads[l]["conv_vec"][1:2] for l in range(DEPTH)], axis=0),
        conv_ln_b=jnp.concatenate([grads[l]["conv_vec"][2:3] for l in range(DEPTH)], axis=0),
        lb=_lb_bwd(hg_lower_bounds, dlb_all),
        hg_norm_g=jnp.concatenate([grads[l]["hg_small"][1:2].reshape(HG_HEADS, HG_D).sum(0, keepdims=True)
                                   for l in range(DEPTH)], axis=0),
        q_norm_g=jnp.stack([_attn_small(grads[l])[0] for l in range(DEPTH)]),
        k_norm_g=jnp.stack([_attn_small(grads[l])[1] for l in range(DEPTH)]),
        sinks=jnp.stack([_attn_small(grads[l])[2] for l in range(DEPTH)]),
    )
    gsum = _allsum_small(_pack_small(small))
    off, _ = _small_offsets()

    def take(name, rows, cols):
        o, _ = off[name]
        return gsum[o:o + rows, 0:cols]

    g_meta_full = take("meta", 32, 512).reshape(N_META, D_MODEL)
    g_convw_full = take("conv_w", 32 * DEPTH, 512).reshape(DEPTH, 32, 512)[:, :CONV_WIDTH]
    small_grads = dict(
        norm_g=take("norm_g", 8, 512),
        meta=lax.dynamic_slice_in_dim(g_meta_full, chip * MS, MS, axis=1),
        conv_w=lax.dynamic_slice_in_dim(g_convw_full, chip * CS, CS, axis=2).reshape(DEPTH * CONV_WIDTH, CS),
        conv_b=take("conv_b", DEPTH, 512), conv_ln_g=take("conv_ln_g", DEPTH, 512), conv_ln_b=take("conv_ln_b", DEPTH, 512),
        lb=take("lb", DEPTH, 512), hg_norm_g=take("hg_norm_g", DEPTH, HG_D), q_norm_g=take("q_norm_g", DEPTH, ATT_HD),
        k_norm_g=take("k_norm_g", DEPTH, ATT_HD), sinks=take("sinks", DEPTH, ATT_Q_HEADS))

    def big_update(w, m, v, a, b, name):
        return _adamw(w, m, v, a.reshape(w.shape), b.reshape(w.shape), name)

    res = {}
    res["w_in"] = [jnp.swapaxes(t, 1, 2) for t in big_update(w_in_t, m_w_in_t, v_w_in_t, mine[0], theirs[0], "adamw_w_in")]
    res["w_conv_out"] = big_update(w_conv_out, m_w_conv_out, v_w_conv_out, mine[1], theirs[1], "adamw_w_ao")
    res["w_hg_out"] = big_update(w_hg_out, m_w_hg_out, v_w_hg_out, mine[2], theirs[2], "adamw_w_bo")
    res["w_att_out"] = big_update(w_att_out, m_w_att_out, v_w_att_out, mine[3], theirs[3], "adamw_w_co")
    res["w_out"] = big_update(w_out, m_w_out, v_w_out, mine[4], theirs[4], "adamw_w_out")

    small_w = dict(meta=(meta_tokens, m_meta_tokens, v_meta_tokens), norm_g=(norm_g, m_norm_g, v_norm_g),
                   conv_w=(conv_w, m_conv_w, v_conv_w), conv_b=(conv_b, m_conv_b, v_conv_b),
                   conv_ln_g=(conv_ln_g, m_conv_ln_g, v_conv_ln_g), conv_ln_b=(conv_ln_b, m_conv_ln_b, v_conv_ln_b),
                   lb=(hg_lower_bounds, m_hg_lower_bounds, v_hg_lower_bounds),
                   hg_norm_g=(hg_norm_g, m_hg_norm_g, v_hg_norm_g), q_norm_g=(q_norm_g, m_q_norm_g, v_q_norm_g),
                   k_norm_g=(k_norm_g, m_k_norm_g, v_k_norm_g), sinks=(attn_sinks, m_attn_sinks, v_attn_sinks))
    view = lambda n, t: t.reshape(-1, 512) if n == "norm_g" else t.reshape(-1, t.shape[-1])
    pw, pm, pv = (_pack_rows([view(n, small_w[n][k]) for n in small_w]) for k in range(3))
    pg = _pack_rows([small_grads[n] for n in small_w])
    packed = [t[0] for t in _adamw(pw[None], pm[None], pv[None], pg[None], None, "adamw_small")]
    o = 0
    for n in small_w:
        r, cdim = view(n, small_w[n][0]).shape
        res[n] = [t[o:o + r, 0:cdim].reshape(small_w[n][0].shape) for t in packed]
        o += -(-r // 8) * 8

    order = [("meta", None), ("norm_g", None), ("w_in", None), ("conv_w", None), ("conv_b", None), ("conv_ln_g", None),
             ("conv_ln_b", None), ("w_conv_out", None), ("lb", None), ("hg_norm_g", None), ("w_hg_out", None),
             ("q_norm_g", None), ("k_norm_g", None), ("sinks", None), ("w_att_out", None), ("w_out", None)]
    outs = [loss, grad_x]
    for k in range(4):
        outs += [res[n][k] for n, _ in order]
    return tuple(outs)


def _pack_rows(arrs):
    parts = []
    for a in arrs:
        r = (-a.shape[0]) % 8
        parts.append(jnp.pad(a, ((0, r), (0, 512 - a.shape[1]))))
    return jnp.concatenate(parts, axis=0)
```

```python
import functools

import jax
import jax.numpy as jnp
from jax import lax
from jax.experimental import pallas as pl
from jax.experimental.pallas import tpu as pltpu

F32 = jnp.float32
BF = jnp.bfloat16

D_MODEL = 1024
DEPTH = 4
CHUNK = 64
N_META = 16
META_PAD = CHUNK - N_META
D_CONV = 512
CONV_WIDTH = 31
HG_HEADS = 4
HG_D = 128
ATT_Q_HEADS = 8
ATT_KV_HEADS = 2
ATT_HD = 64
ATT_GROUP = ATT_Q_HEADS // ATT_KV_HEADS
EPS = 1e-6
F_FLOOR = 1e-30
NEG = -1e30

ADAM_LR = 0.001
ADAM_B1 = 0.9
ADAM_B2 = 0.999
ADAM_EPS = 1e-08
ADAM_WD = 0.01
ADAM_STEP = 10

TR = 640
TRM = TR // 2
CONV_RB = 32
QB = 128
HALO = 128
VMEM_LIMIT = 56 * 1024 * 1024

N_G, N_A, N_B, N_C = 3 * D_MODEL, 3 * D_CONV, 4 * 512, 2 * 512 + 2 * 128

MESH = pl.DeviceIdType.MESH


def _cp(sem=None, vmem=VMEM_LIMIT, **kw):
    if sem is None:
        return pltpu.CompilerParams(vmem_limit_bytes=vmem, **kw)
    return pltpu.CompilerParams(dimension_semantics=sem, vmem_limit_bytes=vmem, **kw)


def _nn(a, b):
    return lax.dot_general(a, b, (((1,), (0,)), ((), ())), preferred_element_type=F32)


def _nt(a, b):
    return lax.dot_general(a, b, (((1,), (1,)), ((), ())), preferred_element_type=F32)


def _tn(a, b):
    return lax.dot_general(a, b, (((0,), (0,)), ((), ())), preferred_element_type=F32)


def _sig(x):
    return jax.nn.sigmoid(x)


def _silu(x):
    return x * _sig(x)


def _silu_pair(x):
    s = _sig(x)
    return x * s, s * (1.0 + x * (1.0 - s))


def _mm_split(t, x):
    hi = x.astype(BF)
    lo = (x - hi.astype(F32)).astype(BF)
    return _nn(t, hi) + _nn(t, lo)


def _chunk_tri(n, upper):
    r = lax.broadcasted_iota(jnp.int32, (n, n), 0)
    c = lax.broadcasted_iota(jnp.int32, (n, n), 1)
    same = jnp.right_shift(r, 6) == jnp.right_shift(c, 6)
    tri = (c >= r) if upper else (c <= r)
    return jnp.where(same & tri, 1.0, 0.0).astype(BF)


def _matmul(a, b, *, ta=False, tb=False, out_dtype, tm, tn, tk, name, col_major_grid=False):
    if ta:
        K, M = a.shape
    else:
        M, K = a.shape
    N = b.shape[0] if tb else b.shape[1]
    assert M % tm == 0 and N % tn == 0 and K % tk == 0, (name, M, N, K, tm, tn, tk)
    nk = K // tk
    if col_major_grid:
        grid = (N // tn, M // tm, nk)
        ij = lambda g0, g1: (g1, g0)
    else:
        grid = (M // tm, N // tn, nk)
        ij = lambda g0, g1: (g0, g1)
    if ta:
        a_spec = pl.BlockSpec((tk, tm), lambda g0, g1, k: (k, ij(g0, g1)[0]))
    else:
        a_spec = pl.BlockSpec((tm, tk), lambda g0, g1, k: (ij(g0, g1)[0], k))
    if tb:
        b_spec = pl.BlockSpec((tn, tk), lambda g0, g1, k: (ij(g0, g1)[1], k))
    else:
        b_spec = pl.BlockSpec((tk, tn), lambda g0, g1, k: (k, ij(g0, g1)[1]))
    o_spec = pl.BlockSpec((tm, tn), lambda g0, g1, k: ij(g0, g1))
    dims = (((0 if ta else 1,), (1 if tb else 0,)), ((), ()))
    use_acc = nk > 1 and out_dtype != F32

    def body(a_ref, b_ref, o_ref, *scr):
        k = pl.program_id(2)
        p = lax.dot_general(a_ref[...].astype(BF), b_ref[...].astype(BF), dims, preferred_element_type=F32)
        if nk == 1:
            o_ref[...] = p.astype(out_dtype)
        else:
            acc = scr[0] if use_acc else o_ref

            @pl.when(k == 0)
            def _():
                acc[...] = p

            @pl.when(k > 0)
            def _():
                acc[...] += p

            if use_acc:
                @pl.when(k == nk - 1)
                def _():
                    o_ref[...] = acc[...].astype(out_dtype)

    return pl.pallas_call(
        body, name=name, grid=grid, in_specs=[a_spec, b_spec], out_specs=o_spec,
        out_shape=jax.ShapeDtypeStruct((M, N), out_dtype),
        scratch_shapes=[pltpu.VMEM((tm, tn), F32)] if use_acc else [],
        compiler_params=_cp(("parallel", "parallel", "arbitrary")),
    )(a, b)


def _rms_fwd(h, g):
    Lp = h.shape[0]

    def body(h_ref, g_ref, o_ref):
        x = h_ref[...]
        r = lax.rsqrt(jnp.mean(x * x, axis=-1, keepdims=True) + EPS)
        o_ref[...] = (x * r * g_ref[...]).astype(BF)

    return pl.pallas_call(
        body, name="rms_fwd", grid=(Lp // TR,),
        in_specs=[pl.BlockSpec((TR, D_MODEL), lambda i: (i, 0)), pl.BlockSpec((1, D_MODEL), lambda i: (0, 0))],
        out_specs=pl.BlockSpec((TR, D_MODEL), lambda i: (i, 0)),
        out_shape=jax.ShapeDtypeStruct((Lp, D_MODEL), BF),
        compiler_params=_cp(("parallel",)),
    )(h, g)


def _glu(ua, row):
    a = ua[:, 0:D_CONV].astype(F32)
    gl = ua[:, D_CONV:2 * D_CONV].astype(F32)
    return jnp.where(row >= META_PAD, a * _sig(gl), 0.0)


_SH_ROWS = TR + CHUNK - 8


def _fill_shifts(src, sh):
    for b in range(1, 8):
        sh[b - 1] = src[pl.ds(b, _SH_ROWS), :]


def _shifted(src, sh, start, n):
    b = start % 8
    if b == 0:
        return src[pl.ds(start, n), :]
    return sh[b - 1, pl.ds(start - b, n), :]


def _conv_fwd(ua, cw, cvec, carry=None):
    Lp = ua.shape[0]
    nt = Lp // TR
    hb = TR // CHUNK

    def body(cur_ref, halo_ref, w_ref, v_ref, ya_ref, yc_ref, ext, sh):
        i = pl.program_id(0)
        row = i * TR + lax.broadcasted_iota(jnp.int32, (TR, 1), 0)
        hrow = i * TR - CHUNK + lax.broadcasted_iota(jnp.int32, (CHUNK, 1), 0)
        ext[pl.ds(0, CHUNK), :] = jnp.where(i > 0, _glu(halo_ref[...], hrow), 0.0)
        ext[pl.ds(CHUNK, TR), :] = _glu(cur_ref[...], row)
        _fill_shifts(ext, sh)
        for rb in range(TR // CONV_RB):
            r0 = rb * CONV_RB
            rows = pl.ds(r0, CONV_RB)
            acc = jnp.zeros((CONV_RB, D_CONV), F32)
            for j in range(CONV_WIDTH):
                acc = acc + _shifted(ext, sh, r0 + CHUNK - (CONV_WIDTH - 1) + j, CONV_RB) * w_ref[j:j + 1, :]
            y = acc + v_ref[0:1, :]
            yc_ref[rows, :] = y
            mu = jnp.mean(y, axis=-1, keepdims=True)
            d = y - mu
            var = jnp.mean(d * d, axis=-1, keepdims=True)
            yn = d * lax.rsqrt(var + EPS) * v_ref[1:2, :] + v_ref[2:3, :]
            ya_ref[rows, :] = (_silu(yn) * _silu(cur_ref[rows, 2 * D_CONV:3 * D_CONV].astype(F32))).astype(BF)

    in_specs = [pl.BlockSpec((TR, N_A), lambda i: (i, 0)),
                pl.BlockSpec((CHUNK, N_A), lambda i: (jnp.maximum(i * hb - 1, 0), 0)),
                pl.BlockSpec((CONV_WIDTH, D_CONV), lambda i: (0, 0)),
                pl.BlockSpec((8, D_CONV), lambda i: (0, 0))]
    out_specs = [pl.BlockSpec((TR, D_CONV), lambda i: (i, 0)), pl.BlockSpec((TR, D_CONV), lambda i: (i, 0))]
    out_shape = [jax.ShapeDtypeStruct((Lp, D_CONV), BF), jax.ShapeDtypeStruct((Lp, D_CONV), F32)]
    scratch = [pltpu.VMEM((TR + CHUNK, D_CONV), F32), pltpu.VMEM((7, _SH_ROWS, D_CONV), F32)]
    return _call_carrying(body, "conv_fwd", nt, in_specs, out_specs, out_shape, scratch, (ua, ua, cw, cvec), carry)


def _conv_bwd(ua, yconv, dya, cw, cvec):
    Lp = ua.shape[0]
    nt = Lp // TR
    hb = TR // CHUNK
    nhb = Lp // CHUNK

    def ln_bwd(y, dout, gate, v_ref):
        mu = jnp.mean(y, axis=-1, keepdims=True)
        d = y - mu
        var = jnp.mean(d * d, axis=-1, keepdims=True)
        rstd = lax.rsqrt(var + EPS)
        xhat = d * rstd
        yn = xhat * v_ref[1:2, :] + v_ref[2:3, :]
        s_gate, ds_gate = _silu_pair(gate)
        s_yn, ds_yn = _silu_pair(yn)
        dyn = dout * s_gate * ds_yn
        dxh = dyn * v_ref[1:2, :]
        dyc = rstd * (dxh - jnp.mean(dxh, axis=-1, keepdims=True) - xhat * jnp.mean(dxh * xhat, axis=-1, keepdims=True))
        return dyc, dyn, xhat, dout * s_yn * ds_gate

    def body(cur_ref, prev_ref, next_ref, yc_ref, ycn_ref, dy_ref, dyn_ref, w_ref, v_ref,
             du_ref, dw_ref, dv_ref, uext, dext, dwacc, ush, dsh):
        i = pl.program_id(0)

        @pl.when(i == 0)
        def _():
            dwacc[...] = jnp.zeros_like(dwacc)
            dv_ref[...] = jnp.zeros_like(dv_ref)

        row = i * TR + lax.broadcasted_iota(jnp.int32, (TR, 1), 0)
        hrow = i * TR - CHUNK + lax.broadcasted_iota(jnp.int32, (CHUNK, 1), 0)
        uext[pl.ds(0, CHUNK), :] = jnp.where(i > 0, _glu(prev_ref[...], hrow), 0.0)
        uext[pl.ds(CHUNK, TR), :] = _glu(cur_ref[...], row)

        s_b = jnp.zeros((1, D_CONV), F32)
        s_g = jnp.zeros((1, D_CONV), F32)
        s_bb = jnp.zeros((1, D_CONV), F32)
        for rb in range(TR // CONV_RB):
            rows = pl.ds(rb * CONV_RB, CONV_RB)
            gate = cur_ref[rows, 2 * D_CONV:3 * D_CONV].astype(F32)
            dout = dy_ref[rows, :].astype(F32)
            dyc, dyn, xhat, dgate = ln_bwd(yc_ref[rows, :], dout, gate, v_ref)
            du_ref[rows, 2 * D_CONV:3 * D_CONV] = dgate.astype(BF)
            dext[rows, :] = dyc
            s_b = s_b + jnp.sum(dyc, axis=0, keepdims=True)
            s_g = s_g + jnp.sum(dyn * xhat, axis=0, keepdims=True)
            s_bb = s_bb + jnp.sum(dyn, axis=0, keepdims=True)
        dv_ref[0:1, :] += s_b
        dv_ref[1:2, :] += s_g
        dv_ref[2:3, :] += s_bb
        dyc_n, _, _, _ = ln_bwd(ycn_ref[...], dyn_ref[...].astype(F32),
                                next_ref[:, 2 * D_CONV:3 * D_CONV].astype(F32), v_ref)
        dext[pl.ds(TR, CHUNK), :] = jnp.where(i < nt - 1, dyc_n, 0.0)
        _fill_shifts(uext, ush)
        _fill_shifts(dext, dsh)

        for rb in range(TR // CONV_RB):
            r0 = rb * CONV_RB
            rows = pl.ds(r0, CONV_RB)
            d_blk = dext[rows, :]
            dglu = jnp.zeros((CONV_RB, D_CONV), F32)
            for j in range(CONV_WIDTH):
                dglu = dglu + _shifted(dext, dsh, r0 + CONV_WIDTH - 1 - j, CONV_RB) * w_ref[j:j + 1, :]
                prod = d_blk * _shifted(uext, ush, r0 + CHUNK - (CONV_WIDTH - 1) + j, CONV_RB)
                part = prod[0:8, :]
                for s in range(1, CONV_RB // 8):
                    part = part + prod[8 * s:8 * s + 8, :]
                dwacc[j] += part
            a = cur_ref[rows, 0:D_CONV].astype(F32)
            sg = _sig(cur_ref[rows, D_CONV:2 * D_CONV].astype(F32))
            grow = i * TR + r0 + lax.broadcasted_iota(jnp.int32, (CONV_RB, 1), 0)
            dglu = jnp.where(grow >= META_PAD, dglu, 0.0)
            du_ref[rows, 0:D_CONV] = (dglu * sg).astype(BF)
            du_ref[rows, D_CONV:2 * D_CONV] = (dglu * a * sg * (1.0 - sg)).astype(BF)

        @pl.when(i == nt - 1)
        def _():
            dw_ref[...] = jnp.sum(dwacc[...], axis=1)

    nxt = lambda i: (jnp.minimum(i * hb + hb, nhb - 1), 0)
    return pl.pallas_call(
        body, name="conv_bwd", grid=(nt,),
        in_specs=[pl.BlockSpec((TR, N_A), lambda i: (i, 0)),
                  pl.BlockSpec((CHUNK, N_A), lambda i: (jnp.maximum(i * hb - 1, 0), 0)),
                  pl.BlockSpec((CHUNK, N_A), nxt),
                  pl.BlockSpec((TR, D_CONV), lambda i: (i, 0)),
                  pl.BlockSpec((CHUNK, D_CONV), nxt),
                  pl.BlockSpec((TR, D_CONV), lambda i: (i, 0)),
                  pl.BlockSpec((CHUNK, D_CONV), nxt),
                  pl.BlockSpec((CONV_WIDTH, D_CONV), lambda i: (0, 0)),
                  pl.BlockSpec((8, D_CONV), lambda i: (0, 0))],
        out_specs=[pl.BlockSpec((TR, N_A), lambda i: (i, 0)),
                   pl.BlockSpec((32, D_CONV), lambda i: (0, 0)),
                   pl.BlockSpec((8, D_CONV), lambda i: (0, 0))],
        out_shape=[jax.ShapeDtypeStruct((Lp, N_A), BF), jax.ShapeDtypeStruct((32, D_CONV), F32),
                   jax.ShapeDtypeStruct((8, D_CONV), F32)],
        scratch_shapes=[pltpu.VMEM((TR + CHUNK, D_CONV), F32), pltpu.VMEM((TR + CHUNK, D_CONV), F32),
                        pltpu.VMEM((32, 8, D_CONV), F32), pltpu.VMEM((7, _SH_ROWS, D_CONV), F32),
                        pltpu.VMEM((7, _SH_ROWS, D_CONV), F32)],
        compiler_params=_cp(("arbitrary",)),
    )(ua, ua, ua, yconv, yconv, dya, dya, cw, cvec)


def _hg_gates(ub_ref, lbv, row):
    q = ub_ref[:, 0:512].astype(F32)
    z = ub_ref[:, 512:1024].astype(F32)
    valid = row >= META_PAD
    sig = _sig(z)
    f = lbv + (1.0 - lbv) * sig
    g = jnp.where(valid, jnp.log(jnp.maximum(f, F_FLOOR)), 0.0)
    k = jnp.where(valid, (1.0 - lbv) * (1.0 - sig), 0.0)
    return q, k, g, sig, f


def _hg_chunk_terms(b_c, q_c, k_c):
    bm = b_c[CHUNK // 2 - 1:CHUNK // 2, :]
    bl = b_c[CHUNK - 1:CHUNK, :]
    e1 = jnp.exp(b_c - bm)
    e2 = jnp.exp(bm - b_c)
    e0 = jnp.exp(b_c)
    e3 = jnp.exp(bl - b_c)
    el = jnp.exp(bl)
    return e1, e2, e0, e3, el, q_c * e1, k_c * e2, q_c * e0, k_c * e3


def _hg_fwd(ub, lb, gn4, carry=None):
    Lp = ub.shape[0]
    nt = Lp // TR
    cpt = TR // CHUNK

    def body(ub_ref, lb_ref, gn_ref, yb_ref, o_ref, ss_ref, st, bsc, qsc, ksc, qes, els, ust, tlo):
        i = pl.program_id(0)

        @pl.when(i == 0)
        def _():
            st[...] = jnp.zeros_like(st)
            tlo[...] = _chunk_tri(TR, False)

        row = i * TR + lax.broadcasted_iota(jnp.int32, (TR, 1), 0)
        q, k, g, _, _ = _hg_gates(ub_ref, lb_ref[...], row)
        qsc[...] = _silu(q)
        ksc[...] = k
        bsc[...] = _mm_split(tlo[...], g)
        tri = lax.broadcasted_iota(jnp.int32, (CHUNK, CHUNK), 1) <= lax.broadcasted_iota(jnp.int32, (CHUNK, CHUNK), 0)

        def intra(c, carry):
            rows = pl.ds(pl.multiple_of(c * CHUNK, CHUNK), CHUNK)
            _, _, _, _, el, qe, ke, qE, kd = _hg_chunk_terms(bsc[rows, :], qsc[rows, :], ksc[rows, :])
            qe, ke, kd = qe.astype(BF), ke.astype(BF), kd.astype(BF)
            qes[rows, :] = qE.astype(BF)
            els[c] = jnp.broadcast_to(el, (8, 512))
            sls = [slice(HG_D * h, HG_D * (h + 1)) for h in range(HG_HEADS)]
            v = [ub_ref[rows, 1024 + HG_D * h:1024 + HG_D * (h + 1)] for h in range(HG_HEADS)]
            a = [_nt(qe[:, sl], ke[:, sl]) for sl in sls]
            u = [_tn(v[h], kd[:, sls[h]]) for h in range(HG_HEADS)]
            a = [jnp.where(tri, x, 0.0).astype(BF) for x in a]
            oi = [_nn(a[h], v[h]) for h in range(HG_HEADS)]
            for h in range(HG_HEADS):
                ust[c, h] = u[h]
                o_ref[rows, sls[h]] = oi[h]
            return carry

        lax.fori_loop(0, cpt, intra, 0, unroll=2)

        for h in range(HG_HEADS):
            sl = slice(HG_D * h, HG_D * (h + 1))
            s = st[h]
            for c in range(cpt):
                ss_ref[c, h] = s
                s = els[c, 0:1, sl] * s + ust[c, h]
            st[h] = s

        def inter(c, carry):
            rows = pl.ds(pl.multiple_of(c * CHUNK, CHUNK), CHUNK)
            for h in range(HG_HEADS):
                sl = slice(HG_D * h, HG_D * (h + 1))
                o_ref[rows, sl] += _nt(qes[rows, sl], ss_ref[c, h].astype(BF))
            return carry

        lax.fori_loop(0, cpt, inter, 0, unroll=2)

        gate = ub_ref[:, 1536:2048].astype(F32)
        for h in range(HG_HEADS):
            sl = slice(HG_D * h, HG_D * (h + 1))
            o = o_ref[:, sl]
            r = lax.rsqrt(jnp.mean(o * o, axis=-1, keepdims=True) + EPS)
            yb_ref[:, sl] = (o * r * gn_ref[:, sl] * _silu(gate[:, sl])).astype(BF)

    in_specs = [pl.BlockSpec((TR, N_B), lambda i: (i, 0)), pl.BlockSpec((1, 512), lambda i: (0, 0)),
                pl.BlockSpec((1, 512), lambda i: (0, 0))]
    out_specs = [pl.BlockSpec((TR, 512), lambda i: (i, 0)), pl.BlockSpec((TR, 512), lambda i: (i, 0)),
                 pl.BlockSpec((cpt, HG_HEADS, HG_D, HG_D), lambda i: (i, 0, 0, 0))]
    out_shape = [jax.ShapeDtypeStruct((Lp, 512), BF), jax.ShapeDtypeStruct((Lp, 512), F32),
                 jax.ShapeDtypeStruct((Lp // CHUNK, HG_HEADS, HG_D, HG_D), F32)]
    scratch = [pltpu.VMEM((HG_HEADS, HG_D, HG_D), F32), pltpu.VMEM((TR, 512), F32),
               pltpu.VMEM((TR, 512), F32), pltpu.VMEM((TR, 512), F32), pltpu.VMEM((TR, 512), BF),
               pltpu.VMEM((cpt, 8, 512), F32), pltpu.VMEM((cpt, HG_HEADS, HG_D, HG_D), F32), pltpu.VMEM((TR, TR), BF)]
    return _call_carrying(body, "hgrn_fwd", nt, in_specs, out_specs, out_shape, scratch, (ub, lb, gn4), carry)


def _hg_bwd(ub, lb, gn4, o_save, s_save, dyb, carry=None):
    Lp = ub.shape[0]
    nt = Lp // TR
    cpt = TR // CHUNK

    def body(ub_ref, lb_ref, gn_ref, o_ref, ss_ref, dy_ref, du_ref, ds_ref,
             dst, bsc, qsc, ksc, dosc, dqsc, dksc, dbsc, els, ust, dss, tlo, tup):
        i = pl.program_id(0)
        t = nt - 1 - i

        @pl.when(i == 0)
        def _():
            dst[...] = jnp.zeros_like(dst)
            ds_ref[...] = jnp.zeros_like(ds_ref)
            tlo[...] = _chunk_tri(TR, False)
            tup[...] = _chunk_tri(TR, True)

        lbv = lb_ref[...]
        row = t * TR + lax.broadcasted_iota(jnp.int32, (TR, 1), 0)
        valid = row >= META_PAD
        q, k, g, sig, f = _hg_gates(ub_ref, lbv, row)
        silu_q, dsilu_q = _silu_pair(q)
        qsc[...] = silu_q
        ksc[...] = k
        bsc[...] = _mm_split(tlo[...], g)

        gate = ub_ref[:, 1536:2048].astype(F32)
        dy = dy_ref[...].astype(F32)
        for h in range(HG_HEADS):
            sl = slice(HG_D * h, HG_D * (h + 1))
            o = o_ref[:, sl]
            r = lax.rsqrt(jnp.mean(o * o, axis=-1, keepdims=True) + EPS)
            ohat = o * r
            silu_g, dsilu_g = _silu_pair(gate[:, sl])
            don = dy[:, sl] * silu_g
            du_ref[:, 1536 + HG_D * h:1536 + HG_D * (h + 1)] = (dy[:, sl] * ohat * gn_ref[:, sl] * dsilu_g).astype(BF)
            ds_ref[1:2, sl] += jnp.sum(don * ohat, axis=0, keepdims=True)
            gd = don * gn_ref[:, sl]
            dosc[:, sl] = r * (gd - ohat * jnp.mean(gd * ohat, axis=-1, keepdims=True))

        tri = lax.broadcasted_iota(jnp.int32, (CHUNK, CHUNK), 1) <= lax.broadcasted_iota(jnp.int32, (CHUNK, CHUNK), 0)
        last = lax.broadcasted_iota(jnp.int32, (CHUNK, 1), 0) == CHUNK - 1

        def incr(c, carry):
            rows = pl.ds(pl.multiple_of(c * CHUNK, CHUNK), CHUNK)
            b_c = bsc[rows, :]
            qE_b = (qsc[rows, :] * jnp.exp(b_c)).astype(BF)
            els[c] = jnp.broadcast_to(jnp.exp(b_c[CHUNK - 1:CHUNK, :]), (8, 512))
            do_c = dosc[rows, :].astype(BF)
            for h in range(HG_HEADS):
                sl = slice(HG_D * h, HG_D * (h + 1))
                ust[c, h] = _tn(do_c[:, sl], qE_b[:, sl])
            return carry

        lax.fori_loop(0, cpt, incr, 0, unroll=2)

        for h in range(HG_HEADS):
            sl = slice(HG_D * h, HG_D * (h + 1))
            d_s = dst[h]
            for c in reversed(range(cpt)):
                dss[c, h] = d_s
                d_s = els[c, 0:1, sl] * d_s + ust[c, h]
            dst[h] = d_s

        def chunk(c, carry):
            r0 = pl.multiple_of(c * CHUNK, CHUNK)
            rows = pl.ds(r0, CHUNK)
            e1, e2, e0, e3, el, qe, ke, qE, kd = _hg_chunk_terms(bsc[rows, :], qsc[rows, :], ksc[rows, :])
            qe_b, ke_b, kd_b = qe.astype(BF), ke.astype(BF), kd.astype(BF)
            do_c = dosc[rows, :].astype(BF)
            hs = range(HG_HEADS)
            sls = [slice(HG_D * h, HG_D * (h + 1)) for h in hs]
            v = [ub_ref[rows, 1024 + HG_D * h:1024 + HG_D * (h + 1)] for h in hs]
            do = [do_c[:, sl] for sl in sls]
            a = [_nt(qe_b[:, sl], ke_b[:, sl]) for sl in sls]
            da = [_nt(do[h], v[h]) for h in hs]
            dqE = [_nn(do[h], ss_ref[c, h].astype(BF)) for h in hs]
            dkd = [_nn(v[h], dss[c, h].astype(BF)) for h in hs]
            dv2 = [_nt(kd_b[:, sls[h]], dss[c, h].astype(BF)) for h in hs]
            a = [jnp.where(tri, x, 0.0).astype(BF) for x in a]
            da = [jnp.where(tri, x, 0.0).astype(BF) for x in da]
            dv = [_tn(a[h], do[h]) + dv2[h] for h in hs]
            dqe = [_nn(da[h], ke_b[:, sls[h]]) for h in hs]
            dke = [_tn(da[h], qe_b[:, sls[h]]) for h in hs]
            for h in hs:
                sl = sls[h]
                del_h = jnp.sum(ss_ref[c, h] * dss[c, h], axis=0, keepdims=True)
                dqsc[rows, sl] = dqE[h] * e0[:, sl] + dqe[h] * e1[:, sl]
                dksc[rows, sl] = dke[h] * e2[:, sl] + dkd[h] * e3[:, sl]
                tkd = dkd[h] * kd[:, sl]
                dbl = jnp.sum(tkd, axis=0, keepdims=True) + del_h * el[:, sl]
                dbsc[rows, sl] = (dqE[h] * qE[:, sl] + dqe[h] * qe[:, sl] - dke[h] * ke[:, sl] - tkd
                                  + jnp.where(last, dbl, 0.0))
                du_ref[rows, 1024 + HG_D * h:1024 + HG_D * (h + 1)] = dv[h].astype(BF)
            return carry

        lax.fori_loop(0, cpt, chunk, 0, unroll=2)

        dg = _mm_split(tup[...], dbsc[...])
        df = jnp.where(valid & (f > F_FLOOR), dg / f, 0.0)
        dk = jnp.where(valid, dksc[...], 0.0)
        dsig = (df - dk) * (1.0 - lbv)
        ds_ref[0:1, :] += jnp.sum((df - dk) * (1.0 - sig), axis=0, keepdims=True)
        du_ref[:, 512:1024] = (dsig * sig * (1.0 - sig)).astype(BF)
        du_ref[:, 0:512] = (dqsc[...] * dsilu_q).astype(BF)

    rev = lambda i: (nt - 1 - i, 0)
    in_specs = [pl.BlockSpec((TR, N_B), rev), pl.BlockSpec((1, 512), lambda i: (0, 0)),
                pl.BlockSpec((1, 512), lambda i: (0, 0)), pl.BlockSpec((TR, 512), rev),
                pl.BlockSpec((cpt, HG_HEADS, HG_D, HG_D), lambda i: (nt - 1 - i, 0, 0, 0)),
                pl.BlockSpec((TR, 512), rev)]
    out_specs = [pl.BlockSpec((TR, N_B), rev), pl.BlockSpec((8, 512), lambda i: (0, 0))]
    out_shape = [jax.ShapeDtypeStruct((Lp, N_B), BF), jax.ShapeDtypeStruct((8, 512), F32)]
    states = pltpu.VMEM((cpt, HG_HEADS, HG_D, HG_D), F32)
    scratch = ([pltpu.VMEM((HG_HEADS, HG_D, HG_D), F32)] + [pltpu.VMEM((TR, 512), F32)] * 7
               + [pltpu.VMEM((cpt, 8, 512), F32), states, states, pltpu.VMEM((TR, TR), BF), pltpu.VMEM((TR, TR), BF)])
    return _call_carrying(body, "hgrn_bwd", nt, in_specs, out_specs, out_shape, scratch,
                          (ub, lb, gn4, o_save, s_save, dyb), carry)


_KCOL = (2 * 512) // 128
_VCOL = _KCOL + 1


def _swa_in_specs(nt, rev):
    tile = (lambda i: nt - 1 - i) if rev else (lambda i: i)
    hpt = TR // HALO
    return [
        pl.BlockSpec((TR, 512), lambda i: (tile(i), 0)),
        pl.BlockSpec((TR, 512), lambda i: (tile(i), 1)),
        pl.BlockSpec((TR, 128), lambda i: (tile(i), _KCOL)),
        pl.BlockSpec((TR, 128), lambda i: (tile(i), _VCOL)),
        pl.BlockSpec((HALO, 128), lambda i: (jnp.maximum(tile(i) * hpt - 1, 0), _KCOL)),
        pl.BlockSpec((HALO, 128), lambda i: (jnp.maximum(tile(i) * hpt - 1, 0), _VCOL)),
        pl.BlockSpec((CHUNK, 128), lambda i: (0, _KCOL)),
        pl.BlockSpec((CHUNK, 128), lambda i: (0, _VCOL)),
        pl.BlockSpec((1, 512), lambda i: (0, 0)),
        pl.BlockSpec((1, 128), lambda i: (0, 0)),
        pl.BlockSpec((1, ATT_Q_HEADS), lambda i: (0, 0)),
    ]


_WROWS = 2 * CHUNK + HALO + TR
_W0 = 2 * CHUNK
_C0 = _W0 + HALO
_SCALE = ATT_HD ** -0.5


def _group_ones(n):
    r = lax.broadcasted_iota(jnp.int32, (n, n), 0)
    c = lax.broadcasted_iota(jnp.int32, (n, n), 1)
    return jnp.where(jnp.right_shift(r, 6) == jnp.right_shift(c, 6), 1.0, 0.0).astype(BF)


def _group_mean(x, ones):
    hi = x.astype(BF)
    lo = (x - hi.astype(F32)).astype(BF)
    return (_nn(hi, ones) + _nn(lo, ones)) * (1.0 / ATT_HD)


def _head_rms(x, ones):
    r = lax.rsqrt(_group_mean(x * x, ones) + EPS)
    return x * r, r


def _swa_windows(kc_ref, vc_ref, kh_ref, vh_ref, km_ref, vm_ref, kg2, ones, kwin, krwin, vwin, vrwin):
    meta = pl.ds(META_PAD, N_META)
    for (k, v, r0, n) in ((km_ref[meta, :], vm_ref[meta, :], 0, N_META), (kh_ref[...], vh_ref[...], _W0, HALO),
                          (kc_ref[...], vc_ref[...], _C0, TR)):
        xhat, _ = _head_rms(k.astype(F32), ones)
        kn = xhat * kg2
        kwin[pl.ds(r0, n), :] = kn.astype(BF)
        krwin[pl.ds(r0, n), :] = pltpu.roll(kn, ATT_HD, 1).astype(BF)
        vwin[pl.ds(r0, n), :] = v
        if vrwin is not None:
            vrwin[pl.ds(r0, n), :] = pltpu.roll(v.astype(F32), ATT_HD, 1).astype(BF)
    zero = jnp.zeros((_W0 - N_META, 128), BF)
    for w in (kwin, krwin, vwin, vrwin):
        if w is not None:
            w[pl.ds(N_META, _W0 - N_META), :] = zero


def _swa_masks_t(t, qb):
    q0 = t * TR + qb * QB
    qc = jnp.right_shift(q0 + lax.broadcasted_iota(jnp.int32, (1, QB), 1), 6)
    kabs = q0 - HALO + lax.broadcasted_iota(jnp.int32, (QB + HALO, 1), 0)
    kc = jnp.right_shift(kabs + HALO, 6) - HALO // CHUNK
    mask_w = (kc <= qc) & (kc >= qc - 2) & (kabs >= META_PAD)
    return qc > 2, mask_w


def _swa_park(dtype):
    return [pltpu.VMEM((ATT_Q_HEADS, N_META, QB), dtype), pltpu.VMEM((ATT_Q_HEADS, QB + HALO, QB), dtype)]


def _swa_ones():
    return [pltpu.VMEM((128, 128), BF), pltpu.VMEM((512, 512), BF)]


def _split_heads(x, lane_hi):
    return jnp.where(lane_hi, 0.0, x).astype(BF), jnp.where(lane_hi, x, 0.0).astype(BF)


def _call_carrying(body, name, nt, in_specs, out_specs, out_shape, scratch, args, carry):
    if carry is None:
        return pl.pallas_call(body, name=name, grid=(nt,), in_specs=in_specs, out_specs=out_specs, out_shape=out_shape,
                              scratch_shapes=scratch, compiler_params=_cp(("arbitrary",)))(*args)
    kind, arrs = carry
    n = len(arrs)
    return pl.pallas_call(
        _carry_exchange(body, len(in_specs), len(out_specs), nt, kind, n), name=name + "_" + kind, grid=(nt,),
        in_specs=in_specs + [_ANY] * n, out_specs=out_specs + [_ANY] * n,
        out_shape=out_shape + _exchange_out_shapes(kind, arrs), scratch_shapes=scratch + _exchange_sems(n),
        compiler_params=_cp(("arbitrary",), has_side_effects=True),
    )(*args, *arrs)


def _swa_fwd(uc, qg8, kg2, sinks, carry=None):
    Lp = uc.shape[0]
    nt = Lp // TR
    nqb = TR // QB

    def body(q_ref, g_ref, kc_ref, vc_ref, kh_ref, vh_ref, km_ref, vm_ref, qg_ref, kg_ref, sk_ref,
             yc_ref, o_ref, lse_ref, kwin, krwin, vwin, vt, qlo, qhi, ot, s_m, s_w, p_m, p_w, g128, g512):
        t = pl.program_id(0)

        @pl.when(t == 0)
        def _():
            g128[...] = _group_ones(128)
            g512[...] = _group_ones(512)

        _swa_windows(kc_ref, vc_ref, kh_ref, vh_ref, km_ref, vm_ref, kg_ref[...], g128[...], kwin, krwin, vwin, None)
        vt[...] = vwin[...].T
        xhat, _ = _head_rms(q_ref[...].astype(F32), g512[...])
        lane_hi = (lax.broadcasted_iota(jnp.int32, (1, 512), 1) & ATT_HD) != 0
        lo, hi = _split_heads(xhat * qg_ref[...] * _SCALE, lane_hi)
        qlo[...] = lo
        qhi[...] = hi
        for qb in range(nqb):
            rows = pl.ds(qb * QB, QB)
            wrows = pl.ds(_W0 + qb * QB, QB + HALO)
            mrows = pl.ds(0, N_META)
            mask_m, mask_w = _swa_masks_t(t, qb)
            for j in range(ATT_Q_HEADS):
                p, e = j // 2, j % 2
                ks = kwin if e == j // ATT_GROUP else krwin
                qp = (qlo, qhi)[e][rows, 128 * p:128 * (p + 1)]
                s_m[j] = _nt(ks[mrows, :], qp)
                s_w[j] = _nt(ks[wrows, :], qp)
            inv = []
            for j in range(ATT_Q_HEADS):
                sm = jnp.where(mask_m, s_m[j], NEG)
                sw = jnp.where(mask_w, s_w[j], NEG)
                sink = sk_ref[:, j:j + 1]
                m = jnp.maximum(jnp.maximum(jnp.max(sm, axis=0, keepdims=True),
                                            jnp.max(sw, axis=0, keepdims=True)), sink)
                em = jnp.exp(sm - m)
                ew = jnp.exp(sw - m)
                den = jnp.sum(em, axis=0, keepdims=True) + jnp.sum(ew, axis=0, keepdims=True) + jnp.exp(sink - m)
                p_m[j] = em.astype(BF)
                p_w[j] = ew.astype(BF)
                lse_ref[j:j + 1, pl.ds(qb * QB, QB)] = m + jnp.log(den)
                inv.append(1.0 / den)
            for j in range(ATT_Q_HEADS):
                vrows = pl.ds(ATT_HD * (j // ATT_GROUP), ATT_HD)
                ot[pl.ds(ATT_HD * j, ATT_HD), pl.ds(qb * QB, QB)] = (
                    _nn(vt[vrows, pl.ds(0, N_META)], p_m[j])
                    + _nn(vt[vrows, pl.ds(_W0 + qb * QB, QB + HALO)], p_w[j])) * inv[j]
        o = ot[...].T
        o_ref[...] = o
        yc_ref[...] = (o * _silu(g_ref[...].astype(F32))).astype(BF)

    win = pltpu.VMEM((_WROWS, 128), BF)
    in_specs = _swa_in_specs(nt, False)
    out_specs = [pl.BlockSpec((TR, 512), lambda i: (i, 0)), pl.BlockSpec((TR, 512), lambda i: (i, 0)),
                 pl.BlockSpec((ATT_Q_HEADS, TR), lambda i: (0, i))]
    out_shape = [jax.ShapeDtypeStruct((Lp, 512), BF), jax.ShapeDtypeStruct((Lp, 512), F32),
                 jax.ShapeDtypeStruct((ATT_Q_HEADS, Lp), F32)]
    scratch = [win, win, win, pltpu.VMEM((128, _WROWS), BF), pltpu.VMEM((TR, 512), BF),
               pltpu.VMEM((TR, 512), BF), pltpu.VMEM((512, TR), F32)] + _swa_park(F32) + _swa_park(BF) + _swa_ones()
    return _call_carrying(body, "swa_fwd", nt, in_specs, out_specs, out_shape, scratch,
                          (uc, uc, uc, uc, uc, uc, uc, uc, qg8, kg2, sinks), carry)


def _swa_bwd(uc, qg8, kg2, sinks, o_save, lse, dyc):
    Lp = uc.shape[0]
    nt = Lp // TR
    nqb = TR // QB

    def body(q_ref, g_ref, kc_ref, vc_ref, kh_ref, vh_ref, km_ref, vm_ref, qg_ref, kg_ref, sk_ref,
             o_ref, lse_ref, dy_ref, du_ref, dg_ref, dsk_ref,
             kwin, krwin, vwin, vrwin, kt, krt, qlo, qhi, dolo, dohi, dqt, dk_dir, dk_rol, dv_dir, dv_rol,
             carry_k, carry_v, meta_k, meta_v, s_m, s_w, dp_m, dp_w, p_m, p_w, ds_m, ds_w, g128, g512):
        i = pl.program_id(0)
        t = nt - 1 - i

        @pl.when(i == 0)
        def _():
            carry_k[...] = jnp.zeros_like(carry_k)
            carry_v[...] = jnp.zeros_like(carry_v)
            meta_k[...] = jnp.zeros_like(meta_k)
            meta_v[...] = jnp.zeros_like(meta_v)
            dg_ref[...] = jnp.zeros_like(dg_ref)
            dsk_ref[...] = jnp.zeros_like(dsk_ref)
            g128[...] = _group_ones(128)
            g512[...] = _group_ones(512)

        ones128 = g128[...]
        ones512 = g512[...]
        _swa_windows(kc_ref, vc_ref, kh_ref, vh_ref, km_ref, vm_ref, kg_ref[...], ones128, kwin, krwin, vwin, vrwin)
        kt[...] = kwin[...].T
        krt[...] = krwin[...].T
        xhat_q, r_q = _head_rms(q_ref[...].astype(F32), ones512)
        lane_hi = (lax.broadcasted_iota(jnp.int32, (1, 512), 1) & ATT_HD) != 0
        lo, hi = _split_heads(xhat_q * qg_ref[...] * _SCALE, lane_hi)
        qlo[...] = lo
        qhi[...] = hi
        gate = g_ref[...].astype(F32)
        dy = dy_ref[...].astype(F32)
        silu_g, dsilu_g = _silu_pair(gate)
        do = dy * silu_g
        o = o_ref[...]
        du_ref[:, 512:1024] = (dy * o * dsilu_g).astype(BF)
        lo, hi = _split_heads(do, lane_hi)
        dolo[...] = lo
        dohi[...] = hi
        hsel = jnp.where(jnp.right_shift(lax.broadcasted_iota(jnp.int32, (ATT_Q_HEADS, 512), 1), 6)
                         == lax.broadcasted_iota(jnp.int32, (ATT_Q_HEADS, 512), 0), 1.0, 0.0).astype(BF)
        prod = do * o
        p_hi = prod.astype(BF)
        d_t = _nt(hsel, p_hi) + _nt(hsel, (prod - p_hi.astype(F32)).astype(BF))
        for acc in (dk_dir, dk_rol, dv_dir, dv_rol):
            acc[...] = jnp.zeros_like(acc)

        for qb in range(nqb):
            rows = pl.ds(qb * QB, QB)
            qcols = pl.ds(qb * QB, QB)
            wrows = pl.ds(_W0 + qb * QB, QB + HALO)
            mrows = pl.ds(0, N_META)
            mask_m, mask_w = _swa_masks_t(t, qb)
            for j in range(ATT_Q_HEADS):
                p, e = j // 2, j % 2
                ks, vs = (kwin, vwin) if e == j // ATT_GROUP else (krwin, vrwin)
                pair = slice(128 * p, 128 * (p + 1))
                qp = (qlo, qhi)[e][rows, pair]
                dop = (dolo, dohi)[e][rows, pair]
                s_m[j] = _nt(ks[mrows, :], qp)
                s_w[j] = _nt(ks[wrows, :], qp)
                dp_m[j] = _nt(vs[mrows, :], dop)
                dp_w[j] = _nt(vs[wrows, :], dop)
            for j in range(ATT_Q_HEADS):
                lse_j = lse_ref[j:j + 1, qcols]
                d_j = d_t[j:j + 1, qb * QB:(qb + 1) * QB]
                em = jnp.exp(jnp.where(mask_m, s_m[j], NEG) - lse_j)
                ew = jnp.exp(jnp.where(mask_w, s_w[j], NEG) - lse_j)
                p_m[j] = em.astype(BF)
                p_w[j] = ew.astype(BF)
                ds_m[j] = (em * (dp_m[j] - d_j)).astype(BF)
                ds_w[j] = (ew * (dp_w[j] - d_j)).astype(BF)
                dsk_ref[j:j + 1, :] -= jnp.exp(sk_ref[:, j:j + 1] - lse_j) * d_j
            for j in range(ATT_Q_HEADS):
                e = j % 2
                ktr = kt if e == j // ATT_GROUP else krt
                hrows = pl.ds(ATT_HD * e, ATT_HD)
                dqt[pl.ds(ATT_HD * j, ATT_HD), qcols] = (_nn(ktr[hrows, pl.ds(0, N_META)], ds_m[j])
                                                         + _nn(ktr[hrows, pl.ds(_W0 + qb * QB, QB + HALO)], ds_w[j]))
            for direct, dk_acc, dv_acc in ((True, dk_dir, dv_dir), (False, dk_rol, dv_rol)):
                heads = [j for j in range(ATT_Q_HEADS) if (j % 2 == j // ATT_GROUP) == direct]
                q_cat = jnp.concatenate([(qlo, qhi)[j % 2][rows, 128 * (j // 2):128 * (j // 2 + 1)] for j in heads], axis=0)
                do_cat = jnp.concatenate([(dolo, dohi)[j % 2][rows, 128 * (j // 2):128 * (j // 2 + 1)] for j in heads], axis=0)
                dk_acc[mrows, :] += _nn(jnp.concatenate([ds_m[j] for j in heads], axis=1), q_cat)
                dk_acc[wrows, :] += _nn(jnp.concatenate([ds_w[j] for j in heads], axis=1), q_cat)
                dv_acc[mrows, :] += _nn(jnp.concatenate([p_m[j] for j in heads], axis=1), do_cat)
                dv_acc[wrows, :] += _nn(jnp.concatenate([p_w[j] for j in heads], axis=1), do_cat)

        dk_dir[...] += pltpu.roll(dk_rol[...], ATT_HD, 1)
        dv_dir[...] += pltpu.roll(dv_rol[...], ATT_HD, 1)
        meta_k[...] += dk_dir[pl.ds(0, N_META), :]
        meta_v[...] += dv_dir[pl.ds(0, N_META), :]
        first = jnp.where(t == 0, 1.0, 0.0)
        dk_dir[pl.ds(_C0 + TR - HALO, HALO), :] += carry_k[...]
        dv_dir[pl.ds(_C0 + TR - HALO, HALO), :] += carry_v[...]
        dk_dir[pl.ds(_C0 + META_PAD, N_META), :] += first * meta_k[...]
        dv_dir[pl.ds(_C0 + META_PAD, N_META), :] += first * meta_v[...]
        carry_k[...] = dk_dir[pl.ds(_W0, HALO), :]
        carry_v[...] = dv_dir[pl.ds(_W0, HALO), :]

        du_ref[:, 1152:1280] = dv_dir[pl.ds(_C0, TR), :].astype(BF)
        xhat_k, r_k = _head_rms(kc_ref[...].astype(F32), ones128)
        dkn = dk_dir[pl.ds(_C0, TR), :]
        dg_ref[1:2, 0:128] += jnp.sum(dkn * xhat_k, axis=0, keepdims=True)
        gd = dkn * kg_ref[...]
        du_ref[:, 1024:1152] = (r_k * (gd - xhat_k * _group_mean(gd * xhat_k, ones128))).astype(BF)
        dqn = dqt[...].T * _SCALE
        dg_ref[0:1, :] += jnp.sum(dqn * xhat_q, axis=0, keepdims=True)
        gd = dqn * qg_ref[...]
        du_ref[:, 0:512] = (r_q * (gd - xhat_q * _group_mean(gd * xhat_q, ones512))).astype(BF)

    rev = lambda i: (nt - 1 - i, 0)
    specs = _swa_in_specs(nt, True)
    win = pltpu.VMEM((_WROWS, 128), BF)
    wint = pltpu.VMEM((128, _WROWS), BF)
    tile_bf = pltpu.VMEM((TR, 512), BF)
    acc = pltpu.VMEM((_WROWS, 128), F32)
    return pl.pallas_call(
        body, name="swa_bwd", grid=(nt,),
        in_specs=specs + [pl.BlockSpec((TR, 512), rev), pl.BlockSpec((ATT_Q_HEADS, TR), lambda i: (0, nt - 1 - i)),
                          pl.BlockSpec((TR, 512), rev)],
        out_specs=[pl.BlockSpec((TR, N_C), rev), pl.BlockSpec((8, 512), lambda i: (0, 0)),
                   pl.BlockSpec((8, 128), lambda i: (0, 0))],
        out_shape=[jax.ShapeDtypeStruct((Lp, N_C), BF), jax.ShapeDtypeStruct((8, 512), F32),
                   jax.ShapeDtypeStruct((8, 128), F32)],
        scratch_shapes=[win, win, win, win, wint, wint, tile_bf, tile_bf, tile_bf, tile_bf,
                        pltpu.VMEM((512, TR), F32), acc, acc, acc, acc,
                        pltpu.VMEM((HALO, 128), F32), pltpu.VMEM((HALO, 128), F32),
                        pltpu.VMEM((N_META, 128), F32), pltpu.VMEM((N_META, 128), F32)]
        + _swa_park(F32) + _swa_park(F32) + _swa_park(BF) + _swa_park(BF) + _swa_ones(),
        compiler_params=_cp(("arbitrary",)),
    )(uc, uc, uc, uc, uc, uc, uc, uc, qg8, kg2, sinks, o_save, lse, dyc)


def _mix_fwd(h, ya, yb, yc, ug, wa, wb, wc, wo, g_next):
    Lp = h.shape[0]
    wspec = lambda r: pl.BlockSpec((r, D_MODEL), lambda i: (0, 0))
    yspec = pl.BlockSpec((TRM, 512), lambda i: (i, 0))
    hspec = pl.BlockSpec((TRM, D_MODEL), lambda i: (i, 0))

    def body(h_ref, ya_ref, yb_ref, yc_ref, ug_ref, wa_ref, wb_ref, wc_ref, wo_ref, gn_ref,
             hn_ref, za_ref, zb_ref, zc_ref, mx_ref, nx_ref):
        mixed = jnp.zeros((TRM, D_MODEL), F32)
        for n, (y_ref, w_ref, z_ref) in enumerate(((ya_ref, wa_ref, za_ref), (yb_ref, wb_ref, zb_ref),
                                                   (yc_ref, wc_ref, zc_ref))):
            z = _nn(y_ref[...], w_ref[...])
            z_ref[...] = z.astype(BF)
            mixed = mixed + _sig(ug_ref[:, D_MODEL * n:D_MODEL * (n + 1)].astype(F32)) * z
        mixed = mixed.astype(BF)
        mx_ref[...] = mixed
        x = h_ref[...] + _nn(mixed, wo_ref[...])
        hn_ref[...] = x
        nx_ref[...] = (x * lax.rsqrt(jnp.mean(x * x, axis=-1, keepdims=True) + EPS) * gn_ref[...]).astype(BF)

    return pl.pallas_call(
        body, name="mix_fwd", grid=(Lp // TRM,),
        in_specs=[hspec, yspec, yspec, yspec, pl.BlockSpec((TRM, N_G), lambda i: (i, 0)),
                  wspec(512), wspec(512), wspec(512), wspec(D_MODEL), wspec(1)],
        out_specs=[hspec, hspec, hspec, hspec, hspec, hspec],
        out_shape=[jax.ShapeDtypeStruct((Lp, D_MODEL), F32)] + [jax.ShapeDtypeStruct((Lp, D_MODEL), BF)] * 5,
        compiler_params=_cp(("parallel",)),
    )(h, ya, yb, yc, ug, wa, wb, wc, wo, g_next)


def _mix_bwd(dh, za, zb, zc, ug, wa, wb, wc, wo):
    Lp = dh.shape[0]
    wspec = lambda r: pl.BlockSpec((r, D_MODEL), lambda i: (0, 0))
    yspec = pl.BlockSpec((TRM, 512), lambda i: (i, 0))
    hspec = pl.BlockSpec((TRM, D_MODEL), lambda i: (i, 0))
    gspec = pl.BlockSpec((TRM, N_G), lambda i: (i, 0))

    def body(dh_ref, za_ref, zb_ref, zc_ref, ug_ref, wa_ref, wb_ref, wc_ref, wo_ref,
             dug_ref, dza_ref, dzb_ref, dzc_ref, dya_ref, dyb_ref, dyc_ref):
        dmix = _nt(dh_ref[...].astype(BF), wo_ref[...])
        for n, (z_ref, w_ref, dz_ref, dy_ref) in enumerate(((za_ref, wa_ref, dza_ref, dya_ref),
                                                            (zb_ref, wb_ref, dzb_ref, dyb_ref),
                                                            (zc_ref, wc_ref, dzc_ref, dyc_ref))):
            sl = slice(D_MODEL * n, D_MODEL * (n + 1))
            gt = _sig(ug_ref[:, sl].astype(F32))
            dz = dmix * gt
            dug_ref[:, sl] = (dz * z_ref[...].astype(F32) * (1.0 - gt)).astype(BF)
            dz = dz.astype(BF)
            dz_ref[...] = dz
            dy_ref[...] = _nt(dz, w_ref[...]).astype(BF)

    bf = lambda n: jax.ShapeDtypeStruct((Lp, n), BF)
    return pl.pallas_call(
        body, name="mix_bwd", grid=(Lp // TRM,),
        in_specs=[hspec, hspec, hspec, hspec, gspec, wspec(512), wspec(512), wspec(512), wspec(D_MODEL)],
        out_specs=[gspec, hspec, hspec, hspec, yspec, yspec, yspec],
        out_shape=[bf(N_G), bf(D_MODEL), bf(D_MODEL), bf(D_MODEL), bf(512), bf(512), bf(512)],
        compiler_params=_cp(("parallel",)),
    )(dh, za, zb, zc, ug, wa, wb, wc, wo)


def _inproj_bwd(dus, ws, h, dh, g, carry=None):
    Lp = h.shape[0]
    widths = [w.shape[0] for w in ws]

    def body(dg_ref, da_ref, db_ref, dc_ref, wg_ref, wa_ref, wb_ref, wc_ref, h_ref, dh_ref, g_ref, o_ref, gg_ref):
        @pl.when(pl.program_id(0) == 0)
        def _():
            gg_ref[...] = jnp.zeros_like(gg_ref)

        dhn = (_nn(dg_ref[...], wg_ref[...]) + _nn(da_ref[...], wa_ref[...])
               + _nn(db_ref[...], wb_ref[...]) + _nn(dc_ref[...], wc_ref[...]))
        x = h_ref[...]
        r = lax.rsqrt(jnp.mean(x * x, axis=-1, keepdims=True) + EPS)
        xhat = x * r
        gg_ref[0:1, :] += jnp.sum(dhn * xhat, axis=0, keepdims=True)
        gd = dhn * g_ref[...]
        o_ref[...] = dh_ref[...] + r * (gd - xhat * jnp.mean(gd * xhat, axis=-1, keepdims=True))

    hspec = pl.BlockSpec((TRM, D_MODEL), lambda i: (i, 0))
    in_specs = ([pl.BlockSpec((TRM, n), lambda i: (i, 0)) for n in widths]
                + [pl.BlockSpec((n, D_MODEL), lambda i: (0, 0), pipeline_mode=pl.Buffered(1)) for n in widths]
                + [hspec, hspec, pl.BlockSpec((1, D_MODEL), lambda i: (0, 0))])
    out_specs = [hspec, pl.BlockSpec((8, D_MODEL), lambda i: (0, 0))]
    out_shape = [jax.ShapeDtypeStruct((Lp, D_MODEL), F32), jax.ShapeDtypeStruct((8, D_MODEL), F32)]
    return _call_carrying(body, "inproj_bwd", Lp // TRM, in_specs, out_specs, out_shape, [],
                          (*dus, *ws, h, dh, g), carry)


def _loss_head(h, tgt_pad, seq):
    Lp = h.shape[0]
    nt = Lp // TR

    def body(h_ref, t_ref, dh_ref, l_ref):
        i = pl.program_id(0)

        @pl.when(i == 0)
        def _():
            l_ref[...] = jnp.zeros_like(l_ref)

        row = i * TR + lax.broadcasted_iota(jnp.int32, (TR, 1), 0)
        e = jnp.where((row >= CHUNK) & (row < CHUNK + seq), h_ref[...] - t_ref[...], 0.0)
        dh_ref[...] = e * (1.0 / D_MODEL)
        l_ref[...] += (0.5 / D_MODEL) * jnp.sum(jnp.sum(e * e, axis=0, keepdims=True), axis=1, keepdims=True)

    hspec = pl.BlockSpec((TR, D_MODEL), lambda i: (i, 0))
    return pl.pallas_call(
        body, name="loss_head", grid=(nt,), in_specs=[hspec, hspec],
        out_specs=[hspec, pl.BlockSpec((8, 128), lambda i: (0, 0))],
        out_shape=[jax.ShapeDtypeStruct((Lp, D_MODEL), F32), jax.ShapeDtypeStruct((8, 128), F32)],
        compiler_params=_cp(("arbitrary",)),
    )(h, tgt_pad)


def _lb_softmax(lb_ref):
    x = lb_ref[...]
    e = jnp.exp(x - jnp.max(x, axis=0, keepdims=True))
    return e / jnp.sum(e, axis=0, keepdims=True)


def _lb_fwd(hg_lb):
    def body(lb_ref, o_ref):
        sm = _lb_softmax(lb_ref)
        acc = jnp.zeros((1, 512), F32)
        for l in range(DEPTH):
            if l > 0:
                acc = acc + sm[l:l + 1, :]
            o_ref[l:l + 1, :] = jnp.clip(acc, 0.0, 1.0)

    return pl.pallas_call(body, name="lb_fwd", out_shape=jax.ShapeDtypeStruct((DEPTH, 512), F32))(hg_lb)


def _lb_bwd(hg_lb, dlb_all):
    def body(lb_ref, d_ref, o_ref):
        sm = _lb_softmax(lb_ref)
        acc = jnp.zeros((1, 512), F32)
        gm = []
        for l in range(DEPTH):
            if l > 0:
                acc = acc + sm[l:l + 1, :]
            gm.append(jnp.where((acc >= 0.0) & (acc <= 1.0), d_ref[l:l + 1, :], 0.0))
        dsm = [jnp.zeros((1, 512), F32)]
        for j in range(1, DEPTH):
            s = gm[j]
            for l in range(j + 1, DEPTH):
                s = s + gm[l]
            dsm.append(s)
        dot = dsm[0] * sm[0:1, :]
        for j in range(1, DEPTH):
            dot = dot + dsm[j] * sm[j:j + 1, :]
        for j in range(DEPTH):
            o_ref[j:j + 1, :] = sm[j:j + 1, :] * (dsm[j] - dot)

    return pl.pallas_call(body, name="lb_bwd", out_shape=jax.ShapeDtypeStruct((DEPTH, 512), F32))(hg_lb, dlb_all)


_ANY = pl.BlockSpec(memory_space=pl.ANY)


def _chip_peers():
    x, y, c = lax.axis_index("x"), lax.axis_index("y"), lax.axis_index("c")
    return (x, y, c), [(1 - x, y, c), (x, 1 - y, c), (1 - x, 1 - y, c)]


def _exchange(kind, ins, outs, send, recv, loc):
    (x, y, c), peers = _chip_peers()
    me = 2 * x + y
    ds = []
    for a in range(len(ins)):
        if kind == "gather":
            ds.append(pltpu.make_async_copy(ins[a], outs[a].at[me], loc.at[a]))
        else:
            ds.append(pltpu.make_async_copy(ins[a].at[me], outs[a].at[0], loc.at[a]))
        for p, (px, py, pc) in enumerate(peers):
            src, dst = (ins[a], outs[a].at[me]) if kind == "gather" else (ins[a].at[2 * px + py], outs[a].at[1 + p])
            ds.append(pltpu.make_async_remote_copy(src_ref=src, dst_ref=dst, send_sem=send.at[a, p],
                                                   recv_sem=recv.at[a, p], device_id=(px, py, pc), device_id_type=MESH))
    return ds


def _exchange_out_shapes(kind, arrs):
    if kind == "gather":
        return [jax.ShapeDtypeStruct((4,) + a.shape, a.dtype) for a in arrs]
    return [jax.ShapeDtypeStruct(a.shape, a.dtype) for a in arrs]


def _exchange_sems(n):
    return [pltpu.SemaphoreType.DMA((n, 3)), pltpu.SemaphoreType.DMA((n, 3)), pltpu.SemaphoreType.DMA((n,))]


def _gather_first(arrs, split):
    n = len(arrs)

    def body(*refs):
        ins, outs = refs[:n], refs[n:2 * n]
        send, recv, loc, fsend, frecv = refs[2 * n:]
        (x, y, c), peers = _chip_peers()
        me = 2 * x + y

        def half(a):
            hr = arrs[a].shape[0] // 2
            return pl.ds(pl.multiple_of(c * hr, 16), hr)

        local = [pltpu.make_async_copy(ins[a], outs[a].at[me], loc.at[a]) for a in range(n)]
        far, fwd = {}, {}
        for a in range(n):
            for p, (px, py, pc) in enumerate(peers):
                src, dst = (ins[a].at[half(a)], outs[a].at[me, half(a)]) if split[a] else (ins[a], outs[a].at[me])
                far[a, p] = pltpu.make_async_remote_copy(src_ref=src, dst_ref=dst, send_sem=send.at[a, p],
                                                         recv_sem=recv.at[a, p], device_id=(px, py, pc), device_id_type=MESH)
                if split[a]:
                    landed = outs[a].at[2 * px + py, half(a)]
                    fwd[a, p] = pltpu.make_async_remote_copy(src_ref=landed, dst_ref=landed, send_sem=fsend.at[a, p],
                                                             recv_sem=frecv.at[a, p], device_id=(x, y, 1 - c),
                                                             device_id_type=MESH)
        for d in local + list(far.values()):
            d.start()
        for key, d in far.items():
            d.wait_recv()
            if key in fwd:
                fwd[key].start()
        for d in fwd.values():
            d.wait()
        for d in far.values():
            d.wait_send()
        for d in local:
            d.wait()

    sems = pltpu.SemaphoreType.DMA((n, 3))
    return pl.pallas_call(
        body, name="gather_first", in_specs=[_ANY] * n, out_specs=[_ANY] * n,
        out_shape=_exchange_out_shapes("gather", arrs),
        scratch_shapes=[sems, sems, pltpu.SemaphoreType.DMA((n,)), sems, sems],
        compiler_params=pltpu.CompilerParams(has_side_effects=True),
    )(*arrs)


def _carry_exchange(body, n_in, n_out, n_steps, kind, n):
    def wrapped(*refs):
        ins, cin = refs[:n_in], refs[n_in:n_in + n]
        outs, cout = refs[n_in + n:n_in + n + n_out], refs[n_in + n + n_out:n_in + 2 * n + n_out]
        scr, sems = refs[n_in + 2 * n + n_out:-3], refs[-3:]
        i = pl.program_id(0)

        @pl.when(i == 0)
        def _():
            for d in _exchange(kind, cin, cout, *sems):
                d.start()

        body(*ins, *outs, *scr)

        @pl.when(i == n_steps - 1)
        def _():
            for d in _exchange(kind, cin, cout, *sems):
                d.wait()

    return wrapped


def _swap_cores(arrs):
    n = len(arrs)

    def body(*refs):
        ins, outs = refs[:n], refs[n:2 * n]
        send, recv = refs[2 * n:]
        x, y, c = lax.axis_index("x"), lax.axis_index("y"), lax.axis_index("c")
        rdmas = []
        for a in range(n):
            r = pltpu.make_async_remote_copy(src_ref=ins[a], dst_ref=outs[a], send_sem=send.at[a], recv_sem=recv.at[a],
                                             device_id=(x, y, 1 - c), device_id_type=MESH)
            r.start()
            rdmas.append(r)
        for r in rdmas:
            r.wait()

    return pl.pallas_call(
        body, name="swap_cores", in_specs=[_ANY] * n, out_specs=[_ANY] * n,
        out_shape=[jax.ShapeDtypeStruct(a.shape, a.dtype) for a in arrs],
        scratch_shapes=[pltpu.SemaphoreType.DMA((n,)), pltpu.SemaphoreType.DMA((n,))],
        compiler_params=pltpu.CompilerParams(has_side_effects=True),
    )(*arrs)


def _allsum_small(p):
    R = p.shape[0]

    def body(p_ref, o_ref, buf, send, recv):
        x, y, c = lax.axis_index("x"), lax.axis_index("y"), lax.axis_index("c")
        me = 4 * x + 2 * y + c
        buf[me] = p_ref[...]
        rdmas = []
        for k in range(1, 8):
            peer = (x ^ (k >> 2), y ^ ((k >> 1) & 1), c ^ (k & 1))
            r = pltpu.make_async_remote_copy(src_ref=p_ref, dst_ref=buf.at[me], send_sem=send.at[k - 1],
                                             recv_sem=recv.at[k - 1], device_id=peer, device_id_type=MESH)
            r.start()
            rdmas.append(r)
        for r in rdmas:
            r.wait()
        acc = buf[0]
        for d in range(1, 8):
            acc = acc + buf[d]
        o_ref[...] = acc

    return pl.pallas_call(
        body, name="allsum_small", out_shape=jax.ShapeDtypeStruct((R, 512), F32),
        in_specs=[pl.BlockSpec(memory_space=pltpu.VMEM)], out_specs=pl.BlockSpec(memory_space=pltpu.VMEM),
        scratch_shapes=[pltpu.VMEM((8, R, 512), F32), pltpu.SemaphoreType.DMA((7,)), pltpu.SemaphoreType.DMA((7,))],
        compiler_params=_cp(has_side_effects=True),
    )(p)


def _row_block(rows):
    return max((d for d in range(16, 513, 16) if rows % d == 0), default=rows)


def _sum4(parts, name):
    _, R, C = parts.shape
    tr = _row_block(R)

    def body(p_ref, o_ref):
        p = [p_ref[k].astype(F32) for k in range(4)]
        o_ref[...] = (((p[0] + p[1]) + p[2]) + p[3]).astype(BF)

    return pl.pallas_call(
        body, name=name, grid=(R // tr,), in_specs=[pl.BlockSpec((4, tr, C), lambda i: (0, i, 0))],
        out_specs=pl.BlockSpec((tr, C), lambda i: (i, 0)), out_shape=jax.ShapeDtypeStruct((R, C), BF),
        compiler_params=_cp(("parallel",)),
    )(parts)


def _adamw(w, m, v, g0, g1, name):
    L, R, C = w.shape
    tr = _row_block(R)
    two = g1 is not None
    c1 = 1.0 / (1.0 - ADAM_B1 ** ADAM_STEP)
    c2 = 1.0 / (1.0 - ADAM_B2 ** ADAM_STEP)

    def body(*refs):
        if two:
            w_ref, m_ref, v_ref, a_ref, b_ref, g_ref, d_ref, nm_ref, nv_ref = refs
            g = a_ref[...].astype(F32) + b_ref[...].astype(F32)
        else:
            w_ref, m_ref, v_ref, a_ref, g_ref, d_ref, nm_ref, nv_ref = refs
            g = a_ref[...]
        g_ref[...] = g
        m = ADAM_B1 * m_ref[...] + (1.0 - ADAM_B1) * g
        v = ADAM_B2 * v_ref[...] + (1.0 - ADAM_B2) * (g * g)
        nm_ref[...] = m
        nv_ref[...] = v
        d_ref[...] = -ADAM_LR * ((m * c1) / (jnp.sqrt(v * c2) + ADAM_EPS) + ADAM_WD * w_ref[...])

    spec = pl.BlockSpec((1, tr, C), lambda l, i: (l, i, 0))
    n_in = 5 if two else 4
    ins = (w, m, v, g0, g1) if two else (w, m, v, g0)
    return pl.pallas_call(
        body, name=name, grid=(L, R // tr), in_specs=[spec] * n_in, out_specs=[spec] * 4,
        out_shape=[jax.ShapeDtypeStruct((L, R, C), F32)] * 4, compiler_params=_cp(("parallel", "parallel")),
    )(*ins)


def _pad8(a):
    r = (-a.shape[0]) % 8
    return a if r == 0 else jnp.pad(a, ((0, r), (0, 0)))


def _local_step(x, tgt, meta, P, shards=None, prep=None, pack=None, mats=None):
    seq = x.shape[0]
    Lp = -(-(seq + CHUNK) // TR) * TR
    tail = Lp - seq - CHUNK
    h = jnp.concatenate([jnp.zeros((META_PAD, D_MODEL), F32), meta, x, jnp.zeros((tail, D_MODEL), F32)], axis=0)
    tgt_pad = jnp.pad(tgt, ((CHUNK, tail), (0, 0)))

    P = list(P)
    saved = []
    hn = _rms_fwd(h, P[0]["norm_g"])
    for l in range(DEPTH):
        p = P[l]
        mm = functools.partial(_matmul, tb=True, out_dtype=BF, tm=TR, tk=D_MODEL, col_major_grid=True)
        ug = mm(hn, p["w_g"], tn=N_G // 2, name="inproj_g")
        ua = mm(hn, p["w_a"], tn=N_A, name="inproj_a")
        ub = mm(hn, p["w_b"], tn=N_B, name="inproj_b")
        uc = mm(hn, p["w_c"], tn=N_C, name="inproj_c")
        nxt = shards[l + 1] if shards is not None and l + 1 < DEPTH else None
        carry = (lambda part: ("gather", part)) if nxt is not None else (lambda part: None)
        res_a = _conv_fwd(ua, p["conv_w"], p["conv_vec"], carry(nxt and nxt[1:2]))
        res_b = _hg_fwd(ub, p["lb"], p["gn4"], carry(nxt and nxt[0:1]))
        own = shards[0][2:] if nxt is not None and l == 0 else []
        res_c = _swa_fwd(uc, p["qg"], p["kg"], p["sinks"], carry(nxt and nxt[2:] + own))
        (ya, yconv), (yb, o_hg, s_hg), (yc, o_at, lse) = res_a[:2], res_b[:3], res_c[:3]
        if nxt is not None:
            P.append(prep(l + 1, [*res_b[3:], *res_a[2:], *res_c[3:3 + len(nxt) - 2]]))
            if own:
                p.update(mats(res_c[3 + len(nxt) - 2:]))
        g_next = P[min(l + 1, DEPTH - 1)]["norm_g"]
        h_new, za, zb, zc, mixed, hn_next = _mix_fwd(h, ya, yb, yc, ug, p["w_ao"], p["w_bo"], p["w_co"], p["w_out"], g_next)
        saved.append(dict(h=h, hn=hn, ug=ug, ua=ua, ub=ub, uc=uc, ya=ya, yconv=yconv, yb=yb, o_hg=o_hg, s_hg=s_hg,
                          yc=yc, o_at=o_at, lse=lse, za=za, zb=zb, zc=zc, mixed=mixed))
        h, hn = h_new, hn_next

    dh, loss8 = _loss_head(h, tgt_pad, seq)

    grads = [None] * DEPTH
    parts = [[None, None] for _ in range(DEPTH)]
    pending = None
    tk_dw = 2 * TR if Lp % (2 * TR) == 0 else TR
    for l in reversed(range(DEPTH)):
        p, s = P[l], saved[l]
        dug, dza, dzb, dzc, dya, dyb, dyc = _mix_bwd(dh, s["za"], s["zb"], s["zc"], s["ug"],
                                                      p["w_ao"], p["w_bo"], p["w_co"], p["w_out"])
        tnmm = functools.partial(_matmul, ta=True, out_dtype=F32, tk=tk_dw)
        g = {}
        g["w_out"] = tnmm(s["mixed"], dh, tm=D_MODEL, tn=D_MODEL, name="dw_out")
        g["w_ao"] = tnmm(s["ya"], dza, tm=512, tn=D_MODEL, name="dw_ao")
        g["w_bo"] = tnmm(s["yb"], dzb, tm=512, tn=D_MODEL, name="dw_bo")
        g["w_co"] = tnmm(s["yc"], dzc, tm=512, tn=D_MODEL, name="dw_co")
        dua, g["conv_w"], g["conv_vec"] = _conv_bwd(s["ua"], s["yconv"], dya, p["conv_w"], p["conv_vec"])
        carry = ("scatter", pending[1]) if pending is not None else None
        res = _hg_bwd(s["ub"], p["lb"], p["gn4"], s["o_hg"], s["s_hg"], dyb, carry)
        dub, g["hg_small"] = res[:2]
        if carry is not None:
            parts[pending[0]][1] = res[2:]
        duc, g["at_gain"], g["at_sink"] = _swa_bwd(s["uc"], p["qg"], p["kg"], p["sinks"], s["o_at"], s["lse"], dyc)
        g["w_g"] = tnmm(dug, s["hn"], tm=N_G // 2, tn=D_MODEL, name="dw_in_g")
        g["w_a"] = tnmm(dua, s["hn"], tm=N_A, tn=D_MODEL, name="dw_in_a")
        g["w_b"] = tnmm(dub, s["hn"], tm=N_B, tn=D_MODEL, name="dw_in_b")
        g["w_c"] = tnmm(duc, s["hn"], tm=N_C, tn=D_MODEL, name="dw_in_c")
        first, second = pack(g) if pack is not None else (None, None)
        if first is not None and l == 0:
            first, second = first + second, []
        res = _inproj_bwd([dug, dua, dub, duc], [p["w_g"], p["w_a"], p["w_b"], p["w_c"]], s["h"], dh, p["norm_g"],
                          ("scatter", first) if first is not None else None)
        dh, g["norm_g"] = res[:2]
        grads[l] = g
        if pack is not None:
            parts[l] = [res[2:3], res[3:]] if l == 0 else [res[2:], None]
            pending = (l, second) if l > 0 else None
    return loss8, dh, grads, parts


def _split_w_in(wt):
    return dict(w_a=wt[0:1536], w_b=wt[1536:3584],
                w_c=jnp.concatenate([wt[3584:4096], wt[4352:4864], wt[4096:4352]], axis=0), w_g=wt[4864:7936])


def _join_w_in(g):
    c = g["w_c"]
    return jnp.concatenate([g["w_a"], g["w_b"], c[0:512], c[1024:1280], c[512:1024], g["w_g"]], axis=0)


def _attn_small(g):
    return (g["at_gain"][0].reshape(ATT_Q_HEADS, ATT_HD).sum(0),
            g["at_gain"][1, 0:128].reshape(ATT_KV_HEADS, ATT_HD).sum(0), g["at_sink"].sum(1))


_SMALL = (("norm_g", 8), ("meta", 32), ("conv_w", 32 * DEPTH), ("conv_b", 8), ("conv_ln_g", 8), ("conv_ln_b", 8),
          ("lb", 8), ("hg_norm_g", 8), ("q_norm_g", 8), ("k_norm_g", 8), ("sinks", 8))


def _small_offsets():
    off, o = {}, 0
    for name, rows in _SMALL:
        off[name] = (o, rows)
        o += rows
    return off, o


def _pack_small(d):
    parts = []
    for name, rows in _SMALL:
        a = d[name]
        parts.append(jnp.pad(a, ((0, rows - a.shape[0]), (0, 512 - a.shape[1]))))
    return jnp.concatenate(parts, axis=0)


def kernel(x, meta_tokens, norm_g, w_in, conv_w, conv_b, conv_ln_g, conv_ln_b, w_conv_out, hg_lower_bounds, hg_norm_g, w_hg_out, q_norm_g, k_norm_g, attn_sinks, w_att_out, w_out, loss_target, m_meta_tokens, m_norm_g, m_w_in, m_conv_w, m_conv_b, m_conv_ln_g, m_conv_ln_b, m_w_conv_out, m_hg_lower_bounds, m_hg_norm_g, m_w_hg_out, m_q_norm_g, m_k_norm_g, m_attn_sinks, m_w_att_out, m_w_out, v_meta_tokens, v_norm_g, v_w_in, v_conv_w, v_conv_b, v_conv_ln_g, v_conv_ln_b, v_w_conv_out, v_hg_lower_bounds, v_hg_norm_g, v_w_hg_out, v_q_norm_g, v_k_norm_g, v_attn_sinks, v_w_att_out, v_w_out):
    xi, yi = lax.axis_index("x"), lax.axis_index("y")
    chip = 2 * xi + yi
    NS = w_in.shape[2]
    CS = conv_w.shape[2]
    MS = meta_tokens.shape[1]

    half = NS // 2
    w_in_t, m_w_in_t, v_w_in_t = (jnp.swapaxes(t, 1, 2) for t in (w_in, m_w_in, v_w_in))
    shards = [[w_in_t[l, :half].astype(BF), w_in_t[l, half:].astype(BF), w_conv_out[l].astype(BF),
               w_hg_out[l].astype(BF), w_att_out[l].astype(BF), w_out[l].astype(BF)] for l in range(DEPTH)]
    *first, g_meta, g_convw = _gather_first(shards[0][:2] + [meta_tokens, conv_w.reshape(DEPTH * CONV_WIDTH, CS)],
                                            [True, True, False, False])
    cols = lambda g: g.transpose(1, 0, 2).reshape(g.shape[1], -1)
    meta_f = cols(g_meta)
    convw_f = cols(g_convw).reshape(DEPTH, CONV_WIDTH, D_CONV)
    lb_all = _lb_fwd(hg_lower_bounds)

    def mats(gathered):
        g_wao, g_wbo, g_wco, g_wout = gathered
        return dict(w_ao=cols(g_wao), w_bo=cols(g_wbo), w_co=cols(g_wco), w_out=g_wout.reshape(D_MODEL, D_MODEL))

    def prep(l, gathered):
        p = _split_w_in(jnp.concatenate(gathered[:2], axis=1).reshape(4 * NS, D_MODEL))
        if len(gathered) > 2:
            p.update(mats(gathered[2:]))
        p.update(norm_g=norm_g[l:l + 1], conv_w=convw_f[l],
                 conv_vec=_pad8(jnp.stack([conv_b[l], conv_ln_g[l], conv_ln_b[l]])),
                 lb=lb_all[l:l + 1], gn4=jnp.tile(hg_norm_g[l:l + 1], (1, HG_HEADS)),
                 qg=jnp.tile(q_norm_g[l:l + 1], (1, ATT_Q_HEADS)), kg=jnp.tile(k_norm_g[l:l + 1], (1, ATT_KV_HEADS)),
                 sinks=attn_sinks[l:l + 1])
        return p

    shard_cols = lambda a: a.reshape(a.shape[0], 4, -1).transpose(1, 0, 2)
    def pack(g):
        win = _join_w_in(g).reshape(4, NS, D_MODEL).astype(BF)
        return [win[:, :half]], [win[:, half:], shard_cols(g["w_ao"]).astype(BF), shard_cols(g["w_bo"]).astype(BF),
                                 shard_cols(g["w_co"]).astype(BF), g["w_out"].reshape(4, MS, D_MODEL).astype(BF)]

    loss8, dh0, grads, parts = _local_step(x[0], loss_target[0], meta_f, [prep(0, first)], shards, prep, pack, mats)
    seq = x.shape[1]
    grad_x = dh0[CHUNK:CHUNK + seq][None]
    loss = lax.psum(loss8[0, 0], ("x", "y", "c"))

    sum4 = functools.partial(_sum4, name="sum_chips")
    mine = [jnp.concatenate([t for l in range(DEPTH) for t in (sum4(parts[l][0][0]), sum4(parts[l][1][0]))], axis=0)]
    mine += [jnp.concatenate([sum4(parts[l][1][a]) for l in range(DEPTH)], axis=0) for a in range(1, 5)]
    theirs = _swap_cores(mine)

    dlb_all = jnp.concatenate([grads[l]["hg_small"][0:1] for l in range(DEPTH)], axis=0)
    small = dict(
        norm_g=jnp.concatenate([grads[l]["norm_g"][0:1] for l in range(DEPTH)], axis=0).reshape(8, 512),
        meta=dh0[META_PAD:CHUNK].reshape(32, 512),
        conv_w=jnp.concatenate([grads[l]["conv_w"] for l in range(DEPTH)], axis=0),
        conv_b=jnp.concatenate([grads[l]["conv_vec"][0:1] for l in range(DEPTH)], axis=0),
        conv_ln_g=jnp.concatenate([grads[l]["conv_vec"][1:2] for l in range(DEPTH)], axis=0),
        conv_ln_b=jnp.concatenate([grads[l]["conv_vec"][2:3] for l in range(DEPTH)], axis=0),
        lb=_lb_bwd(hg_lower_bounds, dlb_all),
        hg_norm_g=jnp.concatenate([grads[l]["hg_small"][1:2].reshape(HG_HEADS, HG_D).sum(0, keepdims=True)
                                   for l in range(DEPTH)], axis=0),
        q_norm_g=jnp.stack([_attn_small(grads[l])[0] for l in range(DEPTH)]),
        k_norm_g=jnp.stack([_attn_small(grads[l])[1] for l in range(DEPTH)]),
        sinks=jnp.stack([_attn_small(grads[l])[2] for l in range(DEPTH)]),
    )
    gsum = _allsum_small(_pack_small(small))
    off, _ = _small_offsets()

    def take(name, rows, cols):
        o, _ = off[name]
        return gsum[o:o + rows, 0:cols]

    g_meta_full = take("meta", 32, 512).reshape(N_META, D_MODEL)
    g_convw_full = take("conv_w", 32 * DEPTH, 512).reshape(DEPTH, 32, 512)[:, :CONV_WIDTH]
    small_grads = dict(
        norm_g=take("norm_g", 8, 512),
        meta=lax.dynamic_slice_in_dim(g_meta_full, chip * MS, MS, axis=1),
        conv_w=lax.dynamic_slice_in_dim(g_convw_full, chip * CS, CS, axis=2).reshape(DEPTH * CONV_WIDTH, CS),
        conv_b=take("conv_b", DEPTH, 512), conv_ln_g=take("conv_ln_g", DEPTH, 512), conv_ln_b=take("conv_ln_b", DEPTH, 512),
        lb=take("lb", DEPTH, 512), hg_norm_g=take("hg_norm_g", DEPTH, HG_D), q_norm_g=take("q_norm_g", DEPTH, ATT_HD),
        k_norm_g=take("k_norm_g", DEPTH, ATT_HD), sinks=take("sinks", DEPTH, ATT_Q_HEADS))

    def big_update(w, m, v, a, b, name):
        return _adamw(w, m, v, a.reshape(w.shape), b.reshape(w.shape), name)

    res = {}
    res["w_in"] = [jnp.swapaxes(t, 1, 2) for t in big_update(w_in_t, m_w_in_t, v_w_in_t, mine[0], theirs[0], "adamw_w_in")]
    res["w_conv_out"] = big_update(w_conv_out, m_w_conv_out, v_w_conv_out, mine[1], theirs[1], "adamw_w_ao")
    res["w_hg_out"] = big_update(w_hg_out, m_w_hg_out, v_w_hg_out, mine[2], theirs[2], "adamw_w_bo")
    res["w_att_out"] = big_update(w_att_out, m_w_att_out, v_w_att_out, mine[3], theirs[3], "adamw_w_co")
    res["w_out"] = big_update(w_out, m_w_out, v_w_out, mine[4], theirs[4], "adamw_w_out")

    small_w = dict(meta=(meta_tokens, m_meta_tokens, v_meta_tokens), norm_g=(norm_g, m_norm_g, v_norm_g),
                   conv_w=(conv_w, m_conv_w, v_conv_w), conv_b=(conv_b, m_conv_b, v_conv_b),
                   conv_ln_g=(conv_ln_g, m_conv_ln_g, v_conv_ln_g), conv_ln_b=(conv_ln_b, m_conv_ln_b, v_conv_ln_b),
                   lb=(hg_lower_bounds, m_hg_lower_bounds, v_hg_lower_bounds),
                   hg_norm_g=(hg_norm_g, m_hg_norm_g, v_hg_norm_g), q_norm_g=(q_norm_g, m_q_norm_g, v_q_norm_g),
                   k_norm_g=(k_norm_g, m_k_norm_g, v_k_norm_g), sinks=(attn_sinks, m_attn_sinks, v_attn_sinks))
    view = lambda n, t: t.reshape(-1, 512) if n == "norm_g" else t.reshape(-1, t.shape[-1])
    pw, pm, pv = (_pack_rows([view(n, small_w[n][k]) for n in small_w]) for k in range(3))
    pg = _pack_rows([small_grads[n] for n in small_w])
    packed = [t[0] for t in _adamw(pw[None], pm[None], pv[None], pg[None], None, "adamw_small")]
    o = 0
    for n in small_w:
        r, cdim = view(n, small_w[n][0]).shape
        res[n] = [t[o:o + r, 0:cdim].reshape(small_w[n][0].shape) for t in packed]
        o += -(-r // 8) * 8

    order = [("meta", None), ("norm_g", None), ("w_in", None), ("conv_w", None), ("conv_b", None), ("conv_ln_g", None),
             ("conv_ln_b", None), ("w_conv_out", None), ("lb", None), ("hg_norm_g", None), ("w_hg_out", None),
             ("q_norm_g", None), ("k_norm_g", None), ("sinks", None), ("w_att_out", None), ("w_out", None)]
    outs = [loss, grad_x]
    for k in range(4):
        outs += [res[n][k] for n, _ in order]
    return tuple(outs)


def _pack_rows(arrs):
    parts = []
    for a in arrs:
        r = (-a.shape[0]) % 8
        parts.append(jnp.pad(a, ((0, r), (0, 512 - a.shape[1]))))
    return jnp.concatenate(parts, axis=0)
```

```python
import functools

import jax
import jax.numpy as jnp
from jax import lax
from jax.experimental import pallas as pl
from jax.experimental.pallas import tpu as pltpu

F32 = jnp.float32
BF = jnp.bfloat16

D_MODEL = 1024
DEPTH = 4
CHUNK = 64
N_META = 16
META_PAD = CHUNK - N_META
D_CONV = 512
CONV_WIDTH = 31
HG_HEADS = 4
HG_D = 128
ATT_Q_HEADS = 8
ATT_KV_HEADS = 2
ATT_HD = 64
ATT_GROUP = ATT_Q_HEADS // ATT_KV_HEADS
EPS = 1e-6
F_FLOOR = 1e-30
NEG = -1e30

ADAM_LR = 0.001
ADAM_B1 = 0.9
ADAM_B2 = 0.999
ADAM_EPS = 1e-08
ADAM_WD = 0.01
ADAM_STEP = 10

TR = 640
TRM = TR // 2
CONV_RB = 32
QB = 128
HALO = 128
VMEM_LIMIT = 56 * 1024 * 1024

N_G, N_A, N_B, N_C = 3 * D_MODEL, 3 * D_CONV, 4 * 512, 2 * 512 + 2 * 128

MESH = pl.DeviceIdType.MESH


def _cp(sem=None, vmem=VMEM_LIMIT, **kw):
    if sem is None:
        return pltpu.CompilerParams(vmem_limit_bytes=vmem, **kw)
    return pltpu.CompilerParams(dimension_semantics=sem, vmem_limit_bytes=vmem, **kw)


def _nn(a, b):
    return lax.dot_general(a, b, (((1,), (0,)), ((), ())), preferred_element_type=F32)


def _nt(a, b):
    return lax.dot_general(a, b, (((1,), (1,)), ((), ())), preferred_element_type=F32)


def _tn(a, b):
    return lax.dot_general(a, b, (((0,), (0,)), ((), ())), preferred_element_type=F32)


def _sig(x):
    return jax.nn.sigmoid(x)


def _silu(x):
    return x * _sig(x)


def _silu_pair(x):
    s = _sig(x)
    return x * s, s * (1.0 + x * (1.0 - s))


def _mm_split(t, x):
    hi = x.astype(BF)
    lo = (x - hi.astype(F32)).astype(BF)
    return _nn(t, hi) + _nn(t, lo)


def _chunk_tri(n, upper):
    r = lax.broadcasted_iota(jnp.int32, (n, n), 0)
    c = lax.broadcasted_iota(jnp.int32, (n, n), 1)
    same = jnp.right_shift(r, 6) == jnp.right_shift(c, 6)
    tri = (c >= r) if upper else (c <= r)
    return jnp.where(same & tri, 1.0, 0.0).astype(BF)


def _matmul(a, b, *, ta=False, tb=False, out_dtype, tm, tn, tk, name, col_major_grid=False):
    if ta:
        K, M = a.shape
    else:
        M, K = a.shape
    N = b.shape[0] if tb else b.shape[1]
    assert M % tm == 0 and N % tn == 0 and K % tk == 0, (name, M, N, K, tm, tn, tk)
    nk = K // tk
    if col_major_grid:
        grid = (N // tn, M // tm, nk)
        ij = lambda g0, g1: (g1, g0)
    else:
        grid = (M // tm, N // tn, nk)
        ij = lambda g0, g1: (g0, g1)
    if ta:
        a_spec = pl.BlockSpec((tk, tm), lambda g0, g1, k: (k, ij(g0, g1)[0]))
    else:
        a_spec = pl.BlockSpec((tm, tk), lambda g0, g1, k: (ij(g0, g1)[0], k))
    if tb:
        b_spec = pl.BlockSpec((tn, tk), lambda g0, g1, k: (ij(g0, g1)[1], k))
    else:
        b_spec = pl.BlockSpec((tk, tn), lambda g0, g1, k: (k, ij(g0, g1)[1]))
    o_spec = pl.BlockSpec((tm, tn), lambda g0, g1, k: ij(g0, g1))
    dims = (((0 if ta else 1,), (1 if tb else 0,)), ((), ()))
    use_acc = nk > 1 and out_dtype != F32

    def body(a_ref, b_ref, o_ref, *scr):
        k = pl.program_id(2)
        p = lax.dot_general(a_ref[...].astype(BF), b_ref[...].astype(BF), dims, preferred_element_type=F32)
        if nk == 1:
            o_ref[...] = p.astype(out_dtype)
        else:
            acc = scr[0] if use_acc else o_ref

            @pl.when(k == 0)
            def _():
                acc[...] = p

            @pl.when(k > 0)
            def _():
                acc[...] += p

            if use_acc:
                @pl.when(k == nk - 1)
                def _():
                    o_ref[...] = acc[...].astype(out_dtype)

    return pl.pallas_call(
        body, name=name, grid=grid, in_specs=[a_spec, b_spec], out_specs=o_spec,
        out_shape=jax.ShapeDtypeStruct((M, N), out_dtype),
        scratch_shapes=[pltpu.VMEM((tm, tn), F32)] if use_acc else [],
        compiler_params=_cp(("parallel", "parallel", "arbitrary")),
    )(a, b)


def _rms_fwd(h, g):
    Lp = h.shape[0]

    def body(h_ref, g_ref, o_ref):
        x = h_ref[...]
        r = lax.rsqrt(jnp.mean(x * x, axis=-1, keepdims=True) + EPS)
        o_ref[...] = (x * r * g_ref[...]).astype(BF)

    return pl.pallas_call(
        body, name="rms_fwd", grid=(Lp // TR,),
        in_specs=[pl.BlockSpec((TR, D_MODEL), lambda i: (i, 0)), pl.BlockSpec((1, D_MODEL), lambda i: (0, 0))],
        out_specs=pl.BlockSpec((TR, D_MODEL), lambda i: (i, 0)),
        out_shape=jax.ShapeDtypeStruct((Lp, D_MODEL), BF),
        compiler_params=_cp(("parallel",)),
    )(h, g)


def _glu(ua, row):
    a = ua[:, 0:D_CONV].astype(F32)
    gl = ua[:, D_CONV:2 * D_CONV].astype(F32)
    return jnp.where(row >= META_PAD, a * _sig(gl), 0.0)


_SH_ROWS = TR + CHUNK - 8


def _fill_shifts(src, sh):
    for b in range(1, 8):
        sh[b - 1] = src[pl.ds(b, _SH_ROWS), :]


def _shifted(src, sh, start, n):
    b = start % 8
    if b == 0:
        return src[pl.ds(start, n), :]
    return sh[b - 1, pl.ds(start - b, n), :]


def _conv_fwd(ua, cw, cvec, carry=None):
    Lp = ua.shape[0]
    nt = Lp // TR
    hb = TR // CHUNK

    def body(cur_ref, halo_ref, w_ref, v_ref, ya_ref, yc_ref, ext, sh):
        i = pl.program_id(0)
        row = i * TR + lax.broadcasted_iota(jnp.int32, (TR, 1), 0)
        hrow = i * TR - CHUNK + lax.broadcasted_iota(jnp.int32, (CHUNK, 1), 0)
        ext[pl.ds(0, CHUNK), :] = jnp.where(i > 0, _glu(halo_ref[...], hrow), 0.0)
        ext[pl.ds(CHUNK, TR), :] = _glu(cur_ref[...], row)
        _fill_shifts(ext, sh)
        for rb in range(TR // CONV_RB):
            r0 = rb * CONV_RB
            rows = pl.ds(r0, CONV_RB)
            acc = jnp.zeros((CONV_RB, D_CONV), F32)
            for j in range(CONV_WIDTH):
                acc = acc + _shifted(ext, sh, r0 + CHUNK - (CONV_WIDTH - 1) + j, CONV_RB) * w_ref[j:j + 1, :]
            y = acc + v_ref[0:1, :]
            yc_ref[rows, :] = y
            mu = jnp.mean(y, axis=-1, keepdims=True)
            d = y - mu
            var = jnp.mean(d * d, axis=-1, keepdims=True)
            yn = d * lax.rsqrt(var + EPS) * v_ref[1:2, :] + v_ref[2:3, :]
            ya_ref[rows, :] = (_silu(yn) * _silu(cur_ref[rows, 2 * D_CONV:3 * D_CONV].astype(F32))).astype(BF)

    in_specs = [pl.BlockSpec((TR, N_A), lambda i: (i, 0)),
                pl.BlockSpec((CHUNK, N_A), lambda i: (jnp.maximum(i * hb - 1, 0), 0)),
                pl.BlockSpec((CONV_WIDTH, D_CONV), lambda i: (0, 0)),
                pl.BlockSpec((8, D_CONV), lambda i: (0, 0))]
    out_specs = [pl.BlockSpec((TR, D_CONV), lambda i: (i, 0)), pl.BlockSpec((TR, D_CONV), lambda i: (i, 0))]
    out_shape = [jax.ShapeDtypeStruct((Lp, D_CONV), BF), jax.ShapeDtypeStruct((Lp, D_CONV), F32)]
    scratch = [pltpu.VMEM((TR + CHUNK, D_CONV), F32), pltpu.VMEM((7, _SH_ROWS, D_CONV), F32)]
    return _call_carrying(body, "conv_fwd", nt, in_specs, out_specs, out_shape, scratch, (ua, ua, cw, cvec), carry)


def _conv_bwd(ua, yconv, dya, cw, cvec):
    Lp = ua.shape[0]
    nt = Lp // TR
    hb = TR // CHUNK
    nhb = Lp // CHUNK

    def ln_bwd(y, dout, gate, v_ref):
        mu = jnp.mean(y, axis=-1, keepdims=True)
        d = y - mu
        var = jnp.mean(d * d, axis=-1, keepdims=True)
        rstd = lax.rsqrt(var + EPS)
        xhat = d * rstd
        yn = xhat * v_ref[1:2, :] + v_ref[2:3, :]
        s_gate, ds_gate = _silu_pair(gate)
        s_yn, ds_yn = _silu_pair(yn)
        dyn = dout * s_gate * ds_yn
        dxh = dyn * v_ref[1:2, :]
        dyc = rstd * (dxh - jnp.mean(dxh, axis=-1, keepdims=True) - xhat * jnp.mean(dxh * xhat, axis=-1, keepdims=True))
        return dyc, dyn, xhat, dout * s_yn * ds_gate

    def body(cur_ref, prev_ref, next_ref, yc_ref, ycn_ref, dy_ref, dyn_ref, w_ref, v_ref,
             du_ref, dw_ref, dv_ref, uext, dext, dwacc, ush, dsh):
        i = pl.program_id(0)

        @pl.when(i == 0)
        def _():
            dwacc[...] = jnp.zeros_like(dwacc)
            dv_ref[...] = jnp.zeros_like(dv_ref)

        row = i * TR + lax.broadcasted_iota(jnp.int32, (TR, 1), 0)
        hrow = i * TR - CHUNK + lax.broadcasted_iota(jnp.int32, (CHUNK, 1), 0)
        uext[pl.ds(0, CHUNK), :] = jnp.where(i > 0, _glu(prev_ref[...], hrow), 0.0)
        uext[pl.ds(CHUNK, TR), :] = _glu(cur_ref[...], row)

        s_b = jnp.zeros((1, D_CONV), F32)
        s_g = jnp.zeros((1, D_CONV), F32)
        s_bb = jnp.zeros((1, D_CONV), F32)
        for rb in range(TR // CONV_RB):
            rows = pl.ds(rb * CONV_RB, CONV_RB)
            gate = cur_ref[rows, 2 * D_CONV:3 * D_CONV].astype(F32)
            dout = dy_ref[rows, :].astype(F32)
            dyc, dyn, xhat, dgate = ln_bwd(yc_ref[rows, :], dout, gate, v_ref)
            du_ref[rows, 2 * D_CONV:3 * D_CONV] = dgate.astype(BF)
            dext[rows, :] = dyc
            s_b = s_b + jnp.sum(dyc, axis=0, keepdims=True)
            s_g = s_g + jnp.sum(dyn * xhat, axis=0, keepdims=True)
            s_bb = s_bb + jnp.sum(dyn, axis=0, keepdims=True)
        dv_ref[0:1, :] += s_b
        dv_ref[1:2, :] += s_g
        dv_ref[2:3, :] += s_bb
        dyc_n, _, _, _ = ln_bwd(ycn_ref[...], dyn_ref[...].astype(F32),
                                next_ref[:, 2 * D_CONV:3 * D_CONV].astype(F32), v_ref)
        dext[pl.ds(TR, CHUNK), :] = jnp.where(i < nt - 1, dyc_n, 0.0)
        _fill_shifts(uext, ush)
        _fill_shifts(dext, dsh)

        for rb in range(TR // CONV_RB):
            r0 = rb * CONV_RB
            rows = pl.ds(r0, CONV_RB)
            d_blk = dext[rows, :]
            dglu = jnp.zeros((CONV_RB, D_CONV), F32)
            for j in range(CONV_WIDTH):
                dglu = dglu + _shifted(dext, dsh, r0 + CONV_WIDTH - 1 - j, CONV_RB) * w_ref[j:j + 1, :]
                prod = d_blk * _shifted(uext, ush, r0 + CHUNK - (CONV_WIDTH - 1) + j, CONV_RB)
                part = prod[0:8, :]
                for s in range(1, CONV_RB // 8):
                    part = part + prod[8 * s:8 * s + 8, :]
                dwacc[j] += part
            a = cur_ref[rows, 0:D_CONV].astype(F32)
            sg = _sig(cur_ref[rows, D_CONV:2 * D_CONV].astype(F32))
            grow = i * TR + r0 + lax.broadcasted_iota(jnp.int32, (CONV_RB, 1), 0)
            dglu = jnp.where(grow >= META_PAD, dglu, 0.0)
            du_ref[rows, 0:D_CONV] = (dglu * sg).astype(BF)
            du_ref[rows, D_CONV:2 * D_CONV] = (dglu * a * sg * (1.0 - sg)).astype(BF)

        @pl.when(i == nt - 1)
        def _():
            dw_ref[...] = jnp.sum(dwacc[...], axis=1)

    nxt = lambda i: (jnp.minimum(i * hb + hb, nhb - 1), 0)
    return pl.pallas_call(
        body, name="conv_bwd", grid=(nt,),
        in_specs=[pl.BlockSpec((TR, N_A), lambda i: (i, 0)),
                  pl.BlockSpec((CHUNK, N_A), lambda i: (jnp.maximum(i * hb - 1, 0), 0)),
                  pl.BlockSpec((CHUNK, N_A), nxt),
                  pl.BlockSpec((TR, D_CONV), lambda i: (i, 0)),
                  pl.BlockSpec((CHUNK, D_CONV), nxt),
                  pl.BlockSpec((TR, D_CONV), lambda i: (i, 0)),
                  pl.BlockSpec((CHUNK, D_CONV), nxt),
                  pl.BlockSpec((CONV_WIDTH, D_CONV), lambda i: (0, 0)),
                  pl.BlockSpec((8, D_CONV), lambda i: (0, 0))],
        out_specs=[pl.BlockSpec((TR, N_A), lambda i: (i, 0)),
                   pl.BlockSpec((32, D_CONV), lambda i: (0, 0)),
                   pl.BlockSpec((8, D_CONV), lambda i: (0, 0))],
        out_shape=[jax.ShapeDtypeStruct((Lp, N_A), BF), jax.ShapeDtypeStruct((32, D_CONV), F32),
                   jax.ShapeDtypeStruct((8, D_CONV), F32)],
        scratch_shapes=[pltpu.VMEM((TR + CHUNK, D_CONV), F32), pltpu.VMEM((TR + CHUNK, D_CONV), F32),
                        pltpu.VMEM((32, 8, D_CONV), F32), pltpu.VMEM((7, _SH_ROWS, D_CONV), F32),
                        pltpu.VMEM((7, _SH_ROWS, D_CONV), F32)],
        compiler_params=_cp(("arbitrary",)),
    )(ua, ua, ua, yconv, yconv, dya, dya, cw, cvec)


def _hg_gates(ub_ref, lbv, row):
    q = ub_ref[:, 0:512].astype(F32)
    z = ub_ref[:, 512:1024].astype(F32)
    valid = row >= META_PAD
    sig = _sig(z)
    f = lbv + (1.0 - lbv) * sig
    g = jnp.where(valid, jnp.log(jnp.maximum(f, F_FLOOR)), 0.0)
    k = jnp.where(valid, (1.0 - lbv) * (1.0 - sig), 0.0)
    return q, k, g, sig, f


def _hg_chunk_terms(b_c, q_c, k_c):
    bm = b_c[CHUNK // 2 - 1:CHUNK // 2, :]
    bl = b_c[CHUNK - 1:CHUNK, :]
    e1 = jnp.exp(b_c - bm)
    e2 = jnp.exp(bm - b_c)
    e0 = jnp.exp(b_c)
    e3 = jnp.exp(bl - b_c)
    el = jnp.exp(bl)
    return e1, e2, e0, e3, el, q_c * e1, k_c * e2, q_c * e0, k_c * e3


def _hg_fwd(ub, lb, gn4, carry=None):
    Lp = ub.shape[0]
    nt = Lp // TR
    cpt = TR // CHUNK

    def body(ub_ref, lb_ref, gn_ref, yb_ref, o_ref, ss_ref, st, bsc, qsc, ksc, qes, els, ust, tlo):
        i = pl.program_id(0)

        @pl.when(i == 0)
        def _():
            st[...] = jnp.zeros_like(st)
            tlo[...] = _chunk_tri(TR, False)

        row = i * TR + lax.broadcasted_iota(jnp.int32, (TR, 1), 0)
        q, k, g, _, _ = _hg_gates(ub_ref, lb_ref[...], row)
        qsc[...] = _silu(q)
        ksc[...] = k
        bsc[...] = _mm_split(tlo[...], g)
        tri = lax.broadcasted_iota(jnp.int32, (CHUNK, CHUNK), 1) <= lax.broadcasted_iota(jnp.int32, (CHUNK, CHUNK), 0)

        def intra(c, carry):
            rows = pl.ds(pl.multiple_of(c * CHUNK, CHUNK), CHUNK)
            _, _, _, _, el, qe, ke, qE, kd = _hg_chunk_terms(bsc[rows, :], qsc[rows, :], ksc[rows, :])
            qe, ke, kd = qe.astype(BF), ke.astype(BF), kd.astype(BF)
            qes[rows, :] = qE.astype(BF)
            els[c] = jnp.broadcast_to(el, (8, 512))
            sls = [slice(HG_D * h, HG_D * (h + 1)) for h in range(HG_HEADS)]
            v = [ub_ref[rows, 1024 + HG_D * h:1024 + HG_D * (h + 1)] for h in range(HG_HEADS)]
            a = [_nt(qe[:, sl], ke[:, sl]) for sl in sls]
            u = [_tn(v[h], kd[:, sls[h]]) for h in range(HG_HEADS)]
            a = [jnp.where(tri, x, 0.0).astype(BF) for x in a]
            oi = [_nn(a[h], v[h]) for h in range(HG_HEADS)]
            for h in range(HG_HEADS):
                ust[c, h] = u[h]
                o_ref[rows, sls[h]] = oi[h]
            return carry

        lax.fori_loop(0, cpt, intra, 0, unroll=2)

        for h in range(HG_HEADS):
            sl = slice(HG_D * h, HG_D * (h + 1))
            s = st[h]
            for c in range(cpt):
                ss_ref[c, h] = s
                s = els[c, 0:1, sl] * s + ust[c, h]
            st[h] = s

        def inter(c, carry):
            rows = pl.ds(pl.multiple_of(c * CHUNK, CHUNK), CHUNK)
            for h in range(HG_HEADS):
                sl = slice(HG_D * h, HG_D * (h + 1))
                o_ref[rows, sl] += _nt(qes[rows, sl], ss_ref[c, h].astype(BF))
            return carry

        lax.fori_loop(0, cpt, inter, 0, unroll=2)

        gate = ub_ref[:, 1536:2048].astype(F32)
        for h in range(HG_HEADS):
            sl = slice(HG_D * h, HG_D * (h + 1))
            o = o_ref[:, sl]
            r = lax.rsqrt(jnp.mean(o * o, axis=-1, keepdims=True) + EPS)
            yb_ref[:, sl] = (o * r * gn_ref[:, sl] * _silu(gate[:, sl])).astype(BF)

    in_specs = [pl.BlockSpec((TR, N_B), lambda i: (i, 0)), pl.BlockSpec((1, 512), lambda i: (0, 0)),
                pl.BlockSpec((1, 512), lambda i: (0, 0))]
    out_specs = [pl.BlockSpec((TR, 512), lambda i: (i, 0)), pl.BlockSpec((TR, 512), lambda i: (i, 0)),
                 pl.BlockSpec((cpt, HG_HEADS, HG_D, HG_D), lambda i: (i, 0, 0, 0))]
    out_shape = [jax.ShapeDtypeStruct((Lp, 512), BF), jax.ShapeDtypeStruct((Lp, 512), F32),
                 jax.ShapeDtypeStruct((Lp // CHUNK, HG_HEADS, HG_D, HG_D), F32)]
    scratch = [pltpu.VMEM((HG_HEADS, HG_D, HG_D), F32), pltpu.VMEM((TR, 512), F32),
               pltpu.VMEM((TR, 512), F32), pltpu.VMEM((TR, 512), F32), pltpu.VMEM((TR, 512), BF),
               pltpu.VMEM((cpt, 8, 512), F32), pltpu.VMEM((cpt, HG_HEADS, HG_D, HG_D), F32), pltpu.VMEM((TR, TR), BF)]
    return _call_carrying(body, "hgrn_fwd", nt, in_specs, out_specs, out_shape, scratch, (ub, lb, gn4), carry)


def _hg_bwd(ub, lb, gn4, o_save, s_save, dyb, carry=None):
    Lp = ub.shape[0]
    nt = Lp // TR
    cpt = TR // CHUNK

    def body(ub_ref, lb_ref, gn_ref, o_ref, ss_ref, dy_ref, du_ref, ds_ref,
             dst, bsc, qsc, ksc, dosc, dqsc, dksc, dbsc, els, ust, dss, tlo, tup):
        i = pl.program_id(0)
        t = nt - 1 - i

        @pl.when(i == 0)
        def _():
            dst[...] = jnp.zeros_like(dst)
            ds_ref[...] = jnp.zeros_like(ds_ref)
            tlo[...] = _chunk_tri(TR, False)
            tup[...] = _chunk_tri(TR, True)

        lbv = lb_ref[...]
        row = t * TR + lax.broadcasted_iota(jnp.int32, (TR, 1), 0)
        valid = row >= META_PAD
        q, k, g, sig, f = _hg_gates(ub_ref, lbv, row)
        silu_q, dsilu_q = _silu_pair(q)
        qsc[...] = silu_q
        ksc[...] = k
        bsc[...] = _mm_split(tlo[...], g)

        gate = ub_ref[:, 1536:2048].astype(F32)
        dy = dy_ref[...].astype(F32)
        for h in range(HG_HEADS):
            sl = slice(HG_D * h, HG_D * (h + 1))
            o = o_ref[:, sl]
            r = lax.rsqrt(jnp.mean(o * o, axis=-1, keepdims=True) + EPS)
            ohat = o * r
            silu_g, dsilu_g = _silu_pair(gate[:, sl])
            don = dy[:, sl] * silu_g
            du_ref[:, 1536 + HG_D * h:1536 + HG_D * (h + 1)] = (dy[:, sl] * ohat * gn_ref[:, sl] * dsilu_g).astype(BF)
            ds_ref[1:2, sl] += jnp.sum(don * ohat, axis=0, keepdims=True)
            gd = don * gn_ref[:, sl]
            dosc[:, sl] = r * (gd - ohat * jnp.mean(gd * ohat, axis=-1, keepdims=True))

        tri = lax.broadcasted_iota(jnp.int32, (CHUNK, CHUNK), 1) <= lax.broadcasted_iota(jnp.int32, (CHUNK, CHUNK), 0)
        last = lax.broadcasted_iota(jnp.int32, (CHUNK, 1), 0) == CHUNK - 1

        def incr(c, carry):
            rows = pl.ds(pl.multiple_of(c * CHUNK, CHUNK), CHUNK)
            b_c = bsc[rows, :]
            qE_b = (qsc[rows, :] * jnp.exp(b_c)).astype(BF)
            els[c] = jnp.broadcast_to(jnp.exp(b_c[CHUNK - 1:CHUNK, :]), (8, 512))
            do_c = dosc[rows, :].astype(BF)
            for h in range(HG_HEADS):
                sl = slice(HG_D * h, HG_D * (h + 1))
                ust[c, h] = _tn(do_c[:, sl], qE_b[:, sl])
            return carry

        lax.fori_loop(0, cpt, incr, 0, unroll=2)

        for h in range(HG_HEADS):
            sl = slice(HG_D * h, HG_D * (h + 1))
            d_s = dst[h]
            for c in reversed(range(cpt)):
                dss[c, h] = d_s
                d_s = els[c, 0:1, sl] * d_s + ust[c, h]
            dst[h] = d_s

        def chunk(c, carry):
            r0 = pl.multiple_of(c * CHUNK, CHUNK)
            rows = pl.ds(r0, CHUNK)
            e1, e2, e0, e3, el, qe, ke, qE, kd = _hg_chunk_terms(bsc[rows, :], qsc[rows, :], ksc[rows, :])
            qe_b, ke_b, kd_b = qe.astype(BF), ke.astype(BF), kd.astype(BF)
            do_c = dosc[rows, :].astype(BF)
            hs = range(HG_HEADS)
            sls = [slice(HG_D * h, HG_D * (h + 1)) for h in hs]
            v = [ub_ref[rows, 1024 + HG_D * h:1024 + HG_D * (h + 1)] for h in hs]
            do = [do_c[:, sl] for sl in sls]
            a = [_nt(qe_b[:, sl], ke_b[:, sl]) for sl in sls]
            da = [_nt(do[h], v[h]) for h in hs]
            dqE = [_nn(do[h], ss_ref[c, h].astype(BF)) for h in hs]
            dkd = [_nn(v[h], dss[c, h].astype(BF)) for h in hs]
            dv2 = [_nt(kd_b[:, sls[h]], dss[c, h].astype(BF)) for h in hs]
            a = [jnp.where(tri, x, 0.0).astype(BF) for x in a]
            da = [jnp.where(tri, x, 0.0).astype(BF) for x in da]
            dv = [_tn(a[h], do[h]) + dv2[h] for h in hs]
            dqe = [_nn(da[h], ke_b[:, sls[h]]) for h in hs]
            dke = [_tn(da[h], qe_b[:, sls[h]]) for h in hs]
            for h in hs:
                sl = sls[h]
                del_h = jnp.sum(ss_ref[c, h] * dss[c, h], axis=0, keepdims=True)
                dqsc[rows, sl] = dqE[h] * e0[:, sl] + dqe[h] * e1[:, sl]
                dksc[rows, sl] = dke[h] * e2[:, sl] + dkd[h] * e3[:, sl]
                tkd = dkd[h] * kd[:, sl]
                dbl = jnp.sum(tkd, axis=0, keepdims=True) + del_h * el[:, sl]
                dbsc[rows, sl] = (dqE[h] * qE[:, sl] + dqe[h] * qe[:, sl] - dke[h] * ke[:, sl] - tkd
                                  + jnp.where(last, dbl, 0.0))
                du_ref[rows, 1024 + HG_D * h:1024 + HG_D * (h + 1)] = dv[h].astype(BF)
            return carry

        lax.fori_loop(0, cpt, chunk, 0, unroll=2)

        dg = _mm_split(tup[...], dbsc[...])
        df = jnp.where(valid & (f > F_FLOOR), dg / f, 0.0)
        dk = jnp.where(valid, dksc[...], 0.0)
        dsig = (df - dk) * (1.0 - lbv)
        ds_ref[0:1, :] += jnp.sum((df - dk) * (1.0 - sig), axis=0, keepdims=True)
        du_ref[:, 512:1024] = (dsig * sig * (1.0 - sig)).astype(BF)
        du_ref[:, 0:512] = (dqsc[...] * dsilu_q).astype(BF)

    rev = lambda i: (nt - 1 - i, 0)
    in_specs = [pl.BlockSpec((TR, N_B), rev), pl.BlockSpec((1, 512), lambda i: (0, 0)),
                pl.BlockSpec((1, 512), lambda i: (0, 0)), pl.BlockSpec((TR, 512), rev),
                pl.BlockSpec((cpt, HG_HEADS, HG_D, HG_D), lambda i: (nt - 1 - i, 0, 0, 0)),
                pl.BlockSpec((TR, 512), rev)]
    out_specs = [pl.BlockSpec((TR, N_B), rev), pl.BlockSpec((8, 512), lambda i: (0, 0))]
    out_shape = [jax.ShapeDtypeStruct((Lp, N_B), BF), jax.ShapeDtypeStruct((8, 512), F32)]
    states = pltpu.VMEM((cpt, HG_HEADS, HG_D, HG_D), F32)
    scratch = ([pltpu.VMEM((HG_HEADS, HG_D, HG_D), F32)] + [pltpu.VMEM((TR, 512), F32)] * 7
               + [pltpu.VMEM((cpt, 8, 512), F32), states, states, pltpu.VMEM((TR, TR), BF), pltpu.VMEM((TR, TR), BF)])
    return _call_carrying(body, "hgrn_bwd", nt, in_specs, out_specs, out_shape, scratch,
                          (ub, lb, gn4, o_save, s_save, dyb), carry)


_KCOL = (2 * 512) // 128
_VCOL = _KCOL + 1


def _swa_in_specs(nt, rev):
    tile = (lambda i: nt - 1 - i) if rev else (lambda i: i)
    hpt = TR // HALO
    return [
        pl.BlockSpec((TR, 512), lambda i: (tile(i), 0)),
        pl.BlockSpec((TR, 512), lambda i: (tile(i), 1)),
        pl.BlockSpec((TR, 128), lambda i: (tile(i), _KCOL)),
        pl.BlockSpec((TR, 128), lambda i: (tile(i), _VCOL)),
        pl.BlockSpec((HALO, 128), lambda i: (jnp.maximum(tile(i) * hpt - 1, 0), _KCOL)),
        pl.BlockSpec((HALO, 128), lambda i: (jnp.maximum(tile(i) * hpt - 1, 0), _VCOL)),
        pl.BlockSpec((CHUNK, 128), lambda i: (0, _KCOL)),
        pl.BlockSpec((CHUNK, 128), lambda i: (0, _VCOL)),
        pl.BlockSpec((1, 512), lambda i: (0, 0)),
        pl.BlockSpec((1, 128), lambda i: (0, 0)),
        pl.BlockSpec((1, ATT_Q_HEADS), lambda i: (0, 0)),
    ]


_WROWS = 2 * CHUNK + HALO + TR
_W0 = 2 * CHUNK
_C0 = _W0 + HALO
_SCALE = ATT_HD ** -0.5


def _group_ones(n):
    r = lax.broadcasted_iota(jnp.int32, (n, n), 0)
    c = lax.broadcasted_iota(jnp.int32, (n, n), 1)
    return jnp.where(jnp.right_shift(r, 6) == jnp.right_shift(c, 6), 1.0, 0.0).astype(BF)


def _group_mean(x, ones):
    hi = x.astype(BF)
    lo = (x - hi.astype(F32)).astype(BF)
    return (_nn(hi, ones) + _nn(lo, ones)) * (1.0 / ATT_HD)


def _head_rms(x, ones):
    r = lax.rsqrt(_group_mean(x * x, ones) + EPS)
    return x * r, r


def _swa_windows(kc_ref, vc_ref, kh_ref, vh_ref, km_ref, vm_ref, kg2, ones, kwin, krwin, vwin, vrwin):
    meta = pl.ds(META_PAD, N_META)
    for (k, v, r0, n) in ((km_ref[meta, :], vm_ref[meta, :], 0, N_META), (kh_ref[...], vh_ref[...], _W0, HALO),
                          (kc_ref[...], vc_ref[...], _C0, TR)):
        xhat, _ = _head_rms(k.astype(F32), ones)
        kn = xhat * kg2
        kwin[pl.ds(r0, n), :] = kn.astype(BF)
        krwin[pl.ds(r0, n), :] = pltpu.roll(kn, ATT_HD, 1).astype(BF)
        vwin[pl.ds(r0, n), :] = v
        if vrwin is not None:
            vrwin[pl.ds(r0, n), :] = pltpu.roll(v.astype(F32), ATT_HD, 1).astype(BF)
    zero = jnp.zeros((_W0 - N_META, 128), BF)
    for w in (kwin, krwin, vwin, vrwin):
        if w is not None:
            w[pl.ds(N_META, _W0 - N_META), :] = zero


def _swa_masks_t(t, qb):
    q0 = t * TR + qb * QB
    qc = jnp.right_shift(q0 + lax.broadcasted_iota(jnp.int32, (1, QB), 1), 6)
    kabs = q0 - HALO + lax.broadcasted_iota(jnp.int32, (QB + HALO, 1), 0)
    kc = jnp.right_shift(kabs + HALO, 6) - HALO // CHUNK
    mask_w = (kc <= qc) & (kc >= qc - 2) & (kabs >= META_PAD)
    return qc > 2, mask_w


def _swa_park(dtype):
    return [pltpu.VMEM((ATT_Q_HEADS, N_META, QB), dtype), pltpu.VMEM((ATT_Q_HEADS, QB + HALO, QB), dtype)]


def _swa_ones():
    return [pltpu.VMEM((128, 128), BF), pltpu.VMEM((512, 512), BF)]


def _split_heads(x, lane_hi):
    return jnp.where(lane_hi, 0.0, x).astype(BF), jnp.where(lane_hi, x, 0.0).astype(BF)


def _call_carrying(body, name, nt, in_specs, out_specs, out_shape, scratch, args, carry):
    if carry is None:
        return pl.pallas_call(body, name=name, grid=(nt,), in_specs=in_specs, out_specs=out_specs, out_shape=out_shape,
                              scratch_shapes=scratch, compiler_params=_cp(("arbitrary",)))(*args)
    kind, arrs = carry
    n = len(arrs)
    return pl.pallas_call(
        _carry_exchange(body, len(in_specs), len(out_specs), nt, kind, n), name=name + "_" + kind, grid=(nt,),
        in_specs=in_specs + [_ANY] * n, out_specs=out_specs + [_ANY] * n,
        out_shape=out_shape + _exchange_out_shapes(kind, arrs), scratch_shapes=scratch + _exchange_sems(n),
        compiler_params=_cp(("arbitrary",), has_side_effects=True),
    )(*args, *arrs)


def _swa_fwd(uc, qg8, kg2, sinks, carry=None):
    Lp = uc.shape[0]
    nt = Lp // TR
    nqb = TR // QB

    def body(q_ref, g_ref, kc_ref, vc_ref, kh_ref, vh_ref, km_ref, vm_ref, qg_ref, kg_ref, sk_ref,
             yc_ref, o_ref, lse_ref, kwin, krwin, vwin, vt, qlo, qhi, ot, s_m, s_w, p_m, p_w, g128, g512):
        t = pl.program_id(0)

        @pl.when(t == 0)
        def _():
            g128[...] = _group_ones(128)
            g512[...] = _group_ones(512)

        _swa_windows(kc_ref, vc_ref, kh_ref, vh_ref, km_ref, vm_ref, kg_ref[...], g128[...], kwin, krwin, vwin, None)
        vt[...] = vwin[...].T
        xhat, _ = _head_rms(q_ref[...].astype(F32), g512[...])
        lane_hi = (lax.broadcasted_iota(jnp.int32, (1, 512), 1) & ATT_HD) != 0
        lo, hi = _split_heads(xhat * qg_ref[...] * _SCALE, lane_hi)
        qlo[...] = lo
        qhi[...] = hi
        for qb in range(nqb):
            rows = pl.ds(qb * QB, QB)
            wrows = pl.ds(_W0 + qb * QB, QB + HALO)
            mrows = pl.ds(0, N_META)
            mask_m, mask_w = _swa_masks_t(t, qb)
            for j in range(ATT_Q_HEADS):
                p, e = j // 2, j % 2
                ks = kwin if e == j // ATT_GROUP else krwin
                qp = (qlo, qhi)[e][rows, 128 * p:128 * (p + 1)]
                s_m[j] = _nt(ks[mrows, :], qp)
                s_w[j] = _nt(ks[wrows, :], qp)
            inv = []
            for j in range(ATT_Q_HEADS):
                sm = jnp.where(mask_m, s_m[j], NEG)
                sw = jnp.where(mask_w, s_w[j], NEG)
                sink = sk_ref[:, j:j + 1]
                m = jnp.maximum(jnp.maximum(jnp.max(sm, axis=0, keepdims=True),
                                            jnp.max(sw, axis=0, keepdims=True)), sink)
                em = jnp.exp(sm - m)
                ew = jnp.exp(sw - m)
                den = jnp.sum(em, axis=0, keepdims=True) + jnp.sum(ew, axis=0, keepdims=True) + jnp.exp(sink - m)
                p_m[j] = em.astype(BF)
                p_w[j] = ew.astype(BF)
                lse_ref[j:j + 1, pl.ds(qb * QB, QB)] = m + jnp.log(den)
                inv.append(1.0 / den)
            for j in range(ATT_Q_HEADS):
                vrows = pl.ds(ATT_HD * (j // ATT_GROUP), ATT_HD)
                ot[pl.ds(ATT_HD * j, ATT_HD), pl.ds(qb * QB, QB)] = (
                    _nn(vt[vrows, pl.ds(0, N_META)], p_m[j])
                    + _nn(vt[vrows, pl.ds(_W0 + qb * QB, QB + HALO)], p_w[j])) * inv[j]
        o = ot[...].T
        o_ref[...] = o
        yc_ref[...] = (o * _silu(g_ref[...].astype(F32))).astype(BF)

    win = pltpu.VMEM((_WROWS, 128), BF)
    in_specs = _swa_in_specs(nt, False)
    out_specs = [pl.BlockSpec((TR, 512), lambda i: (i, 0)), pl.BlockSpec((TR, 512), lambda i: (i, 0)),
                 pl.BlockSpec((ATT_Q_HEADS, TR), lambda i: (0, i))]
    out_shape = [jax.ShapeDtypeStruct((Lp, 512), BF), jax.ShapeDtypeStruct((Lp, 512), F32),
                 jax.ShapeDtypeStruct((ATT_Q_HEADS, Lp), F32)]
    scratch = [win, win, win, pltpu.VMEM((128, _WROWS), BF), pltpu.VMEM((TR, 512), BF),
               pltpu.VMEM((TR, 512), BF), pltpu.VMEM((512, TR), F32)] + _swa_park(F32) + _swa_park(BF) + _swa_ones()
    return _call_carrying(body, "swa_fwd", nt, in_specs, out_specs, out_shape, scratch,
                          (uc, uc, uc, uc, uc, uc, uc, uc, qg8, kg2, sinks), carry)


def _swa_bwd(uc, qg8, kg2, sinks, o_save, lse, dyc):
    Lp = uc.shape[0]
    nt = Lp // TR
    nqb = TR // QB

    def body(q_ref, g_ref, kc_ref, vc_ref, kh_ref, vh_ref, km_ref, vm_ref, qg_ref, kg_ref, sk_ref,
             o_ref, lse_ref, dy_ref, du_ref, dg_ref, dsk_ref,
             kwin, krwin, vwin, vrwin, kt, krt, qlo, qhi, dolo, dohi, dqt, dk_dir, dk_rol, dv_dir, dv_rol,
             carry_k, carry_v, meta_k, meta_v, s_m, s_w, dp_m, dp_w, p_m, p_w, ds_m, ds_w, g128, g512):
        i = pl.program_id(0)
        t = nt - 1 - i

        @pl.when(i == 0)
        def _():
            carry_k[...] = jnp.zeros_like(carry_k)
            carry_v[...] = jnp.zeros_like(carry_v)
            meta_k[...] = jnp.zeros_like(meta_k)
            meta_v[...] = jnp.zeros_like(meta_v)
            dg_ref[...] = jnp.zeros_like(dg_ref)
            dsk_ref[...] = jnp.zeros_like(dsk_ref)
            g128[...] = _group_ones(128)
            g512[...] = _group_ones(512)

        ones128 = g128[...]
        ones512 = g512[...]
        _swa_windows(kc_ref, vc_ref, kh_ref, vh_ref, km_ref, vm_ref, kg_ref[...], ones128, kwin, krwin, vwin, vrwin)
        kt[...] = kwin[...].T
        krt[...] = krwin[...].T
        xhat_q, r_q = _head_rms(q_ref[...].astype(F32), ones512)
        lane_hi = (lax.broadcasted_iota(jnp.int32, (1, 512), 1) & ATT_HD) != 0
        lo, hi = _split_heads(xhat_q * qg_ref[...] * _SCALE, lane_hi)
        qlo[...] = lo
        qhi[...] = hi
        gate = g_ref[...].astype(F32)
        dy = dy_ref[...].astype(F32)
        silu_g, dsilu_g = _silu_pair(gate)
        do = dy * silu_g
        o = o_ref[...]
        du_ref[:, 512:1024] = (dy * o * dsilu_g).astype(BF)
        lo, hi = _split_heads(do, lane_hi)
        dolo[...] = lo
        dohi[...] = hi
        hsel = jnp.where(jnp.right_shift(lax.broadcasted_iota(jnp.int32, (ATT_Q_HEADS, 512), 1), 6)
                         == lax.broadcasted_iota(jnp.int32, (ATT_Q_HEADS, 512), 0), 1.0, 0.0).astype(BF)
        prod = do * o
        p_hi = prod.astype(BF)
        d_t = _nt(hsel, p_hi) + _nt(hsel, (prod - p_hi.astype(F32)).astype(BF))
        for acc in (dk_dir, dk_rol, dv_dir, dv_rol):
            acc[...] = jnp.zeros_like(acc)

        for qb in range(nqb):
            rows = pl.ds(qb * QB, QB)
            qcols = pl.ds(qb * QB, QB)
            wrows = pl.ds(_W0 + qb * QB, QB + HALO)
            mrows = pl.ds(0, N_META)
            mask_m, mask_w = _swa_masks_t(t, qb)
            for j in range(ATT_Q_HEADS):
                p, e = j // 2, j % 2
                ks, vs = (kwin, vwin) if e == j // ATT_GROUP else (krwin, vrwin)
                pair = slice(128 * p, 128 * (p + 1))
                qp = (qlo, qhi)[e][rows, pair]
                dop = (dolo, dohi)[e][rows, pair]
                s_m[j] = _nt(ks[mrows, :], qp)
                s_w[j] = _nt(ks[wrows, :], qp)
                dp_m[j] = _nt(vs[mrows, :], dop)
                dp_w[j] = _nt(vs[wrows, :], dop)
            for j in range(ATT_Q_HEADS):
                lse_j = lse_ref[j:j + 1, qcols]
                d_j = d_t[j:j + 1, qb * QB:(qb + 1) * QB]
                em = jnp.exp(jnp.where(mask_m, s_m[j], NEG) - lse_j)
                ew = jnp.exp(jnp.where(mask_w, s_w[j], NEG) - lse_j)
                p_m[j] = em.astype(BF)
                p_w[j] = ew.astype(BF)
                ds_m[j] = (em * (dp_m[j] - d_j)).astype(BF)
                ds_w[j] = (ew * (dp_w[j] - d_j)).astype(BF)
                dsk_ref[j:j + 1, :] -= jnp.exp(sk_ref[:, j:j + 1] - lse_j) * d_j
            for j in range(ATT_Q_HEADS):
                e = j % 2
                ktr = kt if e == j // ATT_GROUP else krt
                hrows = pl.ds(ATT_HD * e, ATT_HD)
                dqt[pl.ds(ATT_HD * j, ATT_HD), qcols] = (_nn(ktr[hrows, pl.ds(0, N_META)], ds_m[j])
                                                         + _nn(ktr[hrows, pl.ds(_W0 + qb * QB, QB + HALO)], ds_w[j]))
            for direct, dk_acc, dv_acc in ((True, dk_dir, dv_dir), (False, dk_rol, dv_rol)):
                heads = [j for j in range(ATT_Q_HEADS) if (j % 2 == j // ATT_GROUP) == direct]
                q_cat = jnp.concatenate([(qlo, qhi)[j % 2][rows, 128 * (j // 2):128 * (j // 2 + 1)] for j in heads], axis=0)
                do_cat = jnp.concatenate([(dolo, dohi)[j % 2][rows, 128 * (j // 2):128 * (j // 2 + 1)] for j in heads], axis=0)
                dk_acc[mrows, :] += _nn(jnp.concatenate([ds_m[j] for j in heads], axis=1), q_cat)
                dk_acc[wrows, :] += _nn(jnp.concatenate([ds_w[j] for j in heads], axis=1), q_cat)
                dv_acc[mrows, :] += _nn(jnp.concatenate([p_m[j] for j in heads], axis=1), do_cat)
                dv_acc[wrows, :] += _nn(jnp.concatenate([p_w[j] for j in heads], axis=1), do_cat)

        dk_dir[...] += pltpu.roll(dk_rol[...], ATT_HD, 1)
        dv_dir[...] += pltpu.roll(dv_rol[...], ATT_HD, 1)
        meta_k[...] += dk_dir[pl.ds(0, N_META), :]
        meta_v[...] += dv_dir[pl.ds(0, N_META), :]
        first = jnp.where(t == 0, 1.0, 0.0)
        dk_dir[pl.ds(_C0 + TR - HALO, HALO), :] += carry_k[...]
        dv_dir[pl.ds(_C0 + TR - HALO, HALO), :] += carry_v[...]
        dk_dir[pl.ds(_C0 + META_PAD, N_META), :] += first * meta_k[...]
        dv_dir[pl.ds(_C0 + META_PAD, N_META), :] += first * meta_v[...]
        carry_k[...] = dk_dir[pl.ds(_W0, HALO), :]
        carry_v[...] = dv_dir[pl.ds(_W0, HALO), :]

        du_ref[:, 1152:1280] = dv_dir[pl.ds(_C0, TR), :].astype(BF)
        xhat_k, r_k = _head_rms(kc_ref[...].astype(F32), ones128)
        dkn = dk_dir[pl.ds(_C0, TR), :]
        dg_ref[1:2, 0:128] += jnp.sum(dkn * xhat_k, axis=0, keepdims=True)
        gd = dkn * kg_ref[...]
        du_ref[:, 1024:1152] = (r_k * (gd - xhat_k * _group_mean(gd * xhat_k, ones128))).astype(BF)
        dqn = dqt[...].T * _SCALE
        dg_ref[0:1, :] += jnp.sum(dqn * xhat_q, axis=0, keepdims=True)
        gd = dqn * qg_ref[...]
        du_ref[:, 0:512] = (r_q * (gd - xhat_q * _group_mean(gd * xhat_q, ones512))).astype(BF)

    rev = lambda i: (nt - 1 - i, 0)
    specs = _swa_in_specs(nt, True)
    win = pltpu.VMEM((_WROWS, 128), BF)
    wint = pltpu.VMEM((128, _WROWS), BF)
    tile_bf = pltpu.VMEM((TR, 512), BF)
    acc = pltpu.VMEM((_WROWS, 128), F32)
    return pl.pallas_call(
        body, name="swa_bwd", grid=(nt,),
        in_specs=specs + [pl.BlockSpec((TR, 512), rev), pl.BlockSpec((ATT_Q_HEADS, TR), lambda i: (0, nt - 1 - i)),
                          pl.BlockSpec((TR, 512), rev)],
        out_specs=[pl.BlockSpec((TR, N_C), rev), pl.BlockSpec((8, 512), lambda i: (0, 0)),
                   pl.BlockSpec((8, 128), lambda i: (0, 0))],
        out_shape=[jax.ShapeDtypeStruct((Lp, N_C), BF), jax.ShapeDtypeStruct((8, 512), F32),
                   jax.ShapeDtypeStruct((8, 128), F32)],
        scratch_shapes=[win, win, win, win, wint, wint, tile_bf, tile_bf, tile_bf, tile_bf,
                        pltpu.VMEM((512, TR), F32), acc, acc, acc, acc,
                        pltpu.VMEM((HALO, 128), F32), pltpu.VMEM((HALO, 128), F32),
                        pltpu.VMEM((N_META, 128), F32), pltpu.VMEM((N_META, 128), F32)]
        + _swa_park(F32) + _swa_park(F32) + _swa_park(BF) + _swa_park(BF) + _swa_ones(),
        compiler_params=_cp(("arbitrary",)),
    )(uc, uc, uc, uc, uc, uc, uc, uc, qg8, kg2, sinks, o_save, lse, dyc)


def _mix_fwd(h, ya, yb, yc, ug, wa, wb, wc, wo, g_next=None, loss=None):
    Lp = h.shape[0]
    wspec = lambda r: pl.BlockSpec((r, D_MODEL), lambda i: (0, 0))
    yspec = pl.BlockSpec((TRM, 512), lambda i: (i, 0))
    hspec = pl.BlockSpec((TRM, D_MODEL), lambda i: (i, 0))
    last = loss is not None

    def body(h_ref, ya_ref, yb_ref, yc_ref, ug_ref, wa_ref, wb_ref, wc_ref, wo_ref, t_ref, *outs):
        mixed = jnp.zeros((TRM, D_MODEL), F32)
        for n, (y_ref, w_ref) in enumerate(((ya_ref, wa_ref), (yb_ref, wb_ref), (yc_ref, wc_ref))):
            z = _nn(y_ref[...], w_ref[...])
            outs[n][...] = z.astype(BF)
            mixed = mixed + _sig(ug_ref[:, D_MODEL * n:D_MODEL * (n + 1)].astype(F32)) * z
        mixed = mixed.astype(BF)
        outs[3][...] = mixed
        x = h_ref[...] + _nn(mixed, wo_ref[...])
        if last:
            dh_ref, l_ref = outs[4:]
            i = pl.program_id(0)

            @pl.when(i == 0)
            def _():
                l_ref[...] = jnp.zeros_like(l_ref)

            row = i * TRM + lax.broadcasted_iota(jnp.int32, (TRM, 1), 0)
            e = jnp.where((row >= CHUNK) & (row < CHUNK + loss[1]), x - t_ref[...], 0.0)
            dh_ref[...] = e * (1.0 / D_MODEL)
            l_ref[...] += (0.5 / D_MODEL) * jnp.sum(jnp.sum(e * e, axis=0, keepdims=True), axis=1, keepdims=True)
        else:
            x_ref, nx_ref = outs[4:]
            x_ref[...] = x
            nx_ref[...] = (x * lax.rsqrt(jnp.mean(x * x, axis=-1, keepdims=True) + EPS) * t_ref[...]).astype(BF)

    bf = jax.ShapeDtypeStruct((Lp, D_MODEL), BF)
    f32 = jax.ShapeDtypeStruct((Lp, D_MODEL), F32)
    if last:
        tail_in, tail_spec = loss[0], hspec
        out_specs = [hspec] * 5 + [pl.BlockSpec((8, 128), lambda i: (0, 0))]
        out_shape = [bf] * 4 + [f32, jax.ShapeDtypeStruct((8, 128), F32)]
    else:
        tail_in, tail_spec = g_next, wspec(1)
        out_specs = [hspec] * 6
        out_shape = [bf] * 4 + [f32, bf]
    return pl.pallas_call(
        body, name="mix_fwd_loss" if last else "mix_fwd", grid=(Lp // TRM,),
        in_specs=[hspec, yspec, yspec, yspec, pl.BlockSpec((TRM, N_G), lambda i: (i, 0)),
                  wspec(512), wspec(512), wspec(512), wspec(D_MODEL), tail_spec],
        out_specs=out_specs, out_shape=out_shape,
        compiler_params=_cp(("arbitrary",) if last else ("parallel",)),
    )(h, ya, yb, yc, ug, wa, wb, wc, wo, tail_in)


def _mix_bwd(dh, za, zb, zc, ug, wa, wb, wc, wo):
    Lp = dh.shape[0]
    wspec = lambda r: pl.BlockSpec((r, D_MODEL), lambda i: (0, 0))
    yspec = pl.BlockSpec((TRM, 512), lambda i: (i, 0))
    hspec = pl.BlockSpec((TRM, D_MODEL), lambda i: (i, 0))
    gspec = pl.BlockSpec((TRM, N_G), lambda i: (i, 0))

    def body(dh_ref, za_ref, zb_ref, zc_ref, ug_ref, wa_ref, wb_ref, wc_ref, wo_ref,
             dug_ref, dza_ref, dzb_ref, dzc_ref, dya_ref, dyb_ref, dyc_ref):
        dmix = _nt(dh_ref[...].astype(BF), wo_ref[...])
        for n, (z_ref, w_ref, dz_ref, dy_ref) in enumerate(((za_ref, wa_ref, dza_ref, dya_ref),
                                                            (zb_ref, wb_ref, dzb_ref, dyb_ref),
                                                            (zc_ref, wc_ref, dzc_ref, dyc_ref))):
            sl = slice(D_MODEL * n, D_MODEL * (n + 1))
            gt = _sig(ug_ref[:, sl].astype(F32))
            dz = dmix * gt
            dug_ref[:, sl] = (dz * z_ref[...].astype(F32) * (1.0 - gt)).astype(BF)
            dz = dz.astype(BF)
            dz_ref[...] = dz
            dy_ref[...] = _nt(dz, w_ref[...]).astype(BF)

    bf = lambda n: jax.ShapeDtypeStruct((Lp, n), BF)
    return pl.pallas_call(
        body, name="mix_bwd", grid=(Lp // TRM,),
        in_specs=[hspec, hspec, hspec, hspec, gspec, wspec(512), wspec(512), wspec(512), wspec(D_MODEL)],
        out_specs=[gspec, hspec, hspec, hspec, yspec, yspec, yspec],
        out_shape=[bf(N_G), bf(D_MODEL), bf(D_MODEL), bf(D_MODEL), bf(512), bf(512), bf(512)],
        compiler_params=_cp(("parallel",)),
    )(dh, za, zb, zc, ug, wa, wb, wc, wo)


def _inproj_bwd(dus, ws, h, dh, g, carry=None):
    Lp = h.shape[0]
    widths = [w.shape[0] for w in ws]

    def body(dg_ref, da_ref, db_ref, dc_ref, wg_ref, wa_ref, wb_ref, wc_ref, h_ref, dh_ref, g_ref, o_ref, gg_ref):
        @pl.when(pl.program_id(0) == 0)
        def _():
            gg_ref[...] = jnp.zeros_like(gg_ref)

        dhn = (_nn(dg_ref[...], wg_ref[...]) + _nn(da_ref[...], wa_ref[...])
               + _nn(db_ref[...], wb_ref[...]) + _nn(dc_ref[...], wc_ref[...]))
        x = h_ref[...]
        r = lax.rsqrt(jnp.mean(x * x, axis=-1, keepdims=True) + EPS)
        xhat = x * r
        gg_ref[0:1, :] += jnp.sum(dhn * xhat, axis=0, keepdims=True)
        gd = dhn * g_ref[...]
        o_ref[...] = dh_ref[...] + r * (gd - xhat * jnp.mean(gd * xhat, axis=-1, keepdims=True))

    hspec = pl.BlockSpec((TRM, D_MODEL), lambda i: (i, 0))
    in_specs = ([pl.BlockSpec((TRM, n), lambda i: (i, 0)) for n in widths]
                + [pl.BlockSpec((n, D_MODEL), lambda i: (0, 0), pipeline_mode=pl.Buffered(1)) for n in widths]
                + [hspec, hspec, pl.BlockSpec((1, D_MODEL), lambda i: (0, 0))])
    out_specs = [hspec, pl.BlockSpec((8, D_MODEL), lambda i: (0, 0))]
    out_shape = [jax.ShapeDtypeStruct((Lp, D_MODEL), F32), jax.ShapeDtypeStruct((8, D_MODEL), F32)]
    return _call_carrying(body, "inproj_bwd", Lp // TRM, in_specs, out_specs, out_shape, [],
                          (*dus, *ws, h, dh, g), carry)


def _lb_softmax(lb_ref):
    x = lb_ref[...]
    e = jnp.exp(x - jnp.max(x, axis=0, keepdims=True))
    return e / jnp.sum(e, axis=0, keepdims=True)


def _lb_fwd(hg_lb):
    def body(lb_ref, o_ref):
        sm = _lb_softmax(lb_ref)
        acc = jnp.zeros((1, 512), F32)
        for l in range(DEPTH):
            if l > 0:
                acc = acc + sm[l:l + 1, :]
            o_ref[l:l + 1, :] = jnp.clip(acc, 0.0, 1.0)

    return pl.pallas_call(body, name="lb_fwd", out_shape=jax.ShapeDtypeStruct((DEPTH, 512), F32))(hg_lb)


def _lb_bwd(hg_lb, dlb_all):
    def body(lb_ref, d_ref, o_ref):
        sm = _lb_softmax(lb_ref)
        acc = jnp.zeros((1, 512), F32)
        gm = []
        for l in range(DEPTH):
            if l > 0:
                acc = acc + sm[l:l + 1, :]
            gm.append(jnp.where((acc >= 0.0) & (acc <= 1.0), d_ref[l:l + 1, :], 0.0))
        dsm = [jnp.zeros((1, 512), F32)]
        for j in range(1, DEPTH):
            s = gm[j]
            for l in range(j + 1, DEPTH):
                s = s + gm[l]
            dsm.append(s)
        dot = dsm[0] * sm[0:1, :]
        for j in range(1, DEPTH):
            dot = dot + dsm[j] * sm[j:j + 1, :]
        for j in range(DEPTH):
            o_ref[j:j + 1, :] = sm[j:j + 1, :] * (dsm[j] - dot)

    return pl.pallas_call(body, name="lb_bwd", out_shape=jax.ShapeDtypeStruct((DEPTH, 512), F32))(hg_lb, dlb_all)


_ANY = pl.BlockSpec(memory_space=pl.ANY)


def _chip_peers():
    x, y, c = lax.axis_index("x"), lax.axis_index("y"), lax.axis_index("c")
    return (x, y, c), [(1 - x, y, c), (x, 1 - y, c), (1 - x, 1 - y, c)]


def _exchange(kind, ins, outs, send, recv, loc):
    (x, y, c), peers = _chip_peers()
    me = 2 * x + y
    ds = []
    for a in range(len(ins)):
        if kind == "gather":
            ds.append(pltpu.make_async_copy(ins[a], outs[a].at[me], loc.at[a]))
        else:
            ds.append(pltpu.make_async_copy(ins[a].at[me], outs[a].at[0], loc.at[a]))
        for p, (px, py, pc) in enumerate(peers):
            src, dst = (ins[a], outs[a].at[me]) if kind == "gather" else (ins[a].at[2 * px + py], outs[a].at[1 + p])
            ds.append(pltpu.make_async_remote_copy(src_ref=src, dst_ref=dst, send_sem=send.at[a, p],
                                                   recv_sem=recv.at[a, p], device_id=(px, py, pc), device_id_type=MESH))
    return ds


def _exchange_out_shapes(kind, arrs):
    if kind == "gather":
        return [jax.ShapeDtypeStruct((4,) + a.shape, a.dtype) for a in arrs]
    return [jax.ShapeDtypeStruct(a.shape, a.dtype) for a in arrs]


def _exchange_sems(n):
    return [pltpu.SemaphoreType.DMA((n, 3)), pltpu.SemaphoreType.DMA((n, 3)), pltpu.SemaphoreType.DMA((n,))]


def _gather_first(arrs, split):
    n = len(arrs)

    def body(*refs):
        ins, outs = refs[:n], refs[n:2 * n]
        send, recv, loc, fsend, frecv = refs[2 * n:]
        (x, y, c), peers = _chip_peers()
        me = 2 * x + y

        def half(a):
            hr = arrs[a].shape[0] // 2
            return pl.ds(pl.multiple_of(c * hr, 16), hr)

        local = [pltpu.make_async_copy(ins[a], outs[a].at[me], loc.at[a]) for a in range(n)]
        far, fwd = {}, {}
        for a in range(n):
            for p, (px, py, pc) in enumerate(peers):
                src, dst = (ins[a].at[half(a)], outs[a].at[me, half(a)]) if split[a] else (ins[a], outs[a].at[me])
                far[a, p] = pltpu.make_async_remote_copy(src_ref=src, dst_ref=dst, send_sem=send.at[a, p],
                                                         recv_sem=recv.at[a, p], device_id=(px, py, pc), device_id_type=MESH)
                if split[a]:
                    landed = outs[a].at[2 * px + py, half(a)]
                    fwd[a, p] = pltpu.make_async_remote_copy(src_ref=landed, dst_ref=landed, send_sem=fsend.at[a, p],
                                                             recv_sem=frecv.at[a, p], device_id=(x, y, 1 - c),
                                                             device_id_type=MESH)
        for d in local + list(far.values()):
            d.start()
        for key, d in far.items():
            d.wait_recv()
            if key in fwd:
                fwd[key].start()
        for d in fwd.values():
            d.wait()
        for d in far.values():
            d.wait_send()
        for d in local:
            d.wait()

    sems = pltpu.SemaphoreType.DMA((n, 3))
    return pl.pallas_call(
        body, name="gather_first", in_specs=[_ANY] * n, out_specs=[_ANY] * n,
        out_shape=_exchange_out_shapes("gather", arrs),
        scratch_shapes=[sems, sems, pltpu.SemaphoreType.DMA((n,)), sems, sems],
        compiler_params=pltpu.CompilerParams(has_side_effects=True),
    )(*arrs)


def _carry_exchange(body, n_in, n_out, n_steps, kind, n):
    def wrapped(*refs):
        ins, cin = refs[:n_in], refs[n_in:n_in + n]
        outs, cout = refs[n_in + n:n_in + n + n_out], refs[n_in + n + n_out:n_in + 2 * n + n_out]
        scr, sems = refs[n_in + 2 * n + n_out:-3], refs[-3:]
        i = pl.program_id(0)

        @pl.when(i == 0)
        def _():
            for d in _exchange(kind, cin, cout, *sems):
                d.start()

        body(*ins, *outs, *scr)

        @pl.when(i == n_steps - 1)
        def _():
            for d in _exchange(kind, cin, cout, *sems):
                d.wait()

    return wrapped


def _swap_cores(arrs):
    n = len(arrs)

    def body(*refs):
        ins, outs = refs[:n], refs[n:2 * n]
        send, recv = refs[2 * n:]
        x, y, c = lax.axis_index("x"), lax.axis_index("y"), lax.axis_index("c")
        rdmas = []
        for a in range(n):
            r = pltpu.make_async_remote_copy(src_ref=ins[a], dst_ref=outs[a], send_sem=send.at[a], recv_sem=recv.at[a],
                                             device_id=(x, y, 1 - c), device_id_type=MESH)
            r.start()
            rdmas.append(r)
        for r in rdmas:
            r.wait()

    return pl.pallas_call(
        body, name="swap_cores", in_specs=[_ANY] * n, out_specs=[_ANY] * n,
        out_shape=[jax.ShapeDtypeStruct(a.shape, a.dtype) for a in arrs],
        scratch_shapes=[pltpu.SemaphoreType.DMA((n,)), pltpu.SemaphoreType.DMA((n,))],
        compiler_params=pltpu.CompilerParams(has_side_effects=True),
    )(*arrs)


def _allsum_small(p):
    R = p.shape[0]

    def body(p_ref, o_ref, buf, send, recv):
        x, y, c = lax.axis_index("x"), lax.axis_index("y"), lax.axis_index("c")
        me = 4 * x + 2 * y + c
        buf[me] = p_ref[...]
        rdmas = []
        for k in range(1, 8):
            peer = (x ^ (k >> 2), y ^ ((k >> 1) & 1), c ^ (k & 1))
            r = pltpu.make_async_remote_copy(src_ref=p_ref, dst_ref=buf.at[me], send_sem=send.at[k - 1],
                                             recv_sem=recv.at[k - 1], device_id=peer, device_id_type=MESH)
            r.start()
            rdmas.append(r)
        for r in rdmas:
            r.wait()
        acc = buf[0]
        for d in range(1, 8):
            acc = acc + buf[d]
        o_ref[...] = acc

    return pl.pallas_call(
        body, name="allsum_small", out_shape=jax.ShapeDtypeStruct((R, 512), F32),
        in_specs=[pl.BlockSpec(memory_space=pltpu.VMEM)], out_specs=pl.BlockSpec(memory_space=pltpu.VMEM),
        scratch_shapes=[pltpu.VMEM((8, R, 512), F32), pltpu.SemaphoreType.DMA((7,)), pltpu.SemaphoreType.DMA((7,))],
        compiler_params=_cp(has_side_effects=True),
    )(p)


def _row_block(rows):
    return max((d for d in range(16, 513, 16) if rows % d == 0), default=rows)


def _sum4(parts, name):
    _, R, C = parts.shape
    tr = _row_block(R)

    def body(p_ref, o_ref):
        p = [p_ref[k].astype(F32) for k in range(4)]
        o_ref[...] = (((p[0] + p[1]) + p[2]) + p[3]).astype(BF)

    return pl.pallas_call(
        body, name=name, grid=(R // tr,), in_specs=[pl.BlockSpec((4, tr, C), lambda i: (0, i, 0))],
        out_specs=pl.BlockSpec((tr, C), lambda i: (i, 0)), out_shape=jax.ShapeDtypeStruct((R, C), BF),
        compiler_params=_cp(("parallel",)),
    )(parts)


def _adamw(w, m, v, g0, g1, name):
    L, R, C = w.shape
    tr = _row_block(R)
    two = g1 is not None
    c1 = 1.0 / (1.0 - ADAM_B1 ** ADAM_STEP)
    c2 = 1.0 / (1.0 - ADAM_B2 ** ADAM_STEP)

    def body(*refs):
        if two:
            w_ref, m_ref, v_ref, a_ref, b_ref, g_ref, d_ref, nm_ref, nv_ref = refs
            g = a_ref[...].astype(F32) + b_ref[...].astype(F32)
        else:
            w_ref, m_ref, v_ref, a_ref, g_ref, d_ref, nm_ref, nv_ref = refs
            g = a_ref[...]
        g_ref[...] = g
        m = ADAM_B1 * m_ref[...] + (1.0 - ADAM_B1) * g
        v = ADAM_B2 * v_ref[...] + (1.0 - ADAM_B2) * (g * g)
        nm_ref[...] = m
        nv_ref[...] = v
        d_ref[...] = -ADAM_LR * ((m * c1) / (jnp.sqrt(v * c2) + ADAM_EPS) + ADAM_WD * w_ref[...])

    spec = pl.BlockSpec((1, tr, C), lambda l, i: (l, i, 0))
    n_in = 5 if two else 4
    ins = (w, m, v, g0, g1) if two else (w, m, v, g0)
    return pl.pallas_call(
        body, name=name, grid=(L, R // tr), in_specs=[spec] * n_in, out_specs=[spec] * 4,
        out_shape=[jax.ShapeDtypeStruct((L, R, C), F32)] * 4, compiler_params=_cp(("parallel", "parallel")),
    )(*ins)


def _pad8(a):
    r = (-a.shape[0]) % 8
    return a if r == 0 else jnp.pad(a, ((0, r), (0, 0)))


def _local_step(x, tgt, meta, P, shards=None, prep=None, pack=None, mats=None):
    seq = x.shape[0]
    Lp = -(-(seq + CHUNK) // TR) * TR
    tail = Lp - seq - CHUNK
    h = jnp.concatenate([jnp.zeros((META_PAD, D_MODEL), F32), meta, x, jnp.zeros((tail, D_MODEL), F32)], axis=0)
    tgt_pad = jnp.pad(tgt, ((CHUNK, tail), (0, 0)))

    P = list(P)
    saved = []
    hn = _rms_fwd(h, P[0]["norm_g"])
    for l in range(DEPTH):
        p = P[l]
        mm = functools.partial(_matmul, tb=True, out_dtype=BF, tm=TR, tk=D_MODEL, col_major_grid=True)
        ug = mm(hn, p["w_g"], tn=N_G // 2, name="inproj_g")
        ua = mm(hn, p["w_a"], tn=N_A, name="inproj_a")
        ub = mm(hn, p["w_b"], tn=N_B, name="inproj_b")
        uc = mm(hn, p["w_c"], tn=N_C, name="inproj_c")
        nxt = shards[l + 1] if shards is not None and l + 1 < DEPTH else None
        carry = (lambda part: ("gather", part)) if nxt is not None else (lambda part: None)
        res_a = _conv_fwd(ua, p["conv_w"], p["conv_vec"], carry(nxt and nxt[1:2]))
        res_b = _hg_fwd(ub, p["lb"], p["gn4"], carry(nxt and nxt[0:1]))
        own = shards[0][2:] if nxt is not None and l == 0 else []
        res_c = _swa_fwd(uc, p["qg"], p["kg"], p["sinks"], carry(nxt and nxt[2:] + own))
        (ya, yconv), (yb, o_hg, s_hg), (yc, o_at, lse) = res_a[:2], res_b[:3], res_c[:3]
        if nxt is not None:
            P.append(prep(l + 1, [*res_b[3:], *res_a[2:], *res_c[3:3 + len(nxt) - 2]]))
            if own:
                p.update(mats(res_c[3 + len(nxt) - 2:]))
        mix = functools.partial(_mix_fwd, h, ya, yb, yc, ug, p["w_ao"], p["w_bo"], p["w_co"], p["w_out"])
        if l + 1 < DEPTH:
            za, zb, zc, mixed, h_new, hn_next = mix(g_next=P[l + 1]["norm_g"])
        else:
            za, zb, zc, mixed, dh, loss8 = mix(loss=(tgt_pad, seq))
        saved.append(dict(h=h, hn=hn, ug=ug, ua=ua, ub=ub, uc=uc, ya=ya, yconv=yconv, yb=yb, o_hg=o_hg, s_hg=s_hg,
                          yc=yc, o_at=o_at, lse=lse, za=za, zb=zb, zc=zc, mixed=mixed))
        if l + 1 < DEPTH:
            h, hn = h_new, hn_next

    grads = [None] * DEPTH
    parts = [[None, None] for _ in range(DEPTH)]
    pending = None
    tk_dw = 2 * TR if Lp % (2 * TR) == 0 else TR
    for l in reversed(range(DEPTH)):
        p, s = P[l], saved[l]
        dug, dza, dzb, dzc, dya, dyb, dyc = _mix_bwd(dh, s["za"], s["zb"], s["zc"], s["ug"],
                                                      p["w_ao"], p["w_bo"], p["w_co"], p["w_out"])
        tnmm = functools.partial(_matmul, ta=True, out_dtype=F32, tk=tk_dw)
        g = {}
        g["w_out"] = tnmm(s["mixed"], dh, tm=D_MODEL, tn=D_MODEL, name="dw_out")
        g["w_ao"] = tnmm(s["ya"], dza, tm=512, tn=D_MODEL, name="dw_ao")
        g["w_bo"] = tnmm(s["yb"], dzb, tm=512, tn=D_MODEL, name="dw_bo")
        g["w_co"] = tnmm(s["yc"], dzc, tm=512, tn=D_MODEL, name="dw_co")
        dua, g["conv_w"], g["conv_vec"] = _conv_bwd(s["ua"], s["yconv"], dya, p["conv_w"], p["conv_vec"])
        carry = ("scatter", pending[1]) if pending is not None else None
        res = _hg_bwd(s["ub"], p["lb"], p["gn4"], s["o_hg"], s["s_hg"], dyb, carry)
        dub, g["hg_small"] = res[:2]
        if carry is not None:
            parts[pending[0]][1] = res[2:]
        duc, g["at_gain"], g["at_sink"] = _swa_bwd(s["uc"], p["qg"], p["kg"], p["sinks"], s["o_at"], s["lse"], dyc)
        g["w_g"] = tnmm(dug, s["hn"], tm=N_G // 2, tn=D_MODEL, name="dw_in_g")
        g["w_a"] = tnmm(dua, s["hn"], tm=N_A, tn=D_MODEL, name="dw_in_a")
        g["w_b"] = tnmm(dub, s["hn"], tm=N_B, tn=D_MODEL, name="dw_in_b")
        g["w_c"] = tnmm(duc, s["hn"], tm=N_C, tn=D_MODEL, name="dw_in_c")
        first, second = pack(g) if pack is not None else (None, None)
        if first is not None and l == 0:
            first, second = first + second, []
        res = _inproj_bwd([dug, dua, dub, duc], [p["w_g"], p["w_a"], p["w_b"], p["w_c"]], s["h"], dh, p["norm_g"],
                          ("scatter", first) if first is not None else None)
        dh, g["norm_g"] = res[:2]
        grads[l] = g
        if pack is not None:
            parts[l] = [res[2:3], res[3:]] if l == 0 else [res[2:], None]
            pending = (l, second) if l > 0 else None
    return loss8, dh, grads, parts


def _split_w_in(wt):
    return dict(w_a=wt[0:1536], w_b=wt[1536:3584],
                w_c=jnp.concatenate([wt[3584:4096], wt[4352:4864], wt[4096:4352]], axis=0), w_g=wt[4864:7936])


def _join_w_in(g):
    c = g["w_c"]
    return jnp.concatenate([g["w_a"], g["w_b"], c[0:512], c[1024:1280], c[512:1024], g["w_g"]], axis=0)


def _attn_small(g):
    return (g["at_gain"][0].reshape(ATT_Q_HEADS, ATT_HD).sum(0),
            g["at_gain"][1, 0:128].reshape(ATT_KV_HEADS, ATT_HD).sum(0), g["at_sink"].sum(1))


_SMALL = (("norm_g", 8), ("meta", 32), ("conv_w", 32 * DEPTH), ("conv_b", 8), ("conv_ln_g", 8), ("conv_ln_b", 8),
          ("lb", 8), ("hg_norm_g", 8), ("q_norm_g", 8), ("k_norm_g", 8), ("sinks", 8))


def _small_offsets():
    off, o = {}, 0
    for name, rows in _SMALL:
        off[name] = (o, rows)
        o += rows
    return off, o


def _pack_small(d):
    parts = []
    for name, rows in _SMALL:
        a = d[name]
        parts.append(jnp.pad(a, ((0, rows - a.shape[0]), (0, 512 - a.shape[1]))))
    return jnp.concatenate(parts, axis=0)


def kernel(x, meta_tokens, norm_g, w_in, conv_w, conv_b, conv_ln_g, conv_ln_b, w_conv_out, hg_lower_bounds, hg_norm_g, w_hg_out, q_norm_g, k_norm_g, attn_sinks, w_att_out, w_out, loss_target, m_meta_tokens, m_norm_g, m_w_in, m_conv_w, m_conv_b, m_conv_ln_g, m_conv_ln_b, m_w_conv_out, m_hg_lower_bounds, m_hg_norm_g, m_w_hg_out, m_q_norm_g, m_k_norm_g, m_attn_sinks, m_w_att_out, m_w_out, v_meta_tokens, v_norm_g, v_w_in, v_conv_w, v_conv_b, v_conv_ln_g, v_conv_ln_b, v_w_conv_out, v_hg_lower_bounds, v_hg_norm_g, v_w_hg_out, v_q_norm_g, v_k_norm_g, v_attn_sinks, v_w_att_out, v_w_out):
    xi, yi = lax.axis_index("x"), lax.axis_index("y")
    chip = 2 * xi + yi
    NS = w_in.shape[2]
    CS = conv_w.shape[2]
    MS = meta_tokens.shape[1]

    half = NS // 2
    w_in_t, m_w_in_t, v_w_in_t = (jnp.swapaxes(t, 1, 2) for t in (w_in, m_w_in, v_w_in))
    shards = [[w_in_t[l, :half].astype(BF), w_in_t[l, half:].astype(BF), w_conv_out[l].astype(BF),
               w_hg_out[l].astype(BF), w_att_out[l].astype(BF), w_out[l].astype(BF)] for l in range(DEPTH)]
    *first, g_meta, g_convw = _gather_first(shards[0][:2] + [meta_tokens, conv_w.reshape(DEPTH * CONV_WIDTH, CS)],
                                            [True, True, False, False])
    cols = lambda g: g.transpose(1, 0, 2).reshape(g.shape[1], -1)
    meta_f = cols(g_meta)
    convw_f = cols(g_convw).reshape(DEPTH, CONV_WIDTH, D_CONV)
    lb_all = _lb_fwd(hg_lower_bounds)

    def mats(gathered):
        g_wao, g_wbo, g_wco, g_wout = gathered
        return dict(w_ao=cols(g_wao), w_bo=cols(g_wbo), w_co=cols(g_wco), w_out=g_wout.reshape(D_MODEL, D_MODEL))

    def prep(l, gathered):
        p = _split_w_in(jnp.concatenate(gathered[:2], axis=1).reshape(4 * NS, D_MODEL))
        if len(gathered) > 2:
            p.update(mats(gathered[2:]))
        p.update(norm_g=norm_g[l:l + 1], conv_w=convw_f[l],
                 conv_vec=_pad8(jnp.stack([conv_b[l], conv_ln_g[l], conv_ln_b[l]])),
                 lb=lb_all[l:l + 1], gn4=jnp.tile(hg_norm_g[l:l + 1], (1, HG_HEADS)),
                 qg=jnp.tile(q_norm_g[l:l + 1], (1, ATT_Q_HEADS)), kg=jnp.tile(k_norm_g[l:l + 1], (1, ATT_KV_HEADS)),
                 sinks=attn_sinks[l:l + 1])
        return p

    shard_cols = lambda a: a.reshape(a.shape[0], 4, -1).transpose(1, 0, 2)
    def pack(g):
        win = _join_w_in(g).reshape(4, NS, D_MODEL).astype(BF)
        return [win[:, :half]], [win[:, half:], shard_cols(g["w_ao"]).astype(BF), shard_cols(g["w_bo"]).astype(BF),
                                 shard_cols(g["w_co"]).astype(BF), g["w_out"].reshape(4, MS, D_MODEL).astype(BF)]

    loss8, dh0, grads, parts = _local_step(x[0], loss_target[0], meta_f, [prep(0, first)], shards, prep, pack, mats)
    seq = x.shape[1]
    grad_x = dh0[CHUNK:CHUNK + seq][None]
    loss = lax.psum(loss8[0, 0], ("x", "y", "c"))

    sum4 = functools.partial(_sum4, name="sum_chips")
    mine = [jnp.concatenate([t for l in range(DEPTH) for t in (sum4(parts[l][0][0]), sum4(parts[l][1][0]))], axis=0)]
    mine += [jnp.concatenate([sum4(parts[l][1][a]) for l in range(DEPTH)], axis=0) for a in range(1, 5)]
    theirs = _swap_cores(mine)

    dlb_all = jnp.concatenate([grads[l]["hg_small"][0:1] for l in range(DEPTH)], axis=0)
    small = dict(
        norm_g=jnp.concatenate([grads[l]["norm_g"][0:1] for l in range(DEPTH)], axis=0).reshape(8, 512),
        meta=dh0[META_PAD:CHUNK].reshape(32, 512),
        conv_w=jnp.concatenate([grads[l]["conv_w"] for l in range(DEPTH)], axis=0),
        conv_b=jnp.concatenate([grads[l]["conv_vec"][0:1] for l in range(DEPTH)], axis=0),
        conv_ln_g=jnp.concatenate([grads[l]["conv_vec"][1:2] for l in range(DEPTH)], axis=0),
        conv_ln_b=jnp.concatenate([grads[l]["conv_vec"][2:3] for l in range(DEPTH)], axis=0),
        lb=_lb_bwd(hg_lower_bounds, dlb_all),
        hg_norm_g=jnp.concatenate([grads[l]["hg_small"][1:2].reshape(HG_HEADS, HG_D).sum(0, keepdims=True)
                                   for l in range(DEPTH)], axis=0),
        q_norm_g=jnp.stack([_attn_small(grads[l])[0] for l in range(DEPTH)]),
        k_norm_g=jnp.stack([_attn_small(grads[l])[1] for l in range(DEPTH)]),
        sinks=jnp.stack([_attn_small(grads[l])[2] for l in range(DEPTH)]),
    )
    gsum = _allsum_small(_pack_small(small))
    off, _ = _small_offsets()

    def take(name, rows, cols):
        o, _ = off[name]
        return gsum[o:o + rows, 0:cols]

    g_meta_full = take("meta", 32, 512).reshape(N_META, D_MODEL)
    g_convw_full = take("conv_w", 32 * DEPTH, 512).reshape(DEPTH, 32, 512)[:, :CONV_WIDTH]
    small_grads = dict(
        norm_g=take("norm_g", 8, 512),
        meta=lax.dynamic_slice_in_dim(g_meta_full, chip * MS, MS, axis=1),
        conv_w=lax.dynamic_slice_in_dim(g_convw_full, chip * CS, CS, axis=2).reshape(DEPTH * CONV_WIDTH, CS),
        conv_b=take("conv_b", DEPTH, 512), conv_ln_g=take("conv_ln_g", DEPTH, 512), conv_ln_b=take("conv_ln_b", DEPTH, 512),
        lb=take("lb", DEPTH, 512), hg_norm_g=take("hg_norm_g", DEPTH, HG_D), q_norm_g=take("q_norm_g", DEPTH, ATT_HD),
        k_norm_g=take("k_norm_g", DEPTH, ATT_HD), sinks=take("sinks", DEPTH, ATT_Q_HEADS))

    def big_update(w, m, v, a, b, name):
        return _adamw(w, m, v, a.reshape(w.shape), b.reshape(w.shape), name)

    res = {}
    res["w_in"] = [jnp.swapaxes(t, 1, 2) for t in big_update(w_in_t, m_w_in_t, v_w_in_t, mine[0], theirs[0], "adamw_w_in")]
    res["w_conv_out"] = big_update(w_conv_out, m_w_conv_out, v_w_conv_out, mine[1], theirs[1], "adamw_w_ao")
    res["w_hg_out"] = big_update(w_hg_out, m_w_hg_out, v_w_hg_out, mine[2], theirs[2], "adamw_w_bo")
    res["w_att_out"] = big_update(w_att_out, m_w_att_out, v_w_att_out, mine[3], theirs[3], "adamw_w_co")
    res["w_out"] = big_update(w_out, m_w_out, v_w_out, mine[4], theirs[4], "adamw_w_out")

    small_w = dict(meta=(meta_tokens, m_meta_tokens, v_meta_tokens), norm_g=(norm_g, m_norm_g, v_norm_g),
                   conv_w=(conv_w, m_conv_w, v_conv_w), conv_b=(conv_b, m_conv_b, v_conv_b),
                   conv_ln_g=(conv_ln_g, m_conv_ln_g, v_conv_ln_g), conv_ln_b=(conv_ln_b, m_conv_ln_b, v_conv_ln_b),
                   lb=(hg_lower_bounds, m_hg_lower_bounds, v_hg_lower_bounds),
                   hg_norm_g=(hg_norm_g, m_hg_norm_g, v_hg_norm_g), q_norm_g=(q_norm_g, m_q_norm_g, v_q_norm_g),
                   k_norm_g=(k_norm_g, m_k_norm_g, v_k_norm_g), sinks=(attn_sinks, m_attn_sinks, v_attn_sinks))
    view = lambda n, t: t.reshape(-1, 512) if n == "norm_g" else t.reshape(-1, t.shape[-1])
    pw, pm, pv = (_pack_rows([view(n, small_w[n][k]) for n in small_w]) for k in range(3))
    pg = _pack_rows([small_grads[n] for n in small_w])
    packed = [t[0] for t in _adamw(pw[None], pm[None], pv[None], pg[None], None, "adamw_small")]
    o = 0
    for n in small_w:
        r, cdim = view(n, small_w[n][0]).shape
        res[n] = [t[o:o + r, 0:cdim].reshape(small_w[n][0].shape) for t in packed]
        o += -(-r // 8) * 8

    order = [("meta", None), ("norm_g", None), ("w_in", None), ("conv_w", None), ("conv_b", None), ("conv_ln_g", None),
             ("conv_ln_b", None), ("w_conv_out", None), ("lb", None), ("hg_norm_g", None), ("w_hg_out", None),
             ("q_norm_g", None), ("k_norm_g", None), ("sinks", None), ("w_att_out", None), ("w_out", None)]
    outs = [loss, grad_x]
    for k in range(4):
        outs += [res[n][k] for n, _ in order]
    return tuple(outs)


def _pack_rows(arrs):
    parts = []
    for a in arrs:
        r = (-a.shape[0]) % 8
        parts.append(jnp.pad(a, ((0, r), (0, 512 - a.shape[1]))))
    return jnp.concatenate(parts, axis=0)
```

```python
import functools

import jax
import jax.numpy as jnp
from jax import lax
from jax.experimental import pallas as pl
from jax.experimental.pallas import tpu as pltpu

F32 = jnp.float32
BF = jnp.bfloat16

D_MODEL = 1024
DEPTH = 4
CHUNK = 64
N_META = 16
META_PAD = CHUNK - N_META
D_CONV = 512
CONV_WIDTH = 31
HG_HEADS = 4
HG_D = 128
ATT_Q_HEADS = 8
ATT_KV_HEADS = 2
ATT_HD = 64
ATT_GROUP = ATT_Q_HEADS // ATT_KV_HEADS
EPS = 1e-6
F_FLOOR = 1e-30
NEG = -1e30

ADAM_LR = 0.001
ADAM_B1 = 0.9
ADAM_B2 = 0.999
ADAM_EPS = 1e-08
ADAM_WD = 0.01
ADAM_STEP = 10

TR = 640
TRM = TR // 2
CONV_RB = 32
QB = 128
HALO = 128
VMEM_LIMIT = 56 * 1024 * 1024

N_G, N_A, N_B, N_C = 3 * D_MODEL, 3 * D_CONV, 4 * 512, 2 * 512 + 2 * 128

MESH = pl.DeviceIdType.MESH


def _cp(sem=None, vmem=VMEM_LIMIT, **kw):
    if sem is None:
        return pltpu.CompilerParams(vmem_limit_bytes=vmem, **kw)
    return pltpu.CompilerParams(dimension_semantics=sem, vmem_limit_bytes=vmem, **kw)


def _nn(a, b):
    return lax.dot_general(a, b, (((1,), (0,)), ((), ())), preferred_element_type=F32)


def _nt(a, b):
    return lax.dot_general(a, b, (((1,), (1,)), ((), ())), preferred_element_type=F32)


def _tn(a, b):
    return lax.dot_general(a, b, (((0,), (0,)), ((), ())), preferred_element_type=F32)


def _sig(x):
    return jax.nn.sigmoid(x)


def _silu(x):
    return x * _sig(x)


def _silu_pair(x):
    s = _sig(x)
    return x * s, s * (1.0 + x * (1.0 - s))


def _mm_split(t, x):
    hi = x.astype(BF)
    lo = (x - hi.astype(F32)).astype(BF)
    return _nn(t, hi) + _nn(t, lo)


def _chunk_tri(n, upper):
    r = lax.broadcasted_iota(jnp.int32, (n, n), 0)
    c = lax.broadcasted_iota(jnp.int32, (n, n), 1)
    same = jnp.right_shift(r, 6) == jnp.right_shift(c, 6)
    tri = (c >= r) if upper else (c <= r)
    return jnp.where(same & tri, 1.0, 0.0).astype(BF)


def _matmul(a, b, *, ta=False, tb=False, out_dtype, tm, tn, tk, name, col_major_grid=False):
    if ta:
        K, M = a.shape
    else:
        M, K = a.shape
    N = b.shape[0] if tb else b.shape[1]
    assert M % tm == 0 and N % tn == 0 and K % tk == 0, (name, M, N, K, tm, tn, tk)
    nk = K // tk
    if col_major_grid:
        grid = (N // tn, M // tm, nk)
        ij = lambda g0, g1: (g1, g0)
    else:
        grid = (M // tm, N // tn, nk)
        ij = lambda g0, g1: (g0, g1)
    if ta:
        a_spec = pl.BlockSpec((tk, tm), lambda g0, g1, k: (k, ij(g0, g1)[0]))
    else:
        a_spec = pl.BlockSpec((tm, tk), lambda g0, g1, k: (ij(g0, g1)[0], k))
    if tb:
        b_spec = pl.BlockSpec((tn, tk), lambda g0, g1, k: (ij(g0, g1)[1], k))
    else:
        b_spec = pl.BlockSpec((tk, tn), lambda g0, g1, k: (k, ij(g0, g1)[1]))
    o_spec = pl.BlockSpec((tm, tn), lambda g0, g1, k: ij(g0, g1))
    dims = (((0 if ta else 1,), (1 if tb else 0,)), ((), ()))
    use_acc = nk > 1 and out_dtype != F32

    def body(a_ref, b_ref, o_ref, *scr):
        k = pl.program_id(2)
        p = lax.dot_general(a_ref[...].astype(BF), b_ref[...].astype(BF), dims, preferred_element_type=F32)
        if nk == 1:
            o_ref[...] = p.astype(out_dtype)
        else:
            acc = scr[0] if use_acc else o_ref

            @pl.when(k == 0)
            def _():
                acc[...] = p

            @pl.when(k > 0)
            def _():
                acc[...] += p

            if use_acc:
                @pl.when(k == nk - 1)
                def _():
                    o_ref[...] = acc[...].astype(out_dtype)

    return pl.pallas_call(
        body, name=name, grid=grid, in_specs=[a_spec, b_spec], out_specs=o_spec,
        out_shape=jax.ShapeDtypeStruct((M, N), out_dtype),
        scratch_shapes=[pltpu.VMEM((tm, tn), F32)] if use_acc else [],
        compiler_params=_cp(("parallel", "parallel", "arbitrary")),
    )(a, b)


def _dw_branches(ys, dzs, tk):
    Lp = ys[0].shape[0]

    def body(ya_ref, yb_ref, yc_ref, da_ref, db_ref, dc_ref, oa_ref, ob_ref, oc_ref):
        k = pl.program_id(0)
        for y_ref, d_ref, o_ref in ((ya_ref, da_ref, oa_ref), (yb_ref, db_ref, ob_ref), (yc_ref, dc_ref, oc_ref)):
            p = _tn(y_ref[...], d_ref[...])

            @pl.when(k == 0)
            def _():
                o_ref[...] = p

            @pl.when(k > 0)
            def _():
                o_ref[...] += p

    yspec = pl.BlockSpec((tk, 512), lambda k: (k, 0))
    dspec = pl.BlockSpec((tk, D_MODEL), lambda k: (k, 0))
    ospec = pl.BlockSpec((512, D_MODEL), lambda k: (0, 0))
    return pl.pallas_call(
        body, name="dw_branches", grid=(Lp // tk,), in_specs=[yspec] * 3 + [dspec] * 3, out_specs=[ospec] * 3,
        out_shape=[jax.ShapeDtypeStruct((512, D_MODEL), F32)] * 3, compiler_params=_cp(("arbitrary",)),
    )(*ys, *dzs)


def _rms_fwd(h, g):
    Lp = h.shape[0]

    def body(h_ref, g_ref, o_ref):
        x = h_ref[...]
        r = lax.rsqrt(jnp.mean(x * x, axis=-1, keepdims=True) + EPS)
        o_ref[...] = (x * r * g_ref[...]).astype(BF)

    return pl.pallas_call(
        body, name="rms_fwd", grid=(Lp // TR,),
        in_specs=[pl.BlockSpec((TR, D_MODEL), lambda i: (i, 0)), pl.BlockSpec((1, D_MODEL), lambda i: (0, 0))],
        out_specs=pl.BlockSpec((TR, D_MODEL), lambda i: (i, 0)),
        out_shape=jax.ShapeDtypeStruct((Lp, D_MODEL), BF),
        compiler_params=_cp(("parallel",)),
    )(h, g)


def _glu(ua, row):
    a = ua[:, 0:D_CONV].astype(F32)
    gl = ua[:, D_CONV:2 * D_CONV].astype(F32)
    return jnp.where(row >= META_PAD, a * _sig(gl), 0.0)


_SH_ROWS = TR + CHUNK - 8


def _fill_shifts(src, sh):
    for b in range(1, 8):
        sh[b - 1] = src[pl.ds(b, _SH_ROWS), :]


def _shifted(src, sh, start, n):
    b = start % 8
    if b == 0:
        return src[pl.ds(start, n), :]
    return sh[b - 1, pl.ds(start - b, n), :]


def _conv_fwd(ua, cw, cvec, carry=None):
    Lp = ua.shape[0]
    nt = Lp // TR
    hb = TR // CHUNK

    def body(cur_ref, halo_ref, w_ref, v_ref, ya_ref, yc_ref, ext, sh):
        i = pl.program_id(0)
        row = i * TR + lax.broadcasted_iota(jnp.int32, (TR, 1), 0)
        hrow = i * TR - CHUNK + lax.broadcasted_iota(jnp.int32, (CHUNK, 1), 0)
        ext[pl.ds(0, CHUNK), :] = jnp.where(i > 0, _glu(halo_ref[...], hrow), 0.0)
        ext[pl.ds(CHUNK, TR), :] = _glu(cur_ref[...], row)
        _fill_shifts(ext, sh)
        for rb in range(TR // CONV_RB):
            r0 = rb * CONV_RB
            rows = pl.ds(r0, CONV_RB)
            acc = jnp.zeros((CONV_RB, D_CONV), F32)
            for j in range(CONV_WIDTH):
                acc = acc + _shifted(ext, sh, r0 + CHUNK - (CONV_WIDTH - 1) + j, CONV_RB) * w_ref[j:j + 1, :]
            y = acc + v_ref[0:1, :]
            yc_ref[rows, :] = y
            mu = jnp.mean(y, axis=-1, keepdims=True)
            d = y - mu
            var = jnp.mean(d * d, axis=-1, keepdims=True)
            yn = d * lax.rsqrt(var + EPS) * v_ref[1:2, :] + v_ref[2:3, :]
            ya_ref[rows, :] = (_silu(yn) * _silu(cur_ref[rows, 2 * D_CONV:3 * D_CONV].astype(F32))).astype(BF)

    in_specs = [pl.BlockSpec((TR, N_A), lambda i: (i, 0)),
                pl.BlockSpec((CHUNK, N_A), lambda i: (jnp.maximum(i * hb - 1, 0), 0)),
                pl.BlockSpec((CONV_WIDTH, D_CONV), lambda i: (0, 0)),
                pl.BlockSpec((8, D_CONV), lambda i: (0, 0))]
    out_specs = [pl.BlockSpec((TR, D_CONV), lambda i: (i, 0)), pl.BlockSpec((TR, D_CONV), lambda i: (i, 0))]
    out_shape = [jax.ShapeDtypeStruct((Lp, D_CONV), BF), jax.ShapeDtypeStruct((Lp, D_CONV), F32)]
    scratch = [pltpu.VMEM((TR + CHUNK, D_CONV), F32), pltpu.VMEM((7, _SH_ROWS, D_CONV), F32)]
    return _call_carrying(body, "conv_fwd", nt, in_specs, out_specs, out_shape, scratch, (ua, ua, cw, cvec), carry)


def _conv_bwd(ua, yconv, dya, cw, cvec):
    Lp = ua.shape[0]
    nt = Lp // TR
    hb = TR // CHUNK
    nhb = Lp // CHUNK

    def ln_bwd(y, dout, gate, v_ref):
        mu = jnp.mean(y, axis=-1, keepdims=True)
        d = y - mu
        var = jnp.mean(d * d, axis=-1, keepdims=True)
        rstd = lax.rsqrt(var + EPS)
        xhat = d * rstd
        yn = xhat * v_ref[1:2, :] + v_ref[2:3, :]
        s_gate, ds_gate = _silu_pair(gate)
        s_yn, ds_yn = _silu_pair(yn)
        dyn = dout * s_gate * ds_yn
        dxh = dyn * v_ref[1:2, :]
        dyc = rstd * (dxh - jnp.mean(dxh, axis=-1, keepdims=True) - xhat * jnp.mean(dxh * xhat, axis=-1, keepdims=True))
        return dyc, dyn, xhat, dout * s_yn * ds_gate

    def body(cur_ref, prev_ref, next_ref, yc_ref, ycn_ref, dy_ref, dyn_ref, w_ref, v_ref,
             du_ref, dw_ref, dv_ref, uext, dext, dwacc, ush, dsh):
        i = pl.program_id(0)

        @pl.when(i == 0)
        def _():
            dwacc[...] = jnp.zeros_like(dwacc)
            dv_ref[...] = jnp.zeros_like(dv_ref)

        row = i * TR + lax.broadcasted_iota(jnp.int32, (TR, 1), 0)
        hrow = i * TR - CHUNK + lax.broadcasted_iota(jnp.int32, (CHUNK, 1), 0)
        uext[pl.ds(0, CHUNK), :] = jnp.where(i > 0, _glu(prev_ref[...], hrow), 0.0)
        uext[pl.ds(CHUNK, TR), :] = _glu(cur_ref[...], row)

        s_b = jnp.zeros((1, D_CONV), F32)
        s_g = jnp.zeros((1, D_CONV), F32)
        s_bb = jnp.zeros((1, D_CONV), F32)
        for rb in range(TR // CONV_RB):
            rows = pl.ds(rb * CONV_RB, CONV_RB)
            gate = cur_ref[rows, 2 * D_CONV:3 * D_CONV].astype(F32)
            dout = dy_ref[rows, :].astype(F32)
            dyc, dyn, xhat, dgate = ln_bwd(yc_ref[rows, :], dout, gate, v_ref)
            du_ref[rows, 2 * D_CONV:3 * D_CONV] = dgate.astype(BF)
            dext[rows, :] = dyc
            s_b = s_b + jnp.sum(dyc, axis=0, keepdims=True)
            s_g = s_g + jnp.sum(dyn * xhat, axis=0, keepdims=True)
            s_bb = s_bb + jnp.sum(dyn, axis=0, keepdims=True)
        dv_ref[0:1, :] += s_b
        dv_ref[1:2, :] += s_g
        dv_ref[2:3, :] += s_bb
        dyc_n, _, _, _ = ln_bwd(ycn_ref[...], dyn_ref[...].astype(F32),
                                next_ref[:, 2 * D_CONV:3 * D_CONV].astype(F32), v_ref)
        dext[pl.ds(TR, CHUNK), :] = jnp.where(i < nt - 1, dyc_n, 0.0)
        _fill_shifts(uext, ush)
        _fill_shifts(dext, dsh)

        for rb in range(TR // CONV_RB):
            r0 = rb * CONV_RB
            rows = pl.ds(r0, CONV_RB)
            d_blk = dext[rows, :]
            dglu = jnp.zeros((CONV_RB, D_CONV), F32)
            for j in range(CONV_WIDTH):
                dglu = dglu + _shifted(dext, dsh, r0 + CONV_WIDTH - 1 - j, CONV_RB) * w_ref[j:j + 1, :]
                prod = d_blk * _shifted(uext, ush, r0 + CHUNK - (CONV_WIDTH - 1) + j, CONV_RB)
                part = prod[0:8, :]
                for s in range(1, CONV_RB // 8):
                    part = part + prod[8 * s:8 * s + 8, :]
                dwacc[j] += part
            a = cur_ref[rows, 0:D_CONV].astype(F32)
            sg = _sig(cur_ref[rows, D_CONV:2 * D_CONV].astype(F32))
            grow = i * TR + r0 + lax.broadcasted_iota(jnp.int32, (CONV_RB, 1), 0)
            dglu = jnp.where(grow >= META_PAD, dglu, 0.0)
            du_ref[rows, 0:D_CONV] = (dglu * sg).astype(BF)
            du_ref[rows, D_CONV:2 * D_CONV] = (dglu * a * sg * (1.0 - sg)).astype(BF)

        @pl.when(i == nt - 1)
        def _():
            dw_ref[...] = jnp.sum(dwacc[...], axis=1)

    nxt = lambda i: (jnp.minimum(i * hb + hb, nhb - 1), 0)
    return pl.pallas_call(
        body, name="conv_bwd", grid=(nt,),
        in_specs=[pl.BlockSpec((TR, N_A), lambda i: (i, 0)),
                  pl.BlockSpec((CHUNK, N_A), lambda i: (jnp.maximum(i * hb - 1, 0), 0)),
                  pl.BlockSpec((CHUNK, N_A), nxt),
                  pl.BlockSpec((TR, D_CONV), lambda i: (i, 0)),
                  pl.BlockSpec((CHUNK, D_CONV), nxt),
                  pl.BlockSpec((TR, D_CONV), lambda i: (i, 0)),
                  pl.BlockSpec((CHUNK, D_CONV), nxt),
                  pl.BlockSpec((CONV_WIDTH, D_CONV), lambda i: (0, 0)),
                  pl.BlockSpec((8, D_CONV), lambda i: (0, 0))],
        out_specs=[pl.BlockSpec((TR, N_A), lambda i: (i, 0)),
                   pl.BlockSpec((32, D_CONV), lambda i: (0, 0)),
                   pl.BlockSpec((8, D_CONV), lambda i: (0, 0))],
        out_shape=[jax.ShapeDtypeStruct((Lp, N_A), BF), jax.ShapeDtypeStruct((32, D_CONV), F32),
                   jax.ShapeDtypeStruct((8, D_CONV), F32)],
        scratch_shapes=[pltpu.VMEM((TR + CHUNK, D_CONV), F32), pltpu.VMEM((TR + CHUNK, D_CONV), F32),
                        pltpu.VMEM((32, 8, D_CONV), F32), pltpu.VMEM((7, _SH_ROWS, D_CONV), F32),
                        pltpu.VMEM((7, _SH_ROWS, D_CONV), F32)],
        compiler_params=_cp(("arbitrary",)),
    )(ua, ua, ua, yconv, yconv, dya, dya, cw, cvec)


def _hg_gates(ub_ref, lbv, row):
    q = ub_ref[:, 0:512].astype(F32)
    z = ub_ref[:, 512:1024].astype(F32)
    valid = row >= META_PAD
    sig = _sig(z)
    f = lbv + (1.0 - lbv) * sig
    g = jnp.where(valid, jnp.log(jnp.maximum(f, F_FLOOR)), 0.0)
    k = jnp.where(valid, (1.0 - lbv) * (1.0 - sig), 0.0)
    return q, k, g, sig, f


def _hg_chunk_terms(b_c, q_c, k_c):
    bm = b_c[CHUNK // 2 - 1:CHUNK // 2, :]
    bl = b_c[CHUNK - 1:CHUNK, :]
    e1 = jnp.exp(b_c - bm)
    e2 = jnp.exp(bm - b_c)
    e0 = jnp.exp(b_c)
    e3 = jnp.exp(bl - b_c)
    el = jnp.exp(bl)
    return e1, e2, e0, e3, el, q_c * e1, k_c * e2, q_c * e0, k_c * e3


def _hg_fwd(ub, lb, gn4, carry=None):
    Lp = ub.shape[0]
    nt = Lp // TR
    cpt = TR // CHUNK

    def body(ub_ref, lb_ref, gn_ref, yb_ref, o_ref, ss_ref, st, bsc, qsc, ksc, qes, els, ust, tlo):
        i = pl.program_id(0)

        @pl.when(i == 0)
        def _():
            st[...] = jnp.zeros_like(st)
            tlo[...] = _chunk_tri(TR, False)

        row = i * TR + lax.broadcasted_iota(jnp.int32, (TR, 1), 0)
        q, k, g, _, _ = _hg_gates(ub_ref, lb_ref[...], row)
        qsc[...] = _silu(q)
        ksc[...] = k
        bsc[...] = _mm_split(tlo[...], g)
        tri = lax.broadcasted_iota(jnp.int32, (CHUNK, CHUNK), 1) <= lax.broadcasted_iota(jnp.int32, (CHUNK, CHUNK), 0)

        def intra(c, carry):
            rows = pl.ds(pl.multiple_of(c * CHUNK, CHUNK), CHUNK)
            _, _, _, _, el, qe, ke, qE, kd = _hg_chunk_terms(bsc[rows, :], qsc[rows, :], ksc[rows, :])
            qe, ke, kd = qe.astype(BF), ke.astype(BF), kd.astype(BF)
            qes[rows, :] = qE.astype(BF)
            els[c] = jnp.broadcast_to(el, (8, 512))
            sls = [slice(HG_D * h, HG_D * (h + 1)) for h in range(HG_HEADS)]
            v = [ub_ref[rows, 1024 + HG_D * h:1024 + HG_D * (h + 1)] for h in range(HG_HEADS)]
            a = [_nt(qe[:, sl], ke[:, sl]) for sl in sls]
            u = [_tn(v[h], kd[:, sls[h]]) for h in range(HG_HEADS)]
            a = [jnp.where(tri, x, 0.0).astype(BF) for x in a]
            oi = [_nn(a[h], v[h]) for h in range(HG_HEADS)]
            for h in range(HG_HEADS):
                ust[c, h] = u[h]
                o_ref[rows, sls[h]] = oi[h]
            return carry

        lax.fori_loop(0, cpt, intra, 0, unroll=2)

        for h in range(HG_HEADS):
            sl = slice(HG_D * h, HG_D * (h + 1))
            s = st[h]
            for c in range(cpt):
                ss_ref[c, h] = s
                s = els[c, 0:1, sl] * s + ust[c, h]
            st[h] = s

        def inter(c, carry):
            rows = pl.ds(pl.multiple_of(c * CHUNK, CHUNK), CHUNK)
            for h in range(HG_HEADS):
                sl = slice(HG_D * h, HG_D * (h + 1))
                o_ref[rows, sl] += _nt(qes[rows, sl], ss_ref[c, h].astype(BF))
            return carry

        lax.fori_loop(0, cpt, inter, 0, unroll=2)

        gate = ub_ref[:, 1536:2048].astype(F32)
        for h in range(HG_HEADS):
            sl = slice(HG_D * h, HG_D * (h + 1))
            o = o_ref[:, sl]
            r = lax.rsqrt(jnp.mean(o * o, axis=-1, keepdims=True) + EPS)
            yb_ref[:, sl] = (o * r * gn_ref[:, sl] * _silu(gate[:, sl])).astype(BF)

    in_specs = [pl.BlockSpec((TR, N_B), lambda i: (i, 0)), pl.BlockSpec((1, 512), lambda i: (0, 0)),
                pl.BlockSpec((1, 512), lambda i: (0, 0))]
    out_specs = [pl.BlockSpec((TR, 512), lambda i: (i, 0)), pl.BlockSpec((TR, 512), lambda i: (i, 0)),
                 pl.BlockSpec((cpt, HG_HEADS, HG_D, HG_D), lambda i: (i, 0, 0, 0))]
    out_shape = [jax.ShapeDtypeStruct((Lp, 512), BF), jax.ShapeDtypeStruct((Lp, 512), F32),
                 jax.ShapeDtypeStruct((Lp // CHUNK, HG_HEADS, HG_D, HG_D), F32)]
    scratch = [pltpu.VMEM((HG_HEADS, HG_D, HG_D), F32), pltpu.VMEM((TR, 512), F32),
               pltpu.VMEM((TR, 512), F32), pltpu.VMEM((TR, 512), F32), pltpu.VMEM((TR, 512), BF),
               pltpu.VMEM((cpt, 8, 512), F32), pltpu.VMEM((cpt, HG_HEADS, HG_D, HG_D), F32), pltpu.VMEM((TR, TR), BF)]
    return _call_carrying(body, "hgrn_fwd", nt, in_specs, out_specs, out_shape, scratch, (ub, lb, gn4), carry)


def _hg_bwd(ub, lb, gn4, o_save, s_save, dyb, carry=None):
    Lp = ub.shape[0]
    nt = Lp // TR
    cpt = TR // CHUNK

    def body(ub_ref, lb_ref, gn_ref, o_ref, ss_ref, dy_ref, du_ref, ds_ref,
             dst, bsc, qsc, ksc, dosc, dqsc, dksc, dbsc, els, ust, dss, tlo, tup):
        i = pl.program_id(0)
        t = nt - 1 - i

        @pl.when(i == 0)
        def _():
            dst[...] = jnp.zeros_like(dst)
            ds_ref[...] = jnp.zeros_like(ds_ref)
            tlo[...] = _chunk_tri(TR, False)
            tup[...] = _chunk_tri(TR, True)

        lbv = lb_ref[...]
        row = t * TR + lax.broadcasted_iota(jnp.int32, (TR, 1), 0)
        valid = row >= META_PAD
        q, k, g, sig, f = _hg_gates(ub_ref, lbv, row)
        silu_q, dsilu_q = _silu_pair(q)
        qsc[...] = silu_q
        ksc[...] = k
        bsc[...] = _mm_split(tlo[...], g)

        gate = ub_ref[:, 1536:2048].astype(F32)
        dy = dy_ref[...].astype(F32)
        for h in range(HG_HEADS):
            sl = slice(HG_D * h, HG_D * (h + 1))
            o = o_ref[:, sl]
            r = lax.rsqrt(jnp.mean(o * o, axis=-1, keepdims=True) + EPS)
            ohat = o * r
            silu_g, dsilu_g = _silu_pair(gate[:, sl])
            don = dy[:, sl] * silu_g
            du_ref[:, 1536 + HG_D * h:1536 + HG_D * (h + 1)] = (dy[:, sl] * ohat * gn_ref[:, sl] * dsilu_g).astype(BF)
            ds_ref[1:2, sl] += jnp.sum(don * ohat, axis=0, keepdims=True)
            gd = don * gn_ref[:, sl]
            dosc[:, sl] = r * (gd - ohat * jnp.mean(gd * ohat, axis=-1, keepdims=True))

        tri = lax.broadcasted_iota(jnp.int32, (CHUNK, CHUNK), 1) <= lax.broadcasted_iota(jnp.int32, (CHUNK, CHUNK), 0)
        last = lax.broadcasted_iota(jnp.int32, (CHUNK, 1), 0) == CHUNK - 1

        def incr(c, carry):
            rows = pl.ds(pl.multiple_of(c * CHUNK, CHUNK), CHUNK)
            b_c = bsc[rows, :]
            qE_b = (qsc[rows, :] * jnp.exp(b_c)).astype(BF)
            els[c] = jnp.broadcast_to(jnp.exp(b_c[CHUNK - 1:CHUNK, :]), (8, 512))
            do_c = dosc[rows, :].astype(BF)
            for h in range(HG_HEADS):
                sl = slice(HG_D * h, HG_D * (h + 1))
                ust[c, h] = _tn(do_c[:, sl], qE_b[:, sl])
            return carry

        lax.fori_loop(0, cpt, incr, 0, unroll=2)

        for h in range(HG_HEADS):
            sl = slice(HG_D * h, HG_D * (h + 1))
            d_s = dst[h]
            for c in reversed(range(cpt)):
                dss[c, h] = d_s
                d_s = els[c, 0:1, sl] * d_s + ust[c, h]
            dst[h] = d_s

        def chunk(c, carry):
            r0 = pl.multiple_of(c * CHUNK, CHUNK)
            rows = pl.ds(r0, CHUNK)
            e1, e2, e0, e3, el, qe, ke, qE, kd = _hg_chunk_terms(bsc[rows, :], qsc[rows, :], ksc[rows, :])
            qe_b, ke_b, kd_b = qe.astype(BF), ke.astype(BF), kd.astype(BF)
            do_c = dosc[rows, :].astype(BF)
            hs = range(HG_HEADS)
            sls = [slice(HG_D * h, HG_D * (h + 1)) for h in hs]
            v = [ub_ref[rows, 1024 + HG_D * h:1024 + HG_D * (h + 1)] for h in hs]
            do = [do_c[:, sl] for sl in sls]
            a = [_nt(qe_b[:, sl], ke_b[:, sl]) for sl in sls]
            da = [_nt(do[h], v[h]) for h in hs]
            dqE = [_nn(do[h], ss_ref[c, h].astype(BF)) for h in hs]
            dkd = [_nn(v[h], dss[c, h].astype(BF)) for h in hs]
            dv2 = [_nt(kd_b[:, sls[h]], dss[c, h].astype(BF)) for h in hs]
            a = [jnp.where(tri, x, 0.0).astype(BF) for x in a]
            da = [jnp.where(tri, x, 0.0).astype(BF) for x in da]
            dv = [_tn(a[h], do[h]) + dv2[h] for h in hs]
            dqe = [_nn(da[h], ke_b[:, sls[h]]) for h in hs]
            dke = [_tn(da[h], qe_b[:, sls[h]]) for h in hs]
            for h in hs:
                sl = sls[h]
                del_h = jnp.sum(ss_ref[c, h] * dss[c, h], axis=0, keepdims=True)
                dqsc[rows, sl] = dqE[h] * e0[:, sl] + dqe[h] * e1[:, sl]
                dksc[rows, sl] = dke[h] * e2[:, sl] + dkd[h] * e3[:, sl]
                tkd = dkd[h] * kd[:, sl]
                dbl = jnp.sum(tkd, axis=0, keepdims=True) + del_h * el[:, sl]
                dbsc[rows, sl] = (dqE[h] * qE[:, sl] + dqe[h] * qe[:, sl] - dke[h] * ke[:, sl] - tkd
                                  + jnp.where(last, dbl, 0.0))
                du_ref[rows, 1024 + HG_D * h:1024 + HG_D * (h + 1)] = dv[h].astype(BF)
            return carry

        lax.fori_loop(0, cpt, chunk, 0, unroll=2)

        dg = _mm_split(tup[...], dbsc[...])
        df = jnp.where(valid & (f > F_FLOOR), dg / f, 0.0)
        dk = jnp.where(valid, dksc[...], 0.0)
        dsig = (df - dk) * (1.0 - lbv)
        ds_ref[0:1, :] += jnp.sum((df - dk) * (1.0 - sig), axis=0, keepdims=True)
        du_ref[:, 512:1024] = (dsig * sig * (1.0 - sig)).astype(BF)
        du_ref[:, 0:512] = (dqsc[...] * dsilu_q).astype(BF)

    rev = lambda i: (nt - 1 - i, 0)
    in_specs = [pl.BlockSpec((TR, N_B), rev), pl.BlockSpec((1, 512), lambda i: (0, 0)),
                pl.BlockSpec((1, 512), lambda i: (0, 0)), pl.BlockSpec((TR, 512), rev),
                pl.BlockSpec((cpt, HG_HEADS, HG_D, HG_D), lambda i: (nt - 1 - i, 0, 0, 0)),
                pl.BlockSpec((TR, 512), rev)]
    out_specs = [pl.BlockSpec((TR, N_B), rev), pl.BlockSpec((8, 512), lambda i: (0, 0))]
    out_shape = [jax.ShapeDtypeStruct((Lp, N_B), BF), jax.ShapeDtypeStruct((8, 512), F32)]
    states = pltpu.VMEM((cpt, HG_HEADS, HG_D, HG_D), F32)
    scratch = ([pltpu.VMEM((HG_HEADS, HG_D, HG_D), F32)] + [pltpu.VMEM((TR, 512), F32)] * 7
               + [pltpu.VMEM((cpt, 8, 512), F32), states, states, pltpu.VMEM((TR, TR), BF), pltpu.VMEM((TR, TR), BF)])
    return _call_carrying(body, "hgrn_bwd", nt, in_specs, out_specs, out_shape, scratch,
                          (ub, lb, gn4, o_save, s_save, dyb), carry)


_KCOL = (2 * 512) // 128
_VCOL = _KCOL + 1


def _swa_in_specs(nt, rev):
    tile = (lambda i: nt - 1 - i) if rev else (lambda i: i)
    hpt = TR // HALO
    return [
        pl.BlockSpec((TR, 512), lambda i: (tile(i), 0)),
        pl.BlockSpec((TR, 512), lambda i: (tile(i), 1)),
        pl.BlockSpec((TR, 128), lambda i: (tile(i), _KCOL)),
        pl.BlockSpec((TR, 128), lambda i: (tile(i), _VCOL)),
        pl.BlockSpec((HALO, 128), lambda i: (jnp.maximum(tile(i) * hpt - 1, 0), _KCOL)),
        pl.BlockSpec((HALO, 128), lambda i: (jnp.maximum(tile(i) * hpt - 1, 0), _VCOL)),
        pl.BlockSpec((CHUNK, 128), lambda i: (0, _KCOL)),
        pl.BlockSpec((CHUNK, 128), lambda i: (0, _VCOL)),
        pl.BlockSpec((1, 512), lambda i: (0, 0)),
        pl.BlockSpec((1, 128), lambda i: (0, 0)),
        pl.BlockSpec((1, ATT_Q_HEADS), lambda i: (0, 0)),
    ]


_WROWS = 2 * CHUNK + HALO + TR
_W0 = 2 * CHUNK
_C0 = _W0 + HALO
_SCALE = ATT_HD ** -0.5


def _group_ones(n):
    r = lax.broadcasted_iota(jnp.int32, (n, n), 0)
    c = lax.broadcasted_iota(jnp.int32, (n, n), 1)
    return jnp.where(jnp.right_shift(r, 6) == jnp.right_shift(c, 6), 1.0, 0.0).astype(BF)


def _group_mean(x, ones):
    hi = x.astype(BF)
    lo = (x - hi.astype(F32)).astype(BF)
    return (_nn(hi, ones) + _nn(lo, ones)) * (1.0 / ATT_HD)


def _head_rms(x, ones):
    r = lax.rsqrt(_group_mean(x * x, ones) + EPS)
    return x * r, r


def _swa_windows(kc_ref, vc_ref, kh_ref, vh_ref, km_ref, vm_ref, kg2, ones, kwin, krwin, vwin, vrwin):
    meta = pl.ds(META_PAD, N_META)
    for (k, v, r0, n) in ((km_ref[meta, :], vm_ref[meta, :], 0, N_META), (kh_ref[...], vh_ref[...], _W0, HALO),
                          (kc_ref[...], vc_ref[...], _C0, TR)):
        xhat, _ = _head_rms(k.astype(F32), ones)
        kn = xhat * kg2
        kwin[pl.ds(r0, n), :] = kn.astype(BF)
        krwin[pl.ds(r0, n), :] = pltpu.roll(kn, ATT_HD, 1).astype(BF)
        vwin[pl.ds(r0, n), :] = v
        if vrwin is not None:
            vrwin[pl.ds(r0, n), :] = pltpu.roll(v.astype(F32), ATT_HD, 1).astype(BF)
    zero = jnp.zeros((_W0 - N_META, 128), BF)
    for w in (kwin, krwin, vwin, vrwin):
        if w is not None:
            w[pl.ds(N_META, _W0 - N_META), :] = zero


def _swa_masks_t(t, qb):
    q0 = t * TR + qb * QB
    qc = jnp.right_shift(q0 + lax.broadcasted_iota(jnp.int32, (1, QB), 1), 6)
    kabs = q0 - HALO + lax.broadcasted_iota(jnp.int32, (QB + HALO, 1), 0)
    kc = jnp.right_shift(kabs + HALO, 6) - HALO // CHUNK
    mask_w = (kc <= qc) & (kc >= qc - 2) & (kabs >= META_PAD)
    return qc > 2, mask_w


def _swa_park(dtype):
    return [pltpu.VMEM((ATT_Q_HEADS, N_META, QB), dtype), pltpu.VMEM((ATT_Q_HEADS, QB + HALO, QB), dtype)]


def _swa_ones():
    return [pltpu.VMEM((128, 128), BF), pltpu.VMEM((512, 512), BF)]


def _split_heads(x, lane_hi):
    return jnp.where(lane_hi, 0.0, x).astype(BF), jnp.where(lane_hi, x, 0.0).astype(BF)


def _call_carrying(body, name, nt, in_specs, out_specs, out_shape, scratch, args, carry):
    if carry is None:
        return pl.pallas_call(body, name=name, grid=(nt,), in_specs=in_specs, out_specs=out_specs, out_shape=out_shape,
                              scratch_shapes=scratch, compiler_params=_cp(("arbitrary",)))(*args)
    kind, arrs = carry
    n = len(arrs)
    return pl.pallas_call(
        _carry_exchange(body, len(in_specs), len(out_specs), nt, kind, n), name=name + "_" + kind, grid=(nt,),
        in_specs=in_specs + [_ANY] * n, out_specs=out_specs + [_ANY] * n,
        out_shape=out_shape + _exchange_out_shapes(kind, arrs), scratch_shapes=scratch + _exchange_sems(n),
        compiler_params=_cp(("arbitrary",), has_side_effects=True),
    )(*args, *arrs)


def _swa_fwd(uc, qg8, kg2, sinks, carry=None):
    Lp = uc.shape[0]
    nt = Lp // TR
    nqb = TR // QB

    def body(q_ref, g_ref, kc_ref, vc_ref, kh_ref, vh_ref, km_ref, vm_ref, qg_ref, kg_ref, sk_ref,
             yc_ref, o_ref, lse_ref, kwin, krwin, vwin, vt, qlo, qhi, ot, s_m, s_w, p_m, p_w, g128, g512):
        t = pl.program_id(0)

        @pl.when(t == 0)
        def _():
            g128[...] = _group_ones(128)
            g512[...] = _group_ones(512)

        _swa_windows(kc_ref, vc_ref, kh_ref, vh_ref, km_ref, vm_ref, kg_ref[...], g128[...], kwin, krwin, vwin, None)
        vt[...] = vwin[...].T
        xhat, _ = _head_rms(q_ref[...].astype(F32), g512[...])
        lane_hi = (lax.broadcasted_iota(jnp.int32, (1, 512), 1) & ATT_HD) != 0
        lo, hi = _split_heads(xhat * qg_ref[...] * _SCALE, lane_hi)
        qlo[...] = lo
        qhi[...] = hi
        for qb in range(nqb):
            rows = pl.ds(qb * QB, QB)
            wrows = pl.ds(_W0 + qb * QB, QB + HALO)
            mrows = pl.ds(0, N_META)
            mask_m, mask_w = _swa_masks_t(t, qb)
            for j in range(ATT_Q_HEADS):
                p, e = j // 2, j % 2
                ks = kwin if e == j // ATT_GROUP else krwin
                qp = (qlo, qhi)[e][rows, 128 * p:128 * (p + 1)]
                s_m[j] = _nt(ks[mrows, :], qp)
                s_w[j] = _nt(ks[wrows, :], qp)
            inv = []
            for j in range(ATT_Q_HEADS):
                sm = jnp.where(mask_m, s_m[j], NEG)
                sw = jnp.where(mask_w, s_w[j], NEG)
                sink = sk_ref[:, j:j + 1]
                m = jnp.maximum(jnp.maximum(jnp.max(sm, axis=0, keepdims=True),
                                            jnp.max(sw, axis=0, keepdims=True)), sink)
                em = jnp.exp(sm - m)
                ew = jnp.exp(sw - m)
                den = jnp.sum(em, axis=0, keepdims=True) + jnp.sum(ew, axis=0, keepdims=True) + jnp.exp(sink - m)
                p_m[j] = em.astype(BF)
                p_w[j] = ew.astype(BF)
                lse_ref[j:j + 1, pl.ds(qb * QB, QB)] = m + jnp.log(den)
                inv.append(1.0 / den)
            for j in range(ATT_Q_HEADS):
                vrows = pl.ds(ATT_HD * (j // ATT_GROUP), ATT_HD)
                ot[pl.ds(ATT_HD * j, ATT_HD), pl.ds(qb * QB, QB)] = (
                    _nn(vt[vrows, pl.ds(0, N_META)], p_m[j])
                    + _nn(vt[vrows, pl.ds(_W0 + qb * QB, QB + HALO)], p_w[j])) * inv[j]
        o = ot[...].T
        o_ref[...] = o
        yc_ref[...] = (o * _silu(g_ref[...].astype(F32))).astype(BF)

    win = pltpu.VMEM((_WROWS, 128), BF)
    in_specs = _swa_in_specs(nt, False)
    out_specs = [pl.BlockSpec((TR, 512), lambda i: (i, 0)), pl.BlockSpec((TR, 512), lambda i: (i, 0)),
                 pl.BlockSpec((ATT_Q_HEADS, TR), lambda i: (0, i))]
    out_shape = [jax.ShapeDtypeStruct((Lp, 512), BF), jax.ShapeDtypeStruct((Lp, 512), F32),
                 jax.ShapeDtypeStruct((ATT_Q_HEADS, Lp), F32)]
    scratch = [win, win, win, pltpu.VMEM((128, _WROWS), BF), pltpu.VMEM((TR, 512), BF),
               pltpu.VMEM((TR, 512), BF), pltpu.VMEM((512, TR), F32)] + _swa_park(F32) + _swa_park(BF) + _swa_ones()
    return _call_carrying(body, "swa_fwd", nt, in_specs, out_specs, out_shape, scratch,
                          (uc, uc, uc, uc, uc, uc, uc, uc, qg8, kg2, sinks), carry)


def _swa_bwd(uc, qg8, kg2, sinks, o_save, lse, dyc):
    Lp = uc.shape[0]
    nt = Lp // TR
    nqb = TR // QB

    def body(q_ref, g_ref, kc_ref, vc_ref, kh_ref, vh_ref, km_ref, vm_ref, qg_ref, kg_ref, sk_ref,
             o_ref, lse_ref, dy_ref, du_ref, dg_ref, dsk_ref,
             kwin, krwin, vwin, vrwin, kt, krt, qlo, qhi, dolo, dohi, dqt, dk_dir, dk_rol, dv_dir, dv_rol,
             carry_k, carry_v, meta_k, meta_v, s_m, s_w, dp_m, dp_w, p_m, p_w, ds_m, ds_w, g128, g512):
        i = pl.program_id(0)
        t = nt - 1 - i

        @pl.when(i == 0)
        def _():
            carry_k[...] = jnp.zeros_like(carry_k)
            carry_v[...] = jnp.zeros_like(carry_v)
            meta_k[...] = jnp.zeros_like(meta_k)
            meta_v[...] = jnp.zeros_like(meta_v)
            dg_ref[...] = jnp.zeros_like(dg_ref)
            dsk_ref[...] = jnp.zeros_like(dsk_ref)
            g128[...] = _group_ones(128)
            g512[...] = _group_ones(512)

        ones128 = g128[...]
        ones512 = g512[...]
        _swa_windows(kc_ref, vc_ref, kh_ref, vh_ref, km_ref, vm_ref, kg_ref[...], ones128, kwin, krwin, vwin, vrwin)
        kt[...] = kwin[...].T
        krt[...] = krwin[...].T
        xhat_q, r_q = _head_rms(q_ref[...].astype(F32), ones512)
        lane_hi = (lax.broadcasted_iota(jnp.int32, (1, 512), 1) & ATT_HD) != 0
        lo, hi = _split_heads(xhat_q * qg_ref[...] * _SCALE, lane_hi)
        qlo[...] = lo
        qhi[...] = hi
        gate = g_ref[...].astype(F32)
        dy = dy_ref[...].astype(F32)
        silu_g, dsilu_g = _silu_pair(gate)
        do = dy * silu_g
        o = o_ref[...]
        du_ref[:, 512:1024] = (dy * o * dsilu_g).astype(BF)
        lo, hi = _split_heads(do, lane_hi)
        dolo[...] = lo
        dohi[...] = hi
        hsel = jnp.where(jnp.right_shift(lax.broadcasted_iota(jnp.int32, (ATT_Q_HEADS, 512), 1), 6)
                         == lax.broadcasted_iota(jnp.int32, (ATT_Q_HEADS, 512), 0), 1.0, 0.0).astype(BF)
        prod = do * o
        p_hi = prod.astype(BF)
        d_t = _nt(hsel, p_hi) + _nt(hsel, (prod - p_hi.astype(F32)).astype(BF))
        for acc in (dk_dir, dk_rol, dv_dir, dv_rol):
            acc[...] = jnp.zeros_like(acc)

        for qb in range(nqb):
            rows = pl.ds(qb * QB, QB)
            qcols = pl.ds(qb * QB, QB)
            wrows = pl.ds(_W0 + qb * QB, QB + HALO)
            mrows = pl.ds(0, N_META)
            mask_m, mask_w = _swa_masks_t(t, qb)
            for j in range(ATT_Q_HEADS):
                p, e = j // 2, j % 2
                ks, vs = (kwin, vwin) if e == j // ATT_GROUP else (krwin, vrwin)
                pair = slice(128 * p, 128 * (p + 1))
                qp = (qlo, qhi)[e][rows, pair]
                dop = (dolo, dohi)[e][rows, pair]
                s_m[j] = _nt(ks[mrows, :], qp)
                s_w[j] = _nt(ks[wrows, :], qp)
                dp_m[j] = _nt(vs[mrows, :], dop)
                dp_w[j] = _nt(vs[wrows, :], dop)
            for j in range(ATT_Q_HEADS):
                lse_j = lse_ref[j:j + 1, qcols]
                d_j = d_t[j:j + 1, qb * QB:(qb + 1) * QB]
                em = jnp.exp(jnp.where(mask_m, s_m[j], NEG) - lse_j)
                ew = jnp.exp(jnp.where(mask_w, s_w[j], NEG) - lse_j)
                p_m[j] = em.astype(BF)
                p_w[j] = ew.astype(BF)
                ds_m[j] = (em * (dp_m[j] - d_j)).astype(BF)
                ds_w[j] = (ew * (dp_w[j] - d_j)).astype(BF)
                dsk_ref[j:j + 1, :] -= jnp.exp(sk_ref[:, j:j + 1] - lse_j) * d_j
            for j in range(ATT_Q_HEADS):
                e = j % 2
                ktr = kt if e == j // ATT_GROUP else krt
                hrows = pl.ds(ATT_HD * e, ATT_HD)
                dqt[pl.ds(ATT_HD * j, ATT_HD), qcols] = (_nn(ktr[hrows, pl.ds(0, N_META)], ds_m[j])
                                                         + _nn(ktr[hrows, pl.ds(_W0 + qb * QB, QB + HALO)], ds_w[j]))
            for direct, dk_acc, dv_acc in ((True, dk_dir, dv_dir), (False, dk_rol, dv_rol)):
                heads = [j for j in range(ATT_Q_HEADS) if (j % 2 == j // ATT_GROUP) == direct]
                q_cat = jnp.concatenate([(qlo, qhi)[j % 2][rows, 128 * (j // 2):128 * (j // 2 + 1)] for j in heads], axis=0)
                do_cat = jnp.concatenate([(dolo, dohi)[j % 2][rows, 128 * (j // 2):128 * (j // 2 + 1)] for j in heads], axis=0)
                dk_acc[mrows, :] += _nn(jnp.concatenate([ds_m[j] for j in heads], axis=1), q_cat)
                dk_acc[wrows, :] += _nn(jnp.concatenate([ds_w[j] for j in heads], axis=1), q_cat)
                dv_acc[mrows, :] += _nn(jnp.concatenate([p_m[j] for j in heads], axis=1), do_cat)
                dv_acc[wrows, :] += _nn(jnp.concatenate([p_w[j] for j in heads], axis=1), do_cat)

        dk_dir[...] += pltpu.roll(dk_rol[...], ATT_HD, 1)
        dv_dir[...] += pltpu.roll(dv_rol[...], ATT_HD, 1)
        meta_k[...] += dk_dir[pl.ds(0, N_META), :]
        meta_v[...] += dv_dir[pl.ds(0, N_META), :]
        first = jnp.where(t == 0, 1.0, 0.0)
        dk_dir[pl.ds(_C0 + TR - HALO, HALO), :] += carry_k[...]
        dv_dir[pl.ds(_C0 + TR - HALO, HALO), :] += carry_v[...]
        dk_dir[pl.ds(_C0 + META_PAD, N_META), :] += first * meta_k[...]
        dv_dir[pl.ds(_C0 + META_PAD, N_META), :] += first * meta_v[...]
        carry_k[...] = dk_dir[pl.ds(_W0, HALO), :]
        carry_v[...] = dv_dir[pl.ds(_W0, HALO), :]

        du_ref[:, 1152:1280] = dv_dir[pl.ds(_C0, TR), :].astype(BF)
        xhat_k, r_k = _head_rms(kc_ref[...].astype(F32), ones128)
        dkn = dk_dir[pl.ds(_C0, TR), :]
        dg_ref[1:2, 0:128] += jnp.sum(dkn * xhat_k, axis=0, keepdims=True)
        gd = dkn * kg_ref[...]
        du_ref[:, 1024:1152] = (r_k * (gd - xhat_k * _group_mean(gd * xhat_k, ones128))).astype(BF)
        dqn = dqt[...].T * _SCALE
        dg_ref[0:1, :] += jnp.sum(dqn * xhat_q, axis=0, keepdims=True)
        gd = dqn * qg_ref[...]
        du_ref[:, 0:512] = (r_q * (gd - xhat_q * _group_mean(gd * xhat_q, ones512))).astype(BF)

    rev = lambda i: (nt - 1 - i, 0)
    specs = _swa_in_specs(nt, True)
    win = pltpu.VMEM((_WROWS, 128), BF)
    wint = pltpu.VMEM((128, _WROWS), BF)
    tile_bf = pltpu.VMEM((TR, 512), BF)
    acc = pltpu.VMEM((_WROWS, 128), F32)
    return pl.pallas_call(
        body, name="swa_bwd", grid=(nt,),
        in_specs=specs + [pl.BlockSpec((TR, 512), rev), pl.BlockSpec((ATT_Q_HEADS, TR), lambda i: (0, nt - 1 - i)),
                          pl.BlockSpec((TR, 512), rev)],
        out_specs=[pl.BlockSpec((TR, N_C), rev), pl.BlockSpec((8, 512), lambda i: (0, 0)),
                   pl.BlockSpec((8, 128), lambda i: (0, 0))],
        out_shape=[jax.ShapeDtypeStruct((Lp, N_C), BF), jax.ShapeDtypeStruct((8, 512), F32),
                   jax.ShapeDtypeStruct((8, 128), F32)],
        scratch_shapes=[win, win, win, win, wint, wint, tile_bf, tile_bf, tile_bf, tile_bf,
                        pltpu.VMEM((512, TR), F32), acc, acc, acc, acc,
                        pltpu.VMEM((HALO, 128), F32), pltpu.VMEM((HALO, 128), F32),
                        pltpu.VMEM((N_META, 128), F32), pltpu.VMEM((N_META, 128), F32)]
        + _swa_park(F32) + _swa_park(F32) + _swa_park(BF) + _swa_park(BF) + _swa_ones(),
        compiler_params=_cp(("arbitrary",)),
    )(uc, uc, uc, uc, uc, uc, uc, uc, qg8, kg2, sinks, o_save, lse, dyc)


def _mix_fwd(h, ya, yb, yc, ug, wa, wb, wc, wo, g_next=None, loss=None):
    Lp = h.shape[0]
    wspec = lambda r: pl.BlockSpec((r, D_MODEL), lambda i: (0, 0))
    yspec = pl.BlockSpec((TRM, 512), lambda i: (i, 0))
    hspec = pl.BlockSpec((TRM, D_MODEL), lambda i: (i, 0))
    last = loss is not None

    def body(h_ref, ya_ref, yb_ref, yc_ref, ug_ref, wa_ref, wb_ref, wc_ref, wo_ref, t_ref, *outs):
        mixed = jnp.zeros((TRM, D_MODEL), F32)
        for n, (y_ref, w_ref) in enumerate(((ya_ref, wa_ref), (yb_ref, wb_ref), (yc_ref, wc_ref))):
            z = _nn(y_ref[...], w_ref[...])
            outs[n][...] = z.astype(BF)
            mixed = mixed + _sig(ug_ref[:, D_MODEL * n:D_MODEL * (n + 1)].astype(F32)) * z
        mixed = mixed.astype(BF)
        outs[3][...] = mixed
        x = h_ref[...] + _nn(mixed, wo_ref[...])
        if last:
            dh_ref, l_ref = outs[4:]
            i = pl.program_id(0)

            @pl.when(i == 0)
            def _():
                l_ref[...] = jnp.zeros_like(l_ref)

            row = i * TRM + lax.broadcasted_iota(jnp.int32, (TRM, 1), 0)
            e = jnp.where((row >= CHUNK) & (row < CHUNK + loss[1]), x - t_ref[...], 0.0)
            dh_ref[...] = e * (1.0 / D_MODEL)
            l_ref[...] += (0.5 / D_MODEL) * jnp.sum(jnp.sum(e * e, axis=0, keepdims=True), axis=1, keepdims=True)
        else:
            x_ref, nx_ref = outs[4:]
            x_ref[...] = x
            nx_ref[...] = (x * lax.rsqrt(jnp.mean(x * x, axis=-1, keepdims=True) + EPS) * t_ref[...]).astype(BF)

    bf = jax.ShapeDtypeStruct((Lp, D_MODEL), BF)
    f32 = jax.ShapeDtypeStruct((Lp, D_MODEL), F32)
    if last:
        tail_in, tail_spec = loss[0], hspec
        out_specs = [hspec] * 5 + [pl.BlockSpec((8, 128), lambda i: (0, 0))]
        out_shape = [bf] * 4 + [f32, jax.ShapeDtypeStruct((8, 128), F32)]
    else:
        tail_in, tail_spec = g_next, wspec(1)
        out_specs = [hspec] * 6
        out_shape = [bf] * 4 + [f32, bf]
    return pl.pallas_call(
        body, name="mix_fwd_loss" if last else "mix_fwd", grid=(Lp // TRM,),
        in_specs=[hspec, yspec, yspec, yspec, pl.BlockSpec((TRM, N_G), lambda i: (i, 0)),
                  wspec(512), wspec(512), wspec(512), wspec(D_MODEL), tail_spec],
        out_specs=out_specs, out_shape=out_shape,
        compiler_params=_cp(("arbitrary",) if last else ("parallel",)),
    )(h, ya, yb, yc, ug, wa, wb, wc, wo, tail_in)


def _mix_bwd(dh, za, zb, zc, ug, wa, wb, wc, wo):
    Lp = dh.shape[0]
    wspec = lambda r: pl.BlockSpec((r, D_MODEL), lambda i: (0, 0))
    yspec = pl.BlockSpec((TRM, 512), lambda i: (i, 0))
    hspec = pl.BlockSpec((TRM, D_MODEL), lambda i: (i, 0))
    gspec = pl.BlockSpec((TRM, N_G), lambda i: (i, 0))

    def body(dh_ref, za_ref, zb_ref, zc_ref, ug_ref, wa_ref, wb_ref, wc_ref, wo_ref,
             dug_ref, dza_ref, dzb_ref, dzc_ref, dya_ref, dyb_ref, dyc_ref):
        dmix = _nt(dh_ref[...].astype(BF), wo_ref[...])
        for n, (z_ref, w_ref, dz_ref, dy_ref) in enumerate(((za_ref, wa_ref, dza_ref, dya_ref),
                                                            (zb_ref, wb_ref, dzb_ref, dyb_ref),
                                                            (zc_ref, wc_ref, dzc_ref, dyc_ref))):
            sl = slice(D_MODEL * n, D_MODEL * (n + 1))
            gt = _sig(ug_ref[:, sl].astype(F32))
            dz = dmix * gt
            dug_ref[:, sl] = (dz * z_ref[...].astype(F32) * (1.0 - gt)).astype(BF)
            dz = dz.astype(BF)
            dz_ref[...] = dz
            dy_ref[...] = _nt(dz, w_ref[...]).astype(BF)

    bf = lambda n: jax.ShapeDtypeStruct((Lp, n), BF)
    return pl.pallas_call(
        body, name="mix_bwd", grid=(Lp // TRM,),
        in_specs=[hspec, hspec, hspec, hspec, gspec, wspec(512), wspec(512), wspec(512), wspec(D_MODEL)],
        out_specs=[gspec, hspec, hspec, hspec, yspec, yspec, yspec],
        out_shape=[bf(N_G), bf(D_MODEL), bf(D_MODEL), bf(D_MODEL), bf(512), bf(512), bf(512)],
        compiler_params=_cp(("parallel",)),
    )(dh, za, zb, zc, ug, wa, wb, wc, wo)


def _inproj_bwd(dus, ws, h, dh, g, carry=None):
    Lp = h.shape[0]
    widths = [w.shape[0] for w in ws]

    def body(dg_ref, da_ref, db_ref, dc_ref, wg_ref, wa_ref, wb_ref, wc_ref, h_ref, dh_ref, g_ref, o_ref, gg_ref):
        @pl.when(pl.program_id(0) == 0)
        def _():
            gg_ref[...] = jnp.zeros_like(gg_ref)

        dhn = (_nn(dg_ref[...], wg_ref[...]) + _nn(da_ref[...], wa_ref[...])
               + _nn(db_ref[...], wb_ref[...]) + _nn(dc_ref[...], wc_ref[...]))
        x = h_ref[...]
        r = lax.rsqrt(jnp.mean(x * x, axis=-1, keepdims=True) + EPS)
        xhat = x * r
        gg_ref[0:1, :] += jnp.sum(dhn * xhat, axis=0, keepdims=True)
        gd = dhn * g_ref[...]
        o_ref[...] = dh_ref[...] + r * (gd - xhat * jnp.mean(gd * xhat, axis=-1, keepdims=True))

    hspec = pl.BlockSpec((TRM, D_MODEL), lambda i: (i, 0))
    in_specs = ([pl.BlockSpec((TRM, n), lambda i: (i, 0)) for n in widths]
                + [pl.BlockSpec((n, D_MODEL), lambda i: (0, 0), pipeline_mode=pl.Buffered(1)) for n in widths]
                + [hspec, hspec, pl.BlockSpec((1, D_MODEL), lambda i: (0, 0))])
    out_specs = [hspec, pl.BlockSpec((8, D_MODEL), lambda i: (0, 0))]
    out_shape = [jax.ShapeDtypeStruct((Lp, D_MODEL), F32), jax.ShapeDtypeStruct((8, D_MODEL), F32)]
    return _call_carrying(body, "inproj_bwd", Lp // TRM, in_specs, out_specs, out_shape, [],
                          (*dus, *ws, h, dh, g), carry)


def _lb_softmax(lb_ref):
    x = lb_ref[...]
    e = jnp.exp(x - jnp.max(x, axis=0, keepdims=True))
    return e / jnp.sum(e, axis=0, keepdims=True)


def _lb_fwd(hg_lb):
    def body(lb_ref, o_ref):
        sm = _lb_softmax(lb_ref)
        acc = jnp.zeros((1, 512), F32)
        for l in range(DEPTH):
            if l > 0:
                acc = acc + sm[l:l + 1, :]
            o_ref[l:l + 1, :] = jnp.clip(acc, 0.0, 1.0)

    return pl.pallas_call(body, name="lb_fwd", out_shape=jax.ShapeDtypeStruct((DEPTH, 512), F32))(hg_lb)


def _lb_bwd(hg_lb, dlb_all):
    def body(lb_ref, d_ref, o_ref):
        sm = _lb_softmax(lb_ref)
        acc = jnp.zeros((1, 512), F32)
        gm = []
        for l in range(DEPTH):
            if l > 0:
                acc = acc + sm[l:l + 1, :]
            gm.append(jnp.where((acc >= 0.0) & (acc <= 1.0), d_ref[l:l + 1, :], 0.0))
        dsm = [jnp.zeros((1, 512), F32)]
        for j in range(1, DEPTH):
            s = gm[j]
            for l in range(j + 1, DEPTH):
                s = s + gm[l]
            dsm.append(s)
        dot = dsm[0] * sm[0:1, :]
        for j in range(1, DEPTH):
            dot = dot + dsm[j] * sm[j:j + 1, :]
        for j in range(DEPTH):
            o_ref[j:j + 1, :] = sm[j:j + 1, :] * (dsm[j] - dot)

    return pl.pallas_call(body, name="lb_bwd", out_shape=jax.ShapeDtypeStruct((DEPTH, 512), F32))(hg_lb, dlb_all)


_ANY = pl.BlockSpec(memory_space=pl.ANY)


def _chip_peers():
    x, y, c = lax.axis_index("x"), lax.axis_index("y"), lax.axis_index("c")
    return (x, y, c), [(1 - x, y, c), (x, 1 - y, c), (1 - x, 1 - y, c)]


def _exchange(kind, ins, outs, send, recv, loc):
    (x, y, c), peers = _chip_peers()
    me = 2 * x + y
    ds = []
    for a in range(len(ins)):
        if kind == "gather":
            ds.append(pltpu.make_async_copy(ins[a], outs[a].at[me], loc.at[a]))
        else:
            ds.append(pltpu.make_async_copy(ins[a].at[me], outs[a].at[0], loc.at[a]))
        for p, (px, py, pc) in enumerate(peers):
            src, dst = (ins[a], outs[a].at[me]) if kind == "gather" else (ins[a].at[2 * px + py], outs[a].at[1 + p])
            ds.append(pltpu.make_async_remote_copy(src_ref=src, dst_ref=dst, send_sem=send.at[a, p],
                                                   recv_sem=recv.at[a, p], device_id=(px, py, pc), device_id_type=MESH))
    return ds


def _exchange_out_shapes(kind, arrs):
    if kind == "gather":
        return [jax.ShapeDtypeStruct((4,) + a.shape, a.dtype) for a in arrs]
    return [jax.ShapeDtypeStruct(a.shape, a.dtype) for a in arrs]


def _exchange_sems(n):
    return [pltpu.SemaphoreType.DMA((n, 3)), pltpu.SemaphoreType.DMA((n, 3)), pltpu.SemaphoreType.DMA((n,))]


def _gather_first(arrs, split):
    n = len(arrs)

    def body(*refs):
        ins, outs = refs[:n], refs[n:2 * n]
        send, recv, loc, fsend, frecv = refs[2 * n:]
        (x, y, c), peers = _chip_peers()
        me = 2 * x + y

        def half(a):
            hr = arrs[a].shape[0] // 2
            return pl.ds(pl.multiple_of(c * hr, 16), hr)

        local = [pltpu.make_async_copy(ins[a], outs[a].at[me], loc.at[a]) for a in range(n)]
        far, fwd = {}, {}
        for a in range(n):
            for p, (px, py, pc) in enumerate(peers):
                src, dst = (ins[a].at[half(a)], outs[a].at[me, half(a)]) if split[a] else (ins[a], outs[a].at[me])
                far[a, p] = pltpu.make_async_remote_copy(src_ref=src, dst_ref=dst, send_sem=send.at[a, p],
                                                         recv_sem=recv.at[a, p], device_id=(px, py, pc), device_id_type=MESH)
                if split[a]:
                    landed = outs[a].at[2 * px + py, half(a)]
                    fwd[a, p] = pltpu.make_async_remote_copy(src_ref=landed, dst_ref=landed, send_sem=fsend.at[a, p],
                                                             recv_sem=frecv.at[a, p], device_id=(x, y, 1 - c),
                                                             device_id_type=MESH)
        for d in local + list(far.values()):
            d.start()
        for key, d in far.items():
            d.wait_recv()
            if key in fwd:
                fwd[key].start()
        for d in fwd.values():
            d.wait()
        for d in far.values():
            d.wait_send()
        for d in local:
            d.wait()

    sems = pltpu.SemaphoreType.DMA((n, 3))
    return pl.pallas_call(
        body, name="gather_first", in_specs=[_ANY] * n, out_specs=[_ANY] * n,
        out_shape=_exchange_out_shapes("gather", arrs),
        scratch_shapes=[sems, sems, pltpu.SemaphoreType.DMA((n,)), sems, sems],
        compiler_params=pltpu.CompilerParams(has_side_effects=True),
    )(*arrs)


def _carry_exchange(body, n_in, n_out, n_steps, kind, n):
    def wrapped(*refs):
        ins, cin = refs[:n_in], refs[n_in:n_in + n]
        outs, cout = refs[n_in + n:n_in + n + n_out], refs[n_in + n + n_out:n_in + 2 * n + n_out]
        scr, sems = refs[n_in + 2 * n + n_out:-3], refs[-3:]
        i = pl.program_id(0)

        @pl.when(i == 0)
        def _():
            for d in _exchange(kind, cin, cout, *sems):
                d.start()

        body(*ins, *outs, *scr)

        @pl.when(i == n_steps - 1)
        def _():
            for d in _exchange(kind, cin, cout, *sems):
                d.wait()

    return wrapped


def _swap_cores(arrs):
    n = len(arrs)

    def body(*refs):
        ins, outs = refs[:n], refs[n:2 * n]
        send, recv = refs[2 * n:]
        x, y, c = lax.axis_index("x"), lax.axis_index("y"), lax.axis_index("c")
        rdmas = []
        for a in range(n):
            r = pltpu.make_async_remote_copy(src_ref=ins[a], dst_ref=outs[a], send_sem=send.at[a], recv_sem=recv.at[a],
                                             device_id=(x, y, 1 - c), device_id_type=MESH)
            r.start()
            rdmas.append(r)
        for r in rdmas:
            r.wait()

    return pl.pallas_call(
        body, name="swap_cores", in_specs=[_ANY] * n, out_specs=[_ANY] * n,
        out_shape=[jax.ShapeDtypeStruct(a.shape, a.dtype) for a in arrs],
        scratch_shapes=[pltpu.SemaphoreType.DMA((n,)), pltpu.SemaphoreType.DMA((n,))],
        compiler_params=pltpu.CompilerParams(has_side_effects=True),
    )(*arrs)


def _allsum_small(p):
    R = p.shape[0]

    def body(p_ref, o_ref, buf, send, recv):
        x, y, c = lax.axis_index("x"), lax.axis_index("y"), lax.axis_index("c")
        me = 4 * x + 2 * y + c
        buf[me] = p_ref[...]
        rdmas = []
        for k in range(1, 8):
            peer = (x ^ (k >> 2), y ^ ((k >> 1) & 1), c ^ (k & 1))
            r = pltpu.make_async_remote_copy(src_ref=p_ref, dst_ref=buf.at[me], send_sem=send.at[k - 1],
                                             recv_sem=recv.at[k - 1], device_id=peer, device_id_type=MESH)
            r.start()
            rdmas.append(r)
        for r in rdmas:
            r.wait()
        acc = buf[0]
        for d in range(1, 8):
            acc = acc + buf[d]
        o_ref[...] = acc

    return pl.pallas_call(
        body, name="allsum_small", out_shape=jax.ShapeDtypeStruct((R, 512), F32),
        in_specs=[pl.BlockSpec(memory_space=pltpu.VMEM)], out_specs=pl.BlockSpec(memory_space=pltpu.VMEM),
        scratch_shapes=[pltpu.VMEM((8, R, 512), F32), pltpu.SemaphoreType.DMA((7,)), pltpu.SemaphoreType.DMA((7,))],
        compiler_params=_cp(has_side_effects=True),
    )(p)


def _row_block(rows):
    return max((d for d in range(16, 513, 16) if rows % d == 0), default=rows)


def _sum4(parts, name):
    _, R, C = parts.shape
    tr = _row_block(R)

    def body(p_ref, o_ref):
        p = [p_ref[k].astype(F32) for k in range(4)]
        o_ref[...] = (((p[0] + p[1]) + p[2]) + p[3]).astype(BF)

    return pl.pallas_call(
        body, name=name, grid=(R // tr,), in_specs=[pl.BlockSpec((4, tr, C), lambda i: (0, i, 0))],
        out_specs=pl.BlockSpec((tr, C), lambda i: (i, 0)), out_shape=jax.ShapeDtypeStruct((R, C), BF),
        compiler_params=_cp(("parallel",)),
    )(parts)


def _adamw(w, m, v, g0, g1, name):
    L, R, C = w.shape
    tr = _row_block(R)
    two = g1 is not None
    c1 = 1.0 / (1.0 - ADAM_B1 ** ADAM_STEP)
    c2 = 1.0 / (1.0 - ADAM_B2 ** ADAM_STEP)

    def body(*refs):
        if two:
            w_ref, m_ref, v_ref, a_ref, b_ref, g_ref, d_ref, nm_ref, nv_ref = refs
            g = a_ref[...].astype(F32) + b_ref[...].astype(F32)
        else:
            w_ref, m_ref, v_ref, a_ref, g_ref, d_ref, nm_ref, nv_ref = refs
            g = a_ref[...]
        g_ref[...] = g
        m = ADAM_B1 * m_ref[...] + (1.0 - ADAM_B1) * g
        v = ADAM_B2 * v_ref[...] + (1.0 - ADAM_B2) * (g * g)
        nm_ref[...] = m
        nv_ref[...] = v
        d_ref[...] = -ADAM_LR * ((m * c1) / (jnp.sqrt(v * c2) + ADAM_EPS) + ADAM_WD * w_ref[...])

    spec = pl.BlockSpec((1, tr, C), lambda l, i: (l, i, 0))
    n_in = 5 if two else 4
    ins = (w, m, v, g0, g1) if two else (w, m, v, g0)
    return pl.pallas_call(
        body, name=name, grid=(L, R // tr), in_specs=[spec] * n_in, out_specs=[spec] * 4,
        out_shape=[jax.ShapeDtypeStruct((L, R, C), F32)] * 4, compiler_params=_cp(("parallel", "parallel")),
    )(*ins)


def _pad8(a):
    r = (-a.shape[0]) % 8
    return a if r == 0 else jnp.pad(a, ((0, r), (0, 0)))


def _local_step(x, tgt, meta, P, shards=None, prep=None, pack=None, mats=None):
    seq = x.shape[0]
    Lp = -(-(seq + CHUNK) // TR) * TR
    tail = Lp - seq - CHUNK
    h = jnp.concatenate([jnp.zeros((META_PAD, D_MODEL), F32), meta, x, jnp.zeros((tail, D_MODEL), F32)], axis=0)
    tgt_pad = jnp.pad(tgt, ((CHUNK, tail), (0, 0)))

    P = list(P)
    saved = []
    hn = _rms_fwd(h, P[0]["norm_g"])
    for l in range(DEPTH):
        p = P[l]
        mm = functools.partial(_matmul, tb=True, out_dtype=BF, tm=TR, tk=D_MODEL, col_major_grid=True)
        ug = mm(hn, p["w_g"], tn=N_G // 2, name="inproj_g")
        ua = mm(hn, p["w_a"], tn=N_A, name="inproj_a")
        ub = mm(hn, p["w_b"], tn=N_B, name="inproj_b")
        uc = mm(hn, p["w_c"], tn=N_C, name="inproj_c")
        nxt = shards[l + 1] if shards is not None and l + 1 < DEPTH else None
        carry = (lambda part: ("gather", part)) if nxt is not None else (lambda part: None)
        res_a = _conv_fwd(ua, p["conv_w"], p["conv_vec"], carry(nxt and nxt[1:2]))
        res_b = _hg_fwd(ub, p["lb"], p["gn4"], carry(nxt and nxt[0:1]))
        own = shards[0][2:] if nxt is not None and l == 0 else []
        res_c = _swa_fwd(uc, p["qg"], p["kg"], p["sinks"], carry(nxt and nxt[2:] + own))
        (ya, yconv), (yb, o_hg, s_hg), (yc, o_at, lse) = res_a[:2], res_b[:3], res_c[:3]
        if nxt is not None:
            P.append(prep(l + 1, [*res_b[3:], *res_a[2:], *res_c[3:3 + len(nxt) - 2]]))
            if own:
                p.update(mats(res_c[3 + len(nxt) - 2:]))
        mix = functools.partial(_mix_fwd, h, ya, yb, yc, ug, p["w_ao"], p["w_bo"], p["w_co"], p["w_out"])
        if l + 1 < DEPTH:
            za, zb, zc, mixed, h_new, hn_next = mix(g_next=P[l + 1]["norm_g"])
        else:
            za, zb, zc, mixed, dh, loss8 = mix(loss=(tgt_pad, seq))
        saved.append(dict(h=h, hn=hn, ug=ug, ua=ua, ub=ub, uc=uc, ya=ya, yconv=yconv, yb=yb, o_hg=o_hg, s_hg=s_hg,
                          yc=yc, o_at=o_at, lse=lse, za=za, zb=zb, zc=zc, mixed=mixed))
        if l + 1 < DEPTH:
            h, hn = h_new, hn_next

    grads = [None] * DEPTH
    parts = [[None, None] for _ in range(DEPTH)]
    pending = None
    tk_dw = 2 * TR if Lp % (2 * TR) == 0 else TR
    for l in reversed(range(DEPTH)):
        p, s = P[l], saved[l]
        dug, dza, dzb, dzc, dya, dyb, dyc = _mix_bwd(dh, s["za"], s["zb"], s["zc"], s["ug"],
                                                      p["w_ao"], p["w_bo"], p["w_co"], p["w_out"])
        tnmm = functools.partial(_matmul, ta=True, out_dtype=F32, tk=tk_dw)
        g = {}
        g["w_out"] = tnmm(s["mixed"], dh, tm=D_MODEL, tn=D_MODEL, name="dw_out")
        g["w_ao"], g["w_bo"], g["w_co"] = _dw_branches([s["ya"], s["yb"], s["yc"]], [dza, dzb, dzc], tk_dw)
        dua, g["conv_w"], g["conv_vec"] = _conv_bwd(s["ua"], s["yconv"], dya, p["conv_w"], p["conv_vec"])
        carry = ("scatter", pending[1]) if pending is not None else None
        res = _hg_bwd(s["ub"], p["lb"], p["gn4"], s["o_hg"], s["s_hg"], dyb, carry)
        dub, g["hg_small"] = res[:2]
        if carry is not None:
            parts[pending[0]][1] = res[2:]
        duc, g["at_gain"], g["at_sink"] = _swa_bwd(s["uc"], p["qg"], p["kg"], p["sinks"], s["o_at"], s["lse"], dyc)
        g["w_g"] = tnmm(dug, s["hn"], tm=N_G // 2, tn=D_MODEL, name="dw_in_g")
        g["w_a"] = tnmm(dua, s["hn"], tm=N_A, tn=D_MODEL, name="dw_in_a")
        g["w_b"] = tnmm(dub, s["hn"], tm=N_B, tn=D_MODEL, name="dw_in_b")
        g["w_c"] = tnmm(duc, s["hn"], tm=N_C, tn=D_MODEL, name="dw_in_c")
        first, second = pack(g) if pack is not None else (None, None)
        if first is not None and l == 0:
            first, second = first + second, []
        res = _inproj_bwd([dug, dua, dub, duc], [p["w_g"], p["w_a"], p["w_b"], p["w_c"]], s["h"], dh, p["norm_g"],
                          ("scatter", first) if first is not None else None)
        dh, g["norm_g"] = res[:2]
        grads[l] = g
        if pack is not None:
            parts[l] = [res[2:3], res[3:]] if l == 0 else [res[2:], None]
            pending = (l, second) if l > 0 else None
    return loss8, dh, grads, parts


def _split_w_in(wt):
    return dict(w_a=wt[0:1536], w_b=wt[1536:3584],
                w_c=jnp.concatenate([wt[3584:4096], wt[4352:4864], wt[4096:4352]], axis=0), w_g=wt[4864:7936])


def _join_w_in(g):
    c = g["w_c"]
    return jnp.concatenate([g["w_a"], g["w_b"], c[0:512], c[1024:1280], c[512:1024], g["w_g"]], axis=0)


def _attn_small(g):
    return (g["at_gain"][0].reshape(ATT_Q_HEADS, ATT_HD).sum(0),
            g["at_gain"][1, 0:128].reshape(ATT_KV_HEADS, ATT_HD).sum(0), g["at_sink"].sum(1))


_SMALL = (("norm_g", 8), ("meta", 32), ("conv_w", 32 * DEPTH), ("conv_b", 8), ("conv_ln_g", 8), ("conv_ln_b", 8),
          ("lb", 8), ("hg_norm_g", 8), ("q_norm_g", 8), ("k_norm_g", 8), ("sinks", 8))


def _small_offsets():
    off, o = {}, 0
    for name, rows in _SMALL:
        off[name] = (o, rows)
        o += rows
    return off, o


def _pack_small(d):
    parts = []
    for name, rows in _SMALL:
        a = d[name]
        parts.append(jnp.pad(a, ((0, rows - a.shape[0]), (0, 512 - a.shape[1]))))
    return jnp.concatenate(parts, axis=0)


def kernel(x, meta_tokens, norm_g, w_in, conv_w, conv_b, conv_ln_g, conv_ln_b, w_conv_out, hg_lower_bounds, hg_norm_g, w_hg_out, q_norm_g, k_norm_g, attn_sinks, w_att_out, w_out, loss_target, m_meta_tokens, m_norm_g, m_w_in, m_conv_w, m_conv_b, m_conv_ln_g, m_conv_ln_b, m_w_conv_out, m_hg_lower_bounds, m_hg_norm_g, m_w_hg_out, m_q_norm_g, m_k_norm_g, m_attn_sinks, m_w_att_out, m_w_out, v_meta_tokens, v_norm_g, v_w_in, v_conv_w, v_conv_b, v_conv_ln_g, v_conv_ln_b, v_w_conv_out, v_hg_lower_bounds, v_hg_norm_g, v_w_hg_out, v_q_norm_g, v_k_norm_g, v_attn_sinks, v_w_att_out, v_w_out):
    xi, yi = lax.axis_index("x"), lax.axis_index("y")
    chip = 2 * xi + yi
    NS = w_in.shape[2]
    CS = conv_w.shape[2]
    MS = meta_tokens.shape[1]

    half = NS // 2
    w_in_t, m_w_in_t, v_w_in_t = (jnp.swapaxes(t, 1, 2) for t in (w_in, m_w_in, v_w_in))
    shards = [[w_in_t[l, :half].astype(BF), w_in_t[l, half:].astype(BF), w_conv_out[l].astype(BF),
               w_hg_out[l].astype(BF), w_att_out[l].astype(BF), w_out[l].astype(BF)] for l in range(DEPTH)]
    *first, g_meta, g_convw = _gather_first(shards[0][:2] + [meta_tokens, conv_w.reshape(DEPTH * CONV_WIDTH, CS)],
                                            [True, True, False, False])
    cols = lambda g: g.transpose(1, 0, 2).reshape(g.shape[1], -1)
    meta_f = cols(g_meta)
    convw_f = cols(g_convw).reshape(DEPTH, CONV_WIDTH, D_CONV)
    lb_all = _lb_fwd(hg_lower_bounds)

    def mats(gathered):
        g_wao, g_wbo, g_wco, g_wout = gathered
        return dict(w_ao=cols(g_wao), w_bo=cols(g_wbo), w_co=cols(g_wco), w_out=g_wout.reshape(D_MODEL, D_MODEL))

    def prep(l, gathered):
        p = _split_w_in(jnp.concatenate(gathered[:2], axis=1).reshape(4 * NS, D_MODEL))
        if len(gathered) > 2:
            p.update(mats(gathered[2:]))
        p.update(norm_g=norm_g[l:l + 1], conv_w=convw_f[l],
                 conv_vec=_pad8(jnp.stack([conv_b[l], conv_ln_g[l], conv_ln_b[l]])),
                 lb=lb_all[l:l + 1], gn4=jnp.tile(hg_norm_g[l:l + 1], (1, HG_HEADS)),
                 qg=jnp.tile(q_norm_g[l:l + 1], (1, ATT_Q_HEADS)), kg=jnp.tile(k_norm_g[l:l + 1], (1, ATT_KV_HEADS)),
                 sinks=attn_sinks[l:l + 1])
        return p

    shard_cols = lambda a: a.reshape(a.shape[0], 4, -1).transpose(1, 0, 2)
    def pack(g):
        win = _join_w_in(g).reshape(4, NS, D_MODEL).astype(BF)
        return [win[:, :half]], [win[:, half:], shard_cols(g["w_ao"]).astype(BF), shard_cols(g["w_bo"]).astype(BF),
                                 shard_cols(g["w_co"]).astype(BF), g["w_out"].reshape(4, MS, D_MODEL).astype(BF)]

    loss8, dh0, grads, parts = _local_step(x[0], loss_target[0], meta_f, [prep(0, first)], shards, prep, pack, mats)
    seq = x.shape[1]
    grad_x = dh0[CHUNK:CHUNK + seq][None]
    loss = lax.psum(loss8[0, 0], ("x", "y", "c"))

    sum4 = functools.partial(_sum4, name="sum_chips")
    mine = [jnp.concatenate([t for l in range(DEPTH) for t in (sum4(parts[l][0][0]), sum4(parts[l][1][0]))], axis=0)]
    mine += [jnp.concatenate([sum4(parts[l][1][a]) for l in range(DEPTH)], axis=0) for a in range(1, 5)]
    theirs = _swap_cores(mine)

    dlb_all = jnp.concatenate([grads[l]["hg_small"][0:1] for l in range(DEPTH)], axis=0)
    small = dict(
        norm_g=jnp.concatenate([grads[l]["norm_g"][0:1] for l in range(DEPTH)], axis=0).reshape(8, 512),
        meta=dh0[META_PAD:CHUNK].reshape(32, 512),
        conv_w=jnp.concatenate([grads[l]["conv_w"] for l in range(DEPTH)], axis=0),
        conv_b=jnp.concatenate([grads[l]["conv_vec"][0:1] for l in range(DEPTH)], axis=0),
        conv_ln_g=jnp.concatenate([grads[l]["conv_vec"][1:2] for l in range(DEPTH)], axis=0),
        conv_ln_b=jnp.concatenate([grads[l]["conv_vec"][2:3] for l in range(DEPTH)], axis=0),
        lb=_lb_bwd(hg_lower_bounds, dlb_all),
        hg_norm_g=jnp.concatenate([grads[l]["hg_small"][1:2].reshape(HG_HEADS, HG_D).sum(0, keepdims=True)
                                   for l in range(DEPTH)], axis=0),
        q_norm_g=jnp.stack([_attn_small(grads[l])[0] for l in range(DEPTH)]),
        k_norm_g=jnp.stack([_attn_small(grads[l])[1] for l in range(DEPTH)]),
        sinks=jnp.stack([_attn_small(grads[l])[2] for l in range(DEPTH)]),
    )
    gsum = _allsum_small(_pack_small(small))
    off, _ = _small_offsets()

    def take(name, rows, cols):
        o, _ = off[name]
        return gsum[o:o + rows, 0:cols]

    g_meta_full = take("meta", 32, 512).reshape(N_META, D_MODEL)
    g_convw_full = take("conv_w", 32 * DEPTH, 512).reshape(DEPTH, 32, 512)[:, :CONV_WIDTH]
    small_grads = dict(
        norm_g=take("norm_g", 8, 512),
        meta=lax.dynamic_slice_in_dim(g_meta_full, chip * MS, MS, axis=1),
        conv_w=lax.dynamic_slice_in_dim(g_convw_full, chip * CS, CS, axis=2).reshape(DEPTH * CONV_WIDTH, CS),
        conv_b=take("conv_b", DEPTH, 512), conv_ln_g=take("conv_ln_g", DEPTH, 512), conv_ln_b=take("conv_ln_b", DEPTH, 512),
        lb=take("lb", DEPTH, 512), hg_norm_g=take("hg_norm_g", DEPTH, HG_D), q_norm_g=take("q_norm_g", DEPTH, ATT_HD),
        k_norm_g=take("k_norm_g", DEPTH, ATT_HD), sinks=take("sinks", DEPTH, ATT_Q_HEADS))

    def big_update(w, m, v, a, b, name):
        return _adamw(w, m, v, a.reshape(w.shape), b.reshape(w.shape), name)

    res = {}
    res["w_in"] = [jnp.swapaxes(t, 1, 2) for t in big_update(w_in_t, m_w_in_t, v_w_in_t, mine[0], theirs[0], "adamw_w_in")]
    res["w_conv_out"] = big_update(w_conv_out, m_w_conv_out, v_w_conv_out, mine[1], theirs[1], "adamw_w_ao")
    res["w_hg_out"] = big_update(w_hg_out, m_w_hg_out, v_w_hg_out, mine[2], theirs[2], "adamw_w_bo")
    res["w_att_out"] = big_update(w_att_out, m_w_att_out, v_w_att_out, mine[3], theirs[3], "adamw_w_co")
    res["w_out"] = big_update(w_out, m_w_out, v_w_out, mine[4], theirs[4], "adamw_w_out")

    small_w = dict(meta=(meta_tokens, m_meta_tokens, v_meta_tokens), norm_g=(norm_g, m_norm_g, v_norm_g),
                   conv_w=(conv_w, m_conv_w, v_conv_w), conv_b=(conv_b, m_conv_b, v_conv_b),
                   conv_ln_g=(conv_ln_g, m_conv_ln_g, v_conv_ln_g), conv_ln_b=(conv_ln_b, m_conv_ln_b, v_conv_ln_b),
                   lb=(hg_lower_bounds, m_hg_lower_bounds, v_hg_lower_bounds),
                   hg_norm_g=(hg_norm_g, m_hg_norm_g, v_hg_norm_g), q_norm_g=(q_norm_g, m_q_norm_g, v_q_norm_g),
                   k_norm_g=(k_norm_g, m_k_norm_g, v_k_norm_g), sinks=(attn_sinks, m_attn_sinks, v_attn_sinks))
    view = lambda n, t: t.reshape(-1, 512) if n == "norm_g" else t.reshape(-1, t.shape[-1])
    pw, pm, pv = (_pack_rows([view(n, small_w[n][k]) for n in small_w]) for k in range(3))
    pg = _pack_rows([small_grads[n] for n in small_w])
    packed = [t[0] for t in _adamw(pw[None], pm[None], pv[None], pg[None], None, "adamw_small")]
    o = 0
    for n in small_w:
        r, cdim = view(n, small_w[n][0]).shape
        res[n] = [t[o:o + r, 0:cdim].reshape(small_w[n][0].shape) for t in packed]
        o += -(-r // 8) * 8

    order = [("meta", None), ("norm_g", None), ("w_in", None), ("conv_w", None), ("conv_b", None), ("conv_ln_g", None),
             ("conv_ln_b", None), ("w_conv_out", None), ("lb", None), ("hg_norm_g", None), ("w_hg_out", None),
             ("q_norm_g", None), ("k_norm_g", None), ("sinks", None), ("w_att_out", None), ("w_out", None)]
    outs = [loss, grad_x]
    for k in range(4):
        outs += [res[n][k] for n, _ in order]
    return tuple(outs)


def _pack_rows(arrs):
    parts = []
    for a in arrs:
        r = (-a.shape[0]) % 8
        parts.append(jnp.pad(a, ((0, r), (0, 512 - a.shape[1]))))
    return jnp.concatenate(parts, axis=0)
```

```python
import functools

import jax
import jax.numpy as jnp
from jax import lax
from jax.experimental import pallas as pl
from jax.experimental.pallas import tpu as pltpu

F32 = jnp.float32
BF = jnp.bfloat16

D_MODEL = 1024
DEPTH = 4
CHUNK = 64
N_META = 16
META_PAD = CHUNK - N_META
D_CONV = 512
CONV_WIDTH = 31
HG_HEADS = 4
HG_D = 128
ATT_Q_HEADS = 8
ATT_KV_HEADS = 2
ATT_HD = 64
ATT_GROUP = ATT_Q_HEADS // ATT_KV_HEADS
EPS = 1e-6
F_FLOOR = 1e-30
NEG = -1e30

ADAM_LR = 0.001
ADAM_B1 = 0.9
ADAM_B2 = 0.999
ADAM_EPS = 1e-08
ADAM_WD = 0.01
ADAM_STEP = 10

TR = 640
TRM = TR // 2
CONV_RB = 32
QB = 128
HALO = 128
VMEM_LIMIT = 56 * 1024 * 1024

N_G, N_A, N_B, N_C = 3 * D_MODEL, 3 * D_CONV, 4 * 512, 2 * 512 + 2 * 128

MESH = pl.DeviceIdType.MESH


def _cp(sem=None, vmem=VMEM_LIMIT, **kw):
    if sem is None:
        return pltpu.CompilerParams(vmem_limit_bytes=vmem, **kw)
    return pltpu.CompilerParams(dimension_semantics=sem, vmem_limit_bytes=vmem, **kw)


def _nn(a, b):
    return lax.dot_general(a, b, (((1,), (0,)), ((), ())), preferred_element_type=F32)


def _nt(a, b):
    return lax.dot_general(a, b, (((1,), (1,)), ((), ())), preferred_element_type=F32)


def _tn(a, b):
    return lax.dot_general(a, b, (((0,), (0,)), ((), ())), preferred_element_type=F32)


def _sig(x):
    return jax.nn.sigmoid(x)


def _silu(x):
    return x * _sig(x)


def _silu_pair(x):
    s = _sig(x)
    return x * s, s * (1.0 + x * (1.0 - s))


def _mm_split(t, x):
    hi = x.astype(BF)
    lo = (x - hi.astype(F32)).astype(BF)
    return _nn(t, hi) + _nn(t, lo)


def _chunk_tri(n, upper):
    r = lax.broadcasted_iota(jnp.int32, (n, n), 0)
    c = lax.broadcasted_iota(jnp.int32, (n, n), 1)
    same = jnp.right_shift(r, 6) == jnp.right_shift(c, 6)
    tri = (c >= r) if upper else (c <= r)
    return jnp.where(same & tri, 1.0, 0.0).astype(BF)


def _matmul(a, b, *, ta=False, tb=False, out_dtype, tm, tn, tk, name, col_major_grid=False):
    if ta:
        K, M = a.shape
    else:
        M, K = a.shape
    N = b.shape[0] if tb else b.shape[1]
    assert M % tm == 0 and N % tn == 0 and K % tk == 0, (name, M, N, K, tm, tn, tk)
    nk = K // tk
    if col_major_grid:
        grid = (N // tn, M // tm, nk)
        ij = lambda g0, g1: (g1, g0)
    else:
        grid = (M // tm, N // tn, nk)
        ij = lambda g0, g1: (g0, g1)
    if ta:
        a_spec = pl.BlockSpec((tk, tm), lambda g0, g1, k: (k, ij(g0, g1)[0]))
    else:
        a_spec = pl.BlockSpec((tm, tk), lambda g0, g1, k: (ij(g0, g1)[0], k))
    if tb:
        b_spec = pl.BlockSpec((tn, tk), lambda g0, g1, k: (ij(g0, g1)[1], k))
    else:
        b_spec = pl.BlockSpec((tk, tn), lambda g0, g1, k: (k, ij(g0, g1)[1]))
    o_spec = pl.BlockSpec((tm, tn), lambda g0, g1, k: ij(g0, g1))
    dims = (((0 if ta else 1,), (1 if tb else 0,)), ((), ()))
    use_acc = nk > 1 and out_dtype != F32

    def body(a_ref, b_ref, o_ref, *scr):
        k = pl.program_id(2)
        p = lax.dot_general(a_ref[...].astype(BF), b_ref[...].astype(BF), dims, preferred_element_type=F32)
        if nk == 1:
            o_ref[...] = p.astype(out_dtype)
        else:
            acc = scr[0] if use_acc else o_ref

            @pl.when(k == 0)
            def _():
                acc[...] = p

            @pl.when(k > 0)
            def _():
                acc[...] += p

            if use_acc:
                @pl.when(k == nk - 1)
                def _():
                    o_ref[...] = acc[...].astype(out_dtype)

    return pl.pallas_call(
        body, name=name, grid=grid, in_specs=[a_spec, b_spec], out_specs=o_spec,
        out_shape=jax.ShapeDtypeStruct((M, N), out_dtype),
        scratch_shapes=[pltpu.VMEM((tm, tn), F32)] if use_acc else [],
        compiler_params=_cp(("parallel", "parallel", "arbitrary")),
    )(a, b)


def _dw_branches(ys, dzs, tk):
    Lp = ys[0].shape[0]

    def body(ya_ref, yb_ref, yc_ref, da_ref, db_ref, dc_ref, oa_ref, ob_ref, oc_ref):
        k = pl.program_id(0)
        for y_ref, d_ref, o_ref in ((ya_ref, da_ref, oa_ref), (yb_ref, db_ref, ob_ref), (yc_ref, dc_ref, oc_ref)):
            p = _tn(y_ref[...], d_ref[...])

            @pl.when(k == 0)
            def _():
                o_ref[...] = p

            @pl.when(k > 0)
            def _():
                o_ref[...] += p

    yspec = pl.BlockSpec((tk, 512), lambda k: (k, 0))
    dspec = pl.BlockSpec((tk, D_MODEL), lambda k: (k, 0))
    ospec = pl.BlockSpec((512, D_MODEL), lambda k: (0, 0))
    return pl.pallas_call(
        body, name="dw_branches", grid=(Lp // tk,), in_specs=[yspec] * 3 + [dspec] * 3, out_specs=[ospec] * 3,
        out_shape=[jax.ShapeDtypeStruct((512, D_MODEL), F32)] * 3, compiler_params=_cp(("arbitrary",)),
    )(*ys, *dzs)


def _rms_fwd(h, g):
    Lp = h.shape[0]

    def body(h_ref, g_ref, o_ref):
        x = h_ref[...]
        r = lax.rsqrt(jnp.mean(x * x, axis=-1, keepdims=True) + EPS)
        o_ref[...] = (x * r * g_ref[...]).astype(BF)

    return pl.pallas_call(
        body, name="rms_fwd", grid=(Lp // TR,),
        in_specs=[pl.BlockSpec((TR, D_MODEL), lambda i: (i, 0)), pl.BlockSpec((1, D_MODEL), lambda i: (0, 0))],
        out_specs=pl.BlockSpec((TR, D_MODEL), lambda i: (i, 0)),
        out_shape=jax.ShapeDtypeStruct((Lp, D_MODEL), BF),
        compiler_params=_cp(("parallel",)),
    )(h, g)


def _glu(ua, row):
    a = ua[:, 0:D_CONV].astype(F32)
    gl = ua[:, D_CONV:2 * D_CONV].astype(F32)
    return jnp.where(row >= META_PAD, a * _sig(gl), 0.0)


_SH_ROWS = TR + CHUNK - 8


def _fill_shifts(src, sh):
    for b in range(1, 8):
        sh[b - 1] = src[pl.ds(b, _SH_ROWS), :]


def _shifted(src, sh, start, n):
    b = start % 8
    if b == 0:
        return src[pl.ds(start, n), :]
    return sh[b - 1, pl.ds(start - b, n), :]


def _conv_fwd(ua, cw, cvec, carry=None):
    Lp = ua.shape[0]
    nt = Lp // TR
    hb = TR // CHUNK

    def body(cur_ref, halo_ref, w_ref, v_ref, ya_ref, yc_ref, ext, sh):
        i = pl.program_id(0)
        row = i * TR + lax.broadcasted_iota(jnp.int32, (TR, 1), 0)
        hrow = i * TR - CHUNK + lax.broadcasted_iota(jnp.int32, (CHUNK, 1), 0)
        ext[pl.ds(0, CHUNK), :] = jnp.where(i > 0, _glu(halo_ref[...], hrow), 0.0)
        ext[pl.ds(CHUNK, TR), :] = _glu(cur_ref[...], row)
        _fill_shifts(ext, sh)
        for rb in range(TR // CONV_RB):
            r0 = rb * CONV_RB
            rows = pl.ds(r0, CONV_RB)
            acc = jnp.zeros((CONV_RB, D_CONV), F32)
            for j in range(CONV_WIDTH):
                acc = acc + _shifted(ext, sh, r0 + CHUNK - (CONV_WIDTH - 1) + j, CONV_RB) * w_ref[j:j + 1, :]
            y = acc + v_ref[0:1, :]
            yc_ref[rows, :] = y
            mu = jnp.mean(y, axis=-1, keepdims=True)
            d = y - mu
            var = jnp.mean(d * d, axis=-1, keepdims=True)
            yn = d * lax.rsqrt(var + EPS) * v_ref[1:2, :] + v_ref[2:3, :]
            ya_ref[rows, :] = (_silu(yn) * _silu(cur_ref[rows, 2 * D_CONV:3 * D_CONV].astype(F32))).astype(BF)

    in_specs = [pl.BlockSpec((TR, N_A), lambda i: (i, 0)),
                pl.BlockSpec((CHUNK, N_A), lambda i: (jnp.maximum(i * hb - 1, 0), 0)),
                pl.BlockSpec((CONV_WIDTH, D_CONV), lambda i: (0, 0)),
                pl.BlockSpec((8, D_CONV), lambda i: (0, 0))]
    out_specs = [pl.BlockSpec((TR, D_CONV), lambda i: (i, 0)), pl.BlockSpec((TR, D_CONV), lambda i: (i, 0))]
    out_shape = [jax.ShapeDtypeStruct((Lp, D_CONV), BF), jax.ShapeDtypeStruct((Lp, D_CONV), F32)]
    scratch = [pltpu.VMEM((TR + CHUNK, D_CONV), F32), pltpu.VMEM((7, _SH_ROWS, D_CONV), F32)]
    return _call_carrying(body, "conv_fwd", nt, in_specs, out_specs, out_shape, scratch, (ua, ua, cw, cvec), carry)


def _conv_bwd(ua, yconv, dya, cw, cvec):
    Lp = ua.shape[0]
    nt = Lp // TR
    hb = TR // CHUNK
    nhb = Lp // CHUNK

    def ln_bwd(y, dout, gate, v_ref):
        mu = jnp.mean(y, axis=-1, keepdims=True)
        d = y - mu
        var = jnp.mean(d * d, axis=-1, keepdims=True)
        rstd = lax.rsqrt(var + EPS)
        xhat = d * rstd
        yn = xhat * v_ref[1:2, :] + v_ref[2:3, :]
        s_gate, ds_gate = _silu_pair(gate)
        s_yn, ds_yn = _silu_pair(yn)
        dyn = dout * s_gate * ds_yn
        dxh = dyn * v_ref[1:2, :]
        dyc = rstd * (dxh - jnp.mean(dxh, axis=-1, keepdims=True) - xhat * jnp.mean(dxh * xhat, axis=-1, keepdims=True))
        return dyc, dyn, xhat, dout * s_yn * ds_gate

    def body(cur_ref, prev_ref, next_ref, yc_ref, ycn_ref, dy_ref, dyn_ref, w_ref, v_ref,
             du_ref, dw_ref, dv_ref, uext, dext, dwacc, ush, dsh):
        i = pl.program_id(0)

        @pl.when(i == 0)
        def _():
            dwacc[...] = jnp.zeros_like(dwacc)
            dv_ref[...] = jnp.zeros_like(dv_ref)

        row = i * TR + lax.broadcasted_iota(jnp.int32, (TR, 1), 0)
        hrow = i * TR - CHUNK + lax.broadcasted_iota(jnp.int32, (CHUNK, 1), 0)
        uext[pl.ds(0, CHUNK), :] = jnp.where(i > 0, _glu(prev_ref[...], hrow), 0.0)
        uext[pl.ds(CHUNK, TR), :] = _glu(cur_ref[...], row)

        s_b = jnp.zeros((1, D_CONV), F32)
        s_g = jnp.zeros((1, D_CONV), F32)
        s_bb = jnp.zeros((1, D_CONV), F32)
        for rb in range(TR // CONV_RB):
            rows = pl.ds(rb * CONV_RB, CONV_RB)
            gate = cur_ref[rows, 2 * D_CONV:3 * D_CONV].astype(F32)
            dout = dy_ref[rows, :].astype(F32)
            dyc, dyn, xhat, dgate = ln_bwd(yc_ref[rows, :], dout, gate, v_ref)
            du_ref[rows, 2 * D_CONV:3 * D_CONV] = dgate.astype(BF)
            dext[rows, :] = dyc
            s_b = s_b + jnp.sum(dyc, axis=0, keepdims=True)
            s_g = s_g + jnp.sum(dyn * xhat, axis=0, keepdims=True)
            s_bb = s_bb + jnp.sum(dyn, axis=0, keepdims=True)
        dv_ref[0:1, :] += s_b
        dv_ref[1:2, :] += s_g
        dv_ref[2:3, :] += s_bb
        dyc_n, _, _, _ = ln_bwd(ycn_ref[...], dyn_ref[...].astype(F32),
                                next_ref[:, 2 * D_CONV:3 * D_CONV].astype(F32), v_ref)
        dext[pl.ds(TR, CHUNK), :] = jnp.where(i < nt - 1, dyc_n, 0.0)
        _fill_shifts(uext, ush)
        _fill_shifts(dext, dsh)

        for rb in range(TR // CONV_RB):
            r0 = rb * CONV_RB
            rows = pl.ds(r0, CONV_RB)
            d_blk = dext[rows, :]
            dglu = jnp.zeros((CONV_RB, D_CONV), F32)
            for j in range(CONV_WIDTH):
                dglu = dglu + _shifted(dext, dsh, r0 + CONV_WIDTH - 1 - j, CONV_RB) * w_ref[j:j + 1, :]
                prod = d_blk * _shifted(uext, ush, r0 + CHUNK - (CONV_WIDTH - 1) + j, CONV_RB)
                part = prod[0:8, :]
                for s in range(1, CONV_RB // 8):
                    part = part + prod[8 * s:8 * s + 8, :]
                dwacc[j] += part
            a = cur_ref[rows, 0:D_CONV].astype(F32)
            sg = _sig(cur_ref[rows, D_CONV:2 * D_CONV].astype(F32))
            grow = i * TR + r0 + lax.broadcasted_iota(jnp.int32, (CONV_RB, 1), 0)
            dglu = jnp.where(grow >= META_PAD, dglu, 0.0)
            du_ref[rows, 0:D_CONV] = (dglu * sg).astype(BF)
            du_ref[rows, D_CONV:2 * D_CONV] = (dglu * a * sg * (1.0 - sg)).astype(BF)

        @pl.when(i == nt - 1)
        def _():
            dw_ref[...] = jnp.sum(dwacc[...], axis=1)

    nxt = lambda i: (jnp.minimum(i * hb + hb, nhb - 1), 0)
    return pl.pallas_call(
        body, name="conv_bwd", grid=(nt,),
        in_specs=[pl.BlockSpec((TR, N_A), lambda i: (i, 0)),
                  pl.BlockSpec((CHUNK, N_A), lambda i: (jnp.maximum(i * hb - 1, 0), 0)),
                  pl.BlockSpec((CHUNK, N_A), nxt),
                  pl.BlockSpec((TR, D_CONV), lambda i: (i, 0)),
                  pl.BlockSpec((CHUNK, D_CONV), nxt),
                  pl.BlockSpec((TR, D_CONV), lambda i: (i, 0)),
                  pl.BlockSpec((CHUNK, D_CONV), nxt),
                  pl.BlockSpec((CONV_WIDTH, D_CONV), lambda i: (0, 0)),
                  pl.BlockSpec((8, D_CONV), lambda i: (0, 0))],
        out_specs=[pl.BlockSpec((TR, N_A), lambda i: (i, 0)),
                   pl.BlockSpec((32, D_CONV), lambda i: (0, 0)),
                   pl.BlockSpec((8, D_CONV), lambda i: (0, 0))],
        out_shape=[jax.ShapeDtypeStruct((Lp, N_A), BF), jax.ShapeDtypeStruct((32, D_CONV), F32),
                   jax.ShapeDtypeStruct((8, D_CONV), F32)],
        scratch_shapes=[pltpu.VMEM((TR + CHUNK, D_CONV), F32), pltpu.VMEM((TR + CHUNK, D_CONV), F32),
                        pltpu.VMEM((32, 8, D_CONV), F32), pltpu.VMEM((7, _SH_ROWS, D_CONV), F32),
                        pltpu.VMEM((7, _SH_ROWS, D_CONV), F32)],
        compiler_params=_cp(("arbitrary",)),
    )(ua, ua, ua, yconv, yconv, dya, dya, cw, cvec)


def _hg_gates(ub_ref, lbv, row):
    q = ub_ref[:, 0:512].astype(F32)
    z = ub_ref[:, 512:1024].astype(F32)
    valid = row >= META_PAD
    sig = _sig(z)
    f = lbv + (1.0 - lbv) * sig
    g = jnp.where(valid, jnp.log(jnp.maximum(f, F_FLOOR)), 0.0)
    k = jnp.where(valid, (1.0 - lbv) * (1.0 - sig), 0.0)
    return q, k, g, sig, f


def _hg_chunk_terms(b_c, q_c, k_c):
    bm = b_c[CHUNK // 2 - 1:CHUNK // 2, :]
    bl = b_c[CHUNK - 1:CHUNK, :]
    e1 = jnp.exp(b_c - bm)
    e2 = jnp.exp(bm - b_c)
    e0 = jnp.exp(b_c)
    e3 = jnp.exp(bl - b_c)
    el = jnp.exp(bl)
    return e1, e2, e0, e3, el, q_c * e1, k_c * e2, q_c * e0, k_c * e3


def _hg_fwd(ub, lb, gn4, carry=None):
    Lp = ub.shape[0]
    nt = Lp // TR
    cpt = TR // CHUNK

    def body(ub_ref, lb_ref, gn_ref, yb_ref, o_ref, ss_ref, st, bsc, qsc, ksc, qes, els, ust, tlo):
        i = pl.program_id(0)

        @pl.when(i == 0)
        def _():
            st[...] = jnp.zeros_like(st)
            tlo[...] = _chunk_tri(TR, False)

        row = i * TR + lax.broadcasted_iota(jnp.int32, (TR, 1), 0)
        q, k, g, _, _ = _hg_gates(ub_ref, lb_ref[...], row)
        qsc[...] = _silu(q)
        ksc[...] = k
        bsc[...] = _mm_split(tlo[...], g)
        tri = lax.broadcasted_iota(jnp.int32, (CHUNK, CHUNK), 1) <= lax.broadcasted_iota(jnp.int32, (CHUNK, CHUNK), 0)

        def intra(c, carry):
            rows = pl.ds(pl.multiple_of(c * CHUNK, CHUNK), CHUNK)
            _, _, _, _, el, qe, ke, qE, kd = _hg_chunk_terms(bsc[rows, :], qsc[rows, :], ksc[rows, :])
            qe, ke, kd = qe.astype(BF), ke.astype(BF), kd.astype(BF)
            qes[rows, :] = qE.astype(BF)
            els[c] = jnp.broadcast_to(el, (8, 512))
            sls = [slice(HG_D * h, HG_D * (h + 1)) for h in range(HG_HEADS)]
            v = [ub_ref[rows, 1024 + HG_D * h:1024 + HG_D * (h + 1)] for h in range(HG_HEADS)]
            a = [_nt(qe[:, sl], ke[:, sl]) for sl in sls]
            u = [_tn(v[h], kd[:, sls[h]]) for h in range(HG_HEADS)]
            a = [jnp.where(tri, x, 0.0).astype(BF) for x in a]
            oi = [_nn(a[h], v[h]) for h in range(HG_HEADS)]
            for h in range(HG_HEADS):
                ust[c, h] = u[h]
                o_ref[rows, sls[h]] = oi[h]
            return carry

        lax.fori_loop(0, cpt, intra, 0, unroll=2)

        for h in range(HG_HEADS):
            sl = slice(HG_D * h, HG_D * (h + 1))
            s = st[h]
            for c in range(cpt):
                ss_ref[c, h] = s
                s = els[c, 0:1, sl] * s + ust[c, h]
            st[h] = s

        def inter(c, carry):
            rows = pl.ds(pl.multiple_of(c * CHUNK, CHUNK), CHUNK)
            for h in range(HG_HEADS):
                sl = slice(HG_D * h, HG_D * (h + 1))
                o_ref[rows, sl] += _nt(qes[rows, sl], ss_ref[c, h].astype(BF))
            return carry

        lax.fori_loop(0, cpt, inter, 0, unroll=2)

        gate = ub_ref[:, 1536:2048].astype(F32)
        for h in range(HG_HEADS):
            sl = slice(HG_D * h, HG_D * (h + 1))
            o = o_ref[:, sl]
            r = lax.rsqrt(jnp.mean(o * o, axis=-1, keepdims=True) + EPS)
            yb_ref[:, sl] = (o * r * gn_ref[:, sl] * _silu(gate[:, sl])).astype(BF)

    in_specs = [pl.BlockSpec((TR, N_B), lambda i: (i, 0)), pl.BlockSpec((1, 512), lambda i: (0, 0)),
                pl.BlockSpec((1, 512), lambda i: (0, 0))]
    out_specs = [pl.BlockSpec((TR, 512), lambda i: (i, 0)), pl.BlockSpec((TR, 512), lambda i: (i, 0)),
                 pl.BlockSpec((cpt, HG_HEADS, HG_D, HG_D), lambda i: (i, 0, 0, 0))]
    out_shape = [jax.ShapeDtypeStruct((Lp, 512), BF), jax.ShapeDtypeStruct((Lp, 512), F32),
                 jax.ShapeDtypeStruct((Lp // CHUNK, HG_HEADS, HG_D, HG_D), F32)]
    scratch = [pltpu.VMEM((HG_HEADS, HG_D, HG_D), F32), pltpu.VMEM((TR, 512), F32),
               pltpu.VMEM((TR, 512), F32), pltpu.VMEM((TR, 512), F32), pltpu.VMEM((TR, 512), BF),
               pltpu.VMEM((cpt, 8, 512), F32), pltpu.VMEM((cpt, HG_HEADS, HG_D, HG_D), F32), pltpu.VMEM((TR, TR), BF)]
    return _call_carrying(body, "hgrn_fwd", nt, in_specs, out_specs, out_shape, scratch, (ub, lb, gn4), carry)


def _hg_bwd(ub, lb, gn4, o_save, s_save, dyb, carry=None):
    Lp = ub.shape[0]
    nt = Lp // TR
    cpt = TR // CHUNK

    def body(ub_ref, lb_ref, gn_ref, o_ref, ss_ref, dy_ref, du_ref, ds_ref,
             dst, bsc, qsc, ksc, dosc, dqsc, dksc, dbsc, els, ust, dss, tlo, tup):
        i = pl.program_id(0)
        t = nt - 1 - i

        @pl.when(i == 0)
        def _():
            dst[...] = jnp.zeros_like(dst)
            ds_ref[...] = jnp.zeros_like(ds_ref)
            tlo[...] = _chunk_tri(TR, False)
            tup[...] = _chunk_tri(TR, True)

        lbv = lb_ref[...]
        row = t * TR + lax.broadcasted_iota(jnp.int32, (TR, 1), 0)
        valid = row >= META_PAD
        q, k, g, sig, f = _hg_gates(ub_ref, lbv, row)
        silu_q, dsilu_q = _silu_pair(q)
        qsc[...] = silu_q
        ksc[...] = k
        bsc[...] = _mm_split(tlo[...], g)

        gate = ub_ref[:, 1536:2048].astype(F32)
        dy = dy_ref[...].astype(F32)
        for h in range(HG_HEADS):
            sl = slice(HG_D * h, HG_D * (h + 1))
            o = o_ref[:, sl]
            r = lax.rsqrt(jnp.mean(o * o, axis=-1, keepdims=True) + EPS)
            ohat = o * r
            silu_g, dsilu_g = _silu_pair(gate[:, sl])
            don = dy[:, sl] * silu_g
            du_ref[:, 1536 + HG_D * h:1536 + HG_D * (h + 1)] = (dy[:, sl] * ohat * gn_ref[:, sl] * dsilu_g).astype(BF)
            ds_ref[1:2, sl] += jnp.sum(don * ohat, axis=0, keepdims=True)
            gd = don * gn_ref[:, sl]
            dosc[:, sl] = r * (gd - ohat * jnp.mean(gd * ohat, axis=-1, keepdims=True))

        tri = lax.broadcasted_iota(jnp.int32, (CHUNK, CHUNK), 1) <= lax.broadcasted_iota(jnp.int32, (CHUNK, CHUNK), 0)
        last = lax.broadcasted_iota(jnp.int32, (CHUNK, 1), 0) == CHUNK - 1

        def incr(c, carry):
            rows = pl.ds(pl.multiple_of(c * CHUNK, CHUNK), CHUNK)
            b_c = bsc[rows, :]
            qE_b = (qsc[rows, :] * jnp.exp(b_c)).astype(BF)
            els[c] = jnp.broadcast_to(jnp.exp(b_c[CHUNK - 1:CHUNK, :]), (8, 512))
            do_c = dosc[rows, :].astype(BF)
            for h in range(HG_HEADS):
                sl = slice(HG_D * h, HG_D * (h + 1))
                ust[c, h] = _tn(do_c[:, sl], qE_b[:, sl])
            return carry

        lax.fori_loop(0, cpt, incr, 0, unroll=2)

        for h in range(HG_HEADS):
            sl = slice(HG_D * h, HG_D * (h + 1))
            d_s = dst[h]
            for c in reversed(range(cpt)):
                dss[c, h] = d_s
                d_s = els[c, 0:1, sl] * d_s + ust[c, h]
            dst[h] = d_s

        def chunk(c, carry):
            r0 = pl.multiple_of(c * CHUNK, CHUNK)
            rows = pl.ds(r0, CHUNK)
            e1, e2, e0, e3, el, qe, ke, qE, kd = _hg_chunk_terms(bsc[rows, :], qsc[rows, :], ksc[rows, :])
            qe_b, ke_b, kd_b = qe.astype(BF), ke.astype(BF), kd.astype(BF)
            do_c = dosc[rows, :].astype(BF)
            hs = range(HG_HEADS)
            sls = [slice(HG_D * h, HG_D * (h + 1)) for h in hs]
            v = [ub_ref[rows, 1024 + HG_D * h:1024 + HG_D * (h + 1)] for h in hs]
            do = [do_c[:, sl] for sl in sls]
            a = [_nt(qe_b[:, sl], ke_b[:, sl]) for sl in sls]
            da = [_nt(do[h], v[h]) for h in hs]
            dqE = [_nn(do[h], ss_ref[c, h].astype(BF)) for h in hs]
            dkd = [_nn(v[h], dss[c, h].astype(BF)) for h in hs]
            dv2 = [_nt(kd_b[:, sls[h]], dss[c, h].astype(BF)) for h in hs]
            a = [jnp.where(tri, x, 0.0).astype(BF) for x in a]
            da = [jnp.where(tri, x, 0.0).astype(BF) for x in da]
            dv = [_tn(a[h], do[h]) + dv2[h] for h in hs]
            dqe = [_nn(da[h], ke_b[:, sls[h]]) for h in hs]
            dke = [_tn(da[h], qe_b[:, sls[h]]) for h in hs]
            for h in hs:
                sl = sls[h]
                del_h = jnp.sum(ss_ref[c, h] * dss[c, h], axis=0, keepdims=True)
                dqsc[rows, sl] = dqE[h] * e0[:, sl] + dqe[h] * e1[:, sl]
                dksc[rows, sl] = dke[h] * e2[:, sl] + dkd[h] * e3[:, sl]
                tkd = dkd[h] * kd[:, sl]
                dbl = jnp.sum(tkd, axis=0, keepdims=True) + del_h * el[:, sl]
                dbsc[rows, sl] = (dqE[h] * qE[:, sl] + dqe[h] * qe[:, sl] - dke[h] * ke[:, sl] - tkd
                                  + jnp.where(last, dbl, 0.0))
                du_ref[rows, 1024 + HG_D * h:1024 + HG_D * (h + 1)] = dv[h].astype(BF)
            return carry

        lax.fori_loop(0, cpt, chunk, 0, unroll=2)

        dg = _mm_split(tup[...], dbsc[...])
        df = jnp.where(valid & (f > F_FLOOR), dg / f, 0.0)
        dk = jnp.where(valid, dksc[...], 0.0)
        dsig = (df - dk) * (1.0 - lbv)
        ds_ref[0:1, :] += jnp.sum((df - dk) * (1.0 - sig), axis=0, keepdims=True)
        du_ref[:, 512:1024] = (dsig * sig * (1.0 - sig)).astype(BF)
        du_ref[:, 0:512] = (dqsc[...] * dsilu_q).astype(BF)

    rev = lambda i: (nt - 1 - i, 0)
    in_specs = [pl.BlockSpec((TR, N_B), rev), pl.BlockSpec((1, 512), lambda i: (0, 0)),
                pl.BlockSpec((1, 512), lambda i: (0, 0)), pl.BlockSpec((TR, 512), rev),
                pl.BlockSpec((cpt, HG_HEADS, HG_D, HG_D), lambda i: (nt - 1 - i, 0, 0, 0)),
                pl.BlockSpec((TR, 512), rev)]
    out_specs = [pl.BlockSpec((TR, N_B), rev), pl.BlockSpec((8, 512), lambda i: (0, 0))]
    out_shape = [jax.ShapeDtypeStruct((Lp, N_B), BF), jax.ShapeDtypeStruct((8, 512), F32)]
    states = pltpu.VMEM((cpt, HG_HEADS, HG_D, HG_D), F32)
    scratch = ([pltpu.VMEM((HG_HEADS, HG_D, HG_D), F32)] + [pltpu.VMEM((TR, 512), F32)] * 7
               + [pltpu.VMEM((cpt, 8, 512), F32), states, states, pltpu.VMEM((TR, TR), BF), pltpu.VMEM((TR, TR), BF)])
    return _call_carrying(body, "hgrn_bwd", nt, in_specs, out_specs, out_shape, scratch,
                          (ub, lb, gn4, o_save, s_save, dyb), carry)


_KCOL = (2 * 512) // 128
_VCOL = _KCOL + 1


def _swa_in_specs(nt, rev):
    tile = (lambda i: nt - 1 - i) if rev else (lambda i: i)
    hpt = TR // HALO
    return [
        pl.BlockSpec((TR, 512), lambda i: (tile(i), 0)),
        pl.BlockSpec((TR, 512), lambda i: (tile(i), 1)),
        pl.BlockSpec((TR, 128), lambda i: (tile(i), _KCOL)),
        pl.BlockSpec((TR, 128), lambda i: (tile(i), _VCOL)),
        pl.BlockSpec((HALO, 128), lambda i: (jnp.maximum(tile(i) * hpt - 1, 0), _KCOL)),
        pl.BlockSpec((HALO, 128), lambda i: (jnp.maximum(tile(i) * hpt - 1, 0), _VCOL)),
        pl.BlockSpec((CHUNK, 128), lambda i: (0, _KCOL)),
        pl.BlockSpec((CHUNK, 128), lambda i: (0, _VCOL)),
        pl.BlockSpec((1, 512), lambda i: (0, 0)),
        pl.BlockSpec((1, 128), lambda i: (0, 0)),
        pl.BlockSpec((1, ATT_Q_HEADS), lambda i: (0, 0)),
    ]


_WROWS = 2 * CHUNK + HALO + TR
_W0 = 2 * CHUNK
_C0 = _W0 + HALO
_SCALE = ATT_HD ** -0.5


def _group_ones(n):
    r = lax.broadcasted_iota(jnp.int32, (n, n), 0)
    c = lax.broadcasted_iota(jnp.int32, (n, n), 1)
    return jnp.where(jnp.right_shift(r, 6) == jnp.right_shift(c, 6), 1.0, 0.0).astype(BF)


def _group_mean(x, ones):
    hi = x.astype(BF)
    lo = (x - hi.astype(F32)).astype(BF)
    return (_nn(hi, ones) + _nn(lo, ones)) * (1.0 / ATT_HD)


def _head_rms(x, ones):
    r = lax.rsqrt(_group_mean(x * x, ones) + EPS)
    return x * r, r


def _swa_windows(kc_ref, vc_ref, kh_ref, vh_ref, km_ref, vm_ref, kg2, ones, kwin, krwin, vwin, vrwin):
    meta = pl.ds(META_PAD, N_META)
    for (k, v, r0, n) in ((km_ref[meta, :], vm_ref[meta, :], 0, N_META), (kh_ref[...], vh_ref[...], _W0, HALO),
                          (kc_ref[...], vc_ref[...], _C0, TR)):
        xhat, _ = _head_rms(k.astype(F32), ones)
        kn = xhat * kg2
        kwin[pl.ds(r0, n), :] = kn.astype(BF)
        krwin[pl.ds(r0, n), :] = pltpu.roll(kn, ATT_HD, 1).astype(BF)
        vwin[pl.ds(r0, n), :] = v
        if vrwin is not None:
            vrwin[pl.ds(r0, n), :] = pltpu.roll(v.astype(F32), ATT_HD, 1).astype(BF)
    zero = jnp.zeros((_W0 - N_META, 128), BF)
    for w in (kwin, krwin, vwin, vrwin):
        if w is not None:
            w[pl.ds(N_META, _W0 - N_META), :] = zero


def _swa_masks_t(t, qb):
    q0 = t * TR + qb * QB
    qc = jnp.right_shift(q0 + lax.broadcasted_iota(jnp.int32, (1, QB), 1), 6)
    kabs = q0 - HALO + lax.broadcasted_iota(jnp.int32, (QB + HALO, 1), 0)
    kc = jnp.right_shift(kabs + HALO, 6) - HALO // CHUNK
    mask_w = (kc <= qc) & (kc >= qc - 2) & (kabs >= META_PAD)
    return qc > 2, mask_w


def _swa_park(dtype):
    return [pltpu.VMEM((ATT_Q_HEADS, N_META, QB), dtype), pltpu.VMEM((ATT_Q_HEADS, QB + HALO, QB), dtype)]


def _swa_ones():
    return [pltpu.VMEM((128, 128), BF), pltpu.VMEM((512, 512), BF)]


def _split_heads(x, lane_hi):
    return jnp.where(lane_hi, 0.0, x).astype(BF), jnp.where(lane_hi, x, 0.0).astype(BF)


def _call_carrying(body, name, nt, in_specs, out_specs, out_shape, scratch, args, carry):
    if carry is None:
        return pl.pallas_call(body, name=name, grid=(nt,), in_specs=in_specs, out_specs=out_specs, out_shape=out_shape,
                              scratch_shapes=scratch, compiler_params=_cp(("arbitrary",)))(*args)
    kind, arrs = carry
    n = len(arrs)
    return pl.pallas_call(
        _carry_exchange(body, len(in_specs), len(out_specs), nt, kind, n), name=name + "_" + kind, grid=(nt,),
        in_specs=in_specs + [_ANY] * n, out_specs=out_specs + [_ANY] * n,
        out_shape=out_shape + _exchange_out_shapes(kind, arrs), scratch_shapes=scratch + _exchange_sems(n),
        compiler_params=_cp(("arbitrary",), has_side_effects=True),
    )(*args, *arrs)


def _swa_fwd(uc, qg8, kg2, sinks, carry=None):
    Lp = uc.shape[0]
    nt = Lp // TR
    nqb = TR // QB

    def body(q_ref, g_ref, kc_ref, vc_ref, kh_ref, vh_ref, km_ref, vm_ref, qg_ref, kg_ref, sk_ref,
             yc_ref, o_ref, lse_ref, kwin, krwin, vwin, vt, qlo, qhi, ot, s_m, s_w, p_m, p_w, g128, g512):
        t = pl.program_id(0)

        @pl.when(t == 0)
        def _():
            g128[...] = _group_ones(128)
            g512[...] = _group_ones(512)

        _swa_windows(kc_ref, vc_ref, kh_ref, vh_ref, km_ref, vm_ref, kg_ref[...], g128[...], kwin, krwin, vwin, None)
        vt[...] = vwin[...].T
        xhat, _ = _head_rms(q_ref[...].astype(F32), g512[...])
        lane_hi = (lax.broadcasted_iota(jnp.int32, (1, 512), 1) & ATT_HD) != 0
        lo, hi = _split_heads(xhat * qg_ref[...] * _SCALE, lane_hi)
        qlo[...] = lo
        qhi[...] = hi
        for qb in range(nqb):
            rows = pl.ds(qb * QB, QB)
            wrows = pl.ds(_W0 + qb * QB, QB + HALO)
            mrows = pl.ds(0, N_META)
            mask_m, mask_w = _swa_masks_t(t, qb)
            for j in range(ATT_Q_HEADS):
                p, e = j // 2, j % 2
                ks = kwin if e == j // ATT_GROUP else krwin
                qp = (qlo, qhi)[e][rows, 128 * p:128 * (p + 1)]
                s_m[j] = _nt(ks[mrows, :], qp)
                s_w[j] = _nt(ks[wrows, :], qp)
            inv = []
            for j in range(ATT_Q_HEADS):
                sm = jnp.where(mask_m, s_m[j], NEG)
                sw = jnp.where(mask_w, s_w[j], NEG)
                sink = sk_ref[:, j:j + 1]
                m = jnp.maximum(jnp.maximum(jnp.max(sm, axis=0, keepdims=True),
                                            jnp.max(sw, axis=0, keepdims=True)), sink)
                em = jnp.exp(sm - m)
                ew = jnp.exp(sw - m)
                den = jnp.sum(em, axis=0, keepdims=True) + jnp.sum(ew, axis=0, keepdims=True) + jnp.exp(sink - m)
                p_m[j] = em.astype(BF)
                p_w[j] = ew.astype(BF)
                lse_ref[j:j + 1, pl.ds(qb * QB, QB)] = m + jnp.log(den)
                inv.append(1.0 / den)
            for j in range(ATT_Q_HEADS):
                vrows = pl.ds(ATT_HD * (j // ATT_GROUP), ATT_HD)
                ot[pl.ds(ATT_HD * j, ATT_HD), pl.ds(qb * QB, QB)] = (
                    _nn(vt[vrows, pl.ds(0, N_META)], p_m[j])
                    + _nn(vt[vrows, pl.ds(_W0 + qb * QB, QB + HALO)], p_w[j])) * inv[j]
        o = ot[...].T
        o_ref[...] = o
        yc_ref[...] = (o * _silu(g_ref[...].astype(F32))).astype(BF)

    win = pltpu.VMEM((_WROWS, 128), BF)
    in_specs = _swa_in_specs(nt, False)
    out_specs = [pl.BlockSpec((TR, 512), lambda i: (i, 0)), pl.BlockSpec((TR, 512), lambda i: (i, 0)),
                 pl.BlockSpec((ATT_Q_HEADS, TR), lambda i: (0, i))]
    out_shape = [jax.ShapeDtypeStruct((Lp, 512), BF), jax.ShapeDtypeStruct((Lp, 512), F32),
                 jax.ShapeDtypeStruct((ATT_Q_HEADS, Lp), F32)]
    scratch = [win, win, win, pltpu.VMEM((128, _WROWS), BF), pltpu.VMEM((TR, 512), BF),
               pltpu.VMEM((TR, 512), BF), pltpu.VMEM((512, TR), F32)] + _swa_park(F32) + _swa_park(BF) + _swa_ones()
    return _call_carrying(body, "swa_fwd", nt, in_specs, out_specs, out_shape, scratch,
                          (uc, uc, uc, uc, uc, uc, uc, uc, qg8, kg2, sinks), carry)


def _swa_bwd(uc, qg8, kg2, sinks, o_save, lse, dyc):
    Lp = uc.shape[0]
    nt = Lp // TR
    nqb = TR // QB

    def body(q_ref, g_ref, kc_ref, vc_ref, kh_ref, vh_ref, km_ref, vm_ref, qg_ref, kg_ref, sk_ref,
             o_ref, lse_ref, dy_ref, du_ref, dg_ref, dsk_ref,
             kwin, krwin, vwin, vrwin, kt, krt, qlo, qhi, dolo, dohi, dqt, dk_dir, dk_rol, dv_dir, dv_rol,
             carry_k, carry_v, meta_k, meta_v, s_m, s_w, dp_m, dp_w, p_m, p_w, ds_m, ds_w, g128, g512):
        i = pl.program_id(0)
        t = nt - 1 - i

        @pl.when(i == 0)
        def _():
            carry_k[...] = jnp.zeros_like(carry_k)
            carry_v[...] = jnp.zeros_like(carry_v)
            meta_k[...] = jnp.zeros_like(meta_k)
            meta_v[...] = jnp.zeros_like(meta_v)
            dg_ref[...] = jnp.zeros_like(dg_ref)
            dsk_ref[...] = jnp.zeros_like(dsk_ref)
            g128[...] = _group_ones(128)
            g512[...] = _group_ones(512)

        ones128 = g128[...]
        ones512 = g512[...]
        _swa_windows(kc_ref, vc_ref, kh_ref, vh_ref, km_ref, vm_ref, kg_ref[...], ones128, kwin, krwin, vwin, vrwin)
        kt[...] = kwin[...].T
        krt[...] = krwin[...].T
        xhat_q, r_q = _head_rms(q_ref[...].astype(F32), ones512)
        lane_hi = (lax.broadcasted_iota(jnp.int32, (1, 512), 1) & ATT_HD) != 0
        lo, hi = _split_heads(xhat_q * qg_ref[...] * _SCALE, lane_hi)
        qlo[...] = lo
        qhi[...] = hi
        gate = g_ref[...].astype(F32)
        dy = dy_ref[...].astype(F32)
        silu_g, dsilu_g = _silu_pair(gate)
        do = dy * silu_g
        o = o_ref[...]
        du_ref[:, 512:1024] = (dy * o * dsilu_g).astype(BF)
        lo, hi = _split_heads(do, lane_hi)
        dolo[...] = lo
        dohi[...] = hi
        hsel = jnp.where(jnp.right_shift(lax.broadcasted_iota(jnp.int32, (ATT_Q_HEADS, 512), 1), 6)
                         == lax.broadcasted_iota(jnp.int32, (ATT_Q_HEADS, 512), 0), 1.0, 0.0).astype(BF)
        prod = do * o
        p_hi = prod.astype(BF)
        d_t = _nt(hsel, p_hi) + _nt(hsel, (prod - p_hi.astype(F32)).astype(BF))
        for acc in (dk_dir, dk_rol, dv_dir, dv_rol):
            acc[...] = jnp.zeros_like(acc)

        for qb in range(nqb):
            rows = pl.ds(qb * QB, QB)
            qcols = pl.ds(qb * QB, QB)
            wrows = pl.ds(_W0 + qb * QB, QB + HALO)
            mrows = pl.ds(0, N_META)
            mask_m, mask_w = _swa_masks_t(t, qb)
            for j in range(ATT_Q_HEADS):
                p, e = j // 2, j % 2
                ks, vs = (kwin, vwin) if e == j // ATT_GROUP else (krwin, vrwin)
                pair = slice(128 * p, 128 * (p + 1))
                qp = (qlo, qhi)[e][rows, pair]
                dop = (dolo, dohi)[e][rows, pair]
                s_m[j] = _nt(ks[mrows, :], qp)
                s_w[j] = _nt(ks[wrows, :], qp)
                dp_m[j] = _nt(vs[mrows, :], dop)
                dp_w[j] = _nt(vs[wrows, :], dop)
            for j in range(ATT_Q_HEADS):
                lse_j = lse_ref[j:j + 1, qcols]
                d_j = d_t[j:j + 1, qb * QB:(qb + 1) * QB]
                em = jnp.exp(jnp.where(mask_m, s_m[j], NEG) - lse_j)
                ew = jnp.exp(jnp.where(mask_w, s_w[j], NEG) - lse_j)
                p_m[j] = em.astype(BF)
                p_w[j] = ew.astype(BF)
                ds_m[j] = (em * (dp_m[j] - d_j)).astype(BF)
                ds_w[j] = (ew * (dp_w[j] - d_j)).astype(BF)
                dsk_ref[j:j + 1, :] -= jnp.exp(sk_ref[:, j:j + 1] - lse_j) * d_j
            for j in range(ATT_Q_HEADS):
                e = j % 2
                ktr = kt if e == j // ATT_GROUP else krt
                hrows = pl.ds(ATT_HD * e, ATT_HD)
                dqt[pl.ds(ATT_HD * j, ATT_HD), qcols] = (_nn(ktr[hrows, pl.ds(0, N_META)], ds_m[j])
                                                         + _nn(ktr[hrows, pl.ds(_W0 + qb * QB, QB + HALO)], ds_w[j]))
            for direct, dk_acc, dv_acc in ((True, dk_dir, dv_dir), (False, dk_rol, dv_rol)):
                heads = [j for j in range(ATT_Q_HEADS) if (j % 2 == j // ATT_GROUP) == direct]
                q_cat = jnp.concatenate([(qlo, qhi)[j % 2][rows, 128 * (j // 2):128 * (j // 2 + 1)] for j in heads], axis=0)
                do_cat = jnp.concatenate([(dolo, dohi)[j % 2][rows, 128 * (j // 2):128 * (j // 2 + 1)] for j in heads], axis=0)
                dk_acc[mrows, :] += _nn(jnp.concatenate([ds_m[j] for j in heads], axis=1), q_cat)
                dk_acc[wrows, :] += _nn(jnp.concatenate([ds_w[j] for j in heads], axis=1), q_cat)
                dv_acc[mrows, :] += _nn(jnp.concatenate([p_m[j] for j in heads], axis=1), do_cat)
                dv_acc[wrows, :] += _nn(jnp.concatenate([p_w[j] for j in heads], axis=1), do_cat)

        dk_dir[...] += pltpu.roll(dk_rol[...], ATT_HD, 1)
        dv_dir[...] += pltpu.roll(dv_rol[...], ATT_HD, 1)
        meta_k[...] += dk_dir[pl.ds(0, N_META), :]
        meta_v[...] += dv_dir[pl.ds(0, N_META), :]
        first = jnp.where(t == 0, 1.0, 0.0)
        dk_dir[pl.ds(_C0 + TR - HALO, HALO), :] += carry_k[...]
        dv_dir[pl.ds(_C0 + TR - HALO, HALO), :] += carry_v[...]
        dk_dir[pl.ds(_C0 + META_PAD, N_META), :] += first * meta_k[...]
        dv_dir[pl.ds(_C0 + META_PAD, N_META), :] += first * meta_v[...]
        carry_k[...] = dk_dir[pl.ds(_W0, HALO), :]
        carry_v[...] = dv_dir[pl.ds(_W0, HALO), :]

        du_ref[:, 1152:1280] = dv_dir[pl.ds(_C0, TR), :].astype(BF)
        xhat_k, r_k = _head_rms(kc_ref[...].astype(F32), ones128)
        dkn = dk_dir[pl.ds(_C0, TR), :]
        dg_ref[1:2, 0:128] += jnp.sum(dkn * xhat_k, axis=0, keepdims=True)
        gd = dkn * kg_ref[...]
        du_ref[:, 1024:1152] = (r_k * (gd - xhat_k * _group_mean(gd * xhat_k, ones128))).astype(BF)
        dqn = dqt[...].T * _SCALE
        dg_ref[0:1, :] += jnp.sum(dqn * xhat_q, axis=0, keepdims=True)
        gd = dqn * qg_ref[...]
        du_ref[:, 0:512] = (r_q * (gd - xhat_q * _group_mean(gd * xhat_q, ones512))).astype(BF)

    rev = lambda i: (nt - 1 - i, 0)
    specs = _swa_in_specs(nt, True)
    win = pltpu.VMEM((_WROWS, 128), BF)
    wint = pltpu.VMEM((128, _WROWS), BF)
    tile_bf = pltpu.VMEM((TR, 512), BF)
    acc = pltpu.VMEM((_WROWS, 128), F32)
    return pl.pallas_call(
        body, name="swa_bwd", grid=(nt,),
        in_specs=specs + [pl.BlockSpec((TR, 512), rev), pl.BlockSpec((ATT_Q_HEADS, TR), lambda i: (0, nt - 1 - i)),
                          pl.BlockSpec((TR, 512), rev)],
        out_specs=[pl.BlockSpec((TR, N_C), rev), pl.BlockSpec((8, 512), lambda i: (0, 0)),
                   pl.BlockSpec((8, 128), lambda i: (0, 0))],
        out_shape=[jax.ShapeDtypeStruct((Lp, N_C), BF), jax.ShapeDtypeStruct((8, 512), F32),
                   jax.ShapeDtypeStruct((8, 128), F32)],
        scratch_shapes=[win, win, win, win, wint, wint, tile_bf, tile_bf, tile_bf, tile_bf,
                        pltpu.VMEM((512, TR), F32), acc, acc, acc, acc,
                        pltpu.VMEM((HALO, 128), F32), pltpu.VMEM((HALO, 128), F32),
                        pltpu.VMEM((N_META, 128), F32), pltpu.VMEM((N_META, 128), F32)]
        + _swa_park(F32) + _swa_park(F32) + _swa_park(BF) + _swa_park(BF) + _swa_ones(),
        compiler_params=_cp(("arbitrary",)),
    )(uc, uc, uc, uc, uc, uc, uc, uc, qg8, kg2, sinks, o_save, lse, dyc)


def _mix_fwd(h, ya, yb, yc, ug, wa, wb, wc, wo, g_next=None, loss=None):
    Lp = h.shape[0]
    wspec = lambda r: pl.BlockSpec((r, D_MODEL), lambda i: (0, 0))
    yspec = pl.BlockSpec((TRM, 512), lambda i: (i, 0))
    hspec = pl.BlockSpec((TRM, D_MODEL), lambda i: (i, 0))
    last = loss is not None

    def body(h_ref, ya_ref, yb_ref, yc_ref, ug_ref, wa_ref, wb_ref, wc_ref, wo_ref, t_ref, *outs):
        mixed = jnp.zeros((TRM, D_MODEL), F32)
        for n, (y_ref, w_ref) in enumerate(((ya_ref, wa_ref), (yb_ref, wb_ref), (yc_ref, wc_ref))):
            z = _nn(y_ref[...], w_ref[...])
            outs[n][...] = z.astype(BF)
            mixed = mixed + _sig(ug_ref[:, D_MODEL * n:D_MODEL * (n + 1)].astype(F32)) * z
        mixed = mixed.astype(BF)
        outs[3][...] = mixed
        x = h_ref[...] + _nn(mixed, wo_ref[...])
        if last:
            dh_ref, l_ref = outs[4:]
            i = pl.program_id(0)

            @pl.when(i == 0)
            def _():
                l_ref[...] = jnp.zeros_like(l_ref)

            row = i * TRM + lax.broadcasted_iota(jnp.int32, (TRM, 1), 0)
            e = jnp.where((row >= CHUNK) & (row < CHUNK + loss[1]), x - t_ref[...], 0.0)
            dh_ref[...] = e * (1.0 / D_MODEL)
            l_ref[...] += (0.5 / D_MODEL) * jnp.sum(jnp.sum(e * e, axis=0, keepdims=True), axis=1, keepdims=True)
        else:
            x_ref, nx_ref = outs[4:]
            x_ref[...] = x
            nx_ref[...] = (x * lax.rsqrt(jnp.mean(x * x, axis=-1, keepdims=True) + EPS) * t_ref[...]).astype(BF)

    bf = jax.ShapeDtypeStruct((Lp, D_MODEL), BF)
    f32 = jax.ShapeDtypeStruct((Lp, D_MODEL), F32)
    if last:
        tail_in, tail_spec = loss[0], hspec
        out_specs = [hspec] * 5 + [pl.BlockSpec((8, 128), lambda i: (0, 0))]
        out_shape = [bf] * 4 + [f32, jax.ShapeDtypeStruct((8, 128), F32)]
    else:
        tail_in, tail_spec = g_next, wspec(1)
        out_specs = [hspec] * 6
        out_shape = [bf] * 4 + [f32, bf]
    return pl.pallas_call(
        body, name="mix_fwd_loss" if last else "mix_fwd", grid=(Lp // TRM,),
        in_specs=[hspec, yspec, yspec, yspec, pl.BlockSpec((TRM, N_G), lambda i: (i, 0)),
                  wspec(512), wspec(512), wspec(512), wspec(D_MODEL), tail_spec],
        out_specs=out_specs, out_shape=out_shape,
        compiler_params=_cp(("arbitrary",) if last else ("parallel",)),
    )(h, ya, yb, yc, ug, wa, wb, wc, wo, tail_in)


def _mix_bwd(dh, za, zb, zc, ug, wa, wb, wc, wo):
    Lp = dh.shape[0]
    wspec = lambda r: pl.BlockSpec((r, D_MODEL), lambda i: (0, 0))
    yspec = pl.BlockSpec((TRM, 512), lambda i: (i, 0))
    hspec = pl.BlockSpec((TRM, D_MODEL), lambda i: (i, 0))
    gspec = pl.BlockSpec((TRM, N_G), lambda i: (i, 0))

    def body(dh_ref, za_ref, zb_ref, zc_ref, ug_ref, wa_ref, wb_ref, wc_ref, wo_ref,
             dug_ref, dza_ref, dzb_ref, dzc_ref, dya_ref, dyb_ref, dyc_ref):
        dmix = _nt(dh_ref[...].astype(BF), wo_ref[...])
        for n, (z_ref, w_ref, dz_ref, dy_ref) in enumerate(((za_ref, wa_ref, dza_ref, dya_ref),
                                                            (zb_ref, wb_ref, dzb_ref, dyb_ref),
                                                            (zc_ref, wc_ref, dzc_ref, dyc_ref))):
            sl = slice(D_MODEL * n, D_MODEL * (n + 1))
            gt = _sig(ug_ref[:, sl].astype(F32))
            dz = dmix * gt
            dug_ref[:, sl] = (dz * z_ref[...].astype(F32) * (1.0 - gt)).astype(BF)
            dz = dz.astype(BF)
            dz_ref[...] = dz
            dy_ref[...] = _nt(dz, w_ref[...]).astype(BF)

    bf = lambda n: jax.ShapeDtypeStruct((Lp, n), BF)
    return pl.pallas_call(
        body, name="mix_bwd", grid=(Lp // TRM,),
        in_specs=[hspec, hspec, hspec, hspec, gspec, wspec(512), wspec(512), wspec(512), wspec(D_MODEL)],
        out_specs=[gspec, hspec, hspec, hspec, yspec, yspec, yspec],
        out_shape=[bf(N_G), bf(D_MODEL), bf(D_MODEL), bf(D_MODEL), bf(512), bf(512), bf(512)],
        compiler_params=_cp(("parallel",)),
    )(dh, za, zb, zc, ug, wa, wb, wc, wo)


def _inproj_bwd(dus, ws, h, dh, g, carry=None):
    Lp = h.shape[0]
    widths = [w.shape[0] for w in ws]

    def body(dg_ref, da_ref, db_ref, dc_ref, wg_ref, wa_ref, wb_ref, wc_ref, h_ref, dh_ref, g_ref, o_ref, gg_ref):
        @pl.when(pl.program_id(0) == 0)
        def _():
            gg_ref[...] = jnp.zeros_like(gg_ref)

        dhn = (_nn(dg_ref[...], wg_ref[...]) + _nn(da_ref[...], wa_ref[...])
               + _nn(db_ref[...], wb_ref[...]) + _nn(dc_ref[...], wc_ref[...]))
        x = h_ref[...]
        r = lax.rsqrt(jnp.mean(x * x, axis=-1, keepdims=True) + EPS)
        xhat = x * r
        gg_ref[0:1, :] += jnp.sum(dhn * xhat, axis=0, keepdims=True)
        gd = dhn * g_ref[...]
        o_ref[...] = dh_ref[...] + r * (gd - xhat * jnp.mean(gd * xhat, axis=-1, keepdims=True))

    hspec = pl.BlockSpec((TRM, D_MODEL), lambda i: (i, 0))
    in_specs = ([pl.BlockSpec((TRM, n), lambda i: (i, 0)) for n in widths]
                + [pl.BlockSpec((n, D_MODEL), lambda i: (0, 0), pipeline_mode=pl.Buffered(1)) for n in widths]
                + [hspec, hspec, pl.BlockSpec((1, D_MODEL), lambda i: (0, 0))])
    out_specs = [hspec, pl.BlockSpec((8, D_MODEL), lambda i: (0, 0))]
    out_shape = [jax.ShapeDtypeStruct((Lp, D_MODEL), F32), jax.ShapeDtypeStruct((8, D_MODEL), F32)]
    return _call_carrying(body, "inproj_bwd", Lp // TRM, in_specs, out_specs, out_shape, [],
                          (*dus, *ws, h, dh, g), carry)


def _lb_softmax(lb_ref):
    x = lb_ref[...]
    e = jnp.exp(x - jnp.max(x, axis=0, keepdims=True))
    return e / jnp.sum(e, axis=0, keepdims=True)


def _lb_fwd(hg_lb):
    def body(lb_ref, o_ref):
        sm = _lb_softmax(lb_ref)
        acc = jnp.zeros((1, 512), F32)
        for l in range(DEPTH):
            if l > 0:
                acc = acc + sm[l:l + 1, :]
            o_ref[l:l + 1, :] = jnp.clip(acc, 0.0, 1.0)

    return pl.pallas_call(body, name="lb_fwd", out_shape=jax.ShapeDtypeStruct((DEPTH, 512), F32))(hg_lb)


def _lb_bwd(hg_lb, dlb_all):
    def body(lb_ref, d_ref, o_ref):
        sm = _lb_softmax(lb_ref)
        acc = jnp.zeros((1, 512), F32)
        gm = []
        for l in range(DEPTH):
            if l > 0:
                acc = acc + sm[l:l + 1, :]
            gm.append(jnp.where((acc >= 0.0) & (acc <= 1.0), d_ref[l:l + 1, :], 0.0))
        dsm = [jnp.zeros((1, 512), F32)]
        for j in range(1, DEPTH):
            s = gm[j]
            for l in range(j + 1, DEPTH):
                s = s + gm[l]
            dsm.append(s)
        dot = dsm[0] * sm[0:1, :]
        for j in range(1, DEPTH):
            dot = dot + dsm[j] * sm[j:j + 1, :]
        for j in range(DEPTH):
            o_ref[j:j + 1, :] = sm[j:j + 1, :] * (dsm[j] - dot)

    return pl.pallas_call(body, name="lb_bwd", out_shape=jax.ShapeDtypeStruct((DEPTH, 512), F32))(hg_lb, dlb_all)


_ANY = pl.BlockSpec(memory_space=pl.ANY)


def _chip_peers():
    x, y, c = lax.axis_index("x"), lax.axis_index("y"), lax.axis_index("c")
    return (x, y, c), [(1 - x, y, c), (x, 1 - y, c), (1 - x, 1 - y, c)]


def _exchange(kind, ins, outs, send, recv, loc):
    (x, y, c), peers = _chip_peers()
    me = 2 * x + y
    ds = []
    for a in range(len(ins)):
        if kind == "gather":
            ds.append(pltpu.make_async_copy(ins[a], outs[a].at[me], loc.at[a]))
        else:
            ds.append(pltpu.make_async_copy(ins[a].at[me], outs[a].at[0], loc.at[a]))
        for p, (px, py, pc) in enumerate(peers):
            src, dst = (ins[a], outs[a].at[me]) if kind == "gather" else (ins[a].at[2 * px + py], outs[a].at[1 + p])
            ds.append(pltpu.make_async_remote_copy(src_ref=src, dst_ref=dst, send_sem=send.at[a, p],
                                                   recv_sem=recv.at[a, p], device_id=(px, py, pc), device_id_type=MESH))
    return ds


def _exchange_out_shapes(kind, arrs):
    if kind == "gather":
        return [jax.ShapeDtypeStruct((4,) + a.shape, a.dtype) for a in arrs]
    return [jax.ShapeDtypeStruct(a.shape, a.dtype) for a in arrs]


def _exchange_sems(n):
    return [pltpu.SemaphoreType.DMA((n, 3)), pltpu.SemaphoreType.DMA((n, 3)), pltpu.SemaphoreType.DMA((n,))]


def _gather_first(arrs, split):
    n = len(arrs)

    def body(*refs):
        ins, outs = refs[:n], refs[n:2 * n]
        send, recv, loc, fsend, frecv = refs[2 * n:]
        (x, y, c), peers = _chip_peers()
        me = 2 * x + y

        def half(a):
            hr = arrs[a].shape[0] // 2
            return pl.ds(pl.multiple_of(c * hr, 16), hr)

        local = [pltpu.make_async_copy(ins[a], outs[a].at[me], loc.at[a]) for a in range(n)]
        far, fwd = {}, {}
        for a in range(n):
            for p, (px, py, pc) in enumerate(peers):
                src, dst = (ins[a].at[half(a)], outs[a].at[me, half(a)]) if split[a] else (ins[a], outs[a].at[me])
                far[a, p] = pltpu.make_async_remote_copy(src_ref=src, dst_ref=dst, send_sem=send.at[a, p],
                                                         recv_sem=recv.at[a, p], device_id=(px, py, pc), device_id_type=MESH)
                if split[a]:
                    landed = outs[a].at[2 * px + py, half(a)]
                    fwd[a, p] = pltpu.make_async_remote_copy(src_ref=landed, dst_ref=landed, send_sem=fsend.at[a, p],
                                                             recv_sem=frecv.at[a, p], device_id=(x, y, 1 - c),
                                                             device_id_type=MESH)
        for d in local + list(far.values()):
            d.start()
        for key, d in far.items():
            d.wait_recv()
            if key in fwd:
                fwd[key].start()
        for d in fwd.values():
            d.wait()
        for d in far.values():
            d.wait_send()
        for d in local:
            d.wait()

    sems = pltpu.SemaphoreType.DMA((n, 3))
    return pl.pallas_call(
        body, name="gather_first", in_specs=[_ANY] * n, out_specs=[_ANY] * n,
        out_shape=_exchange_out_shapes("gather", arrs),
        scratch_shapes=[sems, sems, pltpu.SemaphoreType.DMA((n,)), sems, sems],
        compiler_params=pltpu.CompilerParams(has_side_effects=True),
    )(*arrs)


def _carry_exchange(body, n_in, n_out, n_steps, kind, n):
    def wrapped(*refs):
        ins, cin = refs[:n_in], refs[n_in:n_in + n]
        outs, cout = refs[n_in + n:n_in + n + n_out], refs[n_in + n + n_out:n_in + 2 * n + n_out]
        scr, sems = refs[n_in + 2 * n + n_out:-3], refs[-3:]
        i = pl.program_id(0)

        @pl.when(i == 0)
        def _():
            for d in _exchange(kind, cin, cout, *sems):
                d.start()

        body(*ins, *outs, *scr)

        @pl.when(i == n_steps - 1)
        def _():
            for d in _exchange(kind, cin, cout, *sems):
                d.wait()

    return wrapped


def _swap_cores(arrs):
    n = len(arrs)

    def body(*refs):
        ins, outs = refs[:n], refs[n:2 * n]
        send, recv = refs[2 * n:]
        x, y, c = lax.axis_index("x"), lax.axis_index("y"), lax.axis_index("c")
        rdmas = []
        for a in range(n):
            r = pltpu.make_async_remote_copy(src_ref=ins[a], dst_ref=outs[a], send_sem=send.at[a], recv_sem=recv.at[a],
                                             device_id=(x, y, 1 - c), device_id_type=MESH)
            r.start()
            rdmas.append(r)
        for r in rdmas:
            r.wait()

    return pl.pallas_call(
        body, name="swap_cores", in_specs=[_ANY] * n, out_specs=[_ANY] * n,
        out_shape=[jax.ShapeDtypeStruct(a.shape, a.dtype) for a in arrs],
        scratch_shapes=[pltpu.SemaphoreType.DMA((n,)), pltpu.SemaphoreType.DMA((n,))],
        compiler_params=pltpu.CompilerParams(has_side_effects=True),
    )(*arrs)


def _allsum_small(p):
    R = p.shape[0]

    def body(p_ref, o_ref, buf, send, recv):
        x, y, c = lax.axis_index("x"), lax.axis_index("y"), lax.axis_index("c")
        me = 4 * x + 2 * y + c
        buf[me] = p_ref[...]
        rdmas = []
        for k in range(1, 8):
            peer = (x ^ (k >> 2), y ^ ((k >> 1) & 1), c ^ (k & 1))
            r = pltpu.make_async_remote_copy(src_ref=p_ref, dst_ref=buf.at[me], send_sem=send.at[k - 1],
                                             recv_sem=recv.at[k - 1], device_id=peer, device_id_type=MESH)
            r.start()
            rdmas.append(r)
        for r in rdmas:
            r.wait()
        acc = buf[0]
        for d in range(1, 8):
            acc = acc + buf[d]
        o_ref[...] = acc

    return pl.pallas_call(
        body, name="allsum_small", out_shape=jax.ShapeDtypeStruct((R, 512), F32),
        in_specs=[pl.BlockSpec(memory_space=pltpu.VMEM)], out_specs=pl.BlockSpec(memory_space=pltpu.VMEM),
        scratch_shapes=[pltpu.VMEM((8, R, 512), F32), pltpu.SemaphoreType.DMA((7,)), pltpu.SemaphoreType.DMA((7,))],
        compiler_params=_cp(has_side_effects=True),
    )(p)


def _row_block(rows):
    return max((d for d in range(16, 513, 16) if rows % d == 0), default=rows)


def _sum4(parts, name):
    _, R, C = parts.shape
    tr = _row_block(R)

    def body(p_ref, o_ref):
        p = [p_ref[k].astype(F32) for k in range(4)]
        o_ref[...] = (((p[0] + p[1]) + p[2]) + p[3]).astype(BF)

    return pl.pallas_call(
        body, name=name, grid=(R // tr,), in_specs=[pl.BlockSpec((4, tr, C), lambda i: (0, i, 0))],
        out_specs=pl.BlockSpec((tr, C), lambda i: (i, 0)), out_shape=jax.ShapeDtypeStruct((R, C), BF),
        compiler_params=_cp(("parallel",)),
    )(parts)


def _adamw(w, m, v, g0, g1, name):
    L, R, C = w.shape
    tr = _row_block(R)
    two = g1 is not None
    c1 = 1.0 / (1.0 - ADAM_B1 ** ADAM_STEP)
    c2 = 1.0 / (1.0 - ADAM_B2 ** ADAM_STEP)

    def body(*refs):
        if two:
            w_ref, m_ref, v_ref, a_ref, b_ref, g_ref, d_ref, nm_ref, nv_ref = refs
            g = a_ref[...].astype(F32) + b_ref[...].astype(F32)
        else:
            w_ref, m_ref, v_ref, a_ref, g_ref, d_ref, nm_ref, nv_ref = refs
            g = a_ref[...]
        g_ref[...] = g
        m = ADAM_B1 * m_ref[...] + (1.0 - ADAM_B1) * g
        v = ADAM_B2 * v_ref[...] + (1.0 - ADAM_B2) * (g * g)
        nm_ref[...] = m
        nv_ref[...] = v
        d_ref[...] = -ADAM_LR * ((m * c1) / (jnp.sqrt(v * c2) + ADAM_EPS) + ADAM_WD * w_ref[...])

    spec = pl.BlockSpec((1, tr, C), lambda l, i: (l, i, 0))
    n_in = 5 if two else 4
    ins = (w, m, v, g0, g1) if two else (w, m, v, g0)
    return pl.pallas_call(
        body, name=name, grid=(L, R // tr), in_specs=[spec] * n_in, out_specs=[spec] * 4,
        out_shape=[jax.ShapeDtypeStruct((L, R, C), F32)] * 4, compiler_params=_cp(("parallel", "parallel")),
    )(*ins)


def _pad8(a):
    r = (-a.shape[0]) % 8
    return a if r == 0 else jnp.pad(a, ((0, r), (0, 0)))


def _local_step(x, tgt, meta, P, shards=None, prep=None, pack=None, mats=None):
    seq = x.shape[0]
    Lp = -(-(seq + CHUNK) // TR) * TR
    tail = Lp - seq - CHUNK
    h = jnp.concatenate([jnp.zeros((META_PAD, D_MODEL), F32), meta, x, jnp.zeros((tail, D_MODEL), F32)], axis=0)
    tgt_pad = jnp.pad(tgt, ((CHUNK, tail), (0, 0)))

    P = list(P)
    saved = []
    hn = _rms_fwd(h, P[0]["norm_g"])
    for l in range(DEPTH):
        p = P[l]
        mm = functools.partial(_matmul, tb=True, out_dtype=BF, tm=TR, tk=D_MODEL, col_major_grid=True)
        ug = mm(hn, p["w_g"], tn=N_G // 2, name="inproj_g")
        ua = mm(hn, p["w_a"], tn=N_A, name="inproj_a")
        ub = mm(hn, p["w_b"], tn=N_B, name="inproj_b")
        uc = mm(hn, p["w_c"], tn=N_C, name="inproj_c")
        nxt = shards[l + 1] if shards is not None and l + 1 < DEPTH else None
        carry = (lambda part: ("gather", part)) if nxt is not None else (lambda part: None)
        res_a = _conv_fwd(ua, p["conv_w"], p["conv_vec"], carry(nxt and nxt[1:2]))
        res_b = _hg_fwd(ub, p["lb"], p["gn4"], carry(nxt and nxt[0:1]))
        own = shards[0][2:] if nxt is not None and l == 0 else []
        res_c = _swa_fwd(uc, p["qg"], p["kg"], p["sinks"], carry(nxt and nxt[2:] + own))
        (ya, yconv), (yb, o_hg, s_hg), (yc, o_at, lse) = res_a[:2], res_b[:3], res_c[:3]
        if nxt is not None:
            P.append(prep(l + 1, [*res_b[3:], *res_a[2:], *res_c[3:3 + len(nxt) - 2]]))
            if own:
                p.update(mats(res_c[3 + len(nxt) - 2:]))
        mix = functools.partial(_mix_fwd, h, ya, yb, yc, ug, p["w_ao"], p["w_bo"], p["w_co"], p["w_out"])
        if l + 1 < DEPTH:
            za, zb, zc, mixed, h_new, hn_next = mix(g_next=P[l + 1]["norm_g"])
        else:
            za, zb, zc, mixed, dh, loss8 = mix(loss=(tgt_pad, seq))
        saved.append(dict(h=h, hn=hn, ug=ug, ua=ua, ub=ub, uc=uc, ya=ya, yconv=yconv, yb=yb, o_hg=o_hg, s_hg=s_hg,
                          yc=yc, o_at=o_at, lse=lse, za=za, zb=zb, zc=zc, mixed=mixed))
        if l + 1 < DEPTH:
            h, hn = h_new, hn_next

    grads = [None] * DEPTH
    parts = [[None, None] for _ in range(DEPTH)]
    pending = None
    tk_dw = next(t for t in (13 * 128, 2 * TR, TR) if Lp % t == 0)
    for l in reversed(range(DEPTH)):
        p, s = P[l], saved[l]
        dug, dza, dzb, dzc, dya, dyb, dyc = _mix_bwd(dh, s["za"], s["zb"], s["zc"], s["ug"],
                                                      p["w_ao"], p["w_bo"], p["w_co"], p["w_out"])
        tnmm = functools.partial(_matmul, ta=True, out_dtype=F32, tk=tk_dw)
        g = {}
        g["w_out"] = tnmm(s["mixed"], dh, tm=D_MODEL, tn=D_MODEL, name="dw_out")
        g["w_ao"], g["w_bo"], g["w_co"] = _dw_branches([s["ya"], s["yb"], s["yc"]], [dza, dzb, dzc], tk_dw)
        dua, g["conv_w"], g["conv_vec"] = _conv_bwd(s["ua"], s["yconv"], dya, p["conv_w"], p["conv_vec"])
        carry = ("scatter", pending[1]) if pending is not None else None
        res = _hg_bwd(s["ub"], p["lb"], p["gn4"], s["o_hg"], s["s_hg"], dyb, carry)
        dub, g["hg_small"] = res[:2]
        if carry is not None:
            parts[pending[0]][1] = res[2:]
        duc, g["at_gain"], g["at_sink"] = _swa_bwd(s["uc"], p["qg"], p["kg"], p["sinks"], s["o_at"], s["lse"], dyc)
        g["w_g"] = tnmm(dug, s["hn"], tm=N_G // 2, tn=D_MODEL, name="dw_in_g")
        g["w_a"] = tnmm(dua, s["hn"], tm=N_A, tn=D_MODEL, name="dw_in_a")
        g["w_b"] = tnmm(dub, s["hn"], tm=N_B, tn=D_MODEL, name="dw_in_b")
        g["w_c"] = tnmm(duc, s["hn"], tm=N_C, tn=D_MODEL, name="dw_in_c")
        first, second = pack(g) if pack is not None else (None, None)
        if first is not None and l == 0:
            first, second = first + second, []
        res = _inproj_bwd([dug, dua, dub, duc], [p["w_g"], p["w_a"], p["w_b"], p["w_c"]], s["h"], dh, p["norm_g"],
                          ("scatter", first) if first is not None else None)
        dh, g["norm_g"] = res[:2]
        grads[l] = g
        if pack is not None:
            parts[l] = [res[2:3], res[3:]] if l == 0 else [res[2:], None]
            pending = (l, second) if l > 0 else None
    return loss8, dh, grads, parts


def _split_w_in(wt):
    return dict(w_a=wt[0:1536], w_b=wt[1536:3584],
                w_c=jnp.concatenate([wt[3584:4096], wt[4352:4864], wt[4096:4352]], axis=0), w_g=wt[4864:7936])


def _join_w_in(g):
    c = g["w_c"]
    return jnp.concatenate([g["w_a"], g["w_b"], c[0:512], c[1024:1280], c[512:1024], g["w_g"]], axis=0)


def _attn_small(g):
    return (g["at_gain"][0].reshape(ATT_Q_HEADS, ATT_HD).sum(0),
            g["at_gain"][1, 0:128].reshape(ATT_KV_HEADS, ATT_HD).sum(0), g["at_sink"].sum(1))


_SMALL = (("norm_g", 8), ("meta", 32), ("conv_w", 32 * DEPTH), ("conv_b", 8), ("conv_ln_g", 8), ("conv_ln_b", 8),
          ("lb", 8), ("hg_norm_g", 8), ("q_norm_g", 8), ("k_norm_g", 8), ("sinks", 8))


def _small_offsets():
    off, o = {}, 0
    for name, rows in _SMALL:
        off[name] = (o, rows)
        o += rows
    return off, o


def _pack_small(d):
    parts = []
    for name, rows in _SMALL:
        a = d[name]
        parts.append(jnp.pad(a, ((0, rows - a.shape[0]), (0, 512 - a.shape[1]))))
    return jnp.concatenate(parts, axis=0)


def kernel(x, meta_tokens, norm_g, w_in, conv_w, conv_b, conv_ln_g, conv_ln_b, w_conv_out, hg_lower_bounds, hg_norm_g, w_hg_out, q_norm_g, k_norm_g, attn_sinks, w_att_out, w_out, loss_target, m_meta_tokens, m_norm_g, m_w_in, m_conv_w, m_conv_b, m_conv_ln_g, m_conv_ln_b, m_w_conv_out, m_hg_lower_bounds, m_hg_norm_g, m_w_hg_out, m_q_norm_g, m_k_norm_g, m_attn_sinks, m_w_att_out, m_w_out, v_meta_tokens, v_norm_g, v_w_in, v_conv_w, v_conv_b, v_conv_ln_g, v_conv_ln_b, v_w_conv_out, v_hg_lower_bounds, v_hg_norm_g, v_w_hg_out, v_q_norm_g, v_k_norm_g, v_attn_sinks, v_w_att_out, v_w_out):
    xi, yi = lax.axis_index("x"), lax.axis_index("y")
    chip = 2 * xi + yi
    NS = w_in.shape[2]
    CS = conv_w.shape[2]
    MS = meta_tokens.shape[1]

    half = NS // 2
    w_in_t, m_w_in_t, v_w_in_t = (jnp.swapaxes(t, 1, 2) for t in (w_in, m_w_in, v_w_in))
    shards = [[w_in_t[l, :half].astype(BF), w_in_t[l, half:].astype(BF), w_conv_out[l].astype(BF),
               w_hg_out[l].astype(BF), w_att_out[l].astype(BF), w_out[l].astype(BF)] for l in range(DEPTH)]
    *first, g_meta, g_convw = _gather_first(shards[0][:2] + [meta_tokens, conv_w.reshape(DEPTH * CONV_WIDTH, CS)],
                                            [True, True, False, False])
    cols = lambda g: g.transpose(1, 0, 2).reshape(g.shape[1], -1)
    meta_f = cols(g_meta)
    convw_f = cols(g_convw).reshape(DEPTH, CONV_WIDTH, D_CONV)
    lb_all = _lb_fwd(hg_lower_bounds)

    def mats(gathered):
        g_wao, g_wbo, g_wco, g_wout = gathered
        return dict(w_ao=cols(g_wao), w_bo=cols(g_wbo), w_co=cols(g_wco), w_out=g_wout.reshape(D_MODEL, D_MODEL))

    def prep(l, gathered):
        p = _split_w_in(jnp.concatenate(gathered[:2], axis=1).reshape(4 * NS, D_MODEL))
        if len(gathered) > 2:
            p.update(mats(gathered[2:]))
        p.update(norm_g=norm_g[l:l + 1], conv_w=convw_f[l],
                 conv_vec=_pad8(jnp.stack([conv_b[l], conv_ln_g[l], conv_ln_b[l]])),
                 lb=lb_all[l:l + 1], gn4=jnp.tile(hg_norm_g[l:l + 1], (1, HG_HEADS)),
                 qg=jnp.tile(q_norm_g[l:l + 1], (1, ATT_Q_HEADS)), kg=jnp.tile(k_norm_g[l:l + 1], (1, ATT_KV_HEADS)),
                 sinks=attn_sinks[l:l + 1])
        return p

    shard_cols = lambda a: a.reshape(a.shape[0], 4, -1).transpose(1, 0, 2)
    def pack(g):
        win = _join_w_in(g).reshape(4, NS, D_MODEL).astype(BF)
        return [win[:, :half]], [win[:, half:], shard_cols(g["w_ao"]).astype(BF), shard_cols(g["w_bo"]).astype(BF),
                                 shard_cols(g["w_co"]).astype(BF), g["w_out"].reshape(4, MS, D_MODEL).astype(BF)]

    loss8, dh0, grads, parts = _local_step(x[0], loss_target[0], meta_f, [prep(0, first)], shards, prep, pack, mats)
    seq = x.shape[1]
    grad_x = dh0[CHUNK:CHUNK + seq][None]
    loss = lax.psum(loss8[0, 0], ("x", "y", "c"))

    sum4 = functools.partial(_sum4, name="sum_chips")
    mine = [jnp.concatenate([t for l in range(DEPTH) for t in (sum4(parts[l][0][0]), sum4(parts[l][1][0]))], axis=0)]
    mine += [jnp.concatenate([sum4(parts[l][1][a]) for l in range(DEPTH)], axis=0) for a in range(1, 5)]
    theirs = _swap_cores(mine)

    dlb_all = jnp.concatenate([grads[l]["hg_small"][0:1] for l in range(DEPTH)], axis=0)
    small = dict(
        norm_g=jnp.concatenate([grads[l]["norm_g"][0:1] for l in range(DEPTH)], axis=0).reshape(8, 512),
        meta=dh0[META_PAD:CHUNK].reshape(32, 512),
        conv_w=jnp.concatenate([grads[l]["conv_w"] for l in range(DEPTH)], axis=0),
        conv_b=jnp.concatenate([grads[l]["conv_vec"][0:1] for l in range(DEPTH)], axis=0),
        conv_ln_g=jnp.concatenate([grads[l]["conv_vec"][1:2] for l in range(DEPTH)], axis=0),
        conv_ln_b=jnp.concatenate([grads[l]["conv_vec"][2:3] for l in range(DEPTH)], axis=0),
        lb=_lb_bwd(hg_lower_bounds, dlb_all),
        hg_norm_g=jnp.concatenate([grads[l]["hg_small"][1:2].reshape(HG_HEADS, HG_D).sum(0, keepdims=True)
                                   for l in range(DEPTH)], axis=0),
        q_norm_g=jnp.stack([_attn_small(grads[l])[0] for l in range(DEPTH)]),
        k_norm_g=jnp.stack([_attn_small(grads[l])[1] for l in range(DEPTH)]),
        sinks=jnp.stack([_attn_small(grads[l])[2] for l in range(DEPTH)]),
    )
    gsum = _allsum_small(_pack_small(small))
    off, _ = _small_offsets()

    def take(name, rows, cols):
        o, _ = off[name]
        return gsum[o:o + rows, 0:cols]

    g_meta_full = take("meta", 32, 512).reshape(N_META, D_MODEL)
    g_convw_full = take("conv_w", 32 * DEPTH, 512).reshape(DEPTH, 32, 512)[:, :CONV_WIDTH]
    small_grads = dict(
        norm_g=take("norm_g", 8, 512),
        meta=lax.dynamic_slice_in_dim(g_meta_full, chip * MS, MS, axis=1),
        conv_w=lax.dynamic_slice_in_dim(g_convw_full, chip * CS, CS, axis=2).reshape(DEPTH * CONV_WIDTH, CS),
        conv_b=take("conv_b", DEPTH, 512), conv_ln_g=take("conv_ln_g", DEPTH, 512), conv_ln_b=take("conv_ln_b", DEPTH, 512),
        lb=take("lb", DEPTH, 512), hg_norm_g=take("hg_norm_g", DEPTH, HG_D), q_norm_g=take("q_norm_g", DEPTH, ATT_HD),
        k_norm_g=take("k_norm_g", DEPTH, ATT_HD), sinks=take("sinks", DEPTH, ATT_Q_HEADS))

    def big_update(w, m, v, a, b, name):
        return _adamw(w, m, v, a.reshape(w.shape), b.reshape(w.shape), name)

    res = {}
    res["w_in"] = [jnp.swapaxes(t, 1, 2) for t in big_update(w_in_t, m_w_in_t, v_w_in_t, mine[0], theirs[0], "adamw_w_in")]
    res["w_conv_out"] = big_update(w_conv_out, m_w_conv_out, v_w_conv_out, mine[1], theirs[1], "adamw_w_ao")
    res["w_hg_out"] = big_update(w_hg_out, m_w_hg_out, v_w_hg_out, mine[2], theirs[2], "adamw_w_bo")
    res["w_att_out"] = big_update(w_att_out, m_w_att_out, v_w_att_out, mine[3], theirs[3], "adamw_w_co")
    res["w_out"] = big_update(w_out, m_w_out, v_w_out, mine[4], theirs[4], "adamw_w_out")

    small_w = dict(meta=(meta_tokens, m_meta_tokens, v_meta_tokens), norm_g=(norm_g, m_norm_g, v_norm_g),
                   conv_w=(conv_w, m_conv_w, v_conv_w), conv_b=(conv_b, m_conv_b, v_conv_b),
                   conv_ln_g=(conv_ln_g, m_conv_ln_g, v_conv_ln_g), conv_ln_b=(conv_ln_b, m_conv_ln_b, v_conv_ln_b),
                   lb=(hg_lower_bounds, m_hg_lower_bounds, v_hg_lower_bounds),
                   hg_norm_g=(hg_norm_g, m_hg_norm_g, v_hg_norm_g), q_norm_g=(q_norm_g, m_q_norm_g, v_q_norm_g),
                   k_norm_g=(k_norm_g, m_k_norm_g, v_k_norm_g), sinks=(attn_sinks, m_attn_sinks, v_attn_sinks))
    view = lambda n, t: t.reshape(-1, 512) if n == "norm_g" else t.reshape(-1, t.shape[-1])
    pw, pm, pv = (_pack_rows([view(n, small_w[n][k]) for n in small_w]) for k in range(3))
    pg = _pack_rows([small_grads[n] for n in small_w])
    packed = [t[0] for t in _adamw(pw[None], pm[None], pv[None], pg[None], None, "adamw_small")]
    o = 0
    for n in small_w:
        r, cdim = view(n, small_w[n][0]).shape
        res[n] = [t[o:o + r, 0:cdim].reshape(small_w[n][0].shape) for t in packed]
        o += -(-r // 8) * 8

    order = [("meta", None), ("norm_g", None), ("w_in", None), ("conv_w", None), ("conv_b", None), ("conv_ln_g", None),
             ("conv_ln_b", None), ("w_conv_out", None), ("lb", None), ("hg_norm_g", None), ("w_hg_out", None),
             ("q_norm_g", None), ("k_norm_g", None), ("sinks", None), ("w_att_out", None), ("w_out", None)]
    outs = [loss, grad_x]
    for k in range(4):
        outs += [res[n][k] for n, _ in order]
    return tuple(outs)


def _pack_rows(arrs):
    parts = []
    for a in arrs:
        r = (-a.shape[0]) % 8
        parts.append(jnp.pad(a, ((0, r), (0, 512 - a.shape[1]))))
    return jnp.concatenate(parts, axis=0)
```

```python
import functools

import jax
import jax.numpy as jnp
from jax import lax
from jax.experimental import pallas as pl
from jax.experimental.pallas import tpu as pltpu

F32 = jnp.float32
BF = jnp.bfloat16

D_MODEL = 1024
DEPTH = 4
CHUNK = 64
N_META = 16
META_PAD = CHUNK - N_META
D_CONV = 512
CONV_WIDTH = 31
HG_HEADS = 4
HG_D = 128
ATT_Q_HEADS = 8
ATT_KV_HEADS = 2
ATT_HD = 64
ATT_GROUP = ATT_Q_HEADS // ATT_KV_HEADS
EPS = 1e-6
F_FLOOR = 1e-30
NEG = -1e30

ADAM_LR = 0.001
ADAM_B1 = 0.9
ADAM_B2 = 0.999
ADAM_EPS = 1e-08
ADAM_WD = 0.01
ADAM_STEP = 10

TR = 640
TRM = TR // 2
CONV_RB = 32
QB = 128
HALO = 128
VMEM_LIMIT = 56 * 1024 * 1024

N_G, N_A, N_B, N_C = 3 * D_MODEL, 3 * D_CONV, 4 * 512, 2 * 512 + 2 * 128

MESH = pl.DeviceIdType.MESH


def _cp(sem=None, vmem=VMEM_LIMIT, **kw):
    if sem is None:
        return pltpu.CompilerParams(vmem_limit_bytes=vmem, **kw)
    return pltpu.CompilerParams(dimension_semantics=sem, vmem_limit_bytes=vmem, **kw)


def _nn(a, b):
    return lax.dot_general(a, b, (((1,), (0,)), ((), ())), preferred_element_type=F32)


def _nt(a, b):
    return lax.dot_general(a, b, (((1,), (1,)), ((), ())), preferred_element_type=F32)


def _tn(a, b):
    return lax.dot_general(a, b, (((0,), (0,)), ((), ())), preferred_element_type=F32)


def _sig(x):
    return jax.nn.sigmoid(x)


def _silu(x):
    return x * _sig(x)


def _silu_pair(x):
    s = _sig(x)
    return x * s, s * (1.0 + x * (1.0 - s))


def _mm_split(t, x):
    hi = x.astype(BF)
    lo = (x - hi.astype(F32)).astype(BF)
    return _nn(t, hi) + _nn(t, lo)


def _chunk_tri(n, upper):
    r = lax.broadcasted_iota(jnp.int32, (n, n), 0)
    c = lax.broadcasted_iota(jnp.int32, (n, n), 1)
    same = jnp.right_shift(r, 6) == jnp.right_shift(c, 6)
    tri = (c >= r) if upper else (c <= r)
    return jnp.where(same & tri, 1.0, 0.0).astype(BF)


def _matmul(a, b, *, ta=False, tb=False, out_dtype, tm, tn, tk, name, col_major_grid=False):
    if ta:
        K, M = a.shape
    else:
        M, K = a.shape
    N = b.shape[0] if tb else b.shape[1]
    assert M % tm == 0 and N % tn == 0 and K % tk == 0, (name, M, N, K, tm, tn, tk)
    nk = K // tk
    if col_major_grid:
        grid = (N // tn, M // tm, nk)
        ij = lambda g0, g1: (g1, g0)
    else:
        grid = (M // tm, N // tn, nk)
        ij = lambda g0, g1: (g0, g1)
    if ta:
        a_spec = pl.BlockSpec((tk, tm), lambda g0, g1, k: (k, ij(g0, g1)[0]))
    else:
        a_spec = pl.BlockSpec((tm, tk), lambda g0, g1, k: (ij(g0, g1)[0], k))
    if tb:
        b_spec = pl.BlockSpec((tn, tk), lambda g0, g1, k: (ij(g0, g1)[1], k))
    else:
        b_spec = pl.BlockSpec((tk, tn), lambda g0, g1, k: (k, ij(g0, g1)[1]))
    o_spec = pl.BlockSpec((tm, tn), lambda g0, g1, k: ij(g0, g1))
    dims = (((0 if ta else 1,), (1 if tb else 0,)), ((), ()))
    use_acc = nk > 1 and out_dtype != F32

    def body(a_ref, b_ref, o_ref, *scr):
        k = pl.program_id(2)
        p = lax.dot_general(a_ref[...].astype(BF), b_ref[...].astype(BF), dims, preferred_element_type=F32)
        if nk == 1:
            o_ref[...] = p.astype(out_dtype)
        else:
            acc = scr[0] if use_acc else o_ref

            @pl.when(k == 0)
            def _():
                acc[...] = p

            @pl.when(k > 0)
            def _():
                acc[...] += p

            if use_acc:
                @pl.when(k == nk - 1)
                def _():
                    o_ref[...] = acc[...].astype(out_dtype)

    return pl.pallas_call(
        body, name=name, grid=grid, in_specs=[a_spec, b_spec], out_specs=o_spec,
        out_shape=jax.ShapeDtypeStruct((M, N), out_dtype),
        scratch_shapes=[pltpu.VMEM((tm, tn), F32)] if use_acc else [],
        compiler_params=_cp(("parallel", "parallel", "arbitrary")),
    )(a, b)


def _dw_branches(ys, dzs, tk):
    Lp = ys[0].shape[0]

    def body(ya_ref, yb_ref, yc_ref, da_ref, db_ref, dc_ref, oa_ref, ob_ref, oc_ref):
        k = pl.program_id(0)
        for y_ref, d_ref, o_ref in ((ya_ref, da_ref, oa_ref), (yb_ref, db_ref, ob_ref), (yc_ref, dc_ref, oc_ref)):
            p = _tn(y_ref[...], d_ref[...])

            @pl.when(k == 0)
            def _():
                o_ref[...] = p

            @pl.when(k > 0)
            def _():
                o_ref[...] += p

    yspec = pl.BlockSpec((tk, 512), lambda k: (k, 0))
    dspec = pl.BlockSpec((tk, D_MODEL), lambda k: (k, 0))
    ospec = pl.BlockSpec((512, D_MODEL), lambda k: (0, 0))
    return pl.pallas_call(
        body, name="dw_branches", grid=(Lp // tk,), in_specs=[yspec] * 3 + [dspec] * 3, out_specs=[ospec] * 3,
        out_shape=[jax.ShapeDtypeStruct((512, D_MODEL), F32)] * 3, compiler_params=_cp(("arbitrary",)),
    )(*ys, *dzs)


def _rms_fwd(h, g):
    Lp = h.shape[0]

    def body(h_ref, g_ref, o_ref):
        x = h_ref[...]
        r = lax.rsqrt(jnp.mean(x * x, axis=-1, keepdims=True) + EPS)
        o_ref[...] = (x * r * g_ref[...]).astype(BF)

    return pl.pallas_call(
        body, name="rms_fwd", grid=(Lp // TR,),
        in_specs=[pl.BlockSpec((TR, D_MODEL), lambda i: (i, 0)), pl.BlockSpec((1, D_MODEL), lambda i: (0, 0))],
        out_specs=pl.BlockSpec((TR, D_MODEL), lambda i: (i, 0)),
        out_shape=jax.ShapeDtypeStruct((Lp, D_MODEL), BF),
        compiler_params=_cp(("parallel",)),
    )(h, g)


def _glu(ua, row):
    a = ua[:, 0:D_CONV].astype(F32)
    gl = ua[:, D_CONV:2 * D_CONV].astype(F32)
    return jnp.where(row >= META_PAD, a * _sig(gl), 0.0)


_SH_ROWS = TR + CHUNK - 8


def _fill_shifts(src, sh):
    x = src[...]
    hi = x.astype(BF)
    lo = (x - hi.astype(F32)).astype(BF)
    rows_in = x.shape[0]
    r = lax.broadcasted_iota(jnp.int32, (128, 144), 0)
    c = lax.broadcasted_iota(jnp.int32, (128, 144), 1)
    for b in range(1, 8):
        band = jnp.where(c == r + b, 1.0, 0.0).astype(BF)
        for s in range(0, _SH_ROWS, 128):
            m = min(128, _SH_ROWS - s)
            k = min(144, rows_in - s)
            p = band[:-(-m // 16) * 16, :k]
            sh[b - 1, pl.ds(s, m), :] = (_nn(p, hi[s:s + k, :]) + _nn(p, lo[s:s + k, :]))[:m]


def _shifted(src, sh, start, n):
    b = start % 8
    if b == 0:
        return src[pl.ds(start, n), :]
    return sh[b - 1, pl.ds(start - b, n), :]


def _conv_fwd(ua, cw, cvec, carry=None):
    Lp = ua.shape[0]
    nt = Lp // TR
    hb = TR // CHUNK

    def body(cur_ref, halo_ref, w_ref, v_ref, ya_ref, yc_ref, ext, sh):
        i = pl.program_id(0)
        row = i * TR + lax.broadcasted_iota(jnp.int32, (TR, 1), 0)
        hrow = i * TR - CHUNK + lax.broadcasted_iota(jnp.int32, (CHUNK, 1), 0)
        ext[pl.ds(0, CHUNK), :] = jnp.where(i > 0, _glu(halo_ref[...], hrow), 0.0)
        ext[pl.ds(CHUNK, TR), :] = _glu(cur_ref[...], row)
        _fill_shifts(ext, sh)
        for rb in range(TR // CONV_RB):
            r0 = rb * CONV_RB
            rows = pl.ds(r0, CONV_RB)
            acc = jnp.zeros((CONV_RB, D_CONV), F32)
            for j in range(CONV_WIDTH):
                acc = acc + _shifted(ext, sh, r0 + CHUNK - (CONV_WIDTH - 1) + j, CONV_RB) * w_ref[j:j + 1, :]
            y = acc + v_ref[0:1, :]
            yc_ref[rows, :] = y
            mu = jnp.mean(y, axis=-1, keepdims=True)
            d = y - mu
            var = jnp.mean(d * d, axis=-1, keepdims=True)
            yn = d * lax.rsqrt(var + EPS) * v_ref[1:2, :] + v_ref[2:3, :]
            ya_ref[rows, :] = (_silu(yn) * _silu(cur_ref[rows, 2 * D_CONV:3 * D_CONV].astype(F32))).astype(BF)

    in_specs = [pl.BlockSpec((TR, N_A), lambda i: (i, 0)),
                pl.BlockSpec((CHUNK, N_A), lambda i: (jnp.maximum(i * hb - 1, 0), 0)),
                pl.BlockSpec((CONV_WIDTH, D_CONV), lambda i: (0, 0)),
                pl.BlockSpec((8, D_CONV), lambda i: (0, 0))]
    out_specs = [pl.BlockSpec((TR, D_CONV), lambda i: (i, 0)), pl.BlockSpec((TR, D_CONV), lambda i: (i, 0))]
    out_shape = [jax.ShapeDtypeStruct((Lp, D_CONV), BF), jax.ShapeDtypeStruct((Lp, D_CONV), F32)]
    scratch = [pltpu.VMEM((TR + CHUNK, D_CONV), F32), pltpu.VMEM((7, _SH_ROWS, D_CONV), F32)]
    return _call_carrying(body, "conv_fwd", nt, in_specs, out_specs, out_shape, scratch, (ua, ua, cw, cvec), carry)


def _conv_bwd(ua, yconv, dya, cw, cvec):
    Lp = ua.shape[0]
    nt = Lp // TR
    hb = TR // CHUNK
    nhb = Lp // CHUNK

    def ln_bwd(y, dout, gate, v_ref):
        mu = jnp.mean(y, axis=-1, keepdims=True)
        d = y - mu
        var = jnp.mean(d * d, axis=-1, keepdims=True)
        rstd = lax.rsqrt(var + EPS)
        xhat = d * rstd
        yn = xhat * v_ref[1:2, :] + v_ref[2:3, :]
        s_gate, ds_gate = _silu_pair(gate)
        s_yn, ds_yn = _silu_pair(yn)
        dyn = dout * s_gate * ds_yn
        dxh = dyn * v_ref[1:2, :]
        dyc = rstd * (dxh - jnp.mean(dxh, axis=-1, keepdims=True) - xhat * jnp.mean(dxh * xhat, axis=-1, keepdims=True))
        return dyc, dyn, xhat, dout * s_yn * ds_gate

    def body(cur_ref, prev_ref, next_ref, yc_ref, ycn_ref, dy_ref, dyn_ref, w_ref, v_ref,
             du_ref, dw_ref, dv_ref, uext, dext, dwacc, ush, dsh):
        i = pl.program_id(0)

        @pl.when(i == 0)
        def _():
            dwacc[...] = jnp.zeros_like(dwacc)
            dv_ref[...] = jnp.zeros_like(dv_ref)

        row = i * TR + lax.broadcasted_iota(jnp.int32, (TR, 1), 0)
        hrow = i * TR - CHUNK + lax.broadcasted_iota(jnp.int32, (CHUNK, 1), 0)
        uext[pl.ds(0, CHUNK), :] = jnp.where(i > 0, _glu(prev_ref[...], hrow), 0.0)
        uext[pl.ds(CHUNK, TR), :] = _glu(cur_ref[...], row)

        s_b = jnp.zeros((1, D_CONV), F32)
        s_g = jnp.zeros((1, D_CONV), F32)
        s_bb = jnp.zeros((1, D_CONV), F32)
        for rb in range(TR // CONV_RB):
            rows = pl.ds(rb * CONV_RB, CONV_RB)
            gate = cur_ref[rows, 2 * D_CONV:3 * D_CONV].astype(F32)
            dout = dy_ref[rows, :].astype(F32)
            dyc, dyn, xhat, dgate = ln_bwd(yc_ref[rows, :], dout, gate, v_ref)
            du_ref[rows, 2 * D_CONV:3 * D_CONV] = dgate.astype(BF)
            dext[rows, :] = dyc
            s_b = s_b + jnp.sum(dyc, axis=0, keepdims=True)
            s_g = s_g + jnp.sum(dyn * xhat, axis=0, keepdims=True)
            s_bb = s_bb + jnp.sum(dyn, axis=0, keepdims=True)
        dv_ref[0:1, :] += s_b
        dv_ref[1:2, :] += s_g
        dv_ref[2:3, :] += s_bb
        dyc_n, _, _, _ = ln_bwd(ycn_ref[...], dyn_ref[...].astype(F32),
                                next_ref[:, 2 * D_CONV:3 * D_CONV].astype(F32), v_ref)
        dext[pl.ds(TR, CHUNK), :] = jnp.where(i < nt - 1, dyc_n, 0.0)
        _fill_shifts(uext, ush)
        _fill_shifts(dext, dsh)

        for rb in range(TR // CONV_RB):
            r0 = rb * CONV_RB
            rows = pl.ds(r0, CONV_RB)
            d_blk = dext[rows, :]
            dglu = jnp.zeros((CONV_RB, D_CONV), F32)
            for j in range(CONV_WIDTH):
                dglu = dglu + _shifted(dext, dsh, r0 + CONV_WIDTH - 1 - j, CONV_RB) * w_ref[j:j + 1, :]
                prod = d_blk * _shifted(uext, ush, r0 + CHUNK - (CONV_WIDTH - 1) + j, CONV_RB)
                part = prod[0:8, :]
                for s in range(1, CONV_RB // 8):
                    part = part + prod[8 * s:8 * s + 8, :]
                dwacc[j] += part
            a = cur_ref[rows, 0:D_CONV].astype(F32)
            sg = _sig(cur_ref[rows, D_CONV:2 * D_CONV].astype(F32))
            grow = i * TR + r0 + lax.broadcasted_iota(jnp.int32, (CONV_RB, 1), 0)
            dglu = jnp.where(grow >= META_PAD, dglu, 0.0)
            du_ref[rows, 0:D_CONV] = (dglu * sg).astype(BF)
            du_ref[rows, D_CONV:2 * D_CONV] = (dglu * a * sg * (1.0 - sg)).astype(BF)

        @pl.when(i == nt - 1)
        def _():
            dw_ref[...] = jnp.sum(dwacc[...], axis=1)

    nxt = lambda i: (jnp.minimum(i * hb + hb, nhb - 1), 0)
    return pl.pallas_call(
        body, name="conv_bwd", grid=(nt,),
        in_specs=[pl.BlockSpec((TR, N_A), lambda i: (i, 0)),
                  pl.BlockSpec((CHUNK, N_A), lambda i: (jnp.maximum(i * hb - 1, 0), 0)),
                  pl.BlockSpec((CHUNK, N_A), nxt),
                  pl.BlockSpec((TR, D_CONV), lambda i: (i, 0)),
                  pl.BlockSpec((CHUNK, D_CONV), nxt),
                  pl.BlockSpec((TR, D_CONV), lambda i: (i, 0)),
                  pl.BlockSpec((CHUNK, D_CONV), nxt),
                  pl.BlockSpec((CONV_WIDTH, D_CONV), lambda i: (0, 0)),
                  pl.BlockSpec((8, D_CONV), lambda i: (0, 0))],
        out_specs=[pl.BlockSpec((TR, N_A), lambda i: (i, 0)),
                   pl.BlockSpec((32, D_CONV), lambda i: (0, 0)),
                   pl.BlockSpec((8, D_CONV), lambda i: (0, 0))],
        out_shape=[jax.ShapeDtypeStruct((Lp, N_A), BF), jax.ShapeDtypeStruct((32, D_CONV), F32),
                   jax.ShapeDtypeStruct((8, D_CONV), F32)],
        scratch_shapes=[pltpu.VMEM((TR + CHUNK, D_CONV), F32), pltpu.VMEM((TR + CHUNK, D_CONV), F32),
                        pltpu.VMEM((32, 8, D_CONV), F32), pltpu.VMEM((7, _SH_ROWS, D_CONV), F32),
                        pltpu.VMEM((7, _SH_ROWS, D_CONV), F32)],
        compiler_params=_cp(("arbitrary",)),
    )(ua, ua, ua, yconv, yconv, dya, dya, cw, cvec)


def _hg_gates(ub_ref, lbv, row):
    q = ub_ref[:, 0:512].astype(F32)
    z = ub_ref[:, 512:1024].astype(F32)
    valid = row >= META_PAD
    sig = _sig(z)
    f = lbv + (1.0 - lbv) * sig
    g = jnp.where(valid, jnp.log(jnp.maximum(f, F_FLOOR)), 0.0)
    k = jnp.where(valid, (1.0 - lbv) * (1.0 - sig), 0.0)
    return q, k, g, sig, f


def _hg_chunk_terms(b_c, q_c, k_c):
    bm = b_c[CHUNK // 2 - 1:CHUNK // 2, :]
    bl = b_c[CHUNK - 1:CHUNK, :]
    e1 = jnp.exp(b_c - bm)
    e2 = jnp.exp(bm - b_c)
    e0 = jnp.exp(b_c)
    e3 = jnp.exp(bl - b_c)
    el = jnp.exp(bl)
    return e1, e2, e0, e3, el, q_c * e1, k_c * e2, q_c * e0, k_c * e3


def _hg_fwd(ub, lb, gn4, carry=None):
    Lp = ub.shape[0]
    nt = Lp // TR
    cpt = TR // CHUNK

    def body(ub_ref, lb_ref, gn_ref, yb_ref, o_ref, ss_ref, st, bsc, qsc, ksc, qes, els, ust, tlo):
        i = pl.program_id(0)

        @pl.when(i == 0)
        def _():
            st[...] = jnp.zeros_like(st)
            tlo[...] = _chunk_tri(TR, False)

        row = i * TR + lax.broadcasted_iota(jnp.int32, (TR, 1), 0)
        q, k, g, _, _ = _hg_gates(ub_ref, lb_ref[...], row)
        qsc[...] = _silu(q)
        ksc[...] = k
        bsc[...] = _mm_split(tlo[...], g)
        tri = lax.broadcasted_iota(jnp.int32, (CHUNK, CHUNK), 1) <= lax.broadcasted_iota(jnp.int32, (CHUNK, CHUNK), 0)

        def intra(c, carry):
            rows = pl.ds(pl.multiple_of(c * CHUNK, CHUNK), CHUNK)
            _, _, _, _, el, qe, ke, qE, kd = _hg_chunk_terms(bsc[rows, :], qsc[rows, :], ksc[rows, :])
            qe, ke, kd = qe.astype(BF), ke.astype(BF), kd.astype(BF)
            qes[rows, :] = qE.astype(BF)
            els[c] = jnp.broadcast_to(el, (8, 512))
            sls = [slice(HG_D * h, HG_D * (h + 1)) for h in range(HG_HEADS)]
            v = [ub_ref[rows, 1024 + HG_D * h:1024 + HG_D * (h + 1)] for h in range(HG_HEADS)]
            a = [_nt(qe[:, sl], ke[:, sl]) for sl in sls]
            u = [_tn(v[h], kd[:, sls[h]]) for h in range(HG_HEADS)]
            a = [jnp.where(tri, x, 0.0).astype(BF) for x in a]
            oi = [_nn(a[h], v[h]) for h in range(HG_HEADS)]
            for h in range(HG_HEADS):
                ust[c, h] = u[h]
                o_ref[rows, sls[h]] = oi[h]
            return carry

        lax.fori_loop(0, cpt, intra, 0, unroll=2)

        for h in range(HG_HEADS):
            sl = slice(HG_D * h, HG_D * (h + 1))
            s = st[h]
            for c in range(cpt):
                ss_ref[c, h] = s
                s = els[c, 0:1, sl] * s + ust[c, h]
            st[h] = s

        def inter(c, carry):
            rows = pl.ds(pl.multiple_of(c * CHUNK, CHUNK), CHUNK)
            for h in range(HG_HEADS):
                sl = slice(HG_D * h, HG_D * (h + 1))
                o_ref[rows, sl] += _nt(qes[rows, sl], ss_ref[c, h].astype(BF))
            return carry

        lax.fori_loop(0, cpt, inter, 0, unroll=2)

        gate = ub_ref[:, 1536:2048].astype(F32)
        for h in range(HG_HEADS):
            sl = slice(HG_D * h, HG_D * (h + 1))
            o = o_ref[:, sl]
            r = lax.rsqrt(jnp.mean(o * o, axis=-1, keepdims=True) + EPS)
            yb_ref[:, sl] = (o * r * gn_ref[:, sl] * _silu(gate[:, sl])).astype(BF)

    in_specs = [pl.BlockSpec((TR, N_B), lambda i: (i, 0)), pl.BlockSpec((1, 512), lambda i: (0, 0)),
                pl.BlockSpec((1, 512), lambda i: (0, 0))]
    out_specs = [pl.BlockSpec((TR, 512), lambda i: (i, 0)), pl.BlockSpec((TR, 512), lambda i: (i, 0)),
                 pl.BlockSpec((cpt, HG_HEADS, HG_D, HG_D), lambda i: (i, 0, 0, 0))]
    out_shape = [jax.ShapeDtypeStruct((Lp, 512), BF), jax.ShapeDtypeStruct((Lp, 512), F32),
                 jax.ShapeDtypeStruct((Lp // CHUNK, HG_HEADS, HG_D, HG_D), F32)]
    scratch = [pltpu.VMEM((HG_HEADS, HG_D, HG_D), F32), pltpu.VMEM((TR, 512), F32),
               pltpu.VMEM((TR, 512), F32), pltpu.VMEM((TR, 512), F32), pltpu.VMEM((TR, 512), BF),
               pltpu.VMEM((cpt, 8, 512), F32), pltpu.VMEM((cpt, HG_HEADS, HG_D, HG_D), F32), pltpu.VMEM((TR, TR), BF)]
    return _call_carrying(body, "hgrn_fwd", nt, in_specs, out_specs, out_shape, scratch, (ub, lb, gn4), carry)


def _hg_bwd(ub, lb, gn4, o_save, s_save, dyb, carry=None):
    Lp = ub.shape[0]
    nt = Lp // TR
    cpt = TR // CHUNK

    def body(ub_ref, lb_ref, gn_ref, o_ref, ss_ref, dy_ref, du_ref, ds_ref,
             dst, bsc, qsc, ksc, dosc, dqsc, dksc, dbsc, els, ust, dss, tlo, tup):
        i = pl.program_id(0)
        t = nt - 1 - i

        @pl.when(i == 0)
        def _():
            dst[...] = jnp.zeros_like(dst)
            ds_ref[...] = jnp.zeros_like(ds_ref)
            tlo[...] = _chunk_tri(TR, False)
            tup[...] = _chunk_tri(TR, True)

        lbv = lb_ref[...]
        row = t * TR + lax.broadcasted_iota(jnp.int32, (TR, 1), 0)
        valid = row >= META_PAD
        q, k, g, sig, f = _hg_gates(ub_ref, lbv, row)
        silu_q, dsilu_q = _silu_pair(q)
        qsc[...] = silu_q
        ksc[...] = k
        bsc[...] = _mm_split(tlo[...], g)

        gate = ub_ref[:, 1536:2048].astype(F32)
        dy = dy_ref[...].astype(F32)
        for h in range(HG_HEADS):
            sl = slice(HG_D * h, HG_D * (h + 1))
            o = o_ref[:, sl]
            r = lax.rsqrt(jnp.mean(o * o, axis=-1, keepdims=True) + EPS)
            ohat = o * r
            silu_g, dsilu_g = _silu_pair(gate[:, sl])
            don = dy[:, sl] * silu_g
            du_ref[:, 1536 + HG_D * h:1536 + HG_D * (h + 1)] = (dy[:, sl] * ohat * gn_ref[:, sl] * dsilu_g).astype(BF)
            ds_ref[1:2, sl] += jnp.sum(don * ohat, axis=0, keepdims=True)
            gd = don * gn_ref[:, sl]
            dosc[:, sl] = r * (gd - ohat * jnp.mean(gd * ohat, axis=-1, keepdims=True))

        tri = lax.broadcasted_iota(jnp.int32, (CHUNK, CHUNK), 1) <= lax.broadcasted_iota(jnp.int32, (CHUNK, CHUNK), 0)
        last = lax.broadcasted_iota(jnp.int32, (CHUNK, 1), 0) == CHUNK - 1

        def incr(c, carry):
            rows = pl.ds(pl.multiple_of(c * CHUNK, CHUNK), CHUNK)
            b_c = bsc[rows, :]
            qE_b = (qsc[rows, :] * jnp.exp(b_c)).astype(BF)
            els[c] = jnp.broadcast_to(jnp.exp(b_c[CHUNK - 1:CHUNK, :]), (8, 512))
            do_c = dosc[rows, :].astype(BF)
            for h in range(HG_HEADS):
                sl = slice(HG_D * h, HG_D * (h + 1))
                ust[c, h] = _tn(do_c[:, sl], qE_b[:, sl])
            return carry

        lax.fori_loop(0, cpt, incr, 0, unroll=2)

        for h in range(HG_HEADS):
            sl = slice(HG_D * h, HG_D * (h + 1))
            d_s = dst[h]
            for c in reversed(range(cpt)):
                dss[c, h] = d_s
                d_s = els[c, 0:1, sl] * d_s + ust[c, h]
            dst[h] = d_s

        def chunk(c, carry):
            r0 = pl.multiple_of(c * CHUNK, CHUNK)
            rows = pl.ds(r0, CHUNK)
            e1, e2, e0, e3, el, qe, ke, qE, kd = _hg_chunk_terms(bsc[rows, :], qsc[rows, :], ksc[rows, :])
            qe_b, ke_b, kd_b = qe.astype(BF), ke.astype(BF), kd.astype(BF)
            do_c = dosc[rows, :].astype(BF)
            hs = range(HG_HEADS)
            sls = [slice(HG_D * h, HG_D * (h + 1)) for h in hs]
            v = [ub_ref[rows, 1024 + HG_D * h:1024 + HG_D * (h + 1)] for h in hs]
            do = [do_c[:, sl] for sl in sls]
            a = [_nt(qe_b[:, sl], ke_b[:, sl]) for sl in sls]
            da = [_nt(do[h], v[h]) for h in hs]
            dqE = [_nn(do[h], ss_ref[c, h].astype(BF)) for h in hs]
            dkd = [_nn(v[h], dss[c, h].astype(BF)) for h in hs]
            dv2 = [_nt(kd_b[:, sls[h]], dss[c, h].astype(BF)) for h in hs]
            a = [jnp.where(tri, x, 0.0).astype(BF) for x in a]
            da = [jnp.where(tri, x, 0.0).astype(BF) for x in da]
            dv = [_tn(a[h], do[h]) + dv2[h] for h in hs]
            dqe = [_nn(da[h], ke_b[:, sls[h]]) for h in hs]
            dke = [_tn(da[h], qe_b[:, sls[h]]) for h in hs]
            for h in hs:
                sl = sls[h]
                del_h = jnp.sum(ss_ref[c, h] * dss[c, h], axis=0, keepdims=True)
                dqsc[rows, sl] = dqE[h] * e0[:, sl] + dqe[h] * e1[:, sl]
                dksc[rows, sl] = dke[h] * e2[:, sl] + dkd[h] * e3[:, sl]
                tkd = dkd[h] * kd[:, sl]
                dbl = jnp.sum(tkd, axis=0, keepdims=True) + del_h * el[:, sl]
                dbsc[rows, sl] = (dqE[h] * qE[:, sl] + dqe[h] * qe[:, sl] - dke[h] * ke[:, sl] - tkd
                                  + jnp.where(last, dbl, 0.0))
                du_ref[rows, 1024 + HG_D * h:1024 + HG_D * (h + 1)] = dv[h].astype(BF)
            return carry

        lax.fori_loop(0, cpt, chunk, 0, unroll=2)

        dg = _mm_split(tup[...], dbsc[...])
        df = jnp.where(valid & (f > F_FLOOR), dg / f, 0.0)
        dk = jnp.where(valid, dksc[...], 0.0)
        dsig = (df - dk) * (1.0 - lbv)
        ds_ref[0:1, :] += jnp.sum((df - dk) * (1.0 - sig), axis=0, keepdims=True)
        du_ref[:, 512:1024] = (dsig * sig * (1.0 - sig)).astype(BF)
        du_ref[:, 0:512] = (dqsc[...] * dsilu_q).astype(BF)

    rev = lambda i: (nt - 1 - i, 0)
    in_specs = [pl.BlockSpec((TR, N_B), rev), pl.BlockSpec((1, 512), lambda i: (0, 0)),
                pl.BlockSpec((1, 512), lambda i: (0, 0)), pl.BlockSpec((TR, 512), rev),
                pl.BlockSpec((cpt, HG_HEADS, HG_D, HG_D), lambda i: (nt - 1 - i, 0, 0, 0)),
                pl.BlockSpec((TR, 512), rev)]
    out_specs = [pl.BlockSpec((TR, N_B), rev), pl.BlockSpec((8, 512), lambda i: (0, 0))]
    out_shape = [jax.ShapeDtypeStruct((Lp, N_B), BF), jax.ShapeDtypeStruct((8, 512), F32)]
    states = pltpu.VMEM((cpt, HG_HEADS, HG_D, HG_D), F32)
    scratch = ([pltpu.VMEM((HG_HEADS, HG_D, HG_D), F32)] + [pltpu.VMEM((TR, 512), F32)] * 7
               + [pltpu.VMEM((cpt, 8, 512), F32), states, states, pltpu.VMEM((TR, TR), BF), pltpu.VMEM((TR, TR), BF)])
    return _call_carrying(body, "hgrn_bwd", nt, in_specs, out_specs, out_shape, scratch,
                          (ub, lb, gn4, o_save, s_save, dyb), carry)


_KCOL = (2 * 512) // 128
_VCOL = _KCOL + 1


def _swa_in_specs(nt, rev):
    tile = (lambda i: nt - 1 - i) if rev else (lambda i: i)
    hpt = TR // HALO
    return [
        pl.BlockSpec((TR, 512), lambda i: (tile(i), 0)),
        pl.BlockSpec((TR, 512), lambda i: (tile(i), 1)),
        pl.BlockSpec((TR, 128), lambda i: (tile(i), _KCOL)),
        pl.BlockSpec((TR, 128), lambda i: (tile(i), _VCOL)),
        pl.BlockSpec((HALO, 128), lambda i: (jnp.maximum(tile(i) * hpt - 1, 0), _KCOL)),
        pl.BlockSpec((HALO, 128), lambda i: (jnp.maximum(tile(i) * hpt - 1, 0), _VCOL)),
        pl.BlockSpec((CHUNK, 128), lambda i: (0, _KCOL)),
        pl.BlockSpec((CHUNK, 128), lambda i: (0, _VCOL)),
        pl.BlockSpec((1, 512), lambda i: (0, 0)),
        pl.BlockSpec((1, 128), lambda i: (0, 0)),
        pl.BlockSpec((1, ATT_Q_HEADS), lambda i: (0, 0)),
    ]


_WROWS = 2 * CHUNK + HALO + TR
_W0 = 2 * CHUNK
_C0 = _W0 + HALO
_SCALE = ATT_HD ** -0.5


def _group_ones(n):
    r = lax.broadcasted_iota(jnp.int32, (n, n), 0)
    c = lax.broadcasted_iota(jnp.int32, (n, n), 1)
    return jnp.where(jnp.right_shift(r, 6) == jnp.right_shift(c, 6), 1.0, 0.0).astype(BF)


def _group_mean(x, ones):
    hi = x.astype(BF)
    lo = (x - hi.astype(F32)).astype(BF)
    return (_nn(hi, ones) + _nn(lo, ones)) * (1.0 / ATT_HD)


def _head_rms(x, ones):
    r = lax.rsqrt(_group_mean(x * x, ones) + EPS)
    return x * r, r


def _swa_windows(kc_ref, vc_ref, kh_ref, vh_ref, km_ref, vm_ref, kg2, ones, kwin, krwin, vwin, vrwin):
    meta = pl.ds(META_PAD, N_META)
    for (k, v, r0, n) in ((km_ref[meta, :], vm_ref[meta, :], 0, N_META), (kh_ref[...], vh_ref[...], _W0, HALO),
                          (kc_ref[...], vc_ref[...], _C0, TR)):
        xhat, _ = _head_rms(k.astype(F32), ones)
        kn = xhat * kg2
        kwin[pl.ds(r0, n), :] = kn.astype(BF)
        krwin[pl.ds(r0, n), :] = pltpu.roll(kn, ATT_HD, 1).astype(BF)
        vwin[pl.ds(r0, n), :] = v
        if vrwin is not None:
            vrwin[pl.ds(r0, n), :] = pltpu.roll(v.astype(F32), ATT_HD, 1).astype(BF)
    zero = jnp.zeros((_W0 - N_META, 128), BF)
    for w in (kwin, krwin, vwin, vrwin):
        if w is not None:
            w[pl.ds(N_META, _W0 - N_META), :] = zero


def _swa_masks_t(t, qb):
    q0 = t * TR + qb * QB
    qc = jnp.right_shift(q0 + lax.broadcasted_iota(jnp.int32, (1, QB), 1), 6)
    kabs = q0 - HALO + lax.broadcasted_iota(jnp.int32, (QB + HALO, 1), 0)
    kc = jnp.right_shift(kabs + HALO, 6) - HALO // CHUNK
    mask_w = (kc <= qc) & (kc >= qc - 2) & (kabs >= META_PAD)
    return qc > 2, mask_w


def _swa_park(dtype):
    return [pltpu.VMEM((ATT_Q_HEADS, N_META, QB), dtype), pltpu.VMEM((ATT_Q_HEADS, QB + HALO, QB), dtype)]


def _swa_ones():
    return [pltpu.VMEM((128, 128), BF), pltpu.VMEM((512, 512), BF)]


def _split_heads(x, lane_hi):
    return jnp.where(lane_hi, 0.0, x).astype(BF), jnp.where(lane_hi, x, 0.0).astype(BF)


def _call_carrying(body, name, nt, in_specs, out_specs, out_shape, scratch, args, carry):
    if carry is None:
        return pl.pallas_call(body, name=name, grid=(nt,), in_specs=in_specs, out_specs=out_specs, out_shape=out_shape,
                              scratch_shapes=scratch, compiler_params=_cp(("arbitrary",)))(*args)
    kind, arrs = carry
    n = len(arrs)
    return pl.pallas_call(
        _carry_exchange(body, len(in_specs), len(out_specs), nt, kind, n), name=name + "_" + kind, grid=(nt,),
        in_specs=in_specs + [_ANY] * n, out_specs=out_specs + [_ANY] * n,
        out_shape=out_shape + _exchange_out_shapes(kind, arrs), scratch_shapes=scratch + _exchange_sems(n),
        compiler_params=_cp(("arbitrary",), has_side_effects=True),
    )(*args, *arrs)


def _swa_fwd(uc, qg8, kg2, sinks, carry=None):
    Lp = uc.shape[0]
    nt = Lp // TR
    nqb = TR // QB

    def body(q_ref, g_ref, kc_ref, vc_ref, kh_ref, vh_ref, km_ref, vm_ref, qg_ref, kg_ref, sk_ref,
             yc_ref, o_ref, lse_ref, kwin, krwin, vwin, vt, qlo, qhi, ot, s_m, s_w, p_m, p_w, g128, g512):
        t = pl.program_id(0)

        @pl.when(t == 0)
        def _():
            g128[...] = _group_ones(128)
            g512[...] = _group_ones(512)

        _swa_windows(kc_ref, vc_ref, kh_ref, vh_ref, km_ref, vm_ref, kg_ref[...], g128[...], kwin, krwin, vwin, None)
        vt[...] = vwin[...].T
        xhat, _ = _head_rms(q_ref[...].astype(F32), g512[...])
        lane_hi = (lax.broadcasted_iota(jnp.int32, (1, 512), 1) & ATT_HD) != 0
        lo, hi = _split_heads(xhat * qg_ref[...] * _SCALE, lane_hi)
        qlo[...] = lo
        qhi[...] = hi
        for qb in range(nqb):
            rows = pl.ds(qb * QB, QB)
            wrows = pl.ds(_W0 + qb * QB, QB + HALO)
            mrows = pl.ds(0, N_META)
            mask_m, mask_w = _swa_masks_t(t, qb)
            for j in range(ATT_Q_HEADS):
                p, e = j // 2, j % 2
                ks = kwin if e == j // ATT_GROUP else krwin
                qp = (qlo, qhi)[e][rows, 128 * p:128 * (p + 1)]
                s_m[j] = _nt(ks[mrows, :], qp)
                s_w[j] = _nt(ks[wrows, :], qp)
            inv = []
            for j in range(ATT_Q_HEADS):
                sm = jnp.where(mask_m, s_m[j], NEG)
                sw = jnp.where(mask_w, s_w[j], NEG)
                sink = sk_ref[:, j:j + 1]
                m = jnp.maximum(jnp.maximum(jnp.max(sm, axis=0, keepdims=True),
                                            jnp.max(sw, axis=0, keepdims=True)), sink)
                em = jnp.exp(sm - m)
                ew = jnp.exp(sw - m)
                den = jnp.sum(em, axis=0, keepdims=True) + jnp.sum(ew, axis=0, keepdims=True) + jnp.exp(sink - m)
                p_m[j] = em.astype(BF)
                p_w[j] = ew.astype(BF)
                lse_ref[j:j + 1, pl.ds(qb * QB, QB)] = m + jnp.log(den)
                inv.append(1.0 / den)
            for j in range(ATT_Q_HEADS):
                vrows = pl.ds(ATT_HD * (j // ATT_GROUP), ATT_HD)
                ot[pl.ds(ATT_HD * j, ATT_HD), pl.ds(qb * QB, QB)] = (
                    _nn(vt[vrows, pl.ds(0, N_META)], p_m[j])
                    + _nn(vt[vrows, pl.ds(_W0 + qb * QB, QB + HALO)], p_w[j])) * inv[j]
        o = ot[...].T
        o_ref[...] = o
        yc_ref[...] = (o * _silu(g_ref[...].astype(F32))).astype(BF)

    win = pltpu.VMEM((_WROWS, 128), BF)
    in_specs = _swa_in_specs(nt, False)
    out_specs = [pl.BlockSpec((TR, 512), lambda i: (i, 0)), pl.BlockSpec((TR, 512), lambda i: (i, 0)),
                 pl.BlockSpec((ATT_Q_HEADS, TR), lambda i: (0, i))]
    out_shape = [jax.ShapeDtypeStruct((Lp, 512), BF), jax.ShapeDtypeStruct((Lp, 512), F32),
                 jax.ShapeDtypeStruct((ATT_Q_HEADS, Lp), F32)]
    scratch = [win, win, win, pltpu.VMEM((128, _WROWS), BF), pltpu.VMEM((TR, 512), BF),
               pltpu.VMEM((TR, 512), BF), pltpu.VMEM((512, TR), F32)] + _swa_park(F32) + _swa_park(BF) + _swa_ones()
    return _call_carrying(body, "swa_fwd", nt, in_specs, out_specs, out_shape, scratch,
                          (uc, uc, uc, uc, uc, uc, uc, uc, qg8, kg2, sinks), carry)


def _swa_bwd(uc, qg8, kg2, sinks, o_save, lse, dyc):
    Lp = uc.shape[0]
    nt = Lp // TR
    nqb = TR // QB

    def body(q_ref, g_ref, kc_ref, vc_ref, kh_ref, vh_ref, km_ref, vm_ref, qg_ref, kg_ref, sk_ref,
             o_ref, lse_ref, dy_ref, du_ref, dg_ref, dsk_ref,
             kwin, krwin, vwin, vrwin, kt, krt, qlo, qhi, dolo, dohi, dqt, dk_dir, dk_rol, dv_dir, dv_rol,
             carry_k, carry_v, meta_k, meta_v, s_m, s_w, dp_m, dp_w, p_m, p_w, ds_m, ds_w, g128, g512):
        i = pl.program_id(0)
        t = nt - 1 - i

        @pl.when(i == 0)
        def _():
            carry_k[...] = jnp.zeros_like(carry_k)
            carry_v[...] = jnp.zeros_like(carry_v)
            meta_k[...] = jnp.zeros_like(meta_k)
            meta_v[...] = jnp.zeros_like(meta_v)
            dg_ref[...] = jnp.zeros_like(dg_ref)
            dsk_ref[...] = jnp.zeros_like(dsk_ref)
            g128[...] = _group_ones(128)
            g512[...] = _group_ones(512)

        ones128 = g128[...]
        ones512 = g512[...]
        _swa_windows(kc_ref, vc_ref, kh_ref, vh_ref, km_ref, vm_ref, kg_ref[...], ones128, kwin, krwin, vwin, vrwin)
        kt[...] = kwin[...].T
        krt[...] = krwin[...].T
        xhat_q, r_q = _head_rms(q_ref[...].astype(F32), ones512)
        lane_hi = (lax.broadcasted_iota(jnp.int32, (1, 512), 1) & ATT_HD) != 0
        lo, hi = _split_heads(xhat_q * qg_ref[...] * _SCALE, lane_hi)
        qlo[...] = lo
        qhi[...] = hi
        gate = g_ref[...].astype(F32)
        dy = dy_ref[...].astype(F32)
        silu_g, dsilu_g = _silu_pair(gate)
        do = dy * silu_g
        o = o_ref[...]
        du_ref[:, 512:1024] = (dy * o * dsilu_g).astype(BF)
        lo, hi = _split_heads(do, lane_hi)
        dolo[...] = lo
        dohi[...] = hi
        hsel = jnp.where(jnp.right_shift(lax.broadcasted_iota(jnp.int32, (ATT_Q_HEADS, 512), 1), 6)
                         == lax.broadcasted_iota(jnp.int32, (ATT_Q_HEADS, 512), 0), 1.0, 0.0).astype(BF)
        prod = do * o
        p_hi = prod.astype(BF)
        d_t = _nt(hsel, p_hi) + _nt(hsel, (prod - p_hi.astype(F32)).astype(BF))
        for acc in (dk_dir, dk_rol, dv_dir, dv_rol):
            acc[...] = jnp.zeros_like(acc)

        for qb in range(nqb):
            rows = pl.ds(qb * QB, QB)
            qcols = pl.ds(qb * QB, QB)
            wrows = pl.ds(_W0 + qb * QB, QB + HALO)
            mrows = pl.ds(0, N_META)
            mask_m, mask_w = _swa_masks_t(t, qb)
            for j in range(ATT_Q_HEADS):
                p, e = j // 2, j % 2
                ks, vs = (kwin, vwin) if e == j // ATT_GROUP else (krwin, vrwin)
                pair = slice(128 * p, 128 * (p + 1))
                qp = (qlo, qhi)[e][rows, pair]
                dop = (dolo, dohi)[e][rows, pair]
                s_m[j] = _nt(ks[mrows, :], qp)
                s_w[j] = _nt(ks[wrows, :], qp)
                dp_m[j] = _nt(vs[mrows, :], dop)
                dp_w[j] = _nt(vs[wrows, :], dop)
            for j in range(ATT_Q_HEADS):
                lse_j = lse_ref[j:j + 1, qcols]
                d_j = d_t[j:j + 1, qb * QB:(qb + 1) * QB]
                em = jnp.exp(jnp.where(mask_m, s_m[j], NEG) - lse_j)
                ew = jnp.exp(jnp.where(mask_w, s_w[j], NEG) - lse_j)
                p_m[j] = em.astype(BF)
                p_w[j] = ew.astype(BF)
                ds_m[j] = (em * (dp_m[j] - d_j)).astype(BF)
                ds_w[j] = (ew * (dp_w[j] - d_j)).astype(BF)
                dsk_ref[j:j + 1, :] -= jnp.exp(sk_ref[:, j:j + 1] - lse_j) * d_j
            for j in range(ATT_Q_HEADS):
                e = j % 2
                ktr = kt if e == j // ATT_GROUP else krt
                hrows = pl.ds(ATT_HD * e, ATT_HD)
                dqt[pl.ds(ATT_HD * j, ATT_HD), qcols] = (_nn(ktr[hrows, pl.ds(0, N_META)], ds_m[j])
                                                         + _nn(ktr[hrows, pl.ds(_W0 + qb * QB, QB + HALO)], ds_w[j]))
            for direct, dk_acc, dv_acc in ((True, dk_dir, dv_dir), (False, dk_rol, dv_rol)):
                heads = [j for j in range(ATT_Q_HEADS) if (j % 2 == j // ATT_GROUP) == direct]
                q_cat = jnp.concatenate([(qlo, qhi)[j % 2][rows, 128 * (j // 2):128 * (j // 2 + 1)] for j in heads], axis=0)
                do_cat = jnp.concatenate([(dolo, dohi)[j % 2][rows, 128 * (j // 2):128 * (j // 2 + 1)] for j in heads], axis=0)
                dk_acc[mrows, :] += _nn(jnp.concatenate([ds_m[j] for j in heads], axis=1), q_cat)
                dk_acc[wrows, :] += _nn(jnp.concatenate([ds_w[j] for j in heads], axis=1), q_cat)
                dv_acc[mrows, :] += _nn(jnp.concatenate([p_m[j] for j in heads], axis=1), do_cat)
                dv_acc[wrows, :] += _nn(jnp.concatenate([p_w[j] for j in heads], axis=1), do_cat)

        dk_dir[...] += pltpu.roll(dk_rol[...], ATT_HD, 1)
        dv_dir[...] += pltpu.roll(dv_rol[...], ATT_HD, 1)
        meta_k[...] += dk_dir[pl.ds(0, N_META), :]
        meta_v[...] += dv_dir[pl.ds(0, N_META), :]
        first = jnp.where(t == 0, 1.0, 0.0)
        dk_dir[pl.ds(_C0 + TR - HALO, HALO), :] += carry_k[...]
        dv_dir[pl.ds(_C0 + TR - HALO, HALO), :] += carry_v[...]
        dk_dir[pl.ds(_C0 + META_PAD, N_META), :] += first * meta_k[...]
        dv_dir[pl.ds(_C0 + META_PAD, N_META), :] += first * meta_v[...]
        carry_k[...] = dk_dir[pl.ds(_W0, HALO), :]
        carry_v[...] = dv_dir[pl.ds(_W0, HALO), :]

        du_ref[:, 1152:1280] = dv_dir[pl.ds(_C0, TR), :].astype(BF)
        xhat_k, r_k = _head_rms(kc_ref[...].astype(F32), ones128)
        dkn = dk_dir[pl.ds(_C0, TR), :]
        dg_ref[1:2, 0:128] += jnp.sum(dkn * xhat_k, axis=0, keepdims=True)
        gd = dkn * kg_ref[...]
        du_ref[:, 1024:1152] = (r_k * (gd - xhat_k * _group_mean(gd * xhat_k, ones128))).astype(BF)
        dqn = dqt[...].T * _SCALE
        dg_ref[0:1, :] += jnp.sum(dqn * xhat_q, axis=0, keepdims=True)
        gd = dqn * qg_ref[...]
        du_ref[:, 0:512] = (r_q * (gd - xhat_q * _group_mean(gd * xhat_q, ones512))).astype(BF)

    rev = lambda i: (nt - 1 - i, 0)
    specs = _swa_in_specs(nt, True)
    win = pltpu.VMEM((_WROWS, 128), BF)
    wint = pltpu.VMEM((128, _WROWS), BF)
    tile_bf = pltpu.VMEM((TR, 512), BF)
    acc = pltpu.VMEM((_WROWS, 128), F32)
    return pl.pallas_call(
        body, name="swa_bwd", grid=(nt,),
        in_specs=specs + [pl.BlockSpec((TR, 512), rev), pl.BlockSpec((ATT_Q_HEADS, TR), lambda i: (0, nt - 1 - i)),
                          pl.BlockSpec((TR, 512), rev)],
        out_specs=[pl.BlockSpec((TR, N_C), rev), pl.BlockSpec((8, 512), lambda i: (0, 0)),
                   pl.BlockSpec((8, 128), lambda i: (0, 0))],
        out_shape=[jax.ShapeDtypeStruct((Lp, N_C), BF), jax.ShapeDtypeStruct((8, 512), F32),
                   jax.ShapeDtypeStruct((8, 128), F32)],
        scratch_shapes=[win, win, win, win, wint, wint, tile_bf, tile_bf, tile_bf, tile_bf,
                        pltpu.VMEM((512, TR), F32), acc, acc, acc, acc,
                        pltpu.VMEM((HALO, 128), F32), pltpu.VMEM((HALO, 128), F32),
                        pltpu.VMEM((N_META, 128), F32), pltpu.VMEM((N_META, 128), F32)]
        + _swa_park(F32) + _swa_park(F32) + _swa_park(BF) + _swa_park(BF) + _swa_ones(),
        compiler_params=_cp(("arbitrary",)),
    )(uc, uc, uc, uc, uc, uc, uc, uc, qg8, kg2, sinks, o_save, lse, dyc)


def _mix_fwd(h, ya, yb, yc, ug, wa, wb, wc, wo, g_next=None, loss=None):
    Lp = h.shape[0]
    wspec = lambda r: pl.BlockSpec((r, D_MODEL), lambda i: (0, 0))
    yspec = pl.BlockSpec((TRM, 512), lambda i: (i, 0))
    hspec = pl.BlockSpec((TRM, D_MODEL), lambda i: (i, 0))
    last = loss is not None

    def body(h_ref, ya_ref, yb_ref, yc_ref, ug_ref, wa_ref, wb_ref, wc_ref, wo_ref, t_ref, *outs):
        mixed = jnp.zeros((TRM, D_MODEL), F32)
        for n, (y_ref, w_ref) in enumerate(((ya_ref, wa_ref), (yb_ref, wb_ref), (yc_ref, wc_ref))):
            z = _nn(y_ref[...], w_ref[...])
            outs[n][...] = z.astype(BF)
            mixed = mixed + _sig(ug_ref[:, D_MODEL * n:D_MODEL * (n + 1)].astype(F32)) * z
        mixed = mixed.astype(BF)
        outs[3][...] = mixed
        x = h_ref[...] + _nn(mixed, wo_ref[...])
        if last:
            dh_ref, l_ref = outs[4:]
            i = pl.program_id(0)

            @pl.when(i == 0)
            def _():
                l_ref[...] = jnp.zeros_like(l_ref)

            row = i * TRM + lax.broadcasted_iota(jnp.int32, (TRM, 1), 0)
            e = jnp.where((row >= CHUNK) & (row < CHUNK + loss[1]), x - t_ref[...], 0.0)
            dh_ref[...] = e * (1.0 / D_MODEL)
            l_ref[...] += (0.5 / D_MODEL) * jnp.sum(jnp.sum(e * e, axis=0, keepdims=True), axis=1, keepdims=True)
        else:
            x_ref, nx_ref = outs[4:]
            x_ref[...] = x
            nx_ref[...] = (x * lax.rsqrt(jnp.mean(x * x, axis=-1, keepdims=True) + EPS) * t_ref[...]).astype(BF)

    bf = jax.ShapeDtypeStruct((Lp, D_MODEL), BF)
    f32 = jax.ShapeDtypeStruct((Lp, D_MODEL), F32)
    if last:
        tail_in, tail_spec = loss[0], hspec
        out_specs = [hspec] * 5 + [pl.BlockSpec((8, 128), lambda i: (0, 0))]
        out_shape = [bf] * 4 + [f32, jax.ShapeDtypeStruct((8, 128), F32)]
    else:
        tail_in, tail_spec = g_next, wspec(1)
        out_specs = [hspec] * 6
        out_shape = [bf] * 4 + [f32, bf]
    return pl.pallas_call(
        body, name="mix_fwd_loss" if last else "mix_fwd", grid=(Lp // TRM,),
        in_specs=[hspec, yspec, yspec, yspec, pl.BlockSpec((TRM, N_G), lambda i: (i, 0)),
                  wspec(512), wspec(512), wspec(512), wspec(D_MODEL), tail_spec],
        out_specs=out_specs, out_shape=out_shape,
        compiler_params=_cp(("arbitrary",) if last else ("parallel",)),
    )(h, ya, yb, yc, ug, wa, wb, wc, wo, tail_in)


def _mix_bwd(dh, za, zb, zc, ug, wa, wb, wc, wo):
    Lp = dh.shape[0]
    wspec = lambda r: pl.BlockSpec((r, D_MODEL), lambda i: (0, 0))
    yspec = pl.BlockSpec((TRM, 512), lambda i: (i, 0))
    hspec = pl.BlockSpec((TRM, D_MODEL), lambda i: (i, 0))
    gspec = pl.BlockSpec((TRM, N_G), lambda i: (i, 0))

    def body(dh_ref, za_ref, zb_ref, zc_ref, ug_ref, wa_ref, wb_ref, wc_ref, wo_ref,
             dug_ref, dza_ref, dzb_ref, dzc_ref, dya_ref, dyb_ref, dyc_ref):
        dmix = _nt(dh_ref[...].astype(BF), wo_ref[...])
        for n, (z_ref, w_ref, dz_ref, dy_ref) in enumerate(((za_ref, wa_ref, dza_ref, dya_ref),
                                                            (zb_ref, wb_ref, dzb_ref, dyb_ref),
                                                            (zc_ref, wc_ref, dzc_ref, dyc_ref))):
            sl = slice(D_MODEL * n, D_MODEL * (n + 1))
            gt = _sig(ug_ref[:, sl].astype(F32))
            dz = dmix * gt
            dug_ref[:, sl] = (dz * z_ref[...].astype(F32) * (1.0 - gt)).astype(BF)
            dz = dz.astype(BF)
            dz_ref[...] = dz
            dy_ref[...] = _nt(dz, w_ref[...]).astype(BF)

    bf = lambda n: jax.ShapeDtypeStruct((Lp, n), BF)
    return pl.pallas_call(
        body, name="mix_bwd", grid=(Lp // TRM,),
        in_specs=[hspec, hspec, hspec, hspec, gspec, wspec(512), wspec(512), wspec(512), wspec(D_MODEL)],
        out_specs=[gspec, hspec, hspec, hspec, yspec, yspec, yspec],
        out_shape=[bf(N_G), bf(D_MODEL), bf(D_MODEL), bf(D_MODEL), bf(512), bf(512), bf(512)],
        compiler_params=_cp(("parallel",)),
    )(dh, za, zb, zc, ug, wa, wb, wc, wo)


def _inproj_bwd(dus, ws, h, dh, g, carry=None):
    Lp = h.shape[0]
    widths = [w.shape[0] for w in ws]

    def body(dg_ref, da_ref, db_ref, dc_ref, wg_ref, wa_ref, wb_ref, wc_ref, h_ref, dh_ref, g_ref, o_ref, gg_ref):
        @pl.when(pl.program_id(0) == 0)
        def _():
            gg_ref[...] = jnp.zeros_like(gg_ref)

        dhn = (_nn(dg_ref[...], wg_ref[...]) + _nn(da_ref[...], wa_ref[...])
               + _nn(db_ref[...], wb_ref[...]) + _nn(dc_ref[...], wc_ref[...]))
        x = h_ref[...]
        r = lax.rsqrt(jnp.mean(x * x, axis=-1, keepdims=True) + EPS)
        xhat = x * r
        gg_ref[0:1, :] += jnp.sum(dhn * xhat, axis=0, keepdims=True)
        gd = dhn * g_ref[...]
        o_ref[...] = dh_ref[...] + r * (gd - xhat * jnp.mean(gd * xhat, axis=-1, keepdims=True))

    hspec = pl.BlockSpec((TRM, D_MODEL), lambda i: (i, 0))
    in_specs = ([pl.BlockSpec((TRM, n), lambda i: (i, 0)) for n in widths]
                + [pl.BlockSpec((n, D_MODEL), lambda i: (0, 0), pipeline_mode=pl.Buffered(1)) for n in widths]
                + [hspec, hspec, pl.BlockSpec((1, D_MODEL), lambda i: (0, 0))])
    out_specs = [hspec, pl.BlockSpec((8, D_MODEL), lambda i: (0, 0))]
    out_shape = [jax.ShapeDtypeStruct((Lp, D_MODEL), F32), jax.ShapeDtypeStruct((8, D_MODEL), F32)]
    return _call_carrying(body, "inproj_bwd", Lp // TRM, in_specs, out_specs, out_shape, [],
                          (*dus, *ws, h, dh, g), carry)


def _lb_softmax(lb_ref):
    x = lb_ref[...]
    e = jnp.exp(x - jnp.max(x, axis=0, keepdims=True))
    return e / jnp.sum(e, axis=0, keepdims=True)


def _lb_fwd(hg_lb):
    def body(lb_ref, o_ref):
        sm = _lb_softmax(lb_ref)
        acc = jnp.zeros((1, 512), F32)
        for l in range(DEPTH):
            if l > 0:
                acc = acc + sm[l:l + 1, :]
            o_ref[l:l + 1, :] = jnp.clip(acc, 0.0, 1.0)

    return pl.pallas_call(body, name="lb_fwd", out_shape=jax.ShapeDtypeStruct((DEPTH, 512), F32))(hg_lb)


def _lb_bwd(hg_lb, dlb_all):
    def body(lb_ref, d_ref, o_ref):
        sm = _lb_softmax(lb_ref)
        acc = jnp.zeros((1, 512), F32)
        gm = []
        for l in range(DEPTH):
            if l > 0:
                acc = acc + sm[l:l + 1, :]
            gm.append(jnp.where((acc >= 0.0) & (acc <= 1.0), d_ref[l:l + 1, :], 0.0))
        dsm = [jnp.zeros((1, 512), F32)]
        for j in range(1, DEPTH):
            s = gm[j]
            for l in range(j + 1, DEPTH):
                s = s + gm[l]
            dsm.append(s)
        dot = dsm[0] * sm[0:1, :]
        for j in range(1, DEPTH):
            dot = dot + dsm[j] * sm[j:j + 1, :]
        for j in range(DEPTH):
            o_ref[j:j + 1, :] = sm[j:j + 1, :] * (dsm[j] - dot)

    return pl.pallas_call(body, name="lb_bwd", out_shape=jax.ShapeDtypeStruct((DEPTH, 512), F32))(hg_lb, dlb_all)


_ANY = pl.BlockSpec(memory_space=pl.ANY)


def _chip_peers():
    x, y, c = lax.axis_index("x"), lax.axis_index("y"), lax.axis_index("c")
    return (x, y, c), [(1 - x, y, c), (x, 1 - y, c), (1 - x, 1 - y, c)]


def _exchange(kind, ins, outs, send, recv, loc):
    (x, y, c), peers = _chip_peers()
    me = 2 * x + y
    ds = []
    for a in range(len(ins)):
        if kind == "gather":
            ds.append(pltpu.make_async_copy(ins[a], outs[a].at[me], loc.at[a]))
        else:
            ds.append(pltpu.make_async_copy(ins[a].at[me], outs[a].at[0], loc.at[a]))
        for p, (px, py, pc) in enumerate(peers):
            src, dst = (ins[a], outs[a].at[me]) if kind == "gather" else (ins[a].at[2 * px + py], outs[a].at[1 + p])
            ds.append(pltpu.make_async_remote_copy(src_ref=src, dst_ref=dst, send_sem=send.at[a, p],
                                                   recv_sem=recv.at[a, p], device_id=(px, py, pc), device_id_type=MESH))
    return ds


def _exchange_out_shapes(kind, arrs):
    if kind == "gather":
        return [jax.ShapeDtypeStruct((4,) + a.shape, a.dtype) for a in arrs]
    return [jax.ShapeDtypeStruct(a.shape, a.dtype) for a in arrs]


def _exchange_sems(n):
    return [pltpu.SemaphoreType.DMA((n, 3)), pltpu.SemaphoreType.DMA((n, 3)), pltpu.SemaphoreType.DMA((n,))]


def _gather_first(arrs, split):
    n = len(arrs)

    def body(*refs):
        ins, outs = refs[:n], refs[n:2 * n]
        send, recv, loc, fsend, frecv = refs[2 * n:]
        (x, y, c), peers = _chip_peers()
        me = 2 * x + y

        def half(a):
            hr = arrs[a].shape[0] // 2
            return pl.ds(pl.multiple_of(c * hr, 16), hr)

        local = [pltpu.make_async_copy(ins[a], outs[a].at[me], loc.at[a]) for a in range(n)]
        far, fwd = {}, {}
        for a in range(n):
            for p, (px, py, pc) in enumerate(peers):
                src, dst = (ins[a].at[half(a)], outs[a].at[me, half(a)]) if split[a] else (ins[a], outs[a].at[me])
                far[a, p] = pltpu.make_async_remote_copy(src_ref=src, dst_ref=dst, send_sem=send.at[a, p],
                                                         recv_sem=recv.at[a, p], device_id=(px, py, pc), device_id_type=MESH)
                if split[a]:
                    landed = outs[a].at[2 * px + py, half(a)]
                    fwd[a, p] = pltpu.make_async_remote_copy(src_ref=landed, dst_ref=landed, send_sem=fsend.at[a, p],
                                                             recv_sem=frecv.at[a, p], device_id=(x, y, 1 - c),
                                                             device_id_type=MESH)
        for d in local + list(far.values()):
            d.start()
        for key, d in far.items():
            d.wait_recv()
            if key in fwd:
                fwd[key].start()
        for d in fwd.values():
            d.wait()
        for d in far.values():
            d.wait_send()
        for d in local:
            d.wait()

    sems = pltpu.SemaphoreType.DMA((n, 3))
    return pl.pallas_call(
        body, name="gather_first", in_specs=[_ANY] * n, out_specs=[_ANY] * n,
        out_shape=_exchange_out_shapes("gather", arrs),
        scratch_shapes=[sems, sems, pltpu.SemaphoreType.DMA((n,)), sems, sems],
        compiler_params=pltpu.CompilerParams(has_side_effects=True),
    )(*arrs)


def _carry_exchange(body, n_in, n_out, n_steps, kind, n):
    def wrapped(*refs):
        ins, cin = refs[:n_in], refs[n_in:n_in + n]
        outs, cout = refs[n_in + n:n_in + n + n_out], refs[n_in + n + n_out:n_in + 2 * n + n_out]
        scr, sems = refs[n_in + 2 * n + n_out:-3], refs[-3:]
        i = pl.program_id(0)

        @pl.when(i == 0)
        def _():
            for d in _exchange(kind, cin, cout, *sems):
                d.start()

        body(*ins, *outs, *scr)

        @pl.when(i == n_steps - 1)
        def _():
            for d in _exchange(kind, cin, cout, *sems):
                d.wait()

    return wrapped


def _swap_cores(arrs):
    n = len(arrs)

    def body(*refs):
        ins, outs = refs[:n], refs[n:2 * n]
        send, recv = refs[2 * n:]
        x, y, c = lax.axis_index("x"), lax.axis_index("y"), lax.axis_index("c")
        rdmas = []
        for a in range(n):
            r = pltpu.make_async_remote_copy(src_ref=ins[a], dst_ref=outs[a], send_sem=send.at[a], recv_sem=recv.at[a],
                                             device_id=(x, y, 1 - c), device_id_type=MESH)
            r.start()
            rdmas.append(r)
        for r in rdmas:
            r.wait()

    return pl.pallas_call(
        body, name="swap_cores", in_specs=[_ANY] * n, out_specs=[_ANY] * n,
        out_shape=[jax.ShapeDtypeStruct(a.shape, a.dtype) for a in arrs],
        scratch_shapes=[pltpu.SemaphoreType.DMA((n,)), pltpu.SemaphoreType.DMA((n,))],
        compiler_params=pltpu.CompilerParams(has_side_effects=True),
    )(*arrs)


def _allsum_small(p):
    R = p.shape[0]

    def body(p_ref, o_ref, buf, send, recv):
        x, y, c = lax.axis_index("x"), lax.axis_index("y"), lax.axis_index("c")
        me = 4 * x + 2 * y + c
        buf[me] = p_ref[...]
        rdmas = []
        for k in range(1, 8):
            peer = (x ^ (k >> 2), y ^ ((k >> 1) & 1), c ^ (k & 1))
            r = pltpu.make_async_remote_copy(src_ref=p_ref, dst_ref=buf.at[me], send_sem=send.at[k - 1],
                                             recv_sem=recv.at[k - 1], device_id=peer, device_id_type=MESH)
            r.start()
            rdmas.append(r)
        for r in rdmas:
            r.wait()
        acc = buf[0]
        for d in range(1, 8):
            acc = acc + buf[d]
        o_ref[...] = acc

    return pl.pallas_call(
        body, name="allsum_small", out_shape=jax.ShapeDtypeStruct((R, 512), F32),
        in_specs=[pl.BlockSpec(memory_space=pltpu.VMEM)], out_specs=pl.BlockSpec(memory_space=pltpu.VMEM),
        scratch_shapes=[pltpu.VMEM((8, R, 512), F32), pltpu.SemaphoreType.DMA((7,)), pltpu.SemaphoreType.DMA((7,))],
        compiler_params=_cp(has_side_effects=True),
    )(p)


def _row_block(rows):
    return max((d for d in range(16, 513, 16) if rows % d == 0), default=rows)


def _sum4(parts, name):
    _, R, C = parts.shape
    tr = _row_block(R)

    def body(p_ref, o_ref):
        p = [p_ref[k].astype(F32) for k in range(4)]
        o_ref[...] = (((p[0] + p[1]) + p[2]) + p[3]).astype(BF)

    return pl.pallas_call(
        body, name=name, grid=(R // tr,), in_specs=[pl.BlockSpec((4, tr, C), lambda i: (0, i, 0))],
        out_specs=pl.BlockSpec((tr, C), lambda i: (i, 0)), out_shape=jax.ShapeDtypeStruct((R, C), BF),
        compiler_params=_cp(("parallel",)),
    )(parts)


def _adamw(w, m, v, g0, g1, name):
    L, R, C = w.shape
    tr = _row_block(R)
    two = g1 is not None
    c1 = 1.0 / (1.0 - ADAM_B1 ** ADAM_STEP)
    c2 = 1.0 / (1.0 - ADAM_B2 ** ADAM_STEP)

    def body(*refs):
        if two:
            w_ref, m_ref, v_ref, a_ref, b_ref, g_ref, d_ref, nm_ref, nv_ref = refs
            g = a_ref[...].astype(F32) + b_ref[...].astype(F32)
        else:
            w_ref, m_ref, v_ref, a_ref, g_ref, d_ref, nm_ref, nv_ref = refs
            g = a_ref[...]
        g_ref[...] = g
        m = ADAM_B1 * m_ref[...] + (1.0 - ADAM_B1) * g
        v = ADAM_B2 * v_ref[...] + (1.0 - ADAM_B2) * (g * g)
        nm_ref[...] = m
        nv_ref[...] = v
        d_ref[...] = -ADAM_LR * ((m * c1) / (jnp.sqrt(v * c2) + ADAM_EPS) + ADAM_WD * w_ref[...])

    spec = pl.BlockSpec((1, tr, C), lambda l, i: (l, i, 0))
    n_in = 5 if two else 4
    ins = (w, m, v, g0, g1) if two else (w, m, v, g0)
    return pl.pallas_call(
        body, name=name, grid=(L, R // tr), in_specs=[spec] * n_in, out_specs=[spec] * 4,
        out_shape=[jax.ShapeDtypeStruct((L, R, C), F32)] * 4, compiler_params=_cp(("parallel", "parallel")),
    )(*ins)


def _pad8(a):
    r = (-a.shape[0]) % 8
    return a if r == 0 else jnp.pad(a, ((0, r), (0, 0)))


def _local_step(x, tgt, meta, P, shards=None, prep=None, pack=None, mats=None):
    seq = x.shape[0]
    Lp = -(-(seq + CHUNK) // TR) * TR
    tail = Lp - seq - CHUNK
    h = jnp.concatenate([jnp.zeros((META_PAD, D_MODEL), F32), meta, x, jnp.zeros((tail, D_MODEL), F32)], axis=0)
    tgt_pad = jnp.pad(tgt, ((CHUNK, tail), (0, 0)))

    P = list(P)
    saved = []
    hn = _rms_fwd(h, P[0]["norm_g"])
    for l in range(DEPTH):
        p = P[l]
        mm = functools.partial(_matmul, tb=True, out_dtype=BF, tm=TR, tk=D_MODEL, col_major_grid=True)
        ug = mm(hn, p["w_g"], tn=N_G // 2, name="inproj_g")
        ua = mm(hn, p["w_a"], tn=N_A, name="inproj_a")
        ub = mm(hn, p["w_b"], tn=N_B, name="inproj_b")
        uc = mm(hn, p["w_c"], tn=N_C, name="inproj_c")
        nxt = shards[l + 1] if shards is not None and l + 1 < DEPTH else None
        carry = (lambda part: ("gather", part)) if nxt is not None else (lambda part: None)
        res_a = _conv_fwd(ua, p["conv_w"], p["conv_vec"], carry(nxt and nxt[1:2]))
        res_b = _hg_fwd(ub, p["lb"], p["gn4"], carry(nxt and nxt[0:1]))
        own = shards[0][2:] if nxt is not None and l == 0 else []
        res_c = _swa_fwd(uc, p["qg"], p["kg"], p["sinks"], carry(nxt and nxt[2:] + own))
        (ya, yconv), (yb, o_hg, s_hg), (yc, o_at, lse) = res_a[:2], res_b[:3], res_c[:3]
        if nxt is not None:
            P.append(prep(l + 1, [*res_b[3:], *res_a[2:], *res_c[3:3 + len(nxt) - 2]]))
            if own:
                p.update(mats(res_c[3 + len(nxt) - 2:]))
        mix = functools.partial(_mix_fwd, h, ya, yb, yc, ug, p["w_ao"], p["w_bo"], p["w_co"], p["w_out"])
        if l + 1 < DEPTH:
            za, zb, zc, mixed, h_new, hn_next = mix(g_next=P[l + 1]["norm_g"])
        else:
            za, zb, zc, mixed, dh, loss8 = mix(loss=(tgt_pad, seq))
        saved.append(dict(h=h, hn=hn, ug=ug, ua=ua, ub=ub, uc=uc, ya=ya, yconv=yconv, yb=yb, o_hg=o_hg, s_hg=s_hg,
                          yc=yc, o_at=o_at, lse=lse, za=za, zb=zb, zc=zc, mixed=mixed))
        if l + 1 < DEPTH:
            h, hn = h_new, hn_next

    grads = [None] * DEPTH
    parts = [[None, None] for _ in range(DEPTH)]
    pending = None
    tk_dw = 2 * TR if Lp % (2 * TR) == 0 else TR
    for l in reversed(range(DEPTH)):
        p, s = P[l], saved[l]
        dug, dza, dzb, dzc, dya, dyb, dyc = _mix_bwd(dh, s["za"], s["zb"], s["zc"], s["ug"],
                                                      p["w_ao"], p["w_bo"], p["w_co"], p["w_out"])
        tnmm = functools.partial(_matmul, ta=True, out_dtype=F32, tk=tk_dw)
        g = {}
        g["w_out"] = tnmm(s["mixed"], dh, tm=D_MODEL, tn=D_MODEL, name="dw_out")
        g["w_ao"], g["w_bo"], g["w_co"] = _dw_branches([s["ya"], s["yb"], s["yc"]], [dza, dzb, dzc], tk_dw)
        dua, g["conv_w"], g["conv_vec"] = _conv_bwd(s["ua"], s["yconv"], dya, p["conv_w"], p["conv_vec"])
        carry = ("scatter", pending[1]) if pending is not None else None
        res = _hg_bwd(s["ub"], p["lb"], p["gn4"], s["o_hg"], s["s_hg"], dyb, carry)
        dub, g["hg_small"] = res[:2]
        if carry is not None:
            parts[pending[0]][1] = res[2:]
        duc, g["at_gain"], g["at_sink"] = _swa_bwd(s["uc"], p["qg"], p["kg"], p["sinks"], s["o_at"], s["lse"], dyc)
        g["w_g"] = tnmm(dug, s["hn"], tm=N_G // 2, tn=D_MODEL, name="dw_in_g")
        g["w_a"] = tnmm(dua, s["hn"], tm=N_A, tn=D_MODEL, name="dw_in_a")
        g["w_b"] = tnmm(dub, s["hn"], tm=N_B, tn=D_MODEL, name="dw_in_b")
        g["w_c"] = tnmm(duc, s["hn"], tm=N_C, tn=D_MODEL, name="dw_in_c")
        first, second = pack(g) if pack is not None else (None, None)
        if first is not None and l == 0:
            first, second = first + second, []
        res = _inproj_bwd([dug, dua, dub, duc], [p["w_g"], p["w_a"], p["w_b"], p["w_c"]], s["h"], dh, p["norm_g"],
                          ("scatter", first) if first is not None else None)
        dh, g["norm_g"] = res[:2]
        grads[l] = g
        if pack is not None:
            parts[l] = [res[2:3], res[3:]] if l == 0 else [res[2:], None]
            pending = (l, second) if l > 0 else None
    return loss8, dh, grads, parts


def _split_w_in(wt):
    return dict(w_a=wt[0:1536], w_b=wt[1536:3584],
                w_c=jnp.concatenate([wt[3584:4096], wt[4352:4864], wt[4096:4352]], axis=0), w_g=wt[4864:7936])


def _join_w_in(g):
    c = g["w_c"]
    return jnp.concatenate([g["w_a"], g["w_b"], c[0:512], c[1024:1280], c[512:1024], g["w_g"]], axis=0)


def _attn_small(g):
    return (g["at_gain"][0].reshape(ATT_Q_HEADS, ATT_HD).sum(0),
            g["at_gain"][1, 0:128].reshape(ATT_KV_HEADS, ATT_HD).sum(0), g["at_sink"].sum(1))


_SMALL = (("norm_g", 8), ("meta", 32), ("conv_w", 32 * DEPTH), ("conv_b", 8), ("conv_ln_g", 8), ("conv_ln_b", 8),
          ("lb", 8), ("hg_norm_g", 8), ("q_norm_g", 8), ("k_norm_g", 8), ("sinks", 8))


def _small_offsets():
    off, o = {}, 0
    for name, rows in _SMALL:
        off[name] = (o, rows)
        o += rows
    return off, o


def _pack_small(d):
    parts = []
    for name, rows in _SMALL:
        a = d[name]
        parts.append(jnp.pad(a, ((0, rows - a.shape[0]), (0, 512 - a.shape[1]))))
    return jnp.concatenate(parts, axis=0)


def kernel(x, meta_tokens, norm_g, w_in, conv_w, conv_b, conv_ln_g, conv_ln_b, w_conv_out, hg_lower_bounds, hg_norm_g, w_hg_out, q_norm_g, k_norm_g, attn_sinks, w_att_out, w_out, loss_target, m_meta_tokens, m_norm_g, m_w_in, m_conv_w, m_conv_b, m_conv_ln_g, m_conv_ln_b, m_w_conv_out, m_hg_lower_bounds, m_hg_norm_g, m_w_hg_out, m_q_norm_g, m_k_norm_g, m_attn_sinks, m_w_att_out, m_w_out, v_meta_tokens, v_norm_g, v_w_in, v_conv_w, v_conv_b, v_conv_ln_g, v_conv_ln_b, v_w_conv_out, v_hg_lower_bounds, v_hg_norm_g, v_w_hg_out, v_q_norm_g, v_k_norm_g, v_attn_sinks, v_w_att_out, v_w_out):
    xi, yi = lax.axis_index("x"), lax.axis_index("y")
    chip = 2 * xi + yi
    NS = w_in.shape[2]
    CS = conv_w.shape[2]
    MS = meta_tokens.shape[1]

    half = NS // 2
    w_in_t, m_w_in_t, v_w_in_t = (jnp.swapaxes(t, 1, 2) for t in (w_in, m_w_in, v_w_in))
    shards = [[w_in_t[l, :half].astype(BF), w_in_t[l, half:].astype(BF), w_conv_out[l].astype(BF),
               w_hg_out[l].astype(BF), w_att_out[l].astype(BF), w_out[l].astype(BF)] for l in range(DEPTH)]
    *first, g_meta, g_convw = _gather_first(shards[0][:2] + [meta_tokens, conv_w.reshape(DEPTH * CONV_WIDTH, CS)],
                                            [True, True, False, False])
    cols = lambda g: g.transpose(1, 0, 2).reshape(g.shape[1], -1)
    meta_f = cols(g_meta)
    convw_f = cols(g_convw).reshape(DEPTH, CONV_WIDTH, D_CONV)
    lb_all = _lb_fwd(hg_lower_bounds)

    def mats(gathered):
        g_wao, g_wbo, g_wco, g_wout = gathered
        return dict(w_ao=cols(g_wao), w_bo=cols(g_wbo), w_co=cols(g_wco), w_out=g_wout.reshape(D_MODEL, D_MODEL))

    def prep(l, gathered):
        p = _split_w_in(jnp.concatenate(gathered[:2], axis=1).reshape(4 * NS, D_MODEL))
        if len(gathered) > 2:
            p.update(mats(gathered[2:]))
        p.update(norm_g=norm_g[l:l + 1], conv_w=convw_f[l],
                 conv_vec=_pad8(jnp.stack([conv_b[l], conv_ln_g[l], conv_ln_b[l]])),
                 lb=lb_all[l:l + 1], gn4=jnp.tile(hg_norm_g[l:l + 1], (1, HG_HEADS)),
                 qg=jnp.tile(q_norm_g[l:l + 1], (1, ATT_Q_HEADS)), kg=jnp.tile(k_norm_g[l:l + 1], (1, ATT_KV_HEADS)),
                 sinks=attn_sinks[l:l + 1])
        return p

    shard_cols = lambda a: a.reshape(a.shape[0], 4, -1).transpose(1, 0, 2)
    def pack(g):
        win = _join_w_in(g).reshape(4, NS, D_MODEL).astype(BF)
        return [win[:, :half]], [win[:, half:], shard_cols(g["w_ao"]).astype(BF), shard_cols(g["w_bo"]).astype(BF),
                                 shard_cols(g["w_co"]).astype(BF), g["w_out"].reshape(4, MS, D_MODEL).astype(BF)]

    loss8, dh0, grads, parts = _local_step(x[0], loss_target[0], meta_f, [prep(0, first)], shards, prep, pack, mats)
    seq = x.shape[1]
    grad_x = dh0[CHUNK:CHUNK + seq][None]
    loss = lax.psum(loss8[0, 0], ("x", "y", "c"))

    sum4 = functools.partial(_sum4, name="sum_chips")
    mine = [jnp.concatenate([t for l in range(DEPTH) for t in (sum4(parts[l][0][0]), sum4(parts[l][1][0]))], axis=0)]
    mine += [jnp.concatenate([sum4(parts[l][1][a]) for l in range(DEPTH)], axis=0) for a in range(1, 5)]
    theirs = _swap_cores(mine)

    dlb_all = jnp.concatenate([grads[l]["hg_small"][0:1] for l in range(DEPTH)], axis=0)
    small = dict(
        norm_g=jnp.concatenate([grads[l]["norm_g"][0:1] for l in range(DEPTH)], axis=0).reshape(8, 512),
        meta=dh0[META_PAD:CHUNK].reshape(32, 512),
        conv_w=jnp.concatenate([grads[l]["conv_w"] for l in range(DEPTH)], axis=0),
        conv_b=jnp.concatenate([grads[l]["conv_vec"][0:1] for l in range(DEPTH)], axis=0),
        conv_ln_g=jnp.concatenate([grads[l]["conv_vec"][1:2] for l in range(DEPTH)], axis=0),
        conv_ln_b=jnp.concatenate([grads[l]["conv_vec"][2:3] for l in range(DEPTH)], axis=0),
        lb=_lb_bwd(hg_lower_bounds, dlb_all),
        hg_norm_g=jnp.concatenate([grads[l]["hg_small"][1:2].reshape(HG_HEADS, HG_D).sum(0, keepdims=True)
                                   for l in range(DEPTH)], axis=0),
        q_norm_g=jnp.stack([_attn_small(grads[l])[0] for l in range(DEPTH)]),
        k_norm_g=jnp.stack([_attn_small(grads[l])[1] for l in range(DEPTH)]),
        sinks=jnp.stack([_attn_small(grads[l])[2] for l in range(DEPTH)]),
    )
    gsum = _allsum_small(_pack_small(small))
    off, _ = _small_offsets()

    def take(name, rows, cols):
        o, _ = off[name]
        return gsum[o:o + rows, 0:cols]

    g_meta_full = take("meta", 32, 512).reshape(N_META, D_MODEL)
    g_convw_full = take("conv_w", 32 * DEPTH, 512).reshape(DEPTH, 32, 512)[:, :CONV_WIDTH]
    small_grads = dict(
        norm_g=take("norm_g", 8, 512),
        meta=lax.dynamic_slice_in_dim(g_meta_full, chip * MS, MS, axis=1),
        conv_w=lax.dynamic_slice_in_dim(g_convw_full, chip * CS, CS, axis=2).reshape(DEPTH * CONV_WIDTH, CS),
        conv_b=take("conv_b", DEPTH, 512), conv_ln_g=take("conv_ln_g", DEPTH, 512), conv_ln_b=take("conv_ln_b", DEPTH, 512),
        lb=take("lb", DEPTH, 512), hg_norm_g=take("hg_norm_g", DEPTH, HG_D), q_norm_g=take("q_norm_g", DEPTH, ATT_HD),
        k_norm_g=take("k_norm_g", DEPTH, ATT_HD), sinks=take("sinks", DEPTH, ATT_Q_HEADS))

    def big_update(w, m, v, a, b, name):
        return _adamw(w, m, v, a.reshape(w.shape), b.reshape(w.shape), name)

    res = {}
    res["w_in"] = [jnp.swapaxes(t, 1, 2) for t in big_update(w_in_t, m_w_in_t, v_w_in_t, mine[0], theirs[0], "adamw_w_in")]
    res["w_conv_out"] = big_update(w_conv_out, m_w_conv_out, v_w_conv_out, mine[1], theirs[1], "adamw_w_ao")
    res["w_hg_out"] = big_update(w_hg_out, m_w_hg_out, v_w_hg_out, mine[2], theirs[2], "adamw_w_bo")
    res["w_att_out"] = big_update(w_att_out, m_w_att_out, v_w_att_out, mine[3], theirs[3], "adamw_w_co")
    res["w_out"] = big_update(w_out, m_w_out, v_w_out, mine[4], theirs[4], "adamw_w_out")

    small_w = dict(meta=(meta_tokens, m_meta_tokens, v_meta_tokens), norm_g=(norm_g, m_norm_g, v_norm_g),
                   conv_w=(conv_w, m_conv_w, v_conv_w), conv_b=(conv_b, m_conv_b, v_conv_b),
                   conv_ln_g=(conv_ln_g, m_conv_ln_g, v_conv_ln_g), conv_ln_b=(conv_ln_b, m_conv_ln_b, v_conv_ln_b),
                   lb=(hg_lower_bounds, m_hg_lower_bounds, v_hg_lower_bounds),
                   hg_norm_g=(hg_norm_g, m_hg_norm_g, v_hg_norm_g), q_norm_g=(q_norm_g, m_q_norm_g, v_q_norm_g),
                   k_norm_g=(k_norm_g, m_k_norm_g, v_k_norm_g), sinks=(attn_sinks, m_attn_sinks, v_attn_sinks))
    view = lambda n, t: t.reshape(-1, 512) if n == "norm_g" else t.reshape(-1, t.shape[-1])
    pw, pm, pv = (_pack_rows([view(n, small_w[n][k]) for n in small_w]) for k in range(3))
    pg = _pack_rows([small_grads[n] for n in small_w])
    packed = [t[0] for t in _adamw(pw[None], pm[None], pv[None], pg[None], None, "adamw_small")]
    o = 0
    for n in small_w:
        r, cdim = view(n, small_w[n][0]).shape
        res[n] = [t[o:o + r, 0:cdim].reshape(small_w[n][0].shape) for t in packed]
        o += -(-r // 8) * 8

    order = [("meta", None), ("norm_g", None), ("w_in", None), ("conv_w", None), ("conv_b", None), ("conv_ln_g", None),
             ("conv_ln_b", None), ("w_conv_out", None), ("lb", None), ("hg_norm_g", None), ("w_hg_out", None),
             ("q_norm_g", None), ("k_norm_g", None), ("sinks", None), ("w_att_out", None), ("w_out", None)]
    outs = [loss, grad_x]
    for k in range(4):
        outs += [res[n][k] for n, _ in order]
    return tuple(outs)


def _pack_rows(arrs):
    parts = []
    for a in arrs:
        r = (-a.shape[0]) % 8
        parts.append(jnp.pad(a, ((0, r), (0, 512 - a.shape[1]))))
    return jnp.concatenate(parts, axis=0)
```
